```python
import math
import jax, jax.numpy as jnp
from jax import lax
import numpy as np

D_MODEL = 1024
BATCH = 8
SEQ = 8192
DEPTH = 4

CHUNK = 64
EPS = 1e-6
NEG_INF = -1e30
N_BRANCH = 3
BR_WIDTH = 512

SGU_BLOCK = 128
SGU_GROUPS = 8
SGU_GROUP_DIM = BR_WIDTH // SGU_GROUPS

MLA_HEADS = 8
MLA_NOPE = 64
MLA_ROPE = 32
MLA_V = 64
MLA_QK = MLA_NOPE + MLA_ROPE
MLA_Q_RANK = 256
MLA_KV_RANK = 128
ROPE_BASE = 10000.0
Q_BLOCK = 128

CA_HEADS = 8
CA_HEAD_DIM = BR_WIDTH // CA_HEADS
LEFT_CHUNKS = 8
BAND = (LEFT_CHUNKS + 1) * CHUNK
REL_CLIP = 128

IN_WIDTHS = (BR_WIDTH, BR_WIDTH, BR_WIDTH,
             MLA_Q_RANK, MLA_KV_RANK, MLA_ROPE, BR_WIDTH,
             BR_WIDTH, BR_WIDTH, BR_WIDTH, BR_WIDTH,
             N_BRANCH * D_MODEL)
D_IN = sum(IN_WIDTHS)

kernel_name = "hybrid_sgu_mla_chunkattn_streaming"


def rmsnorm(x, g):
    xf = x.astype(jnp.float32)
    y = xf * lax.rsqrt(jnp.mean(xf * xf, axis=-1, keepdims=True) + EPS)
    return (y * g.astype(jnp.float32)).astype(x.dtype)


def layernorm(x, g, b):
    xf = x.astype(jnp.float32)
    mu = jnp.mean(xf, axis=-1, keepdims=True)
    xc = xf - mu
    y = xc * lax.rsqrt(jnp.mean(xc * xc, axis=-1, keepdims=True) + EPS)
    return (y * g.astype(jnp.float32) + b.astype(jnp.float32)).astype(x.dtype)


def apply_rope(x, pos):
    half = x.shape[-1] // 2
    inv = ROPE_BASE ** (-jnp.arange(half, dtype=jnp.float32) / half)
    ang = pos.astype(jnp.float32)[:, None] * inv[None, :]
    cos = jnp.cos(ang)[:, None, :]
    sin = jnp.sin(ang)[:, None, :]
    xf = x.astype(jnp.float32)
    x1, x2 = xf[..., :half], xf[..., half:]
    return jnp.concatenate([x1 * cos - x2 * sin, x1 * sin + x2 * cos], axis=-1).astype(x.dtype)


def sgu_mixer(u, v, ln_g, ln_b, w_s, b_s):
    B, S, _ = u.shape
    nb = S // SGU_BLOCK
    v = layernorm(v, ln_g, ln_b)
    vb = v.reshape(B, nb, SGU_BLOCK, SGU_GROUPS, SGU_GROUP_DIM)
    tri = jnp.tril(jnp.ones((SGU_BLOCK, SGU_BLOCK), dtype=bool))
    ws = jnp.where(tri[None], w_s, 0.0).astype(v.dtype)
    mixed = jnp.einsum('gts,bnsgc->bntgc', ws, vb) + b_s.T.astype(v.dtype)[None, None, :, :, None]
    return u * mixed.reshape(B, S, BR_WIDTH)


def mla_mixer(q_down, kv_down, k_rope_in, q_norm_g, kv_norm_g, w_uq, w_ukv, pos):
    B, S, _ = q_down.shape
    cq = rmsnorm(q_down, q_norm_g)
    q = (cq @ w_uq).reshape(B, S, MLA_HEADS, MLA_QK)
    q = jnp.concatenate([q[..., :MLA_NOPE], apply_rope(q[..., MLA_NOPE:], pos)], axis=-1)
    ckv = rmsnorm(kv_down, kv_norm_g)
    kv = (ckv @ w_ukv).reshape(B, S, MLA_HEADS, MLA_NOPE + MLA_V)
    k_nope, v = kv[..., :MLA_NOPE], kv[..., MLA_NOPE:]
    k_r = apply_rope(k_rope_in[:, :, None, :], pos)
    k = jnp.concatenate([k_nope, jnp.broadcast_to(k_r, (B, S, MLA_HEADS, MLA_ROPE))], axis=-1)
    scale = MLA_QK ** -0.5
    nqb = S // Q_BLOCK
    qb = q.reshape(B, nqb, Q_BLOCK, MLA_HEADS, MLA_QK).transpose(1, 0, 2, 3, 4)
    key_chunk = jnp.arange(S) // CHUNK

    def block(args):
        qi, bi = args
        s = jnp.einsum('bqhd,bkhd->bhqk', qi, k).astype(jnp.float32) * scale
        q_chunk = (bi * Q_BLOCK + jnp.arange(Q_BLOCK)) // CHUNK
        mask = key_chunk[None, :] <= q_chunk[:, None]
        s = jnp.where(mask[None, None], s, NEG_INF)
        p = jax.nn.softmax(s, axis=-1).astype(v.dtype)
        return jnp.einsum('bhqk,bkhd->bqhd', p, v)

    o = lax.map(block, (qb, jnp.arange(nqb)))
    return o.transpose(1, 0, 2, 3, 4).reshape(B, S, MLA_HEADS * MLA_V)


def chunk_band_mixer(q, k, v, rel_table):
    B, S, _ = q.shape
    nc = S // CHUNK
    pad = LEFT_CHUNKS * CHUNK
    q = q.reshape(B, S, CA_HEADS, CA_HEAD_DIM)
    k = k.reshape(B, S, CA_HEADS, CA_HEAD_DIM)
    v = v.reshape(B, S, CA_HEADS, CA_HEAD_DIM)
    kp = jnp.pad(k, ((0, 0), (pad, 0), (0, 0), (0, 0)))
    vp = jnp.pad(v, ((0, 0), (pad, 0), (0, 0), (0, 0)))
    qc = q.reshape(B, nc, CHUNK, CA_HEADS, CA_HEAD_DIM).transpose(1, 0, 2, 3, 4)
    i = jnp.arange(CHUNK)
    j = jnp.arange(BAND)
    dist = i[:, None] + pad - j[None, :]
    idx = jnp.clip(dist, -REL_CLIP, REL_CLIP) + REL_CLIP
    bias = rel_table[:, idx].astype(jnp.float32)
    scale = CA_HEAD_DIM ** -0.5

    def chunk(args):
        qi, ci = args
        kb = lax.dynamic_slice_in_dim(kp, ci * CHUNK, BAND, axis=1)
        vb = lax.dynamic_slice_in_dim(vp, ci * CHUNK, BAND, axis=1)
        s = jnp.einsum('bqhd,bkhd->bhqk', qi, kb).astype(jnp.float32) * scale + bias[None]
        valid = j >= (LEFT_CHUNKS - ci) * CHUNK
        s = jnp.where(valid[None, None, None, :], s, NEG_INF)
        p = jax.nn.softmax(s, axis=-1).astype(vb.dtype)
        return jnp.einsum('bhqk,bkhd->bqhd', p, vb)

    o = lax.map(chunk, (qc, jnp.arange(nc)))
    return o.transpose(1, 0, 2, 3, 4).reshape(B, S, BR_WIDTH)


def _fwd_setup_inputs(seed: int = 0) -> dict:
    key = jax.random.key(seed)
    ks = jax.random.split(key, 16)
    f32 = jnp.float32
    L, D = DEPTH, D_MODEL
    nrm = lambda k, shape, s: jax.random.normal(k, shape, f32) * s
    return {
        "x": jax.random.normal(ks[0], (BATCH, SEQ, D), f32),
        "w_in": nrm(ks[1], (L, D, D_IN), D ** -0.5),
        "pre_g": 1.0 + nrm(ks[2], (L, D), 0.1),
        "post_g": 1.0 + nrm(ks[3], (L, D), 0.1),
        "sgu_ln_g": 1.0 + nrm(ks[4], (L, BR_WIDTH), 0.1),
        "sgu_ln_b": nrm(ks[5], (L, BR_WIDTH), 0.02),
        "sgu_w": nrm(ks[6], (L, SGU_GROUPS, SGU_BLOCK, SGU_BLOCK), SGU_BLOCK ** -0.5),
        "sgu_b": 1.0 + nrm(ks[7], (L, SGU_GROUPS, SGU_BLOCK), 0.1),
        "mla_q_norm_g": 1.0 + nrm(ks[8], (L, MLA_Q_RANK), 0.1),
        "mla_kv_norm_g": 1.0 + nrm(ks[9], (L, MLA_KV_RANK), 0.1),
        "mla_w_uq": nrm(ks[10], (L, MLA_Q_RANK, MLA_HEADS * MLA_QK), MLA_Q_RANK ** -0.5),
        "mla_w_ukv": nrm(ks[11], (L, MLA_KV_RANK, MLA_HEADS * (MLA_NOPE + MLA_V)), MLA_KV_RANK ** -0.5),
        "ca_rel_bias": nrm(ks[12], (L, CA_HEADS, 2 * REL_CLIP + 1), 0.5),
        "w_branch": nrm(ks[13], (L, N_BRANCH, BR_WIDTH, D), BR_WIDTH ** -0.5),
        "gate_b": nrm(ks[14], (L, N_BRANCH, D), 0.1),
        "w_out": nrm(ks[15], (L, D, D), D ** -0.5),
    }


def _fwd_reference(x, w_in, pre_g, post_g, sgu_ln_g, sgu_ln_b, sgu_w, sgu_b,
              mla_q_norm_g, mla_kv_norm_g, mla_w_uq, mla_w_ukv, ca_rel_bias,
              w_branch, gate_b, w_out):
    B, S, D = x.shape
    pos = jnp.arange(S)
    offsets = [0]
    for w in IN_WIDTHS:
        offsets.append(offsets[-1] + w)
    for l in range(DEPTH):
        xn = rmsnorm(x, pre_g[l])
        proj = xn @ w_in[l]
        (u_a, v_a, z_a, qd_b, kvd_b, kr_b, z_b,
         q_c, k_c, v_c, z_c, g_logits) = [proj[..., offsets[n]:offsets[n + 1]] for n in range(len(IN_WIDTHS))]
        y_a = sgu_mixer(u_a, v_a, sgu_ln_g[l], sgu_ln_b[l], sgu_w[l], sgu_b[l]) * jax.nn.silu(z_a)
        y_b = mla_mixer(qd_b, kvd_b, kr_b, mla_q_norm_g[l], mla_kv_norm_g[l],
                        mla_w_uq[l], mla_w_ukv[l], pos) * jax.nn.silu(z_b)
        y_c = chunk_band_mixer(q_c, k_c, v_c, ca_rel_bias[l]) * jax.nn.silu(z_c)
        ys = jnp.stack([y_a, y_b, y_c], axis=2)
        br = jnp.einsum('bsnc,ncd->bsnd', ys, w_branch[l])
        gates = jax.nn.sigmoid(g_logits.reshape(B, S, N_BRANCH, D) + gate_b[l])
        merged = jnp.sum(gates * br, axis=2)
        x = x + rmsnorm(merged @ w_out[l], post_g[l])
    return x


import jax as _jax
import jax.numpy as _jnp

TWIN_FORMAT = 'train_step'
FWD_PARAMS = ['x', 'w_in', 'pre_g', 'post_g', 'sgu_ln_g', 'sgu_ln_b', 'sgu_w', 'sgu_b', 'mla_q_norm_g', 'mla_kv_norm_g', 'mla_w_uq', 'mla_w_ukv', 'ca_rel_bias', 'w_branch', 'gate_b', 'w_out']
TWIN_WEIGHTS = ['w_in', 'pre_g', 'post_g', 'sgu_ln_g', 'sgu_ln_b', 'sgu_w', 'sgu_b', 'mla_q_norm_g', 'mla_kv_norm_g', 'mla_w_uq', 'mla_w_ukv', 'ca_rel_bias', 'w_branch', 'gate_b', 'w_out']
TWIN_DIFF_INPUT = 'x'
TWIN_INPUTS = ['x', 'w_in', 'pre_g', 'post_g', 'sgu_ln_g', 'sgu_ln_b', 'sgu_w', 'sgu_b', 'mla_q_norm_g', 'mla_kv_norm_g', 'mla_w_uq', 'mla_w_ukv', 'ca_rel_bias', 'w_branch', 'gate_b', 'w_out', 'loss_target', 'm_w_in', 'm_pre_g', 'm_post_g', 'm_sgu_ln_g', 'm_sgu_ln_b', 'm_sgu_w', 'm_sgu_b', 'm_mla_q_norm_g', 'm_mla_kv_norm_g', 'm_mla_w_uq', 'm_mla_w_ukv', 'm_ca_rel_bias', 'm_w_branch', 'm_gate_b', 'm_w_out', 'v_w_in', 'v_pre_g', 'v_post_g', 'v_sgu_ln_g', 'v_sgu_ln_b', 'v_sgu_w', 'v_sgu_b', 'v_mla_q_norm_g', 'v_mla_kv_norm_g', 'v_mla_w_uq', 'v_mla_w_ukv', 'v_ca_rel_bias', 'v_w_branch', 'v_gate_b', 'v_w_out']
TWIN_OUTPUTS = ['loss', 'grad_x', 'grad_w_in', 'grad_pre_g', 'grad_post_g', 'grad_sgu_ln_g', 'grad_sgu_ln_b', 'grad_sgu_w', 'grad_sgu_b', 'grad_mla_q_norm_g', 'grad_mla_kv_norm_g', 'grad_mla_w_uq', 'grad_mla_w_ukv', 'grad_ca_rel_bias', 'grad_w_branch', 'grad_gate_b', 'grad_w_out', 'delta_w_in', 'delta_pre_g', 'delta_post_g', 'delta_sgu_ln_g', 'delta_sgu_ln_b', 'delta_sgu_w', 'delta_sgu_b', 'delta_mla_q_norm_g', 'delta_mla_kv_norm_g', 'delta_mla_w_uq', 'delta_mla_w_ukv', 'delta_ca_rel_bias', 'delta_w_branch', 'delta_gate_b', 'delta_w_out', 'new_m_w_in', 'new_m_pre_g', 'new_m_post_g', 'new_m_sgu_ln_g', 'new_m_sgu_ln_b', 'new_m_sgu_w', 'new_m_sgu_b', 'new_m_mla_q_norm_g', 'new_m_mla_kv_norm_g', 'new_m_mla_w_uq', 'new_m_mla_w_ukv', 'new_m_ca_rel_bias', 'new_m_w_branch', 'new_m_gate_b', 'new_m_w_out', 'new_v_w_in', 'new_v_pre_g', 'new_v_post_g', 'new_v_sgu_ln_g', 'new_v_sgu_ln_b', 'new_v_sgu_w', 'new_v_sgu_b', 'new_v_mla_q_norm_g', 'new_v_mla_kv_norm_g', 'new_v_mla_w_uq', 'new_v_mla_w_ukv', 'new_v_ca_rel_bias', 'new_v_w_branch', 'new_v_gate_b', 'new_v_w_out']
TWIN_LEAF_KINDS = {'loss': 'loss', 'grad_x': 'grad_x', 'grad_w_in': 'grad_w', 'grad_pre_g': 'grad_w', 'grad_post_g': 'grad_w', 'grad_sgu_ln_g': 'grad_w', 'grad_sgu_ln_b': 'grad_w', 'grad_sgu_w': 'grad_w', 'grad_sgu_b': 'grad_w', 'grad_mla_q_norm_g': 'grad_w', 'grad_mla_kv_norm_g': 'grad_w', 'grad_mla_w_uq': 'grad_w', 'grad_mla_w_ukv': 'grad_w', 'grad_ca_rel_bias': 'grad_w', 'grad_w_branch': 'grad_w', 'grad_gate_b': 'grad_w', 'grad_w_out': 'grad_w', 'delta_w_in': 'delta_w', 'delta_pre_g': 'delta_w', 'delta_post_g': 'delta_w', 'delta_sgu_ln_g': 'delta_w', 'delta_sgu_ln_b': 'delta_w', 'delta_sgu_w': 'delta_w', 'delta_sgu_b': 'delta_w', 'delta_mla_q_norm_g': 'delta_w', 'delta_mla_kv_norm_g': 'delta_w', 'delta_mla_w_uq': 'delta_w', 'delta_mla_w_ukv': 'delta_w', 'delta_ca_rel_bias': 'delta_w', 'delta_w_branch': 'delta_w', 'delta_gate_b': 'delta_w', 'delta_w_out': 'delta_w', 'new_m_w_in': 'new_m', 'new_m_pre_g': 'new_m', 'new_m_post_g': 'new_m', 'new_m_sgu_ln_g': 'new_m', 'new_m_sgu_ln_b': 'new_m', 'new_m_sgu_w': 'new_m', 'new_m_sgu_b': 'new_m', 'new_m_mla_q_norm_g': 'new_m', 'new_m_mla_kv_norm_g': 'new_m', 'new_m_mla_w_uq': 'new_m', 'new_m_mla_w_ukv': 'new_m', 'new_m_ca_rel_bias': 'new_m', 'new_m_w_branch': 'new_m', 'new_m_gate_b': 'new_m', 'new_m_w_out': 'new_m', 'new_v_w_in': 'new_v', 'new_v_pre_g': 'new_v', 'new_v_post_g': 'new_v', 'new_v_sgu_ln_g': 'new_v', 'new_v_sgu_ln_b': 'new_v', 'new_v_sgu_w': 'new_v', 'new_v_sgu_b': 'new_v', 'new_v_mla_q_norm_g': 'new_v', 'new_v_mla_kv_norm_g': 'new_v', 'new_v_mla_w_uq': 'new_v', 'new_v_mla_w_ukv': 'new_v', 'new_v_ca_rel_bias': 'new_v', 'new_v_w_branch': 'new_v', 'new_v_gate_b': 'new_v', 'new_v_w_out': 'new_v'}


def _forward(args):
    return _fwd_reference(*[args[k] for k in FWD_PARAMS])


def _output_shape():
    def fwd():
        inp = _fwd_setup_inputs(0)
        return _fwd_reference(*[inp[k] for k in FWD_PARAMS])
    out = _jax.eval_shape(fwd)
    return out.shape, out.dtype

N_MICROBATCH = 1
ADAM_LR = 0.001
ADAM_B1 = 0.9
ADAM_B2 = 0.999
ADAM_EPS = 1e-08
ADAM_WD = 0.01
ADAM_STEP = 10
PER_EXAMPLE_BATCH_AXIS = {'x': 0, 'loss_target': 0}
SHARED_INPUTS = []
_WEIGHT_DTYPES = {'w_in': _jnp.float32, 'pre_g': _jnp.float32, 'post_g': _jnp.float32, 'sgu_ln_g': _jnp.float32, 'sgu_ln_b': _jnp.float32, 'sgu_w': _jnp.float32, 'sgu_b': _jnp.float32, 'mla_q_norm_g': _jnp.float32, 'mla_kv_norm_g': _jnp.float32, 'mla_w_uq': _jnp.float32, 'mla_w_ukv': _jnp.float32, 'ca_rel_bias': _jnp.float32, 'w_branch': _jnp.float32, 'gate_b': _jnp.float32, 'w_out': _jnp.float32}
MOMENT_SCALE = {'w_in': 6.273877e-01, 'pre_g': 1.793383e+00, 'post_g': 6.434799e+01, 'sgu_ln_g': 7.947918e-01, 'sgu_ln_b': 8.034658e-01, 'sgu_w': 5.548910e-01, 'sgu_b': 8.115724e-01, 'mla_q_norm_g': 2.649043e-01, 'mla_kv_norm_g': 5.822929e-01, 'mla_w_uq': 1.610886e-01, 'mla_w_ukv': 1.914338e-01, 'ca_rel_bias': 9.463004e-02, 'w_branch': 7.205441e-01, 'gate_b': 3.082918e-01, 'w_out': 1.306398e+00}


def _to_microbatches(a, axis):
    t = _jnp.moveaxis(a, axis, 0)
    t = t.reshape((N_MICROBATCH, t.shape[0] // N_MICROBATCH) + t.shape[1:])
    return _jnp.moveaxis(t, 1, axis + 1)


def setup_inputs(seed: int = 0) -> dict:
    inp = _fwd_setup_inputs(seed)
    key = _jax.random.fold_in(_jax.random.key(seed), 7919)
    shape, _ = _output_shape()
    out = dict(inp)
    out["loss_target"] = _jax.random.normal(_jax.random.fold_in(key, 0), shape, _jnp.float32)
    for i, name in enumerate(TWIN_WEIGHTS):
        w = inp[name].astype(_jnp.float32)
        if MOMENT_SCALE is None:
            s = _jnp.sqrt(_jnp.mean(_jnp.square(w)) + 1e-30)
        else:
            s = MOMENT_SCALE[name]
        km, kv = _jax.random.split(_jax.random.fold_in(key, i + 1))
        out[name] = w
        out["m_" + name] = s * _jax.random.normal(km, w.shape, _jnp.float32)
        out["v_" + name] = (s * s) * _jax.random.uniform(kv, w.shape, _jnp.float32, 0.5, 1.5)
    if N_MICROBATCH > 1:
        for name, axis in PER_EXAMPLE_BATCH_AXIS.items():
            out[name] = _to_microbatches(out[name], axis)
    return {'x': out['x'], 'w_in': out['w_in'], 'pre_g': out['pre_g'], 'post_g': out['post_g'], 'sgu_ln_g': out['sgu_ln_g'], 'sgu_ln_b': out['sgu_ln_b'], 'sgu_w': out['sgu_w'], 'sgu_b': out['sgu_b'], 'mla_q_norm_g': out['mla_q_norm_g'], 'mla_kv_norm_g': out['mla_kv_norm_g'], 'mla_w_uq': out['mla_w_uq'], 'mla_w_ukv': out['mla_w_ukv'], 'ca_rel_bias': out['ca_rel_bias'], 'w_branch': out['w_branch'], 'gate_b': out['gate_b'], 'w_out': out['w_out'], 'loss_target': out['loss_target'], 'm_w_in': out['m_w_in'], 'm_pre_g': out['m_pre_g'], 'm_post_g': out['m_post_g'], 'm_sgu_ln_g': out['m_sgu_ln_g'], 'm_sgu_ln_b': out['m_sgu_ln_b'], 'm_sgu_w': out['m_sgu_w'], 'm_sgu_b': out['m_sgu_b'], 'm_mla_q_norm_g': out['m_mla_q_norm_g'], 'm_mla_kv_norm_g': out['m_mla_kv_norm_g'], 'm_mla_w_uq': out['m_mla_w_uq'], 'm_mla_w_ukv': out['m_mla_w_ukv'], 'm_ca_rel_bias': out['m_ca_rel_bias'], 'm_w_branch': out['m_w_branch'], 'm_gate_b': out['m_gate_b'], 'm_w_out': out['m_w_out'], 'v_w_in': out['v_w_in'], 'v_pre_g': out['v_pre_g'], 'v_post_g': out['v_post_g'], 'v_sgu_ln_g': out['v_sgu_ln_g'], 'v_sgu_ln_b': out['v_sgu_ln_b'], 'v_sgu_w': out['v_sgu_w'], 'v_sgu_b': out['v_sgu_b'], 'v_mla_q_norm_g': out['v_mla_q_norm_g'], 'v_mla_kv_norm_g': out['v_mla_kv_norm_g'], 'v_mla_w_uq': out['v_mla_w_uq'], 'v_mla_w_ukv': out['v_mla_w_ukv'], 'v_ca_rel_bias': out['v_ca_rel_bias'], 'v_w_branch': out['v_w_branch'], 'v_gate_b': out['v_gate_b'], 'v_w_out': out['v_w_out']}


def _loss(weights, diff, rest, loss_target):
    with _jax.named_scope("forward"):
        args = {**rest, TWIN_DIFF_INPUT: diff, **{k: w.astype(_WEIGHT_DTYPES[k]) for k, w in weights.items()}}
        y = _forward(args)
    with _jax.named_scope("loss_head"):
        err = _jnp.square(y.astype(_jnp.float32) - loss_target)
        return 0.5 * _jnp.sum(_jnp.mean(err, axis=-1)) if err.ndim else 0.5 * err


def _adamw(w, g, m, v):
    m = ADAM_B1 * m + (1.0 - ADAM_B1) * g
    v = ADAM_B2 * v + (1.0 - ADAM_B2) * _jnp.square(g)
    m_hat = m / (1.0 - ADAM_B1 ** ADAM_STEP)
    v_hat = v / (1.0 - ADAM_B2 ** ADAM_STEP)
    delta = -ADAM_LR * (m_hat / (_jnp.sqrt(v_hat) + ADAM_EPS) + ADAM_WD * w)
    return delta, m, v


def reference(x, w_in, pre_g, post_g, sgu_ln_g, sgu_ln_b, sgu_w, sgu_b, mla_q_norm_g, mla_kv_norm_g, mla_w_uq, mla_w_ukv, ca_rel_bias, w_branch, gate_b, w_out, loss_target, m_w_in, m_pre_g, m_post_g, m_sgu_ln_g, m_sgu_ln_b, m_sgu_w, m_sgu_b, m_mla_q_norm_g, m_mla_kv_norm_g, m_mla_w_uq, m_mla_w_ukv, m_ca_rel_bias, m_w_branch, m_gate_b, m_w_out, v_w_in, v_pre_g, v_post_g, v_sgu_ln_g, v_sgu_ln_b, v_sgu_w, v_sgu_b, v_mla_q_norm_g, v_mla_kv_norm_g, v_mla_w_uq, v_mla_w_ukv, v_ca_rel_bias, v_w_branch, v_gate_b, v_w_out):
    given = dict(x=x, w_in=w_in, pre_g=pre_g, post_g=post_g, sgu_ln_g=sgu_ln_g, sgu_ln_b=sgu_ln_b, sgu_w=sgu_w, sgu_b=sgu_b, mla_q_norm_g=mla_q_norm_g, mla_kv_norm_g=mla_kv_norm_g, mla_w_uq=mla_w_uq, mla_w_ukv=mla_w_ukv, ca_rel_bias=ca_rel_bias, w_branch=w_branch, gate_b=gate_b, w_out=w_out, loss_target=loss_target, m_w_in=m_w_in, m_pre_g=m_pre_g, m_post_g=m_post_g, m_sgu_ln_g=m_sgu_ln_g, m_sgu_ln_b=m_sgu_ln_b, m_sgu_w=m_sgu_w, m_sgu_b=m_sgu_b, m_mla_q_norm_g=m_mla_q_norm_g, m_mla_kv_norm_g=m_mla_kv_norm_g, m_mla_w_uq=m_mla_w_uq, m_mla_w_ukv=m_mla_w_ukv, m_ca_rel_bias=m_ca_rel_bias, m_w_branch=m_w_branch, m_gate_b=m_gate_b, m_w_out=m_w_out, v_w_in=v_w_in, v_pre_g=v_pre_g, v_post_g=v_post_g, v_sgu_ln_g=v_sgu_ln_g, v_sgu_ln_b=v_sgu_ln_b, v_sgu_w=v_sgu_w, v_sgu_b=v_sgu_b, v_mla_q_norm_g=v_mla_q_norm_g, v_mla_kv_norm_g=v_mla_kv_norm_g, v_mla_w_uq=v_mla_w_uq, v_mla_w_ukv=v_mla_w_ukv, v_ca_rel_bias=v_ca_rel_bias, v_w_branch=v_w_branch, v_gate_b=v_gate_b, v_w_out=v_w_out)
    weights = {n: given[n] for n in TWIN_WEIGHTS}
    shared = {n: given[n] for n in SHARED_INPUTS}
    per_example = {n: given[n] for n in ['x']}
    grad_fn = _jax.value_and_grad(_loss, argnums=(0, 1))

    def one_microbatch(ex, loss_target):
        ex = dict(ex)
        diff = ex.pop(TWIN_DIFF_INPUT)
        return grad_fn(weights, diff, {**shared, **ex}, loss_target)

    if N_MICROBATCH == 1:
        loss, (grad_w, grad_x) = one_microbatch(per_example, given["loss_target"])
    else:
        def body(carry, xs):
            loss_sum, grad_sum = carry
            l_k, (gw_k, gx_k) = one_microbatch(xs[0], xs[1])
            with _jax.named_scope("update"):
                return (loss_sum + l_k, _jax.tree.map(_jnp.add, grad_sum, gw_k)), gx_k

        init = (_jnp.zeros((), _jnp.float32), _jax.tree.map(_jnp.zeros_like, weights))
        (loss, grad_w), grad_x = _jax.lax.scan(body, init, (per_example, given["loss_target"]))
    with _jax.named_scope("update"):
        delta_w, new_m, new_v = {}, {}, {}
        for n in TWIN_WEIGHTS:
            delta_w[n], new_m[n], new_v[n] = _adamw(weights[n], grad_w[n], given["m_" + n], given["v_" + n])
    return (loss, grad_x, *[grad_w[n] for n in TWIN_WEIGHTS], *[delta_w[n] for n in TWIN_WEIGHTS],
            *[new_m[n] for n in TWIN_WEIGHTS], *[new_v[n] for n in TWIN_WEIGHTS])
```

```python
import functools

import numpy as np
import jax
import jax.numpy as jnp
from jax import lax
from jax.experimental import pallas as pl
from jax.experimental.pallas import tpu as pltpu

F32 = jnp.float32
BF16 = jnp.bfloat16
MESH = pl.DeviceIdType.MESH

EPS = 1e-6
NEG_INF = -1e30
D_MODEL = 1024
BR_WIDTH = 512
N_BRANCH = 3
N_HEADS = 8
HEAD_PAD = 128
CHUNK_SHIFT = 6
SGU_BLOCK = 128
MLA_NOPE, MLA_ROPE, MLA_V = 64, 32, 64
MLA_QK = MLA_NOPE + MLA_ROPE
MLA_Q_RANK, MLA_KV_RANK = 256, 128
CA_HEAD_DIM = 64
LEFT_CHUNKS = 8
REL_CLIP = 128
ROPE_BASE = 10000.0
D_IN = 7584

ADAM_LR, ADAM_B1, ADAM_B2, ADAM_EPS, ADAM_WD, ADAM_STEP = 0.001, 0.9, 0.999, 1e-08, 0.01, 10

P_QC, P_KC, P_VC, P_QD, P_KVD, P_KR, P_ZB, P_ZC, P_G, P_U, P_V, P_ZA, P_W = (
    0, 512, 1024, 1536, 1792, 1920, 2048, 2560, 3072, 6144, 6656, 7168, 7680)

VMEM_LIMIT = 48 * 1024 * 1024
MLA_TQ = 512
CA_TQ = 256
ROW_TILE = 256
MM_TM = 512
MM_TN = 768
ADD_ROWS_TILE = 512

WEIGHTS = ['w_in', 'pre_g', 'post_g', 'sgu_ln_g', 'sgu_ln_b', 'sgu_w', 'sgu_b', 'mla_q_norm_g',
           'mla_kv_norm_g', 'mla_w_uq', 'mla_w_ukv', 'ca_rel_bias', 'w_branch', 'gate_b', 'w_out']
SHARDED = ['w_in', 'mla_w_uq', 'mla_w_ukv', 'w_branch', 'gate_b', 'w_out']
SMALL = ['pre_g', 'post_g', 'sgu_ln_g', 'sgu_ln_b', 'sgu_w', 'sgu_b', 'mla_q_norm_g',
         'mla_kv_norm_g', 'ca_rel_bias']


def _call(body, **kw):
    return pl.pallas_call(body, **kw)


def _params(n_axes):
    return pltpu.CompilerParams(dimension_semantics=("arbitrary",) * n_axes,
                                vmem_limit_bytes=VMEM_LIMIT)


def _nt(a, b):
    return lax.dot_general(a, b, (((1,), (1,)), ((), ())), preferred_element_type=F32)


def _nn(a, b):
    return jnp.dot(a, b, preferred_element_type=F32)


def _tn(a, b):
    return lax.dot_general(a, b, (((0,), (0,)), ((), ())), preferred_element_type=F32)


def _rms(xv, g):
    r = lax.rsqrt(jnp.mean(xv * xv, axis=-1, keepdims=True) + EPS)
    return xv * r * g, r


def _rms_bwd(xv, g, r, dy):
    gy = dy * g
    dx = r * gy - xv * (r * r * r) * jnp.mean(xv * gy, axis=-1, keepdims=True)
    dg = jnp.sum(dy * (xv * r), axis=0, keepdims=True)
    return dx, dg


def _sigmoid(z):
    return 1.0 / (1.0 + jnp.exp(-z))


def _rope(xv, c, a, b):
    return xv * c + pltpu.roll(xv, 112, 1) * a + pltpu.roll(xv, 16, 1) * b


def _accumulate(ref, val, first):
    @pl.when(first)
    def _():
        ref[...] = val

    @pl.when(jnp.logical_not(first))
    def _():
        ref[...] += val


def norm_matmul(x, g, w):
    S, D = x.shape
    N = w.shape[1]
    tm, tn = min(S, MM_TM), MM_TN

    def body(x_ref, g_ref, w_ref, o_ref, xn_ref):
        @pl.when(pl.program_id(1) == 0)
        def _():
            y, _ = _rms(x_ref[...], g_ref[...])
            xn_ref[...] = y.astype(BF16)

        o_ref[...] = _nn(xn_ref[...], w_ref[...])

    return _call(
        body, name="norm_matmul", grid=(S // tm, N // tn),
        in_specs=[pl.BlockSpec((tm, D), lambda i, j: (i, 0)),
                  pl.BlockSpec((1, D), lambda i, j: (0, 0)),
                  pl.BlockSpec((D, tn), lambda i, j: (0, j))],
        out_specs=[pl.BlockSpec((tm, tn), lambda i, j: (i, j)),
                   pl.BlockSpec((tm, D), lambda i, j: (i, 0))],
        out_shape=[jax.ShapeDtypeStruct((S, N), F32), jax.ShapeDtypeStruct((S, D), BF16)],
        compiler_params=_params(2))(x, g, w)


def proj_bwd_x(dproj, w, x, g, resid):
    S, N = dproj.shape
    D = x.shape[1]
    tm, tk = min(S, MM_TM), MM_TN
    nk = N // tk

    def body(dp_ref, w_ref, x_ref, g_ref, r_ref, dx_ref, dg_ref, acc_ref):
        i, k = pl.program_id(0), pl.program_id(1)

        @pl.when(k == 0)
        def _():
            acc_ref[...] = jnp.zeros_like(acc_ref)

        acc_ref[...] += _nt(dp_ref[...].astype(BF16), w_ref[...])

        @pl.when(k == nk - 1)
        def _():
            xv = x_ref[...]
            _, r = _rms(xv, g_ref[...])
            dx, dg = _rms_bwd(xv, g_ref[...], r, acc_ref[...])
            dx_ref[...] = dx + r_ref[...]
            _accumulate(dg_ref, dg, i == 0)

    return _call(
        body, name="proj_bwd_x", grid=(S // tm, nk),
        in_specs=[pl.BlockSpec((tm, tk), lambda i, k: (i, k)),
                  pl.BlockSpec((D, tk), lambda i, k: (0, k)),
                  pl.BlockSpec((tm, D), lambda i, k: (i, 0)),
                  pl.BlockSpec((1, D), lambda i, k: (0, 0)),
                  pl.BlockSpec((tm, D), lambda i, k: (i, 0))],
        out_specs=[pl.BlockSpec((tm, D), lambda i, k: (i, 0)),
                   pl.BlockSpec((1, D), lambda i, k: (0, 0))],
        out_shape=[jax.ShapeDtypeStruct((S, D), F32), jax.ShapeDtypeStruct((1, D), F32)],
        scratch_shapes=[pltpu.VMEM((tm, D), F32)],
        compiler_params=_params(2))(dproj, w, x, g, resid)


def matmul_tn(a, b, tn):
    S, M = a.shape
    N = b.shape[1]
    tk = min(S, 512)

    def body(a_ref, b_ref, o_ref):
        @pl.when(pl.program_id(1) == 0)
        def _():
            o_ref[...] = jnp.zeros_like(o_ref)

        o_ref[...] += _tn(a_ref[...].astype(BF16), b_ref[...].astype(BF16))

    return _call(
        body, name="matmul_tn", grid=(N // tn, S // tk),
        in_specs=[pl.BlockSpec((tk, M), lambda j, k: (k, 0)),
                  pl.BlockSpec((tk, tn), lambda j, k: (k, j))],
        out_specs=pl.BlockSpec((M, tn), lambda j, k: (0, j)),
        out_shape=jax.ShapeDtypeStruct((M, N), F32),
        compiler_params=_params(2))(a, b)


def _sgu_block(vv, g, b, ws_ref, lane):
    mu = jnp.mean(vv, axis=-1, keepdims=True)
    xc = vv - mu
    r = lax.rsqrt(jnp.mean(xc * xc, axis=-1, keepdims=True) + EPS)
    xhat = xc * r
    vln = (xhat * g + b).astype(BF16)
    pieces = []
    for p in range(4):
        vp = vln[:, p * 128:(p + 1) * 128]
        pieces.append(jnp.where(lane < 64, _nn(ws_ref[2 * p], vp), _nn(ws_ref[2 * p + 1], vp)))
    return xhat, r, vln, jnp.concatenate(pieces, axis=1)


def sgu_fwd(proj, ln_g, ln_b, ws, bias_full):
    S = proj.shape[0]
    ts = ROW_TILE

    def body(u_ref, v_ref, z_ref, g_ref, b_ref, ws_ref, bf_ref, y_ref):
        lane = lax.broadcasted_iota(jnp.int32, (SGU_BLOCK, 128), 1)
        for blk in range(ts // SGU_BLOCK):
            rows = slice(blk * SGU_BLOCK, (blk + 1) * SGU_BLOCK)
            _, _, _, mixed = _sgu_block(v_ref[rows, :], g_ref[...], b_ref[...], ws_ref, lane)
            mixed = mixed + bf_ref[...]
            zz = z_ref[rows, :]
            y_ref[rows, :] = (u_ref[rows, :] * mixed * (zz * _sigmoid(zz))).astype(BF16)

    col = lambda c: pl.BlockSpec((ts, BR_WIDTH), lambda i: (i, c))
    full = lambda shape: pl.BlockSpec(shape, lambda i: (0,) * len(shape))
    return _call(
        body, name="sgu_fwd", grid=(S // ts,),
        in_specs=[col(P_U // 512), col(P_V // 512), col(P_ZA // 512),
                  full((1, BR_WIDTH)), full((1, BR_WIDTH)), full((8, 128, 128)), full((128, BR_WIDTH))],
        out_specs=pl.BlockSpec((ts, BR_WIDTH), lambda i: (i, 0)),
        out_shape=jax.ShapeDtypeStruct((S, BR_WIDTH), BF16),
        compiler_params=_params(1))(proj, proj, proj, ln_g, ln_b, ws, bias_full)


def sgu_bwd(dproj, dy, proj, ln_g, ln_b, ws, ws_t, bias_full):
    S = proj.shape[0]
    ts = ROW_TILE

    def body(dp_in, dy_ref, u_ref, v_ref, z_ref, g_ref, b_ref, ws_ref, wst_ref, bf_ref,
             dp_ref, gg_ref, gb_ref, gws_ref, gbf_ref):
        del dp_in
        first = pl.program_id(0) == 0

        @pl.when(first)
        def _():
            gg_ref[...] = jnp.zeros_like(gg_ref)
            gb_ref[...] = jnp.zeros_like(gb_ref)
            gws_ref[...] = jnp.zeros_like(gws_ref)
            gbf_ref[...] = jnp.zeros_like(gbf_ref)

        lane = lax.broadcasted_iota(jnp.int32, (SGU_BLOCK, 128), 1)
        for blk in range(ts // SGU_BLOCK):
            rows = slice(blk * SGU_BLOCK, (blk + 1) * SGU_BLOCK)
            g = g_ref[...]
            xhat, r, vln, mixed = _sgu_block(v_ref[rows, :], g, b_ref[...], ws_ref, lane)
            mixed = mixed + bf_ref[...]
            zz = z_ref[rows, :]
            uu = u_ref[rows, :]
            dyv = dy_ref[0, rows, :]
            sg = _sigmoid(zz)
            sil = zz * sg
            dmixed = dyv * uu * sil
            dp_ref[rows, 0:512] = dyv * mixed * sil
            dp_ref[rows, 1024:1536] = dyv * uu * mixed * (sg * (1.0 + zz * (1.0 - sg)))
            gbf_ref[...] += dmixed
            dmb = dmixed.astype(BF16)
            pieces = []
            for p in range(4):
                dmp = dmb[:, p * 128:(p + 1) * 128]
                vp = vln[:, p * 128:(p + 1) * 128]
                pieces.append(jnp.where(lane < 64, _nn(wst_ref[2 * p], dmp), _nn(wst_ref[2 * p + 1], dmp)))
                zero = jnp.zeros_like(dmp)
                gws_ref[2 * p] += _nt(jnp.where(lane < 64, dmp, zero), vp)
                gws_ref[2 * p + 1] += _nt(jnp.where(lane >= 64, dmp, zero), vp)
            dvln = jnp.concatenate(pieces, axis=1)
            dxh = dvln * g
            dp_ref[rows, 512:1024] = r * (dxh - jnp.mean(dxh, axis=-1, keepdims=True)
                                          - xhat * jnp.mean(dxh * xhat, axis=-1, keepdims=True))
            gg_ref[...] += jnp.sum(dvln * xhat, axis=0, keepdims=True)
            gb_ref[...] += jnp.sum(dvln, axis=0, keepdims=True)

    col = lambda c: pl.BlockSpec((ts, BR_WIDTH), lambda i: (i, c))
    full = lambda shape: pl.BlockSpec(shape, lambda i: (0,) * len(shape))
    return _call(
        body, name="sgu_bwd", grid=(S // ts,),
        in_specs=[pl.BlockSpec(memory_space=pl.ANY),
                  pl.BlockSpec((1, ts, BR_WIDTH), lambda i: (0, i, 0)),
                  col(P_U // 512), col(P_V // 512), col(P_ZA // 512),
                  full((1, BR_WIDTH)), full((1, BR_WIDTH)), full((8, 128, 128)), full((8, 128, 128)),
                  full((128, BR_WIDTH))],
        out_specs=[pl.BlockSpec((ts, 1536), lambda i: (i, P_U // 1536)),
                   full((1, BR_WIDTH)), full((1, BR_WIDTH)), full((8, 128, 128)), full((128, BR_WIDTH))],
        out_shape=[jax.ShapeDtypeStruct(dproj.shape, F32),
                   jax.ShapeDtypeStruct((1, BR_WIDTH), F32), jax.ShapeDtypeStruct((1, BR_WIDTH), F32),
                   jax.ShapeDtypeStruct((8, 128, 128), F32), jax.ShapeDtypeStruct((128, BR_WIDTH), F32)],
        input_output_aliases={0: 0},
        compiler_params=_params(1))(dproj, dy, proj, proj, proj, ln_g, ln_b, ws, ws_t, bias_full)


def _hspec(ts):
    return pl.BlockSpec((N_HEADS, ts, HEAD_PAD), lambda i: (0, i, 0))


def prep_fwd(proj, tab, qg, kvg, wq, wk, wv):
    S = proj.shape[0]
    ts = ROW_TILE

    def body(qc_ref, kc_ref, vc_ref, qd_ref, kvd_ref, kr_ref, tab_ref, qg_ref, kvg_ref,
             wq_ref, wk_ref, wv_ref, qb, kb, vb, qc, kc, vc, cq_o, ckv_o):
        c, a, b = tab_ref[0], tab_ref[1], tab_ref[2]
        cq, _ = _rms(qd_ref[...], qg_ref[...])
        ckv, _ = _rms(kvd_ref[...], kvg_ref[...])
        cqb, ckvb = cq.astype(BF16), ckv.astype(BF16)
        cq_o[...] = cqb
        ckv_o[...] = ckvb
        krr = _rope(kr_ref[...], c, a, b)
        for h in range(N_HEADS):
            cols = slice(h * HEAD_PAD, (h + 1) * HEAD_PAD)
            qb[h] = _rope(_nn(cqb, wq_ref[:, cols]), c, a, b).astype(BF16)
            kb[h] = (_nn(ckvb, wk_ref[:, cols]) + krr).astype(BF16)
            vb[h] = _nn(ckvb, wv_ref[:, cols]).astype(BF16)
        lane = lax.broadcasted_iota(jnp.int32, (ts, 128), 1)
        for src, dst in ((qc_ref, qc), (kc_ref, kc), (vc_ref, vc)):
            for p in range(4):
                piece = src[:, p * 128:(p + 1) * 128]
                dst[2 * p] = jnp.where(lane < 64, piece, 0.0).astype(BF16)
                dst[2 * p + 1] = jnp.where(lane < 64, pltpu.roll(piece, 64, 1), 0.0).astype(BF16)

    col = lambda w, c: pl.BlockSpec((ts, w), lambda i: (i, c))
    full = lambda shape: pl.BlockSpec(shape, lambda i: (0,) * len(shape))
    hshape = jax.ShapeDtypeStruct((N_HEADS, S, HEAD_PAD), BF16)
    return _call(
        body, name="prep_fwd", grid=(S // ts,),
        in_specs=[col(512, P_QC // 512), col(512, P_KC // 512), col(512, P_VC // 512),
                  col(256, P_QD // 256), col(128, P_KVD // 128), col(128, P_KR // 128),
                  pl.BlockSpec((3, ts, 128), lambda i: (0, i, 0)),
                  full((1, MLA_Q_RANK)), full((1, MLA_KV_RANK)),
                  full((MLA_Q_RANK, 1024)), full((MLA_KV_RANK, 1024)), full((MLA_KV_RANK, 1024))],
        out_specs=[_hspec(ts)] * 6 + [pl.BlockSpec((ts, MLA_Q_RANK), lambda i: (i, 0)),
                                      pl.BlockSpec((ts, MLA_KV_RANK), lambda i: (i, 0))],
        out_shape=[hshape] * 6 + [jax.ShapeDtypeStruct((S, MLA_Q_RANK), BF16),
                                  jax.ShapeDtypeStruct((S, MLA_KV_RANK), BF16)],
        compiler_params=_params(1))(proj, proj, proj, proj, proj, proj, tab, qg, kvg, wq, wk, wv)


def prep_bwd(dproj, dqb, dkb, dvb, dqc, dkc, dvc, proj, tab, qg, kvg, wq, wk, wv):
    S = proj.shape[0]
    ts = ROW_TILE

    def body(dp_in, dqb_r, dkb_r, dvb_r, dqc_r, dkc_r, dvc_r, qd_ref, kvd_ref, tab_ref, qg_ref, kvg_ref,
             wq_ref, wk_ref, wv_ref, dp_ref, dqf, dkf, dvf, gq_ref, gkv_ref):
        del dp_in
        c, a, b = tab_ref[0], -tab_ref[1], -tab_ref[2]
        qd, kvd = qd_ref[...], kvd_ref[...]
        _, rq = _rms(qd, qg_ref[...])
        _, rkv = _rms(kvd, kvg_ref[...])
        dcq = jnp.zeros((ts, MLA_Q_RANK), F32)
        dckv = jnp.zeros((ts, MLA_KV_RANK), F32)
        dksum = jnp.zeros((ts, HEAD_PAD), F32)
        for h in range(N_HEADS):
            cols = slice(h * HEAD_PAD, (h + 1) * HEAD_PAD)
            dqh = _rope(dqb_r[h], c, a, b).astype(BF16)
            dqf[:, cols] = dqh
            dcq = dcq + _nt(dqh, wq_ref[:, cols])
            dk = dkb_r[h]
            dksum = dksum + dk
            dkh = dk.astype(BF16)
            dkf[:, cols] = dkh
            dvh = dvb_r[h].astype(BF16)
            dvf[:, cols] = dvh
            dckv = dckv + _nt(dkh, wk_ref[:, cols]) + _nt(dvh, wv_ref[:, cols])
        lane = lax.broadcasted_iota(jnp.int32, (ts, 128), 1)
        rope_lanes = jnp.logical_and(lane >= MLA_NOPE, lane < MLA_QK)
        dp_ref[:, P_KR:P_KR + 128] = jnp.where(rope_lanes, _rope(dksum, c, a, b), 0.0)
        dqd, gq = _rms_bwd(qd, qg_ref[...], rq, dcq)
        dkvd, gkv = _rms_bwd(kvd, kvg_ref[...], rkv, dckv)
        dp_ref[:, P_QD:P_QD + 256] = dqd
        dp_ref[:, P_KVD:P_KVD + 128] = dkvd
        first = pl.program_id(0) == 0
        _accumulate(gq_ref, gq, first)
        _accumulate(gkv_ref, gkv, first)
        for src, base in ((dqc_r, P_QC), (dkc_r, P_KC), (dvc_r, P_VC)):
            for p in range(4):
                dp_ref[:, base + p * 128:base + (p + 1) * 128] = src[2 * p] + pltpu.roll(src[2 * p + 1], 64, 1)

    col = lambda w, c: pl.BlockSpec((ts, w), lambda i: (i, c))
    full = lambda shape: pl.BlockSpec(shape, lambda i: (0,) * len(shape))
    wide = jax.ShapeDtypeStruct((S, 1024), BF16)
    return _call(
        body, name="prep_bwd", grid=(S // ts,),
        in_specs=[pl.BlockSpec(memory_space=pl.ANY)] + [_hspec(ts)] * 6 +
                 [col(256, P_QD // 256), col(128, P_KVD // 128),
                  pl.BlockSpec((3, ts, 128), lambda i: (0, i, 0)),
                  full((1, MLA_Q_RANK)), full((1, MLA_KV_RANK)),
                  full((MLA_Q_RANK, 1024)), full((MLA_KV_RANK, 1024)), full((MLA_KV_RANK, 1024))],
        out_specs=[pl.BlockSpec((ts, 2048), lambda i: (i, 0))] + [pl.BlockSpec((ts, 1024), lambda i: (i, 0))] * 3 +
                  [full((1, MLA_Q_RANK)), full((1, MLA_KV_RANK))],
        out_shape=[jax.ShapeDtypeStruct(dproj.shape, F32), wide, wide, wide,
                   jax.ShapeDtypeStruct((1, MLA_Q_RANK), F32), jax.ShapeDtypeStruct((1, MLA_KV_RANK), F32)],
        input_output_aliases={0: 0},
        compiler_params=_params(1))(dproj, dqb, dkb, dvb, dqc, dkc, dvc, proj, proj, tab, qg, kvg, wq, wk, wv)


def _visible(band, rel, rows_are_keys, t):
    r = lax.broadcasted_iota(jnp.int32, (t, t), 0) >> CHUNK_SHIFT
    c = lax.broadcasted_iota(jnp.int32, (t, t), 1) >> CHUNK_SHIFT
    d = (c - r) if rows_are_keys else (r - c)
    if not band:
        return d >= 0
    d = d + rel * (t >> CHUNK_SHIFT)
    return jnp.logical_and(d >= 0, d <= LEFT_CHUNKS)


def flash_fwd(q, k, v, bias, band, scale):
    H, S, _ = q.shape
    t = CA_TQ if band else MLA_TQ
    nq = S // t
    nk = 3 if band else nq
    kvb = (lambda i, j: jnp.maximum(i - 2 + j, 0)) if band else (lambda i, j: jnp.minimum(j, i))

    def body(*refs):
        if band:
            q_ref, k_ref, v_ref, b_ref, o_ref, lse_ref, m_s, l_s, acc_s = refs
        else:
            q_ref, k_ref, v_ref, o_ref, lse_ref, m_s, l_s, acc_s = refs
        i, j = pl.program_id(1), pl.program_id(2)

        @pl.when(j == 0)
        def _():
            m_s[...] = jnp.full_like(m_s, NEG_INF)
            l_s[...] = jnp.zeros_like(l_s)
            acc_s[...] = jnp.zeros_like(acc_s)

        def step(masked):
            s = _nt(q_ref[0], k_ref[0]) * scale
            if band:
                s = s + b_ref[0, 0]
            if masked:
                s = jnp.where(_visible(band, 2 - j, False, t), s, NEG_INF)
            m_prev = m_s[...]
            m_new = jnp.maximum(m_prev, jnp.max(s, axis=-1, keepdims=True))
            alpha = jnp.exp(m_prev - m_new)
            p = jnp.exp(s - m_new)
            l_s[...] = alpha * l_s[...] + jnp.sum(p, axis=-1, keepdims=True)
            acc_s[...] = alpha * acc_s[...] + _nn(p.astype(BF16), v_ref[0])
            m_s[...] = m_new

        def finish():
            o_ref[0] = acc_s[...] / l_s[...]
            lse_ref[0] = m_s[...] + jnp.log(l_s[...])

        if band:
            @pl.when(i - 2 + j >= 0)
            def _():
                step(True)

            @pl.when(j == 2)
            def _():
                finish()
        else:
            @pl.when(j < i)
            def _():
                step(False)

            @pl.when(j == i)
            def _():
                step(True)
                finish()

    tile = lambda f: pl.BlockSpec((1, t, HEAD_PAD), f)
    in_specs = [tile(lambda h, i, j: (h, i, 0)), tile(lambda h, i, j: (h, kvb(i, j), 0)),
                tile(lambda h, i, j: (h, kvb(i, j), 0))]
    args = [q, k, v]
    if band:
        in_specs.append(pl.BlockSpec((1, 1, t, t), lambda h, i, j: (h, j, 0, 0)))
        args.append(bias)
    return _call(
        body, name="band_fwd" if band else "mla_fwd", grid=(H, nq, nk), in_specs=in_specs,
        out_specs=[tile(lambda h, i, j: (h, i, 0)), pl.BlockSpec((1, t, 1), lambda h, i, j: (h, i, 0))],
        out_shape=[jax.ShapeDtypeStruct((H, S, HEAD_PAD), F32), jax.ShapeDtypeStruct((H, S, 1), F32)],
        scratch_shapes=[pltpu.VMEM((t, 1), F32), pltpu.VMEM((t, 1), F32), pltpu.VMEM((t, HEAD_PAD), F32)],
        compiler_params=_params(3))(*args)


def flash_bwd_q(q, k, v, do, lse, delta, bias, band, scale):
    H, S, _ = q.shape
    t = CA_TQ if band else MLA_TQ
    nq = S // t
    nk = 3 if band else nq
    kvb = (lambda i, j: jnp.maximum(i - 2 + j, 0)) if band else (lambda i, j: jnp.minimum(j, i))

    def body(*refs):
        if band:
            q_ref, k_ref, v_ref, do_ref, lse_ref, dl_ref, b_ref, dq_ref, db_ref, acc_s = refs
        else:
            q_ref, k_ref, v_ref, do_ref, lse_ref, dl_ref, dq_ref, acc_s = refs
        i, j = pl.program_id(1), pl.program_id(2)

        @pl.when(j == 0)
        def _():
            acc_s[...] = jnp.zeros_like(acc_s)

        if band:
            @pl.when(jnp.logical_and(i == 0, j == 0))
            def _():
                db_ref[...] = jnp.zeros_like(db_ref)

        def step(masked):
            s = _nt(q_ref[0], k_ref[0]) * scale
            if band:
                s = s + b_ref[0, 0]
            if masked:
                s = jnp.where(_visible(band, 2 - j, False, t), s, NEG_INF)
            p = jnp.exp(s - lse_ref[0])
            ds = p * (_nt(do_ref[0], v_ref[0]) - dl_ref[0])
            if band:
                db_ref[0, j] += ds
            acc_s[...] += _nn(ds.astype(BF16), k_ref[0])

        def finish():
            dq_ref[0] = acc_s[...] * scale

        if band:
            @pl.when(i - 2 + j >= 0)
            def _():
                step(True)

            @pl.when(j == 2)
            def _():
                finish()
        else:
            @pl.when(j < i)
            def _():
                step(False)

            @pl.when(j == i)
            def _():
                step(True)
                finish()

    tile = lambda f: pl.BlockSpec((1, t, HEAD_PAD), f)
    stat = pl.BlockSpec((1, t, 1), lambda h, i, j: (h, i, 0))
    in_specs = [tile(lambda h, i, j: (h, i, 0)), tile(lambda h, i, j: (h, kvb(i, j), 0)),
                tile(lambda h, i, j: (h, kvb(i, j), 0)), tile(lambda h, i, j: (h, i, 0)), stat, stat]
    args = [q, k, v, do, lse, delta]
    out_specs = [tile(lambda h, i, j: (h, i, 0))]
    out_shape = [jax.ShapeDtypeStruct((H, S, HEAD_PAD), F32)]
    if band:
        in_specs.append(pl.BlockSpec((1, 1, t, t), lambda h, i, j: (h, j, 0, 0)))
        args.append(bias)
        out_specs.append(pl.BlockSpec((1, 3, t, t), lambda h, i, j: (h, 0, 0, 0)))
        out_shape.append(jax.ShapeDtypeStruct((H, 3, t, t), F32))
    return _call(
        body, name="band_bwd_q" if band else "mla_bwd_q", grid=(H, nq, nk), in_specs=in_specs,
        out_specs=out_specs, out_shape=out_shape,
        scratch_shapes=[pltpu.VMEM((t, HEAD_PAD), F32)],
        compiler_params=_params(3))(*args)


def flash_bwd_kv(q, k, v, do, lse_t, delta_t, bias_t, band, scale):
    H, S, _ = q.shape
    t = CA_TQ if band else MLA_TQ
    nq = S // t
    nj = 3 if band else nq
    qb = lambda kb, j: jnp.minimum(kb + j, nq - 1)

    def body(*refs):
        if band:
            q_ref, k_ref, v_ref, do_ref, lse_ref, dl_ref, b_ref, dk_ref, dv_ref, dk_s, dv_s = refs
        else:
            q_ref, k_ref, v_ref, do_ref, lse_ref, dl_ref, dk_ref, dv_ref, dk_s, dv_s = refs
        kb, j = pl.program_id(1), pl.program_id(2)

        @pl.when(j == 0)
        def _():
            dk_s[...] = jnp.zeros_like(dk_s)
            dv_s[...] = jnp.zeros_like(dv_s)

        def step(masked):
            st = _nt(k_ref[0], q_ref[0]) * scale
            if band:
                st = st + b_ref[0, 0]
            if masked:
                st = jnp.where(_visible(band, j, True, t), st, NEG_INF)
            pt = jnp.exp(st - lse_ref[0])
            dv_s[...] += _nn(pt.astype(BF16), do_ref[0])
            dst = pt * (_nt(v_ref[0], do_ref[0]) - dl_ref[0])
            dk_s[...] += _nn(dst.astype(BF16), q_ref[0])

        if band:
            @pl.when(kb + j < nq)
            def _():
                step(True)
        else:
            @pl.when(j == 0)
            def _():
                step(True)

            @pl.when(jnp.logical_and(j > 0, kb + j < nq))
            def _():
                step(False)

        @pl.when(j == nj - 1)
        def _():
            dk_ref[0] = dk_s[...] * scale
            dv_ref[0] = dv_s[...]

    tile = lambda f: pl.BlockSpec((1, t, HEAD_PAD), f)
    stat = pl.BlockSpec((1, 1, t), lambda h, kb, j: (h, 0, qb(kb, j)))
    in_specs = [tile(lambda h, kb, j: (h, qb(kb, j), 0)), tile(lambda h, kb, j: (h, kb, 0)),
                tile(lambda h, kb, j: (h, kb, 0)), tile(lambda h, kb, j: (h, qb(kb, j), 0)), stat, stat]
    args = [q, k, v, do, lse_t, delta_t]
    if band:
        in_specs.append(pl.BlockSpec((1, 1, t, t), lambda h, kb, j: (h, 2 - j, 0, 0)))
        args.append(bias_t)
    return _call(
        body, name="band_bwd_kv" if band else "mla_bwd_kv", grid=(H, nq, nj), in_specs=in_specs,
        out_specs=[tile(lambda h, kb, j: (h, kb, 0)), tile(lambda h, kb, j: (h, kb, 0))],
        out_shape=[jax.ShapeDtypeStruct((H, S, HEAD_PAD), F32)] * 2,
        scratch_shapes=[pltpu.VMEM((t, HEAD_PAD), F32), pltpu.VMEM((t, HEAD_PAD), F32)],
        compiler_params=_params(3))(*args)


def _compact(o_ref):
    return jnp.concatenate([o_ref[2 * p] + pltpu.roll(o_ref[2 * p + 1], 64, 1) for p in range(4)], axis=1)


def merge_fwd(ob, oc, proj, ya, gate_b, wbr, w_out, x, post_g):
    S = x.shape[0]
    ts = ROW_TILE

    def body(ob_ref, oc_ref, zb_ref, zc_ref, ya_ref, gl_ref, gb_ref, wbr_ref, wo_ref, x_ref, pg_ref,
             xo_ref, yb_ref, yc_ref, mg_ref, out_ref):
        zb, zc = zb_ref[...], zc_ref[...]
        yb = (_compact(ob_ref) * (zb * _sigmoid(zb))).astype(BF16)
        yc = (_compact(oc_ref) * (zc * _sigmoid(zc))).astype(BF16)
        yb_ref[...] = yb
        yc_ref[...] = yc
        merged = jnp.zeros((ts, D_MODEL), F32)
        for n, y in enumerate((ya_ref[...], yb, yc)):
            cols = slice(n * D_MODEL, (n + 1) * D_MODEL)
            gate = _sigmoid(gl_ref[:, cols] + gb_ref[:, cols])
            merged = merged + gate * _nn(y, wbr_ref[n])
        mb = merged.astype(BF16)
        mg_ref[...] = mb
        out = _nn(mb, wo_ref[...])
        out_ref[...] = out
        normed, _ = _rms(out, pg_ref[...])
        xo_ref[...] = x_ref[...] + normed

    row = lambda w: pl.BlockSpec((ts, w), lambda i: (i, 0))
    col = lambda w, c: pl.BlockSpec((ts, w), lambda i: (i, c))
    full = lambda shape: pl.BlockSpec(shape, lambda i: (0,) * len(shape))
    return _call(
        body, name="merge_fwd", grid=(S // ts,),
        in_specs=[_hspec(ts), _hspec(ts), col(512, P_ZB // 512), col(512, P_ZC // 512), row(512),
                  col(3072, P_G // 3072), full((1, 3072)), full((3, BR_WIDTH, D_MODEL)),
                  full((D_MODEL, D_MODEL)), row(D_MODEL), full((1, D_MODEL))],
        out_specs=[row(D_MODEL), row(512), row(512), row(D_MODEL), row(D_MODEL)],
        out_shape=[jax.ShapeDtypeStruct((S, D_MODEL), F32), jax.ShapeDtypeStruct((S, 512), BF16),
                   jax.ShapeDtypeStruct((S, 512), BF16), jax.ShapeDtypeStruct((S, D_MODEL), BF16),
                   jax.ShapeDtypeStruct((S, D_MODEL), F32)],
        compiler_params=_params(1))(ob, oc, proj, proj, ya, proj, gate_b, wbr, w_out, x, post_g)


def post_bwd(g, out, post_g, w_out):
    S = g.shape[0]
    ts = ROW_TILE

    def body(g_ref, out_ref, pg_ref, wo_ref, do_ref, dm_ref, gp_ref):
        ov = out_ref[...]
        _, r = _rms(ov, pg_ref[...])
        dout, gp = _rms_bwd(ov, pg_ref[...], r, g_ref[...])
        db = dout.astype(BF16)
        do_ref[...] = db
        dm_ref[...] = _nt(db, wo_ref[...])
        _accumulate(gp_ref, gp, pl.program_id(0) == 0)

    row = lambda: pl.BlockSpec((ts, D_MODEL), lambda i: (i, 0))
    full = lambda shape: pl.BlockSpec(shape, lambda i: (0,) * len(shape))
    return _call(
        body, name="post_bwd", grid=(S // ts,),
        in_specs=[row(), row(), full((1, D_MODEL)), full((D_MODEL, D_MODEL))],
        out_specs=[row(), row(), full((1, D_MODEL))],
        out_shape=[jax.ShapeDtypeStruct((S, D_MODEL), BF16), jax.ShapeDtypeStruct((S, D_MODEL), F32),
                   jax.ShapeDtypeStruct((1, D_MODEL), F32)],
        compiler_params=_params(1))(g, out, post_g, w_out)


def gate_bwd(dmerged, proj, gate_b, ya, yb, yc, wbr):
    S = dmerged.shape[0]
    ts = ROW_TILE

    def body(dm_ref, gl_ref, gb_ref, ya_ref, yb_ref, yc_ref, wbr_ref,
             dp_ref, dba_ref, dbb_ref, dbc_ref, dy_ref, ggb_ref):
        dm = dm_ref[...]
        ggb = []
        for n, (y_ref, dbr_ref) in enumerate(((ya_ref, dba_ref), (yb_ref, dbb_ref), (yc_ref, dbc_ref))):
            cols = slice(n * D_MODEL, (n + 1) * D_MODEL)
            br = _nn(y_ref[...], wbr_ref[n])
            sg = _sigmoid(gl_ref[:, cols] + gb_ref[:, cols])
            dgl = dm * br * (sg * (1.0 - sg))
            dp_ref[:, cols] = dgl
            ggb.append(jnp.sum(dgl, axis=0, keepdims=True))
            dbr = (dm * sg).astype(BF16)
            dbr_ref[...] = dbr
            dy_ref[n] = _nt(dbr, wbr_ref[n])
        _accumulate(ggb_ref, jnp.concatenate(ggb, axis=1), pl.program_id(0) == 0)

    row = lambda w: pl.BlockSpec((ts, w), lambda i: (i, 0))
    full = lambda shape: pl.BlockSpec(shape, lambda i: (0,) * len(shape))
    wide = jax.ShapeDtypeStruct((S, D_MODEL), BF16)
    return _call(
        body, name="gate_bwd", grid=(S // ts,),
        in_specs=[row(D_MODEL), pl.BlockSpec((ts, 3072), lambda i: (i, P_G // 3072)), full((1, 3072)),
                  row(512), row(512), row(512), full((3, BR_WIDTH, D_MODEL))],
        out_specs=[pl.BlockSpec((ts, 3072), lambda i: (i, P_G // 3072)), row(D_MODEL), row(D_MODEL), row(D_MODEL),
                   pl.BlockSpec((3, ts, 512), lambda i: (0, i, 0)), full((1, 3072))],
        out_shape=[jax.ShapeDtypeStruct((S, P_W), F32), wide, wide, wide,
                   jax.ShapeDtypeStruct((3, S, 512), F32), jax.ShapeDtypeStruct((1, 3072), F32)],
        compiler_params=_params(1))(dmerged, proj, gate_b, ya, yb, yc, wbr)


def ungate_bwd(dproj, dy, ob, oc, proj):
    S = proj.shape[0]
    ts = ROW_TILE

    def body(dp_in, dyb_ref, dyc_ref, ob_ref, oc_ref, zb_ref, zc_ref, dp_ref, dob_ref, doc_ref, dlb_ref, dlc_ref):
        del dp_in
        lane = lax.broadcasted_iota(jnp.int32, (ts, 128), 1)
        for n, (dy_ref, o_ref, z_ref, do_ref, dl_ref) in enumerate(
                ((dyb_ref, ob_ref, zb_ref, dob_ref, dlb_ref), (dyc_ref, oc_ref, zc_ref, doc_ref, dlc_ref))):
            zz = z_ref[...]
            dyv = dy_ref[0]
            sg = _sigmoid(zz)
            dp_ref[:, n * 512:(n + 1) * 512] = dyv * _compact(o_ref) * (sg * (1.0 + zz * (1.0 - sg)))
            do_c = dyv * (zz * sg)
            for p in range(4):
                piece = do_c[:, p * 128:(p + 1) * 128]
                for h, d in ((2 * p, jnp.where(lane < 64, piece, 0.0)),
                             (2 * p + 1, jnp.where(lane < 64, pltpu.roll(piece, 64, 1), 0.0))):
                    do_ref[h] = d.astype(BF16)
                    dl_ref[h] = jnp.sum(d * o_ref[h], axis=-1, keepdims=True)

    col = lambda c: pl.BlockSpec((ts, 512), lambda i: (i, c))
    dysp = lambda n: pl.BlockSpec((1, ts, 512), lambda i: (n, i, 0))
    stat = pl.BlockSpec((N_HEADS, ts, 1), lambda i: (0, i, 0))
    hshape = jax.ShapeDtypeStruct((N_HEADS, S, HEAD_PAD), BF16)
    sshape = jax.ShapeDtypeStruct((N_HEADS, S, 1), F32)
    return _call(
        body, name="ungate_bwd", grid=(S // ts,),
        in_specs=[pl.BlockSpec(memory_space=pl.ANY), dysp(1), dysp(2), _hspec(ts), _hspec(ts),
                  col(P_ZB // 512), col(P_ZC // 512)],
        out_specs=[pl.BlockSpec((ts, 1024), lambda i: (i, P_ZB // 1024)), _hspec(ts), _hspec(ts), stat, stat],
        out_shape=[jax.ShapeDtypeStruct(dproj.shape, F32), hshape, hshape, sshape, sshape],
        input_output_aliases={0: 0},
        compiler_params=_params(1))(dproj, dy, dy, ob, oc, proj, proj)


def loss_head(y, target):
    S, D = y.shape
    ts = ROW_TILE

    def body(y_ref, t_ref, dy_ref, sq_ref):
        d = y_ref[...] - t_ref[...]
        dy_ref[...] = d * (1.0 / D)
        _accumulate(sq_ref, jnp.sum(d * d, axis=0, keepdims=True), pl.program_id(0) == 0)

    row = pl.BlockSpec((ts, D), lambda i: (i, 0))
    return _call(
        body, name="loss_head", grid=(S // ts,), in_specs=[row, row],
        out_specs=[row, pl.BlockSpec((1, D), lambda i: (0, 0))],
        out_shape=[jax.ShapeDtypeStruct((S, D), F32), jax.ShapeDtypeStruct((1, D), F32)],
        compiler_params=_params(1))(y, target)


def adamw(w, g, m, v):
    shape = w.shape
    cols = shape[-1]
    rows = int(np.prod(shape[:-1]))
    tr = rows
    for cand in (512, 256, 128, 64, 32, 16, 8):
        if rows % cand == 0 and cand * cols * 4 <= 1024 * 1024:
            tr = cand
            break
    c1 = 1.0 - ADAM_B1 ** ADAM_STEP
    c2 = 1.0 - ADAM_B2 ** ADAM_STEP

    def body(w_ref, g_ref, m_ref, v_ref, d_ref, mo_ref, vo_ref):
        gv = g_ref[...]
        mn = ADAM_B1 * m_ref[...] + (1.0 - ADAM_B1) * gv
        vn = ADAM_B2 * v_ref[...] + (1.0 - ADAM_B2) * (gv * gv)
        mo_ref[...] = mn
        vo_ref[...] = vn
        d_ref[...] = -ADAM_LR * ((mn / c1) / (jnp.sqrt(vn / c2) + ADAM_EPS) + ADAM_WD * w_ref[...])

    blk = pl.BlockSpec((tr, cols), lambda i: (i, 0))
    sds = jax.ShapeDtypeStruct((rows, cols), F32)
    outs = _call(
        body, name="adamw", grid=(rows // tr,), in_specs=[blk] * 4, out_specs=[blk] * 3,
        out_shape=[sds] * 3, compiler_params=_params(1))(
            *[a.reshape(rows, cols) for a in (w, g, m, v)])
    return [o.reshape(shape) for o in outs]


def add_rows(parts):
    n, rows, cols = parts.shape
    tr = ADD_ROWS_TILE
    assert rows % tr == 0

    def body(p_ref, o_ref):
        acc = p_ref[0]
        for s in range(1, n):
            acc = acc + p_ref[s]
        o_ref[...] = acc

    return _call(
        body, name="add_rows", grid=(rows // tr,),
        in_specs=[pl.BlockSpec((n, tr, cols), lambda i: (0, i, 0))],
        out_specs=pl.BlockSpec((tr, cols), lambda i: (i, 0)),
        out_shape=jax.ShapeDtypeStruct((rows, cols), F32),
        compiler_params=_params(1))(parts)


ANY = pl.BlockSpec(memory_space=pl.ANY)


def _other_chips(x, y):
    return [(1 - x, y), (x, 1 - y), (1 - x, 1 - y)]


def chip_exchange(src, scatter, name):
    rows, cols = src.shape[-2:]

    def body(src_ref, out_ref, send_sems, recv_sems, local_sem):
        x, y, c = lax.axis_index("x"), lax.axis_index("y"), lax.axis_index("c")
        me = 2 * x + y
        local_src = src_ref.at[me] if scatter else src_ref
        mine = pltpu.make_async_copy(local_src, out_ref.at[me], local_sem)
        mine.start()
        copies = []
        for j, (px, py) in enumerate(_other_chips(x, y)):
            cp = pltpu.make_async_remote_copy(
                src_ref=src_ref.at[2 * px + py] if scatter else src_ref,
                dst_ref=out_ref.at[me], send_sem=send_sems.at[j], recv_sem=recv_sems.at[j],
                device_id=(px, py, c), device_id_type=MESH)
            cp.start()
            copies.append(cp)
        for j, (px, py) in enumerate(_other_chips(x, y)):
            pltpu.make_async_remote_copy(
                src_ref=local_src, dst_ref=out_ref.at[2 * px + py], send_sem=send_sems.at[j],
                recv_sem=recv_sems.at[j], device_id=(px, py, c), device_id_type=MESH).wait_recv()
        for cp in copies:
            cp.wait_send()
        mine.wait()

    return _call(
        body, name=name, in_specs=[ANY], out_specs=ANY,
        out_shape=jax.ShapeDtypeStruct((4, rows, cols), src.dtype),
        scratch_shapes=[pltpu.SemaphoreType.DMA((3,)), pltpu.SemaphoreType.DMA((3,)), pltpu.SemaphoreType.DMA])(src)


def sibling_exchange(src):
    def body(src_ref, out_ref, send_sem, recv_sem):
        x, y, c = lax.axis_index("x"), lax.axis_index("y"), lax.axis_index("c")
        cp = pltpu.make_async_remote_copy(src_ref=src_ref, dst_ref=out_ref, send_sem=send_sem, recv_sem=recv_sem,
                                          device_id=(x, y, 1 - c), device_id_type=MESH)
        cp.start()
        cp.wait()

    return _call(
        body, name="sibling_exchange", in_specs=[ANY], out_specs=ANY,
        out_shape=jax.ShapeDtypeStruct(src.shape, src.dtype),
        scratch_shapes=[pltpu.SemaphoreType.DMA, pltpu.SemaphoreType.DMA])(src)


def _perm_in(w):
    z = lambda n: jnp.zeros((w.shape[0], n), w.dtype)
    return jnp.concatenate([w[:, 2464:4000], w[:, 1536:1920], z(64), w[:, 1920:1952], z(32),
                            w[:, 1952:2464], w[:, 4000:4512], w[:, 4512:7584], w[:, 0:1536]], axis=1)


def _unperm_in(p):
    return jnp.concatenate([p[:, 6144:7680], p[:, 1536:1920], p[:, 1984:2016], p[:, 2048:2560],
                            p[:, 0:1536], p[:, 2560:3072], p[:, 3072:6144]], axis=1)


def _pad_heads(w, per_head, lo, hi):
    r = w.shape[0]
    wh = w.reshape(r, N_HEADS, per_head)[:, :, lo:hi]
    return jnp.pad(wh, ((0, 0), (0, 0), (0, HEAD_PAD - (hi - lo)))).reshape(r, N_HEADS * HEAD_PAD)


def _rope_table(S):
    half = MLA_ROPE // 2
    inv = ROPE_BASE ** (-jnp.arange(half, dtype=F32) / half)
    ang = jnp.arange(S).astype(F32)[:, None] * inv[None, :]
    cos, sin = jnp.cos(ang), jnp.sin(ang)
    z = lambda n: jnp.zeros((S, n), F32)
    c = jnp.concatenate([jnp.ones((S, MLA_NOPE), F32), cos, cos, z(32)], axis=1)
    a = jnp.concatenate([z(MLA_NOPE), -sin, z(48)], axis=1)
    b = jnp.concatenate([z(MLA_NOPE + half), sin, z(32)], axis=1)
    return jnp.stack([c, a, b])


def _band_index():
    t = CA_TQ
    d = np.arange(-(t - 1), t)
    return np.stack([np.clip((2 - kj) * t + d, -REL_CLIP, REL_CLIP) + REL_CLIP for kj in range(3)])


def _bias_tiles(table):
    t = CA_TQ
    vals = table[:, _band_index()]
    rowsrep = jnp.broadcast_to(vals[:, :, None, :], (N_HEADS, 3, t, 2 * t - 1)).reshape(N_HEADS, 3, -1)
    skew = jnp.pad(rowsrep, ((0, 0), (0, 0), (0, t))).reshape(N_HEADS, 3, t, 2 * t)[..., :t]
    return skew[..., ::-1]


def _bias_tiles_grad(dtiles):
    t = CA_TQ
    flipped = dtiles[..., ::-1]
    flat = jnp.pad(flipped, ((0, 0), (0, 0), (0, 0), (0, t))).reshape(N_HEADS, 3, -1)[..., :t * (2 * t - 1)]
    diag = jnp.sum(flat.reshape(N_HEADS, 3, t, 2 * t - 1), axis=2)
    onehot = (_band_index()[:, :, None] == np.arange(2 * REL_CLIP + 1)[None, None, :]).astype(np.float32)
    return jnp.einsum('hkd,kdt->ht', diag, jnp.asarray(onehot), precision=lax.Precision.HIGHEST)


def _layer_consts(lw):
    tri = np.tril(np.ones((SGU_BLOCK, SGU_BLOCK), np.float32))
    ws = (lw['sgu_w'] * tri).astype(BF16)
    bias = _bias_tiles(lw['ca_rel_bias'])
    return dict(
        ws=ws, ws_t=jnp.swapaxes(ws, 1, 2), sgu_bias=jnp.repeat(lw['sgu_b'].T, CA_HEAD_DIM, axis=1),
        bias=bias, bias_t=jnp.swapaxes(bias, 2, 3),
        wq=_pad_heads(lw['mla_w_uq'], MLA_QK, 0, MLA_QK),
        wk=_pad_heads(lw['mla_w_ukv'], MLA_NOPE + MLA_V, 0, MLA_NOPE),
        wv=_pad_heads(lw['mla_w_ukv'], MLA_NOPE + MLA_V, MLA_NOPE, MLA_NOPE + MLA_V),
        w_in=_perm_in(lw['w_in']), gate_b=lw['gate_b'].reshape(1, 3 * D_MODEL),
        pre_g=lw['pre_g'][None], post_g=lw['post_g'][None], ln_g=lw['sgu_ln_g'][None], ln_b=lw['sgu_ln_b'][None],
        qg=lw['mla_q_norm_g'][None], kvg=lw['mla_kv_norm_g'][None])


MLA_SCALE = MLA_QK ** -0.5
CA_SCALE = CA_HEAD_DIM ** -0.5


def _layer_fwd(x, lw, k, tab):
    proj, xn = norm_matmul(x, k['pre_g'], k['w_in'])
    ya = sgu_fwd(proj, k['ln_g'], k['ln_b'], k['ws'], k['sgu_bias'])
    qb, kb, vb, qc, kc, vc, cq, ckv = prep_fwd(proj, tab, k['qg'], k['kvg'], k['wq'], k['wk'], k['wv'])
    ob, lse_b = flash_fwd(qb, kb, vb, None, False, MLA_SCALE)
    oc, lse_c = flash_fwd(qc, kc, vc, k['bias'], True, CA_SCALE)
    x_new, yb, yc, merged, out = merge_fwd(ob, oc, proj, ya, k['gate_b'], lw['w_branch'], lw['w_out'], x,
                                           k['post_g'])
    saved = dict(x=x, proj=proj, xn=xn, ya=ya, yb=yb, yc=yc, qb=qb, kb=kb, vb=vb, qc=qc, kc=kc, vc=vc,
                 cq=cq, ckv=ckv, ob=ob, oc=oc, lse_b=lse_b, lse_c=lse_c, merged=merged, out=out)
    return x_new, saved


def _layer_bwd(g, s, lw, k, tab):
    S = g.shape[0]
    H = N_HEADS
    dout, dmerged, g_post = post_bwd(g, s['out'], k['post_g'], lw['w_out'])
    g_w_out = matmul_tn(s['merged'], dout, 512)
    dproj, dba, dbb, dbc, dy, g_gate_b = gate_bwd(dmerged, s['proj'], k['gate_b'], s['ya'], s['yb'], s['yc'],
                                                  lw['w_branch'])
    g_w_branch = jnp.stack([matmul_tn(y, d, 512) for y, d in ((s['ya'], dba), (s['yb'], dbb), (s['yc'], dbc))])
    dproj, dob, doc, dl_b, dl_c = ungate_bwd(dproj, dy, s['ob'], s['oc'], s['proj'])
    row = lambda a: a.reshape(H, 1, S)
    dqb = flash_bwd_q(s['qb'], s['kb'], s['vb'], dob, s['lse_b'], dl_b, None, False, MLA_SCALE)[0]
    dkb, dvb = flash_bwd_kv(s['qb'], s['kb'], s['vb'], dob, row(s['lse_b']), row(dl_b), None, False, MLA_SCALE)
    dqc, dbias = flash_bwd_q(s['qc'], s['kc'], s['vc'], doc, s['lse_c'], dl_c, k['bias'], True, CA_SCALE)
    dkc, dvc = flash_bwd_kv(s['qc'], s['kc'], s['vc'], doc, row(s['lse_c']), row(dl_c), k['bias_t'], True, CA_SCALE)
    dproj, dqf, dkf, dvf, g_qg, g_kvg = prep_bwd(dproj, dqb, dkb, dvb, dqc, dkc, dvc, s['proj'], tab,
                                                 k['qg'], k['kvg'], k['wq'], k['wk'], k['wv'])
    g_wq = matmul_tn(s['cq'], dqf, 512).reshape(MLA_Q_RANK, H, HEAD_PAD)[:, :, :MLA_QK]
    g_wk = matmul_tn(s['ckv'], dkf, 512).reshape(MLA_KV_RANK, H, HEAD_PAD)[:, :, :MLA_NOPE]
    g_wv = matmul_tn(s['ckv'], dvf, 512).reshape(MLA_KV_RANK, H, HEAD_PAD)[:, :, :MLA_V]
    dproj, g_ln_g, g_ln_b, g_ws, g_sgu_bias = sgu_bwd(dproj, dy, s['proj'], k['ln_g'], k['ln_b'], k['ws'],
                                                      k['ws_t'], k['sgu_bias'])
    g_w_in = matmul_tn(s['xn'], dproj, MM_TN)
    dx, g_pre = proj_bwd_x(dproj, k['w_in'], s['x'], k['pre_g'], g)
    tri = np.tril(np.ones((SGU_BLOCK, SGU_BLOCK), np.float32))
    grads = dict(
        w_in=_unperm_in(g_w_in), pre_g=g_pre[0], post_g=g_post[0], sgu_ln_g=g_ln_g[0], sgu_ln_b=g_ln_b[0],
        sgu_w=g_ws * tri, sgu_b=jnp.sum(g_sgu_bias.reshape(SGU_BLOCK, 8, CA_HEAD_DIM), axis=2).T,
        mla_q_norm_g=g_qg[0], mla_kv_norm_g=g_kvg[0],
        mla_w_uq=g_wq.reshape(MLA_Q_RANK, H * MLA_QK),
        mla_w_ukv=jnp.concatenate([g_wk, g_wv], axis=2).reshape(MLA_KV_RANK, H * (MLA_NOPE + MLA_V)),
        ca_rel_bias=_bias_tiles_grad(dbias), w_branch=g_w_branch,
        gate_b=g_gate_b.reshape(N_BRANCH, D_MODEL), w_out=g_w_out)
    return dx, grads


def local_step(x, target, weights):
    S = x.shape[0]
    depth = weights['w_in'].shape[0]
    tab = _rope_table(S)
    layer_w = [{n: weights[n][l] for n in WEIGHTS} for l in range(depth)]
    consts = [_layer_consts(lw) for lw in layer_w]
    saved = []
    for l in range(depth):
        x, s = _layer_fwd(x, layer_w[l], consts[l], tab)
        saved.append(s)
    g, sq = loss_head(x, target)
    layer_grads = [None] * depth
    for l in reversed(range(depth)):
        g, layer_grads[l] = _layer_bwd(g, saved[l], layer_w[l], consts[l], tab)
    grads = {n: jnp.stack([lg[n] for lg in layer_grads]) for n in WEIGHTS}
    return sq, g, grads


def _split4(a, axis):
    shape = a.shape
    a = a.reshape(shape[:axis] + (4, shape[axis] // 4) + shape[axis + 1:])
    return jnp.moveaxis(a, axis, 0).reshape(4, -1)


def _join4(a, shard_shape, axis):
    a = jnp.moveaxis(a.reshape((4,) + tuple(shard_shape)), 0, axis)
    shape = a.shape
    return a.reshape(shape[:axis] + (4 * shape[axis + 1],) + shape[axis + 2:])


SHARD_AXIS = {'w_in': 2, 'mla_w_uq': 2, 'mla_w_ukv': 2, 'w_branch': 3, 'gate_b': 2, 'w_out': 1}


def gather_weights(shards):
    flat, sizes = [], []
    for n in SHARDED:
        a = shards[n]
        a = lax.bitcast_convert_type(a, BF16) if n == 'gate_b' else a.astype(BF16)
        flat.append(a.reshape(-1))
        sizes.append(a.size)
    packed = jnp.concatenate(flat).reshape(-1, 128)
    got = chip_exchange(packed, False, "gather_weights").reshape(4, -1)
    full, off = {}, 0
    for n, size in zip(SHARDED, sizes):
        part = got[:, off:off + size]
        off += size
        if n == 'gate_b':
            part = lax.bitcast_convert_type(part.reshape(4, -1, 2), F32)
        full[n] = _join4(part, shards[n].shape, SHARD_AXIS[n])
    return full


def reduce_grads(grads, small_shapes):
    sharded = jnp.concatenate([_split4(grads[n], SHARD_AXIS[n]) for n in SHARDED], axis=1)
    small = jnp.concatenate([grads[n].reshape(-1) for n in SMALL])
    n_sharded = sharded.shape[1]
    unit = ADD_ROWS_TILE * 128
    quarter = -(-small.size // 4)
    quarter += (-(n_sharded + quarter)) % unit
    small = jnp.pad(small, (0, 4 * quarter - small.size)).reshape(4, quarter)
    packed = jnp.concatenate([sharded, small], axis=1).reshape(4, -1, 128)
    parts = chip_exchange(packed, True, "scatter_grads")
    mine = add_rows(parts)
    theirs = sibling_exchange(mine)
    c = lax.axis_index("c")
    both = jnp.stack([jnp.where(c == 0, mine, theirs), jnp.where(c == 0, theirs, mine)])
    total = add_rows(both).reshape(-1)
    out, off = {}, 0
    for n in SHARDED:
        shape = grads[n].shape
        shard_shape = shape[:SHARD_AXIS[n]] + (shape[SHARD_AXIS[n]] // 4,) + shape[SHARD_AXIS[n] + 1:]
        size = int(np.prod(shard_shape))
        out[n] = total[off:off + size].reshape(shard_shape)
        off += size
    small_all = chip_exchange(total[n_sharded:].reshape(-1, 128), False, "gather_small").reshape(-1)
    off = 0
    for n in SMALL:
        size = int(np.prod(small_shapes[n]))
        out[n] = small_all[off:off + size].reshape(small_shapes[n])
        off += size
    return out


def kernel(x, w_in, pre_g, post_g, sgu_ln_g, sgu_ln_b, sgu_w, sgu_b, mla_q_norm_g, mla_kv_norm_g, mla_w_uq, mla_w_ukv, ca_rel_bias, w_branch, gate_b, w_out, loss_target, m_w_in, m_pre_g, m_post_g, m_sgu_ln_g, m_sgu_ln_b, m_sgu_w, m_sgu_b, m_mla_q_norm_g, m_mla_kv_norm_g, m_mla_w_uq, m_mla_w_ukv, m_ca_rel_bias, m_w_branch, m_gate_b, m_w_out, v_w_in, v_pre_g, v_post_g, v_sgu_ln_g, v_sgu_ln_b, v_sgu_w, v_sgu_b, v_mla_q_norm_g, v_mla_kv_norm_g, v_mla_w_uq, v_mla_w_ukv, v_ca_rel_bias, v_w_branch, v_gate_b, v_w_out):
    w = dict(w_in=w_in, pre_g=pre_g, post_g=post_g, sgu_ln_g=sgu_ln_g, sgu_ln_b=sgu_ln_b, sgu_w=sgu_w, sgu_b=sgu_b,
             mla_q_norm_g=mla_q_norm_g, mla_kv_norm_g=mla_kv_norm_g, mla_w_uq=mla_w_uq, mla_w_ukv=mla_w_ukv,
             ca_rel_bias=ca_rel_bias, w_branch=w_branch, gate_b=gate_b, w_out=w_out)
    m = dict(w_in=m_w_in, pre_g=m_pre_g, post_g=m_post_g, sgu_ln_g=m_sgu_ln_g, sgu_ln_b=m_sgu_ln_b, sgu_w=m_sgu_w,
             sgu_b=m_sgu_b, mla_q_norm_g=m_mla_q_norm_g, mla_kv_norm_g=m_mla_kv_norm_g, mla_w_uq=m_mla_w_uq,
             mla_w_ukv=m_mla_w_ukv, ca_rel_bias=m_ca_rel_bias, w_branch=m_w_branch, gate_b=m_gate_b, w_out=m_w_out)
    v = dict(w_in=v_w_in, pre_g=v_pre_g, post_g=v_post_g, sgu_ln_g=v_sgu_ln_g, sgu_ln_b=v_sgu_ln_b, sgu_w=v_sgu_w,
             sgu_b=v_sgu_b, mla_q_norm_g=v_mla_q_norm_g, mla_kv_norm_g=v_mla_kv_norm_g, mla_w_uq=v_mla_w_uq,
             mla_w_ukv=v_mla_w_ukv, ca_rel_bias=v_ca_rel_bias, w_branch=v_w_branch, gate_b=v_gate_b, w_out=v_w_out)
    full = gather_weights({n: w[n] for n in SHARDED})
    full.update({n: w[n] for n in SMALL})
    sq, grad_x, grads = local_step(x[0], loss_target[0], full)
    loss = lax.psum(0.5 * jnp.sum(sq) / D_MODEL, ("x", "y", "c"))
    total = reduce_grads(grads, {n: w[n].shape for n in SMALL})
    updates = {n: adamw(w[n], total[n], m[n], v[n]) for n in WEIGHTS}
    return (loss, grad_x[None], *[total[n] for n in WEIGHTS], *[updates[n][0] for n in WEIGHTS],
            *[updates[n][1] for n in WEIGHTS], *[updates[n][2] for n in WEIGHTS])
```

```python
import numpy as np
import jax
import jax.numpy as jnp
from jax import lax
from jax.experimental import pallas as pl
from jax.experimental.pallas import tpu as pltpu

F32 = jnp.float32
BF16 = jnp.bfloat16
MESH = pl.DeviceIdType.MESH

EPS = 1e-6
NEG_INF = -1e30
D_MODEL = 1024
BR_WIDTH = 512
N_BRANCH = 3
N_HEADS = 8
HEAD_PAD = 128
CHUNK_SHIFT = 6
SGU_BLOCK = 128
MLA_NOPE, MLA_ROPE, MLA_V = 64, 32, 64
MLA_QK = MLA_NOPE + MLA_ROPE
MLA_Q_RANK, MLA_KV_RANK = 256, 128
CA_HEAD_DIM = 64
REL_CLIP = 128
ROPE_BASE = 10000.0
D_IN = 7584

ADAM_LR, ADAM_B1, ADAM_B2, ADAM_EPS, ADAM_WD, ADAM_STEP = 0.001, 0.9, 0.999, 1e-08, 0.01, 10

P_QC, P_KC, P_VC, P_QD, P_KVD, P_KR, P_ZB, P_ZC, P_G, P_U, P_V, P_ZA, P_W = (
    0, 512, 1024, 1536, 1792, 1920, 2048, 2560, 3072, 6144, 6656, 7168, 7680)
NAT_SEGS = [(0, 1536, P_U), (1536, 1920, P_QD), (1920, 1952, P_KR + MLA_NOPE), (1952, 2464, P_ZB),
            (2464, 4000, P_QC), (4000, 4512, P_ZC), (4512, 7584, P_G)]
SHARD_COLS = D_IN // 4

VMEM_LIMIT = 48 * 1024 * 1024
ATT_T = 512
ROW_TILE = 256
MM_TM = 512
MM_TN = 768
LOG2E = 1.4426950408889634
MLA_SCALE = MLA_QK ** -0.5
CA_SCALE = CA_HEAD_DIM ** -0.5

WEIGHTS = ['w_in', 'pre_g', 'post_g', 'sgu_ln_g', 'sgu_ln_b', 'sgu_w', 'sgu_b', 'mla_q_norm_g',
           'mla_kv_norm_g', 'mla_w_uq', 'mla_w_ukv', 'ca_rel_bias', 'w_branch', 'gate_b', 'w_out']
SHARDED = ['w_in', 'mla_w_uq', 'mla_w_ukv', 'w_branch', 'gate_b', 'w_out']
SMALL = ['pre_g', 'post_g', 'sgu_ln_g', 'sgu_ln_b', 'sgu_w', 'sgu_b', 'mla_q_norm_g',
         'mla_kv_norm_g', 'ca_rel_bias']
SHARD_AXIS = {'w_in': 1, 'mla_w_uq': 1, 'mla_w_ukv': 1, 'w_branch': 2, 'gate_b': 1, 'w_out': 0}


def _call(body, **kw):
    return pl.pallas_call(body, **kw)


def _params(n_axes):
    return pltpu.CompilerParams(dimension_semantics=("arbitrary",) * n_axes,
                                vmem_limit_bytes=VMEM_LIMIT)


def _nt(a, b):
    return lax.dot_general(a, b, (((1,), (1,)), ((), ())), preferred_element_type=F32)


def _nn(a, b):
    return jnp.dot(a, b, preferred_element_type=F32)


def _tn(a, b):
    return lax.dot_general(a, b, (((0,), (0,)), ((), ())), preferred_element_type=F32)


def _rms(xv, g):
    r = lax.rsqrt(jnp.mean(xv * xv, axis=-1, keepdims=True) + EPS)
    return xv * r * g, r


def _rms_bwd(xv, g, r, dy):
    gy = dy * g
    dx = r * gy - xv * (r * r * r) * jnp.mean(xv * gy, axis=-1, keepdims=True)
    dg = jnp.sum(dy * (xv * r), axis=0, keepdims=True)
    return dx, dg


def _sigmoid(z):
    return 1.0 / (1.0 + jnp.exp(-z))


def _rope(xv, c, a, b):
    return xv * c + pltpu.roll(xv, 112, 1) * a + pltpu.roll(xv, 16, 1) * b


def _accumulate(ref, val, first):
    @pl.when(first)
    def _():
        ref[...] = val

    @pl.when(jnp.logical_not(first))
    def _():
        ref[...] += val


def norm_matmul(x, g, w):
    S, D = x.shape
    N = w.shape[1]
    tm, tn = min(S, MM_TM), MM_TN

    def body(x_ref, g_ref, w_ref, o_ref, xn_ref):
        @pl.when(pl.program_id(1) == 0)
        def _():
            y, _ = _rms(x_ref[...], g_ref[...])
            xn_ref[...] = y.astype(BF16)

        o_ref[...] = _nn(xn_ref[...], w_ref[...])

    return _call(
        body, name="norm_matmul", grid=(S // tm, N // tn),
        in_specs=[pl.BlockSpec((tm, D), lambda i, j: (i, 0)),
                  pl.BlockSpec((1, D), lambda i, j: (0, 0)),
                  pl.BlockSpec((D, tn), lambda i, j: (0, j))],
        out_specs=[pl.BlockSpec((tm, tn), lambda i, j: (i, j)),
                   pl.BlockSpec((tm, D), lambda i, j: (i, 0))],
        out_shape=[jax.ShapeDtypeStruct((S, N), F32), jax.ShapeDtypeStruct((S, D), BF16)],
        compiler_params=_params(2))(x, g, w)


def proj_bwd_x(dproj, w, x, g, resid):
    S, N = dproj.shape
    D = x.shape[1]
    tm, tk = min(S, MM_TM), MM_TN
    nk = N // tk

    def body(dp_ref, w_ref, x_ref, g_ref, r_ref, dx_ref, dg_ref, acc_ref):
        i, k = pl.program_id(0), pl.program_id(1)

        @pl.when(k == 0)
        def _():
            acc_ref[...] = jnp.zeros_like(acc_ref)

        acc_ref[...] += _nt(dp_ref[...].astype(BF16), w_ref[...])

        @pl.when(k == nk - 1)
        def _():
            xv = x_ref[...]
            _, r = _rms(xv, g_ref[...])
            dx, dg = _rms_bwd(xv, g_ref[...], r, acc_ref[...])
            dx_ref[...] = dx + r_ref[...]
            _accumulate(dg_ref, dg, i == 0)

    return _call(
        body, name="proj_bwd_x", grid=(S // tm, nk),
        in_specs=[pl.BlockSpec((tm, tk), lambda i, k: (i, k)),
                  pl.BlockSpec((D, tk), lambda i, k: (0, k)),
                  pl.BlockSpec((tm, D), lambda i, k: (i, 0)),
                  pl.BlockSpec((1, D), lambda i, k: (0, 0)),
                  pl.BlockSpec((tm, D), lambda i, k: (i, 0))],
        out_specs=[pl.BlockSpec((tm, D), lambda i, k: (i, 0)),
                   pl.BlockSpec((1, D), lambda i, k: (0, 0))],
        out_shape=[jax.ShapeDtypeStruct((S, D), F32), jax.ShapeDtypeStruct((1, D), F32)],
        scratch_shapes=[pltpu.VMEM((tm, D), F32)],
        compiler_params=_params(2))(dproj, w, x, g, resid)


def matmul_tn(a, b, tn):
    S, M = a.shape
    N = b.shape[1]
    tk = min(S, 512)

    def body(a_ref, b_ref, o_ref):
        @pl.when(pl.program_id(1) == 0)
        def _():
            o_ref[...] = jnp.zeros_like(o_ref)

        o_ref[...] += _tn(a_ref[...].astype(BF16), b_ref[...].astype(BF16))

    return _call(
        body, name="matmul_tn", grid=(N // tn, S // tk),
        in_specs=[pl.BlockSpec((tk, M), lambda j, k: (k, 0)),
                  pl.BlockSpec((tk, tn), lambda j, k: (k, j))],
        out_specs=pl.BlockSpec((M, tn), lambda j, k: (0, j)),
        out_shape=jax.ShapeDtypeStruct((M, N), F32),
        compiler_params=_params(2))(a, b)


def _sgu_block(vv, g, b, ws_ref, lane):
    mu = jnp.mean(vv, axis=-1, keepdims=True)
    xc = vv - mu
    r = lax.rsqrt(jnp.mean(xc * xc, axis=-1, keepdims=True) + EPS)
    xhat = xc * r
    vln = (xhat * g + b).astype(BF16)
    pieces = []
    for p in range(4):
        vp = vln[:, p * 128:(p + 1) * 128]
        pieces.append(jnp.where(lane < 64, _nn(ws_ref[2 * p], vp), _nn(ws_ref[2 * p + 1], vp)))
    return xhat, r, vln, jnp.concatenate(pieces, axis=1)


def sgu_fwd(proj, ln_g, ln_b, ws, bias_full):
    S = proj.shape[0]
    ts = ROW_TILE

    def body(u_ref, v_ref, z_ref, g_ref, b_ref, ws_ref, bf_ref, y_ref):
        lane = lax.broadcasted_iota(jnp.int32, (SGU_BLOCK, 128), 1)
        for blk in range(ts // SGU_BLOCK):
            rows = slice(blk * SGU_BLOCK, (blk + 1) * SGU_BLOCK)
            _, _, _, mixed = _sgu_block(v_ref[rows, :], g_ref[...], b_ref[...], ws_ref, lane)
            mixed = mixed + bf_ref[...]
            zz = z_ref[rows, :]
            y_ref[rows, :] = (u_ref[rows, :] * mixed * (zz * _sigmoid(zz))).astype(BF16)

    col = lambda c: pl.BlockSpec((ts, BR_WIDTH), lambda i: (i, c))
    full = lambda shape: pl.BlockSpec(shape, lambda i: (0,) * len(shape))
    return _call(
        body, name="sgu_fwd", grid=(S // ts,),
        in_specs=[col(P_U // 512), col(P_V // 512), col(P_ZA // 512),
                  full((1, BR_WIDTH)), full((1, BR_WIDTH)), full((8, 128, 128)), full((128, BR_WIDTH))],
        out_specs=pl.BlockSpec((ts, BR_WIDTH), lambda i: (i, 0)),
        out_shape=jax.ShapeDtypeStruct((S, BR_WIDTH), BF16),
        compiler_params=_params(1))(proj, proj, proj, ln_g, ln_b, ws, bias_full)


def sgu_bwd(dproj, dy, proj, ln_g, ln_b, ws, ws_t, bias_full):
    S = proj.shape[0]
    ts = ROW_TILE

    def body(dp_in, dy_ref, u_ref, v_ref, z_ref, g_ref, b_ref, ws_ref, wst_ref, bf_ref,
             dp_ref, gg_ref, gb_ref, gws_ref, gbf_ref):
        del dp_in
        first = pl.program_id(0) == 0

        @pl.when(first)
        def _():
            gg_ref[...] = jnp.zeros_like(gg_ref)
            gb_ref[...] = jnp.zeros_like(gb_ref)
            gws_ref[...] = jnp.zeros_like(gws_ref)
            gbf_ref[...] = jnp.zeros_like(gbf_ref)

        lane = lax.broadcasted_iota(jnp.int32, (SGU_BLOCK, 128), 1)
        for blk in range(ts // SGU_BLOCK):
            rows = slice(blk * SGU_BLOCK, (blk + 1) * SGU_BLOCK)
            g = g_ref[...]
            xhat, r, vln, mixed = _sgu_block(v_ref[rows, :], g, b_ref[...], ws_ref, lane)
            mixed = mixed + bf_ref[...]
            zz = z_ref[rows, :]
            uu = u_ref[rows, :]
            dyv = dy_ref[0, rows, :]
            sg = _sigmoid(zz)
            sil = zz * sg
            dmixed = dyv * uu * sil
            dp_ref[rows, 0:512] = dyv * mixed * sil
            dp_ref[rows, 1024:1536] = dyv * uu * mixed * (sg * (1.0 + zz * (1.0 - sg)))
            gbf_ref[...] += dmixed
            dmb = dmixed.astype(BF16)
            pieces = []
            for p in range(4):
                dmp = dmb[:, p * 128:(p + 1) * 128]
                vp = vln[:, p * 128:(p + 1) * 128]
                pieces.append(jnp.where(lane < 64, _nn(wst_ref[2 * p], dmp), _nn(wst_ref[2 * p + 1], dmp)))
                zero = jnp.zeros_like(dmp)
                gws_ref[2 * p] += _nt(jnp.where(lane < 64, dmp, zero), vp)
                gws_ref[2 * p + 1] += _nt(jnp.where(lane >= 64, dmp, zero), vp)
            dvln = jnp.concatenate(pieces, axis=1)
            dxh = dvln * g
            dp_ref[rows, 512:1024] = r * (dxh - jnp.mean(dxh, axis=-1, keepdims=True)
                                          - xhat * jnp.mean(dxh * xhat, axis=-1, keepdims=True))
            gg_ref[...] += jnp.sum(dvln * xhat, axis=0, keepdims=True)
            gb_ref[...] += jnp.sum(dvln, axis=0, keepdims=True)

    col = lambda c: pl.BlockSpec((ts, BR_WIDTH), lambda i: (i, c))
    full = lambda shape: pl.BlockSpec(shape, lambda i: (0,) * len(shape))
    return _call(
        body, name="sgu_bwd", grid=(S // ts,),
        in_specs=[pl.BlockSpec(memory_space=pl.ANY),
                  pl.BlockSpec((1, ts, BR_WIDTH), lambda i: (0, i, 0)),
                  col(P_U // 512), col(P_V // 512), col(P_ZA // 512),
                  full((1, BR_WIDTH)), full((1, BR_WIDTH)), full((8, 128, 128)), full((8, 128, 128)),
                  full((128, BR_WIDTH))],
        out_specs=[pl.BlockSpec((ts, 1536), lambda i: (i, P_U // 1536)),
                   full((1, BR_WIDTH)), full((1, BR_WIDTH)), full((8, 128, 128)), full((128, BR_WIDTH))],
        out_shape=[jax.ShapeDtypeStruct(dproj.shape, F32),
                   jax.ShapeDtypeStruct((1, BR_WIDTH), F32), jax.ShapeDtypeStruct((1, BR_WIDTH), F32),
                   jax.ShapeDtypeStruct((8, 128, 128), F32), jax.ShapeDtypeStruct((128, BR_WIDTH), F32)],
        input_output_aliases={0: 0},
        compiler_params=_params(1))(dproj, dy, proj, proj, proj, ln_g, ln_b, ws, ws_t, bias_full)


def _hspec(ts):
    return pl.BlockSpec((N_HEADS, ts, HEAD_PAD), lambda i: (0, i, 0))


def prep_fwd(proj, tab, qg, kvg, wq, wk, wv):
    S = proj.shape[0]
    ts = ROW_TILE

    def body(qc_ref, kc_ref, vc_ref, qd_ref, kvd_ref, kr_ref, tab_ref, qg_ref, kvg_ref,
             wq_ref, wk_ref, wv_ref, qb, kb, vb, qc, kc, vc, cq_o, ckv_o):
        c, a, b = tab_ref[0], tab_ref[1], tab_ref[2]
        cq, _ = _rms(qd_ref[...], qg_ref[...])
        ckv, _ = _rms(kvd_ref[...], kvg_ref[...])
        cqb, ckvb = cq.astype(BF16), ckv.astype(BF16)
        cq_o[...] = cqb
        ckv_o[...] = ckvb
        krr = _rope(kr_ref[...], c, a, b)
        for h in range(N_HEADS):
            cols = slice(h * HEAD_PAD, (h + 1) * HEAD_PAD)
            qb[h] = (_rope(_nn(cqb, wq_ref[:, cols]), c, a, b) * (MLA_SCALE * LOG2E)).astype(BF16)
            kb[h] = (_nn(ckvb, wk_ref[:, cols]) + krr).astype(BF16)
            vb[h] = _nn(ckvb, wv_ref[:, cols]).astype(BF16)
        lane = lax.broadcasted_iota(jnp.int32, (ts, 128), 1)
        for src, dst, factor in ((qc_ref, qc, CA_SCALE * LOG2E), (kc_ref, kc, 1.0), (vc_ref, vc, 1.0)):
            for p in range(4):
                piece = src[:, p * 128:(p + 1) * 128] * factor
                dst[2 * p] = jnp.where(lane < 64, piece, 0.0).astype(BF16)
                dst[2 * p + 1] = jnp.where(lane < 64, pltpu.roll(piece, 64, 1), 0.0).astype(BF16)

    col = lambda w, c: pl.BlockSpec((ts, w), lambda i: (i, c))
    full = lambda shape: pl.BlockSpec(shape, lambda i: (0,) * len(shape))
    hshape = jax.ShapeDtypeStruct((N_HEADS, S, HEAD_PAD), BF16)
    return _call(
        body, name="prep_fwd", grid=(S // ts,),
        in_specs=[col(512, P_QC // 512), col(512, P_KC // 512), col(512, P_VC // 512),
                  col(256, P_QD // 256), col(128, P_KVD // 128), col(128, P_KR // 128),
                  pl.BlockSpec((3, ts, 128), lambda i: (0, i, 0)),
                  full((1, MLA_Q_RANK)), full((1, MLA_KV_RANK)),
                  full((MLA_Q_RANK, 1024)), full((MLA_KV_RANK, 1024)), full((MLA_KV_RANK, 1024))],
        out_specs=[_hspec(ts)] * 6 + [pl.BlockSpec((ts, MLA_Q_RANK), lambda i: (i, 0)),
                                      pl.BlockSpec((ts, MLA_KV_RANK), lambda i: (i, 0))],
        out_shape=[hshape] * 6 + [jax.ShapeDtypeStruct((S, MLA_Q_RANK), BF16),
                                  jax.ShapeDtypeStruct((S, MLA_KV_RANK), BF16)],
        compiler_params=_params(1))(proj, proj, proj, proj, proj, proj, tab, qg, kvg, wq, wk, wv)


def prep_bwd(dproj, dqb, dkb, dvb, dqc, dkc, dvc, proj, tab, qg, kvg, wq, wk, wv):
    S = proj.shape[0]
    ts = ROW_TILE

    def body(dp_in, dqb_r, dkb_r, dvb_r, dqc_r, dkc_r, dvc_r, qd_ref, kvd_ref, tab_ref, qg_ref, kvg_ref,
             wq_ref, wk_ref, wv_ref, dp_ref, dqf, dkf, dvf, gq_ref, gkv_ref):
        del dp_in
        c, a, b = tab_ref[0], -tab_ref[1], -tab_ref[2]
        qd, kvd = qd_ref[...], kvd_ref[...]
        _, rq = _rms(qd, qg_ref[...])
        _, rkv = _rms(kvd, kvg_ref[...])
        dcq = jnp.zeros((ts, MLA_Q_RANK), F32)
        dckv = jnp.zeros((ts, MLA_KV_RANK), F32)
        dksum = jnp.zeros((ts, HEAD_PAD), F32)
        for h in range(N_HEADS):
            cols = slice(h * HEAD_PAD, (h + 1) * HEAD_PAD)
            dqh = _rope(dqb_r[h] * MLA_SCALE, c, a, b).astype(BF16)
            dqf[:, cols] = dqh
            dcq = dcq + _nt(dqh, wq_ref[:, cols])
            dk = dkb_r[h] * (1.0 / LOG2E)
            dksum = dksum + dk
            dkh = dk.astype(BF16)
            dkf[:, cols] = dkh
            dvh = dvb_r[h].astype(BF16)
            dvf[:, cols] = dvh
            dckv = dckv + _nt(dkh, wk_ref[:, cols]) + _nt(dvh, wv_ref[:, cols])
        lane = lax.broadcasted_iota(jnp.int32, (ts, 128), 1)
        rope_lanes = jnp.logical_and(lane >= MLA_NOPE, lane < MLA_QK)
        dp_ref[:, P_KR:P_KR + 128] = jnp.where(rope_lanes, _rope(dksum, c, a, b), 0.0)
        dqd, gq = _rms_bwd(qd, qg_ref[...], rq, dcq)
        dkvd, gkv = _rms_bwd(kvd, kvg_ref[...], rkv, dckv)
        dp_ref[:, P_QD:P_QD + 256] = dqd
        dp_ref[:, P_KVD:P_KVD + 128] = dkvd
        first = pl.program_id(0) == 0
        _accumulate(gq_ref, gq, first)
        _accumulate(gkv_ref, gkv, first)
        for src, base, factor in ((dqc_r, P_QC, CA_SCALE), (dkc_r, P_KC, 1.0 / LOG2E), (dvc_r, P_VC, 1.0)):
            for p in range(4):
                dp_ref[:, base + p * 128:base + (p + 1) * 128] = (
                    src[2 * p] + pltpu.roll(src[2 * p + 1], 64, 1)) * factor

    col = lambda w, c: pl.BlockSpec((ts, w), lambda i: (i, c))
    full = lambda shape: pl.BlockSpec(shape, lambda i: (0,) * len(shape))
    wide = jax.ShapeDtypeStruct((S, 1024), BF16)
    return _call(
        body, name="prep_bwd", grid=(S // ts,),
        in_specs=[pl.BlockSpec(memory_space=pl.ANY)] + [_hspec(ts)] * 6 +
                 [col(256, P_QD // 256), col(128, P_KVD // 128),
                  pl.BlockSpec((3, ts, 128), lambda i: (0, i, 0)),
                  full((1, MLA_Q_RANK)), full((1, MLA_KV_RANK)),
                  full((MLA_Q_RANK, 1024)), full((MLA_KV_RANK, 1024)), full((MLA_KV_RANK, 1024))],
        out_specs=[pl.BlockSpec((ts, 2048), lambda i: (i, 0))] + [pl.BlockSpec((ts, 1024), lambda i: (i, 0))] * 3 +
                  [full((1, MLA_Q_RANK)), full((1, MLA_KV_RANK))],
        out_shape=[jax.ShapeDtypeStruct(dproj.shape, F32), wide, wide, wide,
                   jax.ShapeDtypeStruct((1, MLA_Q_RANK), F32), jax.ShapeDtypeStruct((1, MLA_KV_RANK), F32)],
        input_output_aliases={0: 0},
        compiler_params=_params(1))(dproj, dqb, dkb, dvb, dqc, dkc, dvc, proj, proj, tab, qg, kvg, wq, wk, wv)


def _diag_visible(t):
    r = lax.broadcasted_iota(jnp.int32, (t, t), 0) >> CHUNK_SHIFT
    c = lax.broadcasted_iota(jnp.int32, (t, t), 1) >> CHUNK_SHIFT
    return r <= c


def _pair_tables(nq, kv_major):
    if kv_major:
        pairs = [(kb, qi) for kb in range(nq) for qi in range(kb, nq)]
    else:
        pairs = [(kb, qi) for qi in range(nq) for kb in range(qi + 1)]
    return (jnp.asarray(np.array([p[0] for p in pairs], np.int32)),
            jnp.asarray(np.array([p[1] for p in pairs], np.int32)), len(pairs))


def mla_fwd(q, k, vt):
    H, S, _ = q.shape
    t = ATT_T
    kb_tab, qi_tab, n_pairs = _pair_tables(S // t, False)

    def body(kb_ref, qi_ref, q_ref, k_ref, vt_ref, o_ref, lse_ref, m_s, l_s, acc_s):
        p_id = pl.program_id(1)
        kb, qi = kb_ref[p_id], qi_ref[p_id]

        @pl.when(kb == 0)
        def _():
            m_s[...] = jnp.full_like(m_s, NEG_INF)
            l_s[...] = jnp.zeros_like(l_s)
            acc_s[...] = jnp.zeros_like(acc_s)

        def step(masked):
            st = _nt(k_ref[0], q_ref[0])
            if masked:
                st = jnp.where(_diag_visible(t), st, NEG_INF)
            m_prev = m_s[...]
            m_new = jnp.maximum(m_prev, jnp.max(st, axis=0, keepdims=True))
            alpha = jnp.exp2(m_prev - m_new)
            p = jnp.exp2(st - m_new)
            l_s[...] = alpha * l_s[...] + jnp.sum(p, axis=0, keepdims=True)
            acc_s[...] = alpha * acc_s[...] + _nn(vt_ref[0], p.astype(BF16))
            m_s[...] = m_new

        @pl.when(kb < qi)
        def _():
            step(False)

        @pl.when(kb == qi)
        def _():
            step(True)
            o_ref[0] = (acc_s[...] / l_s[...]).T
            lse_ref[0] = m_s[...] + jnp.log2(l_s[...])

    grid_spec = pltpu.PrefetchScalarGridSpec(
        num_scalar_prefetch=2, grid=(H, n_pairs),
        in_specs=[pl.BlockSpec((1, t, HEAD_PAD), lambda h, p, kb, qi: (h, qi[p], 0)),
                  pl.BlockSpec((1, t, HEAD_PAD), lambda h, p, kb, qi: (h, kb[p], 0)),
                  pl.BlockSpec((1, HEAD_PAD, t), lambda h, p, kb, qi: (h, 0, kb[p]))],
        out_specs=[pl.BlockSpec((1, t, HEAD_PAD), lambda h, p, kb, qi: (h, qi[p], 0)),
                   pl.BlockSpec((1, 1, t), lambda h, p, kb, qi: (h, 0, qi[p]))],
        scratch_shapes=[pltpu.VMEM((1, t), F32), pltpu.VMEM((1, t), F32), pltpu.VMEM((HEAD_PAD, t), F32)])
    return _call(
        body, name="mla_fwd", grid_spec=grid_spec,
        out_shape=[jax.ShapeDtypeStruct((H, S, HEAD_PAD), F32), jax.ShapeDtypeStruct((H, 1, S), F32)],
        compiler_params=_params(2))(kb_tab, qi_tab, q, k, vt)


def mla_bwd(q, k, kt, v, do, lse, delta):
    H, S, _ = q.shape
    t = ATT_T
    nq = S // t
    kb_tab, qi_tab, n_pairs = _pair_tables(nq, True)

    def body(kb_ref, qi_ref, q_ref, k_ref, kt_ref, v_ref, do_ref, lse_ref, dl_ref,
             dqt_ref, dk_ref, dv_ref, dk_s, dv_s):
        p_id = pl.program_id(1)
        kb, qi = kb_ref[p_id], qi_ref[p_id]

        @pl.when(p_id == 0)
        def _():
            dqt_ref[...] = jnp.zeros_like(dqt_ref)

        @pl.when(qi == kb)
        def _():
            dk_s[...] = jnp.zeros_like(dk_s)
            dv_s[...] = jnp.zeros_like(dv_s)

        def step(masked):
            st = _nt(k_ref[0], q_ref[0])
            if masked:
                st = jnp.where(_diag_visible(t), st, NEG_INF)
            pt = jnp.exp2(st - lse_ref[0])
            dv_s[...] += _nn(pt.astype(BF16), do_ref[0])
            dsb = (pt * (_nt(v_ref[0], do_ref[0]) - dl_ref[0])).astype(BF16)
            dk_s[...] += _nn(dsb, q_ref[0])
            dqt_ref[0, qi] += _nn(kt_ref[0], dsb)

        @pl.when(qi == kb)
        def _():
            step(True)

        @pl.when(qi > kb)
        def _():
            step(False)

        @pl.when(qi == nq - 1)
        def _():
            dk_ref[0] = dk_s[...]
            dv_ref[0] = dv_s[...]

    qtile = pl.BlockSpec((1, t, HEAD_PAD), lambda h, p, kb, qi: (h, qi[p], 0))
    ktile = pl.BlockSpec((1, t, HEAD_PAD), lambda h, p, kb, qi: (h, kb[p], 0))
    stat = pl.BlockSpec((1, 1, t), lambda h, p, kb, qi: (h, 0, qi[p]))
    grid_spec = pltpu.PrefetchScalarGridSpec(
        num_scalar_prefetch=2, grid=(H, n_pairs),
        in_specs=[qtile, ktile, pl.BlockSpec((1, HEAD_PAD, t), lambda h, p, kb, qi: (h, 0, kb[p])), ktile, qtile,
                  stat, stat],
        out_specs=[pl.BlockSpec((1, nq, HEAD_PAD, t), lambda h, p, kb, qi: (h, 0, 0, 0)), ktile, ktile],
        scratch_shapes=[pltpu.VMEM((t, HEAD_PAD), F32), pltpu.VMEM((t, HEAD_PAD), F32)])
    return _call(
        body, name="mla_bwd", grid_spec=grid_spec,
        out_shape=[jax.ShapeDtypeStruct((H, nq, HEAD_PAD, t), F32),
                   jax.ShapeDtypeStruct((H, S, HEAD_PAD), F32), jax.ShapeDtypeStruct((H, S, HEAD_PAD), F32)],
        compiler_params=_params(2))(kb_tab, qi_tab, q, k, kt, v, do, lse, delta)


def _band_specs(t):
    prev = lambda i: jnp.maximum(i - 1, 0)
    return dict(
        cur=pl.BlockSpec((1, t, HEAD_PAD), lambda h, i: (h, i, 0)),
        prev=pl.BlockSpec((1, t, HEAD_PAD), lambda h, i: (h, prev(i), 0)),
        cur_t=pl.BlockSpec((1, HEAD_PAD, t), lambda h, i: (h, 0, i)),
        prev_t=pl.BlockSpec((1, HEAD_PAD, t), lambda h, i: (h, 0, prev(i))),
        stat=pl.BlockSpec((1, 1, t), lambda h, i: (h, 0, i)),
        bias_prev=pl.BlockSpec((1, 1, t, t), lambda h, i: (h, jnp.where(i == 0, 1, 0), 0, 0)),
        bias_cur=pl.BlockSpec((1, 1, t, t), lambda h, i: (h, 2, 0, 0)))


def band_fwd(q, k, vt, bias):
    H, S, _ = q.shape
    t = ATT_T
    sp = _band_specs(t)

    def body(q_ref, kp_ref, kc_ref, vtp_ref, vtc_ref, bp_ref, bc_ref, o_ref, lse_ref):
        s0 = _nt(kp_ref[0], q_ref[0]) + bp_ref[0, 0]
        s1 = _nt(kc_ref[0], q_ref[0]) + bc_ref[0, 0]
        m = jnp.maximum(jnp.max(s0, axis=0, keepdims=True), jnp.max(s1, axis=0, keepdims=True))
        p0 = jnp.exp2(s0 - m)
        p1 = jnp.exp2(s1 - m)
        l = jnp.sum(p0, axis=0, keepdims=True) + jnp.sum(p1, axis=0, keepdims=True)
        ot = _nn(vtp_ref[0], p0.astype(BF16)) + _nn(vtc_ref[0], p1.astype(BF16))
        o_ref[0] = (ot / l).T
        lse_ref[0] = m + jnp.log2(l)

    return _call(
        body, name="band_fwd", grid=(H, S // t),
        in_specs=[sp['cur'], sp['prev'], sp['cur'], sp['prev_t'], sp['cur_t'], sp['bias_prev'], sp['bias_cur']],
        out_specs=[sp['cur'], sp['stat']],
        out_shape=[jax.ShapeDtypeStruct((H, S, HEAD_PAD), F32), jax.ShapeDtypeStruct((H, 1, S), F32)],
        compiler_params=_params(2))(q, k, k, vt, vt, bias, bias)


def band_bwd(q, k, kt, v, do, lse, delta, bias):
    H, S, _ = q.shape
    t = ATT_T
    sp = _band_specs(t)

    def body(q_ref, kp_ref, kc_ref, ktp_ref, ktc_ref, vp_ref, vc_ref, do_ref, lse_ref, dl_ref, bp_ref, bc_ref,
             dqt_ref, dk_ref, dv_ref, db_ref):
        i = pl.program_id(1)

        @pl.when(i == 0)
        def _():
            dk_ref[...] = jnp.zeros_like(dk_ref)
            dv_ref[...] = jnp.zeros_like(dv_ref)
            db_ref[...] = jnp.zeros_like(db_ref)

        qv, dov = q_ref[0], do_ref[0]
        dqt = jnp.zeros((HEAD_PAD, t), F32)
        windows = ((0, jnp.maximum(i - 1, 0), kp_ref, ktp_ref, vp_ref, bp_ref),
                   (1, i, kc_ref, ktc_ref, vc_ref, bc_ref))
        for w, blk, k_ref, kt_ref, v_ref, b_ref in windows:
            rows = pl.ds(pl.multiple_of(blk * t, t), t)
            pt = jnp.exp2(_nt(k_ref[0], qv) + b_ref[0, 0] - lse_ref[0])
            dv_ref[0, rows, :] += _nn(pt.astype(BF16), dov)
            ds = pt * (_nt(v_ref[0], dov) - dl_ref[0])
            db_ref[0, w] += ds
            dsb = ds.astype(BF16)
            dk_ref[0, rows, :] += _nn(dsb, qv)
            dqt = dqt + _nn(kt_ref[0], dsb)
        dqt_ref[0] = dqt

    whole = pl.BlockSpec((1, S, HEAD_PAD), lambda h, i: (h, 0, 0))
    return _call(
        body, name="band_bwd", grid=(H, S // t),
        in_specs=[sp['cur'], sp['prev'], sp['cur'], sp['prev_t'], sp['cur_t'], sp['prev'], sp['cur'], sp['cur'],
                  sp['stat'], sp['stat'], sp['bias_prev'], sp['bias_cur']],
        out_specs=[sp['cur_t'], whole, whole, pl.BlockSpec((1, 2, t, t), lambda h, i: (h, 0, 0, 0))],
        out_shape=[jax.ShapeDtypeStruct((H, HEAD_PAD, S), F32), jax.ShapeDtypeStruct((H, S, HEAD_PAD), F32),
                   jax.ShapeDtypeStruct((H, S, HEAD_PAD), F32), jax.ShapeDtypeStruct((H, 2, t, t), F32)],
        compiler_params=_params(2))(q, k, k, kt, kt, v, v, do, lse, delta, bias, bias)


def _compact(o_ref):
    return jnp.concatenate([o_ref[2 * p] + pltpu.roll(o_ref[2 * p + 1], 64, 1) for p in range(4)], axis=1)


def merge_fwd(ob, oc, proj, ya, gate_b, wbr, w_out, x, post_g):
    S = x.shape[0]
    ts = ROW_TILE

    def body(ob_ref, oc_ref, zb_ref, zc_ref, ya_ref, gl_ref, gb_ref, wbr_ref, wo_ref, x_ref, pg_ref,
             xo_ref, yb_ref, yc_ref, mg_ref, out_ref):
        zb, zc = zb_ref[...], zc_ref[...]
        yb = (_compact(ob_ref) * (zb * _sigmoid(zb))).astype(BF16)
        yc = (_compact(oc_ref) * (zc * _sigmoid(zc))).astype(BF16)
        yb_ref[...] = yb
        yc_ref[...] = yc
        merged = jnp.zeros((ts, D_MODEL), F32)
        for n, y in enumerate((ya_ref[...], yb, yc)):
            cols = slice(n * D_MODEL, (n + 1) * D_MODEL)
            gate = _sigmoid(gl_ref[:, cols] + gb_ref[:, cols])
            merged = merged + gate * _nn(y, wbr_ref[n])
        mb = merged.astype(BF16)
        mg_ref[...] = mb
        out = _nn(mb, wo_ref[...])
        out_ref[...] = out
        normed, _ = _rms(out, pg_ref[...])
        xo_ref[...] = x_ref[...] + normed

    row = lambda w: pl.BlockSpec((ts, w), lambda i: (i, 0))
    col = lambda w, c: pl.BlockSpec((ts, w), lambda i: (i, c))
    full = lambda shape: pl.BlockSpec(shape, lambda i: (0,) * len(shape))
    return _call(
        body, name="merge_fwd", grid=(S // ts,),
        in_specs=[_hspec(ts), _hspec(ts), col(512, P_ZB // 512), col(512, P_ZC // 512), row(512),
                  col(3072, P_G // 3072), full((1, 3072)), full((3, BR_WIDTH, D_MODEL)),
                  full((D_MODEL, D_MODEL)), row(D_MODEL), full((1, D_MODEL))],
        out_specs=[row(D_MODEL), row(512), row(512), row(D_MODEL), row(D_MODEL)],
        out_shape=[jax.ShapeDtypeStruct((S, D_MODEL), F32), jax.ShapeDtypeStruct((S, 512), BF16),
                   jax.ShapeDtypeStruct((S, 512), BF16), jax.ShapeDtypeStruct((S, D_MODEL), BF16),
                   jax.ShapeDtypeStruct((S, D_MODEL), F32)],
        compiler_params=_params(1))(ob, oc, proj, proj, ya, proj, gate_b, wbr, w_out, x, post_g)


def post_bwd(g, out, post_g, w_out):
    S = g.shape[0]
    ts = ROW_TILE

    def body(g_ref, out_ref, pg_ref, wo_ref, do_ref, dm_ref, gp_ref):
        ov = out_ref[...]
        _, r = _rms(ov, pg_ref[...])
        dout, gp = _rms_bwd(ov, pg_ref[...], r, g_ref[...])
        db = dout.astype(BF16)
        do_ref[...] = db
        dm_ref[...] = _nt(db, wo_ref[...])
        _accumulate(gp_ref, gp, pl.program_id(0) == 0)

    row = lambda: pl.BlockSpec((ts, D_MODEL), lambda i: (i, 0))
    full = lambda shape: pl.BlockSpec(shape, lambda i: (0,) * len(shape))
    return _call(
        body, name="post_bwd", grid=(S // ts,),
        in_specs=[row(), row(), full((1, D_MODEL)), full((D_MODEL, D_MODEL))],
        out_specs=[row(), row(), full((1, D_MODEL))],
        out_shape=[jax.ShapeDtypeStruct((S, D_MODEL), BF16), jax.ShapeDtypeStruct((S, D_MODEL), F32),
                   jax.ShapeDtypeStruct((1, D_MODEL), F32)],
        compiler_params=_params(1))(g, out, post_g, w_out)


def gate_bwd(dmerged, proj, gate_b, ya, yb, yc, wbr):
    S = dmerged.shape[0]
    ts = ROW_TILE

    def body(dm_ref, gl_ref, gb_ref, ya_ref, yb_ref, yc_ref, wbr_ref,
             dp_ref, dba_ref, dbb_ref, dbc_ref, dy_ref, ggb_ref):
        dm = dm_ref[...]
        ggb = []
        for n, (y_ref, dbr_ref) in enumerate(((ya_ref, dba_ref), (yb_ref, dbb_ref), (yc_ref, dbc_ref))):
            cols = slice(n * D_MODEL, (n + 1) * D_MODEL)
            br = _nn(y_ref[...], wbr_ref[n])
            sg = _sigmoid(gl_ref[:, cols] + gb_ref[:, cols])
            dgl = dm * br * (sg * (1.0 - sg))
            dp_ref[:, cols] = dgl
            ggb.append(jnp.sum(dgl, axis=0, keepdims=True))
            dbr = (dm * sg).astype(BF16)
            dbr_ref[...] = dbr
            dy_ref[n] = _nt(dbr, wbr_ref[n])
        _accumulate(ggb_ref, jnp.concatenate(ggb, axis=1), pl.program_id(0) == 0)

    row = lambda w: pl.BlockSpec((ts, w), lambda i: (i, 0))
    full = lambda shape: pl.BlockSpec(shape, lambda i: (0,) * len(shape))
    wide = jax.ShapeDtypeStruct((S, D_MODEL), BF16)
    return _call(
        body, name="gate_bwd", grid=(S // ts,),
        in_specs=[row(D_MODEL), pl.BlockSpec((ts, 3072), lambda i: (i, P_G // 3072)), full((1, 3072)),
                  row(512), row(512), row(512), full((3, BR_WIDTH, D_MODEL))],
        out_specs=[pl.BlockSpec((ts, 3072), lambda i: (i, P_G // 3072)), row(D_MODEL), row(D_MODEL), row(D_MODEL),
                   pl.BlockSpec((3, ts, 512), lambda i: (0, i, 0)), full((1, 3072))],
        out_shape=[jax.ShapeDtypeStruct((S, P_W), F32), wide, wide, wide,
                   jax.ShapeDtypeStruct((3, S, 512), F32), jax.ShapeDtypeStruct((1, 3072), F32)],
        compiler_params=_params(1))(dmerged, proj, gate_b, ya, yb, yc, wbr)


def ungate_bwd(dproj, dy, ob, oc, proj):
    S = proj.shape[0]
    ts = ROW_TILE

    def body(dp_in, dyb_ref, dyc_ref, ob_ref, oc_ref, zb_ref, zc_ref, dp_ref, dob_ref, doc_ref, dlb_ref, dlc_ref):
        del dp_in
        lane = lax.broadcasted_iota(jnp.int32, (ts, 128), 1)
        for n, (dy_ref, o_ref, z_ref, do_ref, dl_ref) in enumerate(
                ((dyb_ref, ob_ref, zb_ref, dob_ref, dlb_ref), (dyc_ref, oc_ref, zc_ref, doc_ref, dlc_ref))):
            zz = z_ref[...]
            dyv = dy_ref[0]
            sg = _sigmoid(zz)
            dp_ref[:, n * 512:(n + 1) * 512] = dyv * _compact(o_ref) * (sg * (1.0 + zz * (1.0 - sg)))
            do_c = dyv * (zz * sg)
            for p in range(4):
                piece = do_c[:, p * 128:(p + 1) * 128]
                for h, d in ((2 * p, jnp.where(lane < 64, piece, 0.0)),
                             (2 * p + 1, jnp.where(lane < 64, pltpu.roll(piece, 64, 1), 0.0))):
                    do_ref[h] = d.astype(BF16)
                    dl_ref[h] = jnp.sum(d * o_ref[h], axis=-1, keepdims=True)

    col = lambda c: pl.BlockSpec((ts, 512), lambda i: (i, c))
    dysp = lambda n: pl.BlockSpec((1, ts, 512), lambda i: (n, i, 0))
    stat = pl.BlockSpec((N_HEADS, ts, 1), lambda i: (0, i, 0))
    hshape = jax.ShapeDtypeStruct((N_HEADS, S, HEAD_PAD), BF16)
    sshape = jax.ShapeDtypeStruct((N_HEADS, S, 1), F32)
    return _call(
        body, name="ungate_bwd", grid=(S // ts,),
        in_specs=[pl.BlockSpec(memory_space=pl.ANY), dysp(1), dysp(2), _hspec(ts), _hspec(ts),
                  col(P_ZB // 512), col(P_ZC // 512)],
        out_specs=[pl.BlockSpec((ts, 1024), lambda i: (i, P_ZB // 1024)), _hspec(ts), _hspec(ts), stat, stat],
        out_shape=[jax.ShapeDtypeStruct(dproj.shape, F32), hshape, hshape, sshape, sshape],
        input_output_aliases={0: 0},
        compiler_params=_params(1))(dproj, dy, dy, ob, oc, proj, proj)


def loss_head(y, target):
    S, D = y.shape
    ts = ROW_TILE

    def body(y_ref, t_ref, dy_ref, sq_ref):
        d = y_ref[...] - t_ref[...]
        dy_ref[...] = d * (1.0 / D)
        _accumulate(sq_ref, jnp.sum(d * d, axis=0, keepdims=True), pl.program_id(0) == 0)

    row = pl.BlockSpec((ts, D), lambda i: (i, 0))
    return _call(
        body, name="loss_head", grid=(S // ts,), in_specs=[row, row],
        out_specs=[row, pl.BlockSpec((1, D), lambda i: (0, 0))],
        out_shape=[jax.ShapeDtypeStruct((S, D), F32), jax.ShapeDtypeStruct((1, D), F32)],
        compiler_params=_params(1))(y, target)


def _row_tile(rows, cols):
    for cand in (1024, 512, 256, 128, 64, 32, 16, 8):
        if rows % cand == 0 and cand * cols * 4 <= 1024 * 1024:
            return cand
    return rows


def adamw(w, grads, m, v):
    shape = w.shape
    cols = shape[-1]
    rows = int(np.prod(shape[:-1]))
    tr = _row_tile(rows, cols)
    n_g = len(grads)
    c1 = 1.0 - ADAM_B1 ** ADAM_STEP
    c2 = 1.0 - ADAM_B2 ** ADAM_STEP

    def body(*refs):
        w_ref, m_ref, v_ref = refs[:3]
        g_refs = refs[3:3 + n_g]
        go_ref, d_ref, mo_ref, vo_ref = refs[3 + n_g:]
        gv = g_refs[0][...]
        for g_ref in g_refs[1:]:
            gv = gv + g_ref[...]
        go_ref[...] = gv
        mn = ADAM_B1 * m_ref[...] + (1.0 - ADAM_B1) * gv
        vn = ADAM_B2 * v_ref[...] + (1.0 - ADAM_B2) * (gv * gv)
        mo_ref[...] = mn
        vo_ref[...] = vn
        d_ref[...] = -ADAM_LR * ((mn / c1) / (jnp.sqrt(vn / c2) + ADAM_EPS) + ADAM_WD * w_ref[...])

    blk = pl.BlockSpec((tr, cols), lambda i: (i, 0))
    sds = jax.ShapeDtypeStruct((rows, cols), F32)
    outs = _call(
        body, name="adamw", grid=(rows // tr,), in_specs=[blk] * (3 + n_g), out_specs=[blk] * 4,
        out_shape=[sds] * 4, compiler_params=_params(1))(
            *[a.reshape(rows, cols) for a in (w, m, v, *grads)])
    return [o.reshape(shape) for o in outs]


def add_lead(parts):
    n = parts.shape[0]
    shape = parts.shape[1:]
    cols = shape[-1]
    rows = int(np.prod(shape[:-1]))
    tr = _row_tile(rows, cols * n)

    def body(p_ref, o_ref):
        acc = p_ref[0]
        for s in range(1, n):
            acc = acc + p_ref[s]
        o_ref[...] = acc

    out = _call(
        body, name="add_lead", grid=(rows // tr,),
        in_specs=[pl.BlockSpec((n, tr, cols), lambda i: (0, i, 0))],
        out_specs=pl.BlockSpec((tr, cols), lambda i: (i, 0)),
        out_shape=jax.ShapeDtypeStruct((rows, cols), F32),
        compiler_params=_params(1))(parts.reshape(n, rows, cols))
    return out.reshape(shape)


ANY = pl.BlockSpec(memory_space=pl.ANY)


def _other_chips(x, y):
    return [(1 - x, y), (x, 1 - y), (1 - x, 1 - y)]


def chip_exchange(arrays, scatter, name):
    n = len(arrays)

    def body(*refs):
        srcs, outs = refs[:n], refs[n:2 * n]
        send_sems, recv_sems, local_sems = refs[2 * n:]
        x, y, c = lax.axis_index("x"), lax.axis_index("y"), lax.axis_index("c")
        me = 2 * x + y
        peers = _other_chips(x, y)
        local_copies, remote_copies = [], []
        for a in range(n):
            local_src = srcs[a].at[me] if scatter else srcs[a]
            mine = pltpu.make_async_copy(local_src, outs[a].at[me], local_sems.at[a])
            mine.start()
            local_copies.append(mine)
            for j, (px, py) in enumerate(peers):
                cp = pltpu.make_async_remote_copy(
                    src_ref=srcs[a].at[2 * px + py] if scatter else srcs[a],
                    dst_ref=outs[a].at[me], send_sem=send_sems.at[3 * a + j], recv_sem=recv_sems.at[3 * a + j],
                    device_id=(px, py, c), device_id_type=MESH)
                cp.start()
                remote_copies.append(cp)
        for a in range(n):
            local_src = srcs[a].at[me] if scatter else srcs[a]
            for j, (px, py) in enumerate(peers):
                pltpu.make_async_remote_copy(
                    src_ref=local_src, dst_ref=outs[a].at[2 * px + py], send_sem=send_sems.at[3 * a + j],
                    recv_sem=recv_sems.at[3 * a + j], device_id=(px, py, c), device_id_type=MESH).wait_recv()
        for cp in remote_copies:
            cp.wait_send()
        for cp in local_copies:
            cp.wait()

    return _call(
        body, name=name, in_specs=[ANY] * n, out_specs=[ANY] * n,
        out_shape=[jax.ShapeDtypeStruct(a.shape if scatter else (4,) + a.shape, a.dtype) for a in arrays],
        scratch_shapes=[pltpu.SemaphoreType.DMA((3 * n,)), pltpu.SemaphoreType.DMA((3 * n,)),
                        pltpu.SemaphoreType.DMA((n,))])(*arrays)


def sibling_exchange(arrays):
    n = len(arrays)

    def body(*refs):
        srcs, outs = refs[:n], refs[n:2 * n]
        send_sems, recv_sems = refs[2 * n:]
        x, y, c = lax.axis_index("x"), lax.axis_index("y"), lax.axis_index("c")
        copies = [pltpu.make_async_remote_copy(src_ref=srcs[a], dst_ref=outs[a], send_sem=send_sems.at[a],
                                               recv_sem=recv_sems.at[a], device_id=(x, y, 1 - c), device_id_type=MESH)
                  for a in range(n)]
        for cp in copies:
            cp.start()
        for cp in copies:
            cp.wait()

    return _call(
        body, name="sibling_exchange", in_specs=[ANY] * n, out_specs=[ANY] * n,
        out_shape=[jax.ShapeDtypeStruct(a.shape, a.dtype) for a in arrays],
        scratch_shapes=[pltpu.SemaphoreType.DMA((n,)), pltpu.SemaphoreType.DMA((n,))])(*arrays)


def _perm_from_shards(sh):
    rows = sh.shape[1]
    pieces, pos = [], 0
    for lo, hi, plo in sorted(NAT_SEGS, key=lambda s: s[2]):
        if plo > pos:
            pieces.append(jnp.zeros((rows, plo - pos), sh.dtype))
            pos = plo
        c = lo
        while c < hi:
            kk = c // SHARD_COLS
            e = min(hi, (kk + 1) * SHARD_COLS)
            pieces.append(sh[kk][:, c - kk * SHARD_COLS:e - kk * SHARD_COLS])
            c = e
        pos += hi - lo
    if pos < P_W:
        pieces.append(jnp.zeros((rows, P_W - pos), sh.dtype))
    return jnp.concatenate(pieces, axis=1)


def _shards_from_perm(p):
    out = []
    for kk in range(4):
        lo_k, hi_k = kk * SHARD_COLS, (kk + 1) * SHARD_COLS
        pieces = []
        for lo, hi, plo in NAT_SEGS:
            a, b = max(lo, lo_k), min(hi, hi_k)
            if a < b:
                pieces.append(p[:, plo + (a - lo):plo + (b - lo)])
        out.append(jnp.concatenate(pieces, axis=1))
    return jnp.stack(out)


def _split4(a, axis):
    shape = a.shape
    a = a.reshape(shape[:axis] + (4, shape[axis] // 4) + shape[axis + 1:])
    return jnp.moveaxis(a, axis, 0)


def _join4(a, axis):
    a = jnp.moveaxis(a, 0, axis)
    shape = a.shape
    return a.reshape(shape[:axis] + (4 * shape[axis + 1],) + shape[axis + 2:])


def _pad_heads(w, per_head, lo, hi):
    r = w.shape[0]
    wh = w.reshape(r, N_HEADS, per_head)[:, :, lo:hi]
    return jnp.pad(wh, ((0, 0), (0, 0), (0, HEAD_PAD - (hi - lo)))).reshape(r, N_HEADS * HEAD_PAD)


def _rope_table(S):
    half = MLA_ROPE // 2
    inv = ROPE_BASE ** (-jnp.arange(half, dtype=F32) / half)
    ang = jnp.arange(S).astype(F32)[:, None] * inv[None, :]
    cos, sin = jnp.cos(ang), jnp.sin(ang)
    z = lambda n: jnp.zeros((S, n), F32)
    c = jnp.concatenate([jnp.ones((S, MLA_NOPE), F32), cos, cos, z(32)], axis=1)
    a = jnp.concatenate([z(MLA_NOPE), -sin, z(48)], axis=1)
    b = jnp.concatenate([z(MLA_NOPE + half), sin, z(32)], axis=1)
    return jnp.stack([c, a, b])


def _band_onehot():
    t = ATT_T
    d = np.arange(-(t - 1), t)
    idx = np.stack([np.clip(off + d, -REL_CLIP, REL_CLIP) + REL_CLIP for off in (t, 0)])
    return (idx[:, :, None] == np.arange(2 * REL_CLIP + 1)[None, None, :]).astype(np.float32)


def _band_mask():
    t = ATT_T
    qc = (np.arange(t) >> CHUNK_SHIFT)[:, None]
    kc = (np.arange(t) >> CHUNK_SHIFT)[None, :]
    return np.stack([kc >= qc, kc <= qc])


def _bias_tiles(table):
    t = ATT_T
    vals = jnp.einsum('hr,wdr->hwd', table, jnp.asarray(_band_onehot()), precision=lax.Precision.HIGHEST)
    rowsrep = jnp.broadcast_to(vals[:, :, None, :], (N_HEADS, 2, t, 2 * t - 1)).reshape(N_HEADS, 2, -1)
    skew = jnp.pad(rowsrep, ((0, 0), (0, 0), (0, t))).reshape(N_HEADS, 2, t, 2 * t)[..., :t]
    tiles = jnp.where(jnp.asarray(_band_mask()), skew[..., ::-1] * LOG2E, NEG_INF)
    tiles = jnp.swapaxes(tiles, 2, 3)
    return jnp.stack([tiles[:, 0], jnp.full((N_HEADS, t, t), NEG_INF, F32), tiles[:, 1]], axis=1)


def _bias_tiles_grad(dtiles):
    t = ATT_T
    flipped = jnp.swapaxes(dtiles, 2, 3)[..., ::-1]
    flat = jnp.pad(flipped, ((0, 0), (0, 0), (0, 0), (0, t))).reshape(N_HEADS, 2, -1)[..., :t * (2 * t - 1)]
    diag = jnp.sum(flat.reshape(N_HEADS, 2, t, 2 * t - 1), axis=2)
    return jnp.einsum('hwd,wdr->hr', diag, jnp.asarray(_band_onehot()), precision=lax.Precision.HIGHEST)


def _layer_consts(lw):
    tri = np.tril(np.ones((SGU_BLOCK, SGU_BLOCK), np.float32))
    ws = (lw['sgu_w'] * tri).astype(BF16)
    return dict(
        ws=ws, ws_t=jnp.swapaxes(ws, 1, 2), sgu_bias=jnp.repeat(lw['sgu_b'].T, CA_HEAD_DIM, axis=1),
        bias=_bias_tiles(lw['ca_rel_bias']),
        wq=_pad_heads(lw['mla_w_uq'], MLA_QK, 0, MLA_QK),
        wk=_pad_heads(lw['mla_w_ukv'], MLA_NOPE + MLA_V, 0, MLA_NOPE),
        wv=_pad_heads(lw['mla_w_ukv'], MLA_NOPE + MLA_V, MLA_NOPE, MLA_NOPE + MLA_V),
        gate_b=lw['gate_b'].reshape(1, 3 * D_MODEL),
        pre_g=lw['pre_g'][None], post_g=lw['post_g'][None], ln_g=lw['sgu_ln_g'][None], ln_b=lw['sgu_ln_b'][None],
        qg=lw['mla_q_norm_g'][None], kvg=lw['mla_kv_norm_g'][None])


def _layer_fwd(x, lw, k, tab):
    proj, xn = norm_matmul(x, k['pre_g'], lw['w_in'])
    ya = sgu_fwd(proj, k['ln_g'], k['ln_b'], k['ws'], k['sgu_bias'])
    qb, kb, vb, qc, kc, vc, cq, ckv = prep_fwd(proj, tab, k['qg'], k['kvg'], k['wq'], k['wk'], k['wv'])
    tr = lambda a: jnp.swapaxes(a, 1, 2)
    ob, lse_b = mla_fwd(qb, kb, tr(vb))
    oc, lse_c = band_fwd(qc, kc, tr(vc), k['bias'])
    x_new, yb, yc, merged, out = merge_fwd(ob, oc, proj, ya, k['gate_b'], lw['w_branch'], lw['w_out'], x,
                                           k['post_g'])
    saved = dict(x=x, proj=proj, xn=xn, ya=ya, yb=yb, yc=yc, qb=qb, kb=kb, vb=vb, qc=qc, kc=kc, vc=vc,
                 cq=cq, ckv=ckv, ob=ob, oc=oc, lse_b=lse_b, lse_c=lse_c, merged=merged, out=out)
    return x_new, saved


def _layer_bwd(g, s, lw, k, tab):
    S = g.shape[0]
    H = N_HEADS
    dout, dmerged, g_post = post_bwd(g, s['out'], k['post_g'], lw['w_out'])
    g_w_out = matmul_tn(s['merged'], dout, 512)
    dproj, dba, dbb, dbc, dy, g_gate_b = gate_bwd(dmerged, s['proj'], k['gate_b'], s['ya'], s['yb'], s['yc'],
                                                  lw['w_branch'])
    g_w_branch = jnp.stack([matmul_tn(y, d, 512) for y, d in ((s['ya'], dba), (s['yb'], dbb), (s['yc'], dbc))])
    dproj, dob, doc, dl_b, dl_c = ungate_bwd(dproj, dy, s['ob'], s['oc'], s['proj'])
    row = lambda a: a.reshape(H, 1, S)
    tr = lambda a: jnp.swapaxes(a, 1, 2)
    dqt, dkb, dvb = mla_bwd(s['qb'], s['kb'], tr(s['kb']), s['vb'], dob, s['lse_b'], row(dl_b))
    dqb = jnp.swapaxes(dqt, 2, 3).reshape(H, S, HEAD_PAD)
    dqct, dkc, dvc, dbias = band_bwd(s['qc'], s['kc'], tr(s['kc']), s['vc'], doc, s['lse_c'], row(dl_c), k['bias'])
    dproj, dqf, dkf, dvf, g_qg, g_kvg = prep_bwd(dproj, dqb, dkb, dvb, tr(dqct), dkc, dvc, s['proj'], tab,
                                                 k['qg'], k['kvg'], k['wq'], k['wk'], k['wv'])
    g_wq = matmul_tn(s['cq'], dqf, 512).reshape(MLA_Q_RANK, H, HEAD_PAD)[:, :, :MLA_QK]
    g_wk = matmul_tn(s['ckv'], dkf, 512).reshape(MLA_KV_RANK, H, HEAD_PAD)[:, :, :MLA_NOPE]
    g_wv = matmul_tn(s['ckv'], dvf, 512).reshape(MLA_KV_RANK, H, HEAD_PAD)[:, :, :MLA_V]
    dproj, g_ln_g, g_ln_b, g_ws, g_sgu_bias = sgu_bwd(dproj, dy, s['proj'], k['ln_g'], k['ln_b'], k['ws'],
                                                      k['ws_t'], k['sgu_bias'])
    g_w_in = matmul_tn(s['xn'], dproj, MM_TN)
    dx, g_pre = proj_bwd_x(dproj, lw['w_in'], s['x'], k['pre_g'], g)
    tri = np.tril(np.ones((SGU_BLOCK, SGU_BLOCK), np.float32))
    grads = dict(
        w_in=g_w_in, pre_g=g_pre[0], post_g=g_post[0], sgu_ln_g=g_ln_g[0], sgu_ln_b=g_ln_b[0],
        sgu_w=g_ws * tri, sgu_b=jnp.sum(g_sgu_bias.reshape(SGU_BLOCK, 8, CA_HEAD_DIM), axis=2).T,
        mla_q_norm_g=g_qg[0], mla_kv_norm_g=g_kvg[0],
        mla_w_uq=g_wq.reshape(MLA_Q_RANK, H * MLA_QK),
        mla_w_ukv=jnp.concatenate([g_wk, g_wv], axis=2).reshape(MLA_KV_RANK, H * (MLA_NOPE + MLA_V)),
        ca_rel_bias=_bias_tiles_grad(dbias), w_branch=g_w_branch,
        gate_b=g_gate_b.reshape(N_BRANCH, D_MODEL), w_out=g_w_out)
    return dx, grads


def local_step(x, target, layer_w, on_layer_grads):
    S = x.shape[0]
    depth = len(layer_w)
    tab = _rope_table(S)
    consts = [_layer_consts(lw) for lw in layer_w]
    saved = []
    for l in range(depth):
        x, s = _layer_fwd(x, layer_w[l], consts[l], tab)
        saved.append(s)
    g, sq = loss_head(x, target)
    for l in reversed(range(depth)):
        g, grads = _layer_bwd(g, saved[l], layer_w[l], consts[l], tab)
        on_layer_grads(l, grads)
    return sq, g


def gather_weights(shards):
    depth = shards['w_in'].shape[0]
    arrays = [shards[n] if n == 'gate_b' else shards[n].astype(BF16) for n in SHARDED]
    got = dict(zip(SHARDED, chip_exchange(arrays, False, "gather_weights")))
    layers = []
    for l in range(depth):
        lw = {n: _join4(got[n][:, l], SHARD_AXIS[n]) for n in SHARDED if n != 'w_in'}
        lw['w_in'] = _perm_from_shards(got['w_in'][:, l])
        layers.append(lw)
    return layers


def _small_pack(grads):
    flat = jnp.concatenate([grads[n].reshape(-1) for n in SMALL])
    quarter = -(-flat.size // (4 * 1024)) * 1024
    return jnp.pad(flat, (0, 4 * quarter - flat.size)).reshape(4, quarter // 128, 128)


def reduce_layer(grads):
    parts = [_shards_from_perm(grads['w_in'])]
    parts += [_split4(grads[n], SHARD_AXIS[n]) for n in SHARDED if n != 'w_in']
    parts.append(_small_pack(grads))
    landed = chip_exchange(parts, True, "scatter_grads")
    mine = [add_lead(p) for p in landed]
    theirs = sibling_exchange(mine)
    return mine, theirs


def kernel(x, w_in, pre_g, post_g, sgu_ln_g, sgu_ln_b, sgu_w, sgu_b, mla_q_norm_g, mla_kv_norm_g, mla_w_uq, mla_w_ukv, ca_rel_bias, w_branch, gate_b, w_out, loss_target, m_w_in, m_pre_g, m_post_g, m_sgu_ln_g, m_sgu_ln_b, m_sgu_w, m_sgu_b, m_mla_q_norm_g, m_mla_kv_norm_g, m_mla_w_uq, m_mla_w_ukv, m_ca_rel_bias, m_w_branch, m_gate_b, m_w_out, v_w_in, v_pre_g, v_post_g, v_sgu_ln_g, v_sgu_ln_b, v_sgu_w, v_sgu_b, v_mla_q_norm_g, v_mla_kv_norm_g, v_mla_w_uq, v_mla_w_ukv, v_ca_rel_bias, v_w_branch, v_gate_b, v_w_out):
    w = dict(w_in=w_in, pre_g=pre_g, post_g=post_g, sgu_ln_g=sgu_ln_g, sgu_ln_b=sgu_ln_b, sgu_w=sgu_w, sgu_b=sgu_b,
             mla_q_norm_g=mla_q_norm_g, mla_kv_norm_g=mla_kv_norm_g, mla_w_uq=mla_w_uq, mla_w_ukv=mla_w_ukv,
             ca_rel_bias=ca_rel_bias, w_branch=w_branch, gate_b=gate_b, w_out=w_out)
    m = dict(w_in=m_w_in, pre_g=m_pre_g, post_g=m_post_g, sgu_ln_g=m_sgu_ln_g, sgu_ln_b=m_sgu_ln_b, sgu_w=m_sgu_w,
             sgu_b=m_sgu_b, mla_q_norm_g=m_mla_q_norm_g, mla_kv_norm_g=m_mla_kv_norm_g, mla_w_uq=m_mla_w_uq,
             mla_w_ukv=m_mla_w_ukv, ca_rel_bias=m_ca_rel_bias, w_branch=m_w_branch, gate_b=m_gate_b, w_out=m_w_out)
    v = dict(w_in=v_w_in, pre_g=v_pre_g, post_g=v_post_g, sgu_ln_g=v_sgu_ln_g, sgu_ln_b=v_sgu_ln_b, sgu_w=v_sgu_w,
             sgu_b=v_sgu_b, mla_q_norm_g=v_mla_q_norm_g, mla_kv_norm_g=v_mla_kv_norm_g, mla_w_uq=v_mla_w_uq,
             mla_w_ukv=v_mla_w_ukv, ca_rel_bias=v_ca_rel_bias, w_branch=v_w_branch, gate_b=v_gate_b, w_out=v_w_out)
    depth = w_in.shape[0]
    layer_w = gather_weights({n: w[n] for n in SHARDED})
    for l in range(depth):
        layer_w[l].update({n: w[n][l] for n in SMALL})

    reduced = [None] * depth

    def on_layer_grads(l, grads):
        reduced[l] = reduce_layer(grads)

    sq, grad_x = local_step(x[0], loss_target[0], layer_w, on_layer_grads)
    loss = lax.psum(0.5 * jnp.sum(sq) / D_MODEL, ("x", "y", "c"))

    out = {}
    for a, n in enumerate(SHARDED):
        mine = jnp.stack([reduced[l][0][a] for l in range(depth)])
        theirs = jnp.stack([reduced[l][1][a] for l in range(depth)])
        out[n] = adamw(w[n], [mine, theirs], m[n], v[n])
    small = jnp.stack([jnp.stack([reduced[l][0][-1] for l in range(depth)]),
                       jnp.stack([reduced[l][1][-1] for l in range(depth)])])
    quarter = add_lead(small)
    full = chip_exchange([quarter], False, "gather_small")[0]
    full = jnp.moveaxis(full, 0, 1).reshape(depth, -1)
    off = 0
    for n in SMALL:
        size = int(np.prod(w[n].shape[1:]))
        out[n] = adamw(w[n], [full[:, off:off + size].reshape(w[n].shape)], m[n], v[n])
        off += size
    return (loss, grad_x[None], *[out[n][0] for n in WEIGHTS], *[out[n][1] for n in WEIGHTS],
            *[out[n][2] for n in WEIGHTS], *[out[n][3] for n in WEIGHTS])
```

```python
import numpy as np
import jax
import jax.numpy as jnp
from jax import lax
from jax.experimental import pallas as pl
from jax.experimental.pallas import tpu as pltpu

F32 = jnp.float32
BF16 = jnp.bfloat16
MESH = pl.DeviceIdType.MESH

EPS = 1e-6
NEG_INF = -1e30
D_MODEL = 1024
BR_WIDTH = 512
N_BRANCH = 3
N_HEADS = 8
HEAD_PAD = 128
CHUNK_SHIFT = 6
SGU_BLOCK = 128
MLA_NOPE, MLA_ROPE, MLA_V = 64, 32, 64
MLA_QK = MLA_NOPE + MLA_ROPE
MLA_Q_RANK, MLA_KV_RANK = 256, 128
CA_HEAD_DIM = 64
REL_CLIP = 128
ROPE_BASE = 10000.0
D_IN = 7584

ADAM_LR, ADAM_B1, ADAM_B2, ADAM_EPS, ADAM_WD, ADAM_STEP = 0.001, 0.9, 0.999, 1e-08, 0.01, 10

P_QC, P_KC, P_VC, P_QD, P_KVD, P_KR, P_ZB, P_ZC, P_G, P_U, P_V, P_ZA, P_W = (
    0, 512, 1024, 1536, 1792, 1920, 2048, 2560, 3072, 6144, 6656, 7168, 7680)
NAT_SEGS = [(0, 1536, P_U), (1536, 1920, P_QD), (1920, 1952, P_KR + MLA_NOPE), (1952, 2464, P_ZB),
            (2464, 4000, P_QC), (4000, 4512, P_ZC), (4512, 7584, P_G)]
SHARD_COLS = D_IN // 4

VMEM_LIMIT = 48 * 1024 * 1024
ATT_T = 512
MLA_HEADS_PER_STEP = 2
ROW_TILE = 256
MM_TM = 512
MM_TN = 768
LOG2E = 1.4426950408889634
MLA_SCALE = MLA_QK ** -0.5
CA_SCALE = CA_HEAD_DIM ** -0.5

WEIGHTS = ['w_in', 'pre_g', 'post_g', 'sgu_ln_g', 'sgu_ln_b', 'sgu_w', 'sgu_b', 'mla_q_norm_g',
           'mla_kv_norm_g', 'mla_w_uq', 'mla_w_ukv', 'ca_rel_bias', 'w_branch', 'gate_b', 'w_out']
SHARDED = ['w_in', 'mla_w_uq', 'mla_w_ukv', 'w_branch', 'gate_b', 'w_out']
SMALL = ['pre_g', 'post_g', 'sgu_ln_g', 'sgu_ln_b', 'sgu_w', 'sgu_b', 'mla_q_norm_g',
         'mla_kv_norm_g', 'ca_rel_bias']
SHARD_AXIS = {'w_in': 1, 'mla_w_uq': 1, 'mla_w_ukv': 1, 'w_branch': 2, 'gate_b': 1, 'w_out': 0}


def _call(body, **kw):
    return pl.pallas_call(body, **kw)


def _params(n_axes):
    return pltpu.CompilerParams(dimension_semantics=("arbitrary",) * n_axes,
                                vmem_limit_bytes=VMEM_LIMIT)


def _nt(a, b):
    return lax.dot_general(a, b, (((1,), (1,)), ((), ())), preferred_element_type=F32)


def _nn(a, b):
    return jnp.dot(a, b, preferred_element_type=F32)


def _tn(a, b):
    return lax.dot_general(a, b, (((0,), (0,)), ((), ())), preferred_element_type=F32)


def _rms(xv, g):
    r = lax.rsqrt(jnp.mean(xv * xv, axis=-1, keepdims=True) + EPS)
    return xv * r * g, r


def _rms_bwd(xv, g, r, dy):
    gy = dy * g
    dx = r * gy - xv * (r * r * r) * jnp.mean(xv * gy, axis=-1, keepdims=True)
    dg = jnp.sum(dy * (xv * r), axis=0, keepdims=True)
    return dx, dg


def _sigmoid(z):
    return 1.0 / (1.0 + jnp.exp(-z))


def _rope(xv, c, a, b):
    return xv * c + pltpu.roll(xv, 112, 1) * a + pltpu.roll(xv, 16, 1) * b


def _accumulate(ref, val, first):
    @pl.when(first)
    def _():
        ref[...] = val

    @pl.when(jnp.logical_not(first))
    def _():
        ref[...] += val


def norm_matmul(x, g, w):
    S, D = x.shape
    N = w.shape[1]
    tm, tn = min(S, MM_TM), MM_TN

    def body(x_ref, g_ref, w_ref, o_ref, xn_ref):
        @pl.when(pl.program_id(1) == 0)
        def _():
            y, _ = _rms(x_ref[...], g_ref[...])
            xn_ref[...] = y.astype(BF16)

        o_ref[...] = _nn(xn_ref[...], w_ref[...])

    return _call(
        body, name="norm_matmul", grid=(S // tm, N // tn),
        in_specs=[pl.BlockSpec((tm, D), lambda i, j: (i, 0)),
                  pl.BlockSpec((1, D), lambda i, j: (0, 0)),
                  pl.BlockSpec((D, tn), lambda i, j: (0, j))],
        out_specs=[pl.BlockSpec((tm, tn), lambda i, j: (i, j)),
                   pl.BlockSpec((tm, D), lambda i, j: (i, 0))],
        out_shape=[jax.ShapeDtypeStruct((S, N), F32), jax.ShapeDtypeStruct((S, D), BF16)],
        compiler_params=_params(2))(x, g, w)


def proj_bwd_x(dproj, w, x, g, resid):
    S, N = dproj.shape
    D = x.shape[1]
    tm, tk = min(S, MM_TM), MM_TN
    nk = N // tk

    def body(dp_ref, w_ref, x_ref, g_ref, r_ref, dx_ref, dg_ref, acc_ref):
        i, k = pl.program_id(0), pl.program_id(1)

        @pl.when(k == 0)
        def _():
            acc_ref[...] = jnp.zeros_like(acc_ref)

        acc_ref[...] += _nt(dp_ref[...].astype(BF16), w_ref[...])

        @pl.when(k == nk - 1)
        def _():
            xv = x_ref[...]
            _, r = _rms(xv, g_ref[...])
            dx, dg = _rms_bwd(xv, g_ref[...], r, acc_ref[...])
            dx_ref[...] = dx + r_ref[...]
            _accumulate(dg_ref, dg, i == 0)

    return _call(
        body, name="proj_bwd_x", grid=(S // tm, nk),
        in_specs=[pl.BlockSpec((tm, tk), lambda i, k: (i, k)),
                  pl.BlockSpec((D, tk), lambda i, k: (0, k)),
                  pl.BlockSpec((tm, D), lambda i, k: (i, 0)),
                  pl.BlockSpec((1, D), lambda i, k: (0, 0)),
                  pl.BlockSpec((tm, D), lambda i, k: (i, 0))],
        out_specs=[pl.BlockSpec((tm, D), lambda i, k: (i, 0)),
                   pl.BlockSpec((1, D), lambda i, k: (0, 0))],
        out_shape=[jax.ShapeDtypeStruct((S, D), F32), jax.ShapeDtypeStruct((1, D), F32)],
        scratch_shapes=[pltpu.VMEM((tm, D), F32)],
        compiler_params=_params(2))(dproj, w, x, g, resid)


def matmul_tn(a, b, tn):
    S, M = a.shape
    N = b.shape[1]
    tk = min(S, 512)

    def body(a_ref, b_ref, o_ref):
        @pl.when(pl.program_id(1) == 0)
        def _():
            o_ref[...] = jnp.zeros_like(o_ref)

        o_ref[...] += _tn(a_ref[...].astype(BF16), b_ref[...].astype(BF16))

    return _call(
        body, name="matmul_tn", grid=(N // tn, S // tk),
        in_specs=[pl.BlockSpec((tk, M), lambda j, k: (k, 0)),
                  pl.BlockSpec((tk, tn), lambda j, k: (k, j))],
        out_specs=pl.BlockSpec((M, tn), lambda j, k: (0, j)),
        out_shape=jax.ShapeDtypeStruct((M, N), F32),
        compiler_params=_params(2))(a, b)


def _sgu_block(vv, g, b, ws_ref, lane):
    mu = jnp.mean(vv, axis=-1, keepdims=True)
    xc = vv - mu
    r = lax.rsqrt(jnp.mean(xc * xc, axis=-1, keepdims=True) + EPS)
    xhat = xc * r
    vln = (xhat * g + b).astype(BF16)
    pieces = []
    for p in range(4):
        vp = vln[:, p * 128:(p + 1) * 128]
        pieces.append(jnp.where(lane < 64, _nn(ws_ref[2 * p], vp), _nn(ws_ref[2 * p + 1], vp)))
    return xhat, r, vln, jnp.concatenate(pieces, axis=1)


def sgu_fwd(proj, ln_g, ln_b, ws, bias_full):
    S = proj.shape[0]
    ts = ROW_TILE

    def body(u_ref, v_ref, z_ref, g_ref, b_ref, ws_ref, bf_ref, y_ref):
        lane = lax.broadcasted_iota(jnp.int32, (SGU_BLOCK, 128), 1)
        for blk in range(ts // SGU_BLOCK):
            rows = slice(blk * SGU_BLOCK, (blk + 1) * SGU_BLOCK)
            _, _, _, mixed = _sgu_block(v_ref[rows, :], g_ref[...], b_ref[...], ws_ref, lane)
            mixed = mixed + bf_ref[...]
            zz = z_ref[rows, :]
            y_ref[rows, :] = (u_ref[rows, :] * mixed * (zz * _sigmoid(zz))).astype(BF16)

    col = lambda c: pl.BlockSpec((ts, BR_WIDTH), lambda i: (i, c))
    full = lambda shape: pl.BlockSpec(shape, lambda i: (0,) * len(shape))
    return _call(
        body, name="sgu_fwd", grid=(S // ts,),
        in_specs=[col(P_U // 512), col(P_V // 512), col(P_ZA // 512),
                  full((1, BR_WIDTH)), full((1, BR_WIDTH)), full((8, 128, 128)), full((128, BR_WIDTH))],
        out_specs=pl.BlockSpec((ts, BR_WIDTH), lambda i: (i, 0)),
        out_shape=jax.ShapeDtypeStruct((S, BR_WIDTH), BF16),
        compiler_params=_params(1))(proj, proj, proj, ln_g, ln_b, ws, bias_full)


def sgu_bwd(dproj, dy, proj, ln_g, ln_b, ws, ws_t, bias_full):
    S = proj.shape[0]
    ts = ROW_TILE

    def body(dp_in, dy_ref, u_ref, v_ref, z_ref, g_ref, b_ref, ws_ref, wst_ref, bf_ref,
             dp_ref, gg_ref, gb_ref, gws_ref, gbf_ref):
        del dp_in
        first = pl.program_id(0) == 0

        @pl.when(first)
        def _():
            gg_ref[...] = jnp.zeros_like(gg_ref)
            gb_ref[...] = jnp.zeros_like(gb_ref)
            gws_ref[...] = jnp.zeros_like(gws_ref)
            gbf_ref[...] = jnp.zeros_like(gbf_ref)

        lane = lax.broadcasted_iota(jnp.int32, (SGU_BLOCK, 128), 1)
        for blk in range(ts // SGU_BLOCK):
            rows = slice(blk * SGU_BLOCK, (blk + 1) * SGU_BLOCK)
            g = g_ref[...]
            xhat, r, vln, mixed = _sgu_block(v_ref[rows, :], g, b_ref[...], ws_ref, lane)
            mixed = mixed + bf_ref[...]
            zz = z_ref[rows, :]
            uu = u_ref[rows, :]
            dyv = dy_ref[0, rows, :]
            sg = _sigmoid(zz)
            sil = zz * sg
            dmixed = dyv * uu * sil
            dp_ref[rows, 0:512] = dyv * mixed * sil
            dp_ref[rows, 1024:1536] = dyv * uu * mixed * (sg * (1.0 + zz * (1.0 - sg)))
            gbf_ref[...] += dmixed
            dmb = dmixed.astype(BF16)
            pieces = []
            for p in range(4):
                dmp = dmb[:, p * 128:(p + 1) * 128]
                vp = vln[:, p * 128:(p + 1) * 128]
                pieces.append(jnp.where(lane < 64, _nn(wst_ref[2 * p], dmp), _nn(wst_ref[2 * p + 1], dmp)))
                zero = jnp.zeros_like(dmp)
                gws_ref[2 * p] += _nt(jnp.where(lane < 64, dmp, zero), vp)
                gws_ref[2 * p + 1] += _nt(jnp.where(lane >= 64, dmp, zero), vp)
            dvln = jnp.concatenate(pieces, axis=1)
            dxh = dvln * g
            dp_ref[rows, 512:1024] = r * (dxh - jnp.mean(dxh, axis=-1, keepdims=True)
                                          - xhat * jnp.mean(dxh * xhat, axis=-1, keepdims=True))
            gg_ref[...] += jnp.sum(dvln * xhat, axis=0, keepdims=True)
            gb_ref[...] += jnp.sum(dvln, axis=0, keepdims=True)

    col = lambda c: pl.BlockSpec((ts, BR_WIDTH), lambda i: (i, c))
    full = lambda shape: pl.BlockSpec(shape, lambda i: (0,) * len(shape))
    return _call(
        body, name="sgu_bwd", grid=(S // ts,),
        in_specs=[pl.BlockSpec(memory_space=pl.ANY),
                  pl.BlockSpec((1, ts, BR_WIDTH), lambda i: (0, i, 0)),
                  col(P_U // 512), col(P_V // 512), col(P_ZA // 512),
                  full((1, BR_WIDTH)), full((1, BR_WIDTH)), full((8, 128, 128)), full((8, 128, 128)),
                  full((128, BR_WIDTH))],
        out_specs=[pl.BlockSpec((ts, 1536), lambda i: (i, P_U // 1536)),
                   full((1, BR_WIDTH)), full((1, BR_WIDTH)), full((8, 128, 128)), full((128, BR_WIDTH))],
        out_shape=[jax.ShapeDtypeStruct(dproj.shape, F32),
                   jax.ShapeDtypeStruct((1, BR_WIDTH), F32), jax.ShapeDtypeStruct((1, BR_WIDTH), F32),
                   jax.ShapeDtypeStruct((8, 128, 128), F32), jax.ShapeDtypeStruct((128, BR_WIDTH), F32)],
        input_output_aliases={0: 0},
        compiler_params=_params(1))(dproj, dy, proj, proj, proj, ln_g, ln_b, ws, ws_t, bias_full)


def _hspec(ts):
    return pl.BlockSpec((N_HEADS, ts, HEAD_PAD), lambda i: (0, i, 0))


def prep_fwd(proj, tab, qg, kvg, wq, wk, wv):
    S = proj.shape[0]
    ts = ROW_TILE

    def body(qc_ref, kc_ref, vc_ref, qd_ref, kvd_ref, kr_ref, tab_ref, qg_ref, kvg_ref,
             wq_ref, wk_ref, wv_ref, qb, kb, vb, qc, kc, vc, cq_o, ckv_o):
        c, a, b = tab_ref[0], tab_ref[1], tab_ref[2]
        cq, _ = _rms(qd_ref[...], qg_ref[...])
        ckv, _ = _rms(kvd_ref[...], kvg_ref[...])
        cqb, ckvb = cq.astype(BF16), ckv.astype(BF16)
        cq_o[...] = cqb
        ckv_o[...] = ckvb
        krr = _rope(kr_ref[...], c, a, b)
        for h in range(N_HEADS):
            cols = slice(h * HEAD_PAD, (h + 1) * HEAD_PAD)
            qb[h] = (_rope(_nn(cqb, wq_ref[:, cols]), c, a, b) * (MLA_SCALE * LOG2E)).astype(BF16)
            kb[h] = (_nn(ckvb, wk_ref[:, cols]) + krr).astype(BF16)
            vb[h] = _nn(ckvb, wv_ref[:, cols]).astype(BF16)
        lane = lax.broadcasted_iota(jnp.int32, (ts, 128), 1)
        for src, dst, factor in ((qc_ref, qc, CA_SCALE * LOG2E), (kc_ref, kc, 1.0), (vc_ref, vc, 1.0)):
            for p in range(4):
                piece = src[:, p * 128:(p + 1) * 128] * factor
                dst[2 * p] = jnp.where(lane < 64, piece, 0.0).astype(BF16)
                dst[2 * p + 1] = jnp.where(lane < 64, pltpu.roll(piece, 64, 1), 0.0).astype(BF16)

    col = lambda w, c: pl.BlockSpec((ts, w), lambda i: (i, c))
    full = lambda shape: pl.BlockSpec(shape, lambda i: (0,) * len(shape))
    hshape = jax.ShapeDtypeStruct((N_HEADS, S, HEAD_PAD), BF16)
    return _call(
        body, name="prep_fwd", grid=(S // ts,),
        in_specs=[col(512, P_QC // 512), col(512, P_KC // 512), col(512, P_VC // 512),
                  col(256, P_QD // 256), col(128, P_KVD // 128), col(128, P_KR // 128),
                  pl.BlockSpec((3, ts, 128), lambda i: (0, i, 0)),
                  full((1, MLA_Q_RANK)), full((1, MLA_KV_RANK)),
                  full((MLA_Q_RANK, 1024)), full((MLA_KV_RANK, 1024)), full((MLA_KV_RANK, 1024))],
        out_specs=[_hspec(ts)] * 6 + [pl.BlockSpec((ts, MLA_Q_RANK), lambda i: (i, 0)),
                                      pl.BlockSpec((ts, MLA_KV_RANK), lambda i: (i, 0))],
        out_shape=[hshape] * 6 + [jax.ShapeDtypeStruct((S, MLA_Q_RANK), BF16),
                                  jax.ShapeDtypeStruct((S, MLA_KV_RANK), BF16)],
        compiler_params=_params(1))(proj, proj, proj, proj, proj, proj, tab, qg, kvg, wq, wk, wv)


def prep_bwd(dproj, dqb, dkb, dvb, dqc, dkc, dvc, proj, tab, qg, kvg, wq, wk, wv):
    S = proj.shape[0]
    ts = ROW_TILE

    def body(dp_in, dqb_r, dkb_r, dvb_r, dqc_r, dkc_r, dvc_r, qd_ref, kvd_ref, tab_ref, qg_ref, kvg_ref,
             wq_ref, wk_ref, wv_ref, dp_ref, dqf, dkf, dvf, gq_ref, gkv_ref):
        del dp_in
        c, a, b = tab_ref[0], -tab_ref[1], -tab_ref[2]
        qd, kvd = qd_ref[...], kvd_ref[...]
        _, rq = _rms(qd, qg_ref[...])
        _, rkv = _rms(kvd, kvg_ref[...])
        dcq = jnp.zeros((ts, MLA_Q_RANK), F32)
        dckv = jnp.zeros((ts, MLA_KV_RANK), F32)
        dksum = jnp.zeros((ts, HEAD_PAD), F32)
        for h in range(N_HEADS):
            cols = slice(h * HEAD_PAD, (h + 1) * HEAD_PAD)
            dqh = _rope(dqb_r[h] * MLA_SCALE, c, a, b).astype(BF16)
            dqf[:, cols] = dqh
            dcq = dcq + _nt(dqh, wq_ref[:, cols])
            dk = dkb_r[h] * (1.0 / LOG2E)
            dksum = dksum + dk
            dkh = dk.astype(BF16)
            dkf[:, cols] = dkh
            dvh = dvb_r[h].astype(BF16)
            dvf[:, cols] = dvh
            dckv = dckv + _nt(dkh, wk_ref[:, cols]) + _nt(dvh, wv_ref[:, cols])
        lane = lax.broadcasted_iota(jnp.int32, (ts, 128), 1)
        rope_lanes = jnp.logical_and(lane >= MLA_NOPE, lane < MLA_QK)
        dp_ref[:, P_KR:P_KR + 128] = jnp.where(rope_lanes, _rope(dksum, c, a, b), 0.0)
        dqd, gq = _rms_bwd(qd, qg_ref[...], rq, dcq)
        dkvd, gkv = _rms_bwd(kvd, kvg_ref[...], rkv, dckv)
        dp_ref[:, P_QD:P_QD + 256] = dqd
        dp_ref[:, P_KVD:P_KVD + 128] = dkvd
        first = pl.program_id(0) == 0
        _accumulate(gq_ref, gq, first)
        _accumulate(gkv_ref, gkv, first)
        for src, base, factor in ((dqc_r, P_QC, CA_SCALE), (dkc_r, P_KC, 1.0 / LOG2E), (dvc_r, P_VC, 1.0)):
            for p in range(4):
                dp_ref[:, base + p * 128:base + (p + 1) * 128] = (
                    src[2 * p] + pltpu.roll(src[2 * p + 1], 64, 1)) * factor

    col = lambda w, c: pl.BlockSpec((ts, w), lambda i: (i, c))
    full = lambda shape: pl.BlockSpec(shape, lambda i: (0,) * len(shape))
    wide = jax.ShapeDtypeStruct((S, 1024), BF16)
    return _call(
        body, name="prep_bwd", grid=(S // ts,),
        in_specs=[pl.BlockSpec(memory_space=pl.ANY)] + [_hspec(ts)] * 6 +
                 [col(256, P_QD // 256), col(128, P_KVD // 128),
                  pl.BlockSpec((3, ts, 128), lambda i: (0, i, 0)),
                  full((1, MLA_Q_RANK)), full((1, MLA_KV_RANK)),
                  full((MLA_Q_RANK, 1024)), full((MLA_KV_RANK, 1024)), full((MLA_KV_RANK, 1024))],
        out_specs=[pl.BlockSpec((ts, 2048), lambda i: (i, 0))] + [pl.BlockSpec((ts, 1024), lambda i: (i, 0))] * 3 +
                  [full((1, MLA_Q_RANK)), full((1, MLA_KV_RANK))],
        out_shape=[jax.ShapeDtypeStruct(dproj.shape, F32), wide, wide, wide,
                   jax.ShapeDtypeStruct((1, MLA_Q_RANK), F32), jax.ShapeDtypeStruct((1, MLA_KV_RANK), F32)],
        input_output_aliases={0: 0},
        compiler_params=_params(1))(dproj, dqb, dkb, dvb, dqc, dkc, dvc, proj, proj, tab, qg, kvg, wq, wk, wv)


def _diag_visible(t):
    r = lax.broadcasted_iota(jnp.int32, (t, t), 0) >> CHUNK_SHIFT
    c = lax.broadcasted_iota(jnp.int32, (t, t), 1) >> CHUNK_SHIFT
    return r <= c


def _pair_tables(nq, kv_major):
    if kv_major:
        pairs = [(kb, qi) for kb in range(nq) for qi in range(kb, nq)]
    else:
        pairs = [(kb, qi) for qi in range(nq) for kb in range(qi + 1)]
    return (jnp.asarray(np.array([p[0] for p in pairs], np.int32)),
            jnp.asarray(np.array([p[1] for p in pairs], np.int32)), len(pairs))


def _split_refs(refs, counts):
    out, pos = [], 0
    for c in counts:
        out.append(refs[pos:pos + c])
        pos += c
    return out


def mla_fwd(q, k, vt, exchange=()):
    H, S, _ = q.shape
    t, hb, n_ex = ATT_T, MLA_HEADS_PER_STEP, len(exchange)
    kb_tab, qi_tab, n_pairs = _pair_tables(S // t, False)

    def body(kb_ref, qi_ref, q_ref, k_ref, vt_ref, *rest):
        ex_src, (o_ref, lse_ref), ex_out, (m_s, l_s, acc_s), ex_sems = _split_refs(rest, (n_ex, 2, n_ex, 3, 3 if n_ex else 0))
        hg, p_id = pl.program_id(0), pl.program_id(1)
        kb, qi = kb_ref[p_id], qi_ref[p_id]

        if n_ex:
            @pl.when(jnp.logical_and(hg == 0, p_id == 0))
            def _():
                _exchange_issue(ex_src, ex_out, ex_sems, False, True)

        @pl.when(kb == 0)
        def _():
            m_s[...] = jnp.full_like(m_s, NEG_INF)
            l_s[...] = jnp.zeros_like(l_s)
            acc_s[...] = jnp.zeros_like(acc_s)

        def step(masked):
            for h in range(hb):
                st = _nt(k_ref[h], q_ref[h])
                if masked:
                    st = jnp.where(_diag_visible(t), st, NEG_INF)
                m_prev = m_s[h]
                m_new = jnp.maximum(m_prev, jnp.max(st, axis=0, keepdims=True))
                alpha = jnp.exp2(m_prev - m_new)
                p = jnp.exp2(st - m_new)
                l_s[h] = alpha * l_s[h] + jnp.sum(p, axis=0, keepdims=True)
                acc_s[h] = alpha * acc_s[h] + _nn(vt_ref[h], p.astype(BF16))
                m_s[h] = m_new

        @pl.when(kb < qi)
        def _():
            step(False)

        @pl.when(kb == qi)
        def _():
            step(True)
            for h in range(hb):
                o_ref[h] = (acc_s[h] / l_s[h]).T
                lse_ref[h] = m_s[h] + jnp.log2(l_s[h])

        if n_ex:
            @pl.when(jnp.logical_and(hg == H // hb - 1, p_id == n_pairs - 1))
            def _():
                _exchange_issue(ex_src, ex_out, ex_sems, False, False)

    grid_spec = pltpu.PrefetchScalarGridSpec(
        num_scalar_prefetch=2, grid=(H // hb, n_pairs),
        in_specs=[pl.BlockSpec((hb, t, HEAD_PAD), lambda h, p, kb, qi: (h, qi[p], 0)),
                  pl.BlockSpec((hb, t, HEAD_PAD), lambda h, p, kb, qi: (h, kb[p], 0)),
                  pl.BlockSpec((hb, HEAD_PAD, t), lambda h, p, kb, qi: (h, 0, kb[p]))] + [ANY] * n_ex,
        out_specs=[pl.BlockSpec((hb, t, HEAD_PAD), lambda h, p, kb, qi: (h, qi[p], 0)),
                   pl.BlockSpec((hb, 1, t), lambda h, p, kb, qi: (h, 0, qi[p]))] + [ANY] * n_ex,
        scratch_shapes=[pltpu.VMEM((hb, 1, t), F32), pltpu.VMEM((hb, 1, t), F32),
                        pltpu.VMEM((hb, HEAD_PAD, t), F32)] + _exchange_sems(n_ex))
    outs = _call(
        body, name="mla_fwd_gather" if n_ex else "mla_fwd", grid_spec=grid_spec,
        out_shape=[jax.ShapeDtypeStruct((H, S, HEAD_PAD), F32), jax.ShapeDtypeStruct((H, 1, S), F32)] +
        _exchange_out_shape(exchange, False),
        compiler_params=_params(2))(kb_tab, qi_tab, q, k, vt, *exchange)
    return outs[0], outs[1], outs[2:]


def mla_bwd(q, k, kt, v, do, lse, delta, exchange=()):
    H, S, _ = q.shape
    t, hb, n_ex = ATT_T, MLA_HEADS_PER_STEP, len(exchange)
    nq = S // t
    kb_tab, qi_tab, n_pairs = _pair_tables(nq, True)

    def body(kb_ref, qi_ref, q_ref, k_ref, kt_ref, v_ref, do_ref, lse_ref, dl_ref, *rest):
        ex_src, (dqt_ref, dk_ref, dv_ref), ex_out, (dk_s, dv_s), ex_sems = _split_refs(rest, (n_ex, 3, n_ex, 2, 3 if n_ex else 0))
        hg, p_id = pl.program_id(0), pl.program_id(1)
        kb, qi = kb_ref[p_id], qi_ref[p_id]

        if n_ex:
            @pl.when(jnp.logical_and(hg == 0, p_id == 0))
            def _():
                _exchange_issue(ex_src, ex_out, ex_sems, True, True)

        @pl.when(p_id == 0)
        def _():
            dqt_ref[...] = jnp.zeros_like(dqt_ref)

        @pl.when(qi == kb)
        def _():
            dk_s[...] = jnp.zeros_like(dk_s)
            dv_s[...] = jnp.zeros_like(dv_s)

        def step(masked):
            for h in range(hb):
                st = _nt(k_ref[h], q_ref[h])
                if masked:
                    st = jnp.where(_diag_visible(t), st, NEG_INF)
                pt = jnp.exp2(st - lse_ref[h])
                dv_s[h] += _nn(pt.astype(BF16), do_ref[h])
                dsb = (pt * (_nt(v_ref[h], do_ref[h]) - dl_ref[h])).astype(BF16)
                dk_s[h] += _nn(dsb, q_ref[h])
                dqt_ref[h, qi] += _nn(kt_ref[h], dsb)

        @pl.when(qi == kb)
        def _():
            step(True)

        @pl.when(qi > kb)
        def _():
            step(False)

        @pl.when(qi == nq - 1)
        def _():
            dk_ref[...] = dk_s[...]
            dv_ref[...] = dv_s[...]

        if n_ex:
            @pl.when(jnp.logical_and(hg == H // hb - 1, p_id == n_pairs - 1))
            def _():
                _exchange_issue(ex_src, ex_out, ex_sems, True, False)

    qtile = pl.BlockSpec((hb, t, HEAD_PAD), lambda h, p, kb, qi: (h, qi[p], 0))
    ktile = pl.BlockSpec((hb, t, HEAD_PAD), lambda h, p, kb, qi: (h, kb[p], 0))
    stat = pl.BlockSpec((hb, 1, t), lambda h, p, kb, qi: (h, 0, qi[p]))
    grid_spec = pltpu.PrefetchScalarGridSpec(
        num_scalar_prefetch=2, grid=(H // hb, n_pairs),
        in_specs=[qtile, ktile, pl.BlockSpec((hb, HEAD_PAD, t), lambda h, p, kb, qi: (h, 0, kb[p])), ktile, qtile,
                  stat, stat] + [ANY] * n_ex,
        out_specs=[pl.BlockSpec((hb, nq, HEAD_PAD, t), lambda h, p, kb, qi: (h, 0, 0, 0)), ktile, ktile] +
        [ANY] * n_ex,
        scratch_shapes=[pltpu.VMEM((hb, t, HEAD_PAD), F32), pltpu.VMEM((hb, t, HEAD_PAD), F32)] +
        _exchange_sems(n_ex))
    outs = _call(
        body, name="mla_bwd_scatter" if n_ex else "mla_bwd", grid_spec=grid_spec,
        out_shape=[jax.ShapeDtypeStruct((H, nq, HEAD_PAD, t), F32),
                   jax.ShapeDtypeStruct((H, S, HEAD_PAD), F32), jax.ShapeDtypeStruct((H, S, HEAD_PAD), F32)] +
        _exchange_out_shape(exchange, True),
        compiler_params=_params(2))(kb_tab, qi_tab, q, k, kt, v, do, lse, delta, *exchange)
    return outs[0], outs[1], outs[2], outs[3:]


def _band_specs(t):
    prev = lambda i: jnp.maximum(i - 1, 0)
    return dict(
        cur=pl.BlockSpec((1, t, HEAD_PAD), lambda h, i: (h, i, 0)),
        prev=pl.BlockSpec((1, t, HEAD_PAD), lambda h, i: (h, prev(i), 0)),
        cur_t=pl.BlockSpec((1, HEAD_PAD, t), lambda h, i: (h, 0, i)),
        prev_t=pl.BlockSpec((1, HEAD_PAD, t), lambda h, i: (h, 0, prev(i))),
        stat=pl.BlockSpec((1, 1, t), lambda h, i: (h, 0, i)),
        bias_prev=pl.BlockSpec((1, 1, t, t), lambda h, i: (h, jnp.where(i == 0, 1, 0), 0, 0)),
        bias_cur=pl.BlockSpec((1, 1, t, t), lambda h, i: (h, 2, 0, 0)))


def band_fwd(q, k, vt, bias):
    H, S, _ = q.shape
    t = ATT_T
    sp = _band_specs(t)

    def body(q_ref, kp_ref, kc_ref, vtp_ref, vtc_ref, bp_ref, bc_ref, o_ref, lse_ref):
        s0 = _nt(kp_ref[0], q_ref[0]) + bp_ref[0, 0]
        s1 = _nt(kc_ref[0], q_ref[0]) + bc_ref[0, 0]
        m = jnp.maximum(jnp.max(s0, axis=0, keepdims=True), jnp.max(s1, axis=0, keepdims=True))
        p0 = jnp.exp2(s0 - m)
        p1 = jnp.exp2(s1 - m)
        l = jnp.sum(p0, axis=0, keepdims=True) + jnp.sum(p1, axis=0, keepdims=True)
        ot = _nn(vtp_ref[0], p0.astype(BF16)) + _nn(vtc_ref[0], p1.astype(BF16))
        o_ref[0] = (ot / l).T
        lse_ref[0] = m + jnp.log2(l)

    return _call(
        body, name="band_fwd", grid=(H, S // t),
        in_specs=[sp['cur'], sp['prev'], sp['cur'], sp['prev_t'], sp['cur_t'], sp['bias_prev'], sp['bias_cur']],
        out_specs=[sp['cur'], sp['stat']],
        out_shape=[jax.ShapeDtypeStruct((H, S, HEAD_PAD), F32), jax.ShapeDtypeStruct((H, 1, S), F32)],
        compiler_params=_params(2))(q, k, k, vt, vt, bias, bias)


def band_bwd(q, k, kt, v, do, lse, delta, bias):
    H, S, _ = q.shape
    t = ATT_T
    sp = _band_specs(t)

    def body(q_ref, kp_ref, kc_ref, ktp_ref, ktc_ref, vp_ref, vc_ref, do_ref, lse_ref, dl_ref, bp_ref, bc_ref,
             dqt_ref, dk_ref, dv_ref, db_ref):
        i = pl.program_id(1)

        @pl.when(i == 0)
        def _():
            dk_ref[...] = jnp.zeros_like(dk_ref)
            dv_ref[...] = jnp.zeros_like(dv_ref)
            db_ref[...] = jnp.zeros_like(db_ref)

        qv, dov = q_ref[0], do_ref[0]
        dqt = jnp.zeros((HEAD_PAD, t), F32)
        windows = ((0, jnp.maximum(i - 1, 0), kp_ref, ktp_ref, vp_ref, bp_ref),
                   (1, i, kc_ref, ktc_ref, vc_ref, bc_ref))
        for w, blk, k_ref, kt_ref, v_ref, b_ref in windows:
            rows = pl.ds(pl.multiple_of(blk * t, t), t)
            pt = jnp.exp2(_nt(k_ref[0], qv) + b_ref[0, 0] - lse_ref[0])
            dv_ref[0, rows, :] += _nn(pt.astype(BF16), dov)
            ds = pt * (_nt(v_ref[0], dov) - dl_ref[0])
            db_ref[0, w] += ds
            dsb = ds.astype(BF16)
            dk_ref[0, rows, :] += _nn(dsb, qv)
            dqt = dqt + _nn(kt_ref[0], dsb)
        dqt_ref[0] = dqt

    whole = pl.BlockSpec((1, S, HEAD_PAD), lambda h, i: (h, 0, 0))
    return _call(
        body, name="band_bwd", grid=(H, S // t),
        in_specs=[sp['cur'], sp['prev'], sp['cur'], sp['prev_t'], sp['cur_t'], sp['prev'], sp['cur'], sp['cur'],
                  sp['stat'], sp['stat'], sp['bias_prev'], sp['bias_cur']],
        out_specs=[sp['cur_t'], whole, whole, pl.BlockSpec((1, 2, t, t), lambda h, i: (h, 0, 0, 0))],
        out_shape=[jax.ShapeDtypeStruct((H, HEAD_PAD, S), F32), jax.ShapeDtypeStruct((H, S, HEAD_PAD), F32),
                   jax.ShapeDtypeStruct((H, S, HEAD_PAD), F32), jax.ShapeDtypeStruct((H, 2, t, t), F32)],
        compiler_params=_params(2))(q, k, k, kt, kt, v, v, do, lse, delta, bias, bias)


def _compact(o_ref):
    return jnp.concatenate([o_ref[2 * p] + pltpu.roll(o_ref[2 * p + 1], 64, 1) for p in range(4)], axis=1)


def merge_fwd(ob, oc, proj, ya, gate_b, wbr, w_out, x, post_g):
    S = x.shape[0]
    ts = ROW_TILE

    def body(ob_ref, oc_ref, zb_ref, zc_ref, ya_ref, gl_ref, gb_ref, wbr_ref, wo_ref, x_ref, pg_ref,
             xo_ref, yb_ref, yc_ref, mg_ref, out_ref):
        zb, zc = zb_ref[...], zc_ref[...]
        yb = (_compact(ob_ref) * (zb * _sigmoid(zb))).astype(BF16)
        yc = (_compact(oc_ref) * (zc * _sigmoid(zc))).astype(BF16)
        yb_ref[...] = yb
        yc_ref[...] = yc
        merged = jnp.zeros((ts, D_MODEL), F32)
        for n, y in enumerate((ya_ref[...], yb, yc)):
            cols = slice(n * D_MODEL, (n + 1) * D_MODEL)
            gate = _sigmoid(gl_ref[:, cols] + gb_ref[:, cols])
            merged = merged + gate * _nn(y, wbr_ref[n])
        mb = merged.astype(BF16)
        mg_ref[...] = mb
        out = _nn(mb, wo_ref[...])
        out_ref[...] = out
        normed, _ = _rms(out, pg_ref[...])
        xo_ref[...] = x_ref[...] + normed

    row = lambda w: pl.BlockSpec((ts, w), lambda i: (i, 0))
    col = lambda w, c: pl.BlockSpec((ts, w), lambda i: (i, c))
    full = lambda shape: pl.BlockSpec(shape, lambda i: (0,) * len(shape))
    return _call(
        body, name="merge_fwd", grid=(S // ts,),
        in_specs=[_hspec(ts), _hspec(ts), col(512, P_ZB // 512), col(512, P_ZC // 512), row(512),
                  col(3072, P_G // 3072), full((1, 3072)), full((3, BR_WIDTH, D_MODEL)),
                  full((D_MODEL, D_MODEL)), row(D_MODEL), full((1, D_MODEL))],
        out_specs=[row(D_MODEL), row(512), row(512), row(D_MODEL), row(D_MODEL)],
        out_shape=[jax.ShapeDtypeStruct((S, D_MODEL), F32), jax.ShapeDtypeStruct((S, 512), BF16),
                   jax.ShapeDtypeStruct((S, 512), BF16), jax.ShapeDtypeStruct((S, D_MODEL), BF16),
                   jax.ShapeDtypeStruct((S, D_MODEL), F32)],
        compiler_params=_params(1))(ob, oc, proj, proj, ya, proj, gate_b, wbr, w_out, x, post_g)


def post_bwd(g, out, post_g, w_out):
    S = g.shape[0]
    ts = ROW_TILE

    def body(g_ref, out_ref, pg_ref, wo_ref, do_ref, dm_ref, gp_ref):
        ov = out_ref[...]
        _, r = _rms(ov, pg_ref[...])
        dout, gp = _rms_bwd(ov, pg_ref[...], r, g_ref[...])
        db = dout.astype(BF16)
        do_ref[...] = db
        dm_ref[...] = _nt(db, wo_ref[...])
        _accumulate(gp_ref, gp, pl.program_id(0) == 0)

    row = lambda: pl.BlockSpec((ts, D_MODEL), lambda i: (i, 0))
    full = lambda shape: pl.BlockSpec(shape, lambda i: (0,) * len(shape))
    return _call(
        body, name="post_bwd", grid=(S // ts,),
        in_specs=[row(), row(), full((1, D_MODEL)), full((D_MODEL, D_MODEL))],
        out_specs=[row(), row(), full((1, D_MODEL))],
        out_shape=[jax.ShapeDtypeStruct((S, D_MODEL), BF16), jax.ShapeDtypeStruct((S, D_MODEL), F32),
                   jax.ShapeDtypeStruct((1, D_MODEL), F32)],
        compiler_params=_params(1))(g, out, post_g, w_out)


def gate_bwd(dmerged, proj, gate_b, ya, yb, yc, wbr):
    S = dmerged.shape[0]
    ts = ROW_TILE

    def body(dm_ref, gl_ref, gb_ref, ya_ref, yb_ref, yc_ref, wbr_ref,
             dp_ref, dba_ref, dbb_ref, dbc_ref, dy_ref, ggb_ref):
        dm = dm_ref[...]
        ggb = []
        for n, (y_ref, dbr_ref) in enumerate(((ya_ref, dba_ref), (yb_ref, dbb_ref), (yc_ref, dbc_ref))):
            cols = slice(n * D_MODEL, (n + 1) * D_MODEL)
            br = _nn(y_ref[...], wbr_ref[n])
            sg = _sigmoid(gl_ref[:, cols] + gb_ref[:, cols])
            dgl = dm * br * (sg * (1.0 - sg))
            dp_ref[:, cols] = dgl
            ggb.append(jnp.sum(dgl, axis=0, keepdims=True))
            dbr = (dm * sg).astype(BF16)
            dbr_ref[...] = dbr
            dy_ref[n] = _nt(dbr, wbr_ref[n])
        _accumulate(ggb_ref, jnp.concatenate(ggb, axis=1), pl.program_id(0) == 0)

    row = lambda w: pl.BlockSpec((ts, w), lambda i: (i, 0))
    full = lambda shape: pl.BlockSpec(shape, lambda i: (0,) * len(shape))
    wide = jax.ShapeDtypeStruct((S, D_MODEL), BF16)
    return _call(
        body, name="gate_bwd", grid=(S // ts,),
        in_specs=[row(D_MODEL), pl.BlockSpec((ts, 3072), lambda i: (i, P_G // 3072)), full((1, 3072)),
                  row(512), row(512), row(512), full((3, BR_WIDTH, D_MODEL))],
        out_specs=[pl.BlockSpec((ts, 3072), lambda i: (i, P_G // 3072)), row(D_MODEL), row(D_MODEL), row(D_MODEL),
                   pl.BlockSpec((3, ts, 512), lambda i: (0, i, 0)), full((1, 3072))],
        out_shape=[jax.ShapeDtypeStruct((S, P_W), F32), wide, wide, wide,
                   jax.ShapeDtypeStruct((3, S, 512), F32), jax.ShapeDtypeStruct((1, 3072), F32)],
        compiler_params=_params(1))(dmerged, proj, gate_b, ya, yb, yc, wbr)


def ungate_bwd(dproj, dy, ob, oc, proj):
    S = proj.shape[0]
    ts = ROW_TILE

    def body(dp_in, dyb_ref, dyc_ref, ob_ref, oc_ref, zb_ref, zc_ref, dp_ref, dob_ref, doc_ref, dlb_ref, dlc_ref):
        del dp_in
        lane = lax.broadcasted_iota(jnp.int32, (ts, 128), 1)
        for n, (dy_ref, o_ref, z_ref, do_ref, dl_ref) in enumerate(
                ((dyb_ref, ob_ref, zb_ref, dob_ref, dlb_ref), (dyc_ref, oc_ref, zc_ref, doc_ref, dlc_ref))):
            zz = z_ref[...]
            dyv = dy_ref[0]
            sg = _sigmoid(zz)
            dp_ref[:, n * 512:(n + 1) * 512] = dyv * _compact(o_ref) * (sg * (1.0 + zz * (1.0 - sg)))
            do_c = dyv * (zz * sg)
            for p in range(4):
                piece = do_c[:, p * 128:(p + 1) * 128]
                for h, d in ((2 * p, jnp.where(lane < 64, piece, 0.0)),
                             (2 * p + 1, jnp.where(lane < 64, pltpu.roll(piece, 64, 1), 0.0))):
                    do_ref[h] = d.astype(BF16)
                    dl_ref[h] = jnp.sum(d * o_ref[h], axis=-1, keepdims=True)

    col = lambda c: pl.BlockSpec((ts, 512), lambda i: (i, c))
    dysp = lambda n: pl.BlockSpec((1, ts, 512), lambda i: (n, i, 0))
    stat = pl.BlockSpec((N_HEADS, ts, 1), lambda i: (0, i, 0))
    hshape = jax.ShapeDtypeStruct((N_HEADS, S, HEAD_PAD), BF16)
    sshape = jax.ShapeDtypeStruct((N_HEADS, S, 1), F32)
    return _call(
        body, name="ungate_bwd", grid=(S // ts,),
        in_specs=[pl.BlockSpec(memory_space=pl.ANY), dysp(1), dysp(2), _hspec(ts), _hspec(ts),
                  col(P_ZB // 512), col(P_ZC // 512)],
        out_specs=[pl.BlockSpec((ts, 1024), lambda i: (i, P_ZB // 1024)), _hspec(ts), _hspec(ts), stat, stat],
        out_shape=[jax.ShapeDtypeStruct(dproj.shape, F32), hshape, hshape, sshape, sshape],
        input_output_aliases={0: 0},
        compiler_params=_params(1))(dproj, dy, dy, ob, oc, proj, proj)


def loss_head(y, target):
    S, D = y.shape
    ts = ROW_TILE

    def body(y_ref, t_ref, dy_ref, sq_ref):
        d = y_ref[...] - t_ref[...]
        dy_ref[...] = d * (1.0 / D)
        _accumulate(sq_ref, jnp.sum(d * d, axis=0, keepdims=True), pl.program_id(0) == 0)

    row = pl.BlockSpec((ts, D), lambda i: (i, 0))
    return _call(
        body, name="loss_head", grid=(S // ts,), in_specs=[row, row],
        out_specs=[row, pl.BlockSpec((1, D), lambda i: (0, 0))],
        out_shape=[jax.ShapeDtypeStruct((S, D), F32), jax.ShapeDtypeStruct((1, D), F32)],
        compiler_params=_params(1))(y, target)


def _row_tile(rows, cols):
    for cand in (1024, 512, 256, 128, 64, 32, 16, 8):
        if rows % cand == 0 and cand * cols * 4 <= 1024 * 1024:
            return cand
    return rows


def adamw(w, grads, m, v):
    shape = w.shape
    cols = shape[-1]
    rows = int(np.prod(shape[:-1]))
    tr = _row_tile(rows, cols)
    n_g = len(grads)
    c1 = 1.0 - ADAM_B1 ** ADAM_STEP
    c2 = 1.0 - ADAM_B2 ** ADAM_STEP

    def body(*refs):
        w_ref, m_ref, v_ref = refs[:3]
        g_refs = refs[3:3 + n_g]
        go_ref, d_ref, mo_ref, vo_ref = refs[3 + n_g:]
        gv = g_refs[0][...]
        for g_ref in g_refs[1:]:
            gv = gv + g_ref[...]
        go_ref[...] = gv
        mn = ADAM_B1 * m_ref[...] + (1.0 - ADAM_B1) * gv
        vn = ADAM_B2 * v_ref[...] + (1.0 - ADAM_B2) * (gv * gv)
        mo_ref[...] = mn
        vo_ref[...] = vn
        d_ref[...] = -ADAM_LR * ((mn / c1) / (jnp.sqrt(vn / c2) + ADAM_EPS) + ADAM_WD * w_ref[...])

    blk = pl.BlockSpec((tr, cols), lambda i: (i, 0))
    sds = jax.ShapeDtypeStruct((rows, cols), F32)
    outs = _call(
        body, name="adamw", grid=(rows // tr,), in_specs=[blk] * (3 + n_g), out_specs=[blk] * 4,
        out_shape=[sds] * 4, compiler_params=_params(1))(
            *[a.reshape(rows, cols) for a in (w, m, v, *grads)])
    return [o.reshape(shape) for o in outs]


def add_lead(parts):
    n = parts.shape[0]
    shape = parts.shape[1:]
    cols = shape[-1]
    rows = int(np.prod(shape[:-1]))
    tr = _row_tile(rows, cols * n)

    def body(p_ref, o_ref):
        acc = p_ref[0].astype(F32)
        for s in range(1, n):
            acc = acc + p_ref[s].astype(F32)
        o_ref[...] = acc

    out = _call(
        body, name="add_lead", grid=(rows // tr,),
        in_specs=[pl.BlockSpec((n, tr, cols), lambda i: (0, i, 0))],
        out_specs=pl.BlockSpec((tr, cols), lambda i: (i, 0)),
        out_shape=jax.ShapeDtypeStruct((rows, cols), F32),
        compiler_params=_params(1))(parts.reshape(n, rows, cols))
    return out.reshape(shape)


ANY = pl.BlockSpec(memory_space=pl.ANY)


def _other_chips(x, y):
    return [(1 - x, y), (x, 1 - y), (1 - x, 1 - y)]


def chip_exchange(arrays, scatter, name):
    n = len(arrays)

    def body(*refs):
        _exchange_issue(refs[:n], refs[n:2 * n], refs[2 * n:], scatter, True)
        _exchange_issue(refs[:n], refs[n:2 * n], refs[2 * n:], scatter, False)

    return _call(
        body, name=name, in_specs=[ANY] * n, out_specs=[ANY] * n,
        out_shape=_exchange_out_shape(arrays, scatter), scratch_shapes=_exchange_sems(n))(*arrays)


def _exchange_out_shape(arrays, scatter):
    return [jax.ShapeDtypeStruct(a.shape if scatter else (4,) + a.shape, a.dtype) for a in arrays]


def _exchange_sems(n):
    if n == 0:
        return []
    return [pltpu.SemaphoreType.DMA((3 * n,)), pltpu.SemaphoreType.DMA((3 * n,)), pltpu.SemaphoreType.DMA((n,))]


def _exchange_issue(srcs, outs, sems, scatter, start):
    send_sems, recv_sems, local_sems = sems
    x, y, c = lax.axis_index("x"), lax.axis_index("y"), lax.axis_index("c")
    me = 2 * x + y
    for a in range(len(srcs)):
        local_src = srcs[a].at[me] if scatter else srcs[a]
        mine = pltpu.make_async_copy(local_src, outs[a].at[me], local_sems.at[a])
        sends = []
        for j, (px, py) in enumerate(_other_chips(x, y)):
            pair = dict(send_sem=send_sems.at[3 * a + j], recv_sem=recv_sems.at[3 * a + j],
                        device_id=(px, py, c), device_id_type=MESH)
            sends.append(pltpu.make_async_remote_copy(
                src_ref=srcs[a].at[2 * px + py] if scatter else srcs[a], dst_ref=outs[a].at[me], **pair))
            if not start:
                pltpu.make_async_remote_copy(src_ref=local_src, dst_ref=outs[a].at[2 * px + py], **pair).wait_recv()
        if start:
            mine.start()
            for cp in sends:
                cp.start()
        else:
            for cp in sends:
                cp.wait_send()
            mine.wait()


def sibling_exchange(arrays):
    n = len(arrays)

    def body(*refs):
        srcs, outs = refs[:n], refs[n:2 * n]
        send_sems, recv_sems = refs[2 * n:]
        x, y, c = lax.axis_index("x"), lax.axis_index("y"), lax.axis_index("c")
        copies = [pltpu.make_async_remote_copy(src_ref=srcs[a], dst_ref=outs[a], send_sem=send_sems.at[a],
                                               recv_sem=recv_sems.at[a], device_id=(x, y, 1 - c), device_id_type=MESH)
                  for a in range(n)]
        for cp in copies:
            cp.start()
        for cp in copies:
            cp.wait()

    return _call(
        body, name="sibling_exchange", in_specs=[ANY] * n, out_specs=[ANY] * n,
        out_shape=[jax.ShapeDtypeStruct(a.shape, a.dtype) for a in arrays],
        scratch_shapes=[pltpu.SemaphoreType.DMA((n,)), pltpu.SemaphoreType.DMA((n,))])(*arrays)


def _perm_from_shards(sh):
    rows = sh.shape[1]
    pieces, pos = [], 0
    for lo, hi, plo in sorted(NAT_SEGS, key=lambda s: s[2]):
        if plo > pos:
            pieces.append(jnp.zeros((rows, plo - pos), sh.dtype))
            pos = plo
        c = lo
        while c < hi:
            kk = c // SHARD_COLS
            e = min(hi, (kk + 1) * SHARD_COLS)
            pieces.append(sh[kk][:, c - kk * SHARD_COLS:e - kk * SHARD_COLS])
            c = e
        pos += hi - lo
    if pos < P_W:
        pieces.append(jnp.zeros((rows, P_W - pos), sh.dtype))
    return jnp.concatenate(pieces, axis=1)


def _shards_from_perm(p):
    out = []
    for kk in range(4):
        lo_k, hi_k = kk * SHARD_COLS, (kk + 1) * SHARD_COLS
        pieces = []
        for lo, hi, plo in NAT_SEGS:
            a, b = max(lo, lo_k), min(hi, hi_k)
            if a < b:
                pieces.append(p[:, plo + (a - lo):plo + (b - lo)])
        out.append(jnp.concatenate(pieces, axis=1))
    return jnp.stack(out)


def _split4(a, axis):
    shape = a.shape
    a = a.reshape(shape[:axis] + (4, shape[axis] // 4) + shape[axis + 1:])
    return jnp.moveaxis(a, axis, 0)


def _join4(a, axis):
    a = jnp.moveaxis(a, 0, axis)
    shape = a.shape
    return a.reshape(shape[:axis] + (4 * shape[axis + 1],) + shape[axis + 2:])


def _pad_heads(w, per_head, lo, hi):
    r = w.shape[0]
    wh = w.reshape(r, N_HEADS, per_head)[:, :, lo:hi]
    return jnp.pad(wh, ((0, 0), (0, 0), (0, HEAD_PAD - (hi - lo)))).reshape(r, N_HEADS * HEAD_PAD)


def _rope_table(S):
    half = MLA_ROPE // 2
    inv = ROPE_BASE ** (-jnp.arange(half, dtype=F32) / half)
    ang = jnp.arange(S).astype(F32)[:, None] * inv[None, :]
    cos, sin = jnp.cos(ang), jnp.sin(ang)
    z = lambda n: jnp.zeros((S, n), F32)
    c = jnp.concatenate([jnp.ones((S, MLA_NOPE), F32), cos, cos, z(32)], axis=1)
    a = jnp.concatenate([z(MLA_NOPE), -sin, z(48)], axis=1)
    b = jnp.concatenate([z(MLA_NOPE + half), sin, z(32)], axis=1)
    return jnp.stack([c, a, b])


def _band_onehot():
    t = ATT_T
    m = np.arange(2 * t - 1)
    d = np.where(m < t, m, m - (2 * t - 1))
    idx = np.stack([np.clip(off + d, -REL_CLIP, REL_CLIP) + REL_CLIP for off in (t, 0)])
    return (idx[:, :, None] == np.arange(2 * REL_CLIP + 1)[None, None, :]).astype(np.float32)


def _band_mask():
    t = ATT_T
    kc = (np.arange(t) >> CHUNK_SHIFT)[:, None]
    qc = (np.arange(t) >> CHUNK_SHIFT)[None, :]
    return np.stack([kc >= qc, kc <= qc])


def _bias_tiles(table):
    t = ATT_T
    vals = jnp.einsum('hr,wdr->hwd', table, jnp.asarray(_band_onehot()), precision=lax.Precision.HIGHEST)
    rowsrep = jnp.broadcast_to(vals[:, :, None, :], (N_HEADS, 2, t, 2 * t - 1)).reshape(N_HEADS, 2, -1)
    skew = rowsrep[..., :t * (2 * t - 2)].reshape(N_HEADS, 2, t, 2 * t - 2)[..., :t]
    tiles = jnp.where(jnp.asarray(_band_mask()), skew * LOG2E, NEG_INF)
    return jnp.stack([tiles[:, 0], jnp.full((N_HEADS, t, t), NEG_INF, F32), tiles[:, 1]], axis=1)


def _bias_tiles_grad(dtiles):
    t = ATT_T
    wide = jnp.pad(dtiles, ((0, 0), (0, 0), (0, 0), (0, t - 2))).reshape(N_HEADS, 2, -1)
    flat = jnp.pad(wide, ((0, 0), (0, 0), (0, t)))
    diag = jnp.sum(flat.reshape(N_HEADS, 2, t, 2 * t - 1), axis=2)
    return jnp.einsum('hwd,wdr->hr', diag, jnp.asarray(_band_onehot()), precision=lax.Precision.HIGHEST)


def _layer_consts(lw):
    tri = np.tril(np.ones((SGU_BLOCK, SGU_BLOCK), np.float32))
    ws = (lw['sgu_w'] * tri).astype(BF16)
    return dict(
        ws=ws, ws_t=jnp.swapaxes(ws, 1, 2), sgu_bias=jnp.repeat(lw['sgu_b'].T, CA_HEAD_DIM, axis=1),
        bias=_bias_tiles(lw['ca_rel_bias']),
        wq=_pad_heads(lw['mla_w_uq'], MLA_QK, 0, MLA_QK),
        wk=_pad_heads(lw['mla_w_ukv'], MLA_NOPE + MLA_V, 0, MLA_NOPE),
        wv=_pad_heads(lw['mla_w_ukv'], MLA_NOPE + MLA_V, MLA_NOPE, MLA_NOPE + MLA_V),
        gate_b=lw['gate_b'].reshape(1, 3 * D_MODEL),
        pre_g=lw['pre_g'][None], post_g=lw['post_g'][None], ln_g=lw['sgu_ln_g'][None], ln_b=lw['sgu_ln_b'][None],
        qg=lw['mla_q_norm_g'][None], kvg=lw['mla_kv_norm_g'][None])


def _layer_fwd(x, lw, k, tab, next_shards):
    proj, xn = norm_matmul(x, k['pre_g'], lw['w_in'])
    ya = sgu_fwd(proj, k['ln_g'], k['ln_b'], k['ws'], k['sgu_bias'])
    qb, kb, vb, qc, kc, vc, cq, ckv = prep_fwd(proj, tab, k['qg'], k['kvg'], k['wq'], k['wk'], k['wv'])
    tr = lambda a: jnp.swapaxes(a, 1, 2)
    ob, lse_b, gathered = mla_fwd(qb, kb, tr(vb), next_shards)
    oc, lse_c = band_fwd(qc, kc, tr(vc), k['bias'])
    x_new, yb, yc, merged, out = merge_fwd(ob, oc, proj, ya, k['gate_b'], lw['w_branch'], lw['w_out'], x,
                                           k['post_g'])
    saved = dict(x=x, proj=proj, xn=xn, ya=ya, yb=yb, yc=yc, qb=qb, kb=kb, vb=vb, qc=qc, kc=kc, vc=vc,
                 cq=cq, ckv=ckv, ob=ob, oc=oc, lse_b=lse_b, lse_c=lse_c, merged=merged, out=out)
    return x_new, saved, gathered


def _layer_bwd(g, s, lw, k, tab, pending_parts):
    S = g.shape[0]
    H = N_HEADS
    dout, dmerged, g_post = post_bwd(g, s['out'], k['post_g'], lw['w_out'])
    g_w_out = matmul_tn(s['merged'], dout, 512)
    dproj, dba, dbb, dbc, dy, g_gate_b = gate_bwd(dmerged, s['proj'], k['gate_b'], s['ya'], s['yb'], s['yc'],
                                                  lw['w_branch'])
    g_w_branch = jnp.stack([matmul_tn(y, d, 512) for y, d in ((s['ya'], dba), (s['yb'], dbb), (s['yc'], dbc))])
    dproj, dob, doc, dl_b, dl_c = ungate_bwd(dproj, dy, s['ob'], s['oc'], s['proj'])
    row = lambda a: a.reshape(H, 1, S)
    tr = lambda a: jnp.swapaxes(a, 1, 2)
    dqt, dkb, dvb, landed = mla_bwd(s['qb'], s['kb'], tr(s['kb']), s['vb'], dob, s['lse_b'], row(dl_b),
                                    pending_parts)
    dqb = jnp.swapaxes(dqt, 2, 3).reshape(H, S, HEAD_PAD)
    dqct, dkc, dvc, dbias = band_bwd(s['qc'], s['kc'], tr(s['kc']), s['vc'], doc, s['lse_c'], row(dl_c), k['bias'])
    dproj, dqf, dkf, dvf, g_qg, g_kvg = prep_bwd(dproj, dqb, dkb, dvb, tr(dqct), dkc, dvc, s['proj'], tab,
                                                 k['qg'], k['kvg'], k['wq'], k['wk'], k['wv'])
    g_wq = matmul_tn(s['cq'], dqf, 512).reshape(MLA_Q_RANK, H, HEAD_PAD)[:, :, :MLA_QK]
    g_wk = matmul_tn(s['ckv'], dkf, 512).reshape(MLA_KV_RANK, H, HEAD_PAD)[:, :, :MLA_NOPE]
    g_wv = matmul_tn(s['ckv'], dvf, 512).reshape(MLA_KV_RANK, H, HEAD_PAD)[:, :, :MLA_V]
    dproj, g_ln_g, g_ln_b, g_ws, g_sgu_bias = sgu_bwd(dproj, dy, s['proj'], k['ln_g'], k['ln_b'], k['ws'],
                                                      k['ws_t'], k['sgu_bias'])
    g_w_in = matmul_tn(s['xn'], dproj, MM_TN)
    dx, g_pre = proj_bwd_x(dproj, lw['w_in'], s['x'], k['pre_g'], g)
    tri = np.tril(np.ones((SGU_BLOCK, SGU_BLOCK), np.float32))
    grads = dict(
        w_in=g_w_in, pre_g=g_pre[0], post_g=g_post[0], sgu_ln_g=g_ln_g[0], sgu_ln_b=g_ln_b[0],
        sgu_w=g_ws * tri, sgu_b=jnp.sum(g_sgu_bias.reshape(SGU_BLOCK, 8, CA_HEAD_DIM), axis=2).T,
        mla_q_norm_g=g_qg[0], mla_kv_norm_g=g_kvg[0],
        mla_w_uq=g_wq.reshape(MLA_Q_RANK, H * MLA_QK),
        mla_w_ukv=jnp.concatenate([g_wk, g_wv], axis=2).reshape(MLA_KV_RANK, H * (MLA_NOPE + MLA_V)),
        ca_rel_bias=_bias_tiles_grad(dbias), w_branch=g_w_branch,
        gate_b=g_gate_b.reshape(N_BRANCH, D_MODEL), w_out=g_w_out)
    return dx, grads, landed


BF16_PARTS = ('w_in', 'mla_w_uq', 'mla_w_ukv', 'w_branch', 'w_out')


def _weight_shards(w, l):
    return [w[n][l].astype(BF16) if n in BF16_PARTS else w[n][l] for n in SHARDED]


def _full_weights(gathered, small):
    lw = {n: _join4(a, SHARD_AXIS[n]) for n, a in zip(SHARDED, gathered) if n != 'w_in'}
    lw['w_in'] = _perm_from_shards(gathered[0])
    lw.update(small)
    return lw


def _small_pack(grads):
    flat = jnp.concatenate([grads[n].reshape(-1) for n in SMALL])
    quarter = -(-flat.size // (4 * 1024)) * 1024
    return jnp.pad(flat, (0, 4 * quarter - flat.size)).reshape(4, quarter // 128, 128)


def _grad_parts(grads):
    parts = [_shards_from_perm(grads['w_in'])]
    parts += [_split4(grads[n], SHARD_AXIS[n]) for n in SHARDED if n != 'w_in']
    parts = [p.astype(BF16) if n in BF16_PARTS else p for n, p in zip(SHARDED, parts)]
    return parts + [_small_pack(grads)]


def _sum_landed(landed):
    mine = [add_lead(p) for p in landed]
    return mine, sibling_exchange(mine)


def train_step_local(x, target, w):
    S = x.shape[0]
    depth = w['w_in'].shape[0]
    tab = _rope_table(S)
    gathered = chip_exchange(_weight_shards(w, 0), False, "gather_weights")
    layer_w, consts, saved = [], [], []
    for l in range(depth):
        lw = _full_weights(gathered, {n: w[n][l] for n in SMALL})
        k = _layer_consts(lw)
        x, s, gathered = _layer_fwd(x, lw, k, tab, _weight_shards(w, l + 1) if l + 1 < depth else ())
        layer_w.append(lw)
        consts.append(k)
        saved.append(s)
    g, sq = loss_head(x, target)
    reduced = [None] * depth
    pending = ()
    for l in reversed(range(depth)):
        g, grads, landed = _layer_bwd(g, saved[l], layer_w[l], consts[l], tab, pending)
        if pending:
            reduced[l + 1] = _sum_landed(landed)
        pending = _grad_parts(grads)
    reduced[0] = _sum_landed(chip_exchange(pending, True, "scatter_grads"))
    return sq, g, reduced


def kernel(x, w_in, pre_g, post_g, sgu_ln_g, sgu_ln_b, sgu_w, sgu_b, mla_q_norm_g, mla_kv_norm_g, mla_w_uq, mla_w_ukv, ca_rel_bias, w_branch, gate_b, w_out, loss_target, m_w_in, m_pre_g, m_post_g, m_sgu_ln_g, m_sgu_ln_b, m_sgu_w, m_sgu_b, m_mla_q_norm_g, m_mla_kv_norm_g, m_mla_w_uq, m_mla_w_ukv, m_ca_rel_bias, m_w_branch, m_gate_b, m_w_out, v_w_in, v_pre_g, v_post_g, v_sgu_ln_g, v_sgu_ln_b, v_sgu_w, v_sgu_b, v_mla_q_norm_g, v_mla_kv_norm_g, v_mla_w_uq, v_mla_w_ukv, v_ca_rel_bias, v_w_branch, v_gate_b, v_w_out):
    w = dict(w_in=w_in, pre_g=pre_g, post_g=post_g, sgu_ln_g=sgu_ln_g, sgu_ln_b=sgu_ln_b, sgu_w=sgu_w, sgu_b=sgu_b,
             mla_q_norm_g=mla_q_norm_g, mla_kv_norm_g=mla_kv_norm_g, mla_w_uq=mla_w_uq, mla_w_ukv=mla_w_ukv,
             ca_rel_bias=ca_rel_bias, w_branch=w_branch, gate_b=gate_b, w_out=w_out)
    m = dict(w_in=m_w_in, pre_g=m_pre_g, post_g=m_post_g, sgu_ln_g=m_sgu_ln_g, sgu_ln_b=m_sgu_ln_b, sgu_w=m_sgu_w,
             sgu_b=m_sgu_b, mla_q_norm_g=m_mla_q_norm_g, mla_kv_norm_g=m_mla_kv_norm_g, mla_w_uq=m_mla_w_uq,
             mla_w_ukv=m_mla_w_ukv, ca_rel_bias=m_ca_rel_bias, w_branch=m_w_branch, gate_b=m_gate_b, w_out=m_w_out)
    v = dict(w_in=v_w_in, pre_g=v_pre_g, post_g=v_post_g, sgu_ln_g=v_sgu_ln_g, sgu_ln_b=v_sgu_ln_b, sgu_w=v_sgu_w,
             sgu_b=v_sgu_b, mla_q_norm_g=v_mla_q_norm_g, mla_kv_norm_g=v_mla_kv_norm_g, mla_w_uq=v_mla_w_uq,
             mla_w_ukv=v_mla_w_ukv, ca_rel_bias=v_ca_rel_bias, w_branch=v_w_branch, gate_b=v_gate_b, w_out=v_w_out)
    depth = w_in.shape[0]
    sq, grad_x, reduced = train_step_local(x[0], loss_target[0], w)
    loss = lax.psum(0.5 * jnp.sum(sq) / D_MODEL, ("x", "y", "c"))

    out = {}
    for a, n in enumerate(SHARDED):
        mine = jnp.stack([reduced[l][0][a] for l in range(depth)])
        theirs = jnp.stack([reduced[l][1][a] for l in range(depth)])
        out[n] = adamw(w[n], [mine, theirs], m[n], v[n])
    small = jnp.stack([jnp.stack([reduced[l][0][-1] for l in range(depth)]),
                       jnp.stack([reduced[l][1][-1] for l in range(depth)])])
    quarter = add_lead(small)
    full = chip_exchange([quarter], False, "gather_small")[0]
    full = jnp.moveaxis(full, 0, 1).reshape(depth, -1)
    off = 0
    for n in SMALL:
        size = int(np.prod(w[n].shape[1:]))
        out[n] = adamw(w[n], [full[:, off:off + size].reshape(w[n].shape)], m[n], v[n])
        off += size
    return (loss, grad_x[None], *[out[n][0] for n in WEIGHTS], *[out[n][1] for n in WEIGHTS],
            *[out[n][2] for n in WEIGHTS], *[out[n][3] for n in WEIGHTS])
```

```python
import numpy as np
import jax
import jax.numpy as jnp
from jax import lax
from jax.experimental import pallas as pl
from jax.experimental.pallas import tpu as pltpu

F32 = jnp.float32
BF16 = jnp.bfloat16
MESH = pl.DeviceIdType.MESH

EPS = 1e-6
NEG_INF = -1e30
D_MODEL = 1024
BR_WIDTH = 512
N_BRANCH = 3
N_HEADS = 8
HEAD_PAD = 128
CHUNK_SHIFT = 6
SGU_BLOCK = 128
MLA_NOPE, MLA_ROPE, MLA_V = 64, 32, 64
MLA_QK = MLA_NOPE + MLA_ROPE
MLA_Q_RANK, MLA_KV_RANK = 256, 128
CA_HEAD_DIM = 64
REL_CLIP = 128
ROPE_BASE = 10000.0
D_IN = 7584

ADAM_LR, ADAM_B1, ADAM_B2, ADAM_EPS, ADAM_WD, ADAM_STEP = 0.001, 0.9, 0.999, 1e-08, 0.01, 10

P_QC, P_KC, P_VC, P_QD, P_KVD, P_KR, P_ZB, P_ZC, P_G, P_U, P_V, P_ZA, P_W = (
    0, 512, 1024, 1536, 1792, 1920, 2048, 2560, 3072, 6144, 6656, 7168, 7680)
NAT_SEGS = [(0, 1536, P_U), (1536, 1920, P_QD), (1920, 1952, P_KR + MLA_NOPE), (1952, 2464, P_ZB),
            (2464, 4000, P_QC), (4000, 4512, P_ZC), (4512, 7584, P_G)]
SHARD_COLS = D_IN // 4

VMEM_LIMIT = 48 * 1024 * 1024
ATT_T = 512
MLA_HEADS_PER_STEP = 2
ROW_TILE = 256
MM_TM = 512
MM_TN = 768
LOG2E = 1.4426950408889634
MLA_SCALE = MLA_QK ** -0.5
CA_SCALE = CA_HEAD_DIM ** -0.5

WEIGHTS = ['w_in', 'pre_g', 'post_g', 'sgu_ln_g', 'sgu_ln_b', 'sgu_w', 'sgu_b', 'mla_q_norm_g',
           'mla_kv_norm_g', 'mla_w_uq', 'mla_w_ukv', 'ca_rel_bias', 'w_branch', 'gate_b', 'w_out']
SHARDED = ['w_in', 'mla_w_uq', 'mla_w_ukv', 'w_branch', 'gate_b', 'w_out']
SMALL = ['pre_g', 'post_g', 'sgu_ln_g', 'sgu_ln_b', 'sgu_w', 'sgu_b', 'mla_q_norm_g',
         'mla_kv_norm_g', 'ca_rel_bias']
SHARD_AXIS = {'w_in': 1, 'mla_w_uq': 1, 'mla_w_ukv': 1, 'w_branch': 2, 'gate_b': 1, 'w_out': 0}


def _call(body, **kw):
    return pl.pallas_call(body, **kw)


def _params(n_axes):
    return pltpu.CompilerParams(dimension_semantics=("arbitrary",) * n_axes,
                                vmem_limit_bytes=VMEM_LIMIT)


def _nt(a, b):
    return lax.dot_general(a, b, (((1,), (1,)), ((), ())), preferred_element_type=F32)


def _nn(a, b):
    return jnp.dot(a, b, preferred_element_type=F32)


def _tn(a, b):
    return lax.dot_general(a, b, (((0,), (0,)), ((), ())), preferred_element_type=F32)


def _rms(xv, g):
    r = lax.rsqrt(jnp.mean(xv * xv, axis=-1, keepdims=True) + EPS)
    return xv * r * g, r


def _rms_bwd(xv, g, r, dy):
    gy = dy * g
    dx = r * gy - xv * (r * r * r) * jnp.mean(xv * gy, axis=-1, keepdims=True)
    dg = jnp.sum(dy * (xv * r), axis=0, keepdims=True)
    return dx, dg


def _sigmoid(z):
    return 1.0 / (1.0 + jnp.exp(-z))


def _rope(xv, c, a, b):
    return xv * c + pltpu.roll(xv, 112, 1) * a + pltpu.roll(xv, 16, 1) * b


def _accumulate(ref, val, first):
    @pl.when(first)
    def _():
        ref[...] = val

    @pl.when(jnp.logical_not(first))
    def _():
        ref[...] += val


def norm_matmul(x, g, w):
    S, D = x.shape
    N = w.shape[1]
    tm, tn = min(S, 2 * MM_TM), MM_TN

    def body(x_ref, g_ref, w_ref, o_ref, xn_ref):
        @pl.when(pl.program_id(1) == 0)
        def _():
            y, _ = _rms(x_ref[...], g_ref[...])
            xn_ref[...] = y.astype(BF16)

        o_ref[...] = _nn(xn_ref[...], w_ref[...])

    return _call(
        body, name="norm_matmul", grid=(S // tm, N // tn),
        in_specs=[pl.BlockSpec((tm, D), lambda i, j: (i, 0)),
                  pl.BlockSpec((1, D), lambda i, j: (0, 0)),
                  pl.BlockSpec((D, tn), lambda i, j: (0, j))],
        out_specs=[pl.BlockSpec((tm, tn), lambda i, j: (i, j)),
                   pl.BlockSpec((tm, D), lambda i, j: (i, 0))],
        out_shape=[jax.ShapeDtypeStruct((S, N), F32), jax.ShapeDtypeStruct((S, D), BF16)],
        compiler_params=_params(2))(x, g, w)


def proj_bwd_x(dproj, w, x, g, resid):
    S, N = dproj.shape
    D = x.shape[1]
    tm, tk = min(S, MM_TM), MM_TN
    nk = N // tk

    def body(dp_ref, w_ref, x_ref, g_ref, r_ref, dx_ref, dg_ref, acc_ref):
        i, k = pl.program_id(0), pl.program_id(1)

        @pl.when(k == 0)
        def _():
            acc_ref[...] = jnp.zeros_like(acc_ref)

        acc_ref[...] += _nt(dp_ref[...].astype(BF16), w_ref[...])

        @pl.when(k == nk - 1)
        def _():
            xv = x_ref[...]
            _, r = _rms(xv, g_ref[...])
            dx, dg = _rms_bwd(xv, g_ref[...], r, acc_ref[...])
            dx_ref[...] = dx + r_ref[...]
            _accumulate(dg_ref, dg, i == 0)

    return _call(
        body, name="proj_bwd_x", grid=(S // tm, nk),
        in_specs=[pl.BlockSpec((tm, tk), lambda i, k: (i, k)),
                  pl.BlockSpec((D, tk), lambda i, k: (0, k)),
                  pl.BlockSpec((tm, D), lambda i, k: (i, 0)),
                  pl.BlockSpec((1, D), lambda i, k: (0, 0)),
                  pl.BlockSpec((tm, D), lambda i, k: (i, 0))],
        out_specs=[pl.BlockSpec((tm, D), lambda i, k: (i, 0)),
                   pl.BlockSpec((1, D), lambda i, k: (0, 0))],
        out_shape=[jax.ShapeDtypeStruct((S, D), F32), jax.ShapeDtypeStruct((1, D), F32)],
        scratch_shapes=[pltpu.VMEM((tm, D), F32)],
        compiler_params=_params(2))(dproj, w, x, g, resid)


def matmul_tn(a, b, tn):
    S, M = a.shape
    N = b.shape[1]
    tk = min(S, 2 * MM_TM)

    def body(a_ref, b_ref, o_ref):
        @pl.when(pl.program_id(1) == 0)
        def _():
            o_ref[...] = jnp.zeros_like(o_ref)

        o_ref[...] += _tn(a_ref[...].astype(BF16), b_ref[...].astype(BF16))

    return _call(
        body, name="matmul_tn", grid=(N // tn, S // tk),
        in_specs=[pl.BlockSpec((tk, M), lambda j, k: (k, 0)),
                  pl.BlockSpec((tk, tn), lambda j, k: (k, j))],
        out_specs=pl.BlockSpec((M, tn), lambda j, k: (0, j)),
        out_shape=jax.ShapeDtypeStruct((M, N), F32),
        compiler_params=_params(2))(a, b)


def _sgu_block(vv, g, b, ws_ref, lane):
    mu = jnp.mean(vv, axis=-1, keepdims=True)
    xc = vv - mu
    r = lax.rsqrt(jnp.mean(xc * xc, axis=-1, keepdims=True) + EPS)
    xhat = xc * r
    vln = (xhat * g + b).astype(BF16)
    pieces = []
    for p in range(4):
        vp = vln[:, p * 128:(p + 1) * 128]
        pieces.append(jnp.where(lane < 64, _nn(ws_ref[2 * p], vp), _nn(ws_ref[2 * p + 1], vp)))
    return xhat, r, vln, jnp.concatenate(pieces, axis=1)


def sgu_fwd(proj, ln_g, ln_b, ws, bias_full):
    S = proj.shape[0]
    ts = ROW_TILE

    def body(u_ref, v_ref, z_ref, g_ref, b_ref, ws_ref, bf_ref, y_ref):
        lane = lax.broadcasted_iota(jnp.int32, (SGU_BLOCK, 128), 1)
        for blk in range(ts // SGU_BLOCK):
            rows = slice(blk * SGU_BLOCK, (blk + 1) * SGU_BLOCK)
            _, _, _, mixed = _sgu_block(v_ref[rows, :], g_ref[...], b_ref[...], ws_ref, lane)
            mixed = mixed + bf_ref[...]
            zz = z_ref[rows, :]
            y_ref[rows, :] = (u_ref[rows, :] * mixed * (zz * _sigmoid(zz))).astype(BF16)

    col = lambda c: pl.BlockSpec((ts, BR_WIDTH), lambda i: (i, c))
    full = lambda shape: pl.BlockSpec(shape, lambda i: (0,) * len(shape))
    return _call(
        body, name="sgu_fwd", grid=(S // ts,),
        in_specs=[col(P_U // 512), col(P_V // 512), col(P_ZA // 512),
                  full((1, BR_WIDTH)), full((1, BR_WIDTH)), full((8, 128, 128)), full((128, BR_WIDTH))],
        out_specs=pl.BlockSpec((ts, BR_WIDTH), lambda i: (i, 0)),
        out_shape=jax.ShapeDtypeStruct((S, BR_WIDTH), BF16),
        compiler_params=_params(1))(proj, proj, proj, ln_g, ln_b, ws, bias_full)


def sgu_bwd(dproj, dy, proj, ln_g, ln_b, ws, ws_t, bias_full):
    S = proj.shape[0]
    ts = ROW_TILE

    def body(dp_in, dy_ref, u_ref, v_ref, z_ref, g_ref, b_ref, ws_ref, wst_ref, bf_ref,
             dp_ref, gg_ref, gb_ref, gws_ref, gbf_ref):
        del dp_in
        first = pl.program_id(0) == 0

        @pl.when(first)
        def _():
            gg_ref[...] = jnp.zeros_like(gg_ref)
            gb_ref[...] = jnp.zeros_like(gb_ref)
            gws_ref[...] = jnp.zeros_like(gws_ref)
            gbf_ref[...] = jnp.zeros_like(gbf_ref)

        lane = lax.broadcasted_iota(jnp.int32, (SGU_BLOCK, 128), 1)
        for blk in range(ts // SGU_BLOCK):
            rows = slice(blk * SGU_BLOCK, (blk + 1) * SGU_BLOCK)
            g = g_ref[...]
            xhat, r, vln, mixed = _sgu_block(v_ref[rows, :], g, b_ref[...], ws_ref, lane)
            mixed = mixed + bf_ref[...]
            zz = z_ref[rows, :]
            uu = u_ref[rows, :]
            dyv = dy_ref[0, rows, :]
            sg = _sigmoid(zz)
            sil = zz * sg
            dmixed = dyv * uu * sil
            dp_ref[rows, 0:512] = (dyv * mixed * sil).astype(BF16)
            dp_ref[rows, 1024:1536] = (dyv * uu * mixed * (sg * (1.0 + zz * (1.0 - sg)))).astype(BF16)
            gbf_ref[...] += dmixed
            dmb = dmixed.astype(BF16)
            pieces = []
            for p in range(4):
                dmp = dmb[:, p * 128:(p + 1) * 128]
                vp = vln[:, p * 128:(p + 1) * 128]
                pieces.append(jnp.where(lane < 64, _nn(wst_ref[2 * p], dmp), _nn(wst_ref[2 * p + 1], dmp)))
                zero = jnp.zeros_like(dmp)
                gws_ref[2 * p] += _nt(jnp.where(lane < 64, dmp, zero), vp)
                gws_ref[2 * p + 1] += _nt(jnp.where(lane >= 64, dmp, zero), vp)
            dvln = jnp.concatenate(pieces, axis=1)
            dxh = dvln * g
            dp_ref[rows, 512:1024] = (r * (dxh - jnp.mean(dxh, axis=-1, keepdims=True)
                                           - xhat * jnp.mean(dxh * xhat, axis=-1, keepdims=True))).astype(BF16)
            gg_ref[...] += jnp.sum(dvln * xhat, axis=0, keepdims=True)
            gb_ref[...] += jnp.sum(dvln, axis=0, keepdims=True)

    col = lambda c: pl.BlockSpec((ts, BR_WIDTH), lambda i: (i, c))
    full = lambda shape: pl.BlockSpec(shape, lambda i: (0,) * len(shape))
    return _call(
        body, name="sgu_bwd", grid=(S // ts,),
        in_specs=[pl.BlockSpec(memory_space=pl.ANY),
                  pl.BlockSpec((1, ts, BR_WIDTH), lambda i: (0, i, 0)),
                  col(P_U // 512), col(P_V // 512), col(P_ZA // 512),
                  full((1, BR_WIDTH)), full((1, BR_WIDTH)), full((8, 128, 128)), full((8, 128, 128)),
                  full((128, BR_WIDTH))],
        out_specs=[pl.BlockSpec((ts, 1536), lambda i: (i, P_U // 1536)),
                   full((1, BR_WIDTH)), full((1, BR_WIDTH)), full((8, 128, 128)), full((128, BR_WIDTH))],
        out_shape=[jax.ShapeDtypeStruct(dproj.shape, BF16),
                   jax.ShapeDtypeStruct((1, BR_WIDTH), F32), jax.ShapeDtypeStruct((1, BR_WIDTH), F32),
                   jax.ShapeDtypeStruct((8, 128, 128), F32), jax.ShapeDtypeStruct((128, BR_WIDTH), F32)],
        input_output_aliases={0: 0},
        compiler_params=_params(1))(dproj, dy, proj, proj, proj, ln_g, ln_b, ws, ws_t, bias_full)


def _hspec(ts):
    return pl.BlockSpec((N_HEADS, ts, HEAD_PAD), lambda i: (0, i, 0))


def prep_fwd(proj, tab, qg, kvg, wq, wk, wv):
    S = proj.shape[0]
    ts = ROW_TILE

    def body(qc_ref, kc_ref, vc_ref, qd_ref, kvd_ref, kr_ref, tab_ref, qg_ref, kvg_ref,
             wq_ref, wk_ref, wv_ref, qb, kb, vb, qc, kc, vc, kbt, vbt, kct, vct, cq_o, ckv_o):
        c, a, b = tab_ref[0], tab_ref[1], tab_ref[2]
        cq, _ = _rms(qd_ref[...], qg_ref[...])
        ckv, _ = _rms(kvd_ref[...], kvg_ref[...])
        cqb, ckvb = cq.astype(BF16), ckv.astype(BF16)
        cq_o[...] = cqb
        ckv_o[...] = ckvb
        krr = _rope(kr_ref[...], c, a, b)
        lane = lax.broadcasted_iota(jnp.int32, (ts, 128), 1)
        ones_lane = jnp.where(lane == MLA_V, 1.0, 0.0)
        for h in range(N_HEADS):
            cols = slice(h * HEAD_PAD, (h + 1) * HEAD_PAD)
            qb[h] = (_rope(_nn(cqb, wq_ref[:, cols]), c, a, b) * (MLA_SCALE * LOG2E)).astype(BF16)
            kh = _nn(ckvb, wk_ref[:, cols]) + krr
            vh = _nn(ckvb, wv_ref[:, cols]) + ones_lane
            kb[h], kbt[h] = kh.astype(BF16), kh.T.astype(BF16)
            vb[h], vbt[h] = vh.astype(BF16), vh.T.astype(BF16)
        for p in range(4):
            piece = qc_ref[:, p * 128:(p + 1) * 128] * (CA_SCALE * LOG2E)
            qc[2 * p] = jnp.where(lane < 64, piece, 0.0).astype(BF16)
            qc[2 * p + 1] = jnp.where(lane < 64, pltpu.roll(piece, 64, 1), 0.0).astype(BF16)
            for src, dst, dst_t, pad in ((kc_ref, kc, kct, 0.0), (vc_ref, vc, vct, ones_lane)):
                piece = src[:, p * 128:(p + 1) * 128]
                for h, head in ((2 * p, jnp.where(lane < 64, piece, pad)),
                                (2 * p + 1, jnp.where(lane < 64, pltpu.roll(piece, 64, 1), pad))):
                    dst[h], dst_t[h] = head.astype(BF16), head.T.astype(BF16)

    col = lambda w, c: pl.BlockSpec((ts, w), lambda i: (i, c))
    full = lambda shape: pl.BlockSpec(shape, lambda i: (0,) * len(shape))
    hshape = jax.ShapeDtypeStruct((N_HEADS, S, HEAD_PAD), BF16)
    tshape = jax.ShapeDtypeStruct((N_HEADS, HEAD_PAD, S), BF16)
    tspec = pl.BlockSpec((N_HEADS, HEAD_PAD, ts), lambda i: (0, 0, i))
    return _call(
        body, name="prep_fwd", grid=(S // ts,),
        in_specs=[col(512, P_QC // 512), col(512, P_KC // 512), col(512, P_VC // 512),
                  col(256, P_QD // 256), col(128, P_KVD // 128), col(128, P_KR // 128),
                  pl.BlockSpec((3, ts, 128), lambda i: (0, i, 0)),
                  full((1, MLA_Q_RANK)), full((1, MLA_KV_RANK)),
                  full((MLA_Q_RANK, 1024)), full((MLA_KV_RANK, 1024)), full((MLA_KV_RANK, 1024))],
        out_specs=[_hspec(ts)] * 6 + [tspec] * 4 + [pl.BlockSpec((ts, MLA_Q_RANK), lambda i: (i, 0)),
                                                    pl.BlockSpec((ts, MLA_KV_RANK), lambda i: (i, 0))],
        out_shape=[hshape] * 6 + [tshape] * 4 + [jax.ShapeDtypeStruct((S, MLA_Q_RANK), BF16),
                                                 jax.ShapeDtypeStruct((S, MLA_KV_RANK), BF16)],
        compiler_params=_params(1))(proj, proj, proj, proj, proj, proj, tab, qg, kvg, wq, wk, wv)


def prep_bwd(dproj, dqb, dkb, dvb, dqc, dkc, dvc, proj, tab, qg, kvg, wq, wk, wv):
    S = proj.shape[0]
    ts = ROW_TILE

    def body(dp_in, dqb_r, dkb_r, dvb_r, dqc_r, dkc_r, dvc_r, qd_ref, kvd_ref, tab_ref, qg_ref, kvg_ref,
             wq_ref, wk_ref, wv_ref, dp_ref, dqf, dkf, dvf, gq_ref, gkv_ref):
        del dp_in
        c, a, b = tab_ref[0], -tab_ref[1], -tab_ref[2]
        qd, kvd = qd_ref[...], kvd_ref[...]
        _, rq = _rms(qd, qg_ref[...])
        _, rkv = _rms(kvd, kvg_ref[...])
        dcq = jnp.zeros((ts, MLA_Q_RANK), F32)
        dckv = jnp.zeros((ts, MLA_KV_RANK), F32)
        dksum = jnp.zeros((ts, HEAD_PAD), F32)
        for h in range(N_HEADS):
            cols = slice(h * HEAD_PAD, (h + 1) * HEAD_PAD)
            dqh = _rope(dqb_r[h] * MLA_SCALE, c, a, b).astype(BF16)
            dqf[:, cols] = dqh
            dcq = dcq + _nt(dqh, wq_ref[:, cols])
            dk = dkb_r[h] * (1.0 / LOG2E)
            dksum = dksum + dk
            dkh = dk.astype(BF16)
            dkf[:, cols] = dkh
            dvh = dvb_r[h].astype(BF16)
            dvf[:, cols] = dvh
            dckv = dckv + _nt(dkh, wk_ref[:, cols]) + _nt(dvh, wv_ref[:, cols])
        lane = lax.broadcasted_iota(jnp.int32, (ts, 128), 1)
        rope_lanes = jnp.logical_and(lane >= MLA_NOPE, lane < MLA_QK)
        dp_ref[:, P_KR:P_KR + 128] = jnp.where(rope_lanes, _rope(dksum, c, a, b), 0.0).astype(BF16)
        dqd, gq = _rms_bwd(qd, qg_ref[...], rq, dcq)
        dkvd, gkv = _rms_bwd(kvd, kvg_ref[...], rkv, dckv)
        dp_ref[:, P_QD:P_QD + 256] = dqd.astype(BF16)
        dp_ref[:, P_KVD:P_KVD + 128] = dkvd.astype(BF16)
        first = pl.program_id(0) == 0
        _accumulate(gq_ref, gq, first)
        _accumulate(gkv_ref, gkv, first)
        for src, base, factor in ((dqc_r, P_QC, CA_SCALE), (dkc_r, P_KC, 1.0 / LOG2E), (dvc_r, P_VC, 1.0)):
            for p in range(4):
                dp_ref[:, base + p * 128:base + (p + 1) * 128] = (
                    (src[2 * p] + pltpu.roll(src[2 * p + 1], 64, 1)) * factor).astype(BF16)

    col = lambda w, c: pl.BlockSpec((ts, w), lambda i: (i, c))
    full = lambda shape: pl.BlockSpec(shape, lambda i: (0,) * len(shape))
    wide = jax.ShapeDtypeStruct((S, 1024), BF16)
    return _call(
        body, name="prep_bwd", grid=(S // ts,),
        in_specs=[pl.BlockSpec(memory_space=pl.ANY)] + [_hspec(ts)] * 6 +
                 [col(256, P_QD // 256), col(128, P_KVD // 128),
                  pl.BlockSpec((3, ts, 128), lambda i: (0, i, 0)),
                  full((1, MLA_Q_RANK)), full((1, MLA_KV_RANK)),
                  full((MLA_Q_RANK, 1024)), full((MLA_KV_RANK, 1024)), full((MLA_KV_RANK, 1024))],
        out_specs=[pl.BlockSpec((ts, 2048), lambda i: (i, 0))] + [pl.BlockSpec((ts, 1024), lambda i: (i, 0))] * 3 +
                  [full((1, MLA_Q_RANK)), full((1, MLA_KV_RANK))],
        out_shape=[jax.ShapeDtypeStruct(dproj.shape, BF16), wide, wide, wide,
                   jax.ShapeDtypeStruct((1, MLA_Q_RANK), F32), jax.ShapeDtypeStruct((1, MLA_KV_RANK), F32)],
        input_output_aliases={0: 0},
        compiler_params=_params(1))(dproj, dqb, dkb, dvb, dqc, dkc, dvc, proj, proj, tab, qg, kvg, wq, wk, wv)


def _diag_visible(t):
    r = lax.broadcasted_iota(jnp.int32, (t, t), 0) >> CHUNK_SHIFT
    c = lax.broadcasted_iota(jnp.int32, (t, t), 1) >> CHUNK_SHIFT
    return r <= c


def _pair_tables(nq, kv_major):
    if kv_major:
        pairs = [(kb, qi) for kb in range(nq) for qi in range(kb, nq)]
    else:
        pairs = [(kb, qi) for qi in range(nq) for kb in range(qi + 1)]
    return (jnp.asarray(np.array([p[0] for p in pairs], np.int32)),
            jnp.asarray(np.array([p[1] for p in pairs], np.int32)), len(pairs))


def _finish_softmax(acc, m):
    l = acc[MLA_V:MLA_V + 1, :]
    row = lax.broadcasted_iota(jnp.int32, acc.shape, 0)
    return jnp.where(row < MLA_V, acc / l, 0.0).T, m + jnp.log2(l)


def _split_refs(refs, counts):
    out, pos = [], 0
    for c in counts:
        out.append(refs[pos:pos + c])
        pos += c
    return out


def mla_fwd(q, k, vt, exchange=()):
    H, S, _ = q.shape
    t, hb, n_ex = ATT_T, MLA_HEADS_PER_STEP, len(exchange)
    kb_tab, qi_tab, n_pairs = _pair_tables(S // t, False)

    def body(kb_ref, qi_ref, q_ref, k_ref, vt_ref, *rest):
        ex_src, (o_ref, lse_ref), ex_out, (m_s, acc_s), ex_sems = _split_refs(rest, (n_ex, 2, n_ex, 2, 3 if n_ex else 0))
        hg, p_id = pl.program_id(0), pl.program_id(1)
        kb, qi = kb_ref[p_id], qi_ref[p_id]

        if n_ex:
            @pl.when(jnp.logical_and(hg == 0, p_id == 0))
            def _():
                _exchange_issue(ex_src, ex_out, ex_sems, False, True)

        @pl.when(kb == 0)
        def _():
            m_s[...] = jnp.full_like(m_s, NEG_INF)
            acc_s[...] = jnp.zeros_like(acc_s)

        def step(masked):
            for h in range(hb):
                st = _nt(k_ref[h], q_ref[h])
                if masked:
                    st = jnp.where(_diag_visible(t), st, NEG_INF)
                m_prev = m_s[h]
                m_new = jnp.maximum(m_prev, jnp.max(st, axis=0, keepdims=True))
                p = jnp.exp2(st - m_new)
                acc_s[h] = jnp.exp2(m_prev - m_new) * acc_s[h] + _nn(vt_ref[h], p.astype(BF16))
                m_s[h] = m_new

        @pl.when(kb < qi)
        def _():
            step(False)

        @pl.when(kb == qi)
        def _():
            step(True)
            for h in range(hb):
                o_ref[h], lse_ref[h] = _finish_softmax(acc_s[h], m_s[h])

        if n_ex:
            @pl.when(jnp.logical_and(hg == H // hb - 1, p_id == n_pairs - 1))
            def _():
                _exchange_issue(ex_src, ex_out, ex_sems, False, False)

    grid_spec = pltpu.PrefetchScalarGridSpec(
        num_scalar_prefetch=2, grid=(H // hb, n_pairs),
        in_specs=[pl.BlockSpec((hb, t, HEAD_PAD), lambda h, p, kb, qi: (h, qi[p], 0)),
                  pl.BlockSpec((hb, t, HEAD_PAD), lambda h, p, kb, qi: (h, kb[p], 0)),
                  pl.BlockSpec((hb, HEAD_PAD, t), lambda h, p, kb, qi: (h, 0, kb[p]))] + [ANY] * n_ex,
        out_specs=[pl.BlockSpec((hb, t, HEAD_PAD), lambda h, p, kb, qi: (h, qi[p], 0)),
                   pl.BlockSpec((hb, 1, t), lambda h, p, kb, qi: (h, 0, qi[p]))] + [ANY] * n_ex,
        scratch_shapes=[pltpu.VMEM((hb, 1, t), F32), pltpu.VMEM((hb, HEAD_PAD, t), F32)] + _exchange_sems(n_ex))
    outs = _call(
        body, name="mla_fwd_gather" if n_ex else "mla_fwd", grid_spec=grid_spec,
        out_shape=[jax.ShapeDtypeStruct((H, S, HEAD_PAD), F32), jax.ShapeDtypeStruct((H, 1, S), F32)] +
        _exchange_out_shape(exchange, False),
        compiler_params=_params(2))(kb_tab, qi_tab, q, k, vt, *exchange)
    return outs[0], outs[1], outs[2:]


def mla_bwd(q, k, kt, v, do, lse, delta, exchange=()):
    H, S, _ = q.shape
    t, hb, n_ex = ATT_T, MLA_HEADS_PER_STEP, len(exchange)
    nq = S // t
    kb_tab, qi_tab, n_pairs = _pair_tables(nq, True)

    def body(kb_ref, qi_ref, q_ref, k_ref, kt_ref, v_ref, do_ref, lse_ref, dl_ref, *rest):
        ex_src, (dqt_ref, dk_ref, dv_ref), ex_out, (dk_s, dv_s), ex_sems = _split_refs(rest, (n_ex, 3, n_ex, 2, 3 if n_ex else 0))
        hg, p_id = pl.program_id(0), pl.program_id(1)
        kb, qi = kb_ref[p_id], qi_ref[p_id]

        if n_ex:
            @pl.when(jnp.logical_and(hg == 0, p_id == 0))
            def _():
                _exchange_issue(ex_src, ex_out, ex_sems, True, True)

        @pl.when(p_id == 0)
        def _():
            dqt_ref[...] = jnp.zeros_like(dqt_ref)

        @pl.when(qi == kb)
        def _():
            dk_s[...] = jnp.zeros_like(dk_s)
            dv_s[...] = jnp.zeros_like(dv_s)

        def step(masked):
            for h in range(hb):
                st = _nt(k_ref[h], q_ref[h])
                if masked:
                    st = jnp.where(_diag_visible(t), st, NEG_INF)
                pt = jnp.exp2(st - lse_ref[h])
                dv_s[h] += _nn(pt.astype(BF16), do_ref[h])
                dsb = (pt * (_nt(v_ref[h], do_ref[h]) - dl_ref[h])).astype(BF16)
                dk_s[h] += _nn(dsb, q_ref[h])
                dqt_ref[h, qi] += _nn(kt_ref[h], dsb)

        @pl.when(qi == kb)
        def _():
            step(True)

        @pl.when(qi > kb)
        def _():
            step(False)

        @pl.when(qi == nq - 1)
        def _():
            dk_ref[...] = dk_s[...]
            dv_ref[...] = dv_s[...]

        if n_ex:
            @pl.when(jnp.logical_and(hg == H // hb - 1, p_id == n_pairs - 1))
            def _():
                _exchange_issue(ex_src, ex_out, ex_sems, True, False)

    qtile = pl.BlockSpec((hb, t, HEAD_PAD), lambda h, p, kb, qi: (h, qi[p], 0))
    ktile = pl.BlockSpec((hb, t, HEAD_PAD), lambda h, p, kb, qi: (h, kb[p], 0))
    stat = pl.BlockSpec((hb, 1, t), lambda h, p, kb, qi: (h, 0, qi[p]))
    grid_spec = pltpu.PrefetchScalarGridSpec(
        num_scalar_prefetch=2, grid=(H // hb, n_pairs),
        in_specs=[qtile, ktile, pl.BlockSpec((hb, HEAD_PAD, t), lambda h, p, kb, qi: (h, 0, kb[p])), ktile, qtile,
                  stat, stat] + [ANY] * n_ex,
        out_specs=[pl.BlockSpec((hb, nq, HEAD_PAD, t), lambda h, p, kb, qi: (h, 0, 0, 0)), ktile, ktile] +
        [ANY] * n_ex,
        scratch_shapes=[pltpu.VMEM((hb, t, HEAD_PAD), F32), pltpu.VMEM((hb, t, HEAD_PAD), F32)] +
        _exchange_sems(n_ex))
    outs = _call(
        body, name="mla_bwd_scatter" if n_ex else "mla_bwd", grid_spec=grid_spec,
        out_shape=[jax.ShapeDtypeStruct((H, nq, HEAD_PAD, t), F32),
                   jax.ShapeDtypeStruct((H, S, HEAD_PAD), F32), jax.ShapeDtypeStruct((H, S, HEAD_PAD), F32)] +
        _exchange_out_shape(exchange, True),
        compiler_params=_params(2))(kb_tab, qi_tab, q, k, kt, v, do, lse, delta, *exchange)
    return outs[0], outs[1], outs[2], outs[3:]


def _band_specs(t, hb):
    prev = lambda i: jnp.maximum(i - 1, 0)
    return dict(
        cur=pl.BlockSpec((hb, t, HEAD_PAD), lambda h, i: (h, i, 0)),
        prev=pl.BlockSpec((hb, t, HEAD_PAD), lambda h, i: (h, prev(i), 0)),
        cur_t=pl.BlockSpec((hb, HEAD_PAD, t), lambda h, i: (h, 0, i)),
        prev_t=pl.BlockSpec((hb, HEAD_PAD, t), lambda h, i: (h, 0, prev(i))),
        stat=pl.BlockSpec((hb, 1, t), lambda h, i: (h, 0, i)),
        bias_prev=pl.BlockSpec((hb, 1, t, t), lambda h, i: (h, jnp.where(i == 0, 1, 0), 0, 0)),
        bias_cur=pl.BlockSpec((hb, 1, t, t), lambda h, i: (h, 2, 0, 0)))


def band_fwd(q, k, vt, bias):
    H, S, _ = q.shape
    t, hb = ATT_T, MLA_HEADS_PER_STEP
    sp = _band_specs(t, hb)

    def body(q_ref, kp_ref, kc_ref, vtp_ref, vtc_ref, bp_ref, bc_ref, o_ref, lse_ref):
        for h in range(hb):
            s0 = _nt(kp_ref[h], q_ref[h]) + bp_ref[h, 0]
            s1 = _nt(kc_ref[h], q_ref[h]) + bc_ref[h, 0]
            m = jnp.maximum(jnp.max(s0, axis=0, keepdims=True), jnp.max(s1, axis=0, keepdims=True))
            ot = (_nn(vtp_ref[h], jnp.exp2(s0 - m).astype(BF16)) +
                  _nn(vtc_ref[h], jnp.exp2(s1 - m).astype(BF16)))
            o_ref[h], lse_ref[h] = _finish_softmax(ot, m)

    return _call(
        body, name="band_fwd", grid=(H // hb, S // t),
        in_specs=[sp['cur'], sp['prev'], sp['cur'], sp['prev_t'], sp['cur_t'], sp['bias_prev'], sp['bias_cur']],
        out_specs=[sp['cur'], sp['stat']],
        out_shape=[jax.ShapeDtypeStruct((H, S, HEAD_PAD), F32), jax.ShapeDtypeStruct((H, 1, S), F32)],
        compiler_params=_params(2))(q, k, k, vt, vt, bias, bias)


def band_bwd(q, k, kt, v, do, lse, delta, bias):
    H, S, _ = q.shape
    t = ATT_T
    sp = _band_specs(t, 1)

    def body(q_ref, kp_ref, kc_ref, ktp_ref, ktc_ref, vp_ref, vc_ref, do_ref, lse_ref, dl_ref, bp_ref, bc_ref,
             dqt_ref, dk_ref, dv_ref, db_ref):
        i = pl.program_id(1)

        @pl.when(i == 0)
        def _():
            dk_ref[...] = jnp.zeros_like(dk_ref)
            dv_ref[...] = jnp.zeros_like(dv_ref)
            db_ref[...] = jnp.zeros_like(db_ref)

        qv, dov = q_ref[0], do_ref[0]
        dqt = jnp.zeros((HEAD_PAD, t), F32)
        windows = ((0, jnp.maximum(i - 1, 0), kp_ref, ktp_ref, vp_ref, bp_ref),
                   (1, i, kc_ref, ktc_ref, vc_ref, bc_ref))
        for w, blk, k_ref, kt_ref, v_ref, b_ref in windows:
            rows = pl.ds(pl.multiple_of(blk * t, t), t)
            pt = jnp.exp2(_nt(k_ref[0], qv) + b_ref[0, 0] - lse_ref[0])
            dv_ref[0, rows, :] += _nn(pt.astype(BF16), dov)
            ds = pt * (_nt(v_ref[0], dov) - dl_ref[0])
            db_ref[0, w] += ds
            dsb = ds.astype(BF16)
            dk_ref[0, rows, :] += _nn(dsb, qv)
            dqt = dqt + _nn(kt_ref[0], dsb)
        dqt_ref[0] = dqt

    whole = pl.BlockSpec((1, S, HEAD_PAD), lambda h, i: (h, 0, 0))
    return _call(
        body, name="band_bwd", grid=(H, S // t),
        in_specs=[sp['cur'], sp['prev'], sp['cur'], sp['prev_t'], sp['cur_t'], sp['prev'], sp['cur'], sp['cur'],
                  sp['stat'], sp['stat'], sp['bias_prev'], sp['bias_cur']],
        out_specs=[sp['cur_t'], whole, whole, pl.BlockSpec((1, 2, t, t), lambda h, i: (h, 0, 0, 0))],
        out_shape=[jax.ShapeDtypeStruct((H, HEAD_PAD, S), F32), jax.ShapeDtypeStruct((H, S, HEAD_PAD), F32),
                   jax.ShapeDtypeStruct((H, S, HEAD_PAD), F32), jax.ShapeDtypeStruct((H, 2, t, t), F32)],
        compiler_params=_params(2))(q, k, k, kt, kt, v, v, do, lse, delta, bias, bias)


def _compact(o_ref):
    return jnp.concatenate([o_ref[2 * p] + pltpu.roll(o_ref[2 * p + 1], 64, 1) for p in range(4)], axis=1)


def merge_fwd(ob, oc, proj, ya, gate_b, wbr, w_out, x, post_g):
    S = x.shape[0]
    ts = ROW_TILE

    def body(ob_ref, oc_ref, zb_ref, zc_ref, ya_ref, gl_ref, gb_ref, wbr_ref, wo_ref, x_ref, pg_ref,
             xo_ref, yb_ref, yc_ref, mg_ref, out_ref):
        zb, zc = zb_ref[...], zc_ref[...]
        yb = (_compact(ob_ref) * (zb * _sigmoid(zb))).astype(BF16)
        yc = (_compact(oc_ref) * (zc * _sigmoid(zc))).astype(BF16)
        yb_ref[...] = yb
        yc_ref[...] = yc
        merged = jnp.zeros((ts, D_MODEL), F32)
        for n, y in enumerate((ya_ref[...], yb, yc)):
            cols = slice(n * D_MODEL, (n + 1) * D_MODEL)
            gate = _sigmoid(gl_ref[:, cols] + gb_ref[:, cols])
            merged = merged + gate * _nn(y, wbr_ref[n])
        mb = merged.astype(BF16)
        mg_ref[...] = mb
        out = _nn(mb, wo_ref[...])
        out_ref[...] = out
        normed, _ = _rms(out, pg_ref[...])
        xo_ref[...] = x_ref[...] + normed

    row = lambda w: pl.BlockSpec((ts, w), lambda i: (i, 0))
    col = lambda w, c: pl.BlockSpec((ts, w), lambda i: (i, c))
    full = lambda shape: pl.BlockSpec(shape, lambda i: (0,) * len(shape))
    return _call(
        body, name="merge_fwd", grid=(S // ts,),
        in_specs=[_hspec(ts), _hspec(ts), col(512, P_ZB // 512), col(512, P_ZC // 512), row(512),
                  col(3072, P_G // 3072), full((1, 3072)), full((3, BR_WIDTH, D_MODEL)),
                  full((D_MODEL, D_MODEL)), row(D_MODEL), full((1, D_MODEL))],
        out_specs=[row(D_MODEL), row(512), row(512), row(D_MODEL), row(D_MODEL)],
        out_shape=[jax.ShapeDtypeStruct((S, D_MODEL), F32), jax.ShapeDtypeStruct((S, 512), BF16),
                   jax.ShapeDtypeStruct((S, 512), BF16), jax.ShapeDtypeStruct((S, D_MODEL), BF16),
                   jax.ShapeDtypeStruct((S, D_MODEL), F32)],
        compiler_params=_params(1))(ob, oc, proj, proj, ya, proj, gate_b, wbr, w_out, x, post_g)


def post_bwd(g, out, post_g, w_out):
    S = g.shape[0]
    ts = ROW_TILE

    def body(g_ref, out_ref, pg_ref, wo_ref, do_ref, dm_ref, gp_ref):
        ov = out_ref[...]
        _, r = _rms(ov, pg_ref[...])
        dout, gp = _rms_bwd(ov, pg_ref[...], r, g_ref[...])
        db = dout.astype(BF16)
        do_ref[...] = db
        dm_ref[...] = _nt(db, wo_ref[...])
        _accumulate(gp_ref, gp, pl.program_id(0) == 0)

    row = lambda: pl.BlockSpec((ts, D_MODEL), lambda i: (i, 0))
    full = lambda shape: pl.BlockSpec(shape, lambda i: (0,) * len(shape))
    return _call(
        body, name="post_bwd", grid=(S // ts,),
        in_specs=[row(), row(), full((1, D_MODEL)), full((D_MODEL, D_MODEL))],
        out_specs=[row(), row(), full((1, D_MODEL))],
        out_shape=[jax.ShapeDtypeStruct((S, D_MODEL), BF16), jax.ShapeDtypeStruct((S, D_MODEL), F32),
                   jax.ShapeDtypeStruct((1, D_MODEL), F32)],
        compiler_params=_params(1))(g, out, post_g, w_out)


def gate_bwd(dmerged, proj, gate_b, ya, yb, yc, wbr):
    S = dmerged.shape[0]
    ts = ROW_TILE

    def body(dm_ref, gl_ref, gb_ref, ya_ref, yb_ref, yc_ref, wbr_ref,
             dp_ref, dba_ref, dbb_ref, dbc_ref, dy_ref, ggb_ref):
        dm = dm_ref[...]
        ggb = []
        for n, (y_ref, dbr_ref) in enumerate(((ya_ref, dba_ref), (yb_ref, dbb_ref), (yc_ref, dbc_ref))):
            cols = slice(n * D_MODEL, (n + 1) * D_MODEL)
            br = _nn(y_ref[...], wbr_ref[n])
            sg = _sigmoid(gl_ref[:, cols] + gb_ref[:, cols])
            dgl = dm * br * (sg * (1.0 - sg))
            dp_ref[:, cols] = dgl.astype(BF16)
            ggb.append(jnp.sum(dgl, axis=0, keepdims=True))
            dbr = (dm * sg).astype(BF16)
            dbr_ref[...] = dbr
            dy_ref[n] = _nt(dbr, wbr_ref[n])
        _accumulate(ggb_ref, jnp.concatenate(ggb, axis=1), pl.program_id(0) == 0)

    row = lambda w: pl.BlockSpec((ts, w), lambda i: (i, 0))
    full = lambda shape: pl.BlockSpec(shape, lambda i: (0,) * len(shape))
    wide = jax.ShapeDtypeStruct((S, D_MODEL), BF16)
    return _call(
        body, name="gate_bwd", grid=(S // ts,),
        in_specs=[row(D_MODEL), pl.BlockSpec((ts, 3072), lambda i: (i, P_G // 3072)), full((1, 3072)),
                  row(512), row(512), row(512), full((3, BR_WIDTH, D_MODEL))],
        out_specs=[pl.BlockSpec((ts, 3072), lambda i: (i, P_G // 3072)), row(D_MODEL), row(D_MODEL), row(D_MODEL),
                   pl.BlockSpec((3, ts, 512), lambda i: (0, i, 0)), full((1, 3072))],
        out_shape=[jax.ShapeDtypeStruct((S, P_W), BF16), wide, wide, wide,
                   jax.ShapeDtypeStruct((3, S, 512), F32), jax.ShapeDtypeStruct((1, 3072), F32)],
        compiler_params=_params(1))(dmerged, proj, gate_b, ya, yb, yc, wbr)


def ungate_bwd(dproj, dy, ob, oc, proj):
    S = proj.shape[0]
    ts = ROW_TILE

    def body(dp_in, dyb_ref, dyc_ref, ob_ref, oc_ref, zb_ref, zc_ref, dp_ref, dob_ref, doc_ref, dlb_ref, dlc_ref):
        del dp_in
        lane = lax.broadcasted_iota(jnp.int32, (ts, 128), 1)
        for n, (dy_ref, o_ref, z_ref, do_ref, dl_ref) in enumerate(
                ((dyb_ref, ob_ref, zb_ref, dob_ref, dlb_ref), (dyc_ref, oc_ref, zc_ref, doc_ref, dlc_ref))):
            zz = z_ref[...]
            dyv = dy_ref[0]
            sg = _sigmoid(zz)
            dp_ref[:, n * 512:(n + 1) * 512] = (dyv * _compact(o_ref) * (sg * (1.0 + zz * (1.0 - sg)))).astype(BF16)
            do_c = dyv * (zz * sg)
            for p in range(4):
                piece = do_c[:, p * 128:(p + 1) * 128]
                for h, d in ((2 * p, jnp.where(lane < 64, piece, 0.0)),
                             (2 * p + 1, jnp.where(lane < 64, pltpu.roll(piece, 64, 1), 0.0))):
                    do_ref[h] = d.astype(BF16)
                    dl_ref[h] = jnp.sum(d * o_ref[h], axis=-1, keepdims=True)

    col = lambda c: pl.BlockSpec((ts, 512), lambda i: (i, c))
    dysp = lambda n: pl.BlockSpec((1, ts, 512), lambda i: (n, i, 0))
    stat = pl.BlockSpec((N_HEADS, ts, 1), lambda i: (0, i, 0))
    hshape = jax.ShapeDtypeStruct((N_HEADS, S, HEAD_PAD), BF16)
    sshape = jax.ShapeDtypeStruct((N_HEADS, S, 1), F32)
    return _call(
        body, name="ungate_bwd", grid=(S // ts,),
        in_specs=[pl.BlockSpec(memory_space=pl.ANY), dysp(1), dysp(2), _hspec(ts), _hspec(ts),
                  col(P_ZB // 512), col(P_ZC // 512)],
        out_specs=[pl.BlockSpec((ts, 1024), lambda i: (i, P_ZB // 1024)), _hspec(ts), _hspec(ts), stat, stat],
        out_shape=[jax.ShapeDtypeStruct(dproj.shape, BF16), hshape, hshape, sshape, sshape],
        input_output_aliases={0: 0},
        compiler_params=_params(1))(dproj, dy, dy, ob, oc, proj, proj)


def loss_head(y, target):
    S, D = y.shape
    ts = ROW_TILE

    def body(y_ref, t_ref, dy_ref, sq_ref):
        d = y_ref[...] - t_ref[...]
        dy_ref[...] = d * (1.0 / D)
        _accumulate(sq_ref, jnp.sum(d * d, axis=0, keepdims=True), pl.program_id(0) == 0)

    row = pl.BlockSpec((ts, D), lambda i: (i, 0))
    return _call(
        body, name="loss_head", grid=(S // ts,), in_specs=[row, row],
        out_specs=[row, pl.BlockSpec((1, D), lambda i: (0, 0))],
        out_shape=[jax.ShapeDtypeStruct((S, D), F32), jax.ShapeDtypeStruct((1, D), F32)],
        compiler_params=_params(1))(y, target)


def _row_tile(rows, cols):
    for cand in (1024, 512, 256, 128, 64, 32, 16, 8):
        if rows % cand == 0 and cand * cols * 4 <= 1024 * 1024:
            return cand
    return rows


def adamw(w, grads, m, v):
    shape = w.shape
    cols = shape[-1]
    rows = int(np.prod(shape[:-1]))
    tr = _row_tile(rows, cols)
    n_g = len(grads)
    c1 = 1.0 - ADAM_B1 ** ADAM_STEP
    c2 = 1.0 - ADAM_B2 ** ADAM_STEP

    def body(*refs):
        w_ref, m_ref, v_ref = refs[:3]
        g_refs = refs[3:3 + n_g]
        go_ref, d_ref, mo_ref, vo_ref = refs[3 + n_g:]
        gv = g_refs[0][...]
        for g_ref in g_refs[1:]:
            gv = gv + g_ref[...]
        go_ref[...] = gv
        mn = ADAM_B1 * m_ref[...] + (1.0 - ADAM_B1) * gv
        vn = ADAM_B2 * v_ref[...] + (1.0 - ADAM_B2) * (gv * gv)
        mo_ref[...] = mn
        vo_ref[...] = vn
        d_ref[...] = -ADAM_LR * ((mn / c1) / (jnp.sqrt(vn / c2) + ADAM_EPS) + ADAM_WD * w_ref[...])

    blk = pl.BlockSpec((tr, cols), lambda i: (i, 0))
    sds = jax.ShapeDtypeStruct((rows, cols), F32)
    outs = _call(
        body, name="adamw", grid=(rows // tr,), in_specs=[blk] * (3 + n_g), out_specs=[blk] * 4,
        out_shape=[sds] * 4, compiler_params=_params(1))(
            *[a.reshape(rows, cols) for a in (w, m, v, *grads)])
    return [o.reshape(shape) for o in outs]


def add_lead(parts):
    n = parts.shape[0]
    shape = parts.shape[1:]
    cols = shape[-1]
    rows = int(np.prod(shape[:-1]))
    tr = _row_tile(rows, cols * n)

    def body(p_ref, o_ref):
        acc = p_ref[0].astype(F32)
        for s in range(1, n):
            acc = acc + p_ref[s].astype(F32)
        o_ref[...] = acc

    out = _call(
        body, name="add_lead", grid=(rows // tr,),
        in_specs=[pl.BlockSpec((n, tr, cols), lambda i: (0, i, 0))],
        out_specs=pl.BlockSpec((tr, cols), lambda i: (i, 0)),
        out_shape=jax.ShapeDtypeStruct((rows, cols), F32),
        compiler_params=_params(1))(parts.reshape(n, rows, cols))
    return out.reshape(shape)


ANY = pl.BlockSpec(memory_space=pl.ANY)


def _other_chips(x, y):
    return [(1 - x, y), (x, 1 - y), (1 - x, 1 - y)]


def chip_exchange(arrays, scatter, name):
    n = len(arrays)

    def body(*refs):
        _exchange_issue(refs[:n], refs[n:2 * n], refs[2 * n:], scatter, True)
        _exchange_issue(refs[:n], refs[n:2 * n], refs[2 * n:], scatter, False)

    return _call(
        body, name=name, in_specs=[ANY] * n, out_specs=[ANY] * n,
        out_shape=_exchange_out_shape(arrays, scatter), scratch_shapes=_exchange_sems(n))(*arrays)


def _exchange_out_shape(arrays, scatter):
    return [jax.ShapeDtypeStruct(a.shape if scatter else (4,) + a.shape, a.dtype) for a in arrays]


def _exchange_sems(n):
    if n == 0:
        return []
    return [pltpu.SemaphoreType.DMA((3 * n,)), pltpu.SemaphoreType.DMA((3 * n,)), pltpu.SemaphoreType.DMA((n,))]


def _exchange_issue(srcs, outs, sems, scatter, start):
    send_sems, recv_sems, local_sems = sems
    x, y, c = lax.axis_index("x"), lax.axis_index("y"), lax.axis_index("c")
    me = 2 * x + y
    for a in range(len(srcs)):
        local_src = srcs[a].at[me] if scatter else srcs[a]
        mine = pltpu.make_async_copy(local_src, outs[a].at[me], local_sems.at[a])
        sends = []
        for j, (px, py) in enumerate(_other_chips(x, y)):
            pair = dict(send_sem=send_sems.at[3 * a + j], recv_sem=recv_sems.at[3 * a + j],
                        device_id=(px, py, c), device_id_type=MESH)
            sends.append(pltpu.make_async_remote_copy(
                src_ref=srcs[a].at[2 * px + py] if scatter else srcs[a], dst_ref=outs[a].at[me], **pair))
            if not start:
                pltpu.make_async_remote_copy(src_ref=local_src, dst_ref=outs[a].at[2 * px + py], **pair).wait_recv()
        if start:
            mine.start()
            for cp in sends:
                cp.start()
        else:
            for cp in sends:
                cp.wait_send()
            mine.wait()


def sibling_exchange(arrays):
    n = len(arrays)

    def body(*refs):
        srcs, outs = refs[:n], refs[n:2 * n]
        send_sems, recv_sems = refs[2 * n:]
        x, y, c = lax.axis_index("x"), lax.axis_index("y"), lax.axis_index("c")
        copies = [pltpu.make_async_remote_copy(src_ref=srcs[a], dst_ref=outs[a], send_sem=send_sems.at[a],
                                               recv_sem=recv_sems.at[a], device_id=(x, y, 1 - c), device_id_type=MESH)
                  for a in range(n)]
        for cp in copies:
            cp.start()
        for cp in copies:
            cp.wait()

    return _call(
        body, name="sibling_exchange", in_specs=[ANY] * n, out_specs=[ANY] * n,
        out_shape=[jax.ShapeDtypeStruct(a.shape, a.dtype) for a in arrays],
        scratch_shapes=[pltpu.SemaphoreType.DMA((n,)), pltpu.SemaphoreType.DMA((n,))])(*arrays)


def _perm_from_shards(sh):
    rows = sh.shape[1]
    pieces, pos = [], 0
    for lo, hi, plo in sorted(NAT_SEGS, key=lambda s: s[2]):
        if plo > pos:
            pieces.append(jnp.zeros((rows, plo - pos), sh.dtype))
            pos = plo
        c = lo
        while c < hi:
            kk = c // SHARD_COLS
            e = min(hi, (kk + 1) * SHARD_COLS)
            pieces.append(sh[kk][:, c - kk * SHARD_COLS:e - kk * SHARD_COLS])
            c = e
        pos += hi - lo
    if pos < P_W:
        pieces.append(jnp.zeros((rows, P_W - pos), sh.dtype))
    return jnp.concatenate(pieces, axis=1)


def _shards_from_perm(p):
    out = []
    for kk in range(4):
        lo_k, hi_k = kk * SHARD_COLS, (kk + 1) * SHARD_COLS
        pieces = []
        for lo, hi, plo in NAT_SEGS:
            a, b = max(lo, lo_k), min(hi, hi_k)
            if a < b:
                pieces.append(p[:, plo + (a - lo):plo + (b - lo)])
        out.append(jnp.concatenate(pieces, axis=1))
    return jnp.stack(out)


def _split4(a, axis):
    shape = a.shape
    a = a.reshape(shape[:axis] + (4, shape[axis] // 4) + shape[axis + 1:])
    return jnp.moveaxis(a, axis, 0)


def _join4(a, axis):
    a = jnp.moveaxis(a, 0, axis)
    shape = a.shape
    return a.reshape(shape[:axis] + (4 * shape[axis + 1],) + shape[axis + 2:])


def _pad_heads(w, per_head, lo, hi):
    r = w.shape[0]
    wh = w.reshape(r, N_HEADS, per_head)[:, :, lo:hi]
    return jnp.pad(wh, ((0, 0), (0, 0), (0, HEAD_PAD - (hi - lo)))).reshape(r, N_HEADS * HEAD_PAD)


def _rope_table(S):
    half = MLA_ROPE // 2
    inv = ROPE_BASE ** (-jnp.arange(half, dtype=F32) / half)
    ang = jnp.arange(S).astype(F32)[:, None] * inv[None, :]
    cos, sin = jnp.cos(ang), jnp.sin(ang)
    z = lambda n: jnp.zeros((S, n), F32)
    c = jnp.concatenate([jnp.ones((S, MLA_NOPE), F32), cos, cos, z(32)], axis=1)
    a = jnp.concatenate([z(MLA_NOPE), -sin, z(48)], axis=1)
    b = jnp.concatenate([z(MLA_NOPE + half), sin, z(32)], axis=1)
    return jnp.stack([c, a, b])


def _band_onehot():
    t = ATT_T
    m = np.arange(2 * t - 1)
    d = np.where(m < t, m, m - (2 * t - 1))
    idx = np.stack([np.clip(off + d, -REL_CLIP, REL_CLIP) + REL_CLIP for off in (t, 0)])
    return (idx[:, :, None] == np.arange(2 * REL_CLIP + 1)[None, None, :]).astype(np.float32)


def _band_mask():
    t = ATT_T
    kc = (np.arange(t) >> CHUNK_SHIFT)[:, None]
    qc = (np.arange(t) >> CHUNK_SHIFT)[None, :]
    return np.stack([kc >= qc, kc <= qc])


def _bias_tiles(table):
    t = ATT_T
    vals = jnp.einsum('hr,wdr->hwd', table, jnp.asarray(_band_onehot()), precision=lax.Precision.HIGHEST)
    rowsrep = jnp.broadcast_to(vals[:, :, None, :], (N_HEADS, 2, t, 2 * t - 1)).reshape(N_HEADS, 2, -1)
    skew = rowsrep[..., :t * (2 * t - 2)].reshape(N_HEADS, 2, t, 2 * t - 2)[..., :t]
    tiles = jnp.where(jnp.asarray(_band_mask()), skew * LOG2E, NEG_INF)
    return jnp.stack([tiles[:, 0], jnp.full((N_HEADS, t, t), NEG_INF, F32), tiles[:, 1]], axis=1)


def _bias_tiles_grad(dtiles):
    t = ATT_T
    wide = jnp.pad(dtiles, ((0, 0), (0, 0), (0, 0), (0, t - 2))).reshape(N_HEADS, 2, -1)
    flat = jnp.pad(wide, ((0, 0), (0, 0), (0, t)))
    diag = jnp.sum(flat.reshape(N_HEADS, 2, t, 2 * t - 1), axis=2)
    return jnp.einsum('hwd,wdr->hr', diag, jnp.asarray(_band_onehot()), precision=lax.Precision.HIGHEST)


def _layer_consts(lw):
    tri = np.tril(np.ones((SGU_BLOCK, SGU_BLOCK), np.float32))
    ws = (lw['sgu_w'] * tri).astype(BF16)
    return dict(
        ws=ws, ws_t=jnp.swapaxes(ws, 1, 2), sgu_bias=jnp.repeat(lw['sgu_b'].T, CA_HEAD_DIM, axis=1),
        bias=_bias_tiles(lw['ca_rel_bias']),
        wq=_pad_heads(lw['mla_w_uq'], MLA_QK, 0, MLA_QK),
        wk=_pad_heads(lw['mla_w_ukv'], MLA_NOPE + MLA_V, 0, MLA_NOPE),
        wv=_pad_heads(lw['mla_w_ukv'], MLA_NOPE + MLA_V, MLA_NOPE, MLA_NOPE + MLA_V),
        gate_b=lw['gate_b'].reshape(1, 3 * D_MODEL),
        pre_g=lw['pre_g'][None], post_g=lw['post_g'][None], ln_g=lw['sgu_ln_g'][None], ln_b=lw['sgu_ln_b'][None],
        qg=lw['mla_q_norm_g'][None], kvg=lw['mla_kv_norm_g'][None])


def _layer_fwd(x, lw, k, tab, next_shards):
    proj, xn = norm_matmul(x, k['pre_g'], lw['w_in'])
    ya = sgu_fwd(proj, k['ln_g'], k['ln_b'], k['ws'], k['sgu_bias'])
    qb, kb, vb, qc, kc, vc, kbt, vbt, kct, vct, cq, ckv = prep_fwd(proj, tab, k['qg'], k['kvg'], k['wq'], k['wk'],
                                                                   k['wv'])
    ob, lse_b, gathered = mla_fwd(qb, kb, vbt, next_shards)
    oc, lse_c = band_fwd(qc, kc, vct, k['bias'])
    x_new, yb, yc, merged, out = merge_fwd(ob, oc, proj, ya, k['gate_b'], lw['w_branch'], lw['w_out'], x,
                                           k['post_g'])
    saved = dict(x=x, proj=proj, xn=xn, ya=ya, yb=yb, yc=yc, qb=qb, kb=kb, vb=vb, qc=qc, kc=kc, vc=vc, kbt=kbt, kct=kct,
                 cq=cq, ckv=ckv, ob=ob, oc=oc, lse_b=lse_b, lse_c=lse_c, merged=merged, out=out)
    return x_new, saved, gathered


def _layer_bwd(g, s, lw, k, tab, pending_parts):
    S = g.shape[0]
    H = N_HEADS
    dout, dmerged, g_post = post_bwd(g, s['out'], k['post_g'], lw['w_out'])
    g_w_out = matmul_tn(s['merged'], dout, 512)
    dproj, dba, dbb, dbc, dy, g_gate_b = gate_bwd(dmerged, s['proj'], k['gate_b'], s['ya'], s['yb'], s['yc'],
                                                  lw['w_branch'])
    g_w_branch = jnp.stack([matmul_tn(y, d, 512) for y, d in ((s['ya'], dba), (s['yb'], dbb), (s['yc'], dbc))])
    dproj, dob, doc, dl_b, dl_c = ungate_bwd(dproj, dy, s['ob'], s['oc'], s['proj'])
    row = lambda a: a.reshape(H, 1, S)
    tr = lambda a: jnp.swapaxes(a, 1, 2)
    dqt, dkb, dvb, landed = mla_bwd(s['qb'], s['kb'], s['kbt'], s['vb'], dob, s['lse_b'], row(dl_b), pending_parts)
    dqb = jnp.swapaxes(dqt, 2, 3).reshape(H, S, HEAD_PAD)
    dqct, dkc, dvc, dbias = band_bwd(s['qc'], s['kc'], s['kct'], s['vc'], doc, s['lse_c'], row(dl_c), k['bias'])
    dproj, dqf, dkf, dvf, g_qg, g_kvg = prep_bwd(dproj, dqb, dkb, dvb, tr(dqct), dkc, dvc, s['proj'], tab,
                                                 k['qg'], k['kvg'], k['wq'], k['wk'], k['wv'])
    g_wq = matmul_tn(s['cq'], dqf, 512).reshape(MLA_Q_RANK, H, HEAD_PAD)[:, :, :MLA_QK]
    g_wk = matmul_tn(s['ckv'], dkf, 512).reshape(MLA_KV_RANK, H, HEAD_PAD)[:, :, :MLA_NOPE]
    g_wv = matmul_tn(s['ckv'], dvf, 512).reshape(MLA_KV_RANK, H, HEAD_PAD)[:, :, :MLA_V]
    dproj, g_ln_g, g_ln_b, g_ws, g_sgu_bias = sgu_bwd(dproj, dy, s['proj'], k['ln_g'], k['ln_b'], k['ws'],
                                                      k['ws_t'], k['sgu_bias'])
    g_w_in = matmul_tn(s['xn'], dproj, MM_TN)
    dx, g_pre = proj_bwd_x(dproj, lw['w_in'], s['x'], k['pre_g'], g)
    tri = np.tril(np.ones((SGU_BLOCK, SGU_BLOCK), np.float32))
    grads = dict(
        w_in=g_w_in, pre_g=g_pre[0], post_g=g_post[0], sgu_ln_g=g_ln_g[0], sgu_ln_b=g_ln_b[0],
        sgu_w=g_ws * tri, sgu_b=jnp.sum(g_sgu_bias.reshape(SGU_BLOCK, 8, CA_HEAD_DIM), axis=2).T,
        mla_q_norm_g=g_qg[0], mla_kv_norm_g=g_kvg[0],
        mla_w_uq=g_wq.reshape(MLA_Q_RANK, H * MLA_QK),
        mla_w_ukv=jnp.concatenate([g_wk, g_wv], axis=2).reshape(MLA_KV_RANK, H * (MLA_NOPE + MLA_V)),
        ca_rel_bias=_bias_tiles_grad(dbias), w_branch=g_w_branch,
        gate_b=g_gate_b.reshape(N_BRANCH, D_MODEL), w_out=g_w_out)
    return dx, grads, landed


BF16_PARTS = ('w_in', 'mla_w_uq', 'mla_w_ukv', 'w_branch', 'w_out')


def _weight_shards(w, l):
    return [w[n][l].astype(BF16) if n in BF16_PARTS else w[n][l] for n in SHARDED]


def _full_weights(gathered, small):
    lw = {n: _join4(a, SHARD_AXIS[n]) for n, a in zip(SHARDED, gathered) if n != 'w_in'}
    lw['w_in'] = _perm_from_shards(gathered[0])
    lw.update(small)
    return lw


def _small_pack(grads):
    flat = jnp.concatenate([grads[n].reshape(-1) for n in SMALL])
    quarter = -(-flat.size // (4 * 1024)) * 1024
    return jnp.pad(flat, (0, 4 * quarter - flat.size)).reshape(4, quarter // 128, 128)


def _grad_parts(grads):
    parts = [_shards_from_perm(grads['w_in'])]
    parts += [_split4(grads[n], SHARD_AXIS[n]) for n in SHARDED if n != 'w_in']
    parts = [p.astype(BF16) if n in BF16_PARTS else p for n, p in zip(SHARDED, parts)]
    return parts + [_small_pack(grads)]


def _sum_landed(landed):
    mine = [add_lead(p) for p in landed]
    return mine, sibling_exchange(mine)


def train_step_local(x, target, w):
    S = x.shape[0]
    depth = w['w_in'].shape[0]
    tab = _rope_table(S)
    gathered = chip_exchange(_weight_shards(w, 0), False, "gather_weights")
    layer_w, consts, saved = [], [], []
    for l in range(depth):
        lw = _full_weights(gathered, {n: w[n][l] for n in SMALL})
        k = _layer_consts(lw)
        x, s, gathered = _layer_fwd(x, lw, k, tab, _weight_shards(w, l + 1) if l + 1 < depth else ())
        layer_w.append(lw)
        consts.append(k)
        saved.append(s)
    g, sq = loss_head(x, target)
    reduced = [None] * depth
    pending = ()
    for l in reversed(range(depth)):
        g, grads, landed = _layer_bwd(g, saved[l], layer_w[l], consts[l], tab, pending)
        if pending:
            reduced[l + 1] = _sum_landed(landed)
        pending = _grad_parts(grads)
    reduced[0] = _sum_landed(chip_exchange(pending, True, "scatter_grads"))
    return sq, g, reduced


def kernel(x, w_in, pre_g, post_g, sgu_ln_g, sgu_ln_b, sgu_w, sgu_b, mla_q_norm_g, mla_kv_norm_g, mla_w_uq, mla_w_ukv, ca_rel_bias, w_branch, gate_b, w_out, loss_target, m_w_in, m_pre_g, m_post_g, m_sgu_ln_g, m_sgu_ln_b, m_sgu_w, m_sgu_b, m_mla_q_norm_g, m_mla_kv_norm_g, m_mla_w_uq, m_mla_w_ukv, m_ca_rel_bias, m_w_branch, m_gate_b, m_w_out, v_w_in, v_pre_g, v_post_g, v_sgu_ln_g, v_sgu_ln_b, v_sgu_w, v_sgu_b, v_mla_q_norm_g, v_mla_kv_norm_g, v_mla_w_uq, v_mla_w_ukv, v_ca_rel_bias, v_w_branch, v_gate_b, v_w_out):
    w = dict(w_in=w_in, pre_g=pre_g, post_g=post_g, sgu_ln_g=sgu_ln_g, sgu_ln_b=sgu_ln_b, sgu_w=sgu_w, sgu_b=sgu_b,
             mla_q_norm_g=mla_q_norm_g, mla_kv_norm_g=mla_kv_norm_g, mla_w_uq=mla_w_uq, mla_w_ukv=mla_w_ukv,
             ca_rel_bias=ca_rel_bias, w_branch=w_branch, gate_b=gate_b, w_out=w_out)
    m = dict(w_in=m_w_in, pre_g=m_pre_g, post_g=m_post_g, sgu_ln_g=m_sgu_ln_g, sgu_ln_b=m_sgu_ln_b, sgu_w=m_sgu_w,
             sgu_b=m_sgu_b, mla_q_norm_g=m_mla_q_norm_g, mla_kv_norm_g=m_mla_kv_norm_g, mla_w_uq=m_mla_w_uq,
             mla_w_ukv=m_mla_w_ukv, ca_rel_bias=m_ca_rel_bias, w_branch=m_w_branch, gate_b=m_gate_b, w_out=m_w_out)
    v = dict(w_in=v_w_in, pre_g=v_pre_g, post_g=v_post_g, sgu_ln_g=v_sgu_ln_g, sgu_ln_b=v_sgu_ln_b, sgu_w=v_sgu_w,
             sgu_b=v_sgu_b, mla_q_norm_g=v_mla_q_norm_g, mla_kv_norm_g=v_mla_kv_norm_g, mla_w_uq=v_mla_w_uq,
             mla_w_ukv=v_mla_w_ukv, ca_rel_bias=v_ca_rel_bias, w_branch=v_w_branch, gate_b=v_gate_b, w_out=v_w_out)
    depth = w_in.shape[0]
    sq, grad_x, reduced = train_step_local(x[0], loss_target[0], w)
    loss = lax.psum(0.5 * jnp.sum(sq) / D_MODEL, ("x", "y", "c"))

    out = {}
    for a, n in enumerate(SHARDED):
        mine = jnp.stack([reduced[l][0][a] for l in range(depth)])
        theirs = jnp.stack([reduced[l][1][a] for l in range(depth)])
        out[n] = adamw(w[n], [mine, theirs], m[n], v[n])
    small = jnp.stack([jnp.stack([reduced[l][0][-1] for l in range(depth)]),
                       jnp.stack([reduced[l][1][-1] for l in range(depth)])])
    quarter = add_lead(small)
    full = chip_exchange([quarter], False, "gather_small")[0]
    full = jnp.moveaxis(full, 0, 1).reshape(depth, -1)
    off = 0
    for n in SMALL:
        size = int(np.prod(w[n].shape[1:]))
        out[n] = adamw(w[n], [full[:, off:off + size].reshape(w[n].shape)], m[n], v[n])
        off += size
    return (loss, grad_x[None], *[out[n][0] for n in WEIGHTS], *[out[n][1] for n in WEIGHTS],
            *[out[n][2] for n in WEIGHTS], *[out[n][3] for n in WEIGHTS])
```

```python
import numpy as np
import jax
import jax.numpy as jnp
from jax import lax
from jax.experimental import pallas as pl
from jax.experimental.pallas import tpu as pltpu

F32 = jnp.float32
BF16 = jnp.bfloat16
MESH = pl.DeviceIdType.MESH

EPS = 1e-6
NEG_INF = -1e30
D_MODEL = 1024
BR_WIDTH = 512
N_BRANCH = 3
N_HEADS = 8
HEAD_PAD = 128
CHUNK_SHIFT = 6
SGU_BLOCK = 128
MLA_NOPE, MLA_ROPE, MLA_V = 64, 32, 64
MLA_QK = MLA_NOPE + MLA_ROPE
MLA_Q_RANK, MLA_KV_RANK = 256, 128
CA_HEAD_DIM = 64
REL_CLIP = 128
ROPE_BASE = 10000.0
D_IN = 7584

ADAM_LR, ADAM_B1, ADAM_B2, ADAM_EPS, ADAM_WD, ADAM_STEP = 0.001, 0.9, 0.999, 1e-08, 0.01, 10

P_QC, P_KC, P_VC, P_QD, P_KVD, P_KR, P_ZB, P_ZC, P_G, P_U, P_V, P_ZA, P_W = (
    0, 512, 1024, 1536, 1792, 1920, 2048, 2560, 3072, 6144, 6656, 7168, 7680)
NAT_SEGS = [(0, 1536, P_U), (1536, 1920, P_QD), (1920, 1952, P_KR + MLA_NOPE), (1952, 2464, P_ZB),
            (2464, 4000, P_QC), (4000, 4512, P_ZC), (4512, 7584, P_G)]
SHARD_COLS = D_IN // 4

VMEM_LIMIT = 48 * 1024 * 1024
ATT_T = 512
MLA_HEADS_PER_STEP = 2
MLA_FWD_HEADS_PER_STEP = 4
ROW_TILE = 256
MM_TM = 512
MM_TN = 1536
LOG2E = 1.4426950408889634
MLA_SCALE = MLA_QK ** -0.5
CA_SCALE = CA_HEAD_DIM ** -0.5

WEIGHTS = ['w_in', 'pre_g', 'post_g', 'sgu_ln_g', 'sgu_ln_b', 'sgu_w', 'sgu_b', 'mla_q_norm_g',
           'mla_kv_norm_g', 'mla_w_uq', 'mla_w_ukv', 'ca_rel_bias', 'w_branch', 'gate_b', 'w_out']
SHARDED = ['w_in', 'mla_w_uq', 'mla_w_ukv', 'w_branch', 'gate_b', 'w_out']
SMALL = ['pre_g', 'post_g', 'sgu_ln_g', 'sgu_ln_b', 'sgu_w', 'sgu_b', 'mla_q_norm_g',
         'mla_kv_norm_g', 'ca_rel_bias']
SHARD_AXIS = {'w_in': 1, 'mla_w_uq': 1, 'mla_w_ukv': 1, 'w_branch': 2, 'gate_b': 1, 'w_out': 0}


def _call(body, **kw):
    return pl.pallas_call(body, **kw)


def _params(n_axes):
    return pltpu.CompilerParams(dimension_semantics=("arbitrary",) * n_axes,
                                vmem_limit_bytes=VMEM_LIMIT)


def _nt(a, b):
    return lax.dot_general(a, b, (((1,), (1,)), ((), ())), preferred_element_type=F32)


def _nn(a, b):
    return jnp.dot(a, b, preferred_element_type=F32)


def _tn(a, b):
    return lax.dot_general(a, b, (((0,), (0,)), ((), ())), preferred_element_type=F32)


def _rms(xv, g):
    r = lax.rsqrt(jnp.mean(xv * xv, axis=-1, keepdims=True) + EPS)
    return xv * r * g, r


def _rms_bwd(xv, g, r, dy):
    gy = dy * g
    dx = r * gy - xv * (r * r * r) * jnp.mean(xv * gy, axis=-1, keepdims=True)
    dg = jnp.sum(dy * (xv * r), axis=0, keepdims=True)
    return dx, dg


def _sigmoid(z):
    return 1.0 / (1.0 + jnp.exp(-z))


def _rope(xv, c, a, b):
    return xv * c + pltpu.roll(xv, 112, 1) * a + pltpu.roll(xv, 16, 1) * b


def _accumulate(ref, val, first):
    @pl.when(first)
    def _():
        ref[...] = val

    @pl.when(jnp.logical_not(first))
    def _():
        ref[...] += val


def norm_matmul(x, g, w):
    S, D = x.shape
    N = w.shape[1]
    tm, tn = min(S, 2 * MM_TM), MM_TN

    def body(x_ref, g_ref, w_ref, o_ref, xn_ref):
        @pl.when(pl.program_id(1) == 0)
        def _():
            y, _ = _rms(x_ref[...], g_ref[...])
            xn_ref[...] = y.astype(BF16)

        o_ref[...] = _nn(xn_ref[...], w_ref[...])

    return _call(
        body, name="norm_matmul", grid=(S // tm, N // tn),
        in_specs=[pl.BlockSpec((tm, D), lambda i, j: (i, 0)),
                  pl.BlockSpec((1, D), lambda i, j: (0, 0)),
                  pl.BlockSpec((D, tn), lambda i, j: (0, j))],
        out_specs=[pl.BlockSpec((tm, tn), lambda i, j: (i, j)),
                   pl.BlockSpec((tm, D), lambda i, j: (i, 0))],
        out_shape=[jax.ShapeDtypeStruct((S, N), F32), jax.ShapeDtypeStruct((S, D), BF16)],
        compiler_params=_params(2))(x, g, w)


def proj_bwd_x(dproj, w, x, g, resid):
    S, N = dproj.shape
    D = x.shape[1]
    tm, tk = min(S, MM_TM), MM_TN
    nk = N // tk

    def body(dp_ref, w_ref, x_ref, g_ref, r_ref, dx_ref, dg_ref, acc_ref):
        i, k = pl.program_id(0), pl.program_id(1)

        @pl.when(k == 0)
        def _():
            acc_ref[...] = jnp.zeros_like(acc_ref)

        acc_ref[...] += _nt(dp_ref[...].astype(BF16), w_ref[...])

        @pl.when(k == nk - 1)
        def _():
            xv = x_ref[...]
            _, r = _rms(xv, g_ref[...])
            dx, dg = _rms_bwd(xv, g_ref[...], r, acc_ref[...])
            dx_ref[...] = dx + r_ref[...]
            _accumulate(dg_ref, dg, i == 0)

    return _call(
        body, name="proj_bwd_x", grid=(S // tm, nk),
        in_specs=[pl.BlockSpec((tm, tk), lambda i, k: (i, k)),
                  pl.BlockSpec((D, tk), lambda i, k: (0, k)),
                  pl.BlockSpec((tm, D), lambda i, k: (i, 0)),
                  pl.BlockSpec((1, D), lambda i, k: (0, 0)),
                  pl.BlockSpec((tm, D), lambda i, k: (i, 0))],
        out_specs=[pl.BlockSpec((tm, D), lambda i, k: (i, 0)),
                   pl.BlockSpec((1, D), lambda i, k: (0, 0))],
        out_shape=[jax.ShapeDtypeStruct((S, D), F32), jax.ShapeDtypeStruct((1, D), F32)],
        scratch_shapes=[pltpu.VMEM((tm, D), F32)],
        compiler_params=_params(2))(dproj, w, x, g, resid)


def matmul_tn(a, b, tn):
    S, M = a.shape
    N = b.shape[1]
    tk = min(S, 2 * MM_TM)

    def body(a_ref, b_ref, o_ref):
        @pl.when(pl.program_id(1) == 0)
        def _():
            o_ref[...] = jnp.zeros_like(o_ref)

        o_ref[...] += _tn(a_ref[...].astype(BF16), b_ref[...].astype(BF16))

    return _call(
        body, name="matmul_tn", grid=(N // tn, S // tk),
        in_specs=[pl.BlockSpec((tk, M), lambda j, k: (k, 0)),
                  pl.BlockSpec((tk, tn), lambda j, k: (k, j))],
        out_specs=pl.BlockSpec((M, tn), lambda j, k: (0, j)),
        out_shape=jax.ShapeDtypeStruct((M, N), F32),
        compiler_params=_params(2))(a, b)


def _sgu_block(vv, g, b, ws_ref, lane):
    mu = jnp.mean(vv, axis=-1, keepdims=True)
    xc = vv - mu
    r = lax.rsqrt(jnp.mean(xc * xc, axis=-1, keepdims=True) + EPS)
    xhat = xc * r
    vln = (xhat * g + b).astype(BF16)
    pieces = []
    for p in range(4):
        vp = vln[:, p * 128:(p + 1) * 128]
        pieces.append(jnp.where(lane < 64, _nn(ws_ref[2 * p], vp), _nn(ws_ref[2 * p + 1], vp)))
    return xhat, r, vln, jnp.concatenate(pieces, axis=1)


def sgu_fwd(proj, ln_g, ln_b, ws, bias_full):
    S = proj.shape[0]
    ts = ROW_TILE

    def body(u_ref, v_ref, z_ref, g_ref, b_ref, ws_ref, bf_ref, y_ref):
        lane = lax.broadcasted_iota(jnp.int32, (SGU_BLOCK, 128), 1)
        for blk in range(ts // SGU_BLOCK):
            rows = slice(blk * SGU_BLOCK, (blk + 1) * SGU_BLOCK)
            _, _, _, mixed = _sgu_block(v_ref[rows, :], g_ref[...], b_ref[...], ws_ref, lane)
            mixed = mixed + bf_ref[...]
            zz = z_ref[rows, :]
            y_ref[rows, :] = (u_ref[rows, :] * mixed * (zz * _sigmoid(zz))).astype(BF16)

    col = lambda c: pl.BlockSpec((ts, BR_WIDTH), lambda i: (i, c))
    full = lambda shape: pl.BlockSpec(shape, lambda i: (0,) * len(shape))
    return _call(
        body, name="sgu_fwd", grid=(S // ts,),
        in_specs=[col(P_U // 512), col(P_V // 512), col(P_ZA // 512),
                  full((1, BR_WIDTH)), full((1, BR_WIDTH)), full((8, 128, 128)), full((128, BR_WIDTH))],
        out_specs=pl.BlockSpec((ts, BR_WIDTH), lambda i: (i, 0)),
        out_shape=jax.ShapeDtypeStruct((S, BR_WIDTH), BF16),
        compiler_params=_params(1))(proj, proj, proj, ln_g, ln_b, ws, bias_full)


def sgu_bwd(dproj, dy, proj, ln_g, ln_b, ws, ws_t, bias_full):
    S = proj.shape[0]
    ts = ROW_TILE

    def body(dp_in, dy_ref, u_ref, v_ref, z_ref, g_ref, b_ref, ws_ref, wst_ref, bf_ref,
             dp_ref, gg_ref, gb_ref, gws_ref, gbf_ref):
        del dp_in
        first = pl.program_id(0) == 0

        @pl.when(first)
        def _():
            gg_ref[...] = jnp.zeros_like(gg_ref)
            gb_ref[...] = jnp.zeros_like(gb_ref)
            gws_ref[...] = jnp.zeros_like(gws_ref)
            gbf_ref[...] = jnp.zeros_like(gbf_ref)

        lane = lax.broadcasted_iota(jnp.int32, (SGU_BLOCK, 128), 1)
        for blk in range(ts // SGU_BLOCK):
            rows = slice(blk * SGU_BLOCK, (blk + 1) * SGU_BLOCK)
            g = g_ref[...]
            xhat, r, vln, mixed = _sgu_block(v_ref[rows, :], g, b_ref[...], ws_ref, lane)
            mixed = mixed + bf_ref[...]
            zz = z_ref[rows, :]
            uu = u_ref[rows, :]
            dyv = dy_ref[0, rows, :]
            sg = _sigmoid(zz)
            sil = zz * sg
            dmixed = dyv * uu * sil
            dp_ref[rows, 0:512] = (dyv * mixed * sil).astype(BF16)
            dp_ref[rows, 1024:1536] = (dyv * uu * mixed * (sg * (1.0 + zz * (1.0 - sg)))).astype(BF16)
            gbf_ref[...] += dmixed
            dmb = dmixed.astype(BF16)
            pieces = []
            for p in range(4):
                dmp = dmb[:, p * 128:(p + 1) * 128]
                vp = vln[:, p * 128:(p + 1) * 128]
                pieces.append(jnp.where(lane < 64, _nn(wst_ref[2 * p], dmp), _nn(wst_ref[2 * p + 1], dmp)))
                zero = jnp.zeros_like(dmp)
                gws_ref[2 * p] += _nt(jnp.where(lane < 64, dmp, zero), vp)
                gws_ref[2 * p + 1] += _nt(jnp.where(lane >= 64, dmp, zero), vp)
            dvln = jnp.concatenate(pieces, axis=1)
            dxh = dvln * g
            dp_ref[rows, 512:1024] = (r * (dxh - jnp.mean(dxh, axis=-1, keepdims=True)
                                           - xhat * jnp.mean(dxh * xhat, axis=-1, keepdims=True))).astype(BF16)
            gg_ref[...] += jnp.sum(dvln * xhat, axis=0, keepdims=True)
            gb_ref[...] += jnp.sum(dvln, axis=0, keepdims=True)

    col = lambda c: pl.BlockSpec((ts, BR_WIDTH), lambda i: (i, c))
    full = lambda shape: pl.BlockSpec(shape, lambda i: (0,) * len(shape))
    return _call(
        body, name="sgu_bwd", grid=(S // ts,),
        in_specs=[pl.BlockSpec(memory_space=pl.ANY),
                  pl.BlockSpec((1, ts, BR_WIDTH), lambda i: (0, i, 0)),
                  col(P_U // 512), col(P_V // 512), col(P_ZA // 512),
                  full((1, BR_WIDTH)), full((1, BR_WIDTH)), full((8, 128, 128)), full((8, 128, 128)),
                  full((128, BR_WIDTH))],
        out_specs=[pl.BlockSpec((ts, 1536), lambda i: (i, P_U // 1536)),
                   full((1, BR_WIDTH)), full((1, BR_WIDTH)), full((8, 128, 128)), full((128, BR_WIDTH))],
        out_shape=[jax.ShapeDtypeStruct(dproj.shape, BF16),
                   jax.ShapeDtypeStruct((1, BR_WIDTH), F32), jax.ShapeDtypeStruct((1, BR_WIDTH), F32),
                   jax.ShapeDtypeStruct((8, 128, 128), F32), jax.ShapeDtypeStruct((128, BR_WIDTH), F32)],
        input_output_aliases={0: 0},
        compiler_params=_params(1))(dproj, dy, proj, proj, proj, ln_g, ln_b, ws, ws_t, bias_full)


def _hspec(ts):
    return pl.BlockSpec((N_HEADS, ts, HEAD_PAD), lambda i: (0, i, 0))


def prep_fwd(proj, tab, qg, kvg, wq, wk, wv):
    S = proj.shape[0]
    ts = ROW_TILE

    def body(qc_ref, kc_ref, vc_ref, qd_ref, kvd_ref, kr_ref, tab_ref, qg_ref, kvg_ref,
             wq_ref, wk_ref, wv_ref, qb, kb, vb, qc, kc, vc, kbt, vbt, kct, vct, cq_o, ckv_o):
        c, a, b = tab_ref[0], tab_ref[1], tab_ref[2]
        cq, _ = _rms(qd_ref[...], qg_ref[...])
        ckv, _ = _rms(kvd_ref[...], kvg_ref[...])
        cqb, ckvb = cq.astype(BF16), ckv.astype(BF16)
        cq_o[...] = cqb
        ckv_o[...] = ckvb
        krr = _rope(kr_ref[...], c, a, b)
        lane = lax.broadcasted_iota(jnp.int32, (ts, 128), 1)
        ones_lane = jnp.where(lane == MLA_V, 1.0, 0.0)
        for h in range(N_HEADS):
            cols = slice(h * HEAD_PAD, (h + 1) * HEAD_PAD)
            qb[h] = (_rope(_nn(cqb, wq_ref[:, cols]), c, a, b) * (MLA_SCALE * LOG2E)).astype(BF16)
            kh = _nn(ckvb, wk_ref[:, cols]) + krr
            vh = _nn(ckvb, wv_ref[:, cols]) + ones_lane
            kb[h], kbt[h] = kh.astype(BF16), kh.T.astype(BF16)
            vb[h], vbt[h] = vh.astype(BF16), vh.T.astype(BF16)
        for p in range(4):
            piece = qc_ref[:, p * 128:(p + 1) * 128] * (CA_SCALE * LOG2E)
            qc[2 * p] = jnp.where(lane < 64, piece, 0.0).astype(BF16)
            qc[2 * p + 1] = jnp.where(lane < 64, pltpu.roll(piece, 64, 1), 0.0).astype(BF16)
            for src, dst, dst_t, pad in ((kc_ref, kc, kct, 0.0), (vc_ref, vc, vct, ones_lane)):
                piece = src[:, p * 128:(p + 1) * 128]
                for h, head in ((2 * p, jnp.where(lane < 64, piece, pad)),
                                (2 * p + 1, jnp.where(lane < 64, pltpu.roll(piece, 64, 1), pad))):
                    dst[h], dst_t[h] = head.astype(BF16), head.T.astype(BF16)

    col = lambda w, c: pl.BlockSpec((ts, w), lambda i: (i, c))
    full = lambda shape: pl.BlockSpec(shape, lambda i: (0,) * len(shape))
    hshape = jax.ShapeDtypeStruct((N_HEADS, S, HEAD_PAD), BF16)
    tshape = jax.ShapeDtypeStruct((N_HEADS, HEAD_PAD, S), BF16)
    tspec = pl.BlockSpec((N_HEADS, HEAD_PAD, ts), lambda i: (0, 0, i))
    return _call(
        body, name="prep_fwd", grid=(S // ts,),
        in_specs=[col(512, P_QC // 512), col(512, P_KC // 512), col(512, P_VC // 512),
                  col(256, P_QD // 256), col(128, P_KVD // 128), col(128, P_KR // 128),
                  pl.BlockSpec((3, ts, 128), lambda i: (0, i, 0)),
                  full((1, MLA_Q_RANK)), full((1, MLA_KV_RANK)),
                  full((MLA_Q_RANK, 1024)), full((MLA_KV_RANK, 1024)), full((MLA_KV_RANK, 1024))],
        out_specs=[_hspec(ts)] * 6 + [tspec] * 4 + [pl.BlockSpec((ts, MLA_Q_RANK), lambda i: (i, 0)),
                                                    pl.BlockSpec((ts, MLA_KV_RANK), lambda i: (i, 0))],
        out_shape=[hshape] * 6 + [tshape] * 4 + [jax.ShapeDtypeStruct((S, MLA_Q_RANK), BF16),
                                                 jax.ShapeDtypeStruct((S, MLA_KV_RANK), BF16)],
        compiler_params=_params(1))(proj, proj, proj, proj, proj, proj, tab, qg, kvg, wq, wk, wv)


def prep_bwd(dproj, dqb, dkb, dvb, dqc, dkc, dvc, proj, tab, qg, kvg, wq, wk, wv):
    S = proj.shape[0]
    ts = ROW_TILE

    def body(dp_in, dqb_r, dkb_r, dvb_r, dqc_r, dkc_r, dvc_r, qd_ref, kvd_ref, tab_ref, qg_ref, kvg_ref,
             wq_ref, wk_ref, wv_ref, dp_ref, dqf, dkf, dvf, gq_ref, gkv_ref):
        del dp_in
        c, a, b = tab_ref[0], -tab_ref[1], -tab_ref[2]
        qd, kvd = qd_ref[...], kvd_ref[...]
        _, rq = _rms(qd, qg_ref[...])
        _, rkv = _rms(kvd, kvg_ref[...])
        dcq = jnp.zeros((ts, MLA_Q_RANK), F32)
        dckv = jnp.zeros((ts, MLA_KV_RANK), F32)
        dksum = jnp.zeros((ts, HEAD_PAD), F32)
        for h in range(N_HEADS):
            cols = slice(h * HEAD_PAD, (h + 1) * HEAD_PAD)
            dqh = _rope(dqb_r[h] * MLA_SCALE, c, a, b).astype(BF16)
            dqf[:, cols] = dqh
            dcq = dcq + _nt(dqh, wq_ref[:, cols])
            dk = dkb_r[h] * (1.0 / LOG2E)
            dksum = dksum + dk
            dkh = dk.astype(BF16)
            dkf[:, cols] = dkh
            dvh = dvb_r[h].astype(BF16)
            dvf[:, cols] = dvh
            dckv = dckv + _nt(dkh, wk_ref[:, cols]) + _nt(dvh, wv_ref[:, cols])
        lane = lax.broadcasted_iota(jnp.int32, (ts, 128), 1)
        rope_lanes = jnp.logical_and(lane >= MLA_NOPE, lane < MLA_QK)
        dp_ref[:, P_KR:P_KR + 128] = jnp.where(rope_lanes, _rope(dksum, c, a, b), 0.0).astype(BF16)
        dqd, gq = _rms_bwd(qd, qg_ref[...], rq, dcq)
        dkvd, gkv = _rms_bwd(kvd, kvg_ref[...], rkv, dckv)
        dp_ref[:, P_QD:P_QD + 256] = dqd.astype(BF16)
        dp_ref[:, P_KVD:P_KVD + 128] = dkvd.astype(BF16)
        first = pl.program_id(0) == 0
        _accumulate(gq_ref, gq, first)
        _accumulate(gkv_ref, gkv, first)
        for src, base, factor in ((dqc_r, P_QC, CA_SCALE), (dkc_r, P_KC, 1.0 / LOG2E), (dvc_r, P_VC, 1.0)):
            for p in range(4):
                dp_ref[:, base + p * 128:base + (p + 1) * 128] = (
                    (src[2 * p] + pltpu.roll(src[2 * p + 1], 64, 1)) * factor).astype(BF16)

    col = lambda w, c: pl.BlockSpec((ts, w), lambda i: (i, c))
    full = lambda shape: pl.BlockSpec(shape, lambda i: (0,) * len(shape))
    wide = jax.ShapeDtypeStruct((S, 1024), BF16)
    return _call(
        body, name="prep_bwd", grid=(S // ts,),
        in_specs=[pl.BlockSpec(memory_space=pl.ANY)] + [_hspec(ts)] * 6 +
                 [col(256, P_QD // 256), col(128, P_KVD // 128),
                  pl.BlockSpec((3, ts, 128), lambda i: (0, i, 0)),
                  full((1, MLA_Q_RANK)), full((1, MLA_KV_RANK)),
                  full((MLA_Q_RANK, 1024)), full((MLA_KV_RANK, 1024)), full((MLA_KV_RANK, 1024))],
        out_specs=[pl.BlockSpec((ts, 2048), lambda i: (i, 0))] + [pl.BlockSpec((ts, 1024), lambda i: (i, 0))] * 3 +
                  [full((1, MLA_Q_RANK)), full((1, MLA_KV_RANK))],
        out_shape=[jax.ShapeDtypeStruct(dproj.shape, BF16), wide, wide, wide,
                   jax.ShapeDtypeStruct((1, MLA_Q_RANK), F32), jax.ShapeDtypeStruct((1, MLA_KV_RANK), F32)],
        input_output_aliases={0: 0},
        compiler_params=_params(1))(dproj, dqb, dkb, dvb, dqc, dkc, dvc, proj, proj, tab, qg, kvg, wq, wk, wv)


def _diag_visible(t):
    r = lax.broadcasted_iota(jnp.int32, (t, t), 0) >> CHUNK_SHIFT
    c = lax.broadcasted_iota(jnp.int32, (t, t), 1) >> CHUNK_SHIFT
    return r <= c


def _pair_tables(nq, kv_major):
    if kv_major:
        pairs = [(kb, qi) for kb in range(nq) for qi in range(kb, nq)]
    else:
        pairs = [(kb, qi) for qi in range(nq) for kb in range(qi + 1)]
    return (jnp.asarray(np.array([p[0] for p in pairs], np.int32)),
            jnp.asarray(np.array([p[1] for p in pairs], np.int32)), len(pairs))


def _finish_softmax(acc, m):
    l = acc[MLA_V:MLA_V + 1, :]
    row = lax.broadcasted_iota(jnp.int32, acc.shape, 0)
    return jnp.where(row < MLA_V, acc / l, 0.0).T, m + jnp.log2(l)


def _split_refs(refs, counts):
    out, pos = [], 0
    for c in counts:
        out.append(refs[pos:pos + c])
        pos += c
    return out


def mla_fwd(q, k, vt, exchange=()):
    H, S, _ = q.shape
    t, hb, n_ex = ATT_T, MLA_FWD_HEADS_PER_STEP, len(exchange)
    kb_tab, qi_tab, n_pairs = _pair_tables(S // t, False)

    def body(kb_ref, qi_ref, q_ref, k_ref, vt_ref, *rest):
        ex_src, (o_ref, lse_ref), ex_out, (m_s, acc_s), ex_sems = _split_refs(rest, (n_ex, 2, n_ex, 2, 3 if n_ex else 0))
        hg, p_id = pl.program_id(0), pl.program_id(1)
        kb, qi = kb_ref[p_id], qi_ref[p_id]

        if n_ex:
            @pl.when(jnp.logical_and(hg == 0, p_id == 0))
            def _():
                _exchange_issue(ex_src, ex_out, ex_sems, False, True)

        @pl.when(kb == 0)
        def _():
            m_s[...] = jnp.full_like(m_s, NEG_INF)
            acc_s[...] = jnp.zeros_like(acc_s)

        def step(masked):
            for h in range(hb):
                st = _nt(k_ref[h], q_ref[h])
                if masked:
                    st = jnp.where(_diag_visible(t), st, NEG_INF)
                m_prev = m_s[h]
                m_new = jnp.maximum(m_prev, jnp.max(st, axis=0, keepdims=True))
                p = jnp.exp2(st - m_new)
                acc_s[h] = jnp.exp2(m_prev - m_new) * acc_s[h] + _nn(vt_ref[h], p.astype(BF16))
                m_s[h] = m_new

        @pl.when(kb < qi)
        def _():
            step(False)

        @pl.when(kb == qi)
        def _():
            step(True)
            for h in range(hb):
                o_ref[h], lse_ref[h] = _finish_softmax(acc_s[h], m_s[h])

        if n_ex:
            @pl.when(jnp.logical_and(hg == H // hb - 1, p_id == n_pairs - 1))
            def _():
                _exchange_issue(ex_src, ex_out, ex_sems, False, False)

    grid_spec = pltpu.PrefetchScalarGridSpec(
        num_scalar_prefetch=2, grid=(H // hb, n_pairs),
        in_specs=[pl.BlockSpec((hb, t, HEAD_PAD), lambda h, p, kb, qi: (h, qi[p], 0)),
                  pl.BlockSpec((hb, t, HEAD_PAD), lambda h, p, kb, qi: (h, kb[p], 0)),
                  pl.BlockSpec((hb, HEAD_PAD, t), lambda h, p, kb, qi: (h, 0, kb[p]))] + [ANY] * n_ex,
        out_specs=[pl.BlockSpec((hb, t, HEAD_PAD), lambda h, p, kb, qi: (h, qi[p], 0)),
                   pl.BlockSpec((hb, 1, t), lambda h, p, kb, qi: (h, 0, qi[p]))] + [ANY] * n_ex,
        scratch_shapes=[pltpu.VMEM((hb, 1, t), F32), pltpu.VMEM((hb, HEAD_PAD, t), F32)] + _exchange_sems(n_ex))
    outs = _call(
        body, name="mla_fwd_gather" if n_ex else "mla_fwd", grid_spec=grid_spec,
        out_shape=[jax.ShapeDtypeStruct((H, S, HEAD_PAD), F32), jax.ShapeDtypeStruct((H, 1, S), F32)] +
        _exchange_out_shape(exchange, False),
        compiler_params=_params(2))(kb_tab, qi_tab, q, k, vt, *exchange)
    return outs[0], outs[1], outs[2:]


def mla_bwd(q, k, kt, v, do, lse, delta, exchange=()):
    H, S, _ = q.shape
    t, hb, n_ex = ATT_T, MLA_HEADS_PER_STEP, len(exchange)
    nq = S // t
    kb_tab, qi_tab, n_pairs = _pair_tables(nq, True)

    def body(kb_ref, qi_ref, q_ref, k_ref, kt_ref, v_ref, do_ref, lse_ref, dl_ref, *rest):
        ex_src, (dqt_ref, dk_ref, dv_ref), ex_out, (dk_s, dv_s), ex_sems = _split_refs(rest, (n_ex, 3, n_ex, 2, 3 if n_ex else 0))
        hg, p_id = pl.program_id(0), pl.program_id(1)
        kb, qi = kb_ref[p_id], qi_ref[p_id]

        if n_ex:
            @pl.when(jnp.logical_and(hg == 0, p_id == 0))
            def _():
                _exchange_issue(ex_src, ex_out, ex_sems, True, True)

        @pl.when(p_id == 0)
        def _():
            dqt_ref[...] = jnp.zeros_like(dqt_ref)

        @pl.when(qi == kb)
        def _():
            dk_s[...] = jnp.zeros_like(dk_s)
            dv_s[...] = jnp.zeros_like(dv_s)

        def step(masked):
            for h in range(hb):
                st = _nt(k_ref[h], q_ref[h])
                if masked:
                    st = jnp.where(_diag_visible(t), st, NEG_INF)
                pt = jnp.exp2(st - lse_ref[h])
                dv_s[h] += _nn(pt.astype(BF16), do_ref[h])
                dsb = (pt * (_nt(v_ref[h], do_ref[h]) - dl_ref[h])).astype(BF16)
                dk_s[h] += _nn(dsb, q_ref[h])
                dqt_ref[h, qi] += _nn(kt_ref[h], dsb)

        @pl.when(qi == kb)
        def _():
            step(True)

        @pl.when(qi > kb)
        def _():
            step(False)

        @pl.when(qi == nq - 1)
        def _():
            dk_ref[...] = dk_s[...]
            dv_ref[...] = dv_s[...]

        if n_ex:
            @pl.when(jnp.logical_and(hg == H // hb - 1, p_id == n_pairs - 1))
            def _():
                _exchange_issue(ex_src, ex_out, ex_sems, True, False)

    qtile = pl.BlockSpec((hb, t, HEAD_PAD), lambda h, p, kb, qi: (h, qi[p], 0))
    ktile = pl.BlockSpec((hb, t, HEAD_PAD), lambda h, p, kb, qi: (h, kb[p], 0))
    stat = pl.BlockSpec((hb, 1, t), lambda h, p, kb, qi: (h, 0, qi[p]))
    grid_spec = pltpu.PrefetchScalarGridSpec(
        num_scalar_prefetch=2, grid=(H // hb, n_pairs),
        in_specs=[qtile, ktile, pl.BlockSpec((hb, HEAD_PAD, t), lambda h, p, kb, qi: (h, 0, kb[p])), ktile, qtile,
                  stat, stat] + [ANY] * n_ex,
        out_specs=[pl.BlockSpec((hb, nq, HEAD_PAD, t), lambda h, p, kb, qi: (h, 0, 0, 0)), ktile, ktile] +
        [ANY] * n_ex,
        scratch_shapes=[pltpu.VMEM((hb, t, HEAD_PAD), F32), pltpu.VMEM((hb, t, HEAD_PAD), F32)] +
        _exchange_sems(n_ex))
    outs = _call(
        body, name="mla_bwd_scatter" if n_ex else "mla_bwd", grid_spec=grid_spec,
        out_shape=[jax.ShapeDtypeStruct((H, nq, HEAD_PAD, t), F32),
                   jax.ShapeDtypeStruct((H, S, HEAD_PAD), F32), jax.ShapeDtypeStruct((H, S, HEAD_PAD), F32)] +
        _exchange_out_shape(exchange, True),
        compiler_params=_params(2))(kb_tab, qi_tab, q, k, kt, v, do, lse, delta, *exchange)
    return outs[0], outs[1], outs[2], outs[3:]


def _band_specs(t, hb):
    prev = lambda i: jnp.maximum(i - 1, 0)
    return dict(
        cur=pl.BlockSpec((hb, t, HEAD_PAD), lambda h, i: (h, i, 0)),
        prev=pl.BlockSpec((hb, t, HEAD_PAD), lambda h, i: (h, prev(i), 0)),
        cur_t=pl.BlockSpec((hb, HEAD_PAD, t), lambda h, i: (h, 0, i)),
        prev_t=pl.BlockSpec((hb, HEAD_PAD, t), lambda h, i: (h, 0, prev(i))),
        stat=pl.BlockSpec((hb, 1, t), lambda h, i: (h, 0, i)),
        bias_prev=pl.BlockSpec((hb, 1, t, t), lambda h, i: (h, jnp.where(i == 0, 1, 0), 0, 0)),
        bias_cur=pl.BlockSpec((hb, 1, t, t), lambda h, i: (h, 2, 0, 0)))


def band_fwd(q, k, vt, bias):
    H, S, _ = q.shape
    t, hb = ATT_T, MLA_HEADS_PER_STEP
    sp = _band_specs(t, hb)

    def body(q_ref, kp_ref, kc_ref, vtp_ref, vtc_ref, bp_ref, bc_ref, o_ref, lse_ref):
        for h in range(hb):
            s0 = _nt(kp_ref[h], q_ref[h]) + bp_ref[h, 0]
            s1 = _nt(kc_ref[h], q_ref[h]) + bc_ref[h, 0]
            m = jnp.maximum(jnp.max(s0, axis=0, keepdims=True), jnp.max(s1, axis=0, keepdims=True))
            ot = (_nn(vtp_ref[h], jnp.exp2(s0 - m).astype(BF16)) +
                  _nn(vtc_ref[h], jnp.exp2(s1 - m).astype(BF16)))
            o_ref[h], lse_ref[h] = _finish_softmax(ot, m)

    return _call(
        body, name="band_fwd", grid=(H // hb, S // t),
        in_specs=[sp['cur'], sp['prev'], sp['cur'], sp['prev_t'], sp['cur_t'], sp['bias_prev'], sp['bias_cur']],
        out_specs=[sp['cur'], sp['stat']],
        out_shape=[jax.ShapeDtypeStruct((H, S, HEAD_PAD), F32), jax.ShapeDtypeStruct((H, 1, S), F32)],
        compiler_params=_params(2))(q, k, k, vt, vt, bias, bias)


def band_bwd(q, k, kt, v, do, lse, delta, bias):
    H, S, _ = q.shape
    t = ATT_T
    sp = _band_specs(t, 1)

    def body(q_ref, kp_ref, kc_ref, ktp_ref, ktc_ref, vp_ref, vc_ref, do_ref, lse_ref, dl_ref, bp_ref, bc_ref,
             dqt_ref, dk_ref, dv_ref, db_ref):
        i = pl.program_id(1)

        @pl.when(i == 0)
        def _():
            dk_ref[...] = jnp.zeros_like(dk_ref)
            dv_ref[...] = jnp.zeros_like(dv_ref)
            db_ref[...] = jnp.zeros_like(db_ref)

        qv, dov = q_ref[0], do_ref[0]
        dqt = jnp.zeros((HEAD_PAD, t), F32)
        windows = ((0, jnp.maximum(i - 1, 0), kp_ref, ktp_ref, vp_ref, bp_ref),
                   (1, i, kc_ref, ktc_ref, vc_ref, bc_ref))
        for w, blk, k_ref, kt_ref, v_ref, b_ref in windows:
            rows = pl.ds(pl.multiple_of(blk * t, t), t)
            pt = jnp.exp2(_nt(k_ref[0], qv) + b_ref[0, 0] - lse_ref[0])
            dv_ref[0, rows, :] += _nn(pt.astype(BF16), dov)
            ds = pt * (_nt(v_ref[0], dov) - dl_ref[0])
            db_ref[0, w] += ds
            dsb = ds.astype(BF16)
            dk_ref[0, rows, :] += _nn(dsb, qv)
            dqt = dqt + _nn(kt_ref[0], dsb)
        dqt_ref[0] = dqt

    whole = pl.BlockSpec((1, S, HEAD_PAD), lambda h, i: (h, 0, 0))
    return _call(
        body, name="band_bwd", grid=(H, S // t),
        in_specs=[sp['cur'], sp['prev'], sp['cur'], sp['prev_t'], sp['cur_t'], sp['prev'], sp['cur'], sp['cur'],
                  sp['stat'], sp['stat'], sp['bias_prev'], sp['bias_cur']],
        out_specs=[sp['cur_t'], whole, whole, pl.BlockSpec((1, 2, t, t), lambda h, i: (h, 0, 0, 0))],
        out_shape=[jax.ShapeDtypeStruct((H, HEAD_PAD, S), F32), jax.ShapeDtypeStruct((H, S, HEAD_PAD), F32),
                   jax.ShapeDtypeStruct((H, S, HEAD_PAD), F32), jax.ShapeDtypeStruct((H, 2, t, t), F32)],
        compiler_params=_params(2))(q, k, k, kt, kt, v, v, do, lse, delta, bias, bias)


def _compact(o_ref):
    return jnp.concatenate([o_ref[2 * p] + pltpu.roll(o_ref[2 * p + 1], 64, 1) for p in range(4)], axis=1)


def merge_fwd(ob, oc, proj, ya, gate_b, wbr, w_out, x, post_g):
    S = x.shape[0]
    ts = ROW_TILE

    def body(ob_ref, oc_ref, zb_ref, zc_ref, ya_ref, gl_ref, gb_ref, wbr_ref, wo_ref, x_ref, pg_ref,
             xo_ref, yb_ref, yc_ref, mg_ref, out_ref):
        zb, zc = zb_ref[...], zc_ref[...]
        yb = (_compact(ob_ref) * (zb * _sigmoid(zb))).astype(BF16)
        yc = (_compact(oc_ref) * (zc * _sigmoid(zc))).astype(BF16)
        yb_ref[...] = yb
        yc_ref[...] = yc
        merged = jnp.zeros((ts, D_MODEL), F32)
        for n, y in enumerate((ya_ref[...], yb, yc)):
            cols = slice(n * D_MODEL, (n + 1) * D_MODEL)
            gate = _sigmoid(gl_ref[:, cols] + gb_ref[:, cols])
            merged = merged + gate * _nn(y, wbr_ref[n])
        mb = merged.astype(BF16)
        mg_ref[...] = mb
        out = _nn(mb, wo_ref[...])
        out_ref[...] = out
        normed, _ = _rms(out, pg_ref[...])
        xo_ref[...] = x_ref[...] + normed

    row = lambda w: pl.BlockSpec((ts, w), lambda i: (i, 0))
    col = lambda w, c: pl.BlockSpec((ts, w), lambda i: (i, c))
    full = lambda shape: pl.BlockSpec(shape, lambda i: (0,) * len(shape))
    return _call(
        body, name="merge_fwd", grid=(S // ts,),
        in_specs=[_hspec(ts), _hspec(ts), col(512, P_ZB // 512), col(512, P_ZC // 512), row(512),
                  col(3072, P_G // 3072), full((1, 3072)), full((3, BR_WIDTH, D_MODEL)),
                  full((D_MODEL, D_MODEL)), row(D_MODEL), full((1, D_MODEL))],
        out_specs=[row(D_MODEL), row(512), row(512), row(D_MODEL), row(D_MODEL)],
        out_shape=[jax.ShapeDtypeStruct((S, D_MODEL), F32), jax.ShapeDtypeStruct((S, 512), BF16),
                   jax.ShapeDtypeStruct((S, 512), BF16), jax.ShapeDtypeStruct((S, D_MODEL), BF16),
                   jax.ShapeDtypeStruct((S, D_MODEL), F32)],
        compiler_params=_params(1))(ob, oc, proj, proj, ya, proj, gate_b, wbr, w_out, x, post_g)


def post_bwd(g, out, post_g, w_out):
    S = g.shape[0]
    ts = ROW_TILE

    def body(g_ref, out_ref, pg_ref, wo_ref, do_ref, dm_ref, gp_ref):
        ov = out_ref[...]
        _, r = _rms(ov, pg_ref[...])
        dout, gp = _rms_bwd(ov, pg_ref[...], r, g_ref[...])
        db = dout.astype(BF16)
        do_ref[...] = db
        dm_ref[...] = _nt(db, wo_ref[...])
        _accumulate(gp_ref, gp, pl.program_id(0) == 0)

    row = lambda: pl.BlockSpec((ts, D_MODEL), lambda i: (i, 0))
    full = lambda shape: pl.BlockSpec(shape, lambda i: (0,) * len(shape))
    return _call(
        body, name="post_bwd", grid=(S // ts,),
        in_specs=[row(), row(), full((1, D_MODEL)), full((D_MODEL, D_MODEL))],
        out_specs=[row(), row(), full((1, D_MODEL))],
        out_shape=[jax.ShapeDtypeStruct((S, D_MODEL), BF16), jax.ShapeDtypeStruct((S, D_MODEL), F32),
                   jax.ShapeDtypeStruct((1, D_MODEL), F32)],
        compiler_params=_params(1))(g, out, post_g, w_out)


def gate_bwd(dmerged, proj, gate_b, ya, yb, yc, wbr):
    S = dmerged.shape[0]
    ts = ROW_TILE

    def body(dm_ref, gl_ref, gb_ref, ya_ref, yb_ref, yc_ref, wbr_ref,
             dp_ref, dba_ref, dbb_ref, dbc_ref, dy_ref, ggb_ref):
        dm = dm_ref[...]
        ggb = []
        for n, (y_ref, dbr_ref) in enumerate(((ya_ref, dba_ref), (yb_ref, dbb_ref), (yc_ref, dbc_ref))):
            cols = slice(n * D_MODEL, (n + 1) * D_MODEL)
            br = _nn(y_ref[...], wbr_ref[n])
            sg = _sigmoid(gl_ref[:, cols] + gb_ref[:, cols])
            dgl = dm * br * (sg * (1.0 - sg))
            dp_ref[:, cols] = dgl.astype(BF16)
            ggb.append(jnp.sum(dgl, axis=0, keepdims=True))
            dbr = (dm * sg).astype(BF16)
            dbr_ref[...] = dbr
            dy_ref[n] = _nt(dbr, wbr_ref[n])
        _accumulate(ggb_ref, jnp.concatenate(ggb, axis=1), pl.program_id(0) == 0)

    row = lambda w: pl.BlockSpec((ts, w), lambda i: (i, 0))
    full = lambda shape: pl.BlockSpec(shape, lambda i: (0,) * len(shape))
    wide = jax.ShapeDtypeStruct((S, D_MODEL), BF16)
    return _call(
        body, name="gate_bwd", grid=(S // ts,),
        in_specs=[row(D_MODEL), pl.BlockSpec((ts, 3072), lambda i: (i, P_G // 3072)), full((1, 3072)),
                  row(512), row(512), row(512), full((3, BR_WIDTH, D_MODEL))],
        out_specs=[pl.BlockSpec((ts, 3072), lambda i: (i, P_G // 3072)), row(D_MODEL), row(D_MODEL), row(D_MODEL),
                   pl.BlockSpec((3, ts, 512), lambda i: (0, i, 0)), full((1, 3072))],
        out_shape=[jax.ShapeDtypeStruct((S, P_W), BF16), wide, wide, wide,
                   jax.ShapeDtypeStruct((3, S, 512), F32), jax.ShapeDtypeStruct((1, 3072), F32)],
        compiler_params=_params(1))(dmerged, proj, gate_b, ya, yb, yc, wbr)


def ungate_bwd(dproj, dy, ob, oc, proj):
    S = proj.shape[0]
    ts = ROW_TILE

    def body(dp_in, dyb_ref, dyc_ref, ob_ref, oc_ref, zb_ref, zc_ref, dp_ref, dob_ref, doc_ref, dlb_ref, dlc_ref):
        del dp_in
        lane = lax.broadcasted_iota(jnp.int32, (ts, 128), 1)
        for n, (dy_ref, o_ref, z_ref, do_ref, dl_ref) in enumerate(
                ((dyb_ref, ob_ref, zb_ref, dob_ref, dlb_ref), (dyc_ref, oc_ref, zc_ref, doc_ref, dlc_ref))):
            zz = z_ref[...]
            dyv = dy_ref[0]
            sg = _sigmoid(zz)
            dp_ref[:, n * 512:(n + 1) * 512] = (dyv * _compact(o_ref) * (sg * (1.0 + zz * (1.0 - sg)))).astype(BF16)
            do_c = dyv * (zz * sg)
            for p in range(4):
                piece = do_c[:, p * 128:(p + 1) * 128]
                for h, d in ((2 * p, jnp.where(lane < 64, piece, 0.0)),
                             (2 * p + 1, jnp.where(lane < 64, pltpu.roll(piece, 64, 1), 0.0))):
                    do_ref[h] = d.astype(BF16)
                    dl_ref[h] = jnp.sum(d * o_ref[h], axis=-1, keepdims=True)

    col = lambda c: pl.BlockSpec((ts, 512), lambda i: (i, c))
    dysp = lambda n: pl.BlockSpec((1, ts, 512), lambda i: (n, i, 0))
    stat = pl.BlockSpec((N_HEADS, ts, 1), lambda i: (0, i, 0))
    hshape = jax.ShapeDtypeStruct((N_HEADS, S, HEAD_PAD), BF16)
    sshape = jax.ShapeDtypeStruct((N_HEADS, S, 1), F32)
    return _call(
        body, name="ungate_bwd", grid=(S // ts,),
        in_specs=[pl.BlockSpec(memory_space=pl.ANY), dysp(1), dysp(2), _hspec(ts), _hspec(ts),
                  col(P_ZB // 512), col(P_ZC // 512)],
        out_specs=[pl.BlockSpec((ts, 1024), lambda i: (i, P_ZB // 1024)), _hspec(ts), _hspec(ts), stat, stat],
        out_shape=[jax.ShapeDtypeStruct(dproj.shape, BF16), hshape, hshape, sshape, sshape],
        input_output_aliases={0: 0},
        compiler_params=_params(1))(dproj, dy, dy, ob, oc, proj, proj)


def loss_head(y, target):
    S, D = y.shape
    ts = ROW_TILE

    def body(y_ref, t_ref, dy_ref, sq_ref):
        d = y_ref[...] - t_ref[...]
        dy_ref[...] = d * (1.0 / D)
        _accumulate(sq_ref, jnp.sum(d * d, axis=0, keepdims=True), pl.program_id(0) == 0)

    row = pl.BlockSpec((ts, D), lambda i: (i, 0))
    return _call(
        body, name="loss_head", grid=(S // ts,), in_specs=[row, row],
        out_specs=[row, pl.BlockSpec((1, D), lambda i: (0, 0))],
        out_shape=[jax.ShapeDtypeStruct((S, D), F32), jax.ShapeDtypeStruct((1, D), F32)],
        compiler_params=_params(1))(y, target)


def _row_tile(rows, cols):
    for cand in (1024, 512, 256, 128, 64, 32, 16, 8):
        if rows % cand == 0 and cand * cols * 4 <= 1024 * 1024:
            return cand
    return rows


def adamw(w, grads, m, v):
    shape = w.shape
    cols = shape[-1]
    rows = int(np.prod(shape[:-1]))
    tr = _row_tile(rows, cols)
    n_g = len(grads)
    c1 = 1.0 - ADAM_B1 ** ADAM_STEP
    c2 = 1.0 - ADAM_B2 ** ADAM_STEP

    def body(*refs):
        w_ref, m_ref, v_ref = refs[:3]
        g_refs = refs[3:3 + n_g]
        go_ref, d_ref, mo_ref, vo_ref = refs[3 + n_g:]
        gv = g_refs[0][...]
        for g_ref in g_refs[1:]:
            gv = gv + g_ref[...]
        go_ref[...] = gv
        mn = ADAM_B1 * m_ref[...] + (1.0 - ADAM_B1) * gv
        vn = ADAM_B2 * v_ref[...] + (1.0 - ADAM_B2) * (gv * gv)
        mo_ref[...] = mn
        vo_ref[...] = vn
        d_ref[...] = -ADAM_LR * ((mn / c1) / (jnp.sqrt(vn / c2) + ADAM_EPS) + ADAM_WD * w_ref[...])

    blk = pl.BlockSpec((tr, cols), lambda i: (i, 0))
    sds = jax.ShapeDtypeStruct((rows, cols), F32)
    outs = _call(
        body, name="adamw", grid=(rows // tr,), in_specs=[blk] * (3 + n_g), out_specs=[blk] * 4,
        out_shape=[sds] * 4, compiler_params=_params(1))(
            *[a.reshape(rows, cols) for a in (w, m, v, *grads)])
    return [o.reshape(shape) for o in outs]


def add_lead(parts):
    n = parts.shape[0]
    shape = parts.shape[1:]
    cols = shape[-1]
    rows = int(np.prod(shape[:-1]))
    tr = _row_tile(rows, cols * n)

    def body(p_ref, o_ref):
        acc = p_ref[0].astype(F32)
        for s in range(1, n):
            acc = acc + p_ref[s].astype(F32)
        o_ref[...] = acc

    out = _call(
        body, name="add_lead", grid=(rows // tr,),
        in_specs=[pl.BlockSpec((n, tr, cols), lambda i: (0, i, 0))],
        out_specs=pl.BlockSpec((tr, cols), lambda i: (i, 0)),
        out_shape=jax.ShapeDtypeStruct((rows, cols), F32),
        compiler_params=_params(1))(parts.reshape(n, rows, cols))
    return out.reshape(shape)


ANY = pl.BlockSpec(memory_space=pl.ANY)


def _other_chips(x, y):
    return [(1 - x, y), (x, 1 - y), (1 - x, 1 - y)]


def chip_exchange(arrays, scatter, name):
    n = len(arrays)

    def body(*refs):
        _exchange_issue(refs[:n], refs[n:2 * n], refs[2 * n:], scatter, True)
        _exchange_issue(refs[:n], refs[n:2 * n], refs[2 * n:], scatter, False)

    return _call(
        body, name=name, in_specs=[ANY] * n, out_specs=[ANY] * n,
        out_shape=_exchange_out_shape(arrays, scatter), scratch_shapes=_exchange_sems(n))(*arrays)


def _exchange_out_shape(arrays, scatter):
    return [jax.ShapeDtypeStruct(a.shape if scatter else (4,) + a.shape, a.dtype) for a in arrays]


def _exchange_sems(n):
    if n == 0:
        return []
    return [pltpu.SemaphoreType.DMA((3 * n,)), pltpu.SemaphoreType.DMA((3 * n,)), pltpu.SemaphoreType.DMA((n,))]


def _exchange_issue(srcs, outs, sems, scatter, start):
    send_sems, recv_sems, local_sems = sems
    x, y, c = lax.axis_index("x"), lax.axis_index("y"), lax.axis_index("c")
    me = 2 * x + y
    for a in range(len(srcs)):
        local_src = srcs[a].at[me] if scatter else srcs[a]
        mine = pltpu.make_async_copy(local_src, outs[a].at[me], local_sems.at[a])
        sends = []
        for j, (px, py) in enumerate(_other_chips(x, y)):
            pair = dict(send_sem=send_sems.at[3 * a + j], recv_sem=recv_sems.at[3 * a + j],
                        device_id=(px, py, c), device_id_type=MESH)
            sends.append(pltpu.make_async_remote_copy(
                src_ref=srcs[a].at[2 * px + py] if scatter else srcs[a], dst_ref=outs[a].at[me], **pair))
            if not start:
                pltpu.make_async_remote_copy(src_ref=local_src, dst_ref=outs[a].at[2 * px + py], **pair).wait_recv()
        if start:
            mine.start()
            for cp in sends:
                cp.start()
        else:
            for cp in sends:
                cp.wait_send()
            mine.wait()


def sibling_exchange(arrays):
    n = len(arrays)

    def body(*refs):
        srcs, outs = refs[:n], refs[n:2 * n]
        send_sems, recv_sems = refs[2 * n:]
        x, y, c = lax.axis_index("x"), lax.axis_index("y"), lax.axis_index("c")
        copies = [pltpu.make_async_remote_copy(src_ref=srcs[a], dst_ref=outs[a], send_sem=send_sems.at[a],
                                               recv_sem=recv_sems.at[a], device_id=(x, y, 1 - c), device_id_type=MESH)
                  for a in range(n)]
        for cp in copies:
            cp.start()
        for cp in copies:
            cp.wait()

    return _call(
        body, name="sibling_exchange", in_specs=[ANY] * n, out_specs=[ANY] * n,
        out_shape=[jax.ShapeDtypeStruct(a.shape, a.dtype) for a in arrays],
        scratch_shapes=[pltpu.SemaphoreType.DMA((n,)), pltpu.SemaphoreType.DMA((n,))])(*arrays)


def _perm_from_shards(sh):
    rows = sh.shape[1]
    pieces, pos = [], 0
    for lo, hi, plo in sorted(NAT_SEGS, key=lambda s: s[2]):
        if plo > pos:
            pieces.append(jnp.zeros((rows, plo - pos), sh.dtype))
            pos = plo
        c = lo
        while c < hi:
            kk = c // SHARD_COLS
            e = min(hi, (kk + 1) * SHARD_COLS)
            pieces.append(sh[kk][:, c - kk * SHARD_COLS:e - kk * SHARD_COLS])
            c = e
        pos += hi - lo
    if pos < P_W:
        pieces.append(jnp.zeros((rows, P_W - pos), sh.dtype))
    return jnp.concatenate(pieces, axis=1)


def _shards_from_perm(p):
    out = []
    for kk in range(4):
        lo_k, hi_k = kk * SHARD_COLS, (kk + 1) * SHARD_COLS
        pieces = []
        for lo, hi, plo in NAT_SEGS:
            a, b = max(lo, lo_k), min(hi, hi_k)
            if a < b:
                pieces.append(p[:, plo + (a - lo):plo + (b - lo)])
        out.append(jnp.concatenate(pieces, axis=1))
    return jnp.stack(out)


def _split4(a, axis):
    shape = a.shape
    a = a.reshape(shape[:axis] + (4, shape[axis] // 4) + shape[axis + 1:])
    return jnp.moveaxis(a, axis, 0)


def _join4(a, axis):
    a = jnp.moveaxis(a, 0, axis)
    shape = a.shape
    return a.reshape(shape[:axis] + (4 * shape[axis + 1],) + shape[axis + 2:])


def _pad_heads(w, per_head, lo, hi):
    r = w.shape[0]
    wh = w.reshape(r, N_HEADS, per_head)[:, :, lo:hi]
    return jnp.pad(wh, ((0, 0), (0, 0), (0, HEAD_PAD - (hi - lo)))).reshape(r, N_HEADS * HEAD_PAD)


def _rope_table(S):
    half = MLA_ROPE // 2
    inv = ROPE_BASE ** (-jnp.arange(half, dtype=F32) / half)
    ang = jnp.arange(S).astype(F32)[:, None] * inv[None, :]
    cos, sin = jnp.cos(ang), jnp.sin(ang)
    z = lambda n: jnp.zeros((S, n), F32)
    c = jnp.concatenate([jnp.ones((S, MLA_NOPE), F32), cos, cos, z(32)], axis=1)
    a = jnp.concatenate([z(MLA_NOPE), -sin, z(48)], axis=1)
    b = jnp.concatenate([z(MLA_NOPE + half), sin, z(32)], axis=1)
    return jnp.stack([c, a, b])


def _band_onehot():
    t = ATT_T
    m = np.arange(2 * t)
    d = np.where(m < t, m, m - 2 * t)
    idx = np.stack([np.clip(off + d, -REL_CLIP, REL_CLIP) + REL_CLIP for off in (t, 0)])
    return (idx[:, :, None] == np.arange(2 * REL_CLIP + 1)[None, None, :]).astype(np.float32)


def bias_expand(diag):
    t = ATT_T

    def body(d_ref, o_ref):
        kc = lax.broadcasted_iota(jnp.int32, (t, t), 0) >> CHUNK_SHIFT
        qc = lax.broadcasted_iota(jnp.int32, (t, t), 1) >> CHUNK_SHIFT
        for w, visible in ((0, kc >= qc), (1, kc <= qc)):
            rows = jnp.broadcast_to(d_ref[0, w:w + 1, :], (t, 2 * t))
            skew = pltpu.roll(rows, 0, 1, stride=1, stride_axis=0)[:, :t]
            o_ref[0, 2 * w] = jnp.where(visible, skew * LOG2E, NEG_INF)
        o_ref[0, 1] = jnp.full((t, t), NEG_INF, F32)

    return _call(
        body, name="bias_expand", grid=(N_HEADS,),
        in_specs=[pl.BlockSpec((1, 2, 2 * t), lambda h: (h, 0, 0))],
        out_specs=pl.BlockSpec((1, 3, t, t), lambda h: (h, 0, 0, 0)),
        out_shape=jax.ShapeDtypeStruct((N_HEADS, 3, t, t), F32),
        compiler_params=_params(1))(diag)


def bias_fold(dtiles):
    t = ATT_T

    def body(d_ref, o_ref):
        pad = jnp.zeros((8, t), F32)
        for w in range(2):
            acc = jnp.concatenate([d_ref[0, w, 0:8, :], pad], axis=1)
            for g in range(1, t // 8):
                grp = jnp.concatenate([d_ref[0, w, 8 * g:8 * g + 8, :], pad], axis=1)
                acc = acc + pltpu.roll(grp, 2 * t - 8 * g, 1)
            out = acc[0:1, :]
            for s in range(1, 8):
                out = out + pltpu.roll(acc, 2 * t - s, 1)[s:s + 1, :]
            o_ref[0, w:w + 1, :] = out

    return _call(
        body, name="bias_fold", grid=(N_HEADS,),
        in_specs=[pl.BlockSpec((1, 2, t, t), lambda h: (h, 0, 0, 0))],
        out_specs=pl.BlockSpec((1, 2, 2 * t), lambda h: (h, 0, 0)),
        out_shape=jax.ShapeDtypeStruct((N_HEADS, 2, 2 * t), F32),
        compiler_params=_params(1))(dtiles)


def _bias_tiles(table):
    diag = jnp.einsum('hr,wdr->hwd', table, jnp.asarray(_band_onehot()), precision=lax.Precision.HIGHEST)
    return bias_expand(diag)


def _bias_tiles_grad(dtiles):
    return jnp.einsum('hwd,wdr->hr', bias_fold(dtiles), jnp.asarray(_band_onehot()),
                      precision=lax.Precision.HIGHEST)


def _layer_consts(lw):
    tri = np.tril(np.ones((SGU_BLOCK, SGU_BLOCK), np.float32))
    ws = (lw['sgu_w'] * tri).astype(BF16)
    return dict(
        ws=ws, ws_t=jnp.swapaxes(ws, 1, 2), sgu_bias=jnp.repeat(lw['sgu_b'].T, CA_HEAD_DIM, axis=1),
        bias=_bias_tiles(lw['ca_rel_bias']),
        wq=_pad_heads(lw['mla_w_uq'], MLA_QK, 0, MLA_QK),
        wk=_pad_heads(lw['mla_w_ukv'], MLA_NOPE + MLA_V, 0, MLA_NOPE),
        wv=_pad_heads(lw['mla_w_ukv'], MLA_NOPE + MLA_V, MLA_NOPE, MLA_NOPE + MLA_V),
        gate_b=lw['gate_b'].reshape(1, 3 * D_MODEL),
        pre_g=lw['pre_g'][None], post_g=lw['post_g'][None], ln_g=lw['sgu_ln_g'][None], ln_b=lw['sgu_ln_b'][None],
        qg=lw['mla_q_norm_g'][None], kvg=lw['mla_kv_norm_g'][None])


def _layer_fwd(x, lw, k, tab, next_shards):
    proj, xn = norm_matmul(x, k['pre_g'], lw['w_in'])
    ya = sgu_fwd(proj, k['ln_g'], k['ln_b'], k['ws'], k['sgu_bias'])
    qb, kb, vb, qc, kc, vc, kbt, vbt, kct, vct, cq, ckv = prep_fwd(proj, tab, k['qg'], k['kvg'], k['wq'], k['wk'],
                                                                   k['wv'])
    ob, lse_b, gathered = mla_fwd(qb, kb, vbt, next_shards)
    oc, lse_c = band_fwd(qc, kc, vct, k['bias'])
    x_new, yb, yc, merged, out = merge_fwd(ob, oc, proj, ya, k['gate_b'], lw['w_branch'], lw['w_out'], x,
                                           k['post_g'])
    saved = dict(x=x, proj=proj, xn=xn, ya=ya, yb=yb, yc=yc, qb=qb, kb=kb, vb=vb, qc=qc, kc=kc, vc=vc, kbt=kbt, kct=kct,
                 cq=cq, ckv=ckv, ob=ob, oc=oc, lse_b=lse_b, lse_c=lse_c, merged=merged, out=out)
    return x_new, saved, gathered


def _layer_bwd(g, s, lw, k, tab, pending_parts):
    S = g.shape[0]
    H = N_HEADS
    dout, dmerged, g_post = post_bwd(g, s['out'], k['post_g'], lw['w_out'])
    g_w_out = matmul_tn(s['merged'], dout, 512)
    dproj, dba, dbb, dbc, dy, g_gate_b = gate_bwd(dmerged, s['proj'], k['gate_b'], s['ya'], s['yb'], s['yc'],
                                                  lw['w_branch'])
    g_w_branch = jnp.stack([matmul_tn(y, d, 512) for y, d in ((s['ya'], dba), (s['yb'], dbb), (s['yc'], dbc))])
    dproj, dob, doc, dl_b, dl_c = ungate_bwd(dproj, dy, s['ob'], s['oc'], s['proj'])
    row = lambda a: a.reshape(H, 1, S)
    tr = lambda a: jnp.swapaxes(a, 1, 2)
    dqt, dkb, dvb, landed = mla_bwd(s['qb'], s['kb'], s['kbt'], s['vb'], dob, s['lse_b'], row(dl_b), pending_parts)
    dqb = jnp.swapaxes(dqt, 2, 3).reshape(H, S, HEAD_PAD)
    dqct, dkc, dvc, dbias = band_bwd(s['qc'], s['kc'], s['kct'], s['vc'], doc, s['lse_c'], row(dl_c), k['bias'])
    dproj, dqf, dkf, dvf, g_qg, g_kvg = prep_bwd(dproj, dqb, dkb, dvb, tr(dqct), dkc, dvc, s['proj'], tab,
                                                 k['qg'], k['kvg'], k['wq'], k['wk'], k['wv'])
    g_wq = matmul_tn(s['cq'], dqf, 512).reshape(MLA_Q_RANK, H, HEAD_PAD)[:, :, :MLA_QK]
    g_wk = matmul_tn(s['ckv'], dkf, 512).reshape(MLA_KV_RANK, H, HEAD_PAD)[:, :, :MLA_NOPE]
    g_wv = matmul_tn(s['ckv'], dvf, 512).reshape(MLA_KV_RANK, H, HEAD_PAD)[:, :, :MLA_V]
    dproj, g_ln_g, g_ln_b, g_ws, g_sgu_bias = sgu_bwd(dproj, dy, s['proj'], k['ln_g'], k['ln_b'], k['ws'],
                                                      k['ws_t'], k['sgu_bias'])
    g_w_in = matmul_tn(s['xn'], dproj, MM_TN)
    dx, g_pre = proj_bwd_x(dproj, lw['w_in'], s['x'], k['pre_g'], g)
    tri = np.tril(np.ones((SGU_BLOCK, SGU_BLOCK), np.float32))
    grads = dict(
        w_in=g_w_in, pre_g=g_pre[0], post_g=g_post[0], sgu_ln_g=g_ln_g[0], sgu_ln_b=g_ln_b[0],
        sgu_w=g_ws * tri, sgu_b=jnp.sum(g_sgu_bias.reshape(SGU_BLOCK, 8, CA_HEAD_DIM), axis=2).T,
        mla_q_norm_g=g_qg[0], mla_kv_norm_g=g_kvg[0],
        mla_w_uq=g_wq.reshape(MLA_Q_RANK, H * MLA_QK),
        mla_w_ukv=jnp.concatenate([g_wk, g_wv], axis=2).reshape(MLA_KV_RANK, H * (MLA_NOPE + MLA_V)),
        ca_rel_bias=_bias_tiles_grad(dbias), w_branch=g_w_branch,
        gate_b=g_gate_b.reshape(N_BRANCH, D_MODEL), w_out=g_w_out)
    return dx, grads, landed


BF16_PARTS = ('w_in', 'mla_w_uq', 'mla_w_ukv', 'w_branch', 'w_out')


def _weight_shards(w, l):
    return [w[n][l].astype(BF16) if n in BF16_PARTS else w[n][l] for n in SHARDED]


def _full_weights(gathered, small):
    lw = {n: _join4(a, SHARD_AXIS[n]) for n, a in zip(SHARDED, gathered) if n != 'w_in'}
    lw['w_in'] = _perm_from_shards(gathered[0])
    lw.update(small)
    return lw


def _small_pack(grads):
    flat = jnp.concatenate([grads[n].reshape(-1) for n in SMALL])
    quarter = -(-flat.size // (4 * 1024)) * 1024
    return jnp.pad(flat, (0, 4 * quarter - flat.size)).reshape(4, quarter // 128, 128)


def _grad_parts(grads):
    parts = [_shards_from_perm(grads['w_in'])]
    parts += [_split4(grads[n], SHARD_AXIS[n]) for n in SHARDED if n != 'w_in']
    parts = [p.astype(BF16) if n in BF16_PARTS else p for n, p in zip(SHARDED, parts)]
    return parts + [_small_pack(grads)]


def _sum_landed(landed):
    mine = [add_lead(p) for p in landed]
    return mine, sibling_exchange(mine)


def train_step_local(x, target, w):
    S = x.shape[0]
    depth = w['w_in'].shape[0]
    tab = _rope_table(S)
    gathered = chip_exchange(_weight_shards(w, 0), False, "gather_weights")
    layer_w, consts, saved = [], [], []
    for l in range(depth):
        lw = _full_weights(gathered, {n: w[n][l] for n in SMALL})
        k = _layer_consts(lw)
        x, s, gathered = _layer_fwd(x, lw, k, tab, _weight_shards(w, l + 1) if l + 1 < depth else ())
        layer_w.append(lw)
        consts.append(k)
        saved.append(s)
    g, sq = loss_head(x, target)
    reduced = [None] * depth
    pending = ()
    for l in reversed(range(depth)):
        g, grads, landed = _layer_bwd(g, saved[l], layer_w[l], consts[l], tab, pending)
        if pending:
            reduced[l + 1] = _sum_landed(landed)
        pending = _grad_parts(grads)
    reduced[0] = _sum_landed(chip_exchange(pending, True, "scatter_grads"))
    return sq, g, reduced


def kernel(x, w_in, pre_g, post_g, sgu_ln_g, sgu_ln_b, sgu_w, sgu_b, mla_q_norm_g, mla_kv_norm_g, mla_w_uq, mla_w_ukv, ca_rel_bias, w_branch, gate_b, w_out, loss_target, m_w_in, m_pre_g, m_post_g, m_sgu_ln_g, m_sgu_ln_b, m_sgu_w, m_sgu_b, m_mla_q_norm_g, m_mla_kv_norm_g, m_mla_w_uq, m_mla_w_ukv, m_ca_rel_bias, m_w_branch, m_gate_b, m_w_out, v_w_in, v_pre_g, v_post_g, v_sgu_ln_g, v_sgu_ln_b, v_sgu_w, v_sgu_b, v_mla_q_norm_g, v_mla_kv_norm_g, v_mla_w_uq, v_mla_w_ukv, v_ca_rel_bias, v_w_branch, v_gate_b, v_w_out):
    w = dict(w_in=w_in, pre_g=pre_g, post_g=post_g, sgu_ln_g=sgu_ln_g, sgu_ln_b=sgu_ln_b, sgu_w=sgu_w, sgu_b=sgu_b,
             mla_q_norm_g=mla_q_norm_g, mla_kv_norm_g=mla_kv_norm_g, mla_w_uq=mla_w_uq, mla_w_ukv=mla_w_ukv,
             ca_rel_bias=ca_rel_bias, w_branch=w_branch, gate_b=gate_b, w_out=w_out)
    m = dict(w_in=m_w_in, pre_g=m_pre_g, post_g=m_post_g, sgu_ln_g=m_sgu_ln_g, sgu_ln_b=m_sgu_ln_b, sgu_w=m_sgu_w,
             sgu_b=m_sgu_b, mla_q_norm_g=m_mla_q_norm_g, mla_kv_norm_g=m_mla_kv_norm_g, mla_w_uq=m_mla_w_uq,
             mla_w_ukv=m_mla_w_ukv, ca_rel_bias=m_ca_rel_bias, w_branch=m_w_branch, gate_b=m_gate_b, w_out=m_w_out)
    v = dict(w_in=v_w_in, pre_g=v_pre_g, post_g=v_post_g, sgu_ln_g=v_sgu_ln_g, sgu_ln_b=v_sgu_ln_b, sgu_w=v_sgu_w,
             sgu_b=v_sgu_b, mla_q_norm_g=v_mla_q_norm_g, mla_kv_norm_g=v_mla_kv_norm_g, mla_w_uq=v_mla_w_uq,
             mla_w_ukv=v_mla_w_ukv, ca_rel_bias=v_ca_rel_bias, w_branch=v_w_branch, gate_b=v_gate_b, w_out=v_w_out)
    depth = w_in.shape[0]
    sq, grad_x, reduced = train_step_local(x[0], loss_target[0], w)
    loss = lax.psum(0.5 * jnp.sum(sq) / D_MODEL, ("x", "y", "c"))

    out = {}
    for a, n in enumerate(SHARDED):
        mine = jnp.stack([reduced[l][0][a] for l in range(depth)])
        theirs = jnp.stack([reduced[l][1][a] for l in range(depth)])
        out[n] = adamw(w[n], [mine, theirs], m[n], v[n])
    small = jnp.stack([jnp.stack([reduced[l][0][-1] for l in range(depth)]),
                       jnp.stack([reduced[l][1][-1] for l in range(depth)])])
    quarter = add_lead(small)
    full = chip_exchange([quarter], False, "gather_small")[0]
    full = jnp.moveaxis(full, 0, 1).reshape(depth, -1)
    off = 0
    for n in SMALL:
        size = int(np.prod(w[n].shape[1:]))
        out[n] = adamw(w[n], [full[:, off:off + size].reshape(w[n].shape)], m[n], v[n])
        off += size
    return (loss, grad_x[None], *[out[n][0] for n in WEIGHTS], *[out[n][1] for n in WEIGHTS],
            *[out[n][2] for n in WEIGHTS], *[out[n][3] for n in WEIGHTS])
```

```python
import numpy as np
import jax
import jax.numpy as jnp
from jax import lax
from jax.experimental import pallas as pl
from jax.experimental.pallas import tpu as pltpu

F32 = jnp.float32
BF16 = jnp.bfloat16
MESH = pl.DeviceIdType.MESH

EPS = 1e-6
NEG_INF = -1e30
D_MODEL = 1024
BR_WIDTH = 512
N_BRANCH = 3
N_HEADS = 8
HEAD_PAD = 128
CHUNK_SHIFT = 6
SGU_BLOCK = 128
MLA_NOPE, MLA_ROPE, MLA_V = 64, 32, 64
MLA_QK = MLA_NOPE + MLA_ROPE
MLA_Q_RANK, MLA_KV_RANK = 256, 128
CA_HEAD_DIM = 64
REL_CLIP = 128
ROPE_BASE = 10000.0
D_IN = 7584

ADAM_LR, ADAM_B1, ADAM_B2, ADAM_EPS, ADAM_WD, ADAM_STEP = 0.001, 0.9, 0.999, 1e-08, 0.01, 10

P_QC, P_KC, P_VC, P_QD, P_KVD, P_KR, P_ZB, P_ZC, P_G, P_U, P_V, P_ZA, P_W = (
    0, 512, 1024, 1536, 1792, 1920, 2048, 2560, 3072, 6144, 6656, 7168, 7680)
NAT_SEGS = [(0, 1536, P_U), (1536, 1920, P_QD), (1920, 1952, P_KR + MLA_NOPE), (1952, 2464, P_ZB),
            (2464, 4000, P_QC), (4000, 4512, P_ZC), (4512, 7584, P_G)]
SHARD_COLS = D_IN // 4

VMEM_LIMIT = 48 * 1024 * 1024
ATT_T = 512
MLA_HEADS_PER_STEP = 2
MLA_FWD_HEADS_PER_STEP = 4
ROW_TILE = 256
MM_TM = 512
MM_TN = 1536
LOG2E = 1.4426950408889634
MLA_SCALE = MLA_QK ** -0.5
CA_SCALE = CA_HEAD_DIM ** -0.5

WEIGHTS = ['w_in', 'pre_g', 'post_g', 'sgu_ln_g', 'sgu_ln_b', 'sgu_w', 'sgu_b', 'mla_q_norm_g',
           'mla_kv_norm_g', 'mla_w_uq', 'mla_w_ukv', 'ca_rel_bias', 'w_branch', 'gate_b', 'w_out']
SHARDED = ['w_in', 'mla_w_uq', 'mla_w_ukv', 'w_branch', 'gate_b', 'w_out']
SMALL = ['pre_g', 'post_g', 'sgu_ln_g', 'sgu_ln_b', 'sgu_w', 'sgu_b', 'mla_q_norm_g',
         'mla_kv_norm_g', 'ca_rel_bias']
SHARD_AXIS = {'w_in': 1, 'mla_w_uq': 1, 'mla_w_ukv': 1, 'w_branch': 2, 'gate_b': 1, 'w_out': 0}


def _call(body, **kw):
    return pl.pallas_call(body, **kw)


def _params(n_axes):
    return pltpu.CompilerParams(dimension_semantics=("arbitrary",) * n_axes,
                                vmem_limit_bytes=VMEM_LIMIT)


def _nt(a, b):
    return lax.dot_general(a, b, (((1,), (1,)), ((), ())), preferred_element_type=F32)


def _nn(a, b):
    return jnp.dot(a, b, preferred_element_type=F32)


def _tn(a, b):
    return lax.dot_general(a, b, (((0,), (0,)), ((), ())), preferred_element_type=F32)


def _rms(xv, g):
    r = lax.rsqrt(jnp.mean(xv * xv, axis=-1, keepdims=True) + EPS)
    return xv * r * g, r


def _rms_bwd(xv, g, r, dy):
    gy = dy * g
    dx = r * gy - xv * (r * r * r) * jnp.mean(xv * gy, axis=-1, keepdims=True)
    dg = jnp.sum(dy * (xv * r), axis=0, keepdims=True)
    return dx, dg


def _sigmoid(z):
    return 1.0 / (1.0 + jnp.exp(-z))


def _rope(xv, c, a, b):
    return xv * c + pltpu.roll(xv, 112, 1) * a + pltpu.roll(xv, 16, 1) * b


def _accumulate(ref, val, first):
    @pl.when(first)
    def _():
        ref[...] = val

    @pl.when(jnp.logical_not(first))
    def _():
        ref[...] += val


def norm_matmul(x, g, w):
    S, D = x.shape
    N = w.shape[1]
    tm, tn = min(S, 2 * MM_TM), MM_TN

    def body(x_ref, g_ref, w_ref, o_ref, xn_ref):
        @pl.when(pl.program_id(1) == 0)
        def _():
            y, _ = _rms(x_ref[...], g_ref[...])
            xn_ref[...] = y.astype(BF16)

        o_ref[...] = _nn(xn_ref[...], w_ref[...])

    return _call(
        body, name="norm_matmul", grid=(S // tm, N // tn),
        in_specs=[pl.BlockSpec((tm, D), lambda i, j: (i, 0)),
                  pl.BlockSpec((1, D), lambda i, j: (0, 0)),
                  pl.BlockSpec((D, tn), lambda i, j: (0, j))],
        out_specs=[pl.BlockSpec((tm, tn), lambda i, j: (i, j)),
                   pl.BlockSpec((tm, D), lambda i, j: (i, 0))],
        out_shape=[jax.ShapeDtypeStruct((S, N), F32), jax.ShapeDtypeStruct((S, D), BF16)],
        compiler_params=_params(2))(x, g, w)


def proj_bwd_x(dproj, w, x, g, resid):
    S, N = dproj.shape
    D = x.shape[1]
    tm, tk = min(S, MM_TM), MM_TN
    nk = N // tk

    def body(dp_ref, w_ref, x_ref, g_ref, r_ref, dx_ref, dg_ref, acc_ref):
        i, k = pl.program_id(0), pl.program_id(1)

        @pl.when(k == 0)
        def _():
            acc_ref[...] = jnp.zeros_like(acc_ref)

        acc_ref[...] += _nt(dp_ref[...].astype(BF16), w_ref[...])

        @pl.when(k == nk - 1)
        def _():
            xv = x_ref[...]
            _, r = _rms(xv, g_ref[...])
            dx, dg = _rms_bwd(xv, g_ref[...], r, acc_ref[...])
            dx_ref[...] = dx + r_ref[...]
            _accumulate(dg_ref, dg, i == 0)

    return _call(
        body, name="proj_bwd_x", grid=(S // tm, nk),
        in_specs=[pl.BlockSpec((tm, tk), lambda i, k: (i, k)),
                  pl.BlockSpec((D, tk), lambda i, k: (0, k)),
                  pl.BlockSpec((tm, D), lambda i, k: (i, 0)),
                  pl.BlockSpec((1, D), lambda i, k: (0, 0)),
                  pl.BlockSpec((tm, D), lambda i, k: (i, 0))],
        out_specs=[pl.BlockSpec((tm, D), lambda i, k: (i, 0)),
                   pl.BlockSpec((1, D), lambda i, k: (0, 0))],
        out_shape=[jax.ShapeDtypeStruct((S, D), F32), jax.ShapeDtypeStruct((1, D), F32)],
        scratch_shapes=[pltpu.VMEM((tm, D), F32)],
        compiler_params=_params(2))(dproj, w, x, g, resid)


def matmul_tn(a, b, tn):
    S, M = a.shape
    N = b.shape[1]
    tk = min(S, 2 * MM_TM)

    def body(a_ref, b_ref, o_ref):
        @pl.when(pl.program_id(1) == 0)
        def _():
            o_ref[...] = jnp.zeros_like(o_ref)

        o_ref[...] += _tn(a_ref[...].astype(BF16), b_ref[...].astype(BF16))

    return _call(
        body, name="matmul_tn", grid=(N // tn, S // tk),
        in_specs=[pl.BlockSpec((tk, M), lambda j, k: (k, 0)),
                  pl.BlockSpec((tk, tn), lambda j, k: (k, j))],
        out_specs=pl.BlockSpec((M, tn), lambda j, k: (0, j)),
        out_shape=jax.ShapeDtypeStruct((M, N), F32),
        compiler_params=_params(2))(a, b)


def _sgu_block(vv, g, b, ws_ref, lane):
    mu = jnp.mean(vv, axis=-1, keepdims=True)
    xc = vv - mu
    r = lax.rsqrt(jnp.mean(xc * xc, axis=-1, keepdims=True) + EPS)
    xhat = xc * r
    vln = (xhat * g + b).astype(BF16)
    pieces = []
    for p in range(4):
        vp = vln[:, p * 128:(p + 1) * 128]
        pieces.append(jnp.where(lane < 64, _nn(ws_ref[2 * p], vp), _nn(ws_ref[2 * p + 1], vp)))
    return xhat, r, vln, jnp.concatenate(pieces, axis=1)


def sgu_fwd(proj, ln_g, ln_b, ws, bias_full):
    S = proj.shape[0]
    ts = ROW_TILE

    def body(u_ref, v_ref, z_ref, g_ref, b_ref, ws_ref, bf_ref, y_ref):
        lane = lax.broadcasted_iota(jnp.int32, (SGU_BLOCK, 128), 1)
        for blk in range(ts // SGU_BLOCK):
            rows = slice(blk * SGU_BLOCK, (blk + 1) * SGU_BLOCK)
            _, _, _, mixed = _sgu_block(v_ref[rows, :], g_ref[...], b_ref[...], ws_ref, lane)
            mixed = mixed + bf_ref[...]
            zz = z_ref[rows, :]
            y_ref[rows, :] = (u_ref[rows, :] * mixed * (zz * _sigmoid(zz))).astype(BF16)

    col = lambda c: pl.BlockSpec((ts, BR_WIDTH), lambda i: (i, c))
    full = lambda shape: pl.BlockSpec(shape, lambda i: (0,) * len(shape))
    return _call(
        body, name="sgu_fwd", grid=(S // ts,),
        in_specs=[col(P_U // 512), col(P_V // 512), col(P_ZA // 512),
                  full((1, BR_WIDTH)), full((1, BR_WIDTH)), full((8, 128, 128)), full((128, BR_WIDTH))],
        out_specs=pl.BlockSpec((ts, BR_WIDTH), lambda i: (i, 0)),
        out_shape=jax.ShapeDtypeStruct((S, BR_WIDTH), BF16),
        compiler_params=_params(1))(proj, proj, proj, ln_g, ln_b, ws, bias_full)


def sgu_bwd(dproj, dy, proj, ln_g, ln_b, ws, ws_t, bias_full):
    S = proj.shape[0]
    ts = ROW_TILE

    def body(dp_in, dy_ref, u_ref, v_ref, z_ref, g_ref, b_ref, ws_ref, wst_ref, bf_ref,
             dp_ref, gg_ref, gb_ref, gws_ref, gbf_ref):
        del dp_in
        first = pl.program_id(0) == 0

        @pl.when(first)
        def _():
            gg_ref[...] = jnp.zeros_like(gg_ref)
            gb_ref[...] = jnp.zeros_like(gb_ref)
            gws_ref[...] = jnp.zeros_like(gws_ref)
            gbf_ref[...] = jnp.zeros_like(gbf_ref)

        lane = lax.broadcasted_iota(jnp.int32, (SGU_BLOCK, 128), 1)
        for blk in range(ts // SGU_BLOCK):
            rows = slice(blk * SGU_BLOCK, (blk + 1) * SGU_BLOCK)
            g = g_ref[...]
            xhat, r, vln, mixed = _sgu_block(v_ref[rows, :], g, b_ref[...], ws_ref, lane)
            mixed = mixed + bf_ref[...]
            zz = z_ref[rows, :]
            uu = u_ref[rows, :]
            dyv = dy_ref[0, rows, :]
            sg = _sigmoid(zz)
            sil = zz * sg
            dmixed = dyv * uu * sil
            dp_ref[rows, 0:512] = (dyv * mixed * sil).astype(BF16)
            dp_ref[rows, 1024:1536] = (dyv * uu * mixed * (sg * (1.0 + zz * (1.0 - sg)))).astype(BF16)
            gbf_ref[...] += dmixed
            dmb = dmixed.astype(BF16)
            pieces = []
            for p in range(4):
                dmp = dmb[:, p * 128:(p + 1) * 128]
                vp = vln[:, p * 128:(p + 1) * 128]
                pieces.append(jnp.where(lane < 64, _nn(wst_ref[2 * p], dmp), _nn(wst_ref[2 * p + 1], dmp)))
                zero = jnp.zeros_like(dmp)
                gws_ref[2 * p] += _nt(jnp.where(lane < 64, dmp, zero), vp)
                gws_ref[2 * p + 1] += _nt(jnp.where(lane >= 64, dmp, zero), vp)
            dvln = jnp.concatenate(pieces, axis=1)
            dxh = dvln * g
            dp_ref[rows, 512:1024] = (r * (dxh - jnp.mean(dxh, axis=-1, keepdims=True)
                                           - xhat * jnp.mean(dxh * xhat, axis=-1, keepdims=True))).astype(BF16)
            gg_ref[...] += jnp.sum(dvln * xhat, axis=0, keepdims=True)
            gb_ref[...] += jnp.sum(dvln, axis=0, keepdims=True)

    col = lambda c: pl.BlockSpec((ts, BR_WIDTH), lambda i: (i, c))
    full = lambda shape: pl.BlockSpec(shape, lambda i: (0,) * len(shape))
    return _call(
        body, name="sgu_bwd", grid=(S // ts,),
        in_specs=[pl.BlockSpec(memory_space=pl.ANY),
                  pl.BlockSpec((1, ts, BR_WIDTH), lambda i: (0, i, 0)),
                  col(P_U // 512), col(P_V // 512), col(P_ZA // 512),
                  full((1, BR_WIDTH)), full((1, BR_WIDTH)), full((8, 128, 128)), full((8, 128, 128)),
                  full((128, BR_WIDTH))],
        out_specs=[pl.BlockSpec((ts, 1536), lambda i: (i, P_U // 1536)),
                   full((1, BR_WIDTH)), full((1, BR_WIDTH)), full((8, 128, 128)), full((128, BR_WIDTH))],
        out_shape=[jax.ShapeDtypeStruct(dproj.shape, BF16),
                   jax.ShapeDtypeStruct((1, BR_WIDTH), F32), jax.ShapeDtypeStruct((1, BR_WIDTH), F32),
                   jax.ShapeDtypeStruct((8, 128, 128), F32), jax.ShapeDtypeStruct((128, BR_WIDTH), F32)],
        input_output_aliases={0: 0},
        compiler_params=_params(1))(dproj, dy, proj, proj, proj, ln_g, ln_b, ws, ws_t, bias_full)


def _hspec(ts):
    return pl.BlockSpec((N_HEADS, ts, HEAD_PAD), lambda i: (0, i, 0))


def prep_fwd(proj, tab, qg, kvg, wq, wk, wv):
    S = proj.shape[0]
    ts = ROW_TILE

    def body(qc_ref, kc_ref, vc_ref, qd_ref, kvd_ref, kr_ref, tab_ref, qg_ref, kvg_ref,
             wq_ref, wk_ref, wv_ref, qb, kb, vb, qc, kc, vc, kbt, vbt, kct, vct, cq_o, ckv_o):
        c, a, b = tab_ref[0], tab_ref[1], tab_ref[2]
        cq, _ = _rms(qd_ref[...], qg_ref[...])
        ckv, _ = _rms(kvd_ref[...], kvg_ref[...])
        cqb, ckvb = cq.astype(BF16), ckv.astype(BF16)
        cq_o[...] = cqb
        ckv_o[...] = ckvb
        krr = _rope(kr_ref[...], c, a, b)
        lane = lax.broadcasted_iota(jnp.int32, (ts, 128), 1)
        ones_lane = jnp.where(lane == MLA_V, 1.0, 0.0)
        for h in range(N_HEADS):
            cols = slice(h * HEAD_PAD, (h + 1) * HEAD_PAD)
            qb[h] = (_rope(_nn(cqb, wq_ref[:, cols]), c, a, b) * (MLA_SCALE * LOG2E)).astype(BF16)
            kh = _nn(ckvb, wk_ref[:, cols]) + krr
            vh = _nn(ckvb, wv_ref[:, cols]) + ones_lane
            kb[h], kbt[h] = kh.astype(BF16), kh.T.astype(BF16)
            vb[h], vbt[h] = vh.astype(BF16), vh.T.astype(BF16)
        for p in range(4):
            piece = qc_ref[:, p * 128:(p + 1) * 128] * (CA_SCALE * LOG2E)
            qc[2 * p] = jnp.where(lane < 64, piece, 0.0).astype(BF16)
            qc[2 * p + 1] = jnp.where(lane < 64, pltpu.roll(piece, 64, 1), 0.0).astype(BF16)
            for src, dst, dst_t, pad in ((kc_ref, kc, kct, 0.0), (vc_ref, vc, vct, ones_lane)):
                piece = src[:, p * 128:(p + 1) * 128]
                for h, head in ((2 * p, jnp.where(lane < 64, piece, pad)),
                                (2 * p + 1, jnp.where(lane < 64, pltpu.roll(piece, 64, 1), pad))):
                    dst[h], dst_t[h] = head.astype(BF16), head.T.astype(BF16)

    col = lambda w, c: pl.BlockSpec((ts, w), lambda i: (i, c))
    full = lambda shape: pl.BlockSpec(shape, lambda i: (0,) * len(shape))
    hshape = jax.ShapeDtypeStruct((N_HEADS, S, HEAD_PAD), BF16)
    tshape = jax.ShapeDtypeStruct((N_HEADS, HEAD_PAD, S), BF16)
    tspec = pl.BlockSpec((N_HEADS, HEAD_PAD, ts), lambda i: (0, 0, i))
    return _call(
        body, name="prep_fwd", grid=(S // ts,),
        in_specs=[col(512, P_QC // 512), col(512, P_KC // 512), col(512, P_VC // 512),
                  col(256, P_QD // 256), col(128, P_KVD // 128), col(128, P_KR // 128),
                  pl.BlockSpec((3, ts, 128), lambda i: (0, i, 0)),
                  full((1, MLA_Q_RANK)), full((1, MLA_KV_RANK)),
                  full((MLA_Q_RANK, 1024)), full((MLA_KV_RANK, 1024)), full((MLA_KV_RANK, 1024))],
        out_specs=[_hspec(ts)] * 6 + [tspec] * 4 + [pl.BlockSpec((ts, MLA_Q_RANK), lambda i: (i, 0)),
                                                    pl.BlockSpec((ts, MLA_KV_RANK), lambda i: (i, 0))],
        out_shape=[hshape] * 6 + [tshape] * 4 + [jax.ShapeDtypeStruct((S, MLA_Q_RANK), BF16),
                                                 jax.ShapeDtypeStruct((S, MLA_KV_RANK), BF16)],
        compiler_params=_params(1))(proj, proj, proj, proj, proj, proj, tab, qg, kvg, wq, wk, wv)


def prep_bwd(dproj, dqb, dkb, dvb, dqc, dkc, dvc, proj, tab, qg, kvg, wq, wk, wv):
    S = proj.shape[0]
    ts = ROW_TILE

    def body(dp_in, dqb_r, dkb_r, dvb_r, dqc_r, dkc_r, dvc_r, qd_ref, kvd_ref, tab_ref, qg_ref, kvg_ref,
             wq_ref, wk_ref, wv_ref, dp_ref, dqf, dkf, dvf, gq_ref, gkv_ref):
        del dp_in
        c, a, b = tab_ref[0], -tab_ref[1], -tab_ref[2]
        qd, kvd = qd_ref[...], kvd_ref[...]
        _, rq = _rms(qd, qg_ref[...])
        _, rkv = _rms(kvd, kvg_ref[...])
        dcq = jnp.zeros((ts, MLA_Q_RANK), F32)
        dckv = jnp.zeros((ts, MLA_KV_RANK), F32)
        dksum = jnp.zeros((ts, HEAD_PAD), F32)
        for h in range(N_HEADS):
            cols = slice(h * HEAD_PAD, (h + 1) * HEAD_PAD)
            dqh = _rope(dqb_r[h].astype(F32) * MLA_SCALE, c, a, b).astype(BF16)
            dqf[:, cols] = dqh
            dcq = dcq + _nt(dqh, wq_ref[:, cols])
            dk = dkb_r[h].astype(F32) * (1.0 / LOG2E)
            dksum = dksum + dk
            dkh = dk.astype(BF16)
            dkf[:, cols] = dkh
            dvh = dvb_r[h].astype(BF16)
            dvf[:, cols] = dvh
            dckv = dckv + _nt(dkh, wk_ref[:, cols]) + _nt(dvh, wv_ref[:, cols])
        lane = lax.broadcasted_iota(jnp.int32, (ts, 128), 1)
        rope_lanes = jnp.logical_and(lane >= MLA_NOPE, lane < MLA_QK)
        dp_ref[:, P_KR:P_KR + 128] = jnp.where(rope_lanes, _rope(dksum, c, a, b), 0.0).astype(BF16)
        dqd, gq = _rms_bwd(qd, qg_ref[...], rq, dcq)
        dkvd, gkv = _rms_bwd(kvd, kvg_ref[...], rkv, dckv)
        dp_ref[:, P_QD:P_QD + 256] = dqd.astype(BF16)
        dp_ref[:, P_KVD:P_KVD + 128] = dkvd.astype(BF16)
        first = pl.program_id(0) == 0
        _accumulate(gq_ref, gq, first)
        _accumulate(gkv_ref, gkv, first)
        for src, base, factor in ((dqc_r, P_QC, CA_SCALE), (dkc_r, P_KC, 1.0 / LOG2E), (dvc_r, P_VC, 1.0)):
            for p in range(4):
                dp_ref[:, base + p * 128:base + (p + 1) * 128] = (
                    (src[2 * p].astype(F32) + pltpu.roll(src[2 * p + 1].astype(F32), 64, 1)) * factor).astype(BF16)

    col = lambda w, c: pl.BlockSpec((ts, w), lambda i: (i, c))
    full = lambda shape: pl.BlockSpec(shape, lambda i: (0,) * len(shape))
    wide = jax.ShapeDtypeStruct((S, 1024), BF16)
    return _call(
        body, name="prep_bwd", grid=(S // ts,),
        in_specs=[pl.BlockSpec(memory_space=pl.ANY)] + [_hspec(ts)] * 6 +
                 [col(256, P_QD // 256), col(128, P_KVD // 128),
                  pl.BlockSpec((3, ts, 128), lambda i: (0, i, 0)),
                  full((1, MLA_Q_RANK)), full((1, MLA_KV_RANK)),
                  full((MLA_Q_RANK, 1024)), full((MLA_KV_RANK, 1024)), full((MLA_KV_RANK, 1024))],
        out_specs=[pl.BlockSpec((ts, 2048), lambda i: (i, 0))] + [pl.BlockSpec((ts, 1024), lambda i: (i, 0))] * 3 +
                  [full((1, MLA_Q_RANK)), full((1, MLA_KV_RANK))],
        out_shape=[jax.ShapeDtypeStruct(dproj.shape, BF16), wide, wide, wide,
                   jax.ShapeDtypeStruct((1, MLA_Q_RANK), F32), jax.ShapeDtypeStruct((1, MLA_KV_RANK), F32)],
        input_output_aliases={0: 0},
        compiler_params=_params(1))(dproj, dqb, dkb, dvb, dqc, dkc, dvc, proj, proj, tab, qg, kvg, wq, wk, wv)


def _diag_visible(t):
    r = lax.broadcasted_iota(jnp.int32, (t, t), 0) >> CHUNK_SHIFT
    c = lax.broadcasted_iota(jnp.int32, (t, t), 1) >> CHUNK_SHIFT
    return r <= c


def _pair_tables(nq, kv_major):
    if kv_major:
        pairs = [(kb, qi) for kb in range(nq) for qi in range(kb, nq)]
    else:
        pairs = [(kb, qi) for qi in range(nq) for kb in range(qi + 1)]
    return (jnp.asarray(np.array([p[0] for p in pairs], np.int32)),
            jnp.asarray(np.array([p[1] for p in pairs], np.int32)), len(pairs))


def _finish_softmax(acc, m):
    l = acc[MLA_V:MLA_V + 1, :]
    row = lax.broadcasted_iota(jnp.int32, acc.shape, 0)
    return jnp.where(row < MLA_V, acc / l, 0.0).T.astype(BF16), m + jnp.log2(l)


def _split_refs(refs, counts):
    out, pos = [], 0
    for c in counts:
        out.append(refs[pos:pos + c])
        pos += c
    return out


def mla_fwd(q, k, vt, exchange=()):
    H, S, _ = q.shape
    t, hb, n_ex = ATT_T, MLA_FWD_HEADS_PER_STEP, len(exchange)
    kb_tab, qi_tab, n_pairs = _pair_tables(S // t, False)

    def body(kb_ref, qi_ref, q_ref, k_ref, vt_ref, *rest):
        ex_src, (o_ref, lse_ref), ex_out, (m_s, acc_s), ex_sems = _split_refs(rest, (n_ex, 2, n_ex, 2, 3 if n_ex else 0))
        hg, p_id = pl.program_id(0), pl.program_id(1)
        kb, qi = kb_ref[p_id], qi_ref[p_id]

        if n_ex:
            @pl.when(jnp.logical_and(hg == 0, p_id == 0))
            def _():
                _exchange_issue(ex_src, ex_out, ex_sems, False, True)

        @pl.when(kb == 0)
        def _():
            m_s[...] = jnp.full_like(m_s, NEG_INF)
            acc_s[...] = jnp.zeros_like(acc_s)

        def step(masked):
            for h in range(hb):
                st = _nt(k_ref[h], q_ref[h])
                if masked:
                    st = jnp.where(_diag_visible(t), st, NEG_INF)
                m_prev = m_s[h]
                m_new = jnp.maximum(m_prev, jnp.max(st, axis=0, keepdims=True))
                p = jnp.exp2(st - m_new)
                acc_s[h] = jnp.exp2(m_prev - m_new) * acc_s[h] + _nn(vt_ref[h], p.astype(BF16))
                m_s[h] = m_new

        @pl.when(kb < qi)
        def _():
            step(False)

        @pl.when(kb == qi)
        def _():
            step(True)
            for h in range(hb):
                o_ref[h], lse_ref[h] = _finish_softmax(acc_s[h], m_s[h])

        if n_ex:
            @pl.when(jnp.logical_and(hg == H // hb - 1, p_id == n_pairs - 1))
            def _():
                _exchange_issue(ex_src, ex_out, ex_sems, False, False)

    grid_spec = pltpu.PrefetchScalarGridSpec(
        num_scalar_prefetch=2, grid=(H // hb, n_pairs),
        in_specs=[pl.BlockSpec((hb, t, HEAD_PAD), lambda h, p, kb, qi: (h, qi[p], 0)),
                  pl.BlockSpec((hb, t, HEAD_PAD), lambda h, p, kb, qi: (h, kb[p], 0)),
                  pl.BlockSpec((hb, HEAD_PAD, t), lambda h, p, kb, qi: (h, 0, kb[p]))] + [ANY] * n_ex,
        out_specs=[pl.BlockSpec((hb, t, HEAD_PAD), lambda h, p, kb, qi: (h, qi[p], 0)),
                   pl.BlockSpec((hb, 1, t), lambda h, p, kb, qi: (h, 0, qi[p]))] + [ANY] * n_ex,
        scratch_shapes=[pltpu.VMEM((hb, 1, t), F32), pltpu.VMEM((hb, HEAD_PAD, t), F32)] + _exchange_sems(n_ex))
    outs = _call(
        body, name="mla_fwd_gather" if n_ex else "mla_fwd", grid_spec=grid_spec,
        out_shape=[jax.ShapeDtypeStruct((H, S, HEAD_PAD), BF16), jax.ShapeDtypeStruct((H, 1, S), F32)] +
        _exchange_out_shape(exchange, False),
        compiler_params=_params(2))(kb_tab, qi_tab, q, k, vt, *exchange)
    return outs[0], outs[1], outs[2:]


def mla_bwd(q, k, kt, v, do, lse, delta, exchange=()):
    H, S, _ = q.shape
    t, hb, n_ex = ATT_T, MLA_HEADS_PER_STEP, len(exchange)
    nq = S // t
    kb_tab, qi_tab, n_pairs = _pair_tables(nq, True)

    def body(kb_ref, qi_ref, q_ref, k_ref, kt_ref, v_ref, do_ref, lse_ref, dl_ref, *rest):
        ex_src, (dq_ref, dk_ref, dv_ref), ex_out, (dqt_s, dk_s, dv_s), ex_sems = _split_refs(
            rest, (n_ex, 3, n_ex, 3, 3 if n_ex else 0))
        hg, p_id = pl.program_id(0), pl.program_id(1)
        kb, qi = kb_ref[p_id], qi_ref[p_id]

        if n_ex:
            @pl.when(jnp.logical_and(hg == 0, p_id == 0))
            def _():
                _exchange_issue(ex_src, ex_out, ex_sems, True, True)

        @pl.when(p_id == 0)
        def _():
            dqt_s[...] = jnp.zeros_like(dqt_s)

        @pl.when(qi == kb)
        def _():
            dk_s[...] = jnp.zeros_like(dk_s)
            dv_s[...] = jnp.zeros_like(dv_s)

        def step(masked):
            for h in range(hb):
                st = _nt(k_ref[h], q_ref[h])
                if masked:
                    st = jnp.where(_diag_visible(t), st, NEG_INF)
                pt = jnp.exp2(st - lse_ref[h])
                dv_s[h] += _nn(pt.astype(BF16), do_ref[h])
                dsb = (pt * (_nt(v_ref[h], do_ref[h]) - dl_ref[h])).astype(BF16)
                dk_s[h] += _nn(dsb, q_ref[h])
                dqt_s[h, qi] += _nn(kt_ref[h], dsb)

        @pl.when(qi == kb)
        def _():
            step(True)
            rows = pl.ds(pl.multiple_of(qi * t, t), t)
            for h in range(hb):
                dq_ref[h, rows, :] = dqt_s[h, qi].T.astype(BF16)

        @pl.when(qi > kb)
        def _():
            step(False)

        @pl.when(qi == nq - 1)
        def _():
            dk_ref[...] = dk_s[...].astype(BF16)
            dv_ref[...] = dv_s[...].astype(BF16)

        if n_ex:
            @pl.when(jnp.logical_and(hg == H // hb - 1, p_id == n_pairs - 1))
            def _():
                _exchange_issue(ex_src, ex_out, ex_sems, True, False)

    qtile = pl.BlockSpec((hb, t, HEAD_PAD), lambda h, p, kb, qi: (h, qi[p], 0))
    ktile = pl.BlockSpec((hb, t, HEAD_PAD), lambda h, p, kb, qi: (h, kb[p], 0))
    stat = pl.BlockSpec((hb, 1, t), lambda h, p, kb, qi: (h, 0, qi[p]))
    grid_spec = pltpu.PrefetchScalarGridSpec(
        num_scalar_prefetch=2, grid=(H // hb, n_pairs),
        in_specs=[qtile, ktile, pl.BlockSpec((hb, HEAD_PAD, t), lambda h, p, kb, qi: (h, 0, kb[p])), ktile, qtile,
                  stat, stat] + [ANY] * n_ex,
        out_specs=[pl.BlockSpec((hb, S, HEAD_PAD), lambda h, p, kb, qi: (h, 0, 0)), ktile, ktile] + [ANY] * n_ex,
        scratch_shapes=[pltpu.VMEM((hb, nq, HEAD_PAD, t), F32), pltpu.VMEM((hb, t, HEAD_PAD), F32),
                        pltpu.VMEM((hb, t, HEAD_PAD), F32)] + _exchange_sems(n_ex))
    outs = _call(
        body, name="mla_bwd_scatter" if n_ex else "mla_bwd", grid_spec=grid_spec,
        out_shape=[jax.ShapeDtypeStruct((H, S, HEAD_PAD), BF16)] * 3 + _exchange_out_shape(exchange, True),
        compiler_params=_params(2))(kb_tab, qi_tab, q, k, kt, v, do, lse, delta, *exchange)
    return outs[0], outs[1], outs[2], outs[3:]


def _band_specs(t, hb):
    prev = lambda i: jnp.maximum(i - 1, 0)
    return dict(
        cur=pl.BlockSpec((hb, t, HEAD_PAD), lambda h, i: (h, i, 0)),
        prev=pl.BlockSpec((hb, t, HEAD_PAD), lambda h, i: (h, prev(i), 0)),
        cur_t=pl.BlockSpec((hb, HEAD_PAD, t), lambda h, i: (h, 0, i)),
        prev_t=pl.BlockSpec((hb, HEAD_PAD, t), lambda h, i: (h, 0, prev(i))),
        stat=pl.BlockSpec((hb, 1, t), lambda h, i: (h, 0, i)),
        bias_prev=pl.BlockSpec((hb, 1, t, t), lambda h, i: (h, jnp.where(i == 0, 1, 0), 0, 0)),
        bias_cur=pl.BlockSpec((hb, 1, t, t), lambda h, i: (h, 2, 0, 0)))


def band_fwd(q, k, vt, bias):
    H, S, _ = q.shape
    t, hb = ATT_T, MLA_HEADS_PER_STEP
    sp = _band_specs(t, hb)

    def body(q_ref, kp_ref, kc_ref, vtp_ref, vtc_ref, bp_ref, bc_ref, o_ref, lse_ref):
        for h in range(hb):
            s0 = _nt(kp_ref[h], q_ref[h]) + bp_ref[h, 0]
            s1 = _nt(kc_ref[h], q_ref[h]) + bc_ref[h, 0]
            m = jnp.maximum(jnp.max(s0, axis=0, keepdims=True), jnp.max(s1, axis=0, keepdims=True))
            ot = (_nn(vtp_ref[h], jnp.exp2(s0 - m).astype(BF16)) +
                  _nn(vtc_ref[h], jnp.exp2(s1 - m).astype(BF16)))
            o_ref[h], lse_ref[h] = _finish_softmax(ot, m)

    return _call(
        body, name="band_fwd", grid=(H // hb, S // t),
        in_specs=[sp['cur'], sp['prev'], sp['cur'], sp['prev_t'], sp['cur_t'], sp['bias_prev'], sp['bias_cur']],
        out_specs=[sp['cur'], sp['stat']],
        out_shape=[jax.ShapeDtypeStruct((H, S, HEAD_PAD), BF16), jax.ShapeDtypeStruct((H, 1, S), F32)],
        compiler_params=_params(2))(q, k, k, vt, vt, bias, bias)


def band_bwd(q, k, kt, v, do, lse, delta, bias):
    H, S, _ = q.shape
    t = ATT_T
    sp = _band_specs(t, 1)

    def body(q_ref, kp_ref, kc_ref, ktp_ref, ktc_ref, vp_ref, vc_ref, do_ref, lse_ref, dl_ref, bp_ref, bc_ref,
             dq_ref, dk_ref, dv_ref, db_ref):
        i = pl.program_id(1)

        @pl.when(i == 0)
        def _():
            dk_ref[...] = jnp.zeros_like(dk_ref)
            dv_ref[...] = jnp.zeros_like(dv_ref)
            db_ref[...] = jnp.zeros_like(db_ref)

        qv, dov = q_ref[0], do_ref[0]
        dqt = jnp.zeros((HEAD_PAD, t), F32)
        windows = ((0, jnp.maximum(i - 1, 0), kp_ref, ktp_ref, vp_ref, bp_ref),
                   (1, i, kc_ref, ktc_ref, vc_ref, bc_ref))
        for w, blk, k_ref, kt_ref, v_ref, b_ref in windows:
            rows = pl.ds(pl.multiple_of(blk * t, t), t)
            pt = jnp.exp2(_nt(k_ref[0], qv) + b_ref[0, 0] - lse_ref[0])
            dv_ref[0, rows, :] += _nn(pt.astype(BF16), dov)
            ds = pt * (_nt(v_ref[0], dov) - dl_ref[0])
            db_ref[0, w] += ds
            dsb = ds.astype(BF16)
            dk_ref[0, rows, :] += _nn(dsb, qv)
            dqt = dqt + _nn(kt_ref[0], dsb)
        dq_ref[0] = dqt.T.astype(BF16)

    whole = pl.BlockSpec((1, S, HEAD_PAD), lambda h, i: (h, 0, 0))
    return _call(
        body, name="band_bwd", grid=(H, S // t),
        in_specs=[sp['cur'], sp['prev'], sp['cur'], sp['prev_t'], sp['cur_t'], sp['prev'], sp['cur'], sp['cur'],
                  sp['stat'], sp['stat'], sp['bias_prev'], sp['bias_cur']],
        out_specs=[sp['cur'], whole, whole, pl.BlockSpec((1, 2, t, t), lambda h, i: (h, 0, 0, 0))],
        out_shape=[jax.ShapeDtypeStruct((H, S, HEAD_PAD), BF16), jax.ShapeDtypeStruct((H, S, HEAD_PAD), F32),
                   jax.ShapeDtypeStruct((H, S, HEAD_PAD), F32), jax.ShapeDtypeStruct((H, 2, t, t), F32)],
        compiler_params=_params(2))(q, k, k, kt, kt, v, v, do, lse, delta, bias, bias)


def _compact(o_ref):
    return jnp.concatenate([o_ref[2 * p].astype(F32) + pltpu.roll(o_ref[2 * p + 1].astype(F32), 64, 1)
                            for p in range(4)], axis=1)


def merge_fwd(ob, oc, proj, ya, gate_b, wbr, w_out, x, post_g):
    S = x.shape[0]
    ts = ROW_TILE

    def body(ob_ref, oc_ref, zb_ref, zc_ref, ya_ref, gl_ref, gb_ref, wbr_ref, wo_ref, x_ref, pg_ref,
             xo_ref, yb_ref, yc_ref, mg_ref, out_ref):
        zb, zc = zb_ref[...], zc_ref[...]
        yb = (_compact(ob_ref) * (zb * _sigmoid(zb))).astype(BF16)
        yc = (_compact(oc_ref) * (zc * _sigmoid(zc))).astype(BF16)
        yb_ref[...] = yb
        yc_ref[...] = yc
        merged = jnp.zeros((ts, D_MODEL), F32)
        for n, y in enumerate((ya_ref[...], yb, yc)):
            cols = slice(n * D_MODEL, (n + 1) * D_MODEL)
            gate = _sigmoid(gl_ref[:, cols] + gb_ref[:, cols])
            merged = merged + gate * _nn(y, wbr_ref[n])
        mb = merged.astype(BF16)
        mg_ref[...] = mb
        out = _nn(mb, wo_ref[...])
        out_ref[...] = out
        normed, _ = _rms(out, pg_ref[...])
        xo_ref[...] = x_ref[...] + normed

    row = lambda w: pl.BlockSpec((ts, w), lambda i: (i, 0))
    col = lambda w, c: pl.BlockSpec((ts, w), lambda i: (i, c))
    full = lambda shape: pl.BlockSpec(shape, lambda i: (0,) * len(shape))
    return _call(
        body, name="merge_fwd", grid=(S // ts,),
        in_specs=[_hspec(ts), _hspec(ts), col(512, P_ZB // 512), col(512, P_ZC // 512), row(512),
                  col(3072, P_G // 3072), full((1, 3072)), full((3, BR_WIDTH, D_MODEL)),
                  full((D_MODEL, D_MODEL)), row(D_MODEL), full((1, D_MODEL))],
        out_specs=[row(D_MODEL), row(512), row(512), row(D_MODEL), row(D_MODEL)],
        out_shape=[jax.ShapeDtypeStruct((S, D_MODEL), F32), jax.ShapeDtypeStruct((S, 512), BF16),
                   jax.ShapeDtypeStruct((S, 512), BF16), jax.ShapeDtypeStruct((S, D_MODEL), BF16),
                   jax.ShapeDtypeStruct((S, D_MODEL), F32)],
        compiler_params=_params(1))(ob, oc, proj, proj, ya, proj, gate_b, wbr, w_out, x, post_g)


def post_bwd(g, out, post_g, w_out):
    S = g.shape[0]
    ts = ROW_TILE

    def body(g_ref, out_ref, pg_ref, wo_ref, do_ref, dm_ref, gp_ref):
        ov = out_ref[...]
        _, r = _rms(ov, pg_ref[...])
        dout, gp = _rms_bwd(ov, pg_ref[...], r, g_ref[...])
        db = dout.astype(BF16)
        do_ref[...] = db
        dm_ref[...] = _nt(db, wo_ref[...])
        _accumulate(gp_ref, gp, pl.program_id(0) == 0)

    row = lambda: pl.BlockSpec((ts, D_MODEL), lambda i: (i, 0))
    full = lambda shape: pl.BlockSpec(shape, lambda i: (0,) * len(shape))
    return _call(
        body, name="post_bwd", grid=(S // ts,),
        in_specs=[row(), row(), full((1, D_MODEL)), full((D_MODEL, D_MODEL))],
        out_specs=[row(), row(), full((1, D_MODEL))],
        out_shape=[jax.ShapeDtypeStruct((S, D_MODEL), BF16), jax.ShapeDtypeStruct((S, D_MODEL), F32),
                   jax.ShapeDtypeStruct((1, D_MODEL), F32)],
        compiler_params=_params(1))(g, out, post_g, w_out)


def gate_bwd(dmerged, proj, gate_b, ya, yb, yc, wbr):
    S = dmerged.shape[0]
    ts = ROW_TILE

    def body(dm_ref, gl_ref, gb_ref, ya_ref, yb_ref, yc_ref, wbr_ref,
             dp_ref, dba_ref, dbb_ref, dbc_ref, dy_ref, ggb_ref):
        dm = dm_ref[...]
        ggb = []
        for n, (y_ref, dbr_ref) in enumerate(((ya_ref, dba_ref), (yb_ref, dbb_ref), (yc_ref, dbc_ref))):
            cols = slice(n * D_MODEL, (n + 1) * D_MODEL)
            br = _nn(y_ref[...], wbr_ref[n])
            sg = _sigmoid(gl_ref[:, cols] + gb_ref[:, cols])
            dgl = dm * br * (sg * (1.0 - sg))
            dp_ref[:, cols] = dgl.astype(BF16)
            ggb.append(jnp.sum(dgl, axis=0, keepdims=True))
            dbr = (dm * sg).astype(BF16)
            dbr_ref[...] = dbr
            dy_ref[n] = _nt(dbr, wbr_ref[n])
        _accumulate(ggb_ref, jnp.concatenate(ggb, axis=1), pl.program_id(0) == 0)

    row = lambda w: pl.BlockSpec((ts, w), lambda i: (i, 0))
    full = lambda shape: pl.BlockSpec(shape, lambda i: (0,) * len(shape))
    wide = jax.ShapeDtypeStruct((S, D_MODEL), BF16)
    return _call(
        body, name="gate_bwd", grid=(S // ts,),
        in_specs=[row(D_MODEL), pl.BlockSpec((ts, 3072), lambda i: (i, P_G // 3072)), full((1, 3072)),
                  row(512), row(512), row(512), full((3, BR_WIDTH, D_MODEL))],
        out_specs=[pl.BlockSpec((ts, 3072), lambda i: (i, P_G // 3072)), row(D_MODEL), row(D_MODEL), row(D_MODEL),
                   pl.BlockSpec((3, ts, 512), lambda i: (0, i, 0)), full((1, 3072))],
        out_shape=[jax.ShapeDtypeStruct((S, P_W), BF16), wide, wide, wide,
                   jax.ShapeDtypeStruct((3, S, 512), F32), jax.ShapeDtypeStruct((1, 3072), F32)],
        compiler_params=_params(1))(dmerged, proj, gate_b, ya, yb, yc, wbr)


def ungate_bwd(dproj, dy, ob, oc, proj):
    S = proj.shape[0]
    ts = ROW_TILE

    def body(dp_in, dyb_ref, dyc_ref, ob_ref, oc_ref, zb_ref, zc_ref, dp_ref, dob_ref, doc_ref, dlb_ref, dlc_ref):
        del dp_in
        lane = lax.broadcasted_iota(jnp.int32, (ts, 128), 1)
        for n, (dy_ref, o_ref, z_ref, do_ref, dl_ref) in enumerate(
                ((dyb_ref, ob_ref, zb_ref, dob_ref, dlb_ref), (dyc_ref, oc_ref, zc_ref, doc_ref, dlc_ref))):
            zz = z_ref[...]
            dyv = dy_ref[0]
            sg = _sigmoid(zz)
            dp_ref[:, n * 512:(n + 1) * 512] = (dyv * _compact(o_ref) * (sg * (1.0 + zz * (1.0 - sg)))).astype(BF16)
            do_c = dyv * (zz * sg)
            for p in range(4):
                piece = do_c[:, p * 128:(p + 1) * 128]
                for h, d in ((2 * p, jnp.where(lane < 64, piece, 0.0)),
                             (2 * p + 1, jnp.where(lane < 64, pltpu.roll(piece, 64, 1), 0.0))):
                    do_ref[h] = d.astype(BF16)
                    dl_ref[h] = jnp.sum(d * o_ref[h].astype(F32), axis=-1, keepdims=True)

    col = lambda c: pl.BlockSpec((ts, 512), lambda i: (i, c))
    dysp = lambda n: pl.BlockSpec((1, ts, 512), lambda i: (n, i, 0))
    stat = pl.BlockSpec((N_HEADS, ts, 1), lambda i: (0, i, 0))
    hshape = jax.ShapeDtypeStruct((N_HEADS, S, HEAD_PAD), BF16)
    sshape = jax.ShapeDtypeStruct((N_HEADS, S, 1), F32)
    return _call(
        body, name="ungate_bwd", grid=(S // ts,),
        in_specs=[pl.BlockSpec(memory_space=pl.ANY), dysp(1), dysp(2), _hspec(ts), _hspec(ts),
                  col(P_ZB // 512), col(P_ZC // 512)],
        out_specs=[pl.BlockSpec((ts, 1024), lambda i: (i, P_ZB // 1024)), _hspec(ts), _hspec(ts), stat, stat],
        out_shape=[jax.ShapeDtypeStruct(dproj.shape, BF16), hshape, hshape, sshape, sshape],
        input_output_aliases={0: 0},
        compiler_params=_params(1))(dproj, dy, dy, ob, oc, proj, proj)


def loss_head(y, target):
    S, D = y.shape
    ts = ROW_TILE

    def body(y_ref, t_ref, dy_ref, sq_ref):
        d = y_ref[...] - t_ref[...]
        dy_ref[...] = d * (1.0 / D)
        _accumulate(sq_ref, jnp.sum(d * d, axis=0, keepdims=True), pl.program_id(0) == 0)

    row = pl.BlockSpec((ts, D), lambda i: (i, 0))
    return _call(
        body, name="loss_head", grid=(S // ts,), in_specs=[row, row],
        out_specs=[row, pl.BlockSpec((1, D), lambda i: (0, 0))],
        out_shape=[jax.ShapeDtypeStruct((S, D), F32), jax.ShapeDtypeStruct((1, D), F32)],
        compiler_params=_params(1))(y, target)


def _row_tile(rows, cols):
    for cand in (1024, 512, 256, 128, 64, 32, 16, 8):
        if rows % cand == 0 and cand * cols * 4 <= 1024 * 1024:
            return cand
    return rows


def adamw(w, grads, m, v):
    shape = w.shape
    cols = shape[-1]
    rows = int(np.prod(shape[:-1]))
    tr = _row_tile(rows, cols)
    n_g = len(grads)
    c1 = 1.0 - ADAM_B1 ** ADAM_STEP
    c2 = 1.0 - ADAM_B2 ** ADAM_STEP

    def body(*refs):
        w_ref, m_ref, v_ref = refs[:3]
        g_refs = refs[3:3 + n_g]
        go_ref, d_ref, mo_ref, vo_ref = refs[3 + n_g:]
        gv = g_refs[0][...]
        for g_ref in g_refs[1:]:
            gv = gv + g_ref[...]
        go_ref[...] = gv
        mn = ADAM_B1 * m_ref[...] + (1.0 - ADAM_B1) * gv
        vn = ADAM_B2 * v_ref[...] + (1.0 - ADAM_B2) * (gv * gv)
        mo_ref[...] = mn
        vo_ref[...] = vn
        d_ref[...] = -ADAM_LR * ((mn / c1) / (jnp.sqrt(vn / c2) + ADAM_EPS) + ADAM_WD * w_ref[...])

    blk = pl.BlockSpec((tr, cols), lambda i: (i, 0))
    sds = jax.ShapeDtypeStruct((rows, cols), F32)
    outs = _call(
        body, name="adamw", grid=(rows // tr,), in_specs=[blk] * (3 + n_g), out_specs=[blk] * 4,
        out_shape=[sds] * 4, compiler_params=_params(1))(
            *[a.reshape(rows, cols) for a in (w, m, v, *grads)])
    return [o.reshape(shape) for o in outs]


def add_lead(parts):
    n = parts.shape[0]
    shape = parts.shape[1:]
    cols = shape[-1]
    rows = int(np.prod(shape[:-1]))
    tr = _row_tile(rows, cols * n)

    def body(p_ref, o_ref):
        acc = p_ref[0].astype(F32)
        for s in range(1, n):
            acc = acc + p_ref[s].astype(F32)
        o_ref[...] = acc

    out = _call(
        body, name="add_lead", grid=(rows // tr,),
        in_specs=[pl.BlockSpec((n, tr, cols), lambda i: (0, i, 0))],
        out_specs=pl.BlockSpec((tr, cols), lambda i: (i, 0)),
        out_shape=jax.ShapeDtypeStruct((rows, cols), F32),
        compiler_params=_params(1))(parts.reshape(n, rows, cols))
    return out.reshape(shape)


ANY = pl.BlockSpec(memory_space=pl.ANY)


def _other_chips(x, y):
    return [(1 - x, y), (x, 1 - y), (1 - x, 1 - y)]


def chip_exchange(arrays, scatter, name):
    n = len(arrays)

    def body(*refs):
        _exchange_issue(refs[:n], refs[n:2 * n], refs[2 * n:], scatter, True)
        _exchange_issue(refs[:n], refs[n:2 * n], refs[2 * n:], scatter, False)

    return _call(
        body, name=name, in_specs=[ANY] * n, out_specs=[ANY] * n,
        out_shape=_exchange_out_shape(arrays, scatter), scratch_shapes=_exchange_sems(n))(*arrays)


def _exchange_out_shape(arrays, scatter):
    return [jax.ShapeDtypeStruct(a.shape if scatter else (4,) + a.shape, a.dtype) for a in arrays]


def _exchange_sems(n):
    if n == 0:
        return []
    return [pltpu.SemaphoreType.DMA((3 * n,)), pltpu.SemaphoreType.DMA((3 * n,)), pltpu.SemaphoreType.DMA((n,))]


def _exchange_issue(srcs, outs, sems, scatter, start):
    send_sems, recv_sems, local_sems = sems
    x, y, c = lax.axis_index("x"), lax.axis_index("y"), lax.axis_index("c")
    me = 2 * x + y
    for a in range(len(srcs)):
        local_src = srcs[a].at[me] if scatter else srcs[a]
        mine = pltpu.make_async_copy(local_src, outs[a].at[me], local_sems.at[a])
        sends = []
        for j, (px, py) in enumerate(_other_chips(x, y)):
            pair = dict(send_sem=send_sems.at[3 * a + j], recv_sem=recv_sems.at[3 * a + j],
                        device_id=(px, py, c), device_id_type=MESH)
            sends.append(pltpu.make_async_remote_copy(
                src_ref=srcs[a].at[2 * px + py] if scatter else srcs[a], dst_ref=outs[a].at[me], **pair))
            if not start:
                pltpu.make_async_remote_copy(src_ref=local_src, dst_ref=outs[a].at[2 * px + py], **pair).wait_recv()
        if start:
            mine.start()
            for cp in sends:
                cp.start()
        else:
            for cp in sends:
                cp.wait_send()
            mine.wait()


def sibling_exchange(arrays):
    n = len(arrays)

    def body(*refs):
        srcs, outs = refs[:n], refs[n:2 * n]
        send_sems, recv_sems = refs[2 * n:]
        x, y, c = lax.axis_index("x"), lax.axis_index("y"), lax.axis_index("c")
        copies = [pltpu.make_async_remote_copy(src_ref=srcs[a], dst_ref=outs[a], send_sem=send_sems.at[a],
                                               recv_sem=recv_sems.at[a], device_id=(x, y, 1 - c), device_id_type=MESH)
                  for a in range(n)]
        for cp in copies:
            cp.start()
        for cp in copies:
            cp.wait()

    return _call(
        body, name="sibling_exchange", in_specs=[ANY] * n, out_specs=[ANY] * n,
        out_shape=[jax.ShapeDtypeStruct(a.shape, a.dtype) for a in arrays],
        scratch_shapes=[pltpu.SemaphoreType.DMA((n,)), pltpu.SemaphoreType.DMA((n,))])(*arrays)


def _perm_from_shards(sh):
    rows = sh.shape[1]
    pieces, pos = [], 0
    for lo, hi, plo in sorted(NAT_SEGS, key=lambda s: s[2]):
        if plo > pos:
            pieces.append(jnp.zeros((rows, plo - pos), sh.dtype))
            pos = plo
        c = lo
        while c < hi:
            kk = c // SHARD_COLS
            e = min(hi, (kk + 1) * SHARD_COLS)
            pieces.append(sh[kk][:, c - kk * SHARD_COLS:e - kk * SHARD_COLS])
            c = e
        pos += hi - lo
    if pos < P_W:
        pieces.append(jnp.zeros((rows, P_W - pos), sh.dtype))
    return jnp.concatenate(pieces, axis=1)


def _shards_from_perm(p):
    out = []
    for kk in range(4):
        lo_k, hi_k = kk * SHARD_COLS, (kk + 1) * SHARD_COLS
        pieces = []
        for lo, hi, plo in NAT_SEGS:
            a, b = max(lo, lo_k), min(hi, hi_k)
            if a < b:
                pieces.append(p[:, plo + (a - lo):plo + (b - lo)])
        out.append(jnp.concatenate(pieces, axis=1))
    return jnp.stack(out)


def _split4(a, axis):
    shape = a.shape
    a = a.reshape(shape[:axis] + (4, shape[axis] // 4) + shape[axis + 1:])
    return jnp.moveaxis(a, axis, 0)


def _join4(a, axis):
    a = jnp.moveaxis(a, 0, axis)
    shape = a.shape
    return a.reshape(shape[:axis] + (4 * shape[axis + 1],) + shape[axis + 2:])


def _pad_heads(w, per_head, lo, hi):
    r = w.shape[0]
    wh = w.reshape(r, N_HEADS, per_head)[:, :, lo:hi]
    return jnp.pad(wh, ((0, 0), (0, 0), (0, HEAD_PAD - (hi - lo)))).reshape(r, N_HEADS * HEAD_PAD)


def _rope_table(S):
    half = MLA_ROPE // 2
    inv = ROPE_BASE ** (-jnp.arange(half, dtype=F32) / half)
    ang = jnp.arange(S).astype(F32)[:, None] * inv[None, :]
    cos, sin = jnp.cos(ang), jnp.sin(ang)
    z = lambda n: jnp.zeros((S, n), F32)
    c = jnp.concatenate([jnp.ones((S, MLA_NOPE), F32), cos, cos, z(32)], axis=1)
    a = jnp.concatenate([z(MLA_NOPE), -sin, z(48)], axis=1)
    b = jnp.concatenate([z(MLA_NOPE + half), sin, z(32)], axis=1)
    return jnp.stack([c, a, b])


def _band_onehot():
    t = ATT_T
    m = np.arange(2 * t)
    d = np.where(m < t, m, m - 2 * t)
    idx = np.stack([np.clip(off + d, -REL_CLIP, REL_CLIP) + REL_CLIP for off in (t, 0)])
    return (idx[:, :, None] == np.arange(2 * REL_CLIP + 1)[None, None, :]).astype(np.float32)


def bias_expand(diag):
    t = ATT_T

    def body(d_ref, o_ref):
        kc = lax.broadcasted_iota(jnp.int32, (t, t), 0) >> CHUNK_SHIFT
        qc = lax.broadcasted_iota(jnp.int32, (t, t), 1) >> CHUNK_SHIFT
        for w, visible in ((0, kc >= qc), (1, kc <= qc)):
            rows = jnp.broadcast_to(d_ref[0, w:w + 1, :], (t, 2 * t))
            skew = pltpu.roll(rows, 0, 1, stride=1, stride_axis=0)[:, :t]
            o_ref[0, 2 * w] = jnp.where(visible, skew * LOG2E, NEG_INF)
        o_ref[0, 1] = jnp.full((t, t), NEG_INF, F32)

    return _call(
        body, name="bias_expand", grid=(N_HEADS,),
        in_specs=[pl.BlockSpec((1, 2, 2 * t), lambda h: (h, 0, 0))],
        out_specs=pl.BlockSpec((1, 3, t, t), lambda h: (h, 0, 0, 0)),
        out_shape=jax.ShapeDtypeStruct((N_HEADS, 3, t, t), F32),
        compiler_params=_params(1))(diag)


def bias_fold(dtiles):
    t = ATT_T

    def body(d_ref, o_ref):
        pad = jnp.zeros((8, t), F32)
        for w in range(2):
            acc = jnp.concatenate([d_ref[0, w, 0:8, :], pad], axis=1)
            for g in range(1, t // 8):
                grp = jnp.concatenate([d_ref[0, w, 8 * g:8 * g + 8, :], pad], axis=1)
                acc = acc + pltpu.roll(grp, 2 * t - 8 * g, 1)
            out = acc[0:1, :]
            for s in range(1, 8):
                out = out + pltpu.roll(acc, 2 * t - s, 1)[s:s + 1, :]
            o_ref[0, w:w + 1, :] = out

    return _call(
        body, name="bias_fold", grid=(N_HEADS,),
        in_specs=[pl.BlockSpec((1, 2, t, t), lambda h: (h, 0, 0, 0))],
        out_specs=pl.BlockSpec((1, 2, 2 * t), lambda h: (h, 0, 0)),
        out_shape=jax.ShapeDtypeStruct((N_HEADS, 2, 2 * t), F32),
        compiler_params=_params(1))(dtiles)


def _bias_tiles(table):
    diag = jnp.einsum('hr,wdr->hwd', table, jnp.asarray(_band_onehot()), precision=lax.Precision.HIGHEST)
    return bias_expand(diag)


def _bias_tiles_grad(dtiles):
    return jnp.einsum('hwd,wdr->hr', bias_fold(dtiles), jnp.asarray(_band_onehot()),
                      precision=lax.Precision.HIGHEST)


def _layer_consts(lw):
    tri = np.tril(np.ones((SGU_BLOCK, SGU_BLOCK), np.float32))
    ws = (lw['sgu_w'] * tri).astype(BF16)
    return dict(
        ws=ws, ws_t=jnp.swapaxes(ws, 1, 2), sgu_bias=jnp.repeat(lw['sgu_b'].T, CA_HEAD_DIM, axis=1),
        bias=_bias_tiles(lw['ca_rel_bias']),
        wq=_pad_heads(lw['mla_w_uq'], MLA_QK, 0, MLA_QK),
        wk=_pad_heads(lw['mla_w_ukv'], MLA_NOPE + MLA_V, 0, MLA_NOPE),
        wv=_pad_heads(lw['mla_w_ukv'], MLA_NOPE + MLA_V, MLA_NOPE, MLA_NOPE + MLA_V),
        gate_b=lw['gate_b'].reshape(1, 3 * D_MODEL),
        pre_g=lw['pre_g'][None], post_g=lw['post_g'][None], ln_g=lw['sgu_ln_g'][None], ln_b=lw['sgu_ln_b'][None],
        qg=lw['mla_q_norm_g'][None], kvg=lw['mla_kv_norm_g'][None])


def _layer_fwd(x, lw, k, tab, next_shards):
    proj, xn = norm_matmul(x, k['pre_g'], lw['w_in'])
    ya = sgu_fwd(proj, k['ln_g'], k['ln_b'], k['ws'], k['sgu_bias'])
    qb, kb, vb, qc, kc, vc, kbt, vbt, kct, vct, cq, ckv = prep_fwd(proj, tab, k['qg'], k['kvg'], k['wq'], k['wk'],
                                                                   k['wv'])
    ob, lse_b, gathered = mla_fwd(qb, kb, vbt, next_shards)
    oc, lse_c = band_fwd(qc, kc, vct, k['bias'])
    x_new, yb, yc, merged, out = merge_fwd(ob, oc, proj, ya, k['gate_b'], lw['w_branch'], lw['w_out'], x,
                                           k['post_g'])
    saved = dict(x=x, proj=proj, xn=xn, ya=ya, yb=yb, yc=yc, qb=qb, kb=kb, vb=vb, qc=qc, kc=kc, vc=vc, kbt=kbt, kct=kct,
                 cq=cq, ckv=ckv, ob=ob, oc=oc, lse_b=lse_b, lse_c=lse_c, merged=merged, out=out)
    return x_new, saved, gathered


def _layer_bwd(g, s, lw, k, tab, pending_parts):
    S = g.shape[0]
    H = N_HEADS
    dout, dmerged, g_post = post_bwd(g, s['out'], k['post_g'], lw['w_out'])
    g_w_out = matmul_tn(s['merged'], dout, 512)
    dproj, dba, dbb, dbc, dy, g_gate_b = gate_bwd(dmerged, s['proj'], k['gate_b'], s['ya'], s['yb'], s['yc'],
                                                  lw['w_branch'])
    g_w_branch = jnp.stack([matmul_tn(y, d, 512) for y, d in ((s['ya'], dba), (s['yb'], dbb), (s['yc'], dbc))])
    dproj, dob, doc, dl_b, dl_c = ungate_bwd(dproj, dy, s['ob'], s['oc'], s['proj'])
    row = lambda a: a.reshape(H, 1, S)
    dqb, dkb, dvb, landed = mla_bwd(s['qb'], s['kb'], s['kbt'], s['vb'], dob, s['lse_b'], row(dl_b), pending_parts)
    dqc, dkc, dvc, dbias = band_bwd(s['qc'], s['kc'], s['kct'], s['vc'], doc, s['lse_c'], row(dl_c), k['bias'])
    dproj, dqf, dkf, dvf, g_qg, g_kvg = prep_bwd(dproj, dqb, dkb, dvb, dqc, dkc, dvc, s['proj'], tab,
                                                 k['qg'], k['kvg'], k['wq'], k['wk'], k['wv'])
    g_wq = matmul_tn(s['cq'], dqf, 512).reshape(MLA_Q_RANK, H, HEAD_PAD)[:, :, :MLA_QK]
    g_wk = matmul_tn(s['ckv'], dkf, 512).reshape(MLA_KV_RANK, H, HEAD_PAD)[:, :, :MLA_NOPE]
    g_wv = matmul_tn(s['ckv'], dvf, 512).reshape(MLA_KV_RANK, H, HEAD_PAD)[:, :, :MLA_V]
    dproj, g_ln_g, g_ln_b, g_ws, g_sgu_bias = sgu_bwd(dproj, dy, s['proj'], k['ln_g'], k['ln_b'], k['ws'],
                                                      k['ws_t'], k['sgu_bias'])
    g_w_in = matmul_tn(s['xn'], dproj, MM_TN)
    dx, g_pre = proj_bwd_x(dproj, lw['w_in'], s['x'], k['pre_g'], g)
    tri = np.tril(np.ones((SGU_BLOCK, SGU_BLOCK), np.float32))
    grads = dict(
        w_in=g_w_in, pre_g=g_pre[0], post_g=g_post[0], sgu_ln_g=g_ln_g[0], sgu_ln_b=g_ln_b[0],
        sgu_w=g_ws * tri, sgu_b=jnp.sum(g_sgu_bias.reshape(SGU_BLOCK, 8, CA_HEAD_DIM), axis=2).T,
        mla_q_norm_g=g_qg[0], mla_kv_norm_g=g_kvg[0],
        mla_w_uq=g_wq.reshape(MLA_Q_RANK, H * MLA_QK),
        mla_w_ukv=jnp.concatenate([g_wk, g_wv], axis=2).reshape(MLA_KV_RANK, H * (MLA_NOPE + MLA_V)),
        ca_rel_bias=_bias_tiles_grad(dbias), w_branch=g_w_branch,
        gate_b=g_gate_b.reshape(N_BRANCH, D_MODEL), w_out=g_w_out)
    return dx, grads, landed


BF16_PARTS = ('w_in', 'mla_w_uq', 'mla_w_ukv', 'w_branch', 'w_out')


def _weight_shards(w, l):
    return [w[n][l].astype(BF16) if n in BF16_PARTS else w[n][l] for n in SHARDED]


def _full_weights(gathered, small):
    lw = {n: _join4(a, SHARD_AXIS[n]) for n, a in zip(SHARDED, gathered) if n != 'w_in'}
    lw['w_in'] = _perm_from_shards(gathered[0])
    lw.update(small)
    return lw


def _small_pack(grads):
    flat = jnp.concatenate([grads[n].reshape(-1) for n in SMALL])
    quarter = -(-flat.size // (4 * 1024)) * 1024
    return jnp.pad(flat, (0, 4 * quarter - flat.size)).reshape(4, quarter // 128, 128)


def _grad_parts(grads):
    parts = [_shards_from_perm(grads['w_in'])]
    parts += [_split4(grads[n], SHARD_AXIS[n]) for n in SHARDED if n != 'w_in']
    parts = [p.astype(BF16) if n in BF16_PARTS else p for n, p in zip(SHARDED, parts)]
    return parts + [_small_pack(grads)]


def _sum_landed(landed):
    mine = [add_lead(p) for p in landed]
    return mine, sibling_exchange(mine)


def train_step_local(x, target, w):
    S = x.shape[0]
    depth = w['w_in'].shape[0]
    tab = _rope_table(S)
    gathered = chip_exchange(_weight_shards(w, 0), False, "gather_weights")
    layer_w, consts, saved = [], [], []
    for l in range(depth):
        lw = _full_weights(gathered, {n: w[n][l] for n in SMALL})
        k = _layer_consts(lw)
        x, s, gathered = _layer_fwd(x, lw, k, tab, _weight_shards(w, l + 1) if l + 1 < depth else ())
        layer_w.append(lw)
        consts.append(k)
        saved.append(s)
    g, sq = loss_head(x, target)
    reduced = [None] * depth
    pending = ()
    for l in reversed(range(depth)):
        g, grads, landed = _layer_bwd(g, saved[l], layer_w[l], consts[l], tab, pending)
        if pending:
            reduced[l + 1] = _sum_landed(landed)
        pending = _grad_parts(grads)
    reduced[0] = _sum_landed(chip_exchange(pending, True, "scatter_grads"))
    return sq, g, reduced


def kernel(x, w_in, pre_g, post_g, sgu_ln_g, sgu_ln_b, sgu_w, sgu_b, mla_q_norm_g, mla_kv_norm_g, mla_w_uq, mla_w_ukv, ca_rel_bias, w_branch, gate_b, w_out, loss_target, m_w_in, m_pre_g, m_post_g, m_sgu_ln_g, m_sgu_ln_b, m_sgu_w, m_sgu_b, m_mla_q_norm_g, m_mla_kv_norm_g, m_mla_w_uq, m_mla_w_ukv, m_ca_rel_bias, m_w_branch, m_gate_b, m_w_out, v_w_in, v_pre_g, v_post_g, v_sgu_ln_g, v_sgu_ln_b, v_sgu_w, v_sgu_b, v_mla_q_norm_g, v_mla_kv_norm_g, v_mla_w_uq, v_mla_w_ukv, v_ca_rel_bias, v_w_branch, v_gate_b, v_w_out):
    w = dict(w_in=w_in, pre_g=pre_g, post_g=post_g, sgu_ln_g=sgu_ln_g, sgu_ln_b=sgu_ln_b, sgu_w=sgu_w, sgu_b=sgu_b,
             mla_q_norm_g=mla_q_norm_g, mla_kv_norm_g=mla_kv_norm_g, mla_w_uq=mla_w_uq, mla_w_ukv=mla_w_ukv,
             ca_rel_bias=ca_rel_bias, w_branch=w_branch, gate_b=gate_b, w_out=w_out)
    m = dict(w_in=m_w_in, pre_g=m_pre_g, post_g=m_post_g, sgu_ln_g=m_sgu_ln_g, sgu_ln_b=m_sgu_ln_b, sgu_w=m_sgu_w,
             sgu_b=m_sgu_b, mla_q_norm_g=m_mla_q_norm_g, mla_kv_norm_g=m_mla_kv_norm_g, mla_w_uq=m_mla_w_uq,
             mla_w_ukv=m_mla_w_ukv, ca_rel_bias=m_ca_rel_bias, w_branch=m_w_branch, gate_b=m_gate_b, w_out=m_w_out)
    v = dict(w_in=v_w_in, pre_g=v_pre_g, post_g=v_post_g, sgu_ln_g=v_sgu_ln_g, sgu_ln_b=v_sgu_ln_b, sgu_w=v_sgu_w,
             sgu_b=v_sgu_b, mla_q_norm_g=v_mla_q_norm_g, mla_kv_norm_g=v_mla_kv_norm_g, mla_w_uq=v_mla_w_uq,
             mla_w_ukv=v_mla_w_ukv, ca_rel_bias=v_ca_rel_bias, w_branch=v_w_branch, gate_b=v_gate_b, w_out=v_w_out)
    depth = w_in.shape[0]
    sq, grad_x, reduced = train_step_local(x[0], loss_target[0], w)
    loss = lax.psum(0.5 * jnp.sum(sq) / D_MODEL, ("x", "y", "c"))

    out = {}
    for a, n in enumerate(SHARDED):
        mine = jnp.stack([reduced[l][0][a] for l in range(depth)])
        theirs = jnp.stack([reduced[l][1][a] for l in range(depth)])
        out[n] = adamw(w[n], [mine, theirs], m[n], v[n])
    small = jnp.stack([jnp.stack([reduced[l][0][-1] for l in range(depth)]),
                       jnp.stack([reduced[l][1][-1] for l in range(depth)])])
    quarter = add_lead(small)
    full = chip_exchange([quarter], False, "gather_small")[0]
    full = jnp.moveaxis(full, 0, 1).reshape(depth, -1)
    off = 0
    for n in SMALL:
        size = int(np.prod(w[n].shape[1:]))
        out[n] = adamw(w[n], [full[:, off:off + size].reshape(w[n].shape)], m[n], v[n])
        off += size
    return (loss, grad_x[None], *[out[n][0] for n in WEIGHTS], *[out[n][1] for n in WEIGHTS],
            *[out[n][2] for n in WEIGHTS], *[out[n][3] for n in WEIGHTS])
```

```python
import numpy as np
import jax
import jax.numpy as jnp
from jax import lax
from jax.experimental import pallas as pl
from jax.experimental.pallas import tpu as pltpu

F32 = jnp.float32
BF16 = jnp.bfloat16
MESH = pl.DeviceIdType.MESH

EPS = 1e-6
NEG_INF = -1e30
D_MODEL = 1024
BR_WIDTH = 512
N_BRANCH = 3
N_HEADS = 8
HEAD_PAD = 128
CHUNK_SHIFT = 6
SGU_BLOCK = 128
MLA_NOPE, MLA_ROPE, MLA_V = 64, 32, 64
MLA_QK = MLA_NOPE + MLA_ROPE
MLA_Q_RANK, MLA_KV_RANK = 256, 128
CA_HEAD_DIM = 64
REL_CLIP = 128
ROPE_BASE = 10000.0
D_IN = 7584

ADAM_LR, ADAM_B1, ADAM_B2, ADAM_EPS, ADAM_WD, ADAM_STEP = 0.001, 0.9, 0.999, 1e-08, 0.01, 10

P_QC, P_KC, P_VC, P_QD, P_KVD, P_KR, P_ZB, P_ZC, P_G, P_U, P_V, P_ZA, P_W = (
    0, 512, 1024, 1536, 1792, 1920, 2048, 2560, 3072, 6144, 6656, 7168, 7680)
NAT_SEGS = [(0, 1536, P_U), (1536, 1920, P_QD), (1920, 1952, P_KR + MLA_NOPE), (1952, 2464, P_ZB),
            (2464, 4000, P_QC), (4000, 4512, P_ZC), (4512, 7584, P_G)]
SHARD_COLS = D_IN // 4

VMEM_LIMIT = 48 * 1024 * 1024
ATT_T = 512
MLA_HEADS_PER_STEP = 2
MLA_FWD_HEADS_PER_STEP = 4
ROW_TILE = 256
MM_TM = 512
MM_TN = 1536
LOG2E = 1.4426950408889634
MLA_SCALE = MLA_QK ** -0.5
CA_SCALE = CA_HEAD_DIM ** -0.5

WEIGHTS = ['w_in', 'pre_g', 'post_g', 'sgu_ln_g', 'sgu_ln_b', 'sgu_w', 'sgu_b', 'mla_q_norm_g',
           'mla_kv_norm_g', 'mla_w_uq', 'mla_w_ukv', 'ca_rel_bias', 'w_branch', 'gate_b', 'w_out']
SHARDED = ['w_in', 'mla_w_uq', 'mla_w_ukv', 'w_branch', 'gate_b', 'w_out']
SMALL = ['pre_g', 'post_g', 'sgu_ln_g', 'sgu_ln_b', 'sgu_w', 'sgu_b', 'mla_q_norm_g',
         'mla_kv_norm_g', 'ca_rel_bias']
SHARD_AXIS = {'w_in': 1, 'mla_w_uq': 1, 'mla_w_ukv': 1, 'w_branch': 2, 'gate_b': 1, 'w_out': 0}


def _call(body, **kw):
    return pl.pallas_call(body, **kw)


def _params(n_axes):
    return pltpu.CompilerParams(dimension_semantics=("arbitrary",) * n_axes,
                                vmem_limit_bytes=VMEM_LIMIT)


def _nt(a, b):
    return lax.dot_general(a, b, (((1,), (1,)), ((), ())), preferred_element_type=F32)


def _nn(a, b):
    return jnp.dot(a, b, preferred_element_type=F32)


def _tn(a, b):
    return lax.dot_general(a, b, (((0,), (0,)), ((), ())), preferred_element_type=F32)


def _rms(xv, g):
    r = lax.rsqrt(jnp.mean(xv * xv, axis=-1, keepdims=True) + EPS)
    return xv * r * g, r


def _rms_bwd(xv, g, r, dy):
    gy = dy * g
    dx = r * gy - xv * (r * r * r) * jnp.mean(xv * gy, axis=-1, keepdims=True)
    dg = jnp.sum(dy * (xv * r), axis=0, keepdims=True)
    return dx, dg


def _sigmoid(z):
    return 1.0 / (1.0 + jnp.exp(-z))


def _rope(xv, c, a, b):
    return xv * c + pltpu.roll(xv, 112, 1) * a + pltpu.roll(xv, 16, 1) * b


def _accumulate(ref, val, first):
    @pl.when(first)
    def _():
        ref[...] = val

    @pl.when(jnp.logical_not(first))
    def _():
        ref[...] += val


def norm_matmul(x, g, w):
    S, D = x.shape
    N = w.shape[1]
    tm, tn = min(S, 2 * MM_TM), MM_TN

    def body(x_ref, g_ref, w_ref, o_ref, xn_ref):
        @pl.when(pl.program_id(1) == 0)
        def _():
            y, _ = _rms(x_ref[...], g_ref[...])
            xn_ref[...] = y.astype(BF16)

        o_ref[...] = _nn(xn_ref[...], w_ref[...])

    return _call(
        body, name="norm_matmul", grid=(S // tm, N // tn),
        in_specs=[pl.BlockSpec((tm, D), lambda i, j: (i, 0)),
                  pl.BlockSpec((1, D), lambda i, j: (0, 0)),
                  pl.BlockSpec((D, tn), lambda i, j: (0, j))],
        out_specs=[pl.BlockSpec((tm, tn), lambda i, j: (i, j)),
                   pl.BlockSpec((tm, D), lambda i, j: (i, 0))],
        out_shape=[jax.ShapeDtypeStruct((S, N), F32), jax.ShapeDtypeStruct((S, D), BF16)],
        compiler_params=_params(2))(x, g, w)


def proj_bwd_x(dproj, w, x, g, resid, exchange=()):
    S, N = dproj.shape
    D = x.shape[1]
    tm, tk = min(S, MM_TM), MM_TN
    nk, n_ex = N // tk, len(exchange)

    def body(dp_ref, w_ref, x_ref, g_ref, r_ref, *rest):
        ex_src, (dx_ref, dg_ref), ex_out, (acc_ref,), ex_sems = _split_refs(
            rest, (n_ex, 2, n_ex, 1, 3 if n_ex else 0))
        i, k = pl.program_id(0), pl.program_id(1)

        if n_ex:
            @pl.when(jnp.logical_and(i == 0, k == 0))
            def _():
                _exchange_issue(ex_src, ex_out, ex_sems, True, True)

        @pl.when(k == 0)
        def _():
            acc_ref[...] = jnp.zeros_like(acc_ref)

        acc_ref[...] += _nt(dp_ref[...].astype(BF16), w_ref[...])

        @pl.when(k == nk - 1)
        def _():
            xv = x_ref[...]
            _, r = _rms(xv, g_ref[...])
            dx, dg = _rms_bwd(xv, g_ref[...], r, acc_ref[...])
            dx_ref[...] = dx + r_ref[...]
            _accumulate(dg_ref, dg, i == 0)

        if n_ex:
            @pl.when(jnp.logical_and(i == S // tm - 1, k == nk - 1))
            def _():
                _exchange_issue(ex_src, ex_out, ex_sems, True, False)

    outs = _call(
        body, name="proj_bwd_x_scatter" if n_ex else "proj_bwd_x", grid=(S // tm, nk),
        in_specs=[pl.BlockSpec((tm, tk), lambda i, k: (i, k)),
                  pl.BlockSpec((D, tk), lambda i, k: (0, k)),
                  pl.BlockSpec((tm, D), lambda i, k: (i, 0)),
                  pl.BlockSpec((1, D), lambda i, k: (0, 0)),
                  pl.BlockSpec((tm, D), lambda i, k: (i, 0))] + [ANY] * n_ex,
        out_specs=[pl.BlockSpec((tm, D), lambda i, k: (i, 0)),
                   pl.BlockSpec((1, D), lambda i, k: (0, 0))] + [ANY] * n_ex,
        out_shape=[jax.ShapeDtypeStruct((S, D), F32), jax.ShapeDtypeStruct((1, D), F32)] +
        _exchange_out_shape(exchange, True),
        scratch_shapes=[pltpu.VMEM((tm, D), F32)] + _exchange_sems(n_ex),
        compiler_params=_params(2))(dproj, w, x, g, resid, *exchange)
    return outs[0], outs[1], outs[2:]


def matmul_tn(a, b, tn):
    S, M = a.shape
    N = b.shape[1]
    tk = min(S, 2 * MM_TM)

    def body(a_ref, b_ref, o_ref):
        @pl.when(pl.program_id(1) == 0)
        def _():
            o_ref[...] = jnp.zeros_like(o_ref)

        o_ref[...] += _tn(a_ref[...].astype(BF16), b_ref[...].astype(BF16))

    return _call(
        body, name="matmul_tn", grid=(N // tn, S // tk),
        in_specs=[pl.BlockSpec((tk, M), lambda j, k: (k, 0)),
                  pl.BlockSpec((tk, tn), lambda j, k: (k, j))],
        out_specs=pl.BlockSpec((M, tn), lambda j, k: (0, j)),
        out_shape=jax.ShapeDtypeStruct((M, N), F32),
        compiler_params=_params(2))(a, b)


def _sgu_block(vv, g, b, ws_ref, lane):
    mu = jnp.mean(vv, axis=-1, keepdims=True)
    xc = vv - mu
    r = lax.rsqrt(jnp.mean(xc * xc, axis=-1, keepdims=True) + EPS)
    xhat = xc * r
    vln = (xhat * g + b).astype(BF16)
    pieces = []
    for p in range(4):
        vp = vln[:, p * 128:(p + 1) * 128]
        pieces.append(jnp.where(lane < 64, _nn(ws_ref[2 * p], vp), _nn(ws_ref[2 * p + 1], vp)))
    return xhat, r, vln, jnp.concatenate(pieces, axis=1)


def sgu_fwd(proj, ln_g, ln_b, ws, bias_full):
    S = proj.shape[0]
    ts = ROW_TILE

    def body(u_ref, v_ref, z_ref, g_ref, b_ref, ws_ref, bf_ref, y_ref):
        lane = lax.broadcasted_iota(jnp.int32, (SGU_BLOCK, 128), 1)
        for blk in range(ts // SGU_BLOCK):
            rows = slice(blk * SGU_BLOCK, (blk + 1) * SGU_BLOCK)
            _, _, _, mixed = _sgu_block(v_ref[rows, :], g_ref[...], b_ref[...], ws_ref, lane)
            mixed = mixed + bf_ref[...]
            zz = z_ref[rows, :]
            y_ref[rows, :] = (u_ref[rows, :] * mixed * (zz * _sigmoid(zz))).astype(BF16)

    col = lambda c: pl.BlockSpec((ts, BR_WIDTH), lambda i: (i, c))
    full = lambda shape: pl.BlockSpec(shape, lambda i: (0,) * len(shape))
    return _call(
        body, name="sgu_fwd", grid=(S // ts,),
        in_specs=[col(P_U // 512), col(P_V // 512), col(P_ZA // 512),
                  full((1, BR_WIDTH)), full((1, BR_WIDTH)), full((8, 128, 128)), full((128, BR_WIDTH))],
        out_specs=pl.BlockSpec((ts, BR_WIDTH), lambda i: (i, 0)),
        out_shape=jax.ShapeDtypeStruct((S, BR_WIDTH), BF16),
        compiler_params=_params(1))(proj, proj, proj, ln_g, ln_b, ws, bias_full)


def sgu_bwd(dproj, dy, proj, ln_g, ln_b, ws, ws_t, bias_full):
    S = proj.shape[0]
    ts = ROW_TILE

    def body(dp_in, dy_ref, u_ref, v_ref, z_ref, g_ref, b_ref, ws_ref, wst_ref, bf_ref,
             dp_ref, gg_ref, gb_ref, gws_ref, gbf_ref):
        del dp_in
        first = pl.program_id(0) == 0

        @pl.when(first)
        def _():
            gg_ref[...] = jnp.zeros_like(gg_ref)
            gb_ref[...] = jnp.zeros_like(gb_ref)
            gws_ref[...] = jnp.zeros_like(gws_ref)
            gbf_ref[...] = jnp.zeros_like(gbf_ref)

        lane = lax.broadcasted_iota(jnp.int32, (SGU_BLOCK, 128), 1)
        for blk in range(ts // SGU_BLOCK):
            rows = slice(blk * SGU_BLOCK, (blk + 1) * SGU_BLOCK)
            g = g_ref[...]
            xhat, r, vln, mixed = _sgu_block(v_ref[rows, :], g, b_ref[...], ws_ref, lane)
            mixed = mixed + bf_ref[...]
            zz = z_ref[rows, :]
            uu = u_ref[rows, :]
            dyv = dy_ref[0, rows, :]
            sg = _sigmoid(zz)
            sil = zz * sg
            dmixed = dyv * uu * sil
            dp_ref[rows, 0:512] = (dyv * mixed * sil).astype(BF16)
            dp_ref[rows, 1024:1536] = (dyv * uu * mixed * (sg * (1.0 + zz * (1.0 - sg)))).astype(BF16)
            gbf_ref[...] += dmixed
            dmb = dmixed.astype(BF16)
            pieces = []
            for p in range(4):
                dmp = dmb[:, p * 128:(p + 1) * 128]
                vp = vln[:, p * 128:(p + 1) * 128]
                pieces.append(jnp.where(lane < 64, _nn(wst_ref[2 * p], dmp), _nn(wst_ref[2 * p + 1], dmp)))
                zero = jnp.zeros_like(dmp)
                gws_ref[2 * p] += _nt(jnp.where(lane < 64, dmp, zero), vp)
                gws_ref[2 * p + 1] += _nt(jnp.where(lane >= 64, dmp, zero), vp)
            dvln = jnp.concatenate(pieces, axis=1)
            dxh = dvln * g
            dp_ref[rows, 512:1024] = (r * (dxh - jnp.mean(dxh, axis=-1, keepdims=True)
                                           - xhat * jnp.mean(dxh * xhat, axis=-1, keepdims=True))).astype(BF16)
            gg_ref[...] += jnp.sum(dvln * xhat, axis=0, keepdims=True)
            gb_ref[...] += jnp.sum(dvln, axis=0, keepdims=True)

    col = lambda c: pl.BlockSpec((ts, BR_WIDTH), lambda i: (i, c))
    full = lambda shape: pl.BlockSpec(shape, lambda i: (0,) * len(shape))
    return _call(
        body, name="sgu_bwd", grid=(S // ts,),
        in_specs=[pl.BlockSpec(memory_space=pl.ANY),
                  pl.BlockSpec((1, ts, BR_WIDTH), lambda i: (0, i, 0)),
                  col(P_U // 512), col(P_V // 512), col(P_ZA // 512),
                  full((1, BR_WIDTH)), full((1, BR_WIDTH)), full((8, 128, 128)), full((8, 128, 128)),
                  full((128, BR_WIDTH))],
        out_specs=[pl.BlockSpec((ts, 1536), lambda i: (i, P_U // 1536)),
                   full((1, BR_WIDTH)), full((1, BR_WIDTH)), full((8, 128, 128)), full((128, BR_WIDTH))],
        out_shape=[jax.ShapeDtypeStruct(dproj.shape, BF16),
                   jax.ShapeDtypeStruct((1, BR_WIDTH), F32), jax.ShapeDtypeStruct((1, BR_WIDTH), F32),
                   jax.ShapeDtypeStruct((8, 128, 128), F32), jax.ShapeDtypeStruct((128, BR_WIDTH), F32)],
        input_output_aliases={0: 0},
        compiler_params=_params(1))(dproj, dy, proj, proj, proj, ln_g, ln_b, ws, ws_t, bias_full)


def _hspec(ts):
    return pl.BlockSpec((N_HEADS, ts, HEAD_PAD), lambda i: (0, i, 0))


def prep_fwd(proj, tab, qg, kvg, wq, wk, wv):
    S = proj.shape[0]
    ts = ROW_TILE

    def body(qc_ref, kc_ref, vc_ref, qd_ref, kvd_ref, kr_ref, tab_ref, qg_ref, kvg_ref,
             wq_ref, wk_ref, wv_ref, qb, kb, vb, qc, kc, vc, kbt, vbt, kct, vct, cq_o, ckv_o):
        c, a, b = tab_ref[0], tab_ref[1], tab_ref[2]
        cq, _ = _rms(qd_ref[...], qg_ref[...])
        ckv, _ = _rms(kvd_ref[...], kvg_ref[...])
        cqb, ckvb = cq.astype(BF16), ckv.astype(BF16)
        cq_o[...] = cqb
        ckv_o[...] = ckvb
        krr = _rope(kr_ref[...], c, a, b)
        lane = lax.broadcasted_iota(jnp.int32, (ts, 128), 1)
        ones_lane = jnp.where(lane == MLA_V, 1.0, 0.0)
        for h in range(N_HEADS):
            cols = slice(h * HEAD_PAD, (h + 1) * HEAD_PAD)
            qb[h] = (_rope(_nn(cqb, wq_ref[:, cols]), c, a, b) * (MLA_SCALE * LOG2E)).astype(BF16)
            kh = _nn(ckvb, wk_ref[:, cols]) + krr
            vh = _nn(ckvb, wv_ref[:, cols]) + ones_lane
            kb[h], kbt[h] = kh.astype(BF16), kh.T.astype(BF16)
            vb[h], vbt[h] = vh.astype(BF16), vh.T.astype(BF16)
        for p in range(4):
            piece = qc_ref[:, p * 128:(p + 1) * 128] * (CA_SCALE * LOG2E)
            qc[2 * p] = jnp.where(lane < 64, piece, 0.0).astype(BF16)
            qc[2 * p + 1] = jnp.where(lane < 64, pltpu.roll(piece, 64, 1), 0.0).astype(BF16)
            for src, dst, dst_t, pad in ((kc_ref, kc, kct, 0.0), (vc_ref, vc, vct, ones_lane)):
                piece = src[:, p * 128:(p + 1) * 128]
                for h, head in ((2 * p, jnp.where(lane < 64, piece, pad)),
                                (2 * p + 1, jnp.where(lane < 64, pltpu.roll(piece, 64, 1), pad))):
                    dst[h], dst_t[h] = head.astype(BF16), head.T.astype(BF16)

    col = lambda w, c: pl.BlockSpec((ts, w), lambda i: (i, c))
    full = lambda shape: pl.BlockSpec(shape, lambda i: (0,) * len(shape))
    hshape = jax.ShapeDtypeStruct((N_HEADS, S, HEAD_PAD), BF16)
    tshape = jax.ShapeDtypeStruct((N_HEADS, HEAD_PAD, S), BF16)
    tspec = pl.BlockSpec((N_HEADS, HEAD_PAD, ts), lambda i: (0, 0, i))
    return _call(
        body, name="prep_fwd", grid=(S // ts,),
        in_specs=[col(512, P_QC // 512), col(512, P_KC // 512), col(512, P_VC // 512),
                  col(256, P_QD // 256), col(128, P_KVD // 128), col(128, P_KR // 128),
                  pl.BlockSpec((3, ts, 128), lambda i: (0, i, 0)),
                  full((1, MLA_Q_RANK)), full((1, MLA_KV_RANK)),
                  full((MLA_Q_RANK, 1024)), full((MLA_KV_RANK, 1024)), full((MLA_KV_RANK, 1024))],
        out_specs=[_hspec(ts)] * 6 + [tspec] * 4 + [pl.BlockSpec((ts, MLA_Q_RANK), lambda i: (i, 0)),
                                                    pl.BlockSpec((ts, MLA_KV_RANK), lambda i: (i, 0))],
        out_shape=[hshape] * 6 + [tshape] * 4 + [jax.ShapeDtypeStruct((S, MLA_Q_RANK), BF16),
                                                 jax.ShapeDtypeStruct((S, MLA_KV_RANK), BF16)],
        compiler_params=_params(1))(proj, proj, proj, proj, proj, proj, tab, qg, kvg, wq, wk, wv)


def prep_bwd(dproj, dqb, dkb, dvb, dqc, dkc, dvc, proj, tab, qg, kvg, wq, wk, wv):
    S = proj.shape[0]
    ts = ROW_TILE

    def body(dp_in, dqb_r, dkb_r, dvb_r, dqc_r, dkc_r, dvc_r, qd_ref, kvd_ref, tab_ref, qg_ref, kvg_ref,
             wq_ref, wk_ref, wv_ref, dp_ref, dqf, dkf, dvf, gq_ref, gkv_ref):
        del dp_in
        c, a, b = tab_ref[0], -tab_ref[1], -tab_ref[2]
        qd, kvd = qd_ref[...], kvd_ref[...]
        _, rq = _rms(qd, qg_ref[...])
        _, rkv = _rms(kvd, kvg_ref[...])
        dcq = jnp.zeros((ts, MLA_Q_RANK), F32)
        dckv = jnp.zeros((ts, MLA_KV_RANK), F32)
        dksum = jnp.zeros((ts, HEAD_PAD), F32)
        for h in range(N_HEADS):
            cols = slice(h * HEAD_PAD, (h + 1) * HEAD_PAD)
            dqh = _rope(dqb_r[h].astype(F32) * MLA_SCALE, c, a, b).astype(BF16)
            dqf[:, cols] = dqh
            dcq = dcq + _nt(dqh, wq_ref[:, cols])
            dk = dkb_r[h].astype(F32) * (1.0 / LOG2E)
            dksum = dksum + dk
            dkh = dk.astype(BF16)
            dkf[:, cols] = dkh
            dvh = dvb_r[h].astype(BF16)
            dvf[:, cols] = dvh
            dckv = dckv + _nt(dkh, wk_ref[:, cols]) + _nt(dvh, wv_ref[:, cols])
        lane = lax.broadcasted_iota(jnp.int32, (ts, 128), 1)
        rope_lanes = jnp.logical_and(lane >= MLA_NOPE, lane < MLA_QK)
        dp_ref[:, P_KR:P_KR + 128] = jnp.where(rope_lanes, _rope(dksum, c, a, b), 0.0).astype(BF16)
        dqd, gq = _rms_bwd(qd, qg_ref[...], rq, dcq)
        dkvd, gkv = _rms_bwd(kvd, kvg_ref[...], rkv, dckv)
        dp_ref[:, P_QD:P_QD + 256] = dqd.astype(BF16)
        dp_ref[:, P_KVD:P_KVD + 128] = dkvd.astype(BF16)
        first = pl.program_id(0) == 0
        _accumulate(gq_ref, gq, first)
        _accumulate(gkv_ref, gkv, first)
        for src, base, factor in ((dqc_r, P_QC, CA_SCALE), (dkc_r, P_KC, 1.0 / LOG2E), (dvc_r, P_VC, 1.0)):
            for p in range(4):
                dp_ref[:, base + p * 128:base + (p + 1) * 128] = (
                    (src[2 * p].astype(F32) + pltpu.roll(src[2 * p + 1].astype(F32), 64, 1)) * factor).astype(BF16)

    col = lambda w, c: pl.BlockSpec((ts, w), lambda i: (i, c))
    full = lambda shape: pl.BlockSpec(shape, lambda i: (0,) * len(shape))
    wide = jax.ShapeDtypeStruct((S, 1024), BF16)
    return _call(
        body, name="prep_bwd", grid=(S // ts,),
        in_specs=[pl.BlockSpec(memory_space=pl.ANY)] + [_hspec(ts)] * 6 +
                 [col(256, P_QD // 256), col(128, P_KVD // 128),
                  pl.BlockSpec((3, ts, 128), lambda i: (0, i, 0)),
                  full((1, MLA_Q_RANK)), full((1, MLA_KV_RANK)),
                  full((MLA_Q_RANK, 1024)), full((MLA_KV_RANK, 1024)), full((MLA_KV_RANK, 1024))],
        out_specs=[pl.BlockSpec((ts, 2048), lambda i: (i, 0))] + [pl.BlockSpec((ts, 1024), lambda i: (i, 0))] * 3 +
                  [full((1, MLA_Q_RANK)), full((1, MLA_KV_RANK))],
        out_shape=[jax.ShapeDtypeStruct(dproj.shape, BF16), wide, wide, wide,
                   jax.ShapeDtypeStruct((1, MLA_Q_RANK), F32), jax.ShapeDtypeStruct((1, MLA_KV_RANK), F32)],
        input_output_aliases={0: 0},
        compiler_params=_params(1))(dproj, dqb, dkb, dvb, dqc, dkc, dvc, proj, proj, tab, qg, kvg, wq, wk, wv)


def _diag_visible(t):
    r = lax.broadcasted_iota(jnp.int32, (t, t), 0) >> CHUNK_SHIFT
    c = lax.broadcasted_iota(jnp.int32, (t, t), 1) >> CHUNK_SHIFT
    return r <= c


def _pair_tables(nq, kv_major):
    if kv_major:
        pairs = [(kb, qi) for kb in range(nq) for qi in range(kb, nq)]
    else:
        pairs = [(kb, qi) for qi in range(nq) for kb in range(qi + 1)]
    return (jnp.asarray(np.array([p[0] for p in pairs], np.int32)),
            jnp.asarray(np.array([p[1] for p in pairs], np.int32)), len(pairs))


def _finish_softmax(acc, m):
    l = acc[MLA_V:MLA_V + 1, :]
    row = lax.broadcasted_iota(jnp.int32, acc.shape, 0)
    return jnp.where(row < MLA_V, acc / l, 0.0).T.astype(BF16), m + jnp.log2(l)


def _split_refs(refs, counts):
    out, pos = [], 0
    for c in counts:
        out.append(refs[pos:pos + c])
        pos += c
    return out


def mla_fwd(q, k, vt, exchange=()):
    H, S, _ = q.shape
    t, hb, n_ex = ATT_T, MLA_FWD_HEADS_PER_STEP, len(exchange)
    kb_tab, qi_tab, n_pairs = _pair_tables(S // t, False)

    def body(kb_ref, qi_ref, q_ref, k_ref, vt_ref, *rest):
        ex_src, (o_ref, lse_ref), ex_out, (m_s, acc_s), ex_sems = _split_refs(rest, (n_ex, 2, n_ex, 2, 3 if n_ex else 0))
        hg, p_id = pl.program_id(0), pl.program_id(1)
        kb, qi = kb_ref[p_id], qi_ref[p_id]

        if n_ex:
            @pl.when(jnp.logical_and(hg == 0, p_id == 0))
            def _():
                _exchange_issue(ex_src, ex_out, ex_sems, False, True)

        @pl.when(kb == 0)
        def _():
            m_s[...] = jnp.full_like(m_s, NEG_INF)
            acc_s[...] = jnp.zeros_like(acc_s)

        def step(masked):
            for h in range(hb):
                st = _nt(k_ref[h], q_ref[h])
                if masked:
                    st = jnp.where(_diag_visible(t), st, NEG_INF)
                m_prev = m_s[h]
                m_new = jnp.maximum(m_prev, jnp.max(st, axis=0, keepdims=True))
                p = jnp.exp2(st - m_new)
                acc_s[h] = jnp.exp2(m_prev - m_new) * acc_s[h] + _nn(vt_ref[h], p.astype(BF16))
                m_s[h] = m_new

        @pl.when(kb < qi)
        def _():
            step(False)

        @pl.when(kb == qi)
        def _():
            step(True)
            for h in range(hb):
                o_ref[h], lse_ref[h] = _finish_softmax(acc_s[h], m_s[h])

        if n_ex:
            @pl.when(jnp.logical_and(hg == H // hb - 1, p_id == n_pairs - 1))
            def _():
                _exchange_issue(ex_src, ex_out, ex_sems, False, False)

    grid_spec = pltpu.PrefetchScalarGridSpec(
        num_scalar_prefetch=2, grid=(H // hb, n_pairs),
        in_specs=[pl.BlockSpec((hb, t, HEAD_PAD), lambda h, p, kb, qi: (h, qi[p], 0)),
                  pl.BlockSpec((hb, t, HEAD_PAD), lambda h, p, kb, qi: (h, kb[p], 0)),
                  pl.BlockSpec((hb, HEAD_PAD, t), lambda h, p, kb, qi: (h, 0, kb[p]))] + [ANY] * n_ex,
        out_specs=[pl.BlockSpec((hb, t, HEAD_PAD), lambda h, p, kb, qi: (h, qi[p], 0)),
                   pl.BlockSpec((hb, 1, t), lambda h, p, kb, qi: (h, 0, qi[p]))] + [ANY] * n_ex,
        scratch_shapes=[pltpu.VMEM((hb, 1, t), F32), pltpu.VMEM((hb, HEAD_PAD, t), F32)] + _exchange_sems(n_ex))
    outs = _call(
        body, name="mla_fwd_gather" if n_ex else "mla_fwd", grid_spec=grid_spec,
        out_shape=[jax.ShapeDtypeStruct((H, S, HEAD_PAD), BF16), jax.ShapeDtypeStruct((H, 1, S), F32)] +
        _exchange_out_shape(exchange, False),
        compiler_params=_params(2))(kb_tab, qi_tab, q, k, vt, *exchange)
    return outs[0], outs[1], outs[2:]


def mla_bwd(q, k, kt, v, do, lse, delta, exchange=()):
    H, S, _ = q.shape
    t, hb, n_ex = ATT_T, MLA_HEADS_PER_STEP, len(exchange)
    nq = S // t
    kb_tab, qi_tab, n_pairs = _pair_tables(nq, True)

    def body(kb_ref, qi_ref, q_ref, k_ref, kt_ref, v_ref, do_ref, lse_ref, dl_ref, *rest):
        ex_src, (dq_ref, dk_ref, dv_ref), ex_out, (dqt_s, dk_s, dv_s), ex_sems = _split_refs(
            rest, (n_ex, 3, n_ex, 3, 3 if n_ex else 0))
        hg, p_id = pl.program_id(0), pl.program_id(1)
        kb, qi = kb_ref[p_id], qi_ref[p_id]

        if n_ex:
            @pl.when(jnp.logical_and(hg == 0, p_id == 0))
            def _():
                _exchange_issue(ex_src, ex_out, ex_sems, True, True)

        @pl.when(p_id == 0)
        def _():
            dqt_s[...] = jnp.zeros_like(dqt_s)

        @pl.when(qi == kb)
        def _():
            dk_s[...] = jnp.zeros_like(dk_s)
            dv_s[...] = jnp.zeros_like(dv_s)

        def step(masked):
            for h in range(hb):
                st = _nt(k_ref[h], q_ref[h])
                if masked:
                    st = jnp.where(_diag_visible(t), st, NEG_INF)
                pt = jnp.exp2(st - lse_ref[h])
                dv_s[h] += _nn(pt.astype(BF16), do_ref[h])
                dsb = (pt * (_nt(v_ref[h], do_ref[h]) - dl_ref[h])).astype(BF16)
                dk_s[h] += _nn(dsb, q_ref[h])
                dqt_s[h, qi] += _nn(kt_ref[h], dsb)

        @pl.when(qi == kb)
        def _():
            step(True)
            rows = pl.ds(pl.multiple_of(qi * t, t), t)
            for h in range(hb):
                dq_ref[h, rows, :] = dqt_s[h, qi].T.astype(BF16)

        @pl.when(qi > kb)
        def _():
            step(False)

        @pl.when(qi == nq - 1)
        def _():
            dk_ref[...] = dk_s[...].astype(BF16)
            dv_ref[...] = dv_s[...].astype(BF16)

        if n_ex:
            @pl.when(jnp.logical_and(hg == H // hb - 1, p_id == n_pairs - 1))
            def _():
                _exchange_issue(ex_src, ex_out, ex_sems, True, False)

    qtile = pl.BlockSpec((hb, t, HEAD_PAD), lambda h, p, kb, qi: (h, qi[p], 0))
    ktile = pl.BlockSpec((hb, t, HEAD_PAD), lambda h, p, kb, qi: (h, kb[p], 0))
    stat = pl.BlockSpec((hb, 1, t), lambda h, p, kb, qi: (h, 0, qi[p]))
    grid_spec = pltpu.PrefetchScalarGridSpec(
        num_scalar_prefetch=2, grid=(H // hb, n_pairs),
        in_specs=[qtile, ktile, pl.BlockSpec((hb, HEAD_PAD, t), lambda h, p, kb, qi: (h, 0, kb[p])), ktile, qtile,
                  stat, stat] + [ANY] * n_ex,
        out_specs=[pl.BlockSpec((hb, S, HEAD_PAD), lambda h, p, kb, qi: (h, 0, 0)), ktile, ktile] + [ANY] * n_ex,
        scratch_shapes=[pltpu.VMEM((hb, nq, HEAD_PAD, t), F32), pltpu.VMEM((hb, t, HEAD_PAD), F32),
                        pltpu.VMEM((hb, t, HEAD_PAD), F32)] + _exchange_sems(n_ex))
    outs = _call(
        body, name="mla_bwd_scatter" if n_ex else "mla_bwd", grid_spec=grid_spec,
        out_shape=[jax.ShapeDtypeStruct((H, S, HEAD_PAD), BF16)] * 3 + _exchange_out_shape(exchange, True),
        compiler_params=_params(2))(kb_tab, qi_tab, q, k, kt, v, do, lse, delta, *exchange)
    return outs[0], outs[1], outs[2], outs[3:]


def _band_specs(t, hb):
    prev = lambda i: jnp.maximum(i - 1, 0)
    return dict(
        cur=pl.BlockSpec((hb, t, HEAD_PAD), lambda h, i: (h, i, 0)),
        prev=pl.BlockSpec((hb, t, HEAD_PAD), lambda h, i: (h, prev(i), 0)),
        cur_t=pl.BlockSpec((hb, HEAD_PAD, t), lambda h, i: (h, 0, i)),
        prev_t=pl.BlockSpec((hb, HEAD_PAD, t), lambda h, i: (h, 0, prev(i))),
        stat=pl.BlockSpec((hb, 1, t), lambda h, i: (h, 0, i)),
        bias_prev=pl.BlockSpec((hb, 1, t, t), lambda h, i: (h, jnp.where(i == 0, 1, 0), 0, 0)),
        bias_cur=pl.BlockSpec((hb, 1, t, t), lambda h, i: (h, 2, 0, 0)))


def band_fwd(q, k, vt, bias):
    H, S, _ = q.shape
    t, hb = ATT_T, MLA_HEADS_PER_STEP
    sp = _band_specs(t, hb)

    def body(q_ref, kp_ref, kc_ref, vtp_ref, vtc_ref, bp_ref, bc_ref, o_ref, lse_ref):
        for h in range(hb):
            s0 = _nt(kp_ref[h], q_ref[h]) + bp_ref[h, 0]
            s1 = _nt(kc_ref[h], q_ref[h]) + bc_ref[h, 0]
            m = jnp.maximum(jnp.max(s0, axis=0, keepdims=True), jnp.max(s1, axis=0, keepdims=True))
            ot = (_nn(vtp_ref[h], jnp.exp2(s0 - m).astype(BF16)) +
                  _nn(vtc_ref[h], jnp.exp2(s1 - m).astype(BF16)))
            o_ref[h], lse_ref[h] = _finish_softmax(ot, m)

    return _call(
        body, name="band_fwd", grid=(H // hb, S // t),
        in_specs=[sp['cur'], sp['prev'], sp['cur'], sp['prev_t'], sp['cur_t'], sp['bias_prev'], sp['bias_cur']],
        out_specs=[sp['cur'], sp['stat']],
        out_shape=[jax.ShapeDtypeStruct((H, S, HEAD_PAD), BF16), jax.ShapeDtypeStruct((H, 1, S), F32)],
        compiler_params=_params(2))(q, k, k, vt, vt, bias, bias)


def band_bwd(q, k, kt, v, do, lse, delta, bias):
    H, S, _ = q.shape
    t = ATT_T
    sp = _band_specs(t, 1)

    def body(q_ref, kp_ref, kc_ref, ktp_ref, ktc_ref, vp_ref, vc_ref, do_ref, lse_ref, dl_ref, bp_ref, bc_ref,
             dq_ref, dk_ref, dv_ref, db_ref):
        i = pl.program_id(1)

        @pl.when(i == 0)
        def _():
            dk_ref[...] = jnp.zeros_like(dk_ref)
            dv_ref[...] = jnp.zeros_like(dv_ref)
            db_ref[...] = jnp.zeros_like(db_ref)

        qv, dov = q_ref[0], do_ref[0]
        dqt = jnp.zeros((HEAD_PAD, t), F32)
        windows = ((0, jnp.maximum(i - 1, 0), kp_ref, ktp_ref, vp_ref, bp_ref),
                   (1, i, kc_ref, ktc_ref, vc_ref, bc_ref))
        for w, blk, k_ref, kt_ref, v_ref, b_ref in windows:
            rows = pl.ds(pl.multiple_of(blk * t, t), t)
            pt = jnp.exp2(_nt(k_ref[0], qv) + b_ref[0, 0] - lse_ref[0])
            dv_ref[0, rows, :] += _nn(pt.astype(BF16), dov)
            ds = pt * (_nt(v_ref[0], dov) - dl_ref[0])
            db_ref[0, w] += ds
            dsb = ds.astype(BF16)
            dk_ref[0, rows, :] += _nn(dsb, qv)
            dqt = dqt + _nn(kt_ref[0], dsb)
        dq_ref[0] = dqt.T.astype(BF16)

    whole = pl.BlockSpec((1, S, HEAD_PAD), lambda h, i: (h, 0, 0))
    return _call(
        body, name="band_bwd", grid=(H, S // t),
        in_specs=[sp['cur'], sp['prev'], sp['cur'], sp['prev_t'], sp['cur_t'], sp['prev'], sp['cur'], sp['cur'],
                  sp['stat'], sp['stat'], sp['bias_prev'], sp['bias_cur']],
        out_specs=[sp['cur'], whole, whole, pl.BlockSpec((1, 2, t, t), lambda h, i: (h, 0, 0, 0))],
        out_shape=[jax.ShapeDtypeStruct((H, S, HEAD_PAD), BF16), jax.ShapeDtypeStruct((H, S, HEAD_PAD), F32),
                   jax.ShapeDtypeStruct((H, S, HEAD_PAD), F32), jax.ShapeDtypeStruct((H, 2, t, t), F32)],
        compiler_params=_params(2))(q, k, k, kt, kt, v, v, do, lse, delta, bias, bias)


def _compact(o_ref):
    return jnp.concatenate([o_ref[2 * p].astype(F32) + pltpu.roll(o_ref[2 * p + 1].astype(F32), 64, 1)
                            for p in range(4)], axis=1)


def merge_fwd(ob, oc, proj, ya, gate_b, wbr, w_out, x, post_g):
    S = x.shape[0]
    ts = ROW_TILE

    def body(ob_ref, oc_ref, zb_ref, zc_ref, ya_ref, gl_ref, gb_ref, wbr_ref, wo_ref, x_ref, pg_ref,
             xo_ref, yb_ref, yc_ref, mg_ref, out_ref):
        zb, zc = zb_ref[...], zc_ref[...]
        yb = (_compact(ob_ref) * (zb * _sigmoid(zb))).astype(BF16)
        yc = (_compact(oc_ref) * (zc * _sigmoid(zc))).astype(BF16)
        yb_ref[...] = yb
        yc_ref[...] = yc
        merged = jnp.zeros((ts, D_MODEL), F32)
        for n, y in enumerate((ya_ref[...], yb, yc)):
            cols = slice(n * D_MODEL, (n + 1) * D_MODEL)
            gate = _sigmoid(gl_ref[:, cols] + gb_ref[:, cols])
            merged = merged + gate * _nn(y, wbr_ref[n])
        mb = merged.astype(BF16)
        mg_ref[...] = mb
        out = _nn(mb, wo_ref[...])
        out_ref[...] = out
        normed, _ = _rms(out, pg_ref[...])
        xo_ref[...] = x_ref[...] + normed

    row = lambda w: pl.BlockSpec((ts, w), lambda i: (i, 0))
    col = lambda w, c: pl.BlockSpec((ts, w), lambda i: (i, c))
    full = lambda shape: pl.BlockSpec(shape, lambda i: (0,) * len(shape))
    return _call(
        body, name="merge_fwd", grid=(S // ts,),
        in_specs=[_hspec(ts), _hspec(ts), col(512, P_ZB // 512), col(512, P_ZC // 512), row(512),
                  col(3072, P_G // 3072), full((1, 3072)), full((3, BR_WIDTH, D_MODEL)),
                  full((D_MODEL, D_MODEL)), row(D_MODEL), full((1, D_MODEL))],
        out_specs=[row(D_MODEL), row(512), row(512), row(D_MODEL), row(D_MODEL)],
        out_shape=[jax.ShapeDtypeStruct((S, D_MODEL), F32), jax.ShapeDtypeStruct((S, 512), BF16),
                   jax.ShapeDtypeStruct((S, 512), BF16), jax.ShapeDtypeStruct((S, D_MODEL), BF16),
                   jax.ShapeDtypeStruct((S, D_MODEL), F32)],
        compiler_params=_params(1))(ob, oc, proj, proj, ya, proj, gate_b, wbr, w_out, x, post_g)


def post_bwd(g, out, post_g, w_out):
    S = g.shape[0]
    ts = ROW_TILE

    def body(g_ref, out_ref, pg_ref, wo_ref, do_ref, dm_ref, gp_ref):
        ov = out_ref[...]
        _, r = _rms(ov, pg_ref[...])
        dout, gp = _rms_bwd(ov, pg_ref[...], r, g_ref[...])
        db = dout.astype(BF16)
        do_ref[...] = db
        dm_ref[...] = _nt(db, wo_ref[...])
        _accumulate(gp_ref, gp, pl.program_id(0) == 0)

    row = lambda: pl.BlockSpec((ts, D_MODEL), lambda i: (i, 0))
    full = lambda shape: pl.BlockSpec(shape, lambda i: (0,) * len(shape))
    return _call(
        body, name="post_bwd", grid=(S // ts,),
        in_specs=[row(), row(), full((1, D_MODEL)), full((D_MODEL, D_MODEL))],
        out_specs=[row(), row(), full((1, D_MODEL))],
        out_shape=[jax.ShapeDtypeStruct((S, D_MODEL), BF16), jax.ShapeDtypeStruct((S, D_MODEL), F32),
                   jax.ShapeDtypeStruct((1, D_MODEL), F32)],
        compiler_params=_params(1))(g, out, post_g, w_out)


def gate_bwd(dmerged, proj, gate_b, ya, yb, yc, wbr):
    S = dmerged.shape[0]
    ts = ROW_TILE

    def body(dm_ref, gl_ref, gb_ref, ya_ref, yb_ref, yc_ref, wbr_ref,
             dp_ref, dba_ref, dbb_ref, dbc_ref, dy_ref, ggb_ref):
        dm = dm_ref[...]
        ggb = []
        for n, (y_ref, dbr_ref) in enumerate(((ya_ref, dba_ref), (yb_ref, dbb_ref), (yc_ref, dbc_ref))):
            cols = slice(n * D_MODEL, (n + 1) * D_MODEL)
            br = _nn(y_ref[...], wbr_ref[n])
            sg = _sigmoid(gl_ref[:, cols] + gb_ref[:, cols])
            dgl = dm * br * (sg * (1.0 - sg))
            dp_ref[:, cols] = dgl.astype(BF16)
            ggb.append(jnp.sum(dgl, axis=0, keepdims=True))
            dbr = (dm * sg).astype(BF16)
            dbr_ref[...] = dbr
            dy_ref[n] = _nt(dbr, wbr_ref[n])
        _accumulate(ggb_ref, jnp.concatenate(ggb, axis=1), pl.program_id(0) == 0)

    row = lambda w: pl.BlockSpec((ts, w), lambda i: (i, 0))
    full = lambda shape: pl.BlockSpec(shape, lambda i: (0,) * len(shape))
    wide = jax.ShapeDtypeStruct((S, D_MODEL), BF16)
    return _call(
        body, name="gate_bwd", grid=(S // ts,),
        in_specs=[row(D_MODEL), pl.BlockSpec((ts, 3072), lambda i: (i, P_G // 3072)), full((1, 3072)),
                  row(512), row(512), row(512), full((3, BR_WIDTH, D_MODEL))],
        out_specs=[pl.BlockSpec((ts, 3072), lambda i: (i, P_G // 3072)), row(D_MODEL), row(D_MODEL), row(D_MODEL),
                   pl.BlockSpec((3, ts, 512), lambda i: (0, i, 0)), full((1, 3072))],
        out_shape=[jax.ShapeDtypeStruct((S, P_W), BF16), wide, wide, wide,
                   jax.ShapeDtypeStruct((3, S, 512), F32), jax.ShapeDtypeStruct((1, 3072), F32)],
        compiler_params=_params(1))(dmerged, proj, gate_b, ya, yb, yc, wbr)


def ungate_bwd(dproj, dy, ob, oc, proj):
    S = proj.shape[0]
    ts = ROW_TILE

    def body(dp_in, dyb_ref, dyc_ref, ob_ref, oc_ref, zb_ref, zc_ref, dp_ref, dob_ref, doc_ref, dlb_ref, dlc_ref):
        del dp_in
        lane = lax.broadcasted_iota(jnp.int32, (ts, 128), 1)
        for n, (dy_ref, o_ref, z_ref, do_ref, dl_ref) in enumerate(
                ((dyb_ref, ob_ref, zb_ref, dob_ref, dlb_ref), (dyc_ref, oc_ref, zc_ref, doc_ref, dlc_ref))):
            zz = z_ref[...]
            dyv = dy_ref[0]
            sg = _sigmoid(zz)
            dp_ref[:, n * 512:(n + 1) * 512] = (dyv * _compact(o_ref) * (sg * (1.0 + zz * (1.0 - sg)))).astype(BF16)
            do_c = dyv * (zz * sg)
            for p in range(4):
                piece = do_c[:, p * 128:(p + 1) * 128]
                for h, d in ((2 * p, jnp.where(lane < 64, piece, 0.0)),
                             (2 * p + 1, jnp.where(lane < 64, pltpu.roll(piece, 64, 1), 0.0))):
                    do_ref[h] = d.astype(BF16)
                    dl_ref[h] = jnp.sum(d * o_ref[h].astype(F32), axis=-1, keepdims=True)

    col = lambda c: pl.BlockSpec((ts, 512), lambda i: (i, c))
    dysp = lambda n: pl.BlockSpec((1, ts, 512), lambda i: (n, i, 0))
    stat = pl.BlockSpec((N_HEADS, ts, 1), lambda i: (0, i, 0))
    hshape = jax.ShapeDtypeStruct((N_HEADS, S, HEAD_PAD), BF16)
    sshape = jax.ShapeDtypeStruct((N_HEADS, S, 1), F32)
    return _call(
        body, name="ungate_bwd", grid=(S // ts,),
        in_specs=[pl.BlockSpec(memory_space=pl.ANY), dysp(1), dysp(2), _hspec(ts), _hspec(ts),
                  col(P_ZB // 512), col(P_ZC // 512)],
        out_specs=[pl.BlockSpec((ts, 1024), lambda i: (i, P_ZB // 1024)), _hspec(ts), _hspec(ts), stat, stat],
        out_shape=[jax.ShapeDtypeStruct(dproj.shape, BF16), hshape, hshape, sshape, sshape],
        input_output_aliases={0: 0},
        compiler_params=_params(1))(dproj, dy, dy, ob, oc, proj, proj)


def loss_head(y, target):
    S, D = y.shape
    ts = ROW_TILE

    def body(y_ref, t_ref, dy_ref, sq_ref):
        d = y_ref[...] - t_ref[...]
        dy_ref[...] = d * (1.0 / D)
        _accumulate(sq_ref, jnp.sum(d * d, axis=0, keepdims=True), pl.program_id(0) == 0)

    row = pl.BlockSpec((ts, D), lambda i: (i, 0))
    return _call(
        body, name="loss_head", grid=(S // ts,), in_specs=[row, row],
        out_specs=[row, pl.BlockSpec((1, D), lambda i: (0, 0))],
        out_shape=[jax.ShapeDtypeStruct((S, D), F32), jax.ShapeDtypeStruct((1, D), F32)],
        compiler_params=_params(1))(y, target)


def _row_tile(rows, cols):
    for cand in (1024, 512, 256, 128, 64, 32, 16, 8):
        if rows % cand == 0 and cand * cols * 4 <= 1024 * 1024:
            return cand
    return rows


def adamw(w, grads, m, v):
    shape = w.shape
    cols = shape[-1]
    rows = int(np.prod(shape[:-1]))
    tr = _row_tile(rows, cols)
    n_g = len(grads)
    c1 = 1.0 - ADAM_B1 ** ADAM_STEP
    c2 = 1.0 - ADAM_B2 ** ADAM_STEP

    def body(*refs):
        w_ref, m_ref, v_ref = refs[:3]
        g_refs = refs[3:3 + n_g]
        go_ref, d_ref, mo_ref, vo_ref = refs[3 + n_g:]
        gv = g_refs[0][...]
        for g_ref in g_refs[1:]:
            gv = gv + g_ref[...]
        go_ref[...] = gv
        mn = ADAM_B1 * m_ref[...] + (1.0 - ADAM_B1) * gv
        vn = ADAM_B2 * v_ref[...] + (1.0 - ADAM_B2) * (gv * gv)
        mo_ref[...] = mn
        vo_ref[...] = vn
        d_ref[...] = -ADAM_LR * ((mn / c1) / (jnp.sqrt(vn / c2) + ADAM_EPS) + ADAM_WD * w_ref[...])

    blk = pl.BlockSpec((tr, cols), lambda i: (i, 0))
    sds = jax.ShapeDtypeStruct((rows, cols), F32)
    outs = _call(
        body, name="adamw", grid=(rows // tr,), in_specs=[blk] * (3 + n_g), out_specs=[blk] * 4,
        out_shape=[sds] * 4, compiler_params=_params(1))(
            *[a.reshape(rows, cols) for a in (w, m, v, *grads)])
    return [o.reshape(shape) for o in outs]


def add_lead(parts):
    n = parts.shape[0]
    shape = parts.shape[1:]
    cols = shape[-1]
    rows = int(np.prod(shape[:-1]))
    tr = _row_tile(rows, cols * n)

    def body(p_ref, o_ref):
        acc = p_ref[0].astype(F32)
        for s in range(1, n):
            acc = acc + p_ref[s].astype(F32)
        o_ref[...] = acc

    out = _call(
        body, name="add_lead", grid=(rows // tr,),
        in_specs=[pl.BlockSpec((n, tr, cols), lambda i: (0, i, 0))],
        out_specs=pl.BlockSpec((tr, cols), lambda i: (i, 0)),
        out_shape=jax.ShapeDtypeStruct((rows, cols), F32),
        compiler_params=_params(1))(parts.reshape(n, rows, cols))
    return out.reshape(shape)


ANY = pl.BlockSpec(memory_space=pl.ANY)


def _other_chips(x, y):
    return [(1 - x, y), (x, 1 - y), (1 - x, 1 - y)]


def chip_exchange(arrays, scatter, name):
    n = len(arrays)

    def body(*refs):
        _exchange_issue(refs[:n], refs[n:2 * n], refs[2 * n:], scatter, True)
        _exchange_issue(refs[:n], refs[n:2 * n], refs[2 * n:], scatter, False)

    return _call(
        body, name=name, in_specs=[ANY] * n, out_specs=[ANY] * n,
        out_shape=_exchange_out_shape(arrays, scatter), scratch_shapes=_exchange_sems(n))(*arrays)


def _exchange_out_shape(arrays, scatter):
    return [jax.ShapeDtypeStruct(a.shape if scatter else (4,) + a.shape, a.dtype) for a in arrays]


def _exchange_sems(n):
    if n == 0:
        return []
    return [pltpu.SemaphoreType.DMA((3 * n,)), pltpu.SemaphoreType.DMA((3 * n,)), pltpu.SemaphoreType.DMA((n,))]


def _exchange_issue(srcs, outs, sems, scatter, start):
    send_sems, recv_sems, local_sems = sems
    x, y, c = lax.axis_index("x"), lax.axis_index("y"), lax.axis_index("c")
    me = 2 * x + y
    for a in range(len(srcs)):
        local_src = srcs[a].at[me] if scatter else srcs[a]
        mine = pltpu.make_async_copy(local_src, outs[a].at[me], local_sems.at[a])
        sends = []
        for j, (px, py) in enumerate(_other_chips(x, y)):
            pair = dict(send_sem=send_sems.at[3 * a + j], recv_sem=recv_sems.at[3 * a + j],
                        device_id=(px, py, c), device_id_type=MESH)
            sends.append(pltpu.make_async_remote_copy(
                src_ref=srcs[a].at[2 * px + py] if scatter else srcs[a], dst_ref=outs[a].at[me], **pair))
            if not start:
                pltpu.make_async_remote_copy(src_ref=local_src, dst_ref=outs[a].at[2 * px + py], **pair).wait_recv()
        if start:
            mine.start()
            for cp in sends:
                cp.start()
        else:
            for cp in sends:
                cp.wait_send()
            mine.wait()


def sibling_exchange(arrays):
    n = len(arrays)

    def body(*refs):
        srcs, outs = refs[:n], refs[n:2 * n]
        send_sems, recv_sems = refs[2 * n:]
        x, y, c = lax.axis_index("x"), lax.axis_index("y"), lax.axis_index("c")
        copies = [pltpu.make_async_remote_copy(src_ref=srcs[a], dst_ref=outs[a], send_sem=send_sems.at[a],
                                               recv_sem=recv_sems.at[a], device_id=(x, y, 1 - c), device_id_type=MESH)
                  for a in range(n)]
        for cp in copies:
            cp.start()
        for cp in copies:
            cp.wait()

    return _call(
        body, name="sibling_exchange", in_specs=[ANY] * n, out_specs=[ANY] * n,
        out_shape=[jax.ShapeDtypeStruct(a.shape, a.dtype) for a in arrays],
        scratch_shapes=[pltpu.SemaphoreType.DMA((n,)), pltpu.SemaphoreType.DMA((n,))])(*arrays)


def _perm_from_shards(sh):
    rows = sh.shape[1]
    pieces, pos = [], 0
    for lo, hi, plo in sorted(NAT_SEGS, key=lambda s: s[2]):
        if plo > pos:
            pieces.append(jnp.zeros((rows, plo - pos), sh.dtype))
            pos = plo
        c = lo
        while c < hi:
            kk = c // SHARD_COLS
            e = min(hi, (kk + 1) * SHARD_COLS)
            pieces.append(sh[kk][:, c - kk * SHARD_COLS:e - kk * SHARD_COLS])
            c = e
        pos += hi - lo
    if pos < P_W:
        pieces.append(jnp.zeros((rows, P_W - pos), sh.dtype))
    return jnp.concatenate(pieces, axis=1)


def _shards_from_perm(p):
    out = []
    for kk in range(4):
        lo_k, hi_k = kk * SHARD_COLS, (kk + 1) * SHARD_COLS
        pieces = []
        for lo, hi, plo in NAT_SEGS:
            a, b = max(lo, lo_k), min(hi, hi_k)
            if a < b:
                pieces.append(p[:, plo + (a - lo):plo + (b - lo)])
        out.append(jnp.concatenate(pieces, axis=1))
    return jnp.stack(out)


def _split4(a, axis):
    shape = a.shape
    a = a.reshape(shape[:axis] + (4, shape[axis] // 4) + shape[axis + 1:])
    return jnp.moveaxis(a, axis, 0)


def _join4(a, axis):
    a = jnp.moveaxis(a, 0, axis)
    shape = a.shape
    return a.reshape(shape[:axis] + (4 * shape[axis + 1],) + shape[axis + 2:])


def _pad_heads(w, per_head, lo, hi):
    r = w.shape[0]
    wh = w.reshape(r, N_HEADS, per_head)[:, :, lo:hi]
    return jnp.pad(wh, ((0, 0), (0, 0), (0, HEAD_PAD - (hi - lo)))).reshape(r, N_HEADS * HEAD_PAD)


def _rope_table(S):
    half = MLA_ROPE // 2
    inv = ROPE_BASE ** (-jnp.arange(half, dtype=F32) / half)
    ang = jnp.arange(S).astype(F32)[:, None] * inv[None, :]
    cos, sin = jnp.cos(ang), jnp.sin(ang)
    z = lambda n: jnp.zeros((S, n), F32)
    c = jnp.concatenate([jnp.ones((S, MLA_NOPE), F32), cos, cos, z(32)], axis=1)
    a = jnp.concatenate([z(MLA_NOPE), -sin, z(48)], axis=1)
    b = jnp.concatenate([z(MLA_NOPE + half), sin, z(32)], axis=1)
    return jnp.stack([c, a, b])


def _band_onehot():
    t = ATT_T
    m = np.arange(2 * t)
    d = np.where(m < t, m, m - 2 * t)
    idx = np.stack([np.clip(off + d, -REL_CLIP, REL_CLIP) + REL_CLIP for off in (t, 0)])
    return (idx[:, :, None] == np.arange(2 * REL_CLIP + 1)[None, None, :]).astype(np.float32)


def bias_expand(diag):
    t = ATT_T

    def body(d_ref, o_ref):
        kc = lax.broadcasted_iota(jnp.int32, (t, t), 0) >> CHUNK_SHIFT
        qc = lax.broadcasted_iota(jnp.int32, (t, t), 1) >> CHUNK_SHIFT
        for w, visible in ((0, kc >= qc), (1, kc <= qc)):
            rows = jnp.broadcast_to(d_ref[0, w:w + 1, :], (t, 2 * t))
            skew = pltpu.roll(rows, 0, 1, stride=1, stride_axis=0)[:, :t]
            o_ref[0, 2 * w] = jnp.where(visible, skew * LOG2E, NEG_INF)
        o_ref[0, 1] = jnp.full((t, t), NEG_INF, F32)

    return _call(
        body, name="bias_expand", grid=(N_HEADS,),
        in_specs=[pl.BlockSpec((1, 2, 2 * t), lambda h: (h, 0, 0))],
        out_specs=pl.BlockSpec((1, 3, t, t), lambda h: (h, 0, 0, 0)),
        out_shape=jax.ShapeDtypeStruct((N_HEADS, 3, t, t), F32),
        compiler_params=_params(1))(diag)


def bias_fold(dtiles):
    t = ATT_T

    def body(d_ref, o_ref):
        pad = jnp.zeros((8, t), F32)
        for w in range(2):
            acc = jnp.concatenate([d_ref[0, w, 0:8, :], pad], axis=1)
            for g in range(1, t // 8):
                grp = jnp.concatenate([d_ref[0, w, 8 * g:8 * g + 8, :], pad], axis=1)
                acc = acc + pltpu.roll(grp, 2 * t - 8 * g, 1)
            out = acc[0:1, :]
            for s in range(1, 8):
                out = out + pltpu.roll(acc, 2 * t - s, 1)[s:s + 1, :]
            o_ref[0, w:w + 1, :] = out

    return _call(
        body, name="bias_fold", grid=(N_HEADS,),
        in_specs=[pl.BlockSpec((1, 2, t, t), lambda h: (h, 0, 0, 0))],
        out_specs=pl.BlockSpec((1, 2, 2 * t), lambda h: (h, 0, 0)),
        out_shape=jax.ShapeDtypeStruct((N_HEADS, 2, 2 * t), F32),
        compiler_params=_params(1))(dtiles)


def _bias_tiles(table):
    diag = jnp.einsum('hr,wdr->hwd', table, jnp.asarray(_band_onehot()), precision=lax.Precision.HIGHEST)
    return bias_expand(diag)


def _bias_tiles_grad(dtiles):
    return jnp.einsum('hwd,wdr->hr', bias_fold(dtiles), jnp.asarray(_band_onehot()),
                      precision=lax.Precision.HIGHEST)


def _layer_consts(lw):
    tri = np.tril(np.ones((SGU_BLOCK, SGU_BLOCK), np.float32))
    ws = (lw['sgu_w'] * tri).astype(BF16)
    return dict(
        ws=ws, ws_t=jnp.swapaxes(ws, 1, 2), sgu_bias=jnp.repeat(lw['sgu_b'].T, CA_HEAD_DIM, axis=1),
        bias=_bias_tiles(lw['ca_rel_bias']),
        wq=_pad_heads(lw['mla_w_uq'], MLA_QK, 0, MLA_QK),
        wk=_pad_heads(lw['mla_w_ukv'], MLA_NOPE + MLA_V, 0, MLA_NOPE),
        wv=_pad_heads(lw['mla_w_ukv'], MLA_NOPE + MLA_V, MLA_NOPE, MLA_NOPE + MLA_V),
        gate_b=lw['gate_b'].reshape(1, 3 * D_MODEL),
        pre_g=lw['pre_g'][None], post_g=lw['post_g'][None], ln_g=lw['sgu_ln_g'][None], ln_b=lw['sgu_ln_b'][None],
        qg=lw['mla_q_norm_g'][None], kvg=lw['mla_kv_norm_g'][None])


def _layer_fwd(x, lw, k, tab, next_shards):
    proj, xn = norm_matmul(x, k['pre_g'], lw['w_in'])
    ya = sgu_fwd(proj, k['ln_g'], k['ln_b'], k['ws'], k['sgu_bias'])
    qb, kb, vb, qc, kc, vc, kbt, vbt, kct, vct, cq, ckv = prep_fwd(proj, tab, k['qg'], k['kvg'], k['wq'], k['wk'],
                                                                   k['wv'])
    ob, lse_b, gathered = mla_fwd(qb, kb, vbt, next_shards)
    oc, lse_c = band_fwd(qc, kc, vct, k['bias'])
    x_new, yb, yc, merged, out = merge_fwd(ob, oc, proj, ya, k['gate_b'], lw['w_branch'], lw['w_out'], x,
                                           k['post_g'])
    saved = dict(x=x, proj=proj, xn=xn, ya=ya, yb=yb, yc=yc, qb=qb, kb=kb, vb=vb, qc=qc, kc=kc, vc=vc, kbt=kbt, kct=kct,
                 cq=cq, ckv=ckv, ob=ob, oc=oc, lse_b=lse_b, lse_c=lse_c, merged=merged, out=out)
    return x_new, saved, gathered


def _layer_bwd(g, s, lw, k, tab, pending_parts, scatter_own):
    S = g.shape[0]
    H = N_HEADS
    dout, dmerged, g_post = post_bwd(g, s['out'], k['post_g'], lw['w_out'])
    g_w_out = matmul_tn(s['merged'], dout, 512)
    dproj, dba, dbb, dbc, dy, g_gate_b = gate_bwd(dmerged, s['proj'], k['gate_b'], s['ya'], s['yb'], s['yc'],
                                                  lw['w_branch'])
    g_w_branch = jnp.stack([matmul_tn(y, d, 512) for y, d in ((s['ya'], dba), (s['yb'], dbb), (s['yc'], dbc))])
    dproj, dob, doc, dl_b, dl_c = ungate_bwd(dproj, dy, s['ob'], s['oc'], s['proj'])
    row = lambda a: a.reshape(H, 1, S)
    dqb, dkb, dvb, landed = mla_bwd(s['qb'], s['kb'], s['kbt'], s['vb'], dob, s['lse_b'], row(dl_b), pending_parts)
    dqc, dkc, dvc, dbias = band_bwd(s['qc'], s['kc'], s['kct'], s['vc'], doc, s['lse_c'], row(dl_c), k['bias'])
    dproj, dqf, dkf, dvf, g_qg, g_kvg = prep_bwd(dproj, dqb, dkb, dvb, dqc, dkc, dvc, s['proj'], tab,
                                                 k['qg'], k['kvg'], k['wq'], k['wk'], k['wv'])
    g_wq = matmul_tn(s['cq'], dqf, 512).reshape(MLA_Q_RANK, H, HEAD_PAD)[:, :, :MLA_QK]
    g_wk = matmul_tn(s['ckv'], dkf, 512).reshape(MLA_KV_RANK, H, HEAD_PAD)[:, :, :MLA_NOPE]
    g_wv = matmul_tn(s['ckv'], dvf, 512).reshape(MLA_KV_RANK, H, HEAD_PAD)[:, :, :MLA_V]
    dproj, g_ln_g, g_ln_b, g_ws, g_sgu_bias = sgu_bwd(dproj, dy, s['proj'], k['ln_g'], k['ln_b'], k['ws'],
                                                      k['ws_t'], k['sgu_bias'])
    g_w_in = matmul_tn(s['xn'], dproj, MM_TN)
    sharded = _sharded_parts(dict(
        w_in=g_w_in, mla_w_uq=g_wq.reshape(MLA_Q_RANK, H * MLA_QK),
        mla_w_ukv=jnp.concatenate([g_wk, g_wv], axis=2).reshape(MLA_KV_RANK, H * (MLA_NOPE + MLA_V)),
        w_branch=g_w_branch, gate_b=g_gate_b.reshape(N_BRANCH, D_MODEL), w_out=g_w_out))
    dx, g_pre, own_landed = proj_bwd_x(dproj, lw['w_in'], s['x'], k['pre_g'], g, sharded if scatter_own else ())
    tri = np.tril(np.ones((SGU_BLOCK, SGU_BLOCK), np.float32))
    small = _small_pack(dict(
        pre_g=g_pre[0], post_g=g_post[0], sgu_ln_g=g_ln_g[0], sgu_ln_b=g_ln_b[0],
        sgu_w=g_ws * tri, sgu_b=jnp.sum(g_sgu_bias.reshape(SGU_BLOCK, 8, CA_HEAD_DIM), axis=2).T,
        mla_q_norm_g=g_qg[0], mla_kv_norm_g=g_kvg[0], ca_rel_bias=_bias_tiles_grad(dbias)))
    return dx, (own_landed if scatter_own else sharded), small, landed


BF16_PARTS = ('w_in', 'mla_w_uq', 'mla_w_ukv', 'w_branch', 'w_out')


def _weight_shards(w, l):
    return [w[n][l].astype(BF16) if n in BF16_PARTS else w[n][l] for n in SHARDED]


def _full_weights(gathered, small):
    lw = {n: _join4(a, SHARD_AXIS[n]) for n, a in zip(SHARDED, gathered) if n != 'w_in'}
    lw['w_in'] = _perm_from_shards(gathered[0])
    lw.update(small)
    return lw


def _small_pack(grads):
    flat = jnp.concatenate([grads[n].reshape(-1) for n in SMALL])
    quarter = -(-flat.size // (4 * 1024)) * 1024
    return jnp.pad(flat, (0, 4 * quarter - flat.size)).reshape(4, quarter // 128, 128)


def _sharded_parts(grads):
    parts = [_shards_from_perm(grads['w_in'])]
    parts += [_split4(grads[n], SHARD_AXIS[n]) for n in SHARDED if n != 'w_in']
    return [p.astype(BF16) if n in BF16_PARTS else p for n, p in zip(SHARDED, parts)]


def train_step_local(x, target, w):
    S = x.shape[0]
    depth = w['w_in'].shape[0]
    tab = _rope_table(S)
    gathered = chip_exchange(_weight_shards(w, 0), False, "gather_weights")
    layer_w, consts, saved = [], [], []
    for l in range(depth):
        lw = _full_weights(gathered, {n: w[n][l] for n in SMALL})
        k = _layer_consts(lw)
        x, s, gathered = _layer_fwd(x, lw, k, tab, _weight_shards(w, l + 1) if l + 1 < depth else ())
        layer_w.append(lw)
        consts.append(k)
        saved.append(s)
    g, sq = loss_head(x, target)
    mine = [None] * depth
    pending = ()
    for l in reversed(range(depth)):
        g, sharded, small, landed = _layer_bwd(g, saved[l], layer_w[l], consts[l], tab, pending, l == 0)
        if pending:
            mine[l + 1] = [add_lead(p) for p in landed]
        pending = list(sharded) + [small]
    mine[0] = [add_lead(p) for p in pending[:-1] + list(chip_exchange(pending[-1:], True, "scatter_small"))]
    n_parts = len(mine[0])
    theirs = sibling_exchange([p for layer in mine for p in layer])
    return sq, g, [(mine[l], theirs[l * n_parts:(l + 1) * n_parts]) for l in range(depth)]


def kernel(x, w_in, pre_g, post_g, sgu_ln_g, sgu_ln_b, sgu_w, sgu_b, mla_q_norm_g, mla_kv_norm_g, mla_w_uq, mla_w_ukv, ca_rel_bias, w_branch, gate_b, w_out, loss_target, m_w_in, m_pre_g, m_post_g, m_sgu_ln_g, m_sgu_ln_b, m_sgu_w, m_sgu_b, m_mla_q_norm_g, m_mla_kv_norm_g, m_mla_w_uq, m_mla_w_ukv, m_ca_rel_bias, m_w_branch, m_gate_b, m_w_out, v_w_in, v_pre_g, v_post_g, v_sgu_ln_g, v_sgu_ln_b, v_sgu_w, v_sgu_b, v_mla_q_norm_g, v_mla_kv_norm_g, v_mla_w_uq, v_mla_w_ukv, v_ca_rel_bias, v_w_branch, v_gate_b, v_w_out):
    w = dict(w_in=w_in, pre_g=pre_g, post_g=post_g, sgu_ln_g=sgu_ln_g, sgu_ln_b=sgu_ln_b, sgu_w=sgu_w, sgu_b=sgu_b,
             mla_q_norm_g=mla_q_norm_g, mla_kv_norm_g=mla_kv_norm_g, mla_w_uq=mla_w_uq, mla_w_ukv=mla_w_ukv,
             ca_rel_bias=ca_rel_bias, w_branch=w_branch, gate_b=gate_b, w_out=w_out)
    m = dict(w_in=m_w_in, pre_g=m_pre_g, post_g=m_post_g, sgu_ln_g=m_sgu_ln_g, sgu_ln_b=m_sgu_ln_b, sgu_w=m_sgu_w,
             sgu_b=m_sgu_b, mla_q_norm_g=m_mla_q_norm_g, mla_kv_norm_g=m_mla_kv_norm_g, mla_w_uq=m_mla_w_uq,
             mla_w_ukv=m_mla_w_ukv, ca_rel_bias=m_ca_rel_bias, w_branch=m_w_branch, gate_b=m_gate_b, w_out=m_w_out)
    v = dict(w_in=v_w_in, pre_g=v_pre_g, post_g=v_post_g, sgu_ln_g=v_sgu_ln_g, sgu_ln_b=v_sgu_ln_b, sgu_w=v_sgu_w,
             sgu_b=v_sgu_b, mla_q_norm_g=v_mla_q_norm_g, mla_kv_norm_g=v_mla_kv_norm_g, mla_w_uq=v_mla_w_uq,
             mla_w_ukv=v_mla_w_ukv, ca_rel_bias=v_ca_rel_bias, w_branch=v_w_branch, gate_b=v_gate_b, w_out=v_w_out)
    depth = w_in.shape[0]
    sq, grad_x, reduced = train_step_local(x[0], loss_target[0], w)
    loss = lax.psum(0.5 * jnp.sum(sq) / D_MODEL, ("x", "y", "c"))

    out = {}
    for a, n in enumerate(SHARDED):
        mine = jnp.stack([reduced[l][0][a] for l in range(depth)])
        theirs = jnp.stack([reduced[l][1][a] for l in range(depth)])
        out[n] = adamw(w[n], [mine, theirs], m[n], v[n])
    small = jnp.stack([jnp.stack([reduced[l][0][-1] for l in range(depth)]),
                       jnp.stack([reduced[l][1][-1] for l in range(depth)])])
    quarter = add_lead(small)
    full = chip_exchange([quarter], False, "gather_small")[0]
    full = jnp.moveaxis(full, 0, 1).reshape(depth, -1)
    off = 0
    for n in SMALL:
        size = int(np.prod(w[n].shape[1:]))
        out[n] = adamw(w[n], [full[:, off:off + size].reshape(w[n].shape)], m[n], v[n])
        off += size
    return (loss, grad_x[None], *[out[n][0] for n in WEIGHTS], *[out[n][1] for n in WEIGHTS],
            *[out[n][2] for n in WEIGHTS], *[out[n][3] for n in WEIGHTS])
```

```python
import numpy as np
import jax
import jax.numpy as jnp
from jax import lax
from jax.experimental import pallas as pl
from jax.experimental.pallas import tpu as pltpu

F32 = jnp.float32
BF16 = jnp.bfloat16
MESH = pl.DeviceIdType.MESH

EPS = 1e-6
NEG_INF = -1e30
D_MODEL = 1024
BR_WIDTH = 512
N_BRANCH = 3
N_HEADS = 8
HEAD_PAD = 128
CHUNK_SHIFT = 6
SGU_BLOCK = 128
MLA_NOPE, MLA_ROPE, MLA_V = 64, 32, 64
MLA_QK = MLA_NOPE + MLA_ROPE
MLA_Q_RANK, MLA_KV_RANK = 256, 128
CA_HEAD_DIM = 64
REL_CLIP = 128
ROPE_BASE = 10000.0
D_IN = 7584

ADAM_LR, ADAM_B1, ADAM_B2, ADAM_EPS, ADAM_WD, ADAM_STEP = 0.001, 0.9, 0.999, 1e-08, 0.01, 10

P_QC, P_KC, P_VC, P_QD, P_KVD, P_KR, P_ZB, P_ZC, P_G, P_U, P_V, P_ZA, P_W = (
    0, 512, 1024, 1536, 1792, 1920, 2048, 2560, 3072, 6144, 6656, 7168, 7680)
NAT_SEGS = [(0, 1536, P_U), (1536, 1920, P_QD), (1920, 1952, P_KR + MLA_NOPE), (1952, 2464, P_ZB),
            (2464, 4000, P_QC), (4000, 4512, P_ZC), (4512, 7584, P_G)]
SHARD_COLS = D_IN // 4

VMEM_LIMIT = 48 * 1024 * 1024
ATT_T = 512
MLA_HEADS_PER_STEP = 2
MLA_FWD_HEADS_PER_STEP = 4
ROW_TILE = 256
LIGHT_ROW_TILE = 512
MM_TM = 512
MM_TN = 1536
LOG2E = 1.4426950408889634
MLA_SCALE = MLA_QK ** -0.5
CA_SCALE = CA_HEAD_DIM ** -0.5

WEIGHTS = ['w_in', 'pre_g', 'post_g', 'sgu_ln_g', 'sgu_ln_b', 'sgu_w', 'sgu_b', 'mla_q_norm_g',
           'mla_kv_norm_g', 'mla_w_uq', 'mla_w_ukv', 'ca_rel_bias', 'w_branch', 'gate_b', 'w_out']
SHARDED = ['w_in', 'mla_w_uq', 'mla_w_ukv', 'w_branch', 'gate_b', 'w_out']
SMALL = ['pre_g', 'post_g', 'sgu_ln_g', 'sgu_ln_b', 'sgu_w', 'sgu_b', 'mla_q_norm_g',
         'mla_kv_norm_g', 'ca_rel_bias']
SHARD_AXIS = {'w_in': 1, 'mla_w_uq': 1, 'mla_w_ukv': 1, 'w_branch': 2, 'gate_b': 1, 'w_out': 0}


def _call(body, **kw):
    return pl.pallas_call(body, **kw)


def _params(n_axes):
    return pltpu.CompilerParams(dimension_semantics=("arbitrary",) * n_axes,
                                vmem_limit_bytes=VMEM_LIMIT)


def _nt(a, b):
    return lax.dot_general(a, b, (((1,), (1,)), ((), ())), preferred_element_type=F32)


def _nn(a, b):
    return jnp.dot(a, b, preferred_element_type=F32)


def _tn(a, b):
    return lax.dot_general(a, b, (((0,), (0,)), ((), ())), preferred_element_type=F32)


def _rms(xv, g):
    r = lax.rsqrt(jnp.mean(xv * xv, axis=-1, keepdims=True) + EPS)
    return xv * r * g, r


def _rms_bwd(xv, g, r, dy):
    gy = dy * g
    dx = r * gy - xv * (r * r * r) * jnp.mean(xv * gy, axis=-1, keepdims=True)
    dg = jnp.sum(dy * (xv * r), axis=0, keepdims=True)
    return dx, dg


def _sigmoid(z):
    return 1.0 / (1.0 + jnp.exp(-z))


def _rope(xv, c, a, b):
    return xv * c + pltpu.roll(xv, 112, 1) * a + pltpu.roll(xv, 16, 1) * b


def _accumulate(ref, val, first):
    @pl.when(first)
    def _():
        ref[...] = val

    @pl.when(jnp.logical_not(first))
    def _():
        ref[...] += val


def norm_matmul(x, g, w):
    S, D = x.shape
    N = w.shape[1]
    tm, tn = min(S, 2 * MM_TM), MM_TN

    def body(x_ref, g_ref, w_ref, o_ref, xn_ref):
        @pl.when(pl.program_id(1) == 0)
        def _():
            y, _ = _rms(x_ref[...], g_ref[...])
            xn_ref[...] = y.astype(BF16)

        o_ref[...] = _nn(xn_ref[...], w_ref[...])

    return _call(
        body, name="norm_matmul", grid=(S // tm, N // tn),
        in_specs=[pl.BlockSpec((tm, D), lambda i, j: (i, 0)),
                  pl.BlockSpec((1, D), lambda i, j: (0, 0)),
                  pl.BlockSpec((D, tn), lambda i, j: (0, j))],
        out_specs=[pl.BlockSpec((tm, tn), lambda i, j: (i, j)),
                   pl.BlockSpec((tm, D), lambda i, j: (i, 0))],
        out_shape=[jax.ShapeDtypeStruct((S, N), F32), jax.ShapeDtypeStruct((S, D), BF16)],
        compiler_params=_params(2))(x, g, w)


def proj_bwd_x(dproj, w, x, g, resid, exchange=()):
    S, N = dproj.shape
    D = x.shape[1]
    tm, tk = min(S, MM_TM), MM_TN
    nk, n_ex = N // tk, len(exchange)

    def body(dp_ref, w_ref, x_ref, g_ref, r_ref, *rest):
        ex_src, (dx_ref, dg_ref), ex_out, (acc_ref,), ex_sems = _split_refs(
            rest, (n_ex, 2, n_ex, 1, 3 if n_ex else 0))
        i, k = pl.program_id(0), pl.program_id(1)

        if n_ex:
            @pl.when(jnp.logical_and(i == 0, k == 0))
            def _():
                _exchange_issue(ex_src, ex_out, ex_sems, True, True)

        @pl.when(k == 0)
        def _():
            acc_ref[...] = jnp.zeros_like(acc_ref)

        acc_ref[...] += _nt(dp_ref[...].astype(BF16), w_ref[...])

        @pl.when(k == nk - 1)
        def _():
            xv = x_ref[...]
            _, r = _rms(xv, g_ref[...])
            dx, dg = _rms_bwd(xv, g_ref[...], r, acc_ref[...])
            dx_ref[...] = dx + r_ref[...]
            _accumulate(dg_ref, dg, i == 0)

        if n_ex:
            @pl.when(jnp.logical_and(i == S // tm - 1, k == nk - 1))
            def _():
                _exchange_issue(ex_src, ex_out, ex_sems, True, False)

    outs = _call(
        body, name="proj_bwd_x_scatter" if n_ex else "proj_bwd_x", grid=(S // tm, nk),
        in_specs=[pl.BlockSpec((tm, tk), lambda i, k: (i, k)),
                  pl.BlockSpec((D, tk), lambda i, k: (0, k)),
                  pl.BlockSpec((tm, D), lambda i, k: (i, 0)),
                  pl.BlockSpec((1, D), lambda i, k: (0, 0)),
                  pl.BlockSpec((tm, D), lambda i, k: (i, 0))] + [ANY] * n_ex,
        out_specs=[pl.BlockSpec((tm, D), lambda i, k: (i, 0)),
                   pl.BlockSpec((1, D), lambda i, k: (0, 0))] + [ANY] * n_ex,
        out_shape=[jax.ShapeDtypeStruct((S, D), F32), jax.ShapeDtypeStruct((1, D), F32)] +
        _exchange_out_shape(exchange, True),
        scratch_shapes=[pltpu.VMEM((tm, D), F32)] + _exchange_sems(n_ex),
        compiler_params=_params(2))(dproj, w, x, g, resid, *exchange)
    return outs[0], outs[1], outs[2:]


def matmul_tn(a, b, tn):
    S, M = a.shape
    N = b.shape[1]
    tk = min(S, 2 * MM_TM)

    def body(a_ref, b_ref, o_ref):
        @pl.when(pl.program_id(1) == 0)
        def _():
            o_ref[...] = jnp.zeros_like(o_ref)

        o_ref[...] += _tn(a_ref[...].astype(BF16), b_ref[...].astype(BF16))

    return _call(
        body, name="matmul_tn", grid=(N // tn, S // tk),
        in_specs=[pl.BlockSpec((tk, M), lambda j, k: (k, 0)),
                  pl.BlockSpec((tk, tn), lambda j, k: (k, j))],
        out_specs=pl.BlockSpec((M, tn), lambda j, k: (0, j)),
        out_shape=jax.ShapeDtypeStruct((M, N), F32),
        compiler_params=_params(2))(a, b)


def _sgu_block(vv, g, b, ws_ref, lane):
    mu = jnp.mean(vv, axis=-1, keepdims=True)
    xc = vv - mu
    r = lax.rsqrt(jnp.mean(xc * xc, axis=-1, keepdims=True) + EPS)
    xhat = xc * r
    vln = (xhat * g + b).astype(BF16)
    pieces = []
    for p in range(4):
        vp = vln[:, p * 128:(p + 1) * 128]
        pieces.append(jnp.where(lane < 64, _nn(ws_ref[2 * p], vp), _nn(ws_ref[2 * p + 1], vp)))
    return xhat, r, vln, jnp.concatenate(pieces, axis=1)


def sgu_fwd(proj, ln_g, ln_b, ws, bias_full):
    S = proj.shape[0]
    ts = LIGHT_ROW_TILE

    def body(u_ref, v_ref, z_ref, g_ref, b_ref, ws_ref, bf_ref, y_ref):
        lane = lax.broadcasted_iota(jnp.int32, (SGU_BLOCK, 128), 1)
        for blk in range(ts // SGU_BLOCK):
            rows = slice(blk * SGU_BLOCK, (blk + 1) * SGU_BLOCK)
            _, _, _, mixed = _sgu_block(v_ref[rows, :], g_ref[...], b_ref[...], ws_ref, lane)
            mixed = mixed + bf_ref[...]
            zz = z_ref[rows, :]
            y_ref[rows, :] = (u_ref[rows, :] * mixed * (zz * _sigmoid(zz))).astype(BF16)

    col = lambda c: pl.BlockSpec((ts, BR_WIDTH), lambda i: (i, c))
    full = lambda shape: pl.BlockSpec(shape, lambda i: (0,) * len(shape))
    return _call(
        body, name="sgu_fwd", grid=(S // ts,),
        in_specs=[col(P_U // 512), col(P_V // 512), col(P_ZA // 512),
                  full((1, BR_WIDTH)), full((1, BR_WIDTH)), full((8, 128, 128)), full((128, BR_WIDTH))],
        out_specs=pl.BlockSpec((ts, BR_WIDTH), lambda i: (i, 0)),
        out_shape=jax.ShapeDtypeStruct((S, BR_WIDTH), BF16),
        compiler_params=_params(1))(proj, proj, proj, ln_g, ln_b, ws, bias_full)


def sgu_bwd(dproj, dy, proj, ln_g, ln_b, ws, ws_t, bias_full):
    S = proj.shape[0]
    ts = LIGHT_ROW_TILE

    def body(dp_in, dy_ref, u_ref, v_ref, z_ref, g_ref, b_ref, ws_ref, wst_ref, bf_ref,
             dp_ref, gg_ref, gb_ref, gws_ref, gbf_ref):
        del dp_in
        first = pl.program_id(0) == 0

        @pl.when(first)
        def _():
            gg_ref[...] = jnp.zeros_like(gg_ref)
            gb_ref[...] = jnp.zeros_like(gb_ref)
            gws_ref[...] = jnp.zeros_like(gws_ref)
            gbf_ref[...] = jnp.zeros_like(gbf_ref)

        lane = lax.broadcasted_iota(jnp.int32, (SGU_BLOCK, 128), 1)
        for blk in range(ts // SGU_BLOCK):
            rows = slice(blk * SGU_BLOCK, (blk + 1) * SGU_BLOCK)
            g = g_ref[...]
            xhat, r, vln, mixed = _sgu_block(v_ref[rows, :], g, b_ref[...], ws_ref, lane)
            mixed = mixed + bf_ref[...]
            zz = z_ref[rows, :]
            uu = u_ref[rows, :]
            dyv = dy_ref[0, rows, :]
            sg = _sigmoid(zz)
            sil = zz * sg
            dmixed = dyv * uu * sil
            dp_ref[rows, 0:512] = (dyv * mixed * sil).astype(BF16)
            dp_ref[rows, 1024:1536] = (dyv * uu * mixed * (sg * (1.0 + zz * (1.0 - sg)))).astype(BF16)
            gbf_ref[...] += dmixed
            dmb = dmixed.astype(BF16)
            pieces = []
            for p in range(4):
                dmp = dmb[:, p * 128:(p + 1) * 128]
                vp = vln[:, p * 128:(p + 1) * 128]
                pieces.append(jnp.where(lane < 64, _nn(wst_ref[2 * p], dmp), _nn(wst_ref[2 * p + 1], dmp)))
                zero = jnp.zeros_like(dmp)
                gws_ref[2 * p] += _nt(jnp.where(lane < 64, dmp, zero), vp)
                gws_ref[2 * p + 1] += _nt(jnp.where(lane >= 64, dmp, zero), vp)
            dvln = jnp.concatenate(pieces, axis=1)
            dxh = dvln * g
            dp_ref[rows, 512:1024] = (r * (dxh - jnp.mean(dxh, axis=-1, keepdims=True)
                                           - xhat * jnp.mean(dxh * xhat, axis=-1, keepdims=True))).astype(BF16)
            gg_ref[...] += jnp.sum(dvln * xhat, axis=0, keepdims=True)
            gb_ref[...] += jnp.sum(dvln, axis=0, keepdims=True)

    col = lambda c: pl.BlockSpec((ts, BR_WIDTH), lambda i: (i, c))
    full = lambda shape: pl.BlockSpec(shape, lambda i: (0,) * len(shape))
    return _call(
        body, name="sgu_bwd", grid=(S // ts,),
        in_specs=[pl.BlockSpec(memory_space=pl.ANY),
                  pl.BlockSpec((1, ts, BR_WIDTH), lambda i: (0, i, 0)),
                  col(P_U // 512), col(P_V // 512), col(P_ZA // 512),
                  full((1, BR_WIDTH)), full((1, BR_WIDTH)), full((8, 128, 128)), full((8, 128, 128)),
                  full((128, BR_WIDTH))],
        out_specs=[pl.BlockSpec((ts, 1536), lambda i: (i, P_U // 1536)),
                   full((1, BR_WIDTH)), full((1, BR_WIDTH)), full((8, 128, 128)), full((128, BR_WIDTH))],
        out_shape=[jax.ShapeDtypeStruct(dproj.shape, BF16),
                   jax.ShapeDtypeStruct((1, BR_WIDTH), F32), jax.ShapeDtypeStruct((1, BR_WIDTH), F32),
                   jax.ShapeDtypeStruct((8, 128, 128), F32), jax.ShapeDtypeStruct((128, BR_WIDTH), F32)],
        input_output_aliases={0: 0},
        compiler_params=_params(1))(dproj, dy, proj, proj, proj, ln_g, ln_b, ws, ws_t, bias_full)


def _hspec(ts):
    return pl.BlockSpec((N_HEADS, ts, HEAD_PAD), lambda i: (0, i, 0))


def prep_fwd(proj, tab, qg, kvg, wq, wk, wv):
    S = proj.shape[0]
    ts = LIGHT_ROW_TILE

    def body(qc_ref, kc_ref, vc_ref, qd_ref, kvd_ref, kr_ref, tab_ref, qg_ref, kvg_ref,
             wq_ref, wk_ref, wv_ref, qb, kb, vb, qc, kc, vc, kbt, vbt, kct, vct, cq_o, ckv_o):
        c, a, b = tab_ref[0], tab_ref[1], tab_ref[2]
        cq, _ = _rms(qd_ref[...], qg_ref[...])
        ckv, _ = _rms(kvd_ref[...], kvg_ref[...])
        cqb, ckvb = cq.astype(BF16), ckv.astype(BF16)
        cq_o[...] = cqb
        ckv_o[...] = ckvb
        krr = _rope(kr_ref[...], c, a, b)
        lane = lax.broadcasted_iota(jnp.int32, (ts, 128), 1)
        ones_lane = jnp.where(lane == MLA_V, 1.0, 0.0)
        for h in range(N_HEADS):
            cols = slice(h * HEAD_PAD, (h + 1) * HEAD_PAD)
            qb[h] = (_rope(_nn(cqb, wq_ref[:, cols]), c, a, b) * (MLA_SCALE * LOG2E)).astype(BF16)
            kh = _nn(ckvb, wk_ref[:, cols]) + krr
            vh = _nn(ckvb, wv_ref[:, cols]) + ones_lane
            kb[h], kbt[h] = kh.astype(BF16), kh.T.astype(BF16)
            vb[h], vbt[h] = vh.astype(BF16), vh.T.astype(BF16)
        for p in range(4):
            piece = qc_ref[:, p * 128:(p + 1) * 128] * (CA_SCALE * LOG2E)
            qc[2 * p] = jnp.where(lane < 64, piece, 0.0).astype(BF16)
            qc[2 * p + 1] = jnp.where(lane < 64, pltpu.roll(piece, 64, 1), 0.0).astype(BF16)
            for src, dst, dst_t, pad in ((kc_ref, kc, kct, 0.0), (vc_ref, vc, vct, ones_lane)):
                piece = src[:, p * 128:(p + 1) * 128]
                for h, head in ((2 * p, jnp.where(lane < 64, piece, pad)),
                                (2 * p + 1, jnp.where(lane < 64, pltpu.roll(piece, 64, 1), pad))):
                    dst[h], dst_t[h] = head.astype(BF16), head.T.astype(BF16)

    col = lambda w, c: pl.BlockSpec((ts, w), lambda i: (i, c))
    full = lambda shape: pl.BlockSpec(shape, lambda i: (0,) * len(shape))
    hshape = jax.ShapeDtypeStruct((N_HEADS, S, HEAD_PAD), BF16)
    tshape = jax.ShapeDtypeStruct((N_HEADS, HEAD_PAD, S), BF16)
    tspec = pl.BlockSpec((N_HEADS, HEAD_PAD, ts), lambda i: (0, 0, i))
    return _call(
        body, name="prep_fwd", grid=(S // ts,),
        in_specs=[col(512, P_QC // 512), col(512, P_KC // 512), col(512, P_VC // 512),
                  col(256, P_QD // 256), col(128, P_KVD // 128), col(128, P_KR // 128),
                  pl.BlockSpec((3, ts, 128), lambda i: (0, i, 0)),
                  full((1, MLA_Q_RANK)), full((1, MLA_KV_RANK)),
                  full((MLA_Q_RANK, 1024)), full((MLA_KV_RANK, 1024)), full((MLA_KV_RANK, 1024))],
        out_specs=[_hspec(ts)] * 6 + [tspec] * 4 + [pl.BlockSpec((ts, MLA_Q_RANK), lambda i: (i, 0)),
                                                    pl.BlockSpec((ts, MLA_KV_RANK), lambda i: (i, 0))],
        out_shape=[hshape] * 6 + [tshape] * 4 + [jax.ShapeDtypeStruct((S, MLA_Q_RANK), BF16),
                                                 jax.ShapeDtypeStruct((S, MLA_KV_RANK), BF16)],
        compiler_params=_params(1))(proj, proj, proj, proj, proj, proj, tab, qg, kvg, wq, wk, wv)


def prep_bwd(dproj, dqb, dkb, dvb, dqc, dkc, dvc, proj, tab, qg, kvg, wq, wk, wv):
    S = proj.shape[0]
    ts = LIGHT_ROW_TILE

    def body(dp_in, dqb_r, dkb_r, dvb_r, dqc_r, dkc_r, dvc_r, qd_ref, kvd_ref, tab_ref, qg_ref, kvg_ref,
             wq_ref, wk_ref, wv_ref, dp_ref, dqf, dkf, dvf, gq_ref, gkv_ref):
        del dp_in
        c, a, b = tab_ref[0], -tab_ref[1], -tab_ref[2]
        qd, kvd = qd_ref[...], kvd_ref[...]
        _, rq = _rms(qd, qg_ref[...])
        _, rkv = _rms(kvd, kvg_ref[...])
        dcq = jnp.zeros((ts, MLA_Q_RANK), F32)
        dckv = jnp.zeros((ts, MLA_KV_RANK), F32)
        dksum = jnp.zeros((ts, HEAD_PAD), F32)
        for h in range(N_HEADS):
            cols = slice(h * HEAD_PAD, (h + 1) * HEAD_PAD)
            dqh = _rope(dqb_r[h].astype(F32) * MLA_SCALE, c, a, b).astype(BF16)
            dqf[:, cols] = dqh
            dcq = dcq + _nt(dqh, wq_ref[:, cols])
            dk = dkb_r[h].astype(F32) * (1.0 / LOG2E)
            dksum = dksum + dk
            dkh = dk.astype(BF16)
            dkf[:, cols] = dkh
            dvh = dvb_r[h].astype(BF16)
            dvf[:, cols] = dvh
            dckv = dckv + _nt(dkh, wk_ref[:, cols]) + _nt(dvh, wv_ref[:, cols])
        lane = lax.broadcasted_iota(jnp.int32, (ts, 128), 1)
        rope_lanes = jnp.logical_and(lane >= MLA_NOPE, lane < MLA_QK)
        dp_ref[:, P_KR:P_KR + 128] = jnp.where(rope_lanes, _rope(dksum, c, a, b), 0.0).astype(BF16)
        dqd, gq = _rms_bwd(qd, qg_ref[...], rq, dcq)
        dkvd, gkv = _rms_bwd(kvd, kvg_ref[...], rkv, dckv)
        dp_ref[:, P_QD:P_QD + 256] = dqd.astype(BF16)
        dp_ref[:, P_KVD:P_KVD + 128] = dkvd.astype(BF16)
        first = pl.program_id(0) == 0
        _accumulate(gq_ref, gq, first)
        _accumulate(gkv_ref, gkv, first)
        for src, base, factor in ((dqc_r, P_QC, CA_SCALE), (dkc_r, P_KC, 1.0 / LOG2E), (dvc_r, P_VC, 1.0)):
            for p in range(4):
                dp_ref[:, base + p * 128:base + (p + 1) * 128] = (
                    (src[2 * p].astype(F32) + pltpu.roll(src[2 * p + 1].astype(F32), 64, 1)) * factor).astype(BF16)

    col = lambda w, c: pl.BlockSpec((ts, w), lambda i: (i, c))
    full = lambda shape: pl.BlockSpec(shape, lambda i: (0,) * len(shape))
    wide = jax.ShapeDtypeStruct((S, 1024), BF16)
    return _call(
        body, name="prep_bwd", grid=(S // ts,),
        in_specs=[pl.BlockSpec(memory_space=pl.ANY)] + [_hspec(ts)] * 6 +
                 [col(256, P_QD // 256), col(128, P_KVD // 128),
                  pl.BlockSpec((3, ts, 128), lambda i: (0, i, 0)),
                  full((1, MLA_Q_RANK)), full((1, MLA_KV_RANK)),
                  full((MLA_Q_RANK, 1024)), full((MLA_KV_RANK, 1024)), full((MLA_KV_RANK, 1024))],
        out_specs=[pl.BlockSpec((ts, 2048), lambda i: (i, 0))] + [pl.BlockSpec((ts, 1024), lambda i: (i, 0))] * 3 +
                  [full((1, MLA_Q_RANK)), full((1, MLA_KV_RANK))],
        out_shape=[jax.ShapeDtypeStruct(dproj.shape, BF16), wide, wide, wide,
                   jax.ShapeDtypeStruct((1, MLA_Q_RANK), F32), jax.ShapeDtypeStruct((1, MLA_KV_RANK), F32)],
        input_output_aliases={0: 0},
        compiler_params=_params(1))(dproj, dqb, dkb, dvb, dqc, dkc, dvc, proj, proj, tab, qg, kvg, wq, wk, wv)


def _diag_visible(t):
    r = lax.broadcasted_iota(jnp.int32, (t, t), 0) >> CHUNK_SHIFT
    c = lax.broadcasted_iota(jnp.int32, (t, t), 1) >> CHUNK_SHIFT
    return r <= c


def _pair_tables(nq, kv_major):
    if kv_major:
        pairs = [(kb, qi) for kb in range(nq) for qi in range(kb, nq)]
    else:
        pairs = [(kb, qi) for qi in range(nq) for kb in range(qi + 1)]
    return (jnp.asarray(np.array([p[0] for p in pairs], np.int32)),
            jnp.asarray(np.array([p[1] for p in pairs], np.int32)), len(pairs))


def _finish_softmax(acc, m):
    l = acc[MLA_V:MLA_V + 1, :]
    row = lax.broadcasted_iota(jnp.int32, acc.shape, 0)
    return jnp.where(row < MLA_V, acc / l, 0.0).T.astype(BF16), m + jnp.log2(l)


def _split_refs(refs, counts):
    out, pos = [], 0
    for c in counts:
        out.append(refs[pos:pos + c])
        pos += c
    return out


def mla_fwd(q, k, vt, exchange=()):
    H, S, _ = q.shape
    t, hb, n_ex = ATT_T, MLA_FWD_HEADS_PER_STEP, len(exchange)
    kb_tab, qi_tab, n_pairs = _pair_tables(S // t, False)

    def body(kb_ref, qi_ref, q_ref, k_ref, vt_ref, *rest):
        ex_src, (o_ref, lse_ref), ex_out, (m_s, acc_s), ex_sems = _split_refs(rest, (n_ex, 2, n_ex, 2, 3 if n_ex else 0))
        hg, p_id = pl.program_id(0), pl.program_id(1)
        kb, qi = kb_ref[p_id], qi_ref[p_id]

        if n_ex:
            @pl.when(jnp.logical_and(hg == 0, p_id == 0))
            def _():
                _exchange_issue(ex_src, ex_out, ex_sems, False, True)

        @pl.when(kb == 0)
        def _():
            m_s[...] = jnp.full_like(m_s, NEG_INF)
            acc_s[...] = jnp.zeros_like(acc_s)

        def step(masked):
            for h in range(hb):
                st = _nt(k_ref[h], q_ref[h])
                if masked:
                    st = jnp.where(_diag_visible(t), st, NEG_INF)
                m_prev = m_s[h]
                m_new = jnp.maximum(m_prev, jnp.max(st, axis=0, keepdims=True))
                p = jnp.exp2(st - m_new)
                acc_s[h] = jnp.exp2(m_prev - m_new) * acc_s[h] + _nn(vt_ref[h], p.astype(BF16))
                m_s[h] = m_new

        @pl.when(kb < qi)
        def _():
            step(False)

        @pl.when(kb == qi)
        def _():
            step(True)
            for h in range(hb):
                o_ref[h], lse_ref[h] = _finish_softmax(acc_s[h], m_s[h])

        if n_ex:
            @pl.when(jnp.logical_and(hg == H // hb - 1, p_id == n_pairs - 1))
            def _():
                _exchange_issue(ex_src, ex_out, ex_sems, False, False)

    grid_spec = pltpu.PrefetchScalarGridSpec(
        num_scalar_prefetch=2, grid=(H // hb, n_pairs),
        in_specs=[pl.BlockSpec((hb, t, HEAD_PAD), lambda h, p, kb, qi: (h, qi[p], 0)),
                  pl.BlockSpec((hb, t, HEAD_PAD), lambda h, p, kb, qi: (h, kb[p], 0)),
                  pl.BlockSpec((hb, HEAD_PAD, t), lambda h, p, kb, qi: (h, 0, kb[p]))] + [ANY] * n_ex,
        out_specs=[pl.BlockSpec((hb, t, HEAD_PAD), lambda h, p, kb, qi: (h, qi[p], 0)),
                   pl.BlockSpec((hb, 1, t), lambda h, p, kb, qi: (h, 0, qi[p]))] + [ANY] * n_ex,
        scratch_shapes=[pltpu.VMEM((hb, 1, t), F32), pltpu.VMEM((hb, HEAD_PAD, t), F32)] + _exchange_sems(n_ex))
    outs = _call(
        body, name="mla_fwd_gather" if n_ex else "mla_fwd", grid_spec=grid_spec,
        out_shape=[jax.ShapeDtypeStruct((H, S, HEAD_PAD), BF16), jax.ShapeDtypeStruct((H, 1, S), F32)] +
        _exchange_out_shape(exchange, False),
        compiler_params=_params(2))(kb_tab, qi_tab, q, k, vt, *exchange)
    return outs[0], outs[1], outs[2:]


def mla_bwd(q, k, kt, v, do, lse, delta, exchange=()):
    H, S, _ = q.shape
    t, hb, n_ex = ATT_T, MLA_HEADS_PER_STEP, len(exchange)
    nq = S // t
    kb_tab, qi_tab, n_pairs = _pair_tables(nq, True)

    def body(kb_ref, qi_ref, q_ref, k_ref, kt_ref, v_ref, do_ref, lse_ref, dl_ref, *rest):
        ex_src, (dq_ref, dk_ref, dv_ref), ex_out, (dqt_s, dk_s, dv_s), ex_sems = _split_refs(
            rest, (n_ex, 3, n_ex, 3, 3 if n_ex else 0))
        hg, p_id = pl.program_id(0), pl.program_id(1)
        kb, qi = kb_ref[p_id], qi_ref[p_id]

        if n_ex:
            @pl.when(jnp.logical_and(hg == 0, p_id == 0))
            def _():
                _exchange_issue(ex_src, ex_out, ex_sems, True, True)

        @pl.when(p_id == 0)
        def _():
            dqt_s[...] = jnp.zeros_like(dqt_s)

        @pl.when(qi == kb)
        def _():
            dk_s[...] = jnp.zeros_like(dk_s)
            dv_s[...] = jnp.zeros_like(dv_s)

        def step(masked):
            for h in range(hb):
                st = _nt(k_ref[h], q_ref[h])
                if masked:
                    st = jnp.where(_diag_visible(t), st, NEG_INF)
                pt = jnp.exp2(st - lse_ref[h])
                dv_s[h] += _nn(pt.astype(BF16), do_ref[h])
                dsb = (pt * (_nt(v_ref[h], do_ref[h]) - dl_ref[h])).astype(BF16)
                dk_s[h] += _nn(dsb, q_ref[h])
                dqt_s[h, qi] += _nn(kt_ref[h], dsb)

        @pl.when(qi == kb)
        def _():
            step(True)
            rows = pl.ds(pl.multiple_of(qi * t, t), t)
            for h in range(hb):
                dq_ref[h, rows, :] = dqt_s[h, qi].T.astype(BF16)

        @pl.when(qi > kb)
        def _():
            step(False)

        @pl.when(qi == nq - 1)
        def _():
            dk_ref[...] = dk_s[...].astype(BF16)
            dv_ref[...] = dv_s[...].astype(BF16)

        if n_ex:
            @pl.when(jnp.logical_and(hg == H // hb - 1, p_id == n_pairs - 1))
            def _():
                _exchange_issue(ex_src, ex_out, ex_sems, True, False)

    qtile = pl.BlockSpec((hb, t, HEAD_PAD), lambda h, p, kb, qi: (h, qi[p], 0))
    ktile = pl.BlockSpec((hb, t, HEAD_PAD), lambda h, p, kb, qi: (h, kb[p], 0))
    stat = pl.BlockSpec((hb, 1, t), lambda h, p, kb, qi: (h, 0, qi[p]))
    grid_spec = pltpu.PrefetchScalarGridSpec(
        num_scalar_prefetch=2, grid=(H // hb, n_pairs),
        in_specs=[qtile, ktile, pl.BlockSpec((hb, HEAD_PAD, t), lambda h, p, kb, qi: (h, 0, kb[p])), ktile, qtile,
                  stat, stat] + [ANY] * n_ex,
        out_specs=[pl.BlockSpec((hb, S, HEAD_PAD), lambda h, p, kb, qi: (h, 0, 0)), ktile, ktile] + [ANY] * n_ex,
        scratch_shapes=[pltpu.VMEM((hb, nq, HEAD_PAD, t), F32), pltpu.VMEM((hb, t, HEAD_PAD), F32),
                        pltpu.VMEM((hb, t, HEAD_PAD), F32)] + _exchange_sems(n_ex))
    outs = _call(
        body, name="mla_bwd_scatter" if n_ex else "mla_bwd", grid_spec=grid_spec,
        out_shape=[jax.ShapeDtypeStruct((H, S, HEAD_PAD), BF16)] * 3 + _exchange_out_shape(exchange, True),
        compiler_params=_params(2))(kb_tab, qi_tab, q, k, kt, v, do, lse, delta, *exchange)
    return outs[0], outs[1], outs[2], outs[3:]


def _band_specs(t, hb):
    prev = lambda i: jnp.maximum(i - 1, 0)
    return dict(
        cur=pl.BlockSpec((hb, t, HEAD_PAD), lambda h, i: (h, i, 0)),
        prev=pl.BlockSpec((hb, t, HEAD_PAD), lambda h, i: (h, prev(i), 0)),
        cur_t=pl.BlockSpec((hb, HEAD_PAD, t), lambda h, i: (h, 0, i)),
        prev_t=pl.BlockSpec((hb, HEAD_PAD, t), lambda h, i: (h, 0, prev(i))),
        stat=pl.BlockSpec((hb, 1, t), lambda h, i: (h, 0, i)),
        bias_prev=pl.BlockSpec((hb, 1, t, t), lambda h, i: (h, jnp.where(i == 0, 1, 0), 0, 0)),
        bias_cur=pl.BlockSpec((hb, 1, t, t), lambda h, i: (h, 2, 0, 0)))


def band_fwd(q, k, vt, bias):
    H, S, _ = q.shape
    t, hb = ATT_T, MLA_HEADS_PER_STEP
    sp = _band_specs(t, hb)

    def body(q_ref, kp_ref, kc_ref, vtp_ref, vtc_ref, bp_ref, bc_ref, o_ref, lse_ref):
        for h in range(hb):
            s0 = _nt(kp_ref[h], q_ref[h]) + bp_ref[h, 0]
            s1 = _nt(kc_ref[h], q_ref[h]) + bc_ref[h, 0]
            m = jnp.maximum(jnp.max(s0, axis=0, keepdims=True), jnp.max(s1, axis=0, keepdims=True))
            ot = (_nn(vtp_ref[h], jnp.exp2(s0 - m).astype(BF16)) +
                  _nn(vtc_ref[h], jnp.exp2(s1 - m).astype(BF16)))
            o_ref[h], lse_ref[h] = _finish_softmax(ot, m)

    return _call(
        body, name="band_fwd", grid=(H // hb, S // t),
        in_specs=[sp['cur'], sp['prev'], sp['cur'], sp['prev_t'], sp['cur_t'], sp['bias_prev'], sp['bias_cur']],
        out_specs=[sp['cur'], sp['stat']],
        out_shape=[jax.ShapeDtypeStruct((H, S, HEAD_PAD), BF16), jax.ShapeDtypeStruct((H, 1, S), F32)],
        compiler_params=_params(2))(q, k, k, vt, vt, bias, bias)


def band_bwd(q, k, kt, v, do, lse, delta, bias):
    H, S, _ = q.shape
    t = ATT_T
    sp = _band_specs(t, 1)

    def body(q_ref, kp_ref, kc_ref, ktp_ref, ktc_ref, vp_ref, vc_ref, do_ref, lse_ref, dl_ref, bp_ref, bc_ref,
             dq_ref, dk_ref, dv_ref, db_ref):
        i = pl.program_id(1)

        @pl.when(i == 0)
        def _():
            dk_ref[...] = jnp.zeros_like(dk_ref)
            dv_ref[...] = jnp.zeros_like(dv_ref)
            db_ref[...] = jnp.zeros_like(db_ref)

        qv, dov = q_ref[0], do_ref[0]
        dqt = jnp.zeros((HEAD_PAD, t), F32)
        windows = ((0, jnp.maximum(i - 1, 0), kp_ref, ktp_ref, vp_ref, bp_ref),
                   (1, i, kc_ref, ktc_ref, vc_ref, bc_ref))
        for w, blk, k_ref, kt_ref, v_ref, b_ref in windows:
            rows = pl.ds(pl.multiple_of(blk * t, t), t)
            pt = jnp.exp2(_nt(k_ref[0], qv) + b_ref[0, 0] - lse_ref[0])
            dv_ref[0, rows, :] += _nn(pt.astype(BF16), dov)
            ds = pt * (_nt(v_ref[0], dov) - dl_ref[0])
            db_ref[0, w] += ds
            dsb = ds.astype(BF16)
            dk_ref[0, rows, :] += _nn(dsb, qv)
            dqt = dqt + _nn(kt_ref[0], dsb)
        dq_ref[0] = dqt.T.astype(BF16)

    whole = pl.BlockSpec((1, S, HEAD_PAD), lambda h, i: (h, 0, 0))
    return _call(
        body, name="band_bwd", grid=(H, S // t),
        in_specs=[sp['cur'], sp['prev'], sp['cur'], sp['prev_t'], sp['cur_t'], sp['prev'], sp['cur'], sp['cur'],
                  sp['stat'], sp['stat'], sp['bias_prev'], sp['bias_cur']],
        out_specs=[sp['cur'], whole, whole, pl.BlockSpec((1, 2, t, t), lambda h, i: (h, 0, 0, 0))],
        out_shape=[jax.ShapeDtypeStruct((H, S, HEAD_PAD), BF16), jax.ShapeDtypeStruct((H, S, HEAD_PAD), F32),
                   jax.ShapeDtypeStruct((H, S, HEAD_PAD), F32), jax.ShapeDtypeStruct((H, 2, t, t), F32)],
        compiler_params=_params(2))(q, k, k, kt, kt, v, v, do, lse, delta, bias, bias)


def _compact(o_ref):
    return jnp.concatenate([o_ref[2 * p].astype(F32) + pltpu.roll(o_ref[2 * p + 1].astype(F32), 64, 1)
                            for p in range(4)], axis=1)


def merge_fwd(ob, oc, proj, ya, gate_b, wbr, w_out, x, post_g):
    S = x.shape[0]
    ts = ROW_TILE

    def body(ob_ref, oc_ref, zb_ref, zc_ref, ya_ref, gl_ref, gb_ref, wbr_ref, wo_ref, x_ref, pg_ref,
             xo_ref, yb_ref, yc_ref, mg_ref, out_ref):
        zb, zc = zb_ref[...], zc_ref[...]
        yb = (_compact(ob_ref) * (zb * _sigmoid(zb))).astype(BF16)
        yc = (_compact(oc_ref) * (zc * _sigmoid(zc))).astype(BF16)
        yb_ref[...] = yb
        yc_ref[...] = yc
        merged = jnp.zeros((ts, D_MODEL), F32)
        for n, y in enumerate((ya_ref[...], yb, yc)):
            cols = slice(n * D_MODEL, (n + 1) * D_MODEL)
            gate = _sigmoid(gl_ref[:, cols] + gb_ref[:, cols])
            merged = merged + gate * _nn(y, wbr_ref[n])
        mb = merged.astype(BF16)
        mg_ref[...] = mb
        out = _nn(mb, wo_ref[...])
        out_ref[...] = out
        normed, _ = _rms(out, pg_ref[...])
        xo_ref[...] = x_ref[...] + normed

    row = lambda w: pl.BlockSpec((ts, w), lambda i: (i, 0))
    col = lambda w, c: pl.BlockSpec((ts, w), lambda i: (i, c))
    full = lambda shape: pl.BlockSpec(shape, lambda i: (0,) * len(shape))
    return _call(
        body, name="merge_fwd", grid=(S // ts,),
        in_specs=[_hspec(ts), _hspec(ts), col(512, P_ZB // 512), col(512, P_ZC // 512), row(512),
                  col(3072, P_G // 3072), full((1, 3072)), full((3, BR_WIDTH, D_MODEL)),
                  full((D_MODEL, D_MODEL)), row(D_MODEL), full((1, D_MODEL))],
        out_specs=[row(D_MODEL), row(512), row(512), row(D_MODEL), row(D_MODEL)],
        out_shape=[jax.ShapeDtypeStruct((S, D_MODEL), F32), jax.ShapeDtypeStruct((S, 512), BF16),
                   jax.ShapeDtypeStruct((S, 512), BF16), jax.ShapeDtypeStruct((S, D_MODEL), BF16),
                   jax.ShapeDtypeStruct((S, D_MODEL), F32)],
        compiler_params=_params(1))(ob, oc, proj, proj, ya, proj, gate_b, wbr, w_out, x, post_g)


def gate_bwd(g, out, post_g, w_out, proj, gate_b, ya, yb, yc, wbr):
    S = g.shape[0]
    ts = ROW_TILE

    def body(g_ref, out_ref, pg_ref, wo_ref, gl_ref, gb_ref, ya_ref, yb_ref, yc_ref, wbr_ref,
             dp_ref, do_ref, dba_ref, dbb_ref, dbc_ref, dy_ref, ggb_ref, gp_ref):
        first = pl.program_id(0) == 0
        ov = out_ref[...]
        _, r = _rms(ov, pg_ref[...])
        dout, gp = _rms_bwd(ov, pg_ref[...], r, g_ref[...])
        db = dout.astype(BF16)
        do_ref[...] = db
        _accumulate(gp_ref, gp, first)
        dm = _nt(db, wo_ref[...])
        ggb = []
        for n, (y_ref, dbr_ref) in enumerate(((ya_ref, dba_ref), (yb_ref, dbb_ref), (yc_ref, dbc_ref))):
            cols = slice(n * D_MODEL, (n + 1) * D_MODEL)
            br = _nn(y_ref[...], wbr_ref[n])
            sg = _sigmoid(gl_ref[:, cols] + gb_ref[:, cols])
            dgl = dm * br * (sg * (1.0 - sg))
            dp_ref[:, cols] = dgl.astype(BF16)
            ggb.append(jnp.sum(dgl, axis=0, keepdims=True))
            dbr = (dm * sg).astype(BF16)
            dbr_ref[...] = dbr
            dy_ref[n] = _nt(dbr, wbr_ref[n])
        _accumulate(ggb_ref, jnp.concatenate(ggb, axis=1), first)

    row = lambda w: pl.BlockSpec((ts, w), lambda i: (i, 0))
    full = lambda shape: pl.BlockSpec(shape, lambda i: (0,) * len(shape))
    wide = jax.ShapeDtypeStruct((S, D_MODEL), BF16)
    return _call(
        body, name="gate_bwd", grid=(S // ts,),
        in_specs=[row(D_MODEL), row(D_MODEL), full((1, D_MODEL)), full((D_MODEL, D_MODEL)),
                  pl.BlockSpec((ts, 3072), lambda i: (i, P_G // 3072)), full((1, 3072)),
                  row(512), row(512), row(512), full((3, BR_WIDTH, D_MODEL))],
        out_specs=[pl.BlockSpec((ts, 3072), lambda i: (i, P_G // 3072)), row(D_MODEL), row(D_MODEL), row(D_MODEL),
                   row(D_MODEL), pl.BlockSpec((3, ts, 512), lambda i: (0, i, 0)), full((1, 3072)),
                   full((1, D_MODEL))],
        out_shape=[jax.ShapeDtypeStruct((S, P_W), BF16), wide, wide, wide, wide,
                   jax.ShapeDtypeStruct((3, S, 512), F32), jax.ShapeDtypeStruct((1, 3072), F32),
                   jax.ShapeDtypeStruct((1, D_MODEL), F32)],
        compiler_params=_params(1))(g, out, post_g, w_out, proj, gate_b, ya, yb, yc, wbr)


def ungate_bwd(dproj, dy, ob, oc, proj):
    S = proj.shape[0]
    ts = LIGHT_ROW_TILE

    def body(dp_in, dyb_ref, dyc_ref, ob_ref, oc_ref, zb_ref, zc_ref, dp_ref, dob_ref, doc_ref, dlb_ref, dlc_ref):
        del dp_in
        lane = lax.broadcasted_iota(jnp.int32, (ts, 128), 1)
        for n, (dy_ref, o_ref, z_ref, do_ref, dl_ref) in enumerate(
                ((dyb_ref, ob_ref, zb_ref, dob_ref, dlb_ref), (dyc_ref, oc_ref, zc_ref, doc_ref, dlc_ref))):
            zz = z_ref[...]
            dyv = dy_ref[0]
            sg = _sigmoid(zz)
            dp_ref[:, n * 512:(n + 1) * 512] = (dyv * _compact(o_ref) * (sg * (1.0 + zz * (1.0 - sg)))).astype(BF16)
            do_c = dyv * (zz * sg)
            for p in range(4):
                piece = do_c[:, p * 128:(p + 1) * 128]
                for h, d in ((2 * p, jnp.where(lane < 64, piece, 0.0)),
                             (2 * p + 1, jnp.where(lane < 64, pltpu.roll(piece, 64, 1), 0.0))):
                    do_ref[h] = d.astype(BF16)
                    dl_ref[h] = jnp.sum(d * o_ref[h].astype(F32), axis=-1, keepdims=True)

    col = lambda c: pl.BlockSpec((ts, 512), lambda i: (i, c))
    dysp = lambda n: pl.BlockSpec((1, ts, 512), lambda i: (n, i, 0))
    stat = pl.BlockSpec((N_HEADS, ts, 1), lambda i: (0, i, 0))
    hshape = jax.ShapeDtypeStruct((N_HEADS, S, HEAD_PAD), BF16)
    sshape = jax.ShapeDtypeStruct((N_HEADS, S, 1), F32)
    return _call(
        body, name="ungate_bwd", grid=(S // ts,),
        in_specs=[pl.BlockSpec(memory_space=pl.ANY), dysp(1), dysp(2), _hspec(ts), _hspec(ts),
                  col(P_ZB // 512), col(P_ZC // 512)],
        out_specs=[pl.BlockSpec((ts, 1024), lambda i: (i, P_ZB // 1024)), _hspec(ts), _hspec(ts), stat, stat],
        out_shape=[jax.ShapeDtypeStruct(dproj.shape, BF16), hshape, hshape, sshape, sshape],
        input_output_aliases={0: 0},
        compiler_params=_params(1))(dproj, dy, dy, ob, oc, proj, proj)


def loss_head(y, target):
    S, D = y.shape
    ts = LIGHT_ROW_TILE

    def body(y_ref, t_ref, dy_ref, sq_ref):
        d = y_ref[...] - t_ref[...]
        dy_ref[...] = d * (1.0 / D)
        _accumulate(sq_ref, jnp.sum(d * d, axis=0, keepdims=True), pl.program_id(0) == 0)

    row = pl.BlockSpec((ts, D), lambda i: (i, 0))
    return _call(
        body, name="loss_head", grid=(S // ts,), in_specs=[row, row],
        out_specs=[row, pl.BlockSpec((1, D), lambda i: (0, 0))],
        out_shape=[jax.ShapeDtypeStruct((S, D), F32), jax.ShapeDtypeStruct((1, D), F32)],
        compiler_params=_params(1))(y, target)


def _row_tile(rows, cols):
    for cand in (1024, 512, 256, 128, 64, 32, 16, 8):
        if rows % cand == 0 and cand * cols * 4 <= 1024 * 1024:
            return cand
    return rows


def adamw(w, grads, m, v):
    shape = w.shape
    cols = shape[-1]
    rows = int(np.prod(shape[:-1]))
    tr = _row_tile(rows, cols)
    n_g = len(grads)
    c1 = 1.0 - ADAM_B1 ** ADAM_STEP
    c2 = 1.0 - ADAM_B2 ** ADAM_STEP

    def body(*refs):
        w_ref, m_ref, v_ref = refs[:3]
        g_refs = refs[3:3 + n_g]
        go_ref, d_ref, mo_ref, vo_ref = refs[3 + n_g:]
        gv = g_refs[0][...]
        for g_ref in g_refs[1:]:
            gv = gv + g_ref[...]
        go_ref[...] = gv
        mn = ADAM_B1 * m_ref[...] + (1.0 - ADAM_B1) * gv
        vn = ADAM_B2 * v_ref[...] + (1.0 - ADAM_B2) * (gv * gv)
        mo_ref[...] = mn
        vo_ref[...] = vn
        d_ref[...] = -ADAM_LR * ((mn / c1) / (jnp.sqrt(vn / c2) + ADAM_EPS) + ADAM_WD * w_ref[...])

    blk = pl.BlockSpec((tr, cols), lambda i: (i, 0))
    sds = jax.ShapeDtypeStruct((rows, cols), F32)
    outs = _call(
        body, name="adamw", grid=(rows // tr,), in_specs=[blk] * (3 + n_g), out_specs=[blk] * 4,
        out_shape=[sds] * 4, compiler_params=_params(1))(
            *[a.reshape(rows, cols) for a in (w, m, v, *grads)])
    return [o.reshape(shape) for o in outs]


def add_lead(parts):
    n = parts.shape[0]
    shape = parts.shape[1:]
    cols = shape[-1]
    rows = int(np.prod(shape[:-1]))
    tr = _row_tile(rows, cols * n)

    def body(p_ref, o_ref):
        acc = p_ref[0].astype(F32)
        for s in range(1, n):
            acc = acc + p_ref[s].astype(F32)
        o_ref[...] = acc

    out = _call(
        body, name="add_lead", grid=(rows // tr,),
        in_specs=[pl.BlockSpec((n, tr, cols), lambda i: (0, i, 0))],
        out_specs=pl.BlockSpec((tr, cols), lambda i: (i, 0)),
        out_shape=jax.ShapeDtypeStruct((rows, cols), F32),
        compiler_params=_params(1))(parts.reshape(n, rows, cols))
    return out.reshape(shape)


ANY = pl.BlockSpec(memory_space=pl.ANY)


def _other_chips(x, y):
    return [(1 - x, y), (x, 1 - y), (1 - x, 1 - y)]


def chip_exchange(arrays, scatter, name):
    n = len(arrays)

    def body(*refs):
        _exchange_issue(refs[:n], refs[n:2 * n], refs[2 * n:], scatter, True)
        _exchange_issue(refs[:n], refs[n:2 * n], refs[2 * n:], scatter, False)

    return _call(
        body, name=name, in_specs=[ANY] * n, out_specs=[ANY] * n,
        out_shape=_exchange_out_shape(arrays, scatter), scratch_shapes=_exchange_sems(n))(*arrays)


def _exchange_out_shape(arrays, scatter):
    return [jax.ShapeDtypeStruct(a.shape if scatter else (4,) + a.shape, a.dtype) for a in arrays]


def _exchange_sems(n):
    if n == 0:
        return []
    return [pltpu.SemaphoreType.DMA((3 * n,)), pltpu.SemaphoreType.DMA((3 * n,)), pltpu.SemaphoreType.DMA((n,))]


def _exchange_issue(srcs, outs, sems, scatter, start):
    send_sems, recv_sems, local_sems = sems
    x, y, c = lax.axis_index("x"), lax.axis_index("y"), lax.axis_index("c")
    me = 2 * x + y
    for a in range(len(srcs)):
        local_src = srcs[a].at[me] if scatter else srcs[a]
        mine = pltpu.make_async_copy(local_src, outs[a].at[me], local_sems.at[a])
        sends = []
        for j, (px, py) in enumerate(_other_chips(x, y)):
            pair = dict(send_sem=send_sems.at[3 * a + j], recv_sem=recv_sems.at[3 * a + j],
                        device_id=(px, py, c), device_id_type=MESH)
            sends.append(pltpu.make_async_remote_copy(
                src_ref=srcs[a].at[2 * px + py] if scatter else srcs[a], dst_ref=outs[a].at[me], **pair))
            if not start:
                pltpu.make_async_remote_copy(src_ref=local_src, dst_ref=outs[a].at[2 * px + py], **pair).wait_recv()
        if start:
            mine.start()
            for cp in sends:
                cp.start()
        else:
            for cp in sends:
                cp.wait_send()
            mine.wait()


def sibling_exchange(arrays):
    n = len(arrays)

    def body(*refs):
        srcs, outs = refs[:n], refs[n:2 * n]
        send_sems, recv_sems = refs[2 * n:]
        x, y, c = lax.axis_index("x"), lax.axis_index("y"), lax.axis_index("c")
        copies = [pltpu.make_async_remote_copy(src_ref=srcs[a], dst_ref=outs[a], send_sem=send_sems.at[a],
                                               recv_sem=recv_sems.at[a], device_id=(x, y, 1 - c), device_id_type=MESH)
                  for a in range(n)]
        for cp in copies:
            cp.start()
        for cp in copies:
            cp.wait()

    return _call(
        body, name="sibling_exchange", in_specs=[ANY] * n, out_specs=[ANY] * n,
        out_shape=[jax.ShapeDtypeStruct(a.shape, a.dtype) for a in arrays],
        scratch_shapes=[pltpu.SemaphoreType.DMA((n,)), pltpu.SemaphoreType.DMA((n,))])(*arrays)


def _perm_from_shards(sh):
    rows = sh.shape[1]
    pieces, pos = [], 0
    for lo, hi, plo in sorted(NAT_SEGS, key=lambda s: s[2]):
        if plo > pos:
            pieces.append(jnp.zeros((rows, plo - pos), sh.dtype))
            pos = plo
        c = lo
        while c < hi:
            kk = c // SHARD_COLS
            e = min(hi, (kk + 1) * SHARD_COLS)
            pieces.append(sh[kk][:, c - kk * SHARD_COLS:e - kk * SHARD_COLS])
            c = e
        pos += hi - lo
    if pos < P_W:
        pieces.append(jnp.zeros((rows, P_W - pos), sh.dtype))
    return jnp.concatenate(pieces, axis=1)


def _shards_from_perm(p):
    out = []
    for kk in range(4):
        lo_k, hi_k = kk * SHARD_COLS, (kk + 1) * SHARD_COLS
        pieces = []
        for lo, hi, plo in NAT_SEGS:
            a, b = max(lo, lo_k), min(hi, hi_k)
            if a < b:
                pieces.append(p[:, plo + (a - lo):plo + (b - lo)])
        out.append(jnp.concatenate(pieces, axis=1))
    return jnp.stack(out)


def _split4(a, axis):
    shape = a.shape
    a = a.reshape(shape[:axis] + (4, shape[axis] // 4) + shape[axis + 1:])
    return jnp.moveaxis(a, axis, 0)


def _join4(a, axis):
    a = jnp.moveaxis(a, 0, axis)
    shape = a.shape
    return a.reshape(shape[:axis] + (4 * shape[axis + 1],) + shape[axis + 2:])


def _pad_heads(w, per_head, lo, hi):
    r = w.shape[0]
    wh = w.reshape(r, N_HEADS, per_head)[:, :, lo:hi]
    return jnp.pad(wh, ((0, 0), (0, 0), (0, HEAD_PAD - (hi - lo)))).reshape(r, N_HEADS * HEAD_PAD)


def _rope_table(S):
    half = MLA_ROPE // 2
    inv = ROPE_BASE ** (-jnp.arange(half, dtype=F32) / half)
    ang = jnp.arange(S).astype(F32)[:, None] * inv[None, :]
    cos, sin = jnp.cos(ang), jnp.sin(ang)
    z = lambda n: jnp.zeros((S, n), F32)
    c = jnp.concatenate([jnp.ones((S, MLA_NOPE), F32), cos, cos, z(32)], axis=1)
    a = jnp.concatenate([z(MLA_NOPE), -sin, z(48)], axis=1)
    b = jnp.concatenate([z(MLA_NOPE + half), sin, z(32)], axis=1)
    return jnp.stack([c, a, b])


def _band_onehot():
    t = ATT_T
    m = np.arange(2 * t)
    d = np.where(m < t, m, m - 2 * t)
    idx = np.stack([np.clip(off + d, -REL_CLIP, REL_CLIP) + REL_CLIP for off in (t, 0)])
    return (idx[:, :, None] == np.arange(2 * REL_CLIP + 1)[None, None, :]).astype(np.float32)


def bias_expand(diag):
    t = ATT_T

    def body(d_ref, o_ref):
        kc = lax.broadcasted_iota(jnp.int32, (t, t), 0) >> CHUNK_SHIFT
        qc = lax.broadcasted_iota(jnp.int32, (t, t), 1) >> CHUNK_SHIFT
        for w, visible in ((0, kc >= qc), (1, kc <= qc)):
            rows = jnp.broadcast_to(d_ref[0, w:w + 1, :], (t, 2 * t))
            skew = pltpu.roll(rows, 0, 1, stride=1, stride_axis=0)[:, :t]
            o_ref[0, 2 * w] = jnp.where(visible, skew * LOG2E, NEG_INF)
        o_ref[0, 1] = jnp.full((t, t), NEG_INF, F32)

    return _call(
        body, name="bias_expand", grid=(N_HEADS,),
        in_specs=[pl.BlockSpec((1, 2, 2 * t), lambda h: (h, 0, 0))],
        out_specs=pl.BlockSpec((1, 3, t, t), lambda h: (h, 0, 0, 0)),
        out_shape=jax.ShapeDtypeStruct((N_HEADS, 3, t, t), F32),
        compiler_params=_params(1))(diag)


def bias_fold(dtiles):
    t = ATT_T

    def body(d_ref, o_ref):
        pad = jnp.zeros((8, t), F32)
        for w in range(2):
            acc = jnp.concatenate([d_ref[0, w, 0:8, :], pad], axis=1)
            for g in range(1, t // 8):
                grp = jnp.concatenate([d_ref[0, w, 8 * g:8 * g + 8, :], pad], axis=1)
                acc = acc + pltpu.roll(grp, 2 * t - 8 * g, 1)
            out = acc[0:1, :]
            for s in range(1, 8):
                out = out + pltpu.roll(acc, 2 * t - s, 1)[s:s + 1, :]
            o_ref[0, w:w + 1, :] = out

    return _call(
        body, name="bias_fold", grid=(N_HEADS,),
        in_specs=[pl.BlockSpec((1, 2, t, t), lambda h: (h, 0, 0, 0))],
        out_specs=pl.BlockSpec((1, 2, 2 * t), lambda h: (h, 0, 0)),
        out_shape=jax.ShapeDtypeStruct((N_HEADS, 2, 2 * t), F32),
        compiler_params=_params(1))(dtiles)


def _bias_tiles(table):
    diag = jnp.einsum('hr,wdr->hwd', table, jnp.asarray(_band_onehot()), precision=lax.Precision.HIGHEST)
    return bias_expand(diag)


def _bias_tiles_grad(dtiles):
    return jnp.einsum('hwd,wdr->hr', bias_fold(dtiles), jnp.asarray(_band_onehot()),
                      precision=lax.Precision.HIGHEST)


def _layer_consts(lw):
    tri = np.tril(np.ones((SGU_BLOCK, SGU_BLOCK), np.float32))
    ws = (lw['sgu_w'] * tri).astype(BF16)
    return dict(
        ws=ws, ws_t=jnp.swapaxes(ws, 1, 2), sgu_bias=jnp.repeat(lw['sgu_b'].T, CA_HEAD_DIM, axis=1),
        bias=_bias_tiles(lw['ca_rel_bias']),
        wq=_pad_heads(lw['mla_w_uq'], MLA_QK, 0, MLA_QK),
        wk=_pad_heads(lw['mla_w_ukv'], MLA_NOPE + MLA_V, 0, MLA_NOPE),
        wv=_pad_heads(lw['mla_w_ukv'], MLA_NOPE + MLA_V, MLA_NOPE, MLA_NOPE + MLA_V),
        gate_b=lw['gate_b'].reshape(1, 3 * D_MODEL),
        pre_g=lw['pre_g'][None], post_g=lw['post_g'][None], ln_g=lw['sgu_ln_g'][None], ln_b=lw['sgu_ln_b'][None],
        qg=lw['mla_q_norm_g'][None], kvg=lw['mla_kv_norm_g'][None])


def _layer_fwd(x, lw, k, tab, next_shards):
    proj, xn = norm_matmul(x, k['pre_g'], lw['w_in'])
    ya = sgu_fwd(proj, k['ln_g'], k['ln_b'], k['ws'], k['sgu_bias'])
    qb, kb, vb, qc, kc, vc, kbt, vbt, kct, vct, cq, ckv = prep_fwd(proj, tab, k['qg'], k['kvg'], k['wq'], k['wk'],
                                                                   k['wv'])
    ob, lse_b, gathered = mla_fwd(qb, kb, vbt, next_shards)
    oc, lse_c = band_fwd(qc, kc, vct, k['bias'])
    x_new, yb, yc, merged, out = merge_fwd(ob, oc, proj, ya, k['gate_b'], lw['w_branch'], lw['w_out'], x,
                                           k['post_g'])
    saved = dict(x=x, proj=proj, xn=xn, ya=ya, yb=yb, yc=yc, qb=qb, kb=kb, vb=vb, qc=qc, kc=kc, vc=vc, kbt=kbt, kct=kct,
                 cq=cq, ckv=ckv, ob=ob, oc=oc, lse_b=lse_b, lse_c=lse_c, merged=merged, out=out)
    return x_new, saved, gathered


def _layer_bwd(g, s, lw, k, tab, pending_parts, scatter_own):
    S = g.shape[0]
    H = N_HEADS
    dproj, dout, dba, dbb, dbc, dy, g_gate_b, g_post = gate_bwd(
        g, s['out'], k['post_g'], lw['w_out'], s['proj'], k['gate_b'], s['ya'], s['yb'], s['yc'], lw['w_branch'])
    g_w_out = matmul_tn(s['merged'], dout, 512)
    g_w_branch = jnp.stack([matmul_tn(y, d, 512) for y, d in ((s['ya'], dba), (s['yb'], dbb), (s['yc'], dbc))])
    dproj, dob, doc, dl_b, dl_c = ungate_bwd(dproj, dy, s['ob'], s['oc'], s['proj'])
    row = lambda a: a.reshape(H, 1, S)
    dqb, dkb, dvb, landed = mla_bwd(s['qb'], s['kb'], s['kbt'], s['vb'], dob, s['lse_b'], row(dl_b), pending_parts)
    dqc, dkc, dvc, dbias = band_bwd(s['qc'], s['kc'], s['kct'], s['vc'], doc, s['lse_c'], row(dl_c), k['bias'])
    dproj, dqf, dkf, dvf, g_qg, g_kvg = prep_bwd(dproj, dqb, dkb, dvb, dqc, dkc, dvc, s['proj'], tab,
                                                 k['qg'], k['kvg'], k['wq'], k['wk'], k['wv'])
    g_wq = matmul_tn(s['cq'], dqf, 512).reshape(MLA_Q_RANK, H, HEAD_PAD)[:, :, :MLA_QK]
    g_wk = matmul_tn(s['ckv'], dkf, 512).reshape(MLA_KV_RANK, H, HEAD_PAD)[:, :, :MLA_NOPE]
    g_wv = matmul_tn(s['ckv'], dvf, 512).reshape(MLA_KV_RANK, H, HEAD_PAD)[:, :, :MLA_V]
    dproj, g_ln_g, g_ln_b, g_ws, g_sgu_bias = sgu_bwd(dproj, dy, s['proj'], k['ln_g'], k['ln_b'], k['ws'],
                                                      k['ws_t'], k['sgu_bias'])
    g_w_in = matmul_tn(s['xn'], dproj, MM_TN)
    sharded = _sharded_parts(dict(
        w_in=g_w_in, mla_w_uq=g_wq.reshape(MLA_Q_RANK, H * MLA_QK),
        mla_w_ukv=jnp.concatenate([g_wk, g_wv], axis=2).reshape(MLA_KV_RANK, H * (MLA_NOPE + MLA_V)),
        w_branch=g_w_branch, gate_b=g_gate_b.reshape(N_BRANCH, D_MODEL), w_out=g_w_out))
    dx, g_pre, own_landed = proj_bwd_x(dproj, lw['w_in'], s['x'], k['pre_g'], g, sharded if scatter_own else ())
    tri = np.tril(np.ones((SGU_BLOCK, SGU_BLOCK), np.float32))
    small = _small_pack(dict(
        pre_g=g_pre[0], post_g=g_post[0], sgu_ln_g=g_ln_g[0], sgu_ln_b=g_ln_b[0],
        sgu_w=g_ws * tri, sgu_b=jnp.sum(g_sgu_bias.reshape(SGU_BLOCK, 8, CA_HEAD_DIM), axis=2).T,
        mla_q_norm_g=g_qg[0], mla_kv_norm_g=g_kvg[0], ca_rel_bias=_bias_tiles_grad(dbias)))
    return dx, (own_landed if scatter_own else sharded), small, landed


BF16_PARTS = ('w_in', 'mla_w_uq', 'mla_w_ukv', 'w_branch', 'w_out')


def _weight_shards(w, l):
    return [w[n][l].astype(BF16) if n in BF16_PARTS else w[n][l] for n in SHARDED]


def _full_weights(gathered, small):
    lw = {n: _join4(a, SHARD_AXIS[n]) for n, a in zip(SHARDED, gathered) if n != 'w_in'}
    lw['w_in'] = _perm_from_shards(gathered[0])
    lw.update(small)
    return lw


def _small_pack(grads):
    flat = jnp.concatenate([grads[n].reshape(-1) for n in SMALL])
    quarter = -(-flat.size // (4 * 1024)) * 1024
    return jnp.pad(flat, (0, 4 * quarter - flat.size)).reshape(4, quarter // 128, 128)


def _sharded_parts(grads):
    parts = [_shards_from_perm(grads['w_in'])]
    parts += [_split4(grads[n], SHARD_AXIS[n]) for n in SHARDED if n != 'w_in']
    return [p.astype(BF16) if n in BF16_PARTS else p for n, p in zip(SHARDED, parts)]


def train_step_local(x, target, w):
    S = x.shape[0]
    depth = w['w_in'].shape[0]
    tab = _rope_table(S)
    gathered = chip_exchange(_weight_shards(w, 0), False, "gather_weights")
    layer_w, consts, saved = [], [], []
    for l in range(depth):
        lw = _full_weights(gathered, {n: w[n][l] for n in SMALL})
        k = _layer_consts(lw)
        x, s, gathered = _layer_fwd(x, lw, k, tab, _weight_shards(w, l + 1) if l + 1 < depth else ())
        layer_w.append(lw)
        consts.append(k)
        saved.append(s)
    g, sq = loss_head(x, target)
    mine = [None] * depth
    pending = ()
    for l in reversed(range(depth)):
        g, sharded, small, landed = _layer_bwd(g, saved[l], layer_w[l], consts[l], tab, pending, l == 0)
        if pending:
            mine[l + 1] = [add_lead(p) for p in landed]
        pending = list(sharded) + [small]
    mine[0] = [add_lead(p) for p in pending[:-1] + list(chip_exchange(pending[-1:], True, "scatter_small"))]
    n_parts = len(mine[0])
    theirs = sibling_exchange([p for layer in mine for p in layer])
    return sq, g, [(mine[l], theirs[l * n_parts:(l + 1) * n_parts]) for l in range(depth)]


def kernel(x, w_in, pre_g, post_g, sgu_ln_g, sgu_ln_b, sgu_w, sgu_b, mla_q_norm_g, mla_kv_norm_g, mla_w_uq, mla_w_ukv, ca_rel_bias, w_branch, gate_b, w_out, loss_target, m_w_in, m_pre_g, m_post_g, m_sgu_ln_g, m_sgu_ln_b, m_sgu_w, m_sgu_b, m_mla_q_norm_g, m_mla_kv_norm_g, m_mla_w_uq, m_mla_w_ukv, m_ca_rel_bias, m_w_branch, m_gate_b, m_w_out, v_w_in, v_pre_g, v_post_g, v_sgu_ln_g, v_sgu_ln_b, v_sgu_w, v_sgu_b, v_mla_q_norm_g, v_mla_kv_norm_g, v_mla_w_uq, v_mla_w_ukv, v_ca_rel_bias, v_w_branch, v_gate_b, v_w_out):
    w = dict(w_in=w_in, pre_g=pre_g, post_g=post_g, sgu_ln_g=sgu_ln_g, sgu_ln_b=sgu_ln_b, sgu_w=sgu_w, sgu_b=sgu_b,
             mla_q_norm_g=mla_q_norm_g, mla_kv_norm_g=mla_kv_norm_g, mla_w_uq=mla_w_uq, mla_w_ukv=mla_w_ukv,
             ca_rel_bias=ca_rel_bias, w_branch=w_branch, gate_b=gate_b, w_out=w_out)
    m = dict(w_in=m_w_in, pre_g=m_pre_g, post_g=m_post_g, sgu_ln_g=m_sgu_ln_g, sgu_ln_b=m_sgu_ln_b, sgu_w=m_sgu_w,
             sgu_b=m_sgu_b, mla_q_norm_g=m_mla_q_norm_g, mla_kv_norm_g=m_mla_kv_norm_g, mla_w_uq=m_mla_w_uq,
             mla_w_ukv=m_mla_w_ukv, ca_rel_bias=m_ca_rel_bias, w_branch=m_w_branch, gate_b=m_gate_b, w_out=m_w_out)
    v = dict(w_in=v_w_in, pre_g=v_pre_g, post_g=v_post_g, sgu_ln_g=v_sgu_ln_g, sgu_ln_b=v_sgu_ln_b, sgu_w=v_sgu_w,
             sgu_b=v_sgu_b, mla_q_norm_g=v_mla_q_norm_g, mla_kv_norm_g=v_mla_kv_norm_g, mla_w_uq=v_mla_w_uq,
             mla_w_ukv=v_mla_w_ukv, ca_rel_bias=v_ca_rel_bias, w_branch=v_w_branch, gate_b=v_gate_b, w_out=v_w_out)
    depth = w_in.shape[0]
    sq, grad_x, reduced = train_step_local(x[0], loss_target[0], w)
    loss = lax.psum(0.5 * jnp.sum(sq) / D_MODEL, ("x", "y", "c"))

    out = {}
    for a, n in enumerate(SHARDED):
        mine = jnp.stack([reduced[l][0][a] for l in range(depth)])
        theirs = jnp.stack([reduced[l][1][a] for l in range(depth)])
        out[n] = adamw(w[n], [mine, theirs], m[n], v[n])
    small = jnp.stack([jnp.stack([reduced[l][0][-1] for l in range(depth)]),
                       jnp.stack([reduced[l][1][-1] for l in range(depth)])])
    quarter = add_lead(small)
    full = chip_exchange([quarter], False, "gather_small")[0]
    full = jnp.moveaxis(full, 0, 1).reshape(depth, -1)
    off = 0
    for n in SMALL:
        size = int(np.prod(w[n].shape[1:]))
        out[n] = adamw(w[n], [full[:, off:off + size].reshape(w[n].shape)], m[n], v[n])
        off += size
    return (loss, grad_x[None], *[out[n][0] for n in WEIGHTS], *[out[n][1] for n in WEIGHTS],
            *[out[n][2] for n in WEIGHTS], *[out[n][3] for n in WEIGHTS])
```

```python
import numpy as np
import jax
import jax.numpy as jnp
from jax import lax
from jax.experimental import pallas as pl
from jax.experimental.pallas import tpu as pltpu

F32 = jnp.float32
BF16 = jnp.bfloat16
MESH = pl.DeviceIdType.MESH

EPS = 1e-6
NEG_INF = -1e30
D_MODEL = 1024
BR_WIDTH = 512
N_BRANCH = 3
N_HEADS = 8
HEAD_PAD = 128
CHUNK_SHIFT = 6
SGU_BLOCK = 128
MLA_NOPE, MLA_ROPE, MLA_V = 64, 32, 64
MLA_QK = MLA_NOPE + MLA_ROPE
MLA_Q_RANK, MLA_KV_RANK = 256, 128
CA_HEAD_DIM = 64
REL_CLIP = 128
ROPE_BASE = 10000.0
D_IN = 7584

ADAM_LR, ADAM_B1, ADAM_B2, ADAM_EPS, ADAM_WD, ADAM_STEP = 0.001, 0.9, 0.999, 1e-08, 0.01, 10

P_QC, P_KC, P_VC, P_QD, P_KVD, P_KR, P_ZB, P_ZC, P_G, P_U, P_V, P_ZA, P_W = (
    0, 512, 1024, 1536, 1792, 1920, 2048, 2560, 3072, 6144, 6656, 7168, 7680)
NAT_SEGS = [(0, 1536, P_U), (1536, 1920, P_QD), (1920, 1952, P_KR + MLA_NOPE), (1952, 2464, P_ZB),
            (2464, 4000, P_QC), (4000, 4512, P_ZC), (4512, 7584, P_G)]
SHARD_COLS = D_IN // 4

VMEM_LIMIT = 48 * 1024 * 1024
ATT_T = 512
MLA_HEADS_PER_STEP = 2
MLA_FWD_HEADS_PER_STEP = 4
ROW_TILE = 256
LIGHT_ROW_TILE = 512
MM_TM = 512
MM_TN = 1536
LOG2E = 1.4426950408889634
MLA_SCALE = MLA_QK ** -0.5
CA_SCALE = CA_HEAD_DIM ** -0.5

WEIGHTS = ['w_in', 'pre_g', 'post_g', 'sgu_ln_g', 'sgu_ln_b', 'sgu_w', 'sgu_b', 'mla_q_norm_g',
           'mla_kv_norm_g', 'mla_w_uq', 'mla_w_ukv', 'ca_rel_bias', 'w_branch', 'gate_b', 'w_out']
SHARDED = ['w_in', 'mla_w_uq', 'mla_w_ukv', 'w_branch', 'gate_b', 'w_out']
SMALL = ['pre_g', 'post_g', 'sgu_ln_g', 'sgu_ln_b', 'sgu_w', 'sgu_b', 'mla_q_norm_g',
         'mla_kv_norm_g', 'ca_rel_bias']
SHARD_AXIS = {'w_in': 1, 'mla_w_uq': 1, 'mla_w_ukv': 1, 'w_branch': 2, 'gate_b': 1, 'w_out': 0}


def _call(body, **kw):
    return pl.pallas_call(body, **kw)


def _params(n_axes):
    return pltpu.CompilerParams(dimension_semantics=("arbitrary",) * n_axes,
                                vmem_limit_bytes=VMEM_LIMIT)


def _nt(a, b):
    return lax.dot_general(a, b, (((1,), (1,)), ((), ())), preferred_element_type=F32)


def _nn(a, b):
    return jnp.dot(a, b, preferred_element_type=F32)


def _tn(a, b):
    return lax.dot_general(a, b, (((0,), (0,)), ((), ())), preferred_element_type=F32)


def _rms(xv, g):
    r = lax.rsqrt(jnp.mean(xv * xv, axis=-1, keepdims=True) + EPS)
    return xv * r * g, r


def _rms_bwd(xv, g, r, dy):
    gy = dy * g
    dx = r * gy - xv * (r * r * r) * jnp.mean(xv * gy, axis=-1, keepdims=True)
    dg = jnp.sum(dy * (xv * r), axis=0, keepdims=True)
    return dx, dg


def _sigmoid(z):
    return 1.0 / (1.0 + jnp.exp(-z))


def _rope(xv, c, a, b):
    return xv * c + pltpu.roll(xv, 112, 1) * a + pltpu.roll(xv, 16, 1) * b


def _accumulate(ref, val, first):
    @pl.when(first)
    def _():
        ref[...] = val

    @pl.when(jnp.logical_not(first))
    def _():
        ref[...] += val


def norm_matmul(x, g, w):
    S, D = x.shape
    N = w.shape[1]
    tm, tn = min(S, 2 * MM_TM), MM_TN

    def body(x_ref, g_ref, w_ref, o_ref, xn_ref):
        @pl.when(pl.program_id(1) == 0)
        def _():
            y, _ = _rms(x_ref[...], g_ref[...])
            xn_ref[...] = y.astype(BF16)

        o_ref[...] = _nn(xn_ref[...], w_ref[...])

    return _call(
        body, name="norm_matmul", grid=(S // tm, N // tn),
        in_specs=[pl.BlockSpec((tm, D), lambda i, j: (i, 0)),
                  pl.BlockSpec((1, D), lambda i, j: (0, 0)),
                  pl.BlockSpec((D, tn), lambda i, j: (0, j))],
        out_specs=[pl.BlockSpec((tm, tn), lambda i, j: (i, j)),
                   pl.BlockSpec((tm, D), lambda i, j: (i, 0))],
        out_shape=[jax.ShapeDtypeStruct((S, N), F32), jax.ShapeDtypeStruct((S, D), BF16)],
        compiler_params=_params(2))(x, g, w)


def proj_bwd_x(dproj, w, x, g, resid, exchange=()):
    S, N = dproj.shape
    D = x.shape[1]
    tm, tk = min(S, MM_TM), MM_TN
    nk, n_ex = N // tk, len(exchange)

    def body(dp_ref, w_ref, x_ref, g_ref, r_ref, *rest):
        ex_src, (dx_ref, dg_ref), ex_out, (acc_ref,), ex_sems = _split_refs(
            rest, (n_ex, 2, n_ex, 1, 3 if n_ex else 0))
        i, k = pl.program_id(0), pl.program_id(1)

        if n_ex:
            @pl.when(jnp.logical_and(i == 0, k == 0))
            def _():
                _exchange_issue(ex_src, ex_out, ex_sems, True, True)

        @pl.when(k == 0)
        def _():
            acc_ref[...] = jnp.zeros_like(acc_ref)

        acc_ref[...] += _nt(dp_ref[...].astype(BF16), w_ref[...])

        @pl.when(k == nk - 1)
        def _():
            xv = x_ref[...]
            _, r = _rms(xv, g_ref[...])
            dx, dg = _rms_bwd(xv, g_ref[...], r, acc_ref[...])
            dx_ref[...] = dx + r_ref[...]
            _accumulate(dg_ref, dg, i == 0)

        if n_ex:
            @pl.when(jnp.logical_and(i == S // tm - 1, k == nk - 1))
            def _():
                _exchange_issue(ex_src, ex_out, ex_sems, True, False)

    outs = _call(
        body, name="proj_bwd_x_scatter" if n_ex else "proj_bwd_x", grid=(S // tm, nk),
        in_specs=[pl.BlockSpec((tm, tk), lambda i, k: (i, k)),
                  pl.BlockSpec((D, tk), lambda i, k: (0, k)),
                  pl.BlockSpec((tm, D), lambda i, k: (i, 0)),
                  pl.BlockSpec((1, D), lambda i, k: (0, 0)),
                  pl.BlockSpec((tm, D), lambda i, k: (i, 0))] + [ANY] * n_ex,
        out_specs=[pl.BlockSpec((tm, D), lambda i, k: (i, 0)),
                   pl.BlockSpec((1, D), lambda i, k: (0, 0))] + [ANY] * n_ex,
        out_shape=[jax.ShapeDtypeStruct((S, D), F32), jax.ShapeDtypeStruct((1, D), F32)] +
        _exchange_out_shape(exchange, True),
        scratch_shapes=[pltpu.VMEM((tm, D), F32)] + _exchange_sems(n_ex),
        compiler_params=_params(2))(dproj, w, x, g, resid, *exchange)
    return outs[0], outs[1], outs[2:]


def matmul_tn(a, b, tn):
    S, M = a.shape
    N = b.shape[1]
    tk = min(S, 2 * MM_TM)

    def body(a_ref, b_ref, o_ref):
        @pl.when(pl.program_id(1) == 0)
        def _():
            o_ref[...] = jnp.zeros_like(o_ref)

        o_ref[...] += _tn(a_ref[...].astype(BF16), b_ref[...].astype(BF16))

    return _call(
        body, name="matmul_tn", grid=(N // tn, S // tk),
        in_specs=[pl.BlockSpec((tk, M), lambda j, k: (k, 0)),
                  pl.BlockSpec((tk, tn), lambda j, k: (k, j))],
        out_specs=pl.BlockSpec((M, tn), lambda j, k: (0, j)),
        out_shape=jax.ShapeDtypeStruct((M, N), F32),
        compiler_params=_params(2))(a, b)


def _sgu_block(vv, g, b, ws_ref, lane):
    mu = jnp.mean(vv, axis=-1, keepdims=True)
    xc = vv - mu
    r = lax.rsqrt(jnp.mean(xc * xc, axis=-1, keepdims=True) + EPS)
    xhat = xc * r
    vln = (xhat * g + b).astype(BF16)
    pieces = []
    for p in range(4):
        vp = vln[:, p * 128:(p + 1) * 128]
        pieces.append(jnp.where(lane < 64, _nn(ws_ref[2 * p], vp), _nn(ws_ref[2 * p + 1], vp)))
    return xhat, r, vln, jnp.concatenate(pieces, axis=1)


def sgu_fwd(proj, ln_g, ln_b, ws, bias_full):
    S = proj.shape[0]
    ts = LIGHT_ROW_TILE

    def body(u_ref, v_ref, z_ref, g_ref, b_ref, ws_ref, bf_ref, y_ref):
        lane = lax.broadcasted_iota(jnp.int32, (SGU_BLOCK, 128), 1)
        for blk in range(ts // SGU_BLOCK):
            rows = slice(blk * SGU_BLOCK, (blk + 1) * SGU_BLOCK)
            _, _, _, mixed = _sgu_block(v_ref[rows, :], g_ref[...], b_ref[...], ws_ref, lane)
            mixed = mixed + bf_ref[...]
            zz = z_ref[rows, :]
            y_ref[rows, :] = (u_ref[rows, :] * mixed * (zz * _sigmoid(zz))).astype(BF16)

    col = lambda c: pl.BlockSpec((ts, BR_WIDTH), lambda i: (i, c))
    full = lambda shape: pl.BlockSpec(shape, lambda i: (0,) * len(shape))
    return _call(
        body, name="sgu_fwd", grid=(S // ts,),
        in_specs=[col(P_U // 512), col(P_V // 512), col(P_ZA // 512),
                  full((1, BR_WIDTH)), full((1, BR_WIDTH)), full((8, 128, 128)), full((128, BR_WIDTH))],
        out_specs=pl.BlockSpec((ts, BR_WIDTH), lambda i: (i, 0)),
        out_shape=jax.ShapeDtypeStruct((S, BR_WIDTH), BF16),
        compiler_params=_params(1))(proj, proj, proj, ln_g, ln_b, ws, bias_full)


def sgu_bwd(dproj, dy, proj, ln_g, ln_b, ws, ws_t, bias_full):
    S = proj.shape[0]
    ts = LIGHT_ROW_TILE

    def body(dp_in, dy_ref, u_ref, v_ref, z_ref, g_ref, b_ref, ws_ref, wst_ref, bf_ref,
             dp_ref, gg_ref, gb_ref, gws_ref, gbf_ref):
        del dp_in
        first = pl.program_id(0) == 0

        @pl.when(first)
        def _():
            gg_ref[...] = jnp.zeros_like(gg_ref)
            gb_ref[...] = jnp.zeros_like(gb_ref)
            gws_ref[...] = jnp.zeros_like(gws_ref)
            gbf_ref[...] = jnp.zeros_like(gbf_ref)

        lane = lax.broadcasted_iota(jnp.int32, (SGU_BLOCK, 128), 1)
        for blk in range(ts // SGU_BLOCK):
            rows = slice(blk * SGU_BLOCK, (blk + 1) * SGU_BLOCK)
            g = g_ref[...]
            xhat, r, vln, mixed = _sgu_block(v_ref[rows, :], g, b_ref[...], ws_ref, lane)
            mixed = mixed + bf_ref[...]
            zz = z_ref[rows, :]
            uu = u_ref[rows, :]
            dyv = dy_ref[0, rows, :]
            sg = _sigmoid(zz)
            sil = zz * sg
            dmixed = dyv * uu * sil
            dp_ref[rows, 0:512] = (dyv * mixed * sil).astype(BF16)
            dp_ref[rows, 1024:1536] = (dyv * uu * mixed * (sg * (1.0 + zz * (1.0 - sg)))).astype(BF16)
            gbf_ref[...] += dmixed
            dmb = dmixed.astype(BF16)
            pieces = []
            for p in range(4):
                dmp = dmb[:, p * 128:(p + 1) * 128]
                vp = vln[:, p * 128:(p + 1) * 128]
                pieces.append(jnp.where(lane < 64, _nn(wst_ref[2 * p], dmp), _nn(wst_ref[2 * p + 1], dmp)))
                zero = jnp.zeros_like(dmp)
                gws_ref[2 * p] += _nt(jnp.where(lane < 64, dmp, zero), vp)
                gws_ref[2 * p + 1] += _nt(jnp.where(lane >= 64, dmp, zero), vp)
            dvln = jnp.concatenate(pieces, axis=1)
            dxh = dvln * g
            dp_ref[rows, 512:1024] = (r * (dxh - jnp.mean(dxh, axis=-1, keepdims=True)
                                           - xhat * jnp.mean(dxh * xhat, axis=-1, keepdims=True))).astype(BF16)
            gg_ref[...] += jnp.sum(dvln * xhat, axis=0, keepdims=True)
            gb_ref[...] += jnp.sum(dvln, axis=0, keepdims=True)

    col = lambda c: pl.BlockSpec((ts, BR_WIDTH), lambda i: (i, c))
    full = lambda shape: pl.BlockSpec(shape, lambda i: (0,) * len(shape))
    return _call(
        body, name="sgu_bwd", grid=(S // ts,),
        in_specs=[pl.BlockSpec(memory_space=pl.ANY),
                  pl.BlockSpec((1, ts, BR_WIDTH), lambda i: (0, i, 0)),
                  col(P_U // 512), col(P_V // 512), col(P_ZA // 512),
                  full((1, BR_WIDTH)), full((1, BR_WIDTH)), full((8, 128, 128)), full((8, 128, 128)),
                  full((128, BR_WIDTH))],
        out_specs=[pl.BlockSpec((ts, 1536), lambda i: (i, P_U // 1536)),
                   full((1, BR_WIDTH)), full((1, BR_WIDTH)), full((8, 128, 128)), full((128, BR_WIDTH))],
        out_shape=[jax.ShapeDtypeStruct(dproj.shape, BF16),
                   jax.ShapeDtypeStruct((1, BR_WIDTH), F32), jax.ShapeDtypeStruct((1, BR_WIDTH), F32),
                   jax.ShapeDtypeStruct((8, 128, 128), F32), jax.ShapeDtypeStruct((128, BR_WIDTH), F32)],
        input_output_aliases={0: 0},
        compiler_params=_params(1))(dproj, dy, proj, proj, proj, ln_g, ln_b, ws, ws_t, bias_full)


def _hspec(ts):
    return pl.BlockSpec((N_HEADS, ts, HEAD_PAD), lambda i: (0, i, 0))


def prep_fwd(proj, tab, qg, kvg, wq, wk, wv):
    S = proj.shape[0]
    ts = LIGHT_ROW_TILE

    def body(qc_ref, kc_ref, vc_ref, qd_ref, kvd_ref, kr_ref, tab_ref, qg_ref, kvg_ref,
             wq_ref, wk_ref, wv_ref, qb, kb, vb, qc, kc, vc, kbt, vbt, kct, vct, cq_o, ckv_o):
        c, a, b = tab_ref[0], tab_ref[1], tab_ref[2]
        cq, _ = _rms(qd_ref[...], qg_ref[...])
        ckv, _ = _rms(kvd_ref[...], kvg_ref[...])
        cqb, ckvb = cq.astype(BF16), ckv.astype(BF16)
        cq_o[...] = cqb
        ckv_o[...] = ckvb
        krr = _rope(kr_ref[...], c, a, b)
        lane = lax.broadcasted_iota(jnp.int32, (ts, 128), 1)
        ones_lane = jnp.where(lane == MLA_V, 1.0, 0.0)
        for h in range(N_HEADS):
            cols = slice(h * HEAD_PAD, (h + 1) * HEAD_PAD)
            qb[h] = (_rope(_nn(cqb, wq_ref[:, cols]), c, a, b) * (MLA_SCALE * LOG2E)).astype(BF16)
            kh = _nn(ckvb, wk_ref[:, cols]) + krr
            vh = _nn(ckvb, wv_ref[:, cols]) + ones_lane
            kb[h], kbt[h] = kh.astype(BF16), kh.T.astype(BF16)
            vb[h], vbt[h] = vh.astype(BF16), vh.T.astype(BF16)
        for p in range(4):
            piece = qc_ref[:, p * 128:(p + 1) * 128] * (CA_SCALE * LOG2E)
            qc[2 * p] = jnp.where(lane < 64, piece, 0.0).astype(BF16)
            qc[2 * p + 1] = jnp.where(lane < 64, pltpu.roll(piece, 64, 1), 0.0).astype(BF16)
            for src, dst, dst_t, pad in ((kc_ref, kc, kct, 0.0), (vc_ref, vc, vct, ones_lane)):
                piece = src[:, p * 128:(p + 1) * 128]
                for h, head in ((2 * p, jnp.where(lane < 64, piece, pad)),
                                (2 * p + 1, jnp.where(lane < 64, pltpu.roll(piece, 64, 1), pad))):
                    dst[h], dst_t[h] = head.astype(BF16), head.T.astype(BF16)

    col = lambda w, c: pl.BlockSpec((ts, w), lambda i: (i, c))
    full = lambda shape: pl.BlockSpec(shape, lambda i: (0,) * len(shape))
    hshape = jax.ShapeDtypeStruct((N_HEADS, S, HEAD_PAD), BF16)
    tshape = jax.ShapeDtypeStruct((N_HEADS, HEAD_PAD, S), BF16)
    tspec = pl.BlockSpec((N_HEADS, HEAD_PAD, ts), lambda i: (0, 0, i))
    return _call(
        body, name="prep_fwd", grid=(S // ts,),
        in_specs=[col(512, P_QC // 512), col(512, P_KC // 512), col(512, P_VC // 512),
                  col(256, P_QD // 256), col(128, P_KVD // 128), col(128, P_KR // 128),
                  pl.BlockSpec((3, ts, 128), lambda i: (0, i, 0)),
                  full((1, MLA_Q_RANK)), full((1, MLA_KV_RANK)),
                  full((MLA_Q_RANK, 1024)), full((MLA_KV_RANK, 1024)), full((MLA_KV_RANK, 1024))],
        out_specs=[_hspec(ts)] * 6 + [tspec] * 4 + [pl.BlockSpec((ts, MLA_Q_RANK), lambda i: (i, 0)),
                                                    pl.BlockSpec((ts, MLA_KV_RANK), lambda i: (i, 0))],
        out_shape=[hshape] * 6 + [tshape] * 4 + [jax.ShapeDtypeStruct((S, MLA_Q_RANK), BF16),
                                                 jax.ShapeDtypeStruct((S, MLA_KV_RANK), BF16)],
        compiler_params=_params(1))(proj, proj, proj, proj, proj, proj, tab, qg, kvg, wq, wk, wv)


def prep_bwd(dproj, dqb, dkb, dvb, dqc, dkc, dvc, proj, tab, qg, kvg, wq, wk, wv):
    S = proj.shape[0]
    ts = LIGHT_ROW_TILE

    def body(dp_in, dqb_r, dkb_r, dvb_r, dqc_r, dkc_r, dvc_r, qd_ref, kvd_ref, tab_ref, qg_ref, kvg_ref,
             wq_ref, wk_ref, wv_ref, dp_ref, dqf, dkf, dvf, gq_ref, gkv_ref):
        del dp_in
        c, a, b = tab_ref[0], -tab_ref[1], -tab_ref[2]
        qd, kvd = qd_ref[...], kvd_ref[...]
        _, rq = _rms(qd, qg_ref[...])
        _, rkv = _rms(kvd, kvg_ref[...])
        dcq = jnp.zeros((ts, MLA_Q_RANK), F32)
        dckv = jnp.zeros((ts, MLA_KV_RANK), F32)
        dksum = jnp.zeros((ts, HEAD_PAD), F32)
        for h in range(N_HEADS):
            cols = slice(h * HEAD_PAD, (h + 1) * HEAD_PAD)
            dqh = _rope(dqb_r[h].astype(F32) * MLA_SCALE, c, a, b).astype(BF16)
            dqf[:, cols] = dqh
            dcq = dcq + _nt(dqh, wq_ref[:, cols])
            dk = dkb_r[h].astype(F32) * (1.0 / LOG2E)
            dksum = dksum + dk
            dkh = dk.astype(BF16)
            dkf[:, cols] = dkh
            dvh = dvb_r[h].astype(BF16)
            dvf[:, cols] = dvh
            dckv = dckv + _nt(dkh, wk_ref[:, cols]) + _nt(dvh, wv_ref[:, cols])
        lane = lax.broadcasted_iota(jnp.int32, (ts, 128), 1)
        rope_lanes = jnp.logical_and(lane >= MLA_NOPE, lane < MLA_QK)
        dp_ref[:, P_KR:P_KR + 128] = jnp.where(rope_lanes, _rope(dksum, c, a, b), 0.0).astype(BF16)
        dqd, gq = _rms_bwd(qd, qg_ref[...], rq, dcq)
        dkvd, gkv = _rms_bwd(kvd, kvg_ref[...], rkv, dckv)
        dp_ref[:, P_QD:P_QD + 256] = dqd.astype(BF16)
        dp_ref[:, P_KVD:P_KVD + 128] = dkvd.astype(BF16)
        first = pl.program_id(0) == 0
        _accumulate(gq_ref, gq, first)
        _accumulate(gkv_ref, gkv, first)
        for src, base, factor in ((dqc_r, P_QC, CA_SCALE), (dkc_r, P_KC, 1.0 / LOG2E), (dvc_r, P_VC, 1.0)):
            for p in range(4):
                dp_ref[:, base + p * 128:base + (p + 1) * 128] = (
                    (src[2 * p].astype(F32) + pltpu.roll(src[2 * p + 1].astype(F32), 64, 1)) * factor).astype(BF16)

    col = lambda w, c: pl.BlockSpec((ts, w), lambda i: (i, c))
    full = lambda shape: pl.BlockSpec(shape, lambda i: (0,) * len(shape))
    wide = jax.ShapeDtypeStruct((S, 1024), BF16)
    return _call(
        body, name="prep_bwd", grid=(S // ts,),
        in_specs=[pl.BlockSpec(memory_space=pl.ANY)] + [_hspec(ts)] * 6 +
                 [col(256, P_QD // 256), col(128, P_KVD // 128),
                  pl.BlockSpec((3, ts, 128), lambda i: (0, i, 0)),
                  full((1, MLA_Q_RANK)), full((1, MLA_KV_RANK)),
                  full((MLA_Q_RANK, 1024)), full((MLA_KV_RANK, 1024)), full((MLA_KV_RANK, 1024))],
        out_specs=[pl.BlockSpec((ts, 2048), lambda i: (i, 0))] + [pl.BlockSpec((ts, 1024), lambda i: (i, 0))] * 3 +
                  [full((1, MLA_Q_RANK)), full((1, MLA_KV_RANK))],
        out_shape=[jax.ShapeDtypeStruct(dproj.shape, BF16), wide, wide, wide,
                   jax.ShapeDtypeStruct((1, MLA_Q_RANK), F32), jax.ShapeDtypeStruct((1, MLA_KV_RANK), F32)],
        input_output_aliases={0: 0},
        compiler_params=_params(1))(dproj, dqb, dkb, dvb, dqc, dkc, dvc, proj, proj, tab, qg, kvg, wq, wk, wv)


def _diag_visible(t):
    r = lax.broadcasted_iota(jnp.int32, (t, t), 0) >> CHUNK_SHIFT
    c = lax.broadcasted_iota(jnp.int32, (t, t), 1) >> CHUNK_SHIFT
    return r <= c


def _pair_tables(nq, kv_major):
    if kv_major:
        pairs = [(kb, qi) for kb in range(nq) for qi in range(kb, nq)]
    else:
        pairs = [(kb, qi) for qi in range(nq) for kb in range(qi + 1)]
    return (jnp.asarray(np.array([p[0] for p in pairs], np.int32)),
            jnp.asarray(np.array([p[1] for p in pairs], np.int32)), len(pairs))


def _finish_softmax(acc, m):
    l = acc[MLA_V:MLA_V + 1, :]
    row = lax.broadcasted_iota(jnp.int32, acc.shape, 0)
    return jnp.where(row < MLA_V, acc / l, 0.0).T.astype(BF16), m + jnp.log2(l)


def _split_refs(refs, counts):
    out, pos = [], 0
    for c in counts:
        out.append(refs[pos:pos + c])
        pos += c
    return out


def mla_fwd(q, k, vt, exchange=()):
    H, S, _ = q.shape
    t, hb, n_ex = ATT_T, MLA_FWD_HEADS_PER_STEP, len(exchange)
    kb_tab, qi_tab, n_pairs = _pair_tables(S // t, False)

    def body(kb_ref, qi_ref, q_ref, k_ref, vt_ref, *rest):
        ex_src, (o_ref, lse_ref), ex_out, (m_s, acc_s), ex_sems = _split_refs(rest, (n_ex, 2, n_ex, 2, 3 if n_ex else 0))
        hg, p_id = pl.program_id(0), pl.program_id(1)
        kb, qi = kb_ref[p_id], qi_ref[p_id]

        if n_ex:
            @pl.when(jnp.logical_and(hg == 0, p_id == 0))
            def _():
                _exchange_issue(ex_src, ex_out, ex_sems, False, True)

        @pl.when(kb == 0)
        def _():
            m_s[...] = jnp.full_like(m_s, NEG_INF)
            acc_s[...] = jnp.zeros_like(acc_s)

        def step(masked):
            for h in range(hb):
                st = _nt(k_ref[h], q_ref[h])
                if masked:
                    st = jnp.where(_diag_visible(t), st, NEG_INF)
                m_prev = m_s[h]
                m_new = jnp.maximum(m_prev, jnp.max(st, axis=0, keepdims=True))
                p = jnp.exp2(st - m_new)
                acc_s[h] = jnp.exp2(m_prev - m_new) * acc_s[h] + _nn(vt_ref[h], p.astype(BF16))
                m_s[h] = m_new

        @pl.when(kb < qi)
        def _():
            step(False)

        @pl.when(kb == qi)
        def _():
            step(True)
            for h in range(hb):
                o_ref[h], lse_ref[h] = _finish_softmax(acc_s[h], m_s[h])

        if n_ex:
            @pl.when(jnp.logical_and(hg == H // hb - 1, p_id == n_pairs - 1))
            def _():
                _exchange_issue(ex_src, ex_out, ex_sems, False, False)

    grid_spec = pltpu.PrefetchScalarGridSpec(
        num_scalar_prefetch=2, grid=(H // hb, n_pairs),
        in_specs=[pl.BlockSpec((hb, t, HEAD_PAD), lambda h, p, kb, qi: (h, qi[p], 0)),
                  pl.BlockSpec((hb, t, HEAD_PAD), lambda h, p, kb, qi: (h, kb[p], 0)),
                  pl.BlockSpec((hb, HEAD_PAD, t), lambda h, p, kb, qi: (h, 0, kb[p]))] + [ANY] * n_ex,
        out_specs=[pl.BlockSpec((hb, t, HEAD_PAD), lambda h, p, kb, qi: (h, qi[p], 0)),
                   pl.BlockSpec((hb, 1, t), lambda h, p, kb, qi: (h, 0, qi[p]))] + [ANY] * n_ex,
        scratch_shapes=[pltpu.VMEM((hb, 1, t), F32), pltpu.VMEM((hb, HEAD_PAD, t), F32)] + _exchange_sems(n_ex))
    outs = _call(
        body, name="mla_fwd_gather" if n_ex else "mla_fwd", grid_spec=grid_spec,
        out_shape=[jax.ShapeDtypeStruct((H, S, HEAD_PAD), BF16), jax.ShapeDtypeStruct((H, 1, S), F32)] +
        _exchange_out_shape(exchange, False),
        compiler_params=_params(2))(kb_tab, qi_tab, q, k, vt, *exchange)
    return outs[0], outs[1], outs[2:]


def mla_bwd(q, k, kt, v, do, lse, delta, exchange=()):
    H, S, _ = q.shape
    t, hb, n_ex = ATT_T, MLA_HEADS_PER_STEP, len(exchange)
    nq = S // t
    kb_tab, qi_tab, n_pairs = _pair_tables(nq, True)

    def body(kb_ref, qi_ref, q_ref, k_ref, kt_ref, v_ref, do_ref, lse_ref, dl_ref, *rest):
        ex_src, (dq_ref, dk_ref, dv_ref), ex_out, (dqt_s, dk_s, dv_s), ex_sems = _split_refs(
            rest, (n_ex, 3, n_ex, 3, 3 if n_ex else 0))
        hg, p_id = pl.program_id(0), pl.program_id(1)
        kb, qi = kb_ref[p_id], qi_ref[p_id]

        if n_ex:
            @pl.when(jnp.logical_and(hg == 0, p_id == 0))
            def _():
                _exchange_issue(ex_src, ex_out, ex_sems, True, True)

        @pl.when(p_id == 0)
        def _():
            dqt_s[...] = jnp.zeros_like(dqt_s)

        @pl.when(qi == kb)
        def _():
            dk_s[...] = jnp.zeros_like(dk_s)
            dv_s[...] = jnp.zeros_like(dv_s)

        def step(masked):
            for h in range(hb):
                st = _nt(k_ref[h], q_ref[h])
                if masked:
                    st = jnp.where(_diag_visible(t), st, NEG_INF)
                pt = jnp.exp2(st - lse_ref[h])
                dv_s[h] += _nn(pt.astype(BF16), do_ref[h])
                dsb = (pt * (_nt(v_ref[h], do_ref[h]) - dl_ref[h])).astype(BF16)
                dk_s[h] += _nn(dsb, q_ref[h])
                dqt_s[h, qi] += _nn(kt_ref[h], dsb)

        @pl.when(qi == kb)
        def _():
            step(True)
            rows = pl.ds(pl.multiple_of(qi * t, t), t)
            for h in range(hb):
                dq_ref[h, rows, :] = dqt_s[h, qi].T.astype(BF16)

        @pl.when(qi > kb)
        def _():
            step(False)

        @pl.when(qi == nq - 1)
        def _():
            dk_ref[...] = dk_s[...].astype(BF16)
            dv_ref[...] = dv_s[...].astype(BF16)

        if n_ex:
            @pl.when(jnp.logical_and(hg == H // hb - 1, p_id == n_pairs - 1))
            def _():
                _exchange_issue(ex_src, ex_out, ex_sems, True, False)

    qtile = pl.BlockSpec((hb, t, HEAD_PAD), lambda h, p, kb, qi: (h, qi[p], 0))
    ktile = pl.BlockSpec((hb, t, HEAD_PAD), lambda h, p, kb, qi: (h, kb[p], 0))
    stat = pl.BlockSpec((hb, 1, t), lambda h, p, kb, qi: (h, 0, qi[p]))
    grid_spec = pltpu.PrefetchScalarGridSpec(
        num_scalar_prefetch=2, grid=(H // hb, n_pairs),
        in_specs=[qtile, ktile, pl.BlockSpec((hb, HEAD_PAD, t), lambda h, p, kb, qi: (h, 0, kb[p])), ktile, qtile,
                  stat, stat] + [ANY] * n_ex,
        out_specs=[pl.BlockSpec((hb, S, HEAD_PAD), lambda h, p, kb, qi: (h, 0, 0)), ktile, ktile] + [ANY] * n_ex,
        scratch_shapes=[pltpu.VMEM((hb, nq, HEAD_PAD, t), F32), pltpu.VMEM((hb, t, HEAD_PAD), F32),
                        pltpu.VMEM((hb, t, HEAD_PAD), F32)] + _exchange_sems(n_ex))
    outs = _call(
        body, name="mla_bwd_scatter" if n_ex else "mla_bwd", grid_spec=grid_spec,
        out_shape=[jax.ShapeDtypeStruct((H, S, HEAD_PAD), BF16)] * 3 + _exchange_out_shape(exchange, True),
        compiler_params=_params(2))(kb_tab, qi_tab, q, k, kt, v, do, lse, delta, *exchange)
    return outs[0], outs[1], outs[2], outs[3:]


def _band_specs(t, hb):
    prev = lambda i: jnp.maximum(i - 1, 0)
    return dict(
        cur=pl.BlockSpec((hb, t, HEAD_PAD), lambda h, i: (h, i, 0)),
        prev=pl.BlockSpec((hb, t, HEAD_PAD), lambda h, i: (h, prev(i), 0)),
        cur_t=pl.BlockSpec((hb, HEAD_PAD, t), lambda h, i: (h, 0, i)),
        prev_t=pl.BlockSpec((hb, HEAD_PAD, t), lambda h, i: (h, 0, prev(i))),
        stat=pl.BlockSpec((hb, 1, t), lambda h, i: (h, 0, i)),
        bias_prev=pl.BlockSpec((hb, 1, t, t), lambda h, i: (h, jnp.where(i == 0, 1, 0), 0, 0)),
        bias_cur=pl.BlockSpec((hb, 1, t, t), lambda h, i: (h, 2, 0, 0)))


def band_fwd(q, k, vt, bias):
    H, S, _ = q.shape
    t, hb = ATT_T, MLA_HEADS_PER_STEP
    sp = _band_specs(t, hb)

    def body(q_ref, kp_ref, kc_ref, vtp_ref, vtc_ref, bp_ref, bc_ref, o_ref, lse_ref):
        for h in range(hb):
            s0 = _nt(kp_ref[h], q_ref[h]) + bp_ref[h, 0]
            s1 = _nt(kc_ref[h], q_ref[h]) + bc_ref[h, 0]
            m = jnp.maximum(jnp.max(s0, axis=0, keepdims=True), jnp.max(s1, axis=0, keepdims=True))
            ot = (_nn(vtp_ref[h], jnp.exp2(s0 - m).astype(BF16)) +
                  _nn(vtc_ref[h], jnp.exp2(s1 - m).astype(BF16)))
            o_ref[h], lse_ref[h] = _finish_softmax(ot, m)

    return _call(
        body, name="band_fwd", grid=(H // hb, S // t),
        in_specs=[sp['cur'], sp['prev'], sp['cur'], sp['prev_t'], sp['cur_t'], sp['bias_prev'], sp['bias_cur']],
        out_specs=[sp['cur'], sp['stat']],
        out_shape=[jax.ShapeDtypeStruct((H, S, HEAD_PAD), BF16), jax.ShapeDtypeStruct((H, 1, S), F32)],
        compiler_params=_params(2))(q, k, k, vt, vt, bias, bias)


def band_bwd(q, k, kt, v, do, lse, delta, bias):
    H, S, _ = q.shape
    t = ATT_T
    sp = _band_specs(t, 1)

    def body(q_ref, kp_ref, kc_ref, ktp_ref, ktc_ref, vp_ref, vc_ref, do_ref, lse_ref, dl_ref, bp_ref, bc_ref,
             dq_ref, dk_ref, dv_ref, db_ref):
        i = pl.program_id(1)

        @pl.when(i == 0)
        def _():
            dk_ref[...] = jnp.zeros_like(dk_ref)
            dv_ref[...] = jnp.zeros_like(dv_ref)
            db_ref[...] = jnp.zeros_like(db_ref)

        qv, dov = q_ref[0], do_ref[0]
        dqt = jnp.zeros((HEAD_PAD, t), F32)
        windows = ((0, jnp.maximum(i - 1, 0), kp_ref, ktp_ref, vp_ref, bp_ref),
                   (1, i, kc_ref, ktc_ref, vc_ref, bc_ref))
        for w, blk, k_ref, kt_ref, v_ref, b_ref in windows:
            rows = pl.ds(pl.multiple_of(blk * t, t), t)
            pt = jnp.exp2(_nt(k_ref[0], qv) + b_ref[0, 0] - lse_ref[0])
            dv_ref[0, rows, :] += _nn(pt.astype(BF16), dov)
            ds = pt * (_nt(v_ref[0], dov) - dl_ref[0])
            db_ref[0, w] += ds
            dsb = ds.astype(BF16)
            dk_ref[0, rows, :] += _nn(dsb, qv)
            dqt = dqt + _nn(kt_ref[0], dsb)
        dq_ref[0] = dqt.T.astype(BF16)

    whole = pl.BlockSpec((1, S, HEAD_PAD), lambda h, i: (h, 0, 0))
    return _call(
        body, name="band_bwd", grid=(H, S // t),
        in_specs=[sp['cur'], sp['prev'], sp['cur'], sp['prev_t'], sp['cur_t'], sp['prev'], sp['cur'], sp['cur'],
                  sp['stat'], sp['stat'], sp['bias_prev'], sp['bias_cur']],
        out_specs=[sp['cur'], whole, whole, pl.BlockSpec((1, 2, t, t), lambda h, i: (h, 0, 0, 0))],
        out_shape=[jax.ShapeDtypeStruct((H, S, HEAD_PAD), BF16), jax.ShapeDtypeStruct((H, S, HEAD_PAD), F32),
                   jax.ShapeDtypeStruct((H, S, HEAD_PAD), F32), jax.ShapeDtypeStruct((H, 2, t, t), F32)],
        compiler_params=_params(2))(q, k, k, kt, kt, v, v, do, lse, delta, bias, bias)


def _compact(o_ref):
    return jnp.concatenate([o_ref[2 * p].astype(F32) + pltpu.roll(o_ref[2 * p + 1].astype(F32), 64, 1)
                            for p in range(4)], axis=1)


def merge_fwd(ob, oc, proj, ya, gate_b, wbr, w_out, x, post_g):
    S = x.shape[0]
    ts = ROW_TILE

    def body(ob_ref, oc_ref, zb_ref, zc_ref, ya_ref, gl_ref, gb_ref, wbr_ref, wo_ref, x_ref, pg_ref,
             xo_ref, yb_ref, yc_ref, mg_ref, out_ref):
        zb, zc = zb_ref[...], zc_ref[...]
        yb = (_compact(ob_ref) * (zb * _sigmoid(zb))).astype(BF16)
        yc = (_compact(oc_ref) * (zc * _sigmoid(zc))).astype(BF16)
        yb_ref[...] = yb
        yc_ref[...] = yc
        merged = jnp.zeros((ts, D_MODEL), F32)
        for n, y in enumerate((ya_ref[...], yb, yc)):
            cols = slice(n * D_MODEL, (n + 1) * D_MODEL)
            gate = _sigmoid(gl_ref[:, cols] + gb_ref[:, cols])
            merged = merged + gate * _nn(y, wbr_ref[n])
        mb = merged.astype(BF16)
        mg_ref[...] = mb
        out = _nn(mb, wo_ref[...])
        out_ref[...] = out
        normed, _ = _rms(out, pg_ref[...])
        xo_ref[...] = x_ref[...] + normed

    row = lambda w: pl.BlockSpec((ts, w), lambda i: (i, 0))
    col = lambda w, c: pl.BlockSpec((ts, w), lambda i: (i, c))
    full = lambda shape: pl.BlockSpec(shape, lambda i: (0,) * len(shape))
    return _call(
        body, name="merge_fwd", grid=(S // ts,),
        in_specs=[_hspec(ts), _hspec(ts), col(512, P_ZB // 512), col(512, P_ZC // 512), row(512),
                  col(3072, P_G // 3072), full((1, 3072)), full((3, BR_WIDTH, D_MODEL)),
                  full((D_MODEL, D_MODEL)), row(D_MODEL), full((1, D_MODEL))],
        out_specs=[row(D_MODEL), row(512), row(512), row(D_MODEL), row(D_MODEL)],
        out_shape=[jax.ShapeDtypeStruct((S, D_MODEL), F32), jax.ShapeDtypeStruct((S, 512), BF16),
                   jax.ShapeDtypeStruct((S, 512), BF16), jax.ShapeDtypeStruct((S, D_MODEL), BF16),
                   jax.ShapeDtypeStruct((S, D_MODEL), F32)],
        compiler_params=_params(1))(ob, oc, proj, proj, ya, proj, gate_b, wbr, w_out, x, post_g)


def gate_bwd(g, out, post_g, w_out, proj, gate_b, ya, yb, yc, wbr):
    S = g.shape[0]
    ts = ROW_TILE

    def body(g_ref, out_ref, pg_ref, wo_ref, gl_ref, gb_ref, ya_ref, yb_ref, yc_ref, wbr_ref,
             dp_ref, do_ref, dba_ref, dbb_ref, dbc_ref, dy_ref, ggb_ref, gp_ref):
        first = pl.program_id(0) == 0
        ov = out_ref[...]
        _, r = _rms(ov, pg_ref[...])
        dout, gp = _rms_bwd(ov, pg_ref[...], r, g_ref[...])
        db = dout.astype(BF16)
        do_ref[...] = db
        _accumulate(gp_ref, gp, first)
        dm = _nt(db, wo_ref[...])
        ggb = []
        for n, (y_ref, dbr_ref) in enumerate(((ya_ref, dba_ref), (yb_ref, dbb_ref), (yc_ref, dbc_ref))):
            cols = slice(n * D_MODEL, (n + 1) * D_MODEL)
            br = _nn(y_ref[...], wbr_ref[n])
            sg = _sigmoid(gl_ref[:, cols] + gb_ref[:, cols])
            dgl = dm * br * (sg * (1.0 - sg))
            dp_ref[:, cols] = dgl.astype(BF16)
            ggb.append(jnp.sum(dgl, axis=0, keepdims=True))
            dbr = (dm * sg).astype(BF16)
            dbr_ref[...] = dbr
            dy_ref[n] = _nt(dbr, wbr_ref[n])
        _accumulate(ggb_ref, jnp.concatenate(ggb, axis=1), first)

    row = lambda w: pl.BlockSpec((ts, w), lambda i: (i, 0))
    full = lambda shape: pl.BlockSpec(shape, lambda i: (0,) * len(shape))
    wide = jax.ShapeDtypeStruct((S, D_MODEL), BF16)
    return _call(
        body, name="gate_bwd", grid=(S // ts,),
        in_specs=[row(D_MODEL), row(D_MODEL), full((1, D_MODEL)), full((D_MODEL, D_MODEL)),
                  pl.BlockSpec((ts, 3072), lambda i: (i, P_G // 3072)), full((1, 3072)),
                  row(512), row(512), row(512), full((3, BR_WIDTH, D_MODEL))],
        out_specs=[pl.BlockSpec((ts, 3072), lambda i: (i, P_G // 3072)), row(D_MODEL), row(D_MODEL), row(D_MODEL),
                   row(D_MODEL), pl.BlockSpec((3, ts, 512), lambda i: (0, i, 0)), full((1, 3072)),
                   full((1, D_MODEL))],
        out_shape=[jax.ShapeDtypeStruct((S, P_W), BF16), wide, wide, wide, wide,
                   jax.ShapeDtypeStruct((3, S, 512), F32), jax.ShapeDtypeStruct((1, 3072), F32),
                   jax.ShapeDtypeStruct((1, D_MODEL), F32)],
        compiler_params=_params(1))(g, out, post_g, w_out, proj, gate_b, ya, yb, yc, wbr)


def ungate_bwd(dproj, dy, ob, oc, proj):
    S = proj.shape[0]
    ts = LIGHT_ROW_TILE

    def body(dp_in, dyb_ref, dyc_ref, ob_ref, oc_ref, zb_ref, zc_ref, dp_ref, dob_ref, doc_ref, dlb_ref, dlc_ref):
        del dp_in
        lane = lax.broadcasted_iota(jnp.int32, (ts, 128), 1)
        for n, (dy_ref, o_ref, z_ref, do_ref, dl_ref) in enumerate(
                ((dyb_ref, ob_ref, zb_ref, dob_ref, dlb_ref), (dyc_ref, oc_ref, zc_ref, doc_ref, dlc_ref))):
            zz = z_ref[...]
            dyv = dy_ref[0]
            sg = _sigmoid(zz)
            dp_ref[:, n * 512:(n + 1) * 512] = (dyv * _compact(o_ref) * (sg * (1.0 + zz * (1.0 - sg)))).astype(BF16)
            do_c = dyv * (zz * sg)
            for p in range(4):
                piece = do_c[:, p * 128:(p + 1) * 128]
                for h, d in ((2 * p, jnp.where(lane < 64, piece, 0.0)),
                             (2 * p + 1, jnp.where(lane < 64, pltpu.roll(piece, 64, 1), 0.0))):
                    do_ref[h] = d.astype(BF16)
                    dl_ref[h] = jnp.sum((d * o_ref[h].astype(F32)).T, axis=0, keepdims=True)

    col = lambda c: pl.BlockSpec((ts, 512), lambda i: (i, c))
    dysp = lambda n: pl.BlockSpec((1, ts, 512), lambda i: (n, i, 0))
    stat = pl.BlockSpec((N_HEADS, 1, ts), lambda i: (0, 0, i))
    hshape = jax.ShapeDtypeStruct((N_HEADS, S, HEAD_PAD), BF16)
    sshape = jax.ShapeDtypeStruct((N_HEADS, 1, S), F32)
    return _call(
        body, name="ungate_bwd", grid=(S // ts,),
        in_specs=[pl.BlockSpec(memory_space=pl.ANY), dysp(1), dysp(2), _hspec(ts), _hspec(ts),
                  col(P_ZB // 512), col(P_ZC // 512)],
        out_specs=[pl.BlockSpec((ts, 1024), lambda i: (i, P_ZB // 1024)), _hspec(ts), _hspec(ts), stat, stat],
        out_shape=[jax.ShapeDtypeStruct(dproj.shape, BF16), hshape, hshape, sshape, sshape],
        input_output_aliases={0: 0},
        compiler_params=_params(1))(dproj, dy, dy, ob, oc, proj, proj)


def loss_head(y, target):
    S, D = y.shape
    ts = LIGHT_ROW_TILE

    def body(y_ref, t_ref, dy_ref, sq_ref):
        d = y_ref[...] - t_ref[...]
        dy_ref[...] = d * (1.0 / D)
        _accumulate(sq_ref, jnp.sum(d * d, axis=0, keepdims=True), pl.program_id(0) == 0)

    row = pl.BlockSpec((ts, D), lambda i: (i, 0))
    return _call(
        body, name="loss_head", grid=(S // ts,), in_specs=[row, row],
        out_specs=[row, pl.BlockSpec((1, D), lambda i: (0, 0))],
        out_shape=[jax.ShapeDtypeStruct((S, D), F32), jax.ShapeDtypeStruct((1, D), F32)],
        compiler_params=_params(1))(y, target)


def _row_tile(rows, cols):
    for cand in (1024, 512, 256, 128, 64, 32, 16, 8):
        if rows % cand == 0 and cand * cols * 4 <= 1024 * 1024:
            return cand
    return rows


def adamw(w, grads, m, v):
    shape = w.shape
    cols = shape[-1]
    rows = int(np.prod(shape[:-1]))
    tr = _row_tile(rows, cols)
    n_g = len(grads)
    c1 = 1.0 - ADAM_B1 ** ADAM_STEP
    c2 = 1.0 - ADAM_B2 ** ADAM_STEP

    def body(*refs):
        w_ref, m_ref, v_ref = refs[:3]
        g_refs = refs[3:3 + n_g]
        go_ref, d_ref, mo_ref, vo_ref = refs[3 + n_g:]
        gv = g_refs[0][...]
        for g_ref in g_refs[1:]:
            gv = gv + g_ref[...]
        go_ref[...] = gv
        mn = ADAM_B1 * m_ref[...] + (1.0 - ADAM_B1) * gv
        vn = ADAM_B2 * v_ref[...] + (1.0 - ADAM_B2) * (gv * gv)
        mo_ref[...] = mn
        vo_ref[...] = vn
        d_ref[...] = -ADAM_LR * ((mn / c1) / (jnp.sqrt(vn / c2) + ADAM_EPS) + ADAM_WD * w_ref[...])

    blk = pl.BlockSpec((tr, cols), lambda i: (i, 0))
    sds = jax.ShapeDtypeStruct((rows, cols), F32)
    outs = _call(
        body, name="adamw", grid=(rows // tr,), in_specs=[blk] * (3 + n_g), out_specs=[blk] * 4,
        out_shape=[sds] * 4, compiler_params=_params(1))(
            *[a.reshape(rows, cols) for a in (w, m, v, *grads)])
    return [o.reshape(shape) for o in outs]


def add_lead(parts):
    n = parts.shape[0]
    shape = parts.shape[1:]
    cols = shape[-1]
    rows = int(np.prod(shape[:-1]))
    tr = _row_tile(rows, cols * n)

    def body(p_ref, o_ref):
        acc = p_ref[0].astype(F32)
        for s in range(1, n):
            acc = acc + p_ref[s].astype(F32)
        o_ref[...] = acc

    out = _call(
        body, name="add_lead", grid=(rows // tr,),
        in_specs=[pl.BlockSpec((n, tr, cols), lambda i: (0, i, 0))],
        out_specs=pl.BlockSpec((tr, cols), lambda i: (i, 0)),
        out_shape=jax.ShapeDtypeStruct((rows, cols), F32),
        compiler_params=_params(1))(parts.reshape(n, rows, cols))
    return out.reshape(shape)


ANY = pl.BlockSpec(memory_space=pl.ANY)


def _other_chips(x, y):
    return [(1 - x, y), (x, 1 - y), (1 - x, 1 - y)]


def chip_exchange(arrays, scatter, name):
    n = len(arrays)

    def body(*refs):
        _exchange_issue(refs[:n], refs[n:2 * n], refs[2 * n:], scatter, True)
        _exchange_issue(refs[:n], refs[n:2 * n], refs[2 * n:], scatter, False)

    return _call(
        body, name=name, in_specs=[ANY] * n, out_specs=[ANY] * n,
        out_shape=_exchange_out_shape(arrays, scatter), scratch_shapes=_exchange_sems(n))(*arrays)


def _exchange_out_shape(arrays, scatter):
    return [jax.ShapeDtypeStruct(a.shape if scatter else (4,) + a.shape, a.dtype) for a in arrays]


def _exchange_sems(n):
    if n == 0:
        return []
    return [pltpu.SemaphoreType.DMA((3 * n,)), pltpu.SemaphoreType.DMA((3 * n,)), pltpu.SemaphoreType.DMA((n,))]


def _exchange_issue(srcs, outs, sems, scatter, start):
    send_sems, recv_sems, local_sems = sems
    x, y, c = lax.axis_index("x"), lax.axis_index("y"), lax.axis_index("c")
    me = 2 * x + y
    for a in range(len(srcs)):
        local_src = srcs[a].at[me] if scatter else srcs[a]
        mine = pltpu.make_async_copy(local_src, outs[a].at[me], local_sems.at[a])
        sends = []
        for j, (px, py) in enumerate(_other_chips(x, y)):
            pair = dict(send_sem=send_sems.at[3 * a + j], recv_sem=recv_sems.at[3 * a + j],
                        device_id=(px, py, c), device_id_type=MESH)
            sends.append(pltpu.make_async_remote_copy(
                src_ref=srcs[a].at[2 * px + py] if scatter else srcs[a], dst_ref=outs[a].at[me], **pair))
            if not start:
                pltpu.make_async_remote_copy(src_ref=local_src, dst_ref=outs[a].at[2 * px + py], **pair).wait_recv()
        if start:
            mine.start()
            for cp in sends:
                cp.start()
        else:
            for cp in sends:
                cp.wait_send()
            mine.wait()


def sibling_exchange(arrays):
    n = len(arrays)

    def body(*refs):
        srcs, outs = refs[:n], refs[n:2 * n]
        send_sems, recv_sems = refs[2 * n:]
        x, y, c = lax.axis_index("x"), lax.axis_index("y"), lax.axis_index("c")
        copies = [pltpu.make_async_remote_copy(src_ref=srcs[a], dst_ref=outs[a], send_sem=send_sems.at[a],
                                               recv_sem=recv_sems.at[a], device_id=(x, y, 1 - c), device_id_type=MESH)
                  for a in range(n)]
        for cp in copies:
            cp.start()
        for cp in copies:
            cp.wait()

    return _call(
        body, name="sibling_exchange", in_specs=[ANY] * n, out_specs=[ANY] * n,
        out_shape=[jax.ShapeDtypeStruct(a.shape, a.dtype) for a in arrays],
        scratch_shapes=[pltpu.SemaphoreType.DMA((n,)), pltpu.SemaphoreType.DMA((n,))])(*arrays)


def _perm_from_shards(sh):
    rows = sh.shape[1]
    pieces, pos = [], 0
    for lo, hi, plo in sorted(NAT_SEGS, key=lambda s: s[2]):
        if plo > pos:
            pieces.append(jnp.zeros((rows, plo - pos), sh.dtype))
            pos = plo
        c = lo
        while c < hi:
            kk = c // SHARD_COLS
            e = min(hi, (kk + 1) * SHARD_COLS)
            pieces.append(sh[kk][:, c - kk * SHARD_COLS:e - kk * SHARD_COLS])
            c = e
        pos += hi - lo
    if pos < P_W:
        pieces.append(jnp.zeros((rows, P_W - pos), sh.dtype))
    return jnp.concatenate(pieces, axis=1)


def _shards_from_perm(p):
    out = []
    for kk in range(4):
        lo_k, hi_k = kk * SHARD_COLS, (kk + 1) * SHARD_COLS
        pieces = []
        for lo, hi, plo in NAT_SEGS:
            a, b = max(lo, lo_k), min(hi, hi_k)
            if a < b:
                pieces.append(p[:, plo + (a - lo):plo + (b - lo)])
        out.append(jnp.concatenate(pieces, axis=1))
    return jnp.stack(out)


def _split4(a, axis):
    shape = a.shape
    a = a.reshape(shape[:axis] + (4, shape[axis] // 4) + shape[axis + 1:])
    return jnp.moveaxis(a, axis, 0)


def _join4(a, axis):
    a = jnp.moveaxis(a, 0, axis)
    shape = a.shape
    return a.reshape(shape[:axis] + (4 * shape[axis + 1],) + shape[axis + 2:])


def _pad_heads(w, per_head, lo, hi):
    r = w.shape[0]
    wh = w.reshape(r, N_HEADS, per_head)[:, :, lo:hi]
    return jnp.pad(wh, ((0, 0), (0, 0), (0, HEAD_PAD - (hi - lo)))).reshape(r, N_HEADS * HEAD_PAD)


def _rope_table(S):
    half = MLA_ROPE // 2
    inv = ROPE_BASE ** (-jnp.arange(half, dtype=F32) / half)
    ang = jnp.arange(S).astype(F32)[:, None] * inv[None, :]
    cos, sin = jnp.cos(ang), jnp.sin(ang)
    z = lambda n: jnp.zeros((S, n), F32)
    c = jnp.concatenate([jnp.ones((S, MLA_NOPE), F32), cos, cos, z(32)], axis=1)
    a = jnp.concatenate([z(MLA_NOPE), -sin, z(48)], axis=1)
    b = jnp.concatenate([z(MLA_NOPE + half), sin, z(32)], axis=1)
    return jnp.stack([c, a, b])


def _band_onehot():
    t = ATT_T
    m = np.arange(2 * t)
    d = np.where(m < t, m, m - 2 * t)
    idx = np.stack([np.clip(off + d, -REL_CLIP, REL_CLIP) + REL_CLIP for off in (t, 0)])
    return (idx[:, :, None] == np.arange(2 * REL_CLIP + 1)[None, None, :]).astype(np.float32)


def bias_expand(diag):
    t = ATT_T

    def body(d_ref, o_ref):
        kc = lax.broadcasted_iota(jnp.int32, (t, t), 0) >> CHUNK_SHIFT
        qc = lax.broadcasted_iota(jnp.int32, (t, t), 1) >> CHUNK_SHIFT
        for w, visible in ((0, kc >= qc), (1, kc <= qc)):
            rows = jnp.broadcast_to(d_ref[0, w:w + 1, :], (t, 2 * t))
            skew = pltpu.roll(rows, 0, 1, stride=1, stride_axis=0)[:, :t]
            o_ref[0, 2 * w] = jnp.where(visible, skew * LOG2E, NEG_INF)
        o_ref[0, 1] = jnp.full((t, t), NEG_INF, F32)

    return _call(
        body, name="bias_expand", grid=(N_HEADS,),
        in_specs=[pl.BlockSpec((1, 2, 2 * t), lambda h: (h, 0, 0))],
        out_specs=pl.BlockSpec((1, 3, t, t), lambda h: (h, 0, 0, 0)),
        out_shape=jax.ShapeDtypeStruct((N_HEADS, 3, t, t), F32),
        compiler_params=_params(1))(diag)


def bias_fold(dtiles):
    t = ATT_T

    def body(d_ref, o_ref):
        pad = jnp.zeros((8, t), F32)
        for w in range(2):
            acc = jnp.concatenate([d_ref[0, w, 0:8, :], pad], axis=1)
            for g in range(1, t // 8):
                grp = jnp.concatenate([d_ref[0, w, 8 * g:8 * g + 8, :], pad], axis=1)
                acc = acc + pltpu.roll(grp, 2 * t - 8 * g, 1)
            out = acc[0:1, :]
            for s in range(1, 8):
                out = out + pltpu.roll(acc, 2 * t - s, 1)[s:s + 1, :]
            o_ref[0, w:w + 1, :] = out

    return _call(
        body, name="bias_fold", grid=(N_HEADS,),
        in_specs=[pl.BlockSpec((1, 2, t, t), lambda h: (h, 0, 0, 0))],
        out_specs=pl.BlockSpec((1, 2, 2 * t), lambda h: (h, 0, 0)),
        out_shape=jax.ShapeDtypeStruct((N_HEADS, 2, 2 * t), F32),
        compiler_params=_params(1))(dtiles)


def _bias_tiles(table):
    diag = jnp.einsum('hr,wdr->hwd', table, jnp.asarray(_band_onehot()), precision=lax.Precision.HIGHEST)
    return bias_expand(diag)


def _bias_tiles_grad(dtiles):
    return jnp.einsum('hwd,wdr->hr', bias_fold(dtiles), jnp.asarray(_band_onehot()),
                      precision=lax.Precision.HIGHEST)


def _layer_consts(lw):
    tri = np.tril(np.ones((SGU_BLOCK, SGU_BLOCK), np.float32))
    ws = (lw['sgu_w'] * tri).astype(BF16)
    return dict(
        ws=ws, ws_t=jnp.swapaxes(ws, 1, 2), sgu_bias=jnp.repeat(lw['sgu_b'].T, CA_HEAD_DIM, axis=1),
        bias=_bias_tiles(lw['ca_rel_bias']),
        wq=_pad_heads(lw['mla_w_uq'], MLA_QK, 0, MLA_QK),
        wk=_pad_heads(lw['mla_w_ukv'], MLA_NOPE + MLA_V, 0, MLA_NOPE),
        wv=_pad_heads(lw['mla_w_ukv'], MLA_NOPE + MLA_V, MLA_NOPE, MLA_NOPE + MLA_V),
        gate_b=lw['gate_b'].reshape(1, 3 * D_MODEL),
        pre_g=lw['pre_g'][None], post_g=lw['post_g'][None], ln_g=lw['sgu_ln_g'][None], ln_b=lw['sgu_ln_b'][None],
        qg=lw['mla_q_norm_g'][None], kvg=lw['mla_kv_norm_g'][None])


def _layer_fwd(x, lw, k, tab, next_shards):
    proj, xn = norm_matmul(x, k['pre_g'], lw['w_in'])
    ya = sgu_fwd(proj, k['ln_g'], k['ln_b'], k['ws'], k['sgu_bias'])
    qb, kb, vb, qc, kc, vc, kbt, vbt, kct, vct, cq, ckv = prep_fwd(proj, tab, k['qg'], k['kvg'], k['wq'], k['wk'],
                                                                   k['wv'])
    ob, lse_b, gathered = mla_fwd(qb, kb, vbt, next_shards)
    oc, lse_c = band_fwd(qc, kc, vct, k['bias'])
    x_new, yb, yc, merged, out = merge_fwd(ob, oc, proj, ya, k['gate_b'], lw['w_branch'], lw['w_out'], x,
                                           k['post_g'])
    saved = dict(x=x, proj=proj, xn=xn, ya=ya, yb=yb, yc=yc, qb=qb, kb=kb, vb=vb, qc=qc, kc=kc, vc=vc, kbt=kbt, kct=kct,
                 cq=cq, ckv=ckv, ob=ob, oc=oc, lse_b=lse_b, lse_c=lse_c, merged=merged, out=out)
    return x_new, saved, gathered


def _layer_bwd(g, s, lw, k, tab, pending_parts, scatter_own):
    S = g.shape[0]
    H = N_HEADS
    dproj, dout, dba, dbb, dbc, dy, g_gate_b, g_post = gate_bwd(
        g, s['out'], k['post_g'], lw['w_out'], s['proj'], k['gate_b'], s['ya'], s['yb'], s['yc'], lw['w_branch'])
    g_w_out = matmul_tn(s['merged'], dout, 512)
    g_w_branch = jnp.stack([matmul_tn(y, d, 512) for y, d in ((s['ya'], dba), (s['yb'], dbb), (s['yc'], dbc))])
    dproj, dob, doc, dl_b, dl_c = ungate_bwd(dproj, dy, s['ob'], s['oc'], s['proj'])
    dqb, dkb, dvb, landed = mla_bwd(s['qb'], s['kb'], s['kbt'], s['vb'], dob, s['lse_b'], dl_b, pending_parts)
    dqc, dkc, dvc, dbias = band_bwd(s['qc'], s['kc'], s['kct'], s['vc'], doc, s['lse_c'], dl_c, k['bias'])
    dproj, dqf, dkf, dvf, g_qg, g_kvg = prep_bwd(dproj, dqb, dkb, dvb, dqc, dkc, dvc, s['proj'], tab,
                                                 k['qg'], k['kvg'], k['wq'], k['wk'], k['wv'])
    g_wq = matmul_tn(s['cq'], dqf, 512).reshape(MLA_Q_RANK, H, HEAD_PAD)[:, :, :MLA_QK]
    g_wk = matmul_tn(s['ckv'], dkf, 512).reshape(MLA_KV_RANK, H, HEAD_PAD)[:, :, :MLA_NOPE]
    g_wv = matmul_tn(s['ckv'], dvf, 512).reshape(MLA_KV_RANK, H, HEAD_PAD)[:, :, :MLA_V]
    dproj, g_ln_g, g_ln_b, g_ws, g_sgu_bias = sgu_bwd(dproj, dy, s['proj'], k['ln_g'], k['ln_b'], k['ws'],
                                                      k['ws_t'], k['sgu_bias'])
    g_w_in = matmul_tn(s['xn'], dproj, MM_TN)
    sharded = _sharded_parts(dict(
        w_in=g_w_in, mla_w_uq=g_wq.reshape(MLA_Q_RANK, H * MLA_QK),
        mla_w_ukv=jnp.concatenate([g_wk, g_wv], axis=2).reshape(MLA_KV_RANK, H * (MLA_NOPE + MLA_V)),
        w_branch=g_w_branch, gate_b=g_gate_b.reshape(N_BRANCH, D_MODEL), w_out=g_w_out))
    dx, g_pre, own_landed = proj_bwd_x(dproj, lw['w_in'], s['x'], k['pre_g'], g, sharded if scatter_own else ())
    tri = np.tril(np.ones((SGU_BLOCK, SGU_BLOCK), np.float32))
    small = _small_pack(dict(
        pre_g=g_pre[0], post_g=g_post[0], sgu_ln_g=g_ln_g[0], sgu_ln_b=g_ln_b[0],
        sgu_w=g_ws * tri, sgu_b=jnp.sum(g_sgu_bias.reshape(SGU_BLOCK, 8, CA_HEAD_DIM), axis=2).T,
        mla_q_norm_g=g_qg[0], mla_kv_norm_g=g_kvg[0], ca_rel_bias=_bias_tiles_grad(dbias)))
    return dx, (own_landed if scatter_own else sharded), small, landed


BF16_PARTS = ('w_in', 'mla_w_uq', 'mla_w_ukv', 'w_branch', 'w_out')


def _weight_shards(w, l):
    return [w[n][l].astype(BF16) if n in BF16_PARTS else w[n][l] for n in SHARDED]


def _full_weights(gathered, small):
    lw = {n: _join4(a, SHARD_AXIS[n]) for n, a in zip(SHARDED, gathered) if n != 'w_in'}
    lw['w_in'] = _perm_from_shards(gathered[0])
    lw.update(small)
    return lw


def _small_pack(grads):
    flat = jnp.concatenate([grads[n].reshape(-1) for n in SMALL])
    quarter = -(-flat.size // (4 * 1024)) * 1024
    return jnp.pad(flat, (0, 4 * quarter - flat.size)).reshape(4, quarter // 128, 128)


def _sharded_parts(grads):
    parts = [_shards_from_perm(grads['w_in'])]
    parts += [_split4(grads[n], SHARD_AXIS[n]) for n in SHARDED if n != 'w_in']
    return [p.astype(BF16) if n in BF16_PARTS else p for n, p in zip(SHARDED, parts)]


def train_step_local(x, target, w):
    S = x.shape[0]
    depth = w['w_in'].shape[0]
    tab = _rope_table(S)
    gathered = chip_exchange(_weight_shards(w, 0), False, "gather_weights")
    layer_w, consts, saved = [], [], []
    for l in range(depth):
        lw = _full_weights(gathered, {n: w[n][l] for n in SMALL})
        k = _layer_consts(lw)
        x, s, gathered = _layer_fwd(x, lw, k, tab, _weight_shards(w, l + 1) if l + 1 < depth else ())
        layer_w.append(lw)
        consts.append(k)
        saved.append(s)
    g, sq = loss_head(x, target)
    mine = [None] * depth
    pending = ()
    for l in reversed(range(depth)):
        g, sharded, small, landed = _layer_bwd(g, saved[l], layer_w[l], consts[l], tab, pending, l == 0)
        if pending:
            mine[l + 1] = [add_lead(p) for p in landed]
        pending = list(sharded) + [small]
    mine[0] = [add_lead(p) for p in pending[:-1] + list(chip_exchange(pending[-1:], True, "scatter_small"))]
    n_parts = len(mine[0])
    theirs = sibling_exchange([p for layer in mine for p in layer])
    return sq, g, [(mine[l], theirs[l * n_parts:(l + 1) * n_parts]) for l in range(depth)]


def kernel(x, w_in, pre_g, post_g, sgu_ln_g, sgu_ln_b, sgu_w, sgu_b, mla_q_norm_g, mla_kv_norm_g, mla_w_uq, mla_w_ukv, ca_rel_bias, w_branch, gate_b, w_out, loss_target, m_w_in, m_pre_g, m_post_g, m_sgu_ln_g, m_sgu_ln_b, m_sgu_w, m_sgu_b, m_mla_q_norm_g, m_mla_kv_norm_g, m_mla_w_uq, m_mla_w_ukv, m_ca_rel_bias, m_w_branch, m_gate_b, m_w_out, v_w_in, v_pre_g, v_post_g, v_sgu_ln_g, v_sgu_ln_b, v_sgu_w, v_sgu_b, v_mla_q_norm_g, v_mla_kv_norm_g, v_mla_w_uq, v_mla_w_ukv, v_ca_rel_bias, v_w_branch, v_gate_b, v_w_out):
    w = dict(w_in=w_in, pre_g=pre_g, post_g=post_g, sgu_ln_g=sgu_ln_g, sgu_ln_b=sgu_ln_b, sgu_w=sgu_w, sgu_b=sgu_b,
             mla_q_norm_g=mla_q_norm_g, mla_kv_norm_g=mla_kv_norm_g, mla_w_uq=mla_w_uq, mla_w_ukv=mla_w_ukv,
             ca_rel_bias=ca_rel_bias, w_branch=w_branch, gate_b=gate_b, w_out=w_out)
    m = dict(w_in=m_w_in, pre_g=m_pre_g, post_g=m_post_g, sgu_ln_g=m_sgu_ln_g, sgu_ln_b=m_sgu_ln_b, sgu_w=m_sgu_w,
             sgu_b=m_sgu_b, mla_q_norm_g=m_mla_q_norm_g, mla_kv_norm_g=m_mla_kv_norm_g, mla_w_uq=m_mla_w_uq,
             mla_w_ukv=m_mla_w_ukv, ca_rel_bias=m_ca_rel_bias, w_branch=m_w_branch, gate_b=m_gate_b, w_out=m_w_out)
    v = dict(w_in=v_w_in, pre_g=v_pre_g, post_g=v_post_g, sgu_ln_g=v_sgu_ln_g, sgu_ln_b=v_sgu_ln_b, sgu_w=v_sgu_w,
             sgu_b=v_sgu_b, mla_q_norm_g=v_mla_q_norm_g, mla_kv_norm_g=v_mla_kv_norm_g, mla_w_uq=v_mla_w_uq,
             mla_w_ukv=v_mla_w_ukv, ca_rel_bias=v_ca_rel_bias, w_branch=v_w_branch, gate_b=v_gate_b, w_out=v_w_out)
    depth = w_in.shape[0]
    sq, grad_x, reduced = train_step_local(x[0], loss_target[0], w)
    loss = lax.psum(0.5 * jnp.sum(sq) / D_MODEL, ("x", "y", "c"))

    out = {}
    for a, n in enumerate(SHARDED):
        mine = jnp.stack([reduced[l][0][a] for l in range(depth)])
        theirs = jnp.stack([reduced[l][1][a] for l in range(depth)])
        out[n] = adamw(w[n], [mine, theirs], m[n], v[n])
    small = jnp.stack([jnp.stack([reduced[l][0][-1] for l in range(depth)]),
                       jnp.stack([reduced[l][1][-1] for l in range(depth)])])
    quarter = add_lead(small)
    full = chip_exchange([quarter], False, "gather_small")[0]
    full = jnp.moveaxis(full, 0, 1).reshape(depth, -1)
    off = 0
    for n in SMALL:
        size = int(np.prod(w[n].shape[1:]))
        out[n] = adamw(w[n], [full[:, off:off + size].reshape(w[n].shape)], m[n], v[n])
        off += size
    return (loss, grad_x[None], *[out[n][0] for n in WEIGHTS], *[out[n][1] for n in WEIGHTS],
            *[out[n][2] for n in WEIGHTS], *[out[n][3] for n in WEIGHTS])
```

```python
import numpy as np
import jax
import jax.numpy as jnp
from jax import lax
from jax.experimental import pallas as pl
from jax.experimental.pallas import tpu as pltpu

F32 = jnp.float32
BF16 = jnp.bfloat16
MESH = pl.DeviceIdType.MESH

EPS = 1e-6
NEG_INF = -1e30
D_MODEL = 1024
BR_WIDTH = 512
N_BRANCH = 3
N_HEADS = 8
HEAD_PAD = 128
CHUNK_SHIFT = 6
SGU_BLOCK = 128
MLA_NOPE, MLA_ROPE, MLA_V = 64, 32, 64
MLA_QK = MLA_NOPE + MLA_ROPE
MLA_Q_RANK, MLA_KV_RANK = 256, 128
CA_HEAD_DIM = 64
REL_CLIP = 128
ROPE_BASE = 10000.0
D_IN = 7584

ADAM_LR, ADAM_B1, ADAM_B2, ADAM_EPS, ADAM_WD, ADAM_STEP = 0.001, 0.9, 0.999, 1e-08, 0.01, 10

P_QC, P_KC, P_VC, P_QD, P_KVD, P_KR, P_ZB, P_ZC, P_G, P_U, P_V, P_ZA, P_W = (
    0, 512, 1024, 1536, 1792, 1920, 2048, 2560, 3072, 6144, 6656, 7168, 7680)
NAT_SEGS = [(0, 1536, P_U), (1536, 1920, P_QD), (1920, 1952, P_KR + MLA_NOPE), (1952, 2464, P_ZB),
            (2464, 4000, P_QC), (4000, 4512, P_ZC), (4512, 7584, P_G)]
SHARD_COLS = D_IN // 4

VMEM_LIMIT = 48 * 1024 * 1024
ATT_T = 512
MLA_HEADS_PER_STEP = 2
MLA_FWD_HEADS_PER_STEP = 8
MLA_BWD_HEADS_PER_STEP = 4
ROW_TILE = 256
LIGHT_ROW_TILE = 512
MM_TM = 512
MM_TN = 1536
LOG2E = 1.4426950408889634
MLA_SCALE = MLA_QK ** -0.5
CA_SCALE = CA_HEAD_DIM ** -0.5

WEIGHTS = ['w_in', 'pre_g', 'post_g', 'sgu_ln_g', 'sgu_ln_b', 'sgu_w', 'sgu_b', 'mla_q_norm_g',
           'mla_kv_norm_g', 'mla_w_uq', 'mla_w_ukv', 'ca_rel_bias', 'w_branch', 'gate_b', 'w_out']
SHARDED = ['w_in', 'mla_w_uq', 'mla_w_ukv', 'w_branch', 'gate_b', 'w_out']
SMALL = ['pre_g', 'post_g', 'sgu_ln_g', 'sgu_ln_b', 'sgu_w', 'sgu_b', 'mla_q_norm_g',
         'mla_kv_norm_g', 'ca_rel_bias']
SHARD_AXIS = {'w_in': 1, 'mla_w_uq': 1, 'mla_w_ukv': 1, 'w_branch': 2, 'gate_b': 1, 'w_out': 0}


def _call(body, **kw):
    return pl.pallas_call(body, **kw)


def _params(n_axes):
    return pltpu.CompilerParams(dimension_semantics=("arbitrary",) * n_axes,
                                vmem_limit_bytes=VMEM_LIMIT)


def _nt(a, b):
    return lax.dot_general(a, b, (((1,), (1,)), ((), ())), preferred_element_type=F32)


def _nn(a, b):
    return jnp.dot(a, b, preferred_element_type=F32)


def _tn(a, b):
    return lax.dot_general(a, b, (((0,), (0,)), ((), ())), preferred_element_type=F32)


def _rms(xv, g):
    r = lax.rsqrt(jnp.mean(xv * xv, axis=-1, keepdims=True) + EPS)
    return xv * r * g, r


def _rms_bwd(xv, g, r, dy):
    gy = dy * g
    dx = r * gy - xv * (r * r * r) * jnp.mean(xv * gy, axis=-1, keepdims=True)
    dg = jnp.sum(dy * (xv * r), axis=0, keepdims=True)
    return dx, dg


def _sigmoid(z):
    return 1.0 / (1.0 + jnp.exp(-z))


def _rope(xv, c, a, b):
    return xv * c + pltpu.roll(xv, 112, 1) * a + pltpu.roll(xv, 16, 1) * b


def _accumulate(ref, val, first):
    @pl.when(first)
    def _():
        ref[...] = val

    @pl.when(jnp.logical_not(first))
    def _():
        ref[...] += val


def norm_matmul(x, g, w):
    S, D = x.shape
    N = w.shape[1]
    tm, tn = min(S, 2 * MM_TM), MM_TN

    def body(x_ref, g_ref, w_ref, o_ref, xn_ref):
        @pl.when(pl.program_id(1) == 0)
        def _():
            y, _ = _rms(x_ref[...], g_ref[...])
            xn_ref[...] = y.astype(BF16)

        o_ref[...] = _nn(xn_ref[...], w_ref[...])

    return _call(
        body, name="norm_matmul", grid=(S // tm, N // tn),
        in_specs=[pl.BlockSpec((tm, D), lambda i, j: (i, 0)),
                  pl.BlockSpec((1, D), lambda i, j: (0, 0)),
                  pl.BlockSpec((D, tn), lambda i, j: (0, j))],
        out_specs=[pl.BlockSpec((tm, tn), lambda i, j: (i, j)),
                   pl.BlockSpec((tm, D), lambda i, j: (i, 0))],
        out_shape=[jax.ShapeDtypeStruct((S, N), F32), jax.ShapeDtypeStruct((S, D), BF16)],
        compiler_params=_params(2))(x, g, w)


def proj_bwd_x(dproj, w, x, g, resid, exchange=()):
    S, N = dproj.shape
    D = x.shape[1]
    tm, tk = min(S, MM_TM), MM_TN
    nk, n_ex = N // tk, len(exchange)

    def body(dp_ref, w_ref, x_ref, g_ref, r_ref, *rest):
        ex_src, (dx_ref, dg_ref), ex_out, (acc_ref,), ex_sems = _split_refs(
            rest, (n_ex, 2, n_ex, 1, 3 if n_ex else 0))
        i, k = pl.program_id(0), pl.program_id(1)

        if n_ex:
            @pl.when(jnp.logical_and(i == 0, k == 0))
            def _():
                _exchange_issue(ex_src, ex_out, ex_sems, True, True)

        @pl.when(k == 0)
        def _():
            acc_ref[...] = jnp.zeros_like(acc_ref)

        acc_ref[...] += _nt(dp_ref[...].astype(BF16), w_ref[...])

        @pl.when(k == nk - 1)
        def _():
            xv = x_ref[...]
            _, r = _rms(xv, g_ref[...])
            dx, dg = _rms_bwd(xv, g_ref[...], r, acc_ref[...])
            dx_ref[...] = dx + r_ref[...]
            _accumulate(dg_ref, dg, i == 0)

        if n_ex:
            @pl.when(jnp.logical_and(i == S // tm - 1, k == nk - 1))
            def _():
                _exchange_issue(ex_src, ex_out, ex_sems, True, False)

    outs = _call(
        body, name="proj_bwd_x_scatter" if n_ex else "proj_bwd_x", grid=(S // tm, nk),
        in_specs=[pl.BlockSpec((tm, tk), lambda i, k: (i, k)),
                  pl.BlockSpec((D, tk), lambda i, k: (0, k)),
                  pl.BlockSpec((tm, D), lambda i, k: (i, 0)),
                  pl.BlockSpec((1, D), lambda i, k: (0, 0)),
                  pl.BlockSpec((tm, D), lambda i, k: (i, 0))] + [ANY] * n_ex,
        out_specs=[pl.BlockSpec((tm, D), lambda i, k: (i, 0)),
                   pl.BlockSpec((1, D), lambda i, k: (0, 0))] + [ANY] * n_ex,
        out_shape=[jax.ShapeDtypeStruct((S, D), F32), jax.ShapeDtypeStruct((1, D), F32)] +
        _exchange_out_shape(exchange, True),
        scratch_shapes=[pltpu.VMEM((tm, D), F32)] + _exchange_sems(n_ex),
        compiler_params=_params(2))(dproj, w, x, g, resid, *exchange)
    return outs[0], outs[1], outs[2:]


def matmul_tn(a, b, tn):
    S, M = a.shape
    N = b.shape[1]
    tk = min(S, 2 * MM_TM)

    def body(a_ref, b_ref, o_ref):
        @pl.when(pl.program_id(1) == 0)
        def _():
            o_ref[...] = jnp.zeros_like(o_ref)

        o_ref[...] += _tn(a_ref[...].astype(BF16), b_ref[...].astype(BF16))

    return _call(
        body, name="matmul_tn", grid=(N // tn, S // tk),
        in_specs=[pl.BlockSpec((tk, M), lambda j, k: (k, 0)),
                  pl.BlockSpec((tk, tn), lambda j, k: (k, j))],
        out_specs=pl.BlockSpec((M, tn), lambda j, k: (0, j)),
        out_shape=jax.ShapeDtypeStruct((M, N), F32),
        compiler_params=_params(2))(a, b)


def _sgu_block(vv, g, b, ws_ref, lane):
    mu = jnp.mean(vv, axis=-1, keepdims=True)
    xc = vv - mu
    r = lax.rsqrt(jnp.mean(xc * xc, axis=-1, keepdims=True) + EPS)
    xhat = xc * r
    vln = (xhat * g + b).astype(BF16)
    pieces = []
    for p in range(4):
        vp = vln[:, p * 128:(p + 1) * 128]
        pieces.append(jnp.where(lane < 64, _nn(ws_ref[2 * p], vp), _nn(ws_ref[2 * p + 1], vp)))
    return xhat, r, vln, jnp.concatenate(pieces, axis=1)


def sgu_fwd(proj, ln_g, ln_b, ws, bias_full):
    S = proj.shape[0]
    ts = LIGHT_ROW_TILE

    def body(u_ref, v_ref, z_ref, g_ref, b_ref, ws_ref, bf_ref, y_ref):
        lane = lax.broadcasted_iota(jnp.int32, (SGU_BLOCK, 128), 1)
        for blk in range(ts // SGU_BLOCK):
            rows = slice(blk * SGU_BLOCK, (blk + 1) * SGU_BLOCK)
            _, _, _, mixed = _sgu_block(v_ref[rows, :], g_ref[...], b_ref[...], ws_ref, lane)
            mixed = mixed + bf_ref[...]
            zz = z_ref[rows, :]
            y_ref[rows, :] = (u_ref[rows, :] * mixed * (zz * _sigmoid(zz))).astype(BF16)

    col = lambda c: pl.BlockSpec((ts, BR_WIDTH), lambda i: (i, c))
    full = lambda shape: pl.BlockSpec(shape, lambda i: (0,) * len(shape))
    return _call(
        body, name="sgu_fwd", grid=(S // ts,),
        in_specs=[col(P_U // 512), col(P_V // 512), col(P_ZA // 512),
                  full((1, BR_WIDTH)), full((1, BR_WIDTH)), full((8, 128, 128)), full((128, BR_WIDTH))],
        out_specs=pl.BlockSpec((ts, BR_WIDTH), lambda i: (i, 0)),
        out_shape=jax.ShapeDtypeStruct((S, BR_WIDTH), BF16),
        compiler_params=_params(1))(proj, proj, proj, ln_g, ln_b, ws, bias_full)


def sgu_bwd(dproj, dy, proj, ln_g, ln_b, ws, ws_t, bias_full):
    S = proj.shape[0]
    ts = LIGHT_ROW_TILE

    def body(dp_in, dy_ref, u_ref, v_ref, z_ref, g_ref, b_ref, ws_ref, wst_ref, bf_ref,
             dp_ref, gg_ref, gb_ref, gws_ref, gbf_ref):
        del dp_in
        first = pl.program_id(0) == 0

        @pl.when(first)
        def _():
            gg_ref[...] = jnp.zeros_like(gg_ref)
            gb_ref[...] = jnp.zeros_like(gb_ref)
            gws_ref[...] = jnp.zeros_like(gws_ref)
            gbf_ref[...] = jnp.zeros_like(gbf_ref)

        lane = lax.broadcasted_iota(jnp.int32, (SGU_BLOCK, 128), 1)
        for blk in range(ts // SGU_BLOCK):
            rows = slice(blk * SGU_BLOCK, (blk + 1) * SGU_BLOCK)
            g = g_ref[...]
            xhat, r, vln, mixed = _sgu_block(v_ref[rows, :], g, b_ref[...], ws_ref, lane)
            mixed = mixed + bf_ref[...]
            zz = z_ref[rows, :]
            uu = u_ref[rows, :]
            dyv = dy_ref[0, rows, :]
            sg = _sigmoid(zz)
            sil = zz * sg
            dmixed = dyv * uu * sil
            dp_ref[rows, 0:512] = (dyv * mixed * sil).astype(BF16)
            dp_ref[rows, 1024:1536] = (dyv * uu * mixed * (sg * (1.0 + zz * (1.0 - sg)))).astype(BF16)
            gbf_ref[...] += dmixed
            dmb = dmixed.astype(BF16)
            pieces = []
            for p in range(4):
                dmp = dmb[:, p * 128:(p + 1) * 128]
                vp = vln[:, p * 128:(p + 1) * 128]
                pieces.append(jnp.where(lane < 64, _nn(wst_ref[2 * p], dmp), _nn(wst_ref[2 * p + 1], dmp)))
                zero = jnp.zeros_like(dmp)
                gws_ref[2 * p] += _nt(jnp.where(lane < 64, dmp, zero), vp)
                gws_ref[2 * p + 1] += _nt(jnp.where(lane >= 64, dmp, zero), vp)
            dvln = jnp.concatenate(pieces, axis=1)
            dxh = dvln * g
            dp_ref[rows, 512:1024] = (r * (dxh - jnp.mean(dxh, axis=-1, keepdims=True)
                                           - xhat * jnp.mean(dxh * xhat, axis=-1, keepdims=True))).astype(BF16)
            gg_ref[...] += jnp.sum(dvln * xhat, axis=0, keepdims=True)
            gb_ref[...] += jnp.sum(dvln, axis=0, keepdims=True)

    col = lambda c: pl.BlockSpec((ts, BR_WIDTH), lambda i: (i, c))
    full = lambda shape: pl.BlockSpec(shape, lambda i: (0,) * len(shape))
    return _call(
        body, name="sgu_bwd", grid=(S // ts,),
        in_specs=[pl.BlockSpec(memory_space=pl.ANY),
                  pl.BlockSpec((1, ts, BR_WIDTH), lambda i: (0, i, 0)),
                  col(P_U // 512), col(P_V // 512), col(P_ZA // 512),
                  full((1, BR_WIDTH)), full((1, BR_WIDTH)), full((8, 128, 128)), full((8, 128, 128)),
                  full((128, BR_WIDTH))],
        out_specs=[pl.BlockSpec((ts, 1536), lambda i: (i, P_U // 1536)),
                   full((1, BR_WIDTH)), full((1, BR_WIDTH)), full((8, 128, 128)), full((128, BR_WIDTH))],
        out_shape=[jax.ShapeDtypeStruct(dproj.shape, BF16),
                   jax.ShapeDtypeStruct((1, BR_WIDTH), F32), jax.ShapeDtypeStruct((1, BR_WIDTH), F32),
                   jax.ShapeDtypeStruct((8, 128, 128), F32), jax.ShapeDtypeStruct((128, BR_WIDTH), F32)],
        input_output_aliases={0: 0},
        compiler_params=_params(1))(dproj, dy, proj, proj, proj, ln_g, ln_b, ws, ws_t, bias_full)


def _hspec(ts):
    return pl.BlockSpec((N_HEADS, ts, HEAD_PAD), lambda i: (0, i, 0))


def prep_fwd(proj, tab, qg, kvg, wq, wk, wv):
    S = proj.shape[0]
    ts = LIGHT_ROW_TILE

    def body(qc_ref, kc_ref, vc_ref, qd_ref, kvd_ref, kr_ref, tab_ref, qg_ref, kvg_ref,
             wq_ref, wk_ref, wv_ref, qb, kb, vb, qc, kc, vc, kbt, vbt, kct, vct, cq_o, ckv_o):
        c, a, b = tab_ref[0], tab_ref[1], tab_ref[2]
        cq, _ = _rms(qd_ref[...], qg_ref[...])
        ckv, _ = _rms(kvd_ref[...], kvg_ref[...])
        cqb, ckvb = cq.astype(BF16), ckv.astype(BF16)
        cq_o[...] = cqb
        ckv_o[...] = ckvb
        krr = _rope(kr_ref[...], c, a, b)
        lane = lax.broadcasted_iota(jnp.int32, (ts, 128), 1)
        ones_lane = jnp.where(lane == MLA_V, 1.0, 0.0)
        for h in range(N_HEADS):
            cols = slice(h * HEAD_PAD, (h + 1) * HEAD_PAD)
            qb[h] = (_rope(_nn(cqb, wq_ref[:, cols]), c, a, b) * (MLA_SCALE * LOG2E)).astype(BF16)
            kh = _nn(ckvb, wk_ref[:, cols]) + krr
            vh = _nn(ckvb, wv_ref[:, cols]) + ones_lane
            kb[h], kbt[h] = kh.astype(BF16), kh.T.astype(BF16)
            vb[h], vbt[h] = vh.astype(BF16), vh.T.astype(BF16)
        for p in range(4):
            piece = qc_ref[:, p * 128:(p + 1) * 128] * (CA_SCALE * LOG2E)
            qc[2 * p] = jnp.where(lane < 64, piece, 0.0).astype(BF16)
            qc[2 * p + 1] = jnp.where(lane < 64, pltpu.roll(piece, 64, 1), 0.0).astype(BF16)
            for src, dst, dst_t, pad in ((kc_ref, kc, kct, 0.0), (vc_ref, vc, vct, ones_lane)):
                piece = src[:, p * 128:(p + 1) * 128]
                for h, head in ((2 * p, jnp.where(lane < 64, piece, pad)),
                                (2 * p + 1, jnp.where(lane < 64, pltpu.roll(piece, 64, 1), pad))):
                    dst[h], dst_t[h] = head.astype(BF16), head.T.astype(BF16)

    col = lambda w, c: pl.BlockSpec((ts, w), lambda i: (i, c))
    full = lambda shape: pl.BlockSpec(shape, lambda i: (0,) * len(shape))
    hshape = jax.ShapeDtypeStruct((N_HEADS, S, HEAD_PAD), BF16)
    tshape = jax.ShapeDtypeStruct((N_HEADS, HEAD_PAD, S), BF16)
    tspec = pl.BlockSpec((N_HEADS, HEAD_PAD, ts), lambda i: (0, 0, i))
    return _call(
        body, name="prep_fwd", grid=(S // ts,),
        in_specs=[col(512, P_QC // 512), col(512, P_KC // 512), col(512, P_VC // 512),
                  col(256, P_QD // 256), col(128, P_KVD // 128), col(128, P_KR // 128),
                  pl.BlockSpec((3, ts, 128), lambda i: (0, i, 0)),
                  full((1, MLA_Q_RANK)), full((1, MLA_KV_RANK)),
                  full((MLA_Q_RANK, 1024)), full((MLA_KV_RANK, 1024)), full((MLA_KV_RANK, 1024))],
        out_specs=[_hspec(ts)] * 6 + [tspec] * 4 + [pl.BlockSpec((ts, MLA_Q_RANK), lambda i: (i, 0)),
                                                    pl.BlockSpec((ts, MLA_KV_RANK), lambda i: (i, 0))],
        out_shape=[hshape] * 6 + [tshape] * 4 + [jax.ShapeDtypeStruct((S, MLA_Q_RANK), BF16),
                                                 jax.ShapeDtypeStruct((S, MLA_KV_RANK), BF16)],
        compiler_params=_params(1))(proj, proj, proj, proj, proj, proj, tab, qg, kvg, wq, wk, wv)


def prep_bwd(dproj, dqb, dkb, dvb, dqc, dkc, dvc, proj, tab, qg, kvg, wq, wk, wv):
    S = proj.shape[0]
    ts = LIGHT_ROW_TILE

    def body(dp_in, dqb_r, dkb_r, dvb_r, dqc_r, dkc_r, dvc_r, qd_ref, kvd_ref, tab_ref, qg_ref, kvg_ref,
             wq_ref, wk_ref, wv_ref, dp_ref, dqf, dkf, dvf, gq_ref, gkv_ref):
        del dp_in
        c, a, b = tab_ref[0], -tab_ref[1], -tab_ref[2]
        qd, kvd = qd_ref[...], kvd_ref[...]
        _, rq = _rms(qd, qg_ref[...])
        _, rkv = _rms(kvd, kvg_ref[...])
        dcq = jnp.zeros((ts, MLA_Q_RANK), F32)
        dckv = jnp.zeros((ts, MLA_KV_RANK), F32)
        dksum = jnp.zeros((ts, HEAD_PAD), F32)
        for h in range(N_HEADS):
            cols = slice(h * HEAD_PAD, (h + 1) * HEAD_PAD)
            dqh = _rope(dqb_r[h].astype(F32) * MLA_SCALE, c, a, b).astype(BF16)
            dqf[:, cols] = dqh
            dcq = dcq + _nt(dqh, wq_ref[:, cols])
            dk = dkb_r[h].astype(F32) * (1.0 / LOG2E)
            dksum = dksum + dk
            dkh = dk.astype(BF16)
            dkf[:, cols] = dkh
            dvh = dvb_r[h].astype(BF16)
            dvf[:, cols] = dvh
            dckv = dckv + _nt(dkh, wk_ref[:, cols]) + _nt(dvh, wv_ref[:, cols])
        lane = lax.broadcasted_iota(jnp.int32, (ts, 128), 1)
        rope_lanes = jnp.logical_and(lane >= MLA_NOPE, lane < MLA_QK)
        dp_ref[:, P_KR:P_KR + 128] = jnp.where(rope_lanes, _rope(dksum, c, a, b), 0.0).astype(BF16)
        dqd, gq = _rms_bwd(qd, qg_ref[...], rq, dcq)
        dkvd, gkv = _rms_bwd(kvd, kvg_ref[...], rkv, dckv)
        dp_ref[:, P_QD:P_QD + 256] = dqd.astype(BF16)
        dp_ref[:, P_KVD:P_KVD + 128] = dkvd.astype(BF16)
        first = pl.program_id(0) == 0
        _accumulate(gq_ref, gq, first)
        _accumulate(gkv_ref, gkv, first)
        for src, base, factor in ((dqc_r, P_QC, CA_SCALE), (dkc_r, P_KC, 1.0 / LOG2E), (dvc_r, P_VC, 1.0)):
            for p in range(4):
                dp_ref[:, base + p * 128:base + (p + 1) * 128] = (
                    (src[2 * p].astype(F32) + pltpu.roll(src[2 * p + 1].astype(F32), 64, 1)) * factor).astype(BF16)

    col = lambda w, c: pl.BlockSpec((ts, w), lambda i: (i, c))
    full = lambda shape: pl.BlockSpec(shape, lambda i: (0,) * len(shape))
    wide = jax.ShapeDtypeStruct((S, 1024), BF16)
    return _call(
        body, name="prep_bwd", grid=(S // ts,),
        in_specs=[pl.BlockSpec(memory_space=pl.ANY)] + [_hspec(ts)] * 6 +
                 [col(256, P_QD // 256), col(128, P_KVD // 128),
                  pl.BlockSpec((3, ts, 128), lambda i: (0, i, 0)),
                  full((1, MLA_Q_RANK)), full((1, MLA_KV_RANK)),
                  full((MLA_Q_RANK, 1024)), full((MLA_KV_RANK, 1024)), full((MLA_KV_RANK, 1024))],
        out_specs=[pl.BlockSpec((ts, 2048), lambda i: (i, 0))] + [pl.BlockSpec((ts, 1024), lambda i: (i, 0))] * 3 +
                  [full((1, MLA_Q_RANK)), full((1, MLA_KV_RANK))],
        out_shape=[jax.ShapeDtypeStruct(dproj.shape, BF16), wide, wide, wide,
                   jax.ShapeDtypeStruct((1, MLA_Q_RANK), F32), jax.ShapeDtypeStruct((1, MLA_KV_RANK), F32)],
        input_output_aliases={0: 0},
        compiler_params=_params(1))(dproj, dqb, dkb, dvb, dqc, dkc, dvc, proj, proj, tab, qg, kvg, wq, wk, wv)


def _diag_visible(t):
    r = lax.broadcasted_iota(jnp.int32, (t, t), 0) >> CHUNK_SHIFT
    c = lax.broadcasted_iota(jnp.int32, (t, t), 1) >> CHUNK_SHIFT
    return r <= c


def _pair_tables(nq, kv_major):
    if kv_major:
        pairs = [(kb, qi) for kb in range(nq) for qi in range(kb, nq)]
    else:
        pairs = [(kb, qi) for qi in range(nq) for kb in range(qi + 1)]
    return (jnp.asarray(np.array([p[0] for p in pairs], np.int32)),
            jnp.asarray(np.array([p[1] for p in pairs], np.int32)), len(pairs))


def _finish_softmax(acc, m):
    l = acc[MLA_V:MLA_V + 1, :]
    row = lax.broadcasted_iota(jnp.int32, acc.shape, 0)
    return jnp.where(row < MLA_V, acc / l, 0.0).T.astype(BF16), m + jnp.log2(l)


def _split_refs(refs, counts):
    out, pos = [], 0
    for c in counts:
        out.append(refs[pos:pos + c])
        pos += c
    return out


def mla_fwd(q, k, vt, exchange=()):
    H, S, _ = q.shape
    t, hb, n_ex = ATT_T, MLA_FWD_HEADS_PER_STEP, len(exchange)
    kb_tab, qi_tab, n_pairs = _pair_tables(S // t, False)

    def body(kb_ref, qi_ref, q_ref, k_ref, vt_ref, *rest):
        ex_src, (o_ref, lse_ref), ex_out, (m_s, acc_s), ex_sems = _split_refs(rest, (n_ex, 2, n_ex, 2, 3 if n_ex else 0))
        hg, p_id = pl.program_id(0), pl.program_id(1)
        kb, qi = kb_ref[p_id], qi_ref[p_id]

        if n_ex:
            @pl.when(jnp.logical_and(hg == 0, p_id == 0))
            def _():
                _exchange_issue(ex_src, ex_out, ex_sems, False, True)

        @pl.when(kb == 0)
        def _():
            m_s[...] = jnp.full_like(m_s, NEG_INF)
            acc_s[...] = jnp.zeros_like(acc_s)

        def step(masked):
            for h in range(hb):
                st = _nt(k_ref[h], q_ref[h])
                if masked:
                    st = jnp.where(_diag_visible(t), st, NEG_INF)
                m_prev = m_s[h]
                m_new = jnp.maximum(m_prev, jnp.max(st, axis=0, keepdims=True))
                p = jnp.exp2(st - m_new)
                acc_s[h] = jnp.exp2(m_prev - m_new) * acc_s[h] + _nn(vt_ref[h], p.astype(BF16))
                m_s[h] = m_new

        @pl.when(kb < qi)
        def _():
            step(False)

        @pl.when(kb == qi)
        def _():
            step(True)
            for h in range(hb):
                o_ref[h], lse_ref[h] = _finish_softmax(acc_s[h], m_s[h])

        if n_ex:
            @pl.when(jnp.logical_and(hg == H // hb - 1, p_id == n_pairs - 1))
            def _():
                _exchange_issue(ex_src, ex_out, ex_sems, False, False)

    grid_spec = pltpu.PrefetchScalarGridSpec(
        num_scalar_prefetch=2, grid=(H // hb, n_pairs),
        in_specs=[pl.BlockSpec((hb, t, HEAD_PAD), lambda h, p, kb, qi: (h, qi[p], 0)),
                  pl.BlockSpec((hb, t, HEAD_PAD), lambda h, p, kb, qi: (h, kb[p], 0)),
                  pl.BlockSpec((hb, HEAD_PAD, t), lambda h, p, kb, qi: (h, 0, kb[p]))] + [ANY] * n_ex,
        out_specs=[pl.BlockSpec((hb, t, HEAD_PAD), lambda h, p, kb, qi: (h, qi[p], 0)),
                   pl.BlockSpec((hb, 1, t), lambda h, p, kb, qi: (h, 0, qi[p]))] + [ANY] * n_ex,
        scratch_shapes=[pltpu.VMEM((hb, 1, t), F32), pltpu.VMEM((hb, HEAD_PAD, t), F32)] + _exchange_sems(n_ex))
    outs = _call(
        body, name="mla_fwd_gather" if n_ex else "mla_fwd", grid_spec=grid_spec,
        out_shape=[jax.ShapeDtypeStruct((H, S, HEAD_PAD), BF16), jax.ShapeDtypeStruct((H, 1, S), F32)] +
        _exchange_out_shape(exchange, False),
        compiler_params=_params(2))(kb_tab, qi_tab, q, k, vt, *exchange)
    return outs[0], outs[1], outs[2:]


def mla_bwd(q, k, kt, v, do, lse, delta, exchange=()):
    H, S, _ = q.shape
    t, hb, n_ex = ATT_T, MLA_BWD_HEADS_PER_STEP, len(exchange)
    nq = S // t
    kb_tab, qi_tab, n_pairs = _pair_tables(nq, True)

    def body(kb_ref, qi_ref, q_ref, k_ref, kt_ref, v_ref, do_ref, lse_ref, dl_ref, *rest):
        ex_src, (dq_ref, dk_ref, dv_ref), ex_out, (dqt_s, dk_s, dv_s), ex_sems = _split_refs(
            rest, (n_ex, 3, n_ex, 3, 3 if n_ex else 0))
        hg, p_id = pl.program_id(0), pl.program_id(1)
        kb, qi = kb_ref[p_id], qi_ref[p_id]

        if n_ex:
            @pl.when(jnp.logical_and(hg == 0, p_id == 0))
            def _():
                _exchange_issue(ex_src, ex_out, ex_sems, True, True)

        @pl.when(p_id == 0)
        def _():
            dqt_s[...] = jnp.zeros_like(dqt_s)

        @pl.when(qi == kb)
        def _():
            dk_s[...] = jnp.zeros_like(dk_s)
            dv_s[...] = jnp.zeros_like(dv_s)

        def step(masked):
            for h in range(hb):
                st = _nt(k_ref[h], q_ref[h])
                if masked:
                    st = jnp.where(_diag_visible(t), st, NEG_INF)
                pt = jnp.exp2(st - lse_ref[h])
                dv_s[h] += _nn(pt.astype(BF16), do_ref[h])
                dsb = (pt * (_nt(v_ref[h], do_ref[h]) - dl_ref[h])).astype(BF16)
                dk_s[h] += _nn(dsb, q_ref[h])
                dqt_s[h, qi] += _nn(kt_ref[h], dsb)

        @pl.when(qi == kb)
        def _():
            step(True)
            rows = pl.ds(pl.multiple_of(qi * t, t), t)
            for h in range(hb):
                dq_ref[h, rows, :] = dqt_s[h, qi].T.astype(BF16)

        @pl.when(qi > kb)
        def _():
            step(False)

        @pl.when(qi == nq - 1)
        def _():
            dk_ref[...] = dk_s[...].astype(BF16)
            dv_ref[...] = dv_s[...].astype(BF16)

        if n_ex:
            @pl.when(jnp.logical_and(hg == H // hb - 1, p_id == n_pairs - 1))
            def _():
                _exchange_issue(ex_src, ex_out, ex_sems, True, False)

    qtile = pl.BlockSpec((hb, t, HEAD_PAD), lambda h, p, kb, qi: (h, qi[p], 0))
    ktile = pl.BlockSpec((hb, t, HEAD_PAD), lambda h, p, kb, qi: (h, kb[p], 0))
    stat = pl.BlockSpec((hb, 1, t), lambda h, p, kb, qi: (h, 0, qi[p]))
    grid_spec = pltpu.PrefetchScalarGridSpec(
        num_scalar_prefetch=2, grid=(H // hb, n_pairs),
        in_specs=[qtile, ktile, pl.BlockSpec((hb, HEAD_PAD, t), lambda h, p, kb, qi: (h, 0, kb[p])), ktile, qtile,
                  stat, stat] + [ANY] * n_ex,
        out_specs=[pl.BlockSpec((hb, S, HEAD_PAD), lambda h, p, kb, qi: (h, 0, 0)), ktile, ktile] + [ANY] * n_ex,
        scratch_shapes=[pltpu.VMEM((hb, nq, HEAD_PAD, t), F32), pltpu.VMEM((hb, t, HEAD_PAD), F32),
                        pltpu.VMEM((hb, t, HEAD_PAD), F32)] + _exchange_sems(n_ex))
    outs = _call(
        body, name="mla_bwd_scatter" if n_ex else "mla_bwd", grid_spec=grid_spec,
        out_shape=[jax.ShapeDtypeStruct((H, S, HEAD_PAD), BF16)] * 3 + _exchange_out_shape(exchange, True),
        compiler_params=_params(2))(kb_tab, qi_tab, q, k, kt, v, do, lse, delta, *exchange)
    return outs[0], outs[1], outs[2], outs[3:]


def _band_specs(t, hb):
    prev = lambda i: jnp.maximum(i - 1, 0)
    return dict(
        cur=pl.BlockSpec((hb, t, HEAD_PAD), lambda h, i: (h, i, 0)),
        prev=pl.BlockSpec((hb, t, HEAD_PAD), lambda h, i: (h, prev(i), 0)),
        cur_t=pl.BlockSpec((hb, HEAD_PAD, t), lambda h, i: (h, 0, i)),
        prev_t=pl.BlockSpec((hb, HEAD_PAD, t), lambda h, i: (h, 0, prev(i))),
        stat=pl.BlockSpec((hb, 1, t), lambda h, i: (h, 0, i)),
        bias_prev=pl.BlockSpec((hb, 1, t, t), lambda h, i: (h, jnp.where(i == 0, 1, 0), 0, 0)),
        bias_cur=pl.BlockSpec((hb, 1, t, t), lambda h, i: (h, 2, 0, 0)))


def band_fwd(q, k, vt, bias):
    H, S, _ = q.shape
    t, hb = ATT_T, MLA_HEADS_PER_STEP
    sp = _band_specs(t, hb)

    def body(q_ref, kp_ref, kc_ref, vtp_ref, vtc_ref, bp_ref, bc_ref, o_ref, lse_ref):
        for h in range(hb):
            s0 = _nt(kp_ref[h], q_ref[h]) + bp_ref[h, 0]
            s1 = _nt(kc_ref[h], q_ref[h]) + bc_ref[h, 0]
            m = jnp.maximum(jnp.max(s0, axis=0, keepdims=True), jnp.max(s1, axis=0, keepdims=True))
            ot = (_nn(vtp_ref[h], jnp.exp2(s0 - m).astype(BF16)) +
                  _nn(vtc_ref[h], jnp.exp2(s1 - m).astype(BF16)))
            o_ref[h], lse_ref[h] = _finish_softmax(ot, m)

    return _call(
        body, name="band_fwd", grid=(H // hb, S // t),
        in_specs=[sp['cur'], sp['prev'], sp['cur'], sp['prev_t'], sp['cur_t'], sp['bias_prev'], sp['bias_cur']],
        out_specs=[sp['cur'], sp['stat']],
        out_shape=[jax.ShapeDtypeStruct((H, S, HEAD_PAD), BF16), jax.ShapeDtypeStruct((H, 1, S), F32)],
        compiler_params=_params(2))(q, k, k, vt, vt, bias, bias)


def band_bwd(q, k, kt, v, do, lse, delta, bias):
    H, S, _ = q.shape
    t = ATT_T
    sp = _band_specs(t, 1)

    def body(q_ref, kp_ref, kc_ref, ktp_ref, ktc_ref, vp_ref, vc_ref, do_ref, lse_ref, dl_ref, bp_ref, bc_ref,
             dq_ref, dk_ref, dv_ref, db_ref):
        i = pl.program_id(1)

        @pl.when(i == 0)
        def _():
            dk_ref[...] = jnp.zeros_like(dk_ref)
            dv_ref[...] = jnp.zeros_like(dv_ref)
            db_ref[...] = jnp.zeros_like(db_ref)

        qv, dov = q_ref[0], do_ref[0]
        dqt = jnp.zeros((HEAD_PAD, t), F32)
        windows = ((0, jnp.maximum(i - 1, 0), kp_ref, ktp_ref, vp_ref, bp_ref),
                   (1, i, kc_ref, ktc_ref, vc_ref, bc_ref))
        for w, blk, k_ref, kt_ref, v_ref, b_ref in windows:
            rows = pl.ds(pl.multiple_of(blk * t, t), t)
            pt = jnp.exp2(_nt(k_ref[0], qv) + b_ref[0, 0] - lse_ref[0])
            dv_ref[0, rows, :] += _nn(pt.astype(BF16), dov)
            ds = pt * (_nt(v_ref[0], dov) - dl_ref[0])
            db_ref[0, w] += ds
            dsb = ds.astype(BF16)
            dk_ref[0, rows, :] += _nn(dsb, qv)
            dqt = dqt + _nn(kt_ref[0], dsb)
        dq_ref[0] = dqt.T.astype(BF16)

    whole = pl.BlockSpec((1, S, HEAD_PAD), lambda h, i: (h, 0, 0))
    return _call(
        body, name="band_bwd", grid=(H, S // t),
        in_specs=[sp['cur'], sp['prev'], sp['cur'], sp['prev_t'], sp['cur_t'], sp['prev'], sp['cur'], sp['cur'],
                  sp['stat'], sp['stat'], sp['bias_prev'], sp['bias_cur']],
        out_specs=[sp['cur'], whole, whole, pl.BlockSpec((1, 2, t, t), lambda h, i: (h, 0, 0, 0))],
        out_shape=[jax.ShapeDtypeStruct((H, S, HEAD_PAD), BF16), jax.ShapeDtypeStruct((H, S, HEAD_PAD), F32),
                   jax.ShapeDtypeStruct((H, S, HEAD_PAD), F32), jax.ShapeDtypeStruct((H, 2, t, t), F32)],
        compiler_params=_params(2))(q, k, k, kt, kt, v, v, do, lse, delta, bias, bias)


def _compact(o_ref):
    return jnp.concatenate([o_ref[2 * p].astype(F32) + pltpu.roll(o_ref[2 * p + 1].astype(F32), 64, 1)
                            for p in range(4)], axis=1)


def merge_fwd(ob, oc, proj, ya, gate_b, wbr, w_out, x, post_g):
    S = x.shape[0]
    ts = ROW_TILE

    def body(ob_ref, oc_ref, zb_ref, zc_ref, ya_ref, gl_ref, gb_ref, wbr_ref, wo_ref, x_ref, pg_ref,
             xo_ref, yb_ref, yc_ref, mg_ref, out_ref):
        zb, zc = zb_ref[...], zc_ref[...]
        yb = (_compact(ob_ref) * (zb * _sigmoid(zb))).astype(BF16)
        yc = (_compact(oc_ref) * (zc * _sigmoid(zc))).astype(BF16)
        yb_ref[...] = yb
        yc_ref[...] = yc
        merged = jnp.zeros((ts, D_MODEL), F32)
        for n, y in enumerate((ya_ref[...], yb, yc)):
            cols = slice(n * D_MODEL, (n + 1) * D_MODEL)
            gate = _sigmoid(gl_ref[:, cols] + gb_ref[:, cols])
            merged = merged + gate * _nn(y, wbr_ref[n])
        mb = merged.astype(BF16)
        mg_ref[...] = mb
        out = _nn(mb, wo_ref[...])
        out_ref[...] = out
        normed, _ = _rms(out, pg_ref[...])
        xo_ref[...] = x_ref[...] + normed

    row = lambda w: pl.BlockSpec((ts, w), lambda i: (i, 0))
    col = lambda w, c: pl.BlockSpec((ts, w), lambda i: (i, c))
    full = lambda shape: pl.BlockSpec(shape, lambda i: (0,) * len(shape))
    return _call(
        body, name="merge_fwd", grid=(S // ts,),
        in_specs=[_hspec(ts), _hspec(ts), col(512, P_ZB // 512), col(512, P_ZC // 512), row(512),
                  col(3072, P_G // 3072), full((1, 3072)), full((3, BR_WIDTH, D_MODEL)),
                  full((D_MODEL, D_MODEL)), row(D_MODEL), full((1, D_MODEL))],
        out_specs=[row(D_MODEL), row(512), row(512), row(D_MODEL), row(D_MODEL)],
        out_shape=[jax.ShapeDtypeStruct((S, D_MODEL), F32), jax.ShapeDtypeStruct((S, 512), BF16),
                   jax.ShapeDtypeStruct((S, 512), BF16), jax.ShapeDtypeStruct((S, D_MODEL), BF16),
                   jax.ShapeDtypeStruct((S, D_MODEL), F32)],
        compiler_params=_params(1))(ob, oc, proj, proj, ya, proj, gate_b, wbr, w_out, x, post_g)


def gate_bwd(g, out, post_g, w_out, proj, gate_b, ya, yb, yc, wbr):
    S = g.shape[0]
    ts = ROW_TILE

    def body(g_ref, out_ref, pg_ref, wo_ref, gl_ref, gb_ref, ya_ref, yb_ref, yc_ref, wbr_ref,
             dp_ref, do_ref, dba_ref, dbb_ref, dbc_ref, dy_ref, ggb_ref, gp_ref):
        first = pl.program_id(0) == 0
        ov = out_ref[...]
        _, r = _rms(ov, pg_ref[...])
        dout, gp = _rms_bwd(ov, pg_ref[...], r, g_ref[...])
        db = dout.astype(BF16)
        do_ref[...] = db
        _accumulate(gp_ref, gp, first)
        dm = _nt(db, wo_ref[...])
        ggb = []
        for n, (y_ref, dbr_ref) in enumerate(((ya_ref, dba_ref), (yb_ref, dbb_ref), (yc_ref, dbc_ref))):
            cols = slice(n * D_MODEL, (n + 1) * D_MODEL)
            br = _nn(y_ref[...], wbr_ref[n])
            sg = _sigmoid(gl_ref[:, cols] + gb_ref[:, cols])
            dgl = dm * br * (sg * (1.0 - sg))
            dp_ref[:, cols] = dgl.astype(BF16)
            ggb.append(jnp.sum(dgl, axis=0, keepdims=True))
            dbr = (dm * sg).astype(BF16)
            dbr_ref[...] = dbr
            dy_ref[n] = _nt(dbr, wbr_ref[n])
        _accumulate(ggb_ref, jnp.concatenate(ggb, axis=1), first)

    row = lambda w: pl.BlockSpec((ts, w), lambda i: (i, 0))
    full = lambda shape: pl.BlockSpec(shape, lambda i: (0,) * len(shape))
    wide = jax.ShapeDtypeStruct((S, D_MODEL), BF16)
    return _call(
        body, name="gate_bwd", grid=(S // ts,),
        in_specs=[row(D_MODEL), row(D_MODEL), full((1, D_MODEL)), full((D_MODEL, D_MODEL)),
                  pl.BlockSpec((ts, 3072), lambda i: (i, P_G // 3072)), full((1, 3072)),
                  row(512), row(512), row(512), full((3, BR_WIDTH, D_MODEL))],
        out_specs=[pl.BlockSpec((ts, 3072), lambda i: (i, P_G // 3072)), row(D_MODEL), row(D_MODEL), row(D_MODEL),
                   row(D_MODEL), pl.BlockSpec((3, ts, 512), lambda i: (0, i, 0)), full((1, 3072)),
                   full((1, D_MODEL))],
        out_shape=[jax.ShapeDtypeStruct((S, P_W), BF16), wide, wide, wide, wide,
                   jax.ShapeDtypeStruct((3, S, 512), F32), jax.ShapeDtypeStruct((1, 3072), F32),
                   jax.ShapeDtypeStruct((1, D_MODEL), F32)],
        compiler_params=_params(1))(g, out, post_g, w_out, proj, gate_b, ya, yb, yc, wbr)


def ungate_bwd(dproj, dy, ob, oc, proj):
    S = proj.shape[0]
    ts = LIGHT_ROW_TILE

    def body(dp_in, dyb_ref, dyc_ref, ob_ref, oc_ref, zb_ref, zc_ref, dp_ref, dob_ref, doc_ref, dlb_ref, dlc_ref):
        del dp_in
        lane = lax.broadcasted_iota(jnp.int32, (ts, 128), 1)
        for n, (dy_ref, o_ref, z_ref, do_ref, dl_ref) in enumerate(
                ((dyb_ref, ob_ref, zb_ref, dob_ref, dlb_ref), (dyc_ref, oc_ref, zc_ref, doc_ref, dlc_ref))):
            zz = z_ref[...]
            dyv = dy_ref[0]
            sg = _sigmoid(zz)
            dp_ref[:, n * 512:(n + 1) * 512] = (dyv * _compact(o_ref) * (sg * (1.0 + zz * (1.0 - sg)))).astype(BF16)
            do_c = dyv * (zz * sg)
            for p in range(4):
                piece = do_c[:, p * 128:(p + 1) * 128]
                for h, d in ((2 * p, jnp.where(lane < 64, piece, 0.0)),
                             (2 * p + 1, jnp.where(lane < 64, pltpu.roll(piece, 64, 1), 0.0))):
                    do_ref[h] = d.astype(BF16)
                    dl_ref[h] = jnp.sum((d * o_ref[h].astype(F32)).T, axis=0, keepdims=True)

    col = lambda c: pl.BlockSpec((ts, 512), lambda i: (i, c))
    dysp = lambda n: pl.BlockSpec((1, ts, 512), lambda i: (n, i, 0))
    stat = pl.BlockSpec((N_HEADS, 1, ts), lambda i: (0, 0, i))
    hshape = jax.ShapeDtypeStruct((N_HEADS, S, HEAD_PAD), BF16)
    sshape = jax.ShapeDtypeStruct((N_HEADS, 1, S), F32)
    return _call(
        body, name="ungate_bwd", grid=(S // ts,),
        in_specs=[pl.BlockSpec(memory_space=pl.ANY), dysp(1), dysp(2), _hspec(ts), _hspec(ts),
                  col(P_ZB // 512), col(P_ZC // 512)],
        out_specs=[pl.BlockSpec((ts, 1024), lambda i: (i, P_ZB // 1024)), _hspec(ts), _hspec(ts), stat, stat],
        out_shape=[jax.ShapeDtypeStruct(dproj.shape, BF16), hshape, hshape, sshape, sshape],
        input_output_aliases={0: 0},
        compiler_params=_params(1))(dproj, dy, dy, ob, oc, proj, proj)


def loss_head(y, target):
    S, D = y.shape
    ts = LIGHT_ROW_TILE

    def body(y_ref, t_ref, dy_ref, sq_ref):
        d = y_ref[...] - t_ref[...]
        dy_ref[...] = d * (1.0 / D)
        _accumulate(sq_ref, jnp.sum(d * d, axis=0, keepdims=True), pl.program_id(0) == 0)

    row = pl.BlockSpec((ts, D), lambda i: (i, 0))
    return _call(
        body, name="loss_head", grid=(S // ts,), in_specs=[row, row],
        out_specs=[row, pl.BlockSpec((1, D), lambda i: (0, 0))],
        out_shape=[jax.ShapeDtypeStruct((S, D), F32), jax.ShapeDtypeStruct((1, D), F32)],
        compiler_params=_params(1))(y, target)


def _row_tile(rows, cols):
    for cand in (1024, 512, 256, 128, 64, 32, 16, 8):
        if rows % cand == 0 and cand * cols * 4 <= 1024 * 1024:
            return cand
    return rows


def adamw(w, grads, m, v):
    shape = w.shape
    cols = shape[-1]
    rows = int(np.prod(shape[:-1]))
    tr = _row_tile(rows, cols)
    n_g = len(grads)
    c1 = 1.0 - ADAM_B1 ** ADAM_STEP
    c2 = 1.0 - ADAM_B2 ** ADAM_STEP

    def body(*refs):
        w_ref, m_ref, v_ref = refs[:3]
        g_refs = refs[3:3 + n_g]
        go_ref, d_ref, mo_ref, vo_ref = refs[3 + n_g:]
        gv = g_refs[0][...]
        for g_ref in g_refs[1:]:
            gv = gv + g_ref[...]
        go_ref[...] = gv
        mn = ADAM_B1 * m_ref[...] + (1.0 - ADAM_B1) * gv
        vn = ADAM_B2 * v_ref[...] + (1.0 - ADAM_B2) * (gv * gv)
        mo_ref[...] = mn
        vo_ref[...] = vn
        d_ref[...] = -ADAM_LR * ((mn / c1) / (jnp.sqrt(vn / c2) + ADAM_EPS) + ADAM_WD * w_ref[...])

    blk = pl.BlockSpec((tr, cols), lambda i: (i, 0))
    sds = jax.ShapeDtypeStruct((rows, cols), F32)
    outs = _call(
        body, name="adamw", grid=(rows // tr,), in_specs=[blk] * (3 + n_g), out_specs=[blk] * 4,
        out_shape=[sds] * 4, compiler_params=_params(1))(
            *[a.reshape(rows, cols) for a in (w, m, v, *grads)])
    return [o.reshape(shape) for o in outs]


def add_lead(parts):
    n = parts.shape[0]
    shape = parts.shape[1:]
    cols = shape[-1]
    rows = int(np.prod(shape[:-1]))
    tr = _row_tile(rows, cols * n)

    def body(p_ref, o_ref):
        acc = p_ref[0].astype(F32)
        for s in range(1, n):
            acc = acc + p_ref[s].astype(F32)
        o_ref[...] = acc

    out = _call(
        body, name="add_lead", grid=(rows // tr,),
        in_specs=[pl.BlockSpec((n, tr, cols), lambda i: (0, i, 0))],
        out_specs=pl.BlockSpec((tr, cols), lambda i: (i, 0)),
        out_shape=jax.ShapeDtypeStruct((rows, cols), F32),
        compiler_params=_params(1))(parts.reshape(n, rows, cols))
    return out.reshape(shape)


ANY = pl.BlockSpec(memory_space=pl.ANY)


def _other_chips(x, y):
    return [(1 - x, y), (x, 1 - y), (1 - x, 1 - y)]


def chip_exchange(arrays, scatter, name):
    n = len(arrays)

    def body(*refs):
        _exchange_issue(refs[:n], refs[n:2 * n], refs[2 * n:], scatter, True)
        _exchange_issue(refs[:n], refs[n:2 * n], refs[2 * n:], scatter, False)

    return _call(
        body, name=name, in_specs=[ANY] * n, out_specs=[ANY] * n,
        out_shape=_exchange_out_shape(arrays, scatter), scratch_shapes=_exchange_sems(n))(*arrays)


def _exchange_out_shape(arrays, scatter):
    return [jax.ShapeDtypeStruct(a.shape if scatter else (4,) + a.shape, a.dtype) for a in arrays]


def _exchange_sems(n):
    if n == 0:
        return []
    return [pltpu.SemaphoreType.DMA((3 * n,)), pltpu.SemaphoreType.DMA((3 * n,)), pltpu.SemaphoreType.DMA((n,))]


def _exchange_issue(srcs, outs, sems, scatter, start):
    send_sems, recv_sems, local_sems = sems
    x, y, c = lax.axis_index("x"), lax.axis_index("y"), lax.axis_index("c")
    me = 2 * x + y
    for a in range(len(srcs)):
        local_src = srcs[a].at[me] if scatter else srcs[a]
        mine = pltpu.make_async_copy(local_src, outs[a].at[me], local_sems.at[a])
        sends = []
        for j, (px, py) in enumerate(_other_chips(x, y)):
            pair = dict(send_sem=send_sems.at[3 * a + j], recv_sem=recv_sems.at[3 * a + j],
                        device_id=(px, py, c), device_id_type=MESH)
            sends.append(pltpu.make_async_remote_copy(
                src_ref=srcs[a].at[2 * px + py] if scatter else srcs[a], dst_ref=outs[a].at[me], **pair))
            if not start:
                pltpu.make_async_remote_copy(src_ref=local_src, dst_ref=outs[a].at[2 * px + py], **pair).wait_recv()
        if start:
            mine.start()
            for cp in sends:
                cp.start()
        else:
            for cp in sends:
                cp.wait_send()
            mine.wait()


def sibling_exchange(arrays):
    n = len(arrays)

    def body(*refs):
        srcs, outs = refs[:n], refs[n:2 * n]
        send_sems, recv_sems = refs[2 * n:]
        x, y, c = lax.axis_index("x"), lax.axis_index("y"), lax.axis_index("c")
        copies = [pltpu.make_async_remote_copy(src_ref=srcs[a], dst_ref=outs[a], send_sem=send_sems.at[a],
                                               recv_sem=recv_sems.at[a], device_id=(x, y, 1 - c), device_id_type=MESH)
                  for a in range(n)]
        for cp in copies:
            cp.start()
        for cp in copies:
            cp.wait()

    return _call(
        body, name="sibling_exchange", in_specs=[ANY] * n, out_specs=[ANY] * n,
        out_shape=[jax.ShapeDtypeStruct(a.shape, a.dtype) for a in arrays],
        scratch_shapes=[pltpu.SemaphoreType.DMA((n,)), pltpu.SemaphoreType.DMA((n,))])(*arrays)


def _perm_from_shards(sh):
    rows = sh.shape[1]
    pieces, pos = [], 0
    for lo, hi, plo in sorted(NAT_SEGS, key=lambda s: s[2]):
        if plo > pos:
            pieces.append(jnp.zeros((rows, plo - pos), sh.dtype))
            pos = plo
        c = lo
        while c < hi:
            kk = c // SHARD_COLS
            e = min(hi, (kk + 1) * SHARD_COLS)
            pieces.append(sh[kk][:, c - kk * SHARD_COLS:e - kk * SHARD_COLS])
            c = e
        pos += hi - lo
    if pos < P_W:
        pieces.append(jnp.zeros((rows, P_W - pos), sh.dtype))
    return jnp.concatenate(pieces, axis=1)


def _shards_from_perm(p):
    out = []
    for kk in range(4):
        lo_k, hi_k = kk * SHARD_COLS, (kk + 1) * SHARD_COLS
        pieces = []
        for lo, hi, plo in NAT_SEGS:
            a, b = max(lo, lo_k), min(hi, hi_k)
            if a < b:
                pieces.append(p[:, plo + (a - lo):plo + (b - lo)])
        out.append(jnp.concatenate(pieces, axis=1))
    return jnp.stack(out)


def _split4(a, axis):
    shape = a.shape
    a = a.reshape(shape[:axis] + (4, shape[axis] // 4) + shape[axis + 1:])
    return jnp.moveaxis(a, axis, 0)


def _join4(a, axis):
    a = jnp.moveaxis(a, 0, axis)
    shape = a.shape
    return a.reshape(shape[:axis] + (4 * shape[axis + 1],) + shape[axis + 2:])


def _pad_heads(w, per_head, lo, hi):
    r = w.shape[0]
    wh = w.reshape(r, N_HEADS, per_head)[:, :, lo:hi]
    return jnp.pad(wh, ((0, 0), (0, 0), (0, HEAD_PAD - (hi - lo)))).reshape(r, N_HEADS * HEAD_PAD)


def _rope_table(S):
    half = MLA_ROPE // 2
    inv = ROPE_BASE ** (-jnp.arange(half, dtype=F32) / half)
    ang = jnp.arange(S).astype(F32)[:, None] * inv[None, :]
    cos, sin = jnp.cos(ang), jnp.sin(ang)
    z = lambda n: jnp.zeros((S, n), F32)
    c = jnp.concatenate([jnp.ones((S, MLA_NOPE), F32), cos, cos, z(32)], axis=1)
    a = jnp.concatenate([z(MLA_NOPE), -sin, z(48)], axis=1)
    b = jnp.concatenate([z(MLA_NOPE + half), sin, z(32)], axis=1)
    return jnp.stack([c, a, b])


def _band_onehot():
    t = ATT_T
    m = np.arange(2 * t)
    d = np.where(m < t, m, m - 2 * t)
    idx = np.stack([np.clip(off + d, -REL_CLIP, REL_CLIP) + REL_CLIP for off in (t, 0)])
    return (idx[:, :, None] == np.arange(2 * REL_CLIP + 1)[None, None, :]).astype(np.float32)


def bias_expand(diag):
    t = ATT_T

    def body(d_ref, o_ref):
        kc = lax.broadcasted_iota(jnp.int32, (t, t), 0) >> CHUNK_SHIFT
        qc = lax.broadcasted_iota(jnp.int32, (t, t), 1) >> CHUNK_SHIFT
        for w, visible in ((0, kc >= qc), (1, kc <= qc)):
            rows = jnp.broadcast_to(d_ref[0, w:w + 1, :], (t, 2 * t))
            skew = pltpu.roll(rows, 0, 1, stride=1, stride_axis=0)[:, :t]
            o_ref[0, 2 * w] = jnp.where(visible, skew * LOG2E, NEG_INF)
        o_ref[0, 1] = jnp.full((t, t), NEG_INF, F32)

    return _call(
        body, name="bias_expand", grid=(N_HEADS,),
        in_specs=[pl.BlockSpec((1, 2, 2 * t), lambda h: (h, 0, 0))],
        out_specs=pl.BlockSpec((1, 3, t, t), lambda h: (h, 0, 0, 0)),
        out_shape=jax.ShapeDtypeStruct((N_HEADS, 3, t, t), F32),
        compiler_params=_params(1))(diag)


def bias_fold(dtiles):
    t = ATT_T

    def body(d_ref, o_ref):
        pad = jnp.zeros((8, t), F32)
        for w in range(2):
            acc = jnp.concatenate([d_ref[0, w, 0:8, :], pad], axis=1)
            for g in range(1, t // 8):
                grp = jnp.concatenate([d_ref[0, w, 8 * g:8 * g + 8, :], pad], axis=1)
                acc = acc + pltpu.roll(grp, 2 * t - 8 * g, 1)
            out = acc[0:1, :]
            for s in range(1, 8):
                out = out + pltpu.roll(acc, 2 * t - s, 1)[s:s + 1, :]
            o_ref[0, w:w + 1, :] = out

    return _call(
        body, name="bias_fold", grid=(N_HEADS,),
        in_specs=[pl.BlockSpec((1, 2, t, t), lambda h: (h, 0, 0, 0))],
        out_specs=pl.BlockSpec((1, 2, 2 * t), lambda h: (h, 0, 0)),
        out_shape=jax.ShapeDtypeStruct((N_HEADS, 2, 2 * t), F32),
        compiler_params=_params(1))(dtiles)


def _bias_tiles(table):
    diag = jnp.einsum('hr,wdr->hwd', table, jnp.asarray(_band_onehot()), precision=lax.Precision.HIGHEST)
    return bias_expand(diag)


def _bias_tiles_grad(dtiles):
    return jnp.einsum('hwd,wdr->hr', bias_fold(dtiles), jnp.asarray(_band_onehot()),
                      precision=lax.Precision.HIGHEST)


def _layer_consts(lw):
    tri = np.tril(np.ones((SGU_BLOCK, SGU_BLOCK), np.float32))
    ws = (lw['sgu_w'] * tri).astype(BF16)
    return dict(
        ws=ws, ws_t=jnp.swapaxes(ws, 1, 2), sgu_bias=jnp.repeat(lw['sgu_b'].T, CA_HEAD_DIM, axis=1),
        bias=_bias_tiles(lw['ca_rel_bias']),
        wq=_pad_heads(lw['mla_w_uq'], MLA_QK, 0, MLA_QK),
        wk=_pad_heads(lw['mla_w_ukv'], MLA_NOPE + MLA_V, 0, MLA_NOPE),
        wv=_pad_heads(lw['mla_w_ukv'], MLA_NOPE + MLA_V, MLA_NOPE, MLA_NOPE + MLA_V),
        gate_b=lw['gate_b'].reshape(1, 3 * D_MODEL),
        pre_g=lw['pre_g'][None], post_g=lw['post_g'][None], ln_g=lw['sgu_ln_g'][None], ln_b=lw['sgu_ln_b'][None],
        qg=lw['mla_q_norm_g'][None], kvg=lw['mla_kv_norm_g'][None])


def _layer_fwd(x, lw, k, tab, next_shards):
    proj, xn = norm_matmul(x, k['pre_g'], lw['w_in'])
    ya = sgu_fwd(proj, k['ln_g'], k['ln_b'], k['ws'], k['sgu_bias'])
    qb, kb, vb, qc, kc, vc, kbt, vbt, kct, vct, cq, ckv = prep_fwd(proj, tab, k['qg'], k['kvg'], k['wq'], k['wk'],
                                                                   k['wv'])
    ob, lse_b, gathered = mla_fwd(qb, kb, vbt, next_shards)
    oc, lse_c = band_fwd(qc, kc, vct, k['bias'])
    x_new, yb, yc, merged, out = merge_fwd(ob, oc, proj, ya, k['gate_b'], lw['w_branch'], lw['w_out'], x,
                                           k['post_g'])
    saved = dict(x=x, proj=proj, xn=xn, ya=ya, yb=yb, yc=yc, qb=qb, kb=kb, vb=vb, qc=qc, kc=kc, vc=vc, kbt=kbt, kct=kct,
                 cq=cq, ckv=ckv, ob=ob, oc=oc, lse_b=lse_b, lse_c=lse_c, merged=merged, out=out)
    return x_new, saved, gathered


def _layer_bwd(g, s, lw, k, tab, pending_parts, scatter_own):
    S = g.shape[0]
    H = N_HEADS
    dproj, dout, dba, dbb, dbc, dy, g_gate_b, g_post = gate_bwd(
        g, s['out'], k['post_g'], lw['w_out'], s['proj'], k['gate_b'], s['ya'], s['yb'], s['yc'], lw['w_branch'])
    g_w_out = matmul_tn(s['merged'], dout, 512)
    g_w_branch = jnp.stack([matmul_tn(y, d, 512) for y, d in ((s['ya'], dba), (s['yb'], dbb), (s['yc'], dbc))])
    dproj, dob, doc, dl_b, dl_c = ungate_bwd(dproj, dy, s['ob'], s['oc'], s['proj'])
    dqb, dkb, dvb, landed = mla_bwd(s['qb'], s['kb'], s['kbt'], s['vb'], dob, s['lse_b'], dl_b, pending_parts)
    dqc, dkc, dvc, dbias = band_bwd(s['qc'], s['kc'], s['kct'], s['vc'], doc, s['lse_c'], dl_c, k['bias'])
    dproj, dqf, dkf, dvf, g_qg, g_kvg = prep_bwd(dproj, dqb, dkb, dvb, dqc, dkc, dvc, s['proj'], tab,
                                                 k['qg'], k['kvg'], k['wq'], k['wk'], k['wv'])
    g_wq = matmul_tn(s['cq'], dqf, 512).reshape(MLA_Q_RANK, H, HEAD_PAD)[:, :, :MLA_QK]
    g_wk = matmul_tn(s['ckv'], dkf, 512).reshape(MLA_KV_RANK, H, HEAD_PAD)[:, :, :MLA_NOPE]
    g_wv = matmul_tn(s['ckv'], dvf, 512).reshape(MLA_KV_RANK, H, HEAD_PAD)[:, :, :MLA_V]
    dproj, g_ln_g, g_ln_b, g_ws, g_sgu_bias = sgu_bwd(dproj, dy, s['proj'], k['ln_g'], k['ln_b'], k['ws'],
                                                      k['ws_t'], k['sgu_bias'])
    g_w_in = matmul_tn(s['xn'], dproj, MM_TN)
    sharded = _sharded_parts(dict(
        w_in=g_w_in, mla_w_uq=g_wq.reshape(MLA_Q_RANK, H * MLA_QK),
        mla_w_ukv=jnp.concatenate([g_wk, g_wv], axis=2).reshape(MLA_KV_RANK, H * (MLA_NOPE + MLA_V)),
        w_branch=g_w_branch, gate_b=g_gate_b.reshape(N_BRANCH, D_MODEL), w_out=g_w_out))
    dx, g_pre, own_landed = proj_bwd_x(dproj, lw['w_in'], s['x'], k['pre_g'], g, sharded if scatter_own else ())
    tri = np.tril(np.ones((SGU_BLOCK, SGU_BLOCK), np.float32))
    small = _small_pack(dict(
        pre_g=g_pre[0], post_g=g_post[0], sgu_ln_g=g_ln_g[0], sgu_ln_b=g_ln_b[0],
        sgu_w=g_ws * tri, sgu_b=jnp.sum(g_sgu_bias.reshape(SGU_BLOCK, 8, CA_HEAD_DIM), axis=2).T,
        mla_q_norm_g=g_qg[0], mla_kv_norm_g=g_kvg[0], ca_rel_bias=_bias_tiles_grad(dbias)))
    return dx, (own_landed if scatter_own else sharded), small, landed


BF16_PARTS = ('w_in', 'mla_w_uq', 'mla_w_ukv', 'w_branch', 'w_out')


def _weight_shards(w, l):
    return [w[n][l].astype(BF16) if n in BF16_PARTS else w[n][l] for n in SHARDED]


def _full_weights(gathered, small):
    lw = {n: _join4(a, SHARD_AXIS[n]) for n, a in zip(SHARDED, gathered) if n != 'w_in'}
    lw['w_in'] = _perm_from_shards(gathered[0])
    lw.update(small)
    return lw


def _small_pack(grads):
    flat = jnp.concatenate([grads[n].reshape(-1) for n in SMALL])
    quarter = -(-flat.size // (4 * 1024)) * 1024
    return jnp.pad(flat, (0, 4 * quarter - flat.size)).reshape(4, quarter // 128, 128)


def _sharded_parts(grads):
    parts = [_shards_from_perm(grads['w_in'])]
    parts += [_split4(grads[n], SHARD_AXIS[n]) for n in SHARDED if n != 'w_in']
    return [p.astype(BF16) if n in BF16_PARTS else p for n, p in zip(SHARDED, parts)]


def train_step_local(x, target, w):
    S = x.shape[0]
    depth = w['w_in'].shape[0]
    tab = _rope_table(S)
    gathered = chip_exchange(_weight_shards(w, 0), False, "gather_weights")
    layer_w, consts, saved = [], [], []
    for l in range(depth):
        lw = _full_weights(gathered, {n: w[n][l] for n in SMALL})
        k = _layer_consts(lw)
        x, s, gathered = _layer_fwd(x, lw, k, tab, _weight_shards(w, l + 1) if l + 1 < depth else ())
        layer_w.append(lw)
        consts.append(k)
        saved.append(s)
    g, sq = loss_head(x, target)
    mine = [None] * depth
    pending = ()
    for l in reversed(range(depth)):
        g, sharded, small, landed = _layer_bwd(g, saved[l], layer_w[l], consts[l], tab, pending, l == 0)
        if pending:
            mine[l + 1] = [add_lead(p) for p in landed]
        pending = list(sharded) + [small]
    mine[0] = [add_lead(p) for p in pending[:-1] + list(chip_exchange(pending[-1:], True, "scatter_small"))]
    n_parts = len(mine[0])
    theirs = sibling_exchange([p for layer in mine for p in layer])
    return sq, g, [(mine[l], theirs[l * n_parts:(l + 1) * n_parts]) for l in range(depth)]


def kernel(x, w_in, pre_g, post_g, sgu_ln_g, sgu_ln_b, sgu_w, sgu_b, mla_q_norm_g, mla_kv_norm_g, mla_w_uq, mla_w_ukv, ca_rel_bias, w_branch, gate_b, w_out, loss_target, m_w_in, m_pre_g, m_post_g, m_sgu_ln_g, m_sgu_ln_b, m_sgu_w, m_sgu_b, m_mla_q_norm_g, m_mla_kv_norm_g, m_mla_w_uq, m_mla_w_ukv, m_ca_rel_bias, m_w_branch, m_gate_b, m_w_out, v_w_in, v_pre_g, v_post_g, v_sgu_ln_g, v_sgu_ln_b, v_sgu_w, v_sgu_b, v_mla_q_norm_g, v_mla_kv_norm_g, v_mla_w_uq, v_mla_w_ukv, v_ca_rel_bias, v_w_branch, v_gate_b, v_w_out):
    w = dict(w_in=w_in, pre_g=pre_g, post_g=post_g, sgu_ln_g=sgu_ln_g, sgu_ln_b=sgu_ln_b, sgu_w=sgu_w, sgu_b=sgu_b,
             mla_q_norm_g=mla_q_norm_g, mla_kv_norm_g=mla_kv_norm_g, mla_w_uq=mla_w_uq, mla_w_ukv=mla_w_ukv,
             ca_rel_bias=ca_rel_bias, w_branch=w_branch, gate_b=gate_b, w_out=w_out)
    m = dict(w_in=m_w_in, pre_g=m_pre_g, post_g=m_post_g, sgu_ln_g=m_sgu_ln_g, sgu_ln_b=m_sgu_ln_b, sgu_w=m_sgu_w,
             sgu_b=m_sgu_b, mla_q_norm_g=m_mla_q_norm_g, mla_kv_norm_g=m_mla_kv_norm_g, mla_w_uq=m_mla_w_uq,
             mla_w_ukv=m_mla_w_ukv, ca_rel_bias=m_ca_rel_bias, w_branch=m_w_branch, gate_b=m_gate_b, w_out=m_w_out)
    v = dict(w_in=v_w_in, pre_g=v_pre_g, post_g=v_post_g, sgu_ln_g=v_sgu_ln_g, sgu_ln_b=v_sgu_ln_b, sgu_w=v_sgu_w,
             sgu_b=v_sgu_b, mla_q_norm_g=v_mla_q_norm_g, mla_kv_norm_g=v_mla_kv_norm_g, mla_w_uq=v_mla_w_uq,
             mla_w_ukv=v_mla_w_ukv, ca_rel_bias=v_ca_rel_bias, w_branch=v_w_branch, gate_b=v_gate_b, w_out=v_w_out)
    depth = w_in.shape[0]
    sq, grad_x, reduced = train_step_local(x[0], loss_target[0], w)
    loss = lax.psum(0.5 * jnp.sum(sq) / D_MODEL, ("x", "y", "c"))

    out = {}
    for a, n in enumerate(SHARDED):
        mine = jnp.stack([reduced[l][0][a] for l in range(depth)])
        theirs = jnp.stack([reduced[l][1][a] for l in range(depth)])
        out[n] = adamw(w[n], [mine, theirs], m[n], v[n])
    small = jnp.stack([jnp.stack([reduced[l][0][-1] for l in range(depth)]),
                       jnp.stack([reduced[l][1][-1] for l in range(depth)])])
    quarter = add_lead(small)
    full = chip_exchange([quarter], False, "gather_small")[0]
    full = jnp.moveaxis(full, 0, 1).reshape(depth, -1)
    off = 0
    for n in SMALL:
        size = int(np.prod(w[n].shape[1:]))
        out[n] = adamw(w[n], [full[:, off:off + size].reshape(w[n].shape)], m[n], v[n])
        off += size
    return (loss, grad_x[None], *[out[n][0] for n in WEIGHTS], *[out[n][1] for n in WEIGHTS],
            *[out[n][2] for n in WEIGHTS], *[out[n][3] for n in WEIGHTS])
```

```python
import numpy as np
import jax
import jax.numpy as jnp
from jax import lax
from jax.experimental import pallas as pl
from jax.experimental.pallas import tpu as pltpu

F32 = jnp.float32
BF16 = jnp.bfloat16
MESH = pl.DeviceIdType.MESH

EPS = 1e-6
NEG_INF = -1e30
D_MODEL = 1024
BR_WIDTH = 512
N_BRANCH = 3
N_HEADS = 8
HEAD_PAD = 128
CHUNK_SHIFT = 6
SGU_BLOCK = 128
MLA_NOPE, MLA_ROPE, MLA_V = 64, 32, 64
MLA_QK = MLA_NOPE + MLA_ROPE
MLA_Q_RANK, MLA_KV_RANK = 256, 128
CA_HEAD_DIM = 64
REL_CLIP = 128
ROPE_BASE = 10000.0
D_IN = 7584

ADAM_LR, ADAM_B1, ADAM_B2, ADAM_EPS, ADAM_WD, ADAM_STEP = 0.001, 0.9, 0.999, 1e-08, 0.01, 10

P_QC, P_KC, P_VC, P_QD, P_KVD, P_KR, P_ZB, P_ZC, P_G, P_U, P_V, P_ZA, P_W = (
    0, 512, 1024, 1536, 1792, 1920, 2048, 2560, 3072, 6144, 6656, 7168, 7680)
NAT_SEGS = [(0, 1536, P_U), (1536, 1920, P_QD), (1920, 1952, P_KR + MLA_NOPE), (1952, 2464, P_ZB),
            (2464, 4000, P_QC), (4000, 4512, P_ZC), (4512, 7584, P_G)]
SHARD_COLS = D_IN // 4

VMEM_LIMIT = 48 * 1024 * 1024
ATT_T = 512
BAND_FWD_HEADS_PER_STEP = 4
MLA_FWD_HEADS_PER_STEP = 8
MLA_BWD_HEADS_PER_STEP = 4
ROW_TILE = 256
LIGHT_ROW_TILE = 512
MM_TM = 512
MM_TN = 1536
LOG2E = 1.4426950408889634
MLA_SCALE = MLA_QK ** -0.5
CA_SCALE = CA_HEAD_DIM ** -0.5

WEIGHTS = ['w_in', 'pre_g', 'post_g', 'sgu_ln_g', 'sgu_ln_b', 'sgu_w', 'sgu_b', 'mla_q_norm_g',
           'mla_kv_norm_g', 'mla_w_uq', 'mla_w_ukv', 'ca_rel_bias', 'w_branch', 'gate_b', 'w_out']
SHARDED = ['w_in', 'mla_w_uq', 'mla_w_ukv', 'w_branch', 'gate_b', 'w_out']
SMALL = ['pre_g', 'post_g', 'sgu_ln_g', 'sgu_ln_b', 'sgu_w', 'sgu_b', 'mla_q_norm_g',
         'mla_kv_norm_g', 'ca_rel_bias']
SHARD_AXIS = {'w_in': 1, 'mla_w_uq': 1, 'mla_w_ukv': 1, 'w_branch': 2, 'gate_b': 1, 'w_out': 0}


def _call(body, **kw):
    return pl.pallas_call(body, **kw)


def _params(n_axes):
    return pltpu.CompilerParams(dimension_semantics=("arbitrary",) * n_axes,
                                vmem_limit_bytes=VMEM_LIMIT)


def _nt(a, b):
    return lax.dot_general(a, b, (((1,), (1,)), ((), ())), preferred_element_type=F32)


def _nn(a, b):
    return jnp.dot(a, b, preferred_element_type=F32)


def _tn(a, b):
    return lax.dot_general(a, b, (((0,), (0,)), ((), ())), preferred_element_type=F32)


def _rms(xv, g):
    r = lax.rsqrt(jnp.mean(xv * xv, axis=-1, keepdims=True) + EPS)
    return xv * r * g, r


def _rms_bwd(xv, g, r, dy):
    gy = dy * g
    dx = r * gy - xv * (r * r * r) * jnp.mean(xv * gy, axis=-1, keepdims=True)
    dg = jnp.sum(dy * (xv * r), axis=0, keepdims=True)
    return dx, dg


def _sigmoid(z):
    return 1.0 / (1.0 + jnp.exp(-z))


def _rope(xv, c, a, b):
    return xv * c + pltpu.roll(xv, 112, 1) * a + pltpu.roll(xv, 16, 1) * b


def _accumulate(ref, val, first):
    @pl.when(first)
    def _():
        ref[...] = val

    @pl.when(jnp.logical_not(first))
    def _():
        ref[...] += val


def norm_matmul(x, g, w):
    S, D = x.shape
    N = w.shape[1]
    tm, tn = min(S, 2 * MM_TM), MM_TN

    def body(x_ref, g_ref, w_ref, o_ref, xn_ref):
        @pl.when(pl.program_id(1) == 0)
        def _():
            y, _ = _rms(x_ref[...], g_ref[...])
            xn_ref[...] = y.astype(BF16)

        o_ref[...] = _nn(xn_ref[...], w_ref[...])

    return _call(
        body, name="norm_matmul", grid=(S // tm, N // tn),
        in_specs=[pl.BlockSpec((tm, D), lambda i, j: (i, 0)),
                  pl.BlockSpec((1, D), lambda i, j: (0, 0)),
                  pl.BlockSpec((D, tn), lambda i, j: (0, j))],
        out_specs=[pl.BlockSpec((tm, tn), lambda i, j: (i, j)),
                   pl.BlockSpec((tm, D), lambda i, j: (i, 0))],
        out_shape=[jax.ShapeDtypeStruct((S, N), F32), jax.ShapeDtypeStruct((S, D), BF16)],
        compiler_params=_params(2))(x, g, w)


def proj_bwd_x(dproj, w, x, g, resid, exchange=()):
    S, N = dproj.shape
    D = x.shape[1]
    tm, tk = min(S, MM_TM), MM_TN
    nk, n_ex = N // tk, len(exchange)

    def body(dp_ref, w_ref, x_ref, g_ref, r_ref, *rest):
        ex_src, (dx_ref, dg_ref), ex_out, (acc_ref,), ex_sems = _split_refs(
            rest, (n_ex, 2, n_ex, 1, 3 if n_ex else 0))
        i, k = pl.program_id(0), pl.program_id(1)

        if n_ex:
            @pl.when(jnp.logical_and(i == 0, k == 0))
            def _():
                _exchange_issue(ex_src, ex_out, ex_sems, True, True)

        @pl.when(k == 0)
        def _():
            acc_ref[...] = jnp.zeros_like(acc_ref)

        acc_ref[...] += _nt(dp_ref[...].astype(BF16), w_ref[...])

        @pl.when(k == nk - 1)
        def _():
            xv = x_ref[...]
            _, r = _rms(xv, g_ref[...])
            dx, dg = _rms_bwd(xv, g_ref[...], r, acc_ref[...])
            dx_ref[...] = dx + r_ref[...]
            _accumulate(dg_ref, dg, i == 0)

        if n_ex:
            @pl.when(jnp.logical_and(i == S // tm - 1, k == nk - 1))
            def _():
                _exchange_issue(ex_src, ex_out, ex_sems, True, False)

    outs = _call(
        body, name="proj_bwd_x_scatter" if n_ex else "proj_bwd_x", grid=(S // tm, nk),
        in_specs=[pl.BlockSpec((tm, tk), lambda i, k: (i, k)),
                  pl.BlockSpec((D, tk), lambda i, k: (0, k)),
                  pl.BlockSpec((tm, D), lambda i, k: (i, 0)),
                  pl.BlockSpec((1, D), lambda i, k: (0, 0)),
                  pl.BlockSpec((tm, D), lambda i, k: (i, 0))] + [ANY] * n_ex,
        out_specs=[pl.BlockSpec((tm, D), lambda i, k: (i, 0)),
                   pl.BlockSpec((1, D), lambda i, k: (0, 0))] + [ANY] * n_ex,
        out_shape=[jax.ShapeDtypeStruct((S, D), F32), jax.ShapeDtypeStruct((1, D), F32)] +
        _exchange_out_shape(exchange, True),
        scratch_shapes=[pltpu.VMEM((tm, D), F32)] + _exchange_sems(n_ex),
        compiler_params=_params(2))(dproj, w, x, g, resid, *exchange)
    return outs[0], outs[1], outs[2:]


def matmul_tn(a, b, tn):
    S, M = a.shape
    N = b.shape[1]
    tk = min(S, 2 * MM_TM)

    def body(a_ref, b_ref, o_ref):
        @pl.when(pl.program_id(1) == 0)
        def _():
            o_ref[...] = jnp.zeros_like(o_ref)

        o_ref[...] += _tn(a_ref[...].astype(BF16), b_ref[...].astype(BF16))

    return _call(
        body, name="matmul_tn", grid=(N // tn, S // tk),
        in_specs=[pl.BlockSpec((tk, M), lambda j, k: (k, 0)),
                  pl.BlockSpec((tk, tn), lambda j, k: (k, j))],
        out_specs=pl.BlockSpec((M, tn), lambda j, k: (0, j)),
        out_shape=jax.ShapeDtypeStruct((M, N), F32),
        compiler_params=_params(2))(a, b)


def _sgu_block(vv, g, b, ws_ref, lane):
    mu = jnp.mean(vv, axis=-1, keepdims=True)
    xc = vv - mu
    r = lax.rsqrt(jnp.mean(xc * xc, axis=-1, keepdims=True) + EPS)
    xhat = xc * r
    vln = (xhat * g + b).astype(BF16)
    pieces = []
    for p in range(4):
        vp = vln[:, p * 128:(p + 1) * 128]
        pieces.append(jnp.where(lane < 64, _nn(ws_ref[2 * p], vp), _nn(ws_ref[2 * p + 1], vp)))
    return xhat, r, vln, jnp.concatenate(pieces, axis=1)


def sgu_fwd(proj, ln_g, ln_b, ws, bias_full):
    S = proj.shape[0]
    ts = LIGHT_ROW_TILE

    def body(u_ref, v_ref, z_ref, g_ref, b_ref, ws_ref, bf_ref, y_ref):
        lane = lax.broadcasted_iota(jnp.int32, (SGU_BLOCK, 128), 1)
        for blk in range(ts // SGU_BLOCK):
            rows = slice(blk * SGU_BLOCK, (blk + 1) * SGU_BLOCK)
            _, _, _, mixed = _sgu_block(v_ref[rows, :], g_ref[...], b_ref[...], ws_ref, lane)
            mixed = mixed + bf_ref[...]
            zz = z_ref[rows, :]
            y_ref[rows, :] = (u_ref[rows, :] * mixed * (zz * _sigmoid(zz))).astype(BF16)

    col = lambda c: pl.BlockSpec((ts, BR_WIDTH), lambda i: (i, c))
    full = lambda shape: pl.BlockSpec(shape, lambda i: (0,) * len(shape))
    return _call(
        body, name="sgu_fwd", grid=(S // ts,),
        in_specs=[col(P_U // 512), col(P_V // 512), col(P_ZA // 512),
                  full((1, BR_WIDTH)), full((1, BR_WIDTH)), full((8, 128, 128)), full((128, BR_WIDTH))],
        out_specs=pl.BlockSpec((ts, BR_WIDTH), lambda i: (i, 0)),
        out_shape=jax.ShapeDtypeStruct((S, BR_WIDTH), BF16),
        compiler_params=_params(1))(proj, proj, proj, ln_g, ln_b, ws, bias_full)


def sgu_bwd(dproj, dy, proj, ln_g, ln_b, ws, ws_t, bias_full):
    S = proj.shape[0]
    ts = LIGHT_ROW_TILE

    def body(dp_in, dy_ref, u_ref, v_ref, z_ref, g_ref, b_ref, ws_ref, wst_ref, bf_ref,
             dp_ref, gg_ref, gb_ref, gws_ref, gbf_ref):
        del dp_in
        first = pl.program_id(0) == 0

        @pl.when(first)
        def _():
            gg_ref[...] = jnp.zeros_like(gg_ref)
            gb_ref[...] = jnp.zeros_like(gb_ref)
            gws_ref[...] = jnp.zeros_like(gws_ref)
            gbf_ref[...] = jnp.zeros_like(gbf_ref)

        lane = lax.broadcasted_iota(jnp.int32, (SGU_BLOCK, 128), 1)
        for blk in range(ts // SGU_BLOCK):
            rows = slice(blk * SGU_BLOCK, (blk + 1) * SGU_BLOCK)
            g = g_ref[...]
            xhat, r, vln, mixed = _sgu_block(v_ref[rows, :], g, b_ref[...], ws_ref, lane)
            mixed = mixed + bf_ref[...]
            zz = z_ref[rows, :]
            uu = u_ref[rows, :]
            dyv = dy_ref[0, rows, :]
            sg = _sigmoid(zz)
            sil = zz * sg
            dmixed = dyv * uu * sil
            dp_ref[rows, 0:512] = (dyv * mixed * sil).astype(BF16)
            dp_ref[rows, 1024:1536] = (dyv * uu * mixed * (sg * (1.0 + zz * (1.0 - sg)))).astype(BF16)
            gbf_ref[...] += dmixed
            dmb = dmixed.astype(BF16)
            pieces = []
            for p in range(4):
                dmp = dmb[:, p * 128:(p + 1) * 128]
                vp = vln[:, p * 128:(p + 1) * 128]
                pieces.append(jnp.where(lane < 64, _nn(wst_ref[2 * p], dmp), _nn(wst_ref[2 * p + 1], dmp)))
                zero = jnp.zeros_like(dmp)
                gws_ref[2 * p] += _nt(jnp.where(lane < 64, dmp, zero), vp)
                gws_ref[2 * p + 1] += _nt(jnp.where(lane >= 64, dmp, zero), vp)
            dvln = jnp.concatenate(pieces, axis=1)
            dxh = dvln * g
            dp_ref[rows, 512:1024] = (r * (dxh - jnp.mean(dxh, axis=-1, keepdims=True)
                                           - xhat * jnp.mean(dxh * xhat, axis=-1, keepdims=True))).astype(BF16)
            gg_ref[...] += jnp.sum(dvln * xhat, axis=0, keepdims=True)
            gb_ref[...] += jnp.sum(dvln, axis=0, keepdims=True)

    col = lambda c: pl.BlockSpec((ts, BR_WIDTH), lambda i: (i, c))
    full = lambda shape: pl.BlockSpec(shape, lambda i: (0,) * len(shape))
    return _call(
        body, name="sgu_bwd", grid=(S // ts,),
        in_specs=[pl.BlockSpec(memory_space=pl.ANY),
                  pl.BlockSpec((1, ts, BR_WIDTH), lambda i: (0, i, 0)),
                  col(P_U // 512), col(P_V // 512), col(P_ZA // 512),
                  full((1, BR_WIDTH)), full((1, BR_WIDTH)), full((8, 128, 128)), full((8, 128, 128)),
                  full((128, BR_WIDTH))],
        out_specs=[pl.BlockSpec((ts, 1536), lambda i: (i, P_U // 1536)),
                   full((1, BR_WIDTH)), full((1, BR_WIDTH)), full((8, 128, 128)), full((128, BR_WIDTH))],
        out_shape=[jax.ShapeDtypeStruct(dproj.shape, BF16),
                   jax.ShapeDtypeStruct((1, BR_WIDTH), F32), jax.ShapeDtypeStruct((1, BR_WIDTH), F32),
                   jax.ShapeDtypeStruct((8, 128, 128), F32), jax.ShapeDtypeStruct((128, BR_WIDTH), F32)],
        input_output_aliases={0: 0},
        compiler_params=_params(1))(dproj, dy, proj, proj, proj, ln_g, ln_b, ws, ws_t, bias_full)


def _hspec(ts):
    return pl.BlockSpec((N_HEADS, ts, HEAD_PAD), lambda i: (0, i, 0))


def prep_fwd(proj, tab, qg, kvg, wq, wk, wv):
    S = proj.shape[0]
    ts = LIGHT_ROW_TILE

    def body(qc_ref, kc_ref, vc_ref, qd_ref, kvd_ref, kr_ref, tab_ref, qg_ref, kvg_ref,
             wq_ref, wk_ref, wv_ref, qb, kb, vb, qc, kc, vc, kbt, vbt, kct, vct, cq_o, ckv_o):
        c, a, b = tab_ref[0], tab_ref[1], tab_ref[2]
        cq, _ = _rms(qd_ref[...], qg_ref[...])
        ckv, _ = _rms(kvd_ref[...], kvg_ref[...])
        cqb, ckvb = cq.astype(BF16), ckv.astype(BF16)
        cq_o[...] = cqb
        ckv_o[...] = ckvb
        krr = _rope(kr_ref[...], c, a, b)
        lane = lax.broadcasted_iota(jnp.int32, (ts, 128), 1)
        ones_lane = jnp.where(lane == MLA_V, 1.0, 0.0)
        for h in range(N_HEADS):
            cols = slice(h * HEAD_PAD, (h + 1) * HEAD_PAD)
            qb[h] = (_rope(_nn(cqb, wq_ref[:, cols]), c, a, b) * (MLA_SCALE * LOG2E)).astype(BF16)
            kh = _nn(ckvb, wk_ref[:, cols]) + krr
            vh = _nn(ckvb, wv_ref[:, cols]) + ones_lane
            kb[h], kbt[h] = kh.astype(BF16), kh.T.astype(BF16)
            vb[h], vbt[h] = vh.astype(BF16), vh.T.astype(BF16)
        for p in range(4):
            piece = qc_ref[:, p * 128:(p + 1) * 128] * (CA_SCALE * LOG2E)
            qc[2 * p] = jnp.where(lane < 64, piece, 0.0).astype(BF16)
            qc[2 * p + 1] = jnp.where(lane < 64, pltpu.roll(piece, 64, 1), 0.0).astype(BF16)
            for src, dst, dst_t, pad in ((kc_ref, kc, kct, 0.0), (vc_ref, vc, vct, ones_lane)):
                piece = src[:, p * 128:(p + 1) * 128]
                for h, head in ((2 * p, jnp.where(lane < 64, piece, pad)),
                                (2 * p + 1, jnp.where(lane < 64, pltpu.roll(piece, 64, 1), pad))):
                    dst[h], dst_t[h] = head.astype(BF16), head.T.astype(BF16)

    col = lambda w, c: pl.BlockSpec((ts, w), lambda i: (i, c))
    full = lambda shape: pl.BlockSpec(shape, lambda i: (0,) * len(shape))
    hshape = jax.ShapeDtypeStruct((N_HEADS, S, HEAD_PAD), BF16)
    tshape = jax.ShapeDtypeStruct((N_HEADS, HEAD_PAD, S), BF16)
    tspec = pl.BlockSpec((N_HEADS, HEAD_PAD, ts), lambda i: (0, 0, i))
    return _call(
        body, name="prep_fwd", grid=(S // ts,),
        in_specs=[col(512, P_QC // 512), col(512, P_KC // 512), col(512, P_VC // 512),
                  col(256, P_QD // 256), col(128, P_KVD // 128), col(128, P_KR // 128),
                  pl.BlockSpec((3, ts, 128), lambda i: (0, i, 0)),
                  full((1, MLA_Q_RANK)), full((1, MLA_KV_RANK)),
                  full((MLA_Q_RANK, 1024)), full((MLA_KV_RANK, 1024)), full((MLA_KV_RANK, 1024))],
        out_specs=[_hspec(ts)] * 6 + [tspec] * 4 + [pl.BlockSpec((ts, MLA_Q_RANK), lambda i: (i, 0)),
                                                    pl.BlockSpec((ts, MLA_KV_RANK), lambda i: (i, 0))],
        out_shape=[hshape] * 6 + [tshape] * 4 + [jax.ShapeDtypeStruct((S, MLA_Q_RANK), BF16),
                                                 jax.ShapeDtypeStruct((S, MLA_KV_RANK), BF16)],
        compiler_params=_params(1))(proj, proj, proj, proj, proj, proj, tab, qg, kvg, wq, wk, wv)


def prep_bwd(dproj, dqb, dkb, dvb, dqc, dkc, dvc, proj, tab, qg, kvg, wq, wk, wv):
    S = proj.shape[0]
    ts = LIGHT_ROW_TILE

    def body(dp_in, dqb_r, dkb_r, dvb_r, dqc_r, dkc_r, dvc_r, qd_ref, kvd_ref, tab_ref, qg_ref, kvg_ref,
             wq_ref, wk_ref, wv_ref, dp_ref, dqf, dkf, dvf, gq_ref, gkv_ref):
        del dp_in
        c, a, b = tab_ref[0], -tab_ref[1], -tab_ref[2]
        qd, kvd = qd_ref[...], kvd_ref[...]
        _, rq = _rms(qd, qg_ref[...])
        _, rkv = _rms(kvd, kvg_ref[...])
        dcq = jnp.zeros((ts, MLA_Q_RANK), F32)
        dckv = jnp.zeros((ts, MLA_KV_RANK), F32)
        dksum = jnp.zeros((ts, HEAD_PAD), F32)
        for h in range(N_HEADS):
            cols = slice(h * HEAD_PAD, (h + 1) * HEAD_PAD)
            dqh = _rope(dqb_r[h].astype(F32) * MLA_SCALE, c, a, b).astype(BF16)
            dqf[:, cols] = dqh
            dcq = dcq + _nt(dqh, wq_ref[:, cols])
            dk = dkb_r[h].astype(F32) * (1.0 / LOG2E)
            dksum = dksum + dk
            dkh = dk.astype(BF16)
            dkf[:, cols] = dkh
            dvh = dvb_r[h].astype(BF16)
            dvf[:, cols] = dvh
            dckv = dckv + _nt(dkh, wk_ref[:, cols]) + _nt(dvh, wv_ref[:, cols])
        lane = lax.broadcasted_iota(jnp.int32, (ts, 128), 1)
        rope_lanes = jnp.logical_and(lane >= MLA_NOPE, lane < MLA_QK)
        dp_ref[:, P_KR:P_KR + 128] = jnp.where(rope_lanes, _rope(dksum, c, a, b), 0.0).astype(BF16)
        dqd, gq = _rms_bwd(qd, qg_ref[...], rq, dcq)
        dkvd, gkv = _rms_bwd(kvd, kvg_ref[...], rkv, dckv)
        dp_ref[:, P_QD:P_QD + 256] = dqd.astype(BF16)
        dp_ref[:, P_KVD:P_KVD + 128] = dkvd.astype(BF16)
        first = pl.program_id(0) == 0
        _accumulate(gq_ref, gq, first)
        _accumulate(gkv_ref, gkv, first)
        for src, base, factor in ((dqc_r, P_QC, CA_SCALE), (dkc_r, P_KC, 1.0 / LOG2E), (dvc_r, P_VC, 1.0)):
            for p in range(4):
                dp_ref[:, base + p * 128:base + (p + 1) * 128] = (
                    (src[2 * p].astype(F32) + pltpu.roll(src[2 * p + 1].astype(F32), 64, 1)) * factor).astype(BF16)

    col = lambda w, c: pl.BlockSpec((ts, w), lambda i: (i, c))
    full = lambda shape: pl.BlockSpec(shape, lambda i: (0,) * len(shape))
    wide = jax.ShapeDtypeStruct((S, 1024), BF16)
    return _call(
        body, name="prep_bwd", grid=(S // ts,),
        in_specs=[pl.BlockSpec(memory_space=pl.ANY)] + [_hspec(ts)] * 6 +
                 [col(256, P_QD // 256), col(128, P_KVD // 128),
                  pl.BlockSpec((3, ts, 128), lambda i: (0, i, 0)),
                  full((1, MLA_Q_RANK)), full((1, MLA_KV_RANK)),
                  full((MLA_Q_RANK, 1024)), full((MLA_KV_RANK, 1024)), full((MLA_KV_RANK, 1024))],
        out_specs=[pl.BlockSpec((ts, 2048), lambda i: (i, 0))] + [pl.BlockSpec((ts, 1024), lambda i: (i, 0))] * 3 +
                  [full((1, MLA_Q_RANK)), full((1, MLA_KV_RANK))],
        out_shape=[jax.ShapeDtypeStruct(dproj.shape, BF16), wide, wide, wide,
                   jax.ShapeDtypeStruct((1, MLA_Q_RANK), F32), jax.ShapeDtypeStruct((1, MLA_KV_RANK), F32)],
        input_output_aliases={0: 0},
        compiler_params=_params(1))(dproj, dqb, dkb, dvb, dqc, dkc, dvc, proj, proj, tab, qg, kvg, wq, wk, wv)


def _diag_visible(t):
    r = lax.broadcasted_iota(jnp.int32, (t, t), 0) >> CHUNK_SHIFT
    c = lax.broadcasted_iota(jnp.int32, (t, t), 1) >> CHUNK_SHIFT
    return r <= c


def _pair_tables(nq, kv_major):
    if kv_major:
        pairs = [(kb, qi) for kb in range(nq) for qi in range(kb, nq)]
    else:
        pairs = [(kb, qi) for qi in range(nq) for kb in range(qi + 1)]
    return (jnp.asarray(np.array([p[0] for p in pairs], np.int32)),
            jnp.asarray(np.array([p[1] for p in pairs], np.int32)), len(pairs))


def _finish_softmax(acc, m):
    l = acc[MLA_V:MLA_V + 1, :]
    row = lax.broadcasted_iota(jnp.int32, acc.shape, 0)
    return jnp.where(row < MLA_V, acc / l, 0.0).T.astype(BF16), m + jnp.log2(l)


def _split_refs(refs, counts):
    out, pos = [], 0
    for c in counts:
        out.append(refs[pos:pos + c])
        pos += c
    return out


def mla_fwd(q, k, vt, exchange=()):
    H, S, _ = q.shape
    t, hb, n_ex = ATT_T, MLA_FWD_HEADS_PER_STEP, len(exchange)
    kb_tab, qi_tab, n_pairs = _pair_tables(S // t, False)

    def body(kb_ref, qi_ref, q_ref, k_ref, vt_ref, *rest):
        ex_src, (o_ref, lse_ref), ex_out, (m_s, acc_s), ex_sems = _split_refs(rest, (n_ex, 2, n_ex, 2, 3 if n_ex else 0))
        hg, p_id = pl.program_id(0), pl.program_id(1)
        kb, qi = kb_ref[p_id], qi_ref[p_id]

        if n_ex:
            @pl.when(jnp.logical_and(hg == 0, p_id == 0))
            def _():
                _exchange_issue(ex_src, ex_out, ex_sems, False, True)

        @pl.when(kb == 0)
        def _():
            m_s[...] = jnp.full_like(m_s, NEG_INF)
            acc_s[...] = jnp.zeros_like(acc_s)

        def step(masked):
            for h in range(hb):
                st = _nt(k_ref[h], q_ref[h])
                if masked:
                    st = jnp.where(_diag_visible(t), st, NEG_INF)
                m_prev = m_s[h]
                m_new = jnp.maximum(m_prev, jnp.max(st, axis=0, keepdims=True))
                p = jnp.exp2(st - m_new)
                acc_s[h] = jnp.exp2(m_prev - m_new) * acc_s[h] + _nn(vt_ref[h], p.astype(BF16))
                m_s[h] = m_new

        @pl.when(kb < qi)
        def _():
            step(False)

        @pl.when(kb == qi)
        def _():
            step(True)
            for h in range(hb):
                o_ref[h], lse_ref[h] = _finish_softmax(acc_s[h], m_s[h])

        if n_ex:
            @pl.when(jnp.logical_and(hg == H // hb - 1, p_id == n_pairs - 1))
            def _():
                _exchange_issue(ex_src, ex_out, ex_sems, False, False)

    grid_spec = pltpu.PrefetchScalarGridSpec(
        num_scalar_prefetch=2, grid=(H // hb, n_pairs),
        in_specs=[pl.BlockSpec((hb, t, HEAD_PAD), lambda h, p, kb, qi: (h, qi[p], 0)),
                  pl.BlockSpec((hb, t, HEAD_PAD), lambda h, p, kb, qi: (h, kb[p], 0)),
                  pl.BlockSpec((hb, HEAD_PAD, t), lambda h, p, kb, qi: (h, 0, kb[p]))] + [ANY] * n_ex,
        out_specs=[pl.BlockSpec((hb, t, HEAD_PAD), lambda h, p, kb, qi: (h, qi[p], 0)),
                   pl.BlockSpec((hb, 1, t), lambda h, p, kb, qi: (h, 0, qi[p]))] + [ANY] * n_ex,
        scratch_shapes=[pltpu.VMEM((hb, 1, t), F32), pltpu.VMEM((hb, HEAD_PAD, t), F32)] + _exchange_sems(n_ex))
    outs = _call(
        body, name="mla_fwd_gather" if n_ex else "mla_fwd", grid_spec=grid_spec,
        out_shape=[jax.ShapeDtypeStruct((H, S, HEAD_PAD), BF16), jax.ShapeDtypeStruct((H, 1, S), F32)] +
        _exchange_out_shape(exchange, False),
        compiler_params=_params(2))(kb_tab, qi_tab, q, k, vt, *exchange)
    return outs[0], outs[1], outs[2:]


def mla_bwd(q, k, kt, v, do, lse, delta, exchange=()):
    H, S, _ = q.shape
    t, hb, n_ex = ATT_T, MLA_BWD_HEADS_PER_STEP, len(exchange)
    nq = S // t
    kb_tab, qi_tab, n_pairs = _pair_tables(nq, True)

    def body(kb_ref, qi_ref, q_ref, k_ref, kt_ref, v_ref, do_ref, lse_ref, dl_ref, *rest):
        ex_src, (dq_ref, dk_ref, dv_ref), ex_out, (dqt_s, dk_s, dv_s), ex_sems = _split_refs(
            rest, (n_ex, 3, n_ex, 3, 3 if n_ex else 0))
        hg, p_id = pl.program_id(0), pl.program_id(1)
        kb, qi = kb_ref[p_id], qi_ref[p_id]

        if n_ex:
            @pl.when(jnp.logical_and(hg == 0, p_id == 0))
            def _():
                _exchange_issue(ex_src, ex_out, ex_sems, True, True)

        @pl.when(p_id == 0)
        def _():
            dqt_s[...] = jnp.zeros_like(dqt_s)

        @pl.when(qi == kb)
        def _():
            dk_s[...] = jnp.zeros_like(dk_s)
            dv_s[...] = jnp.zeros_like(dv_s)

        def step(masked):
            for h in range(hb):
                st = _nt(k_ref[h], q_ref[h])
                if masked:
                    st = jnp.where(_diag_visible(t), st, NEG_INF)
                pt = jnp.exp2(st - lse_ref[h])
                dv_s[h] += _nn(pt.astype(BF16), do_ref[h])
                dsb = (pt * (_nt(v_ref[h], do_ref[h]) - dl_ref[h])).astype(BF16)
                dk_s[h] += _nn(dsb, q_ref[h])
                dqt_s[h, qi] += _nn(kt_ref[h], dsb)

        @pl.when(qi == kb)
        def _():
            step(True)
            rows = pl.ds(pl.multiple_of(qi * t, t), t)
            for h in range(hb):
                dq_ref[h, rows, :] = dqt_s[h, qi].T.astype(BF16)

        @pl.when(qi > kb)
        def _():
            step(False)

        @pl.when(qi == nq - 1)
        def _():
            dk_ref[...] = dk_s[...].astype(BF16)
            dv_ref[...] = dv_s[...].astype(BF16)

        if n_ex:
            @pl.when(jnp.logical_and(hg == H // hb - 1, p_id == n_pairs - 1))
            def _():
                _exchange_issue(ex_src, ex_out, ex_sems, True, False)

    qtile = pl.BlockSpec((hb, t, HEAD_PAD), lambda h, p, kb, qi: (h, qi[p], 0))
    ktile = pl.BlockSpec((hb, t, HEAD_PAD), lambda h, p, kb, qi: (h, kb[p], 0))
    stat = pl.BlockSpec((hb, 1, t), lambda h, p, kb, qi: (h, 0, qi[p]))
    grid_spec = pltpu.PrefetchScalarGridSpec(
        num_scalar_prefetch=2, grid=(H // hb, n_pairs),
        in_specs=[qtile, ktile, pl.BlockSpec((hb, HEAD_PAD, t), lambda h, p, kb, qi: (h, 0, kb[p])), ktile, qtile,
                  stat, stat] + [ANY] * n_ex,
        out_specs=[pl.BlockSpec((hb, S, HEAD_PAD), lambda h, p, kb, qi: (h, 0, 0)), ktile, ktile] + [ANY] * n_ex,
        scratch_shapes=[pltpu.VMEM((hb, nq, HEAD_PAD, t), F32), pltpu.VMEM((hb, t, HEAD_PAD), F32),
                        pltpu.VMEM((hb, t, HEAD_PAD), F32)] + _exchange_sems(n_ex))
    outs = _call(
        body, name="mla_bwd_scatter" if n_ex else "mla_bwd", grid_spec=grid_spec,
        out_shape=[jax.ShapeDtypeStruct((H, S, HEAD_PAD), BF16)] * 3 + _exchange_out_shape(exchange, True),
        compiler_params=_params(2))(kb_tab, qi_tab, q, k, kt, v, do, lse, delta, *exchange)
    return outs[0], outs[1], outs[2], outs[3:]


def _band_specs(t, hb):
    prev = lambda i: jnp.maximum(i - 1, 0)
    return dict(
        cur=pl.BlockSpec((hb, t, HEAD_PAD), lambda h, i: (h, i, 0)),
        prev=pl.BlockSpec((hb, t, HEAD_PAD), lambda h, i: (h, prev(i), 0)),
        cur_t=pl.BlockSpec((hb, HEAD_PAD, t), lambda h, i: (h, 0, i)),
        prev_t=pl.BlockSpec((hb, HEAD_PAD, t), lambda h, i: (h, 0, prev(i))),
        stat=pl.BlockSpec((hb, 1, t), lambda h, i: (h, 0, i)),
        bias_prev=pl.BlockSpec((hb, 1, t, t), lambda h, i: (h, jnp.where(i == 0, 1, 0), 0, 0)),
        bias_cur=pl.BlockSpec((hb, 1, t, t), lambda h, i: (h, 2, 0, 0)))


def band_fwd(q, k, vt, bias):
    H, S, _ = q.shape
    t, hb = ATT_T, BAND_FWD_HEADS_PER_STEP
    sp = _band_specs(t, hb)

    def body(q_ref, kp_ref, kc_ref, vtp_ref, vtc_ref, bp_ref, bc_ref, o_ref, lse_ref):
        for h in range(hb):
            s0 = _nt(kp_ref[h], q_ref[h]) + bp_ref[h, 0]
            s1 = _nt(kc_ref[h], q_ref[h]) + bc_ref[h, 0]
            m = jnp.maximum(jnp.max(s0, axis=0, keepdims=True), jnp.max(s1, axis=0, keepdims=True))
            ot = (_nn(vtp_ref[h], jnp.exp2(s0 - m).astype(BF16)) +
                  _nn(vtc_ref[h], jnp.exp2(s1 - m).astype(BF16)))
            o_ref[h], lse_ref[h] = _finish_softmax(ot, m)

    return _call(
        body, name="band_fwd", grid=(H // hb, S // t),
        in_specs=[sp['cur'], sp['prev'], sp['cur'], sp['prev_t'], sp['cur_t'], sp['bias_prev'], sp['bias_cur']],
        out_specs=[sp['cur'], sp['stat']],
        out_shape=[jax.ShapeDtypeStruct((H, S, HEAD_PAD), BF16), jax.ShapeDtypeStruct((H, 1, S), F32)],
        compiler_params=_params(2))(q, k, k, vt, vt, bias, bias)


def band_bwd(q, k, kt, v, do, lse, delta, bias):
    H, S, _ = q.shape
    t = ATT_T
    sp = _band_specs(t, 1)

    def body(q_ref, kp_ref, kc_ref, ktp_ref, ktc_ref, vp_ref, vc_ref, do_ref, lse_ref, dl_ref, bp_ref, bc_ref,
             dq_ref, dk_ref, dv_ref, db_ref):
        i = pl.program_id(1)

        @pl.when(i == 0)
        def _():
            dk_ref[...] = jnp.zeros_like(dk_ref)
            dv_ref[...] = jnp.zeros_like(dv_ref)
            db_ref[...] = jnp.zeros_like(db_ref)

        qv, dov = q_ref[0], do_ref[0]
        dqt = jnp.zeros((HEAD_PAD, t), F32)
        windows = ((0, jnp.maximum(i - 1, 0), kp_ref, ktp_ref, vp_ref, bp_ref),
                   (1, i, kc_ref, ktc_ref, vc_ref, bc_ref))
        for w, blk, k_ref, kt_ref, v_ref, b_ref in windows:
            rows = pl.ds(pl.multiple_of(blk * t, t), t)
            pt = jnp.exp2(_nt(k_ref[0], qv) + b_ref[0, 0] - lse_ref[0])
            dv_ref[0, rows, :] += _nn(pt.astype(BF16), dov)
            ds = pt * (_nt(v_ref[0], dov) - dl_ref[0])
            db_ref[0, w] += ds
            dsb = ds.astype(BF16)
            dk_ref[0, rows, :] += _nn(dsb, qv)
            dqt = dqt + _nn(kt_ref[0], dsb)
        dq_ref[0] = dqt.T.astype(BF16)

    whole = pl.BlockSpec((1, S, HEAD_PAD), lambda h, i: (h, 0, 0))
    return _call(
        body, name="band_bwd", grid=(H, S // t),
        in_specs=[sp['cur'], sp['prev'], sp['cur'], sp['prev_t'], sp['cur_t'], sp['prev'], sp['cur'], sp['cur'],
                  sp['stat'], sp['stat'], sp['bias_prev'], sp['bias_cur']],
        out_specs=[sp['cur'], whole, whole, pl.BlockSpec((1, 2, t, t), lambda h, i: (h, 0, 0, 0))],
        out_shape=[jax.ShapeDtypeStruct((H, S, HEAD_PAD), BF16), jax.ShapeDtypeStruct((H, S, HEAD_PAD), F32),
                   jax.ShapeDtypeStruct((H, S, HEAD_PAD), F32), jax.ShapeDtypeStruct((H, 2, t, t), F32)],
        compiler_params=_params(2))(q, k, k, kt, kt, v, v, do, lse, delta, bias, bias)


def _compact(o_ref):
    return jnp.concatenate([o_ref[2 * p].astype(F32) + pltpu.roll(o_ref[2 * p + 1].astype(F32), 64, 1)
                            for p in range(4)], axis=1)


def merge_fwd(ob, oc, proj, ya, gate_b, wbr, w_out, x, post_g):
    S = x.shape[0]
    ts = ROW_TILE

    def body(ob_ref, oc_ref, zb_ref, zc_ref, ya_ref, gl_ref, gb_ref, wbr_ref, wo_ref, x_ref, pg_ref,
             xo_ref, yb_ref, yc_ref, mg_ref, out_ref):
        zb, zc = zb_ref[...], zc_ref[...]
        yb = (_compact(ob_ref) * (zb * _sigmoid(zb))).astype(BF16)
        yc = (_compact(oc_ref) * (zc * _sigmoid(zc))).astype(BF16)
        yb_ref[...] = yb
        yc_ref[...] = yc
        merged = jnp.zeros((ts, D_MODEL), F32)
        for n, y in enumerate((ya_ref[...], yb, yc)):
            cols = slice(n * D_MODEL, (n + 1) * D_MODEL)
            gate = _sigmoid(gl_ref[:, cols] + gb_ref[:, cols])
            merged = merged + gate * _nn(y, wbr_ref[n])
        mb = merged.astype(BF16)
        mg_ref[...] = mb
        out = _nn(mb, wo_ref[...])
        out_ref[...] = out
        normed, _ = _rms(out, pg_ref[...])
        xo_ref[...] = x_ref[...] + normed

    row = lambda w: pl.BlockSpec((ts, w), lambda i: (i, 0))
    col = lambda w, c: pl.BlockSpec((ts, w), lambda i: (i, c))
    full = lambda shape: pl.BlockSpec(shape, lambda i: (0,) * len(shape))
    return _call(
        body, name="merge_fwd", grid=(S // ts,),
        in_specs=[_hspec(ts), _hspec(ts), col(512, P_ZB // 512), col(512, P_ZC // 512), row(512),
                  col(3072, P_G // 3072), full((1, 3072)), full((3, BR_WIDTH, D_MODEL)),
                  full((D_MODEL, D_MODEL)), row(D_MODEL), full((1, D_MODEL))],
        out_specs=[row(D_MODEL), row(512), row(512), row(D_MODEL), row(D_MODEL)],
        out_shape=[jax.ShapeDtypeStruct((S, D_MODEL), F32), jax.ShapeDtypeStruct((S, 512), BF16),
                   jax.ShapeDtypeStruct((S, 512), BF16), jax.ShapeDtypeStruct((S, D_MODEL), BF16),
                   jax.ShapeDtypeStruct((S, D_MODEL), F32)],
        compiler_params=_params(1))(ob, oc, proj, proj, ya, proj, gate_b, wbr, w_out, x, post_g)


def gate_bwd(g, out, post_g, w_out, proj, gate_b, ya, yb, yc, wbr):
    S = g.shape[0]
    ts = ROW_TILE

    def body(g_ref, out_ref, pg_ref, wo_ref, gl_ref, gb_ref, ya_ref, yb_ref, yc_ref, wbr_ref,
             dp_ref, do_ref, dba_ref, dbb_ref, dbc_ref, dy_ref, ggb_ref, gp_ref):
        first = pl.program_id(0) == 0
        ov = out_ref[...]
        _, r = _rms(ov, pg_ref[...])
        dout, gp = _rms_bwd(ov, pg_ref[...], r, g_ref[...])
        db = dout.astype(BF16)
        do_ref[...] = db
        _accumulate(gp_ref, gp, first)
        dm = _nt(db, wo_ref[...])
        ggb = []
        for n, (y_ref, dbr_ref) in enumerate(((ya_ref, dba_ref), (yb_ref, dbb_ref), (yc_ref, dbc_ref))):
            cols = slice(n * D_MODEL, (n + 1) * D_MODEL)
            br = _nn(y_ref[...], wbr_ref[n])
            sg = _sigmoid(gl_ref[:, cols] + gb_ref[:, cols])
            dgl = dm * br * (sg * (1.0 - sg))
            dp_ref[:, cols] = dgl.astype(BF16)
            ggb.append(jnp.sum(dgl, axis=0, keepdims=True))
            dbr = (dm * sg).astype(BF16)
            dbr_ref[...] = dbr
            dy_ref[n] = _nt(dbr, wbr_ref[n])
        _accumulate(ggb_ref, jnp.concatenate(ggb, axis=1), first)

    row = lambda w: pl.BlockSpec((ts, w), lambda i: (i, 0))
    full = lambda shape: pl.BlockSpec(shape, lambda i: (0,) * len(shape))
    wide = jax.ShapeDtypeStruct((S, D_MODEL), BF16)
    return _call(
        body, name="gate_bwd", grid=(S // ts,),
        in_specs=[row(D_MODEL), row(D_MODEL), full((1, D_MODEL)), full((D_MODEL, D_MODEL)),
                  pl.BlockSpec((ts, 3072), lambda i: (i, P_G // 3072)), full((1, 3072)),
                  row(512), row(512), row(512), full((3, BR_WIDTH, D_MODEL))],
        out_specs=[pl.BlockSpec((ts, 3072), lambda i: (i, P_G // 3072)), row(D_MODEL), row(D_MODEL), row(D_MODEL),
                   row(D_MODEL), pl.BlockSpec((3, ts, 512), lambda i: (0, i, 0)), full((1, 3072)),
                   full((1, D_MODEL))],
        out_shape=[jax.ShapeDtypeStruct((S, P_W), BF16), wide, wide, wide, wide,
                   jax.ShapeDtypeStruct((3, S, 512), F32), jax.ShapeDtypeStruct((1, 3072), F32),
                   jax.ShapeDtypeStruct((1, D_MODEL), F32)],
        compiler_params=_params(1))(g, out, post_g, w_out, proj, gate_b, ya, yb, yc, wbr)


def ungate_bwd(dproj, dy, ob, oc, proj):
    S = proj.shape[0]
    ts = LIGHT_ROW_TILE

    def body(dp_in, dyb_ref, dyc_ref, ob_ref, oc_ref, zb_ref, zc_ref, dp_ref, dob_ref, doc_ref, dlb_ref, dlc_ref):
        del dp_in
        lane = lax.broadcasted_iota(jnp.int32, (ts, 128), 1)
        for n, (dy_ref, o_ref, z_ref, do_ref, dl_ref) in enumerate(
                ((dyb_ref, ob_ref, zb_ref, dob_ref, dlb_ref), (dyc_ref, oc_ref, zc_ref, doc_ref, dlc_ref))):
            zz = z_ref[...]
            dyv = dy_ref[0]
            sg = _sigmoid(zz)
            dp_ref[:, n * 512:(n + 1) * 512] = (dyv * _compact(o_ref) * (sg * (1.0 + zz * (1.0 - sg)))).astype(BF16)
            do_c = dyv * (zz * sg)
            for p in range(4):
                piece = do_c[:, p * 128:(p + 1) * 128]
                for h, d in ((2 * p, jnp.where(lane < 64, piece, 0.0)),
                             (2 * p + 1, jnp.where(lane < 64, pltpu.roll(piece, 64, 1), 0.0))):
                    do_ref[h] = d.astype(BF16)
                    dl_ref[h] = jnp.sum((d * o_ref[h].astype(F32)).T, axis=0, keepdims=True)

    col = lambda c: pl.BlockSpec((ts, 512), lambda i: (i, c))
    dysp = lambda n: pl.BlockSpec((1, ts, 512), lambda i: (n, i, 0))
    stat = pl.BlockSpec((N_HEADS, 1, ts), lambda i: (0, 0, i))
    hshape = jax.ShapeDtypeStruct((N_HEADS, S, HEAD_PAD), BF16)
    sshape = jax.ShapeDtypeStruct((N_HEADS, 1, S), F32)
    return _call(
        body, name="ungate_bwd", grid=(S // ts,),
        in_specs=[pl.BlockSpec(memory_space=pl.ANY), dysp(1), dysp(2), _hspec(ts), _hspec(ts),
                  col(P_ZB // 512), col(P_ZC // 512)],
        out_specs=[pl.BlockSpec((ts, 1024), lambda i: (i, P_ZB // 1024)), _hspec(ts), _hspec(ts), stat, stat],
        out_shape=[jax.ShapeDtypeStruct(dproj.shape, BF16), hshape, hshape, sshape, sshape],
        input_output_aliases={0: 0},
        compiler_params=_params(1))(dproj, dy, dy, ob, oc, proj, proj)


def loss_head(y, target):
    S, D = y.shape
    ts = LIGHT_ROW_TILE

    def body(y_ref, t_ref, dy_ref, sq_ref):
        d = y_ref[...] - t_ref[...]
        dy_ref[...] = d * (1.0 / D)
        _accumulate(sq_ref, jnp.sum(d * d, axis=0, keepdims=True), pl.program_id(0) == 0)

    row = pl.BlockSpec((ts, D), lambda i: (i, 0))
    return _call(
        body, name="loss_head", grid=(S // ts,), in_specs=[row, row],
        out_specs=[row, pl.BlockSpec((1, D), lambda i: (0, 0))],
        out_shape=[jax.ShapeDtypeStruct((S, D), F32), jax.ShapeDtypeStruct((1, D), F32)],
        compiler_params=_params(1))(y, target)


def _row_tile(rows, cols):
    for cand in (1024, 512, 256, 128, 64, 32, 16, 8):
        if rows % cand == 0 and cand * cols * 4 <= 1024 * 1024:
            return cand
    return rows


def adamw(w, grads, m, v):
    shape = w.shape
    cols = shape[-1]
    rows = int(np.prod(shape[:-1]))
    tr = _row_tile(rows, cols)
    n_g = len(grads)
    c1 = 1.0 - ADAM_B1 ** ADAM_STEP
    c2 = 1.0 - ADAM_B2 ** ADAM_STEP

    def body(*refs):
        w_ref, m_ref, v_ref = refs[:3]
        g_refs = refs[3:3 + n_g]
        go_ref, d_ref, mo_ref, vo_ref = refs[3 + n_g:]
        gv = g_refs[0][...]
        for g_ref in g_refs[1:]:
            gv = gv + g_ref[...]
        go_ref[...] = gv
        mn = ADAM_B1 * m_ref[...] + (1.0 - ADAM_B1) * gv
        vn = ADAM_B2 * v_ref[...] + (1.0 - ADAM_B2) * (gv * gv)
        mo_ref[...] = mn
        vo_ref[...] = vn
        d_ref[...] = -ADAM_LR * ((mn / c1) / (jnp.sqrt(vn / c2) + ADAM_EPS) + ADAM_WD * w_ref[...])

    blk = pl.BlockSpec((tr, cols), lambda i: (i, 0))
    sds = jax.ShapeDtypeStruct((rows, cols), F32)
    outs = _call(
        body, name="adamw", grid=(rows // tr,), in_specs=[blk] * (3 + n_g), out_specs=[blk] * 4,
        out_shape=[sds] * 4, compiler_params=_params(1))(
            *[a.reshape(rows, cols) for a in (w, m, v, *grads)])
    return [o.reshape(shape) for o in outs]


def add_lead(parts):
    n = parts.shape[0]
    shape = parts.shape[1:]
    cols = shape[-1]
    rows = int(np.prod(shape[:-1]))
    tr = _row_tile(rows, cols * n)

    def body(p_ref, o_ref):
        acc = p_ref[0].astype(F32)
        for s in range(1, n):
            acc = acc + p_ref[s].astype(F32)
        o_ref[...] = acc

    out = _call(
        body, name="add_lead", grid=(rows // tr,),
        in_specs=[pl.BlockSpec((n, tr, cols), lambda i: (0, i, 0))],
        out_specs=pl.BlockSpec((tr, cols), lambda i: (i, 0)),
        out_shape=jax.ShapeDtypeStruct((rows, cols), F32),
        compiler_params=_params(1))(parts.reshape(n, rows, cols))
    return out.reshape(shape)


ANY = pl.BlockSpec(memory_space=pl.ANY)


def _other_chips(x, y):
    return [(1 - x, y), (x, 1 - y), (1 - x, 1 - y)]


def chip_exchange(arrays, scatter, name):
    n = len(arrays)

    def body(*refs):
        _exchange_issue(refs[:n], refs[n:2 * n], refs[2 * n:], scatter, True)
        _exchange_issue(refs[:n], refs[n:2 * n], refs[2 * n:], scatter, False)

    return _call(
        body, name=name, in_specs=[ANY] * n, out_specs=[ANY] * n,
        out_shape=_exchange_out_shape(arrays, scatter), scratch_shapes=_exchange_sems(n))(*arrays)


def _exchange_out_shape(arrays, scatter):
    return [jax.ShapeDtypeStruct(a.shape if scatter else (4,) + a.shape, a.dtype) for a in arrays]


def _exchange_sems(n):
    if n == 0:
        return []
    return [pltpu.SemaphoreType.DMA((3 * n,)), pltpu.SemaphoreType.DMA((3 * n,)), pltpu.SemaphoreType.DMA((n,))]


def _exchange_issue(srcs, outs, sems, scatter, start):
    send_sems, recv_sems, local_sems = sems
    x, y, c = lax.axis_index("x"), lax.axis_index("y"), lax.axis_index("c")
    me = 2 * x + y
    for a in range(len(srcs)):
        local_src = srcs[a].at[me] if scatter else srcs[a]
        mine = pltpu.make_async_copy(local_src, outs[a].at[me], local_sems.at[a])
        sends = []
        for j, (px, py) in enumerate(_other_chips(x, y)):
            pair = dict(send_sem=send_sems.at[3 * a + j], recv_sem=recv_sems.at[3 * a + j],
                        device_id=(px, py, c), device_id_type=MESH)
            sends.append(pltpu.make_async_remote_copy(
                src_ref=srcs[a].at[2 * px + py] if scatter else srcs[a], dst_ref=outs[a].at[me], **pair))
            if not start:
                pltpu.make_async_remote_copy(src_ref=local_src, dst_ref=outs[a].at[2 * px + py], **pair).wait_recv()
        if start:
            mine.start()
            for cp in sends:
                cp.start()
        else:
            for cp in sends:
                cp.wait_send()
            mine.wait()


def sibling_exchange(arrays):
    n = len(arrays)

    def body(*refs):
        srcs, outs = refs[:n], refs[n:2 * n]
        send_sems, recv_sems = refs[2 * n:]
        x, y, c = lax.axis_index("x"), lax.axis_index("y"), lax.axis_index("c")
        copies = [pltpu.make_async_remote_copy(src_ref=srcs[a], dst_ref=outs[a], send_sem=send_sems.at[a],
                                               recv_sem=recv_sems.at[a], device_id=(x, y, 1 - c), device_id_type=MESH)
                  for a in range(n)]
        for cp in copies:
            cp.start()
        for cp in copies:
            cp.wait()

    return _call(
        body, name="sibling_exchange", in_specs=[ANY] * n, out_specs=[ANY] * n,
        out_shape=[jax.ShapeDtypeStruct(a.shape, a.dtype) for a in arrays],
        scratch_shapes=[pltpu.SemaphoreType.DMA((n,)), pltpu.SemaphoreType.DMA((n,))])(*arrays)


def _perm_from_shards(sh):
    rows = sh.shape[1]
    pieces, pos = [], 0
    for lo, hi, plo in sorted(NAT_SEGS, key=lambda s: s[2]):
        if plo > pos:
            pieces.append(jnp.zeros((rows, plo - pos), sh.dtype))
            pos = plo
        c = lo
        while c < hi:
            kk = c // SHARD_COLS
            e = min(hi, (kk + 1) * SHARD_COLS)
            pieces.append(sh[kk][:, c - kk * SHARD_COLS:e - kk * SHARD_COLS])
            c = e
        pos += hi - lo
    if pos < P_W:
        pieces.append(jnp.zeros((rows, P_W - pos), sh.dtype))
    return jnp.concatenate(pieces, axis=1)


def _shards_from_perm(p):
    out = []
    for kk in range(4):
        lo_k, hi_k = kk * SHARD_COLS, (kk + 1) * SHARD_COLS
        pieces = []
        for lo, hi, plo in NAT_SEGS:
            a, b = max(lo, lo_k), min(hi, hi_k)
            if a < b:
                pieces.append(p[:, plo + (a - lo):plo + (b - lo)])
        out.append(jnp.concatenate(pieces, axis=1))
    return jnp.stack(out)


def _split4(a, axis):
    shape = a.shape
    a = a.reshape(shape[:axis] + (4, shape[axis] // 4) + shape[axis + 1:])
    return jnp.moveaxis(a, axis, 0)


def _join4(a, axis):
    a = jnp.moveaxis(a, 0, axis)
    shape = a.shape
    return a.reshape(shape[:axis] + (4 * shape[axis + 1],) + shape[axis + 2:])


def _pad_heads(w, per_head, lo, hi):
    r = w.shape[0]
    wh = w.reshape(r, N_HEADS, per_head)[:, :, lo:hi]
    return jnp.pad(wh, ((0, 0), (0, 0), (0, HEAD_PAD - (hi - lo)))).reshape(r, N_HEADS * HEAD_PAD)


def _rope_table(S):
    half = MLA_ROPE // 2
    inv = ROPE_BASE ** (-jnp.arange(half, dtype=F32) / half)
    ang = jnp.arange(S).astype(F32)[:, None] * inv[None, :]
    cos, sin = jnp.cos(ang), jnp.sin(ang)
    z = lambda n: jnp.zeros((S, n), F32)
    c = jnp.concatenate([jnp.ones((S, MLA_NOPE), F32), cos, cos, z(32)], axis=1)
    a = jnp.concatenate([z(MLA_NOPE), -sin, z(48)], axis=1)
    b = jnp.concatenate([z(MLA_NOPE + half), sin, z(32)], axis=1)
    return jnp.stack([c, a, b])


def _band_onehot():
    t = ATT_T
    m = np.arange(2 * t)
    d = np.where(m < t, m, m - 2 * t)
    idx = np.stack([np.clip(off + d, -REL_CLIP, REL_CLIP) + REL_CLIP for off in (t, 0)])
    return (idx[:, :, None] == np.arange(2 * REL_CLIP + 1)[None, None, :]).astype(np.float32)


def bias_expand(diag):
    t = ATT_T

    def body(d_ref, o_ref):
        kc = lax.broadcasted_iota(jnp.int32, (t, t), 0) >> CHUNK_SHIFT
        qc = lax.broadcasted_iota(jnp.int32, (t, t), 1) >> CHUNK_SHIFT
        for w, visible in ((0, kc >= qc), (1, kc <= qc)):
            rows = jnp.broadcast_to(d_ref[0, w:w + 1, :], (t, 2 * t))
            skew = pltpu.roll(rows, 0, 1, stride=1, stride_axis=0)[:, :t]
            o_ref[0, 2 * w] = jnp.where(visible, skew * LOG2E, NEG_INF)
        o_ref[0, 1] = jnp.full((t, t), NEG_INF, F32)

    return _call(
        body, name="bias_expand", grid=(N_HEADS,),
        in_specs=[pl.BlockSpec((1, 2, 2 * t), lambda h: (h, 0, 0))],
        out_specs=pl.BlockSpec((1, 3, t, t), lambda h: (h, 0, 0, 0)),
        out_shape=jax.ShapeDtypeStruct((N_HEADS, 3, t, t), F32),
        compiler_params=_params(1))(diag)


def bias_fold(dtiles):
    t = ATT_T

    def body(d_ref, o_ref):
        pad = jnp.zeros((8, t), F32)
        for w in range(2):
            acc = jnp.concatenate([d_ref[0, w, 0:8, :], pad], axis=1)
            for g in range(1, t // 8):
                grp = jnp.concatenate([d_ref[0, w, 8 * g:8 * g + 8, :], pad], axis=1)
                acc = acc + pltpu.roll(grp, 2 * t - 8 * g, 1)
            out = acc[0:1, :]
            for s in range(1, 8):
                out = out + pltpu.roll(acc, 2 * t - s, 1)[s:s + 1, :]
            o_ref[0, w:w + 1, :] = out

    return _call(
        body, name="bias_fold", grid=(N_HEADS,),
        in_specs=[pl.BlockSpec((1, 2, t, t), lambda h: (h, 0, 0, 0))],
        out_specs=pl.BlockSpec((1, 2, 2 * t), lambda h: (h, 0, 0)),
        out_shape=jax.ShapeDtypeStruct((N_HEADS, 2, 2 * t), F32),
        compiler_params=_params(1))(dtiles)


def _bias_tiles(table):
    diag = jnp.einsum('hr,wdr->hwd', table, jnp.asarray(_band_onehot()), precision=lax.Precision.HIGHEST)
    return bias_expand(diag)


def _bias_tiles_grad(dtiles):
    return jnp.einsum('hwd,wdr->hr', bias_fold(dtiles), jnp.asarray(_band_onehot()),
                      precision=lax.Precision.HIGHEST)


def _layer_consts(lw):
    tri = np.tril(np.ones((SGU_BLOCK, SGU_BLOCK), np.float32))
    ws = (lw['sgu_w'] * tri).astype(BF16)
    return dict(
        ws=ws, ws_t=jnp.swapaxes(ws, 1, 2), sgu_bias=jnp.repeat(lw['sgu_b'].T, CA_HEAD_DIM, axis=1),
        bias=_bias_tiles(lw['ca_rel_bias']),
        wq=_pad_heads(lw['mla_w_uq'], MLA_QK, 0, MLA_QK),
        wk=_pad_heads(lw['mla_w_ukv'], MLA_NOPE + MLA_V, 0, MLA_NOPE),
        wv=_pad_heads(lw['mla_w_ukv'], MLA_NOPE + MLA_V, MLA_NOPE, MLA_NOPE + MLA_V),
        gate_b=lw['gate_b'].reshape(1, 3 * D_MODEL),
        pre_g=lw['pre_g'][None], post_g=lw['post_g'][None], ln_g=lw['sgu_ln_g'][None], ln_b=lw['sgu_ln_b'][None],
        qg=lw['mla_q_norm_g'][None], kvg=lw['mla_kv_norm_g'][None])


def _layer_fwd(x, lw, k, tab, next_shards):
    proj, xn = norm_matmul(x, k['pre_g'], lw['w_in'])
    ya = sgu_fwd(proj, k['ln_g'], k['ln_b'], k['ws'], k['sgu_bias'])
    qb, kb, vb, qc, kc, vc, kbt, vbt, kct, vct, cq, ckv = prep_fwd(proj, tab, k['qg'], k['kvg'], k['wq'], k['wk'],
                                                                   k['wv'])
    ob, lse_b, gathered = mla_fwd(qb, kb, vbt, next_shards)
    oc, lse_c = band_fwd(qc, kc, vct, k['bias'])
    x_new, yb, yc, merged, out = merge_fwd(ob, oc, proj, ya, k['gate_b'], lw['w_branch'], lw['w_out'], x,
                                           k['post_g'])
    saved = dict(x=x, proj=proj, xn=xn, ya=ya, yb=yb, yc=yc, qb=qb, kb=kb, vb=vb, qc=qc, kc=kc, vc=vc, kbt=kbt, kct=kct,
                 cq=cq, ckv=ckv, ob=ob, oc=oc, lse_b=lse_b, lse_c=lse_c, merged=merged, out=out)
    return x_new, saved, gathered


def _layer_bwd(g, s, lw, k, tab, pending_parts, scatter_own):
    S = g.shape[0]
    H = N_HEADS
    dproj, dout, dba, dbb, dbc, dy, g_gate_b, g_post = gate_bwd(
        g, s['out'], k['post_g'], lw['w_out'], s['proj'], k['gate_b'], s['ya'], s['yb'], s['yc'], lw['w_branch'])
    g_w_out = matmul_tn(s['merged'], dout, 512)
    g_w_branch = jnp.stack([matmul_tn(y, d, 512) for y, d in ((s['ya'], dba), (s['yb'], dbb), (s['yc'], dbc))])
    dproj, dob, doc, dl_b, dl_c = ungate_bwd(dproj, dy, s['ob'], s['oc'], s['proj'])
    dqb, dkb, dvb, landed = mla_bwd(s['qb'], s['kb'], s['kbt'], s['vb'], dob, s['lse_b'], dl_b, pending_parts)
    dqc, dkc, dvc, dbias = band_bwd(s['qc'], s['kc'], s['kct'], s['vc'], doc, s['lse_c'], dl_c, k['bias'])
    dproj, dqf, dkf, dvf, g_qg, g_kvg = prep_bwd(dproj, dqb, dkb, dvb, dqc, dkc, dvc, s['proj'], tab,
                                                 k['qg'], k['kvg'], k['wq'], k['wk'], k['wv'])
    g_wq = matmul_tn(s['cq'], dqf, 512).reshape(MLA_Q_RANK, H, HEAD_PAD)[:, :, :MLA_QK]
    g_wk = matmul_tn(s['ckv'], dkf, 512).reshape(MLA_KV_RANK, H, HEAD_PAD)[:, :, :MLA_NOPE]
    g_wv = matmul_tn(s['ckv'], dvf, 512).reshape(MLA_KV_RANK, H, HEAD_PAD)[:, :, :MLA_V]
    dproj, g_ln_g, g_ln_b, g_ws, g_sgu_bias = sgu_bwd(dproj, dy, s['proj'], k['ln_g'], k['ln_b'], k['ws'],
                                                      k['ws_t'], k['sgu_bias'])
    g_w_in = matmul_tn(s['xn'], dproj, MM_TN)
    sharded = _sharded_parts(dict(
        w_in=g_w_in, mla_w_uq=g_wq.reshape(MLA_Q_RANK, H * MLA_QK),
        mla_w_ukv=jnp.concatenate([g_wk, g_wv], axis=2).reshape(MLA_KV_RANK, H * (MLA_NOPE + MLA_V)),
        w_branch=g_w_branch, gate_b=g_gate_b.reshape(N_BRANCH, D_MODEL), w_out=g_w_out))
    dx, g_pre, own_landed = proj_bwd_x(dproj, lw['w_in'], s['x'], k['pre_g'], g, sharded if scatter_own else ())
    tri = np.tril(np.ones((SGU_BLOCK, SGU_BLOCK), np.float32))
    small = _small_pack(dict(
        pre_g=g_pre[0], post_g=g_post[0], sgu_ln_g=g_ln_g[0], sgu_ln_b=g_ln_b[0],
        sgu_w=g_ws * tri, sgu_b=jnp.sum(g_sgu_bias.reshape(SGU_BLOCK, 8, CA_HEAD_DIM), axis=2).T,
        mla_q_norm_g=g_qg[0], mla_kv_norm_g=g_kvg[0], ca_rel_bias=_bias_tiles_grad(dbias)))
    return dx, (own_landed if scatter_own else sharded), small, landed


BF16_PARTS = ('w_in', 'mla_w_uq', 'mla_w_ukv', 'w_branch', 'w_out')


def _weight_shards(w, l):
    return [w[n][l].astype(BF16) if n in BF16_PARTS else w[n][l] for n in SHARDED]


def _full_weights(gathered, small):
    lw = {n: _join4(a, SHARD_AXIS[n]) for n, a in zip(SHARDED, gathered) if n != 'w_in'}
    lw['w_in'] = _perm_from_shards(gathered[0])
    lw.update(small)
    return lw


def _small_pack(grads):
    flat = jnp.concatenate([grads[n].reshape(-1) for n in SMALL])
    quarter = -(-flat.size // (4 * 1024)) * 1024
    return jnp.pad(flat, (0, 4 * quarter - flat.size)).reshape(4, quarter // 128, 128)


def _sharded_parts(grads):
    parts = [_shards_from_perm(grads['w_in'])]
    parts += [_split4(grads[n], SHARD_AXIS[n]) for n in SHARDED if n != 'w_in']
    return [p.astype(BF16) if n in BF16_PARTS else p for n, p in zip(SHARDED, parts)]


def train_step_local(x, target, w):
    S = x.shape[0]
    depth = w['w_in'].shape[0]
    tab = _rope_table(S)
    gathered = chip_exchange(_weight_shards(w, 0), False, "gather_weights")
    layer_w, consts, saved = [], [], []
    for l in range(depth):
        lw = _full_weights(gathered, {n: w[n][l] for n in SMALL})
        k = _layer_consts(lw)
        x, s, gathered = _layer_fwd(x, lw, k, tab, _weight_shards(w, l + 1) if l + 1 < depth else ())
        layer_w.append(lw)
        consts.append(k)
        saved.append(s)
    g, sq = loss_head(x, target)
    mine = [None] * depth
    pending = ()
    for l in reversed(range(depth)):
        g, sharded, small, landed = _layer_bwd(g, saved[l], layer_w[l], consts[l], tab, pending, l == 0)
        if pending:
            mine[l + 1] = [add_lead(p) for p in landed]
        pending = list(sharded) + [small]
    mine[0] = [add_lead(p) for p in pending[:-1] + list(chip_exchange(pending[-1:], True, "scatter_small"))]
    n_parts = len(mine[0])
    theirs = sibling_exchange([p for layer in mine for p in layer])
    return sq, g, [(mine[l], theirs[l * n_parts:(l + 1) * n_parts]) for l in range(depth)]


def kernel(x, w_in, pre_g, post_g, sgu_ln_g, sgu_ln_b, sgu_w, sgu_b, mla_q_norm_g, mla_kv_norm_g, mla_w_uq, mla_w_ukv, ca_rel_bias, w_branch, gate_b, w_out, loss_target, m_w_in, m_pre_g, m_post_g, m_sgu_ln_g, m_sgu_ln_b, m_sgu_w, m_sgu_b, m_mla_q_norm_g, m_mla_kv_norm_g, m_mla_w_uq, m_mla_w_ukv, m_ca_rel_bias, m_w_branch, m_gate_b, m_w_out, v_w_in, v_pre_g, v_post_g, v_sgu_ln_g, v_sgu_ln_b, v_sgu_w, v_sgu_b, v_mla_q_norm_g, v_mla_kv_norm_g, v_mla_w_uq, v_mla_w_ukv, v_ca_rel_bias, v_w_branch, v_gate_b, v_w_out):
    w = dict(w_in=w_in, pre_g=pre_g, post_g=post_g, sgu_ln_g=sgu_ln_g, sgu_ln_b=sgu_ln_b, sgu_w=sgu_w, sgu_b=sgu_b,
             mla_q_norm_g=mla_q_norm_g, mla_kv_norm_g=mla_kv_norm_g, mla_w_uq=mla_w_uq, mla_w_ukv=mla_w_ukv,
             ca_rel_bias=ca_rel_bias, w_branch=w_branch, gate_b=gate_b, w_out=w_out)
    m = dict(w_in=m_w_in, pre_g=m_pre_g, post_g=m_post_g, sgu_ln_g=m_sgu_ln_g, sgu_ln_b=m_sgu_ln_b, sgu_w=m_sgu_w,
             sgu_b=m_sgu_b, mla_q_norm_g=m_mla_q_norm_g, mla_kv_norm_g=m_mla_kv_norm_g, mla_w_uq=m_mla_w_uq,
             mla_w_ukv=m_mla_w_ukv, ca_rel_bias=m_ca_rel_bias, w_branch=m_w_branch, gate_b=m_gate_b, w_out=m_w_out)
    v = dict(w_in=v_w_in, pre_g=v_pre_g, post_g=v_post_g, sgu_ln_g=v_sgu_ln_g, sgu_ln_b=v_sgu_ln_b, sgu_w=v_sgu_w,
             sgu_b=v_sgu_b, mla_q_norm_g=v_mla_q_norm_g, mla_kv_norm_g=v_mla_kv_norm_g, mla_w_uq=v_mla_w_uq,
             mla_w_ukv=v_mla_w_ukv, ca_rel_bias=v_ca_rel_bias, w_branch=v_w_branch, gate_b=v_gate_b, w_out=v_w_out)
    depth = w_in.shape[0]
    sq, grad_x, reduced = train_step_local(x[0], loss_target[0], w)
    loss = lax.psum(0.5 * jnp.sum(sq) / D_MODEL, ("x", "y", "c"))

    out = {}
    for a, n in enumerate(SHARDED):
        mine = jnp.stack([reduced[l][0][a] for l in range(depth)])
        theirs = jnp.stack([reduced[l][1][a] for l in range(depth)])
        out[n] = adamw(w[n], [mine, theirs], m[n], v[n])
    small = jnp.stack([jnp.stack([reduced[l][0][-1] for l in range(depth)]),
                       jnp.stack([reduced[l][1][-1] for l in range(depth)])])
    quarter = add_lead(small)
    full = chip_exchange([quarter], False, "gather_small")[0]
    full = jnp.moveaxis(full, 0, 1).reshape(depth, -1)
    off = 0
    for n in SMALL:
        size = int(np.prod(w[n].shape[1:]))
        out[n] = adamw(w[n], [full[:, off:off + size].reshape(w[n].shape)], m[n], v[n])
        off += size
    return (loss, grad_x[None], *[out[n][0] for n in WEIGHTS], *[out[n][1] for n in WEIGHTS],
            *[out[n][2] for n in WEIGHTS], *[out[n][3] for n in WEIGHTS])
```

```python
import numpy as np
import jax
import jax.numpy as jnp
from jax import lax
from jax.experimental import pallas as pl
from jax.experimental.pallas import tpu as pltpu

F32 = jnp.float32
BF16 = jnp.bfloat16
MESH = pl.DeviceIdType.MESH

EPS = 1e-6
NEG_INF = -1e30
D_MODEL = 1024
BR_WIDTH = 512
N_BRANCH = 3
N_HEADS = 8
HEAD_PAD = 128
CHUNK_SHIFT = 6
SGU_BLOCK = 128
MLA_NOPE, MLA_ROPE, MLA_V = 64, 32, 64
MLA_QK = MLA_NOPE + MLA_ROPE
MLA_Q_RANK, MLA_KV_RANK = 256, 128
CA_HEAD_DIM = 64
REL_CLIP = 128
ROPE_BASE = 10000.0
D_IN = 7584

ADAM_LR, ADAM_B1, ADAM_B2, ADAM_EPS, ADAM_WD, ADAM_STEP = 0.001, 0.9, 0.999, 1e-08, 0.01, 10

P_QC, P_KC, P_VC, P_QD, P_KVD, P_KR, P_ZB, P_ZC, P_G, P_U, P_V, P_ZA, P_W = (
    0, 512, 1024, 1536, 1792, 1920, 2048, 2560, 3072, 6144, 6656, 7168, 7680)
NAT_SEGS = [(0, 1536, P_U), (1536, 1920, P_QD), (1920, 1952, P_KR + MLA_NOPE), (1952, 2464, P_ZB),
            (2464, 4000, P_QC), (4000, 4512, P_ZC), (4512, 7584, P_G)]
SHARD_COLS = D_IN // 4

VMEM_LIMIT = 48 * 1024 * 1024
ATT_T = 512
BAND_FWD_HEADS_PER_STEP = 4
MLA_FWD_HEADS_PER_STEP = 8
MLA_BWD_HEADS_PER_STEP = 4
ROW_TILE = 256
LIGHT_ROW_TILE = 512
MM_TM = 512
MM_TN = 1536
LOG2E = 1.4426950408889634
MLA_SCALE = MLA_QK ** -0.5
CA_SCALE = CA_HEAD_DIM ** -0.5

WEIGHTS = ['w_in', 'pre_g', 'post_g', 'sgu_ln_g', 'sgu_ln_b', 'sgu_w', 'sgu_b', 'mla_q_norm_g',
           'mla_kv_norm_g', 'mla_w_uq', 'mla_w_ukv', 'ca_rel_bias', 'w_branch', 'gate_b', 'w_out']
SHARDED = ['w_in', 'mla_w_uq', 'mla_w_ukv', 'w_branch', 'gate_b', 'w_out']
SMALL = ['pre_g', 'post_g', 'sgu_ln_g', 'sgu_ln_b', 'sgu_w', 'sgu_b', 'mla_q_norm_g',
         'mla_kv_norm_g', 'ca_rel_bias']
SHARD_AXIS = {'w_in': 1, 'mla_w_uq': 1, 'mla_w_ukv': 1, 'w_branch': 2, 'gate_b': 1, 'w_out': 0}


def _call(body, **kw):
    return pl.pallas_call(body, **kw)


def _params(n_axes):
    return pltpu.CompilerParams(dimension_semantics=("arbitrary",) * n_axes,
                                vmem_limit_bytes=VMEM_LIMIT)


def _nt(a, b):
    return lax.dot_general(a, b, (((1,), (1,)), ((), ())), preferred_element_type=F32)


def _nn(a, b):
    return jnp.dot(a, b, preferred_element_type=F32)


def _tn(a, b):
    return lax.dot_general(a, b, (((0,), (0,)), ((), ())), preferred_element_type=F32)


def _rms(xv, g):
    r = lax.rsqrt(jnp.mean(xv * xv, axis=-1, keepdims=True) + EPS)
    return xv * r * g, r


def _rms_bwd(xv, g, r, dy):
    gy = dy * g
    dx = r * gy - xv * (r * r * r) * jnp.mean(xv * gy, axis=-1, keepdims=True)
    dg = jnp.sum(dy * (xv * r), axis=0, keepdims=True)
    return dx, dg


def _sigmoid(z):
    return 1.0 / (1.0 + jnp.exp(-z))


def _rope(xv, c, a, b):
    return xv * c + pltpu.roll(xv, 112, 1) * a + pltpu.roll(xv, 16, 1) * b


def _accumulate(ref, val, first):
    @pl.when(first)
    def _():
        ref[...] = val

    @pl.when(jnp.logical_not(first))
    def _():
        ref[...] += val


def norm_matmul(x, g, w):
    S, D = x.shape
    N = w.shape[1]
    tm, tn = min(S, 2 * MM_TM), MM_TN

    def body(x_ref, g_ref, w_ref, o_ref, xnt_ref, xn_s):
        @pl.when(pl.program_id(1) == 0)
        def _():
            y, _ = _rms(x_ref[...], g_ref[...])
            xn_s[...] = y.astype(BF16)
            xnt_ref[...] = y.T.astype(BF16)

        o_ref[...] = _nn(xn_s[...], w_ref[...])

    return _call(
        body, name="norm_matmul", grid=(S // tm, N // tn),
        in_specs=[pl.BlockSpec((tm, D), lambda i, j: (i, 0)),
                  pl.BlockSpec((1, D), lambda i, j: (0, 0)),
                  pl.BlockSpec((D, tn), lambda i, j: (0, j))],
        out_specs=[pl.BlockSpec((tm, tn), lambda i, j: (i, j)),
                   pl.BlockSpec((D, tm), lambda i, j: (0, i))],
        out_shape=[jax.ShapeDtypeStruct((S, N), F32), jax.ShapeDtypeStruct((D, S), BF16)],
        scratch_shapes=[pltpu.VMEM((tm, D), BF16)],
        compiler_params=_params(2))(x, g, w)


def matmul_acc(a, b, tn):
    M, S = a.shape
    N = b.shape[1]
    tk = min(S, 2 * MM_TM)

    def body(a_ref, b_ref, o_ref):
        @pl.when(pl.program_id(1) == 0)
        def _():
            o_ref[...] = jnp.zeros_like(o_ref)

        o_ref[...] += _nn(a_ref[...], b_ref[...])

    return _call(
        body, name="matmul_acc", grid=(N // tn, S // tk),
        in_specs=[pl.BlockSpec((M, tk), lambda j, k: (0, k)),
                  pl.BlockSpec((tk, tn), lambda j, k: (k, j))],
        out_specs=pl.BlockSpec((M, tn), lambda j, k: (0, j)),
        out_shape=jax.ShapeDtypeStruct((M, N), F32),
        compiler_params=_params(2))(a, b)


def proj_bwd_x(dproj, w, x, g, resid, exchange=()):
    S, N = dproj.shape
    D = x.shape[1]
    tm, tk = min(S, MM_TM), MM_TN
    nk, n_ex = N // tk, len(exchange)

    def body(dp_ref, w_ref, x_ref, g_ref, r_ref, *rest):
        ex_src, (dx_ref, dg_ref), ex_out, (acc_ref,), ex_sems = _split_refs(
            rest, (n_ex, 2, n_ex, 1, 3 if n_ex else 0))
        i, k = pl.program_id(0), pl.program_id(1)

        if n_ex:
            @pl.when(jnp.logical_and(i == 0, k == 0))
            def _():
                _exchange_issue(ex_src, ex_out, ex_sems, True, True)

        @pl.when(k == 0)
        def _():
            acc_ref[...] = jnp.zeros_like(acc_ref)

        acc_ref[...] += _nt(dp_ref[...].astype(BF16), w_ref[...])

        @pl.when(k == nk - 1)
        def _():
            xv = x_ref[...]
            _, r = _rms(xv, g_ref[...])
            dx, dg = _rms_bwd(xv, g_ref[...], r, acc_ref[...])
            dx_ref[...] = dx + r_ref[...]
            _accumulate(dg_ref, dg, i == 0)

        if n_ex:
            @pl.when(jnp.logical_and(i == S // tm - 1, k == nk - 1))
            def _():
                _exchange_issue(ex_src, ex_out, ex_sems, True, False)

    outs = _call(
        body, name="proj_bwd_x_scatter" if n_ex else "proj_bwd_x", grid=(S // tm, nk),
        in_specs=[pl.BlockSpec((tm, tk), lambda i, k: (i, k)),
                  pl.BlockSpec((D, tk), lambda i, k: (0, k)),
                  pl.BlockSpec((tm, D), lambda i, k: (i, 0)),
                  pl.BlockSpec((1, D), lambda i, k: (0, 0)),
                  pl.BlockSpec((tm, D), lambda i, k: (i, 0))] + [ANY] * n_ex,
        out_specs=[pl.BlockSpec((tm, D), lambda i, k: (i, 0)),
                   pl.BlockSpec((1, D), lambda i, k: (0, 0))] + [ANY] * n_ex,
        out_shape=[jax.ShapeDtypeStruct((S, D), F32), jax.ShapeDtypeStruct((1, D), F32)] +
        _exchange_out_shape(exchange, True),
        scratch_shapes=[pltpu.VMEM((tm, D), F32)] + _exchange_sems(n_ex),
        compiler_params=_params(2))(dproj, w, x, g, resid, *exchange)
    return outs[0], outs[1], outs[2:]


def matmul_tn(a, b, tn):
    S, M = a.shape
    N = b.shape[1]
    tk = min(S, 2 * MM_TM)

    def body(a_ref, b_ref, o_ref):
        @pl.when(pl.program_id(1) == 0)
        def _():
            o_ref[...] = jnp.zeros_like(o_ref)

        o_ref[...] += _tn(a_ref[...].astype(BF16), b_ref[...].astype(BF16))

    return _call(
        body, name="matmul_tn", grid=(N // tn, S // tk),
        in_specs=[pl.BlockSpec((tk, M), lambda j, k: (k, 0)),
                  pl.BlockSpec((tk, tn), lambda j, k: (k, j))],
        out_specs=pl.BlockSpec((M, tn), lambda j, k: (0, j)),
        out_shape=jax.ShapeDtypeStruct((M, N), F32),
        compiler_params=_params(2))(a, b)


def _sgu_block(vv, g, b, ws_ref, lane):
    mu = jnp.mean(vv, axis=-1, keepdims=True)
    xc = vv - mu
    r = lax.rsqrt(jnp.mean(xc * xc, axis=-1, keepdims=True) + EPS)
    xhat = xc * r
    vln = (xhat * g + b).astype(BF16)
    pieces = []
    for p in range(4):
        vp = vln[:, p * 128:(p + 1) * 128]
        pieces.append(jnp.where(lane < 64, _nn(ws_ref[2 * p], vp), _nn(ws_ref[2 * p + 1], vp)))
    return xhat, r, vln, jnp.concatenate(pieces, axis=1)


def sgu_fwd(proj, ln_g, ln_b, ws, bias_full):
    S = proj.shape[0]
    ts = LIGHT_ROW_TILE

    def body(u_ref, v_ref, z_ref, g_ref, b_ref, ws_ref, bf_ref, y_ref):
        lane = lax.broadcasted_iota(jnp.int32, (SGU_BLOCK, 128), 1)
        for blk in range(ts // SGU_BLOCK):
            rows = slice(blk * SGU_BLOCK, (blk + 1) * SGU_BLOCK)
            _, _, _, mixed = _sgu_block(v_ref[rows, :], g_ref[...], b_ref[...], ws_ref, lane)
            mixed = mixed + bf_ref[...]
            zz = z_ref[rows, :]
            y_ref[rows, :] = (u_ref[rows, :] * mixed * (zz * _sigmoid(zz))).astype(BF16)

    col = lambda c: pl.BlockSpec((ts, BR_WIDTH), lambda i: (i, c))
    full = lambda shape: pl.BlockSpec(shape, lambda i: (0,) * len(shape))
    return _call(
        body, name="sgu_fwd", grid=(S // ts,),
        in_specs=[col(P_U // 512), col(P_V // 512), col(P_ZA // 512),
                  full((1, BR_WIDTH)), full((1, BR_WIDTH)), full((8, 128, 128)), full((128, BR_WIDTH))],
        out_specs=pl.BlockSpec((ts, BR_WIDTH), lambda i: (i, 0)),
        out_shape=jax.ShapeDtypeStruct((S, BR_WIDTH), BF16),
        compiler_params=_params(1))(proj, proj, proj, ln_g, ln_b, ws, bias_full)


def sgu_bwd(dproj, dy, proj, ln_g, ln_b, ws, ws_t, bias_full):
    S = proj.shape[0]
    ts = LIGHT_ROW_TILE

    def body(dp_in, dy_ref, u_ref, v_ref, z_ref, g_ref, b_ref, ws_ref, wst_ref, bf_ref,
             dp_ref, gg_ref, gb_ref, gws_ref, gbf_ref):
        del dp_in
        first = pl.program_id(0) == 0

        @pl.when(first)
        def _():
            gg_ref[...] = jnp.zeros_like(gg_ref)
            gb_ref[...] = jnp.zeros_like(gb_ref)
            gws_ref[...] = jnp.zeros_like(gws_ref)
            gbf_ref[...] = jnp.zeros_like(gbf_ref)

        lane = lax.broadcasted_iota(jnp.int32, (SGU_BLOCK, 128), 1)
        for blk in range(ts // SGU_BLOCK):
            rows = slice(blk * SGU_BLOCK, (blk + 1) * SGU_BLOCK)
            g = g_ref[...]
            xhat, r, vln, mixed = _sgu_block(v_ref[rows, :], g, b_ref[...], ws_ref, lane)
            mixed = mixed + bf_ref[...]
            zz = z_ref[rows, :]
            uu = u_ref[rows, :]
            dyv = dy_ref[0, rows, :]
            sg = _sigmoid(zz)
            sil = zz * sg
            dmixed = dyv * uu * sil
            dp_ref[rows, 0:512] = (dyv * mixed * sil).astype(BF16)
            dp_ref[rows, 1024:1536] = (dyv * uu * mixed * (sg * (1.0 + zz * (1.0 - sg)))).astype(BF16)
            gbf_ref[...] += dmixed
            dmb = dmixed.astype(BF16)
            pieces = []
            for p in range(4):
                dmp = dmb[:, p * 128:(p + 1) * 128]
                vp = vln[:, p * 128:(p + 1) * 128]
                pieces.append(jnp.where(lane < 64, _nn(wst_ref[2 * p], dmp), _nn(wst_ref[2 * p + 1], dmp)))
                zero = jnp.zeros_like(dmp)
                gws_ref[2 * p] += _nt(jnp.where(lane < 64, dmp, zero), vp)
                gws_ref[2 * p + 1] += _nt(jnp.where(lane >= 64, dmp, zero), vp)
            dvln = jnp.concatenate(pieces, axis=1)
            dxh = dvln * g
            dp_ref[rows, 512:1024] = (r * (dxh - jnp.mean(dxh, axis=-1, keepdims=True)
                                           - xhat * jnp.mean(dxh * xhat, axis=-1, keepdims=True))).astype(BF16)
            gg_ref[...] += jnp.sum(dvln * xhat, axis=0, keepdims=True)
            gb_ref[...] += jnp.sum(dvln, axis=0, keepdims=True)

    col = lambda c: pl.BlockSpec((ts, BR_WIDTH), lambda i: (i, c))
    full = lambda shape: pl.BlockSpec(shape, lambda i: (0,) * len(shape))
    return _call(
        body, name="sgu_bwd", grid=(S // ts,),
        in_specs=[pl.BlockSpec(memory_space=pl.ANY),
                  pl.BlockSpec((1, ts, BR_WIDTH), lambda i: (0, i, 0)),
                  col(P_U // 512), col(P_V // 512), col(P_ZA // 512),
                  full((1, BR_WIDTH)), full((1, BR_WIDTH)), full((8, 128, 128)), full((8, 128, 128)),
                  full((128, BR_WIDTH))],
        out_specs=[pl.BlockSpec((ts, 1536), lambda i: (i, P_U // 1536)),
                   full((1, BR_WIDTH)), full((1, BR_WIDTH)), full((8, 128, 128)), full((128, BR_WIDTH))],
        out_shape=[jax.ShapeDtypeStruct(dproj.shape, BF16),
                   jax.ShapeDtypeStruct((1, BR_WIDTH), F32), jax.ShapeDtypeStruct((1, BR_WIDTH), F32),
                   jax.ShapeDtypeStruct((8, 128, 128), F32), jax.ShapeDtypeStruct((128, BR_WIDTH), F32)],
        input_output_aliases={0: 0},
        compiler_params=_params(1))(dproj, dy, proj, proj, proj, ln_g, ln_b, ws, ws_t, bias_full)


def _hspec(ts):
    return pl.BlockSpec((N_HEADS, ts, HEAD_PAD), lambda i: (0, i, 0))


def prep_fwd(proj, tab, qg, kvg, wq, wk, wv):
    S = proj.shape[0]
    ts = LIGHT_ROW_TILE

    def body(qc_ref, kc_ref, vc_ref, qd_ref, kvd_ref, kr_ref, tab_ref, qg_ref, kvg_ref,
             wq_ref, wk_ref, wv_ref, qb, kb, vb, qc, kc, vc, kbt, vbt, kct, vct, cq_o, ckv_o):
        c, a, b = tab_ref[0], tab_ref[1], tab_ref[2]
        cq, _ = _rms(qd_ref[...], qg_ref[...])
        ckv, _ = _rms(kvd_ref[...], kvg_ref[...])
        cqb, ckvb = cq.astype(BF16), ckv.astype(BF16)
        cq_o[...] = cqb
        ckv_o[...] = ckvb
        krr = _rope(kr_ref[...], c, a, b)
        lane = lax.broadcasted_iota(jnp.int32, (ts, 128), 1)
        ones_lane = jnp.where(lane == MLA_V, 1.0, 0.0)
        for h in range(N_HEADS):
            cols = slice(h * HEAD_PAD, (h + 1) * HEAD_PAD)
            qb[h] = (_rope(_nn(cqb, wq_ref[:, cols]), c, a, b) * (MLA_SCALE * LOG2E)).astype(BF16)
            kh = _nn(ckvb, wk_ref[:, cols]) + krr
            vh = _nn(ckvb, wv_ref[:, cols]) + ones_lane
            kb[h], kbt[h] = kh.astype(BF16), kh.T.astype(BF16)
            vb[h], vbt[h] = vh.astype(BF16), vh.T.astype(BF16)
        for p in range(4):
            piece = qc_ref[:, p * 128:(p + 1) * 128] * (CA_SCALE * LOG2E)
            qc[2 * p] = jnp.where(lane < 64, piece, 0.0).astype(BF16)
            qc[2 * p + 1] = jnp.where(lane < 64, pltpu.roll(piece, 64, 1), 0.0).astype(BF16)
            for src, dst, dst_t, pad in ((kc_ref, kc, kct, 0.0), (vc_ref, vc, vct, ones_lane)):
                piece = src[:, p * 128:(p + 1) * 128]
                for h, head in ((2 * p, jnp.where(lane < 64, piece, pad)),
                                (2 * p + 1, jnp.where(lane < 64, pltpu.roll(piece, 64, 1), pad))):
                    dst[h], dst_t[h] = head.astype(BF16), head.T.astype(BF16)

    col = lambda w, c: pl.BlockSpec((ts, w), lambda i: (i, c))
    full = lambda shape: pl.BlockSpec(shape, lambda i: (0,) * len(shape))
    hshape = jax.ShapeDtypeStruct((N_HEADS, S, HEAD_PAD), BF16)
    tshape = jax.ShapeDtypeStruct((N_HEADS, HEAD_PAD, S), BF16)
    tspec = pl.BlockSpec((N_HEADS, HEAD_PAD, ts), lambda i: (0, 0, i))
    return _call(
        body, name="prep_fwd", grid=(S // ts,),
        in_specs=[col(512, P_QC // 512), col(512, P_KC // 512), col(512, P_VC // 512),
                  col(256, P_QD // 256), col(128, P_KVD // 128), col(128, P_KR // 128),
                  pl.BlockSpec((3, ts, 128), lambda i: (0, i, 0)),
                  full((1, MLA_Q_RANK)), full((1, MLA_KV_RANK)),
                  full((MLA_Q_RANK, 1024)), full((MLA_KV_RANK, 1024)), full((MLA_KV_RANK, 1024))],
        out_specs=[_hspec(ts)] * 6 + [tspec] * 4 + [pl.BlockSpec((ts, MLA_Q_RANK), lambda i: (i, 0)),
                                                    pl.BlockSpec((ts, MLA_KV_RANK), lambda i: (i, 0))],
        out_shape=[hshape] * 6 + [tshape] * 4 + [jax.ShapeDtypeStruct((S, MLA_Q_RANK), BF16),
                                                 jax.ShapeDtypeStruct((S, MLA_KV_RANK), BF16)],
        compiler_params=_params(1))(proj, proj, proj, proj, proj, proj, tab, qg, kvg, wq, wk, wv)


def prep_bwd(dproj, dqb, dkb, dvb, dqc, dkc, dvc, proj, tab, qg, kvg, wq, wk, wv):
    S = proj.shape[0]
    ts = LIGHT_ROW_TILE

    def body(dp_in, dqb_r, dkb_r, dvb_r, dqc_r, dkc_r, dvc_r, qd_ref, kvd_ref, tab_ref, qg_ref, kvg_ref,
             wq_ref, wk_ref, wv_ref, dp_ref, dqf, dkf, dvf, gq_ref, gkv_ref):
        del dp_in
        c, a, b = tab_ref[0], -tab_ref[1], -tab_ref[2]
        qd, kvd = qd_ref[...], kvd_ref[...]
        _, rq = _rms(qd, qg_ref[...])
        _, rkv = _rms(kvd, kvg_ref[...])
        dcq = jnp.zeros((ts, MLA_Q_RANK), F32)
        dckv = jnp.zeros((ts, MLA_KV_RANK), F32)
        dksum = jnp.zeros((ts, HEAD_PAD), F32)
        for h in range(N_HEADS):
            cols = slice(h * HEAD_PAD, (h + 1) * HEAD_PAD)
            dqh = _rope(dqb_r[h].astype(F32) * MLA_SCALE, c, a, b).astype(BF16)
            dqf[:, cols] = dqh
            dcq = dcq + _nt(dqh, wq_ref[:, cols])
            dk = dkb_r[h].astype(F32) * (1.0 / LOG2E)
            dksum = dksum + dk
            dkh = dk.astype(BF16)
            dkf[:, cols] = dkh
            dvh = dvb_r[h].astype(BF16)
            dvf[:, cols] = dvh
            dckv = dckv + _nt(dkh, wk_ref[:, cols]) + _nt(dvh, wv_ref[:, cols])
        lane = lax.broadcasted_iota(jnp.int32, (ts, 128), 1)
        rope_lanes = jnp.logical_and(lane >= MLA_NOPE, lane < MLA_QK)
        dp_ref[:, P_KR:P_KR + 128] = jnp.where(rope_lanes, _rope(dksum, c, a, b), 0.0).astype(BF16)
        dqd, gq = _rms_bwd(qd, qg_ref[...], rq, dcq)
        dkvd, gkv = _rms_bwd(kvd, kvg_ref[...], rkv, dckv)
        dp_ref[:, P_QD:P_QD + 256] = dqd.astype(BF16)
        dp_ref[:, P_KVD:P_KVD + 128] = dkvd.astype(BF16)
        first = pl.program_id(0) == 0
        _accumulate(gq_ref, gq, first)
        _accumulate(gkv_ref, gkv, first)
        for src, base, factor in ((dqc_r, P_QC, CA_SCALE), (dkc_r, P_KC, 1.0 / LOG2E), (dvc_r, P_VC, 1.0)):
            for p in range(4):
                dp_ref[:, base + p * 128:base + (p + 1) * 128] = (
                    (src[2 * p].astype(F32) + pltpu.roll(src[2 * p + 1].astype(F32), 64, 1)) * factor).astype(BF16)

    col = lambda w, c: pl.BlockSpec((ts, w), lambda i: (i, c))
    full = lambda shape: pl.BlockSpec(shape, lambda i: (0,) * len(shape))
    wide = jax.ShapeDtypeStruct((S, 1024), BF16)
    return _call(
        body, name="prep_bwd", grid=(S // ts,),
        in_specs=[pl.BlockSpec(memory_space=pl.ANY)] + [_hspec(ts)] * 6 +
                 [col(256, P_QD // 256), col(128, P_KVD // 128),
                  pl.BlockSpec((3, ts, 128), lambda i: (0, i, 0)),
                  full((1, MLA_Q_RANK)), full((1, MLA_KV_RANK)),
                  full((MLA_Q_RANK, 1024)), full((MLA_KV_RANK, 1024)), full((MLA_KV_RANK, 1024))],
        out_specs=[pl.BlockSpec((ts, 2048), lambda i: (i, 0))] + [pl.BlockSpec((ts, 1024), lambda i: (i, 0))] * 3 +
                  [full((1, MLA_Q_RANK)), full((1, MLA_KV_RANK))],
        out_shape=[jax.ShapeDtypeStruct(dproj.shape, BF16), wide, wide, wide,
                   jax.ShapeDtypeStruct((1, MLA_Q_RANK), F32), jax.ShapeDtypeStruct((1, MLA_KV_RANK), F32)],
        input_output_aliases={0: 0},
        compiler_params=_params(1))(dproj, dqb, dkb, dvb, dqc, dkc, dvc, proj, proj, tab, qg, kvg, wq, wk, wv)


def _diag_visible(t):
    r = lax.broadcasted_iota(jnp.int32, (t, t), 0) >> CHUNK_SHIFT
    c = lax.broadcasted_iota(jnp.int32, (t, t), 1) >> CHUNK_SHIFT
    return r <= c


def _pair_tables(nq, kv_major):
    if kv_major:
        pairs = [(kb, qi) for kb in range(nq) for qi in range(kb, nq)]
    else:
        pairs = [(kb, qi) for qi in range(nq) for kb in range(qi + 1)]
    return (jnp.asarray(np.array([p[0] for p in pairs], np.int32)),
            jnp.asarray(np.array([p[1] for p in pairs], np.int32)), len(pairs))


def _finish_softmax(acc, m):
    l = acc[MLA_V:MLA_V + 1, :]
    row = lax.broadcasted_iota(jnp.int32, acc.shape, 0)
    return jnp.where(row < MLA_V, acc / l, 0.0).T.astype(BF16), m + jnp.log2(l)


def _split_refs(refs, counts):
    out, pos = [], 0
    for c in counts:
        out.append(refs[pos:pos + c])
        pos += c
    return out


def mla_fwd(q, k, vt, exchange=()):
    H, S, _ = q.shape
    t, hb, n_ex = ATT_T, MLA_FWD_HEADS_PER_STEP, len(exchange)
    kb_tab, qi_tab, n_pairs = _pair_tables(S // t, False)

    def body(kb_ref, qi_ref, q_ref, k_ref, vt_ref, *rest):
        ex_src, (o_ref, lse_ref), ex_out, (m_s, acc_s), ex_sems = _split_refs(rest, (n_ex, 2, n_ex, 2, 3 if n_ex else 0))
        hg, p_id = pl.program_id(0), pl.program_id(1)
        kb, qi = kb_ref[p_id], qi_ref[p_id]

        if n_ex:
            @pl.when(jnp.logical_and(hg == 0, p_id == 0))
            def _():
                _exchange_issue(ex_src, ex_out, ex_sems, False, True)

        @pl.when(kb == 0)
        def _():
            m_s[...] = jnp.full_like(m_s, NEG_INF)
            acc_s[...] = jnp.zeros_like(acc_s)

        def step(masked):
            for h in range(hb):
                st = _nt(k_ref[h], q_ref[h])
                if masked:
                    st = jnp.where(_diag_visible(t), st, NEG_INF)
                m_prev = m_s[h]
                m_new = jnp.maximum(m_prev, jnp.max(st, axis=0, keepdims=True))
                p = jnp.exp2(st - m_new)
                acc_s[h] = jnp.exp2(m_prev - m_new) * acc_s[h] + _nn(vt_ref[h], p.astype(BF16))
                m_s[h] = m_new

        @pl.when(kb < qi)
        def _():
            step(False)

        @pl.when(kb == qi)
        def _():
            step(True)
            for h in range(hb):
                o_ref[h], lse_ref[h] = _finish_softmax(acc_s[h], m_s[h])

        if n_ex:
            @pl.when(jnp.logical_and(hg == H // hb - 1, p_id == n_pairs - 1))
            def _():
                _exchange_issue(ex_src, ex_out, ex_sems, False, False)

    grid_spec = pltpu.PrefetchScalarGridSpec(
        num_scalar_prefetch=2, grid=(H // hb, n_pairs),
        in_specs=[pl.BlockSpec((hb, t, HEAD_PAD), lambda h, p, kb, qi: (h, qi[p], 0)),
                  pl.BlockSpec((hb, t, HEAD_PAD), lambda h, p, kb, qi: (h, kb[p], 0)),
                  pl.BlockSpec((hb, HEAD_PAD, t), lambda h, p, kb, qi: (h, 0, kb[p]))] + [ANY] * n_ex,
        out_specs=[pl.BlockSpec((hb, t, HEAD_PAD), lambda h, p, kb, qi: (h, qi[p], 0)),
                   pl.BlockSpec((hb, 1, t), lambda h, p, kb, qi: (h, 0, qi[p]))] + [ANY] * n_ex,
        scratch_shapes=[pltpu.VMEM((hb, 1, t), F32), pltpu.VMEM((hb, HEAD_PAD, t), F32)] + _exchange_sems(n_ex))
    outs = _call(
        body, name="mla_fwd_gather" if n_ex else "mla_fwd", grid_spec=grid_spec,
        out_shape=[jax.ShapeDtypeStruct((H, S, HEAD_PAD), BF16), jax.ShapeDtypeStruct((H, 1, S), F32)] +
        _exchange_out_shape(exchange, False),
        compiler_params=_params(2))(kb_tab, qi_tab, q, k, vt, *exchange)
    return outs[0], outs[1], outs[2:]


def mla_bwd(q, k, kt, v, do, lse, delta, exchange=()):
    H, S, _ = q.shape
    t, hb, n_ex = ATT_T, MLA_BWD_HEADS_PER_STEP, len(exchange)
    nq = S // t
    kb_tab, qi_tab, n_pairs = _pair_tables(nq, True)

    def body(kb_ref, qi_ref, q_ref, k_ref, kt_ref, v_ref, do_ref, lse_ref, dl_ref, *rest):
        ex_src, (dq_ref, dk_ref, dv_ref), ex_out, (dqt_s, dk_s, dv_s), ex_sems = _split_refs(
            rest, (n_ex, 3, n_ex, 3, 3 if n_ex else 0))
        hg, p_id = pl.program_id(0), pl.program_id(1)
        kb, qi = kb_ref[p_id], qi_ref[p_id]

        if n_ex:
            @pl.when(jnp.logical_and(hg == 0, p_id == 0))
            def _():
                _exchange_issue(ex_src, ex_out, ex_sems, True, True)

        @pl.when(p_id == 0)
        def _():
            dqt_s[...] = jnp.zeros_like(dqt_s)

        @pl.when(qi == kb)
        def _():
            dk_s[...] = jnp.zeros_like(dk_s)
            dv_s[...] = jnp.zeros_like(dv_s)

        def step(masked):
            for h in range(hb):
                st = _nt(k_ref[h], q_ref[h])
                if masked:
                    st = jnp.where(_diag_visible(t), st, NEG_INF)
                pt = jnp.exp2(st - lse_ref[h])
                dv_s[h] += _nn(pt.astype(BF16), do_ref[h])
                dsb = (pt * (_nt(v_ref[h], do_ref[h]) - dl_ref[h])).astype(BF16)
                dk_s[h] += _nn(dsb, q_ref[h])
                dqt_s[h, qi] += _nn(kt_ref[h], dsb)

        @pl.when(qi == kb)
        def _():
            step(True)
            rows = pl.ds(pl.multiple_of(qi * t, t), t)
            for h in range(hb):
                dq_ref[h, rows, :] = dqt_s[h, qi].T.astype(BF16)

        @pl.when(qi > kb)
        def _():
            step(False)

        @pl.when(qi == nq - 1)
        def _():
            dk_ref[...] = dk_s[...].astype(BF16)
            dv_ref[...] = dv_s[...].astype(BF16)

        if n_ex:
            @pl.when(jnp.logical_and(hg == H // hb - 1, p_id == n_pairs - 1))
            def _():
                _exchange_issue(ex_src, ex_out, ex_sems, True, False)

    qtile = pl.BlockSpec((hb, t, HEAD_PAD), lambda h, p, kb, qi: (h, qi[p], 0))
    ktile = pl.BlockSpec((hb, t, HEAD_PAD), lambda h, p, kb, qi: (h, kb[p], 0))
    stat = pl.BlockSpec((hb, 1, t), lambda h, p, kb, qi: (h, 0, qi[p]))
    grid_spec = pltpu.PrefetchScalarGridSpec(
        num_scalar_prefetch=2, grid=(H // hb, n_pairs),
        in_specs=[qtile, ktile, pl.BlockSpec((hb, HEAD_PAD, t), lambda h, p, kb, qi: (h, 0, kb[p])), ktile, qtile,
                  stat, stat] + [ANY] * n_ex,
        out_specs=[pl.BlockSpec((hb, S, HEAD_PAD), lambda h, p, kb, qi: (h, 0, 0)), ktile, ktile] + [ANY] * n_ex,
        scratch_shapes=[pltpu.VMEM((hb, nq, HEAD_PAD, t), F32), pltpu.VMEM((hb, t, HEAD_PAD), F32),
                        pltpu.VMEM((hb, t, HEAD_PAD), F32)] + _exchange_sems(n_ex))
    outs = _call(
        body, name="mla_bwd_scatter" if n_ex else "mla_bwd", grid_spec=grid_spec,
        out_shape=[jax.ShapeDtypeStruct((H, S, HEAD_PAD), BF16)] * 3 + _exchange_out_shape(exchange, True),
        compiler_params=_params(2))(kb_tab, qi_tab, q, k, kt, v, do, lse, delta, *exchange)
    return outs[0], outs[1], outs[2], outs[3:]


def _band_specs(t, hb):
    prev = lambda i: jnp.maximum(i - 1, 0)
    return dict(
        cur=pl.BlockSpec((hb, t, HEAD_PAD), lambda h, i: (h, i, 0)),
        prev=pl.BlockSpec((hb, t, HEAD_PAD), lambda h, i: (h, prev(i), 0)),
        cur_t=pl.BlockSpec((hb, HEAD_PAD, t), lambda h, i: (h, 0, i)),
        prev_t=pl.BlockSpec((hb, HEAD_PAD, t), lambda h, i: (h, 0, prev(i))),
        stat=pl.BlockSpec((hb, 1, t), lambda h, i: (h, 0, i)),
        bias_prev=pl.BlockSpec((hb, 1, t, t), lambda h, i: (h, jnp.where(i == 0, 1, 0), 0, 0)),
        bias_cur=pl.BlockSpec((hb, 1, t, t), lambda h, i: (h, 2, 0, 0)))


def band_fwd(q, k, vt, bias):
    H, S, _ = q.shape
    t, hb = ATT_T, BAND_FWD_HEADS_PER_STEP
    sp = _band_specs(t, hb)

    def body(q_ref, kp_ref, kc_ref, vtp_ref, vtc_ref, bp_ref, bc_ref, o_ref, lse_ref):
        for h in range(hb):
            s0 = _nt(kp_ref[h], q_ref[h]) + bp_ref[h, 0]
            s1 = _nt(kc_ref[h], q_ref[h]) + bc_ref[h, 0]
            m = jnp.maximum(jnp.max(s0, axis=0, keepdims=True), jnp.max(s1, axis=0, keepdims=True))
            ot = (_nn(vtp_ref[h], jnp.exp2(s0 - m).astype(BF16)) +
                  _nn(vtc_ref[h], jnp.exp2(s1 - m).astype(BF16)))
            o_ref[h], lse_ref[h] = _finish_softmax(ot, m)

    return _call(
        body, name="band_fwd", grid=(H // hb, S // t),
        in_specs=[sp['cur'], sp['prev'], sp['cur'], sp['prev_t'], sp['cur_t'], sp['bias_prev'], sp['bias_cur']],
        out_specs=[sp['cur'], sp['stat']],
        out_shape=[jax.ShapeDtypeStruct((H, S, HEAD_PAD), BF16), jax.ShapeDtypeStruct((H, 1, S), F32)],
        compiler_params=_params(2))(q, k, k, vt, vt, bias, bias)


def band_bwd(q, k, kt, v, do, lse, delta, bias):
    H, S, _ = q.shape
    t = ATT_T
    sp = _band_specs(t, 1)

    def body(q_ref, kp_ref, kc_ref, ktp_ref, ktc_ref, vp_ref, vc_ref, do_ref, lse_ref, dl_ref, bp_ref, bc_ref,
             dq_ref, dk_ref, dv_ref, db_ref):
        i = pl.program_id(1)

        @pl.when(i == 0)
        def _():
            dk_ref[...] = jnp.zeros_like(dk_ref)
            dv_ref[...] = jnp.zeros_like(dv_ref)
            db_ref[...] = jnp.zeros_like(db_ref)

        qv, dov = q_ref[0], do_ref[0]
        dqt = jnp.zeros((HEAD_PAD, t), F32)
        windows = ((0, jnp.maximum(i - 1, 0), kp_ref, ktp_ref, vp_ref, bp_ref),
                   (1, i, kc_ref, ktc_ref, vc_ref, bc_ref))
        for w, blk, k_ref, kt_ref, v_ref, b_ref in windows:
            rows = pl.ds(pl.multiple_of(blk * t, t), t)
            pt = jnp.exp2(_nt(k_ref[0], qv) + b_ref[0, 0] - lse_ref[0])
            dv_ref[0, rows, :] += _nn(pt.astype(BF16), dov)
            ds = pt * (_nt(v_ref[0], dov) - dl_ref[0])
            db_ref[0, w] += ds
            dsb = ds.astype(BF16)
            dk_ref[0, rows, :] += _nn(dsb, qv)
            dqt = dqt + _nn(kt_ref[0], dsb)
        dq_ref[0] = dqt.T.astype(BF16)

    whole = pl.BlockSpec((1, S, HEAD_PAD), lambda h, i: (h, 0, 0))
    return _call(
        body, name="band_bwd", grid=(H, S // t),
        in_specs=[sp['cur'], sp['prev'], sp['cur'], sp['prev_t'], sp['cur_t'], sp['prev'], sp['cur'], sp['cur'],
                  sp['stat'], sp['stat'], sp['bias_prev'], sp['bias_cur']],
        out_specs=[sp['cur'], whole, whole, pl.BlockSpec((1, 2, t, t), lambda h, i: (h, 0, 0, 0))],
        out_shape=[jax.ShapeDtypeStruct((H, S, HEAD_PAD), BF16), jax.ShapeDtypeStruct((H, S, HEAD_PAD), F32),
                   jax.ShapeDtypeStruct((H, S, HEAD_PAD), F32), jax.ShapeDtypeStruct((H, 2, t, t), F32)],
        compiler_params=_params(2))(q, k, k, kt, kt, v, v, do, lse, delta, bias, bias)


def _compact(o_ref):
    return jnp.concatenate([o_ref[2 * p].astype(F32) + pltpu.roll(o_ref[2 * p + 1].astype(F32), 64, 1)
                            for p in range(4)], axis=1)


def merge_fwd(ob, oc, proj, ya, gate_b, wbr, w_out, x, post_g):
    S = x.shape[0]
    ts = ROW_TILE

    def body(ob_ref, oc_ref, zb_ref, zc_ref, ya_ref, gl_ref, gb_ref, wbr_ref, wo_ref, x_ref, pg_ref,
             xo_ref, yb_ref, yc_ref, mg_ref, out_ref):
        zb, zc = zb_ref[...], zc_ref[...]
        yb = (_compact(ob_ref) * (zb * _sigmoid(zb))).astype(BF16)
        yc = (_compact(oc_ref) * (zc * _sigmoid(zc))).astype(BF16)
        yb_ref[...] = yb
        yc_ref[...] = yc
        merged = jnp.zeros((ts, D_MODEL), F32)
        for n, y in enumerate((ya_ref[...], yb, yc)):
            cols = slice(n * D_MODEL, (n + 1) * D_MODEL)
            gate = _sigmoid(gl_ref[:, cols] + gb_ref[:, cols])
            merged = merged + gate * _nn(y, wbr_ref[n])
        mb = merged.astype(BF16)
        mg_ref[...] = mb
        out = _nn(mb, wo_ref[...])
        out_ref[...] = out
        normed, _ = _rms(out, pg_ref[...])
        xo_ref[...] = x_ref[...] + normed

    row = lambda w: pl.BlockSpec((ts, w), lambda i: (i, 0))
    col = lambda w, c: pl.BlockSpec((ts, w), lambda i: (i, c))
    full = lambda shape: pl.BlockSpec(shape, lambda i: (0,) * len(shape))
    return _call(
        body, name="merge_fwd", grid=(S // ts,),
        in_specs=[_hspec(ts), _hspec(ts), col(512, P_ZB // 512), col(512, P_ZC // 512), row(512),
                  col(3072, P_G // 3072), full((1, 3072)), full((3, BR_WIDTH, D_MODEL)),
                  full((D_MODEL, D_MODEL)), row(D_MODEL), full((1, D_MODEL))],
        out_specs=[row(D_MODEL), row(512), row(512), row(D_MODEL), row(D_MODEL)],
        out_shape=[jax.ShapeDtypeStruct((S, D_MODEL), F32), jax.ShapeDtypeStruct((S, 512), BF16),
                   jax.ShapeDtypeStruct((S, 512), BF16), jax.ShapeDtypeStruct((S, D_MODEL), BF16),
                   jax.ShapeDtypeStruct((S, D_MODEL), F32)],
        compiler_params=_params(1))(ob, oc, proj, proj, ya, proj, gate_b, wbr, w_out, x, post_g)


def gate_bwd(g, out, post_g, w_out, proj, gate_b, ya, yb, yc, wbr):
    S = g.shape[0]
    ts = ROW_TILE

    def body(g_ref, out_ref, pg_ref, wo_ref, gl_ref, gb_ref, ya_ref, yb_ref, yc_ref, wbr_ref,
             dp_ref, do_ref, dba_ref, dbb_ref, dbc_ref, dy_ref, ggb_ref, gp_ref):
        first = pl.program_id(0) == 0
        ov = out_ref[...]
        _, r = _rms(ov, pg_ref[...])
        dout, gp = _rms_bwd(ov, pg_ref[...], r, g_ref[...])
        db = dout.astype(BF16)
        do_ref[...] = db
        _accumulate(gp_ref, gp, first)
        dm = _nt(db, wo_ref[...])
        ggb = []
        for n, (y_ref, dbr_ref) in enumerate(((ya_ref, dba_ref), (yb_ref, dbb_ref), (yc_ref, dbc_ref))):
            cols = slice(n * D_MODEL, (n + 1) * D_MODEL)
            br = _nn(y_ref[...], wbr_ref[n])
            sg = _sigmoid(gl_ref[:, cols] + gb_ref[:, cols])
            dgl = dm * br * (sg * (1.0 - sg))
            dp_ref[:, cols] = dgl.astype(BF16)
            ggb.append(jnp.sum(dgl, axis=0, keepdims=True))
            dbr = (dm * sg).astype(BF16)
            dbr_ref[...] = dbr
            dy_ref[n] = _nt(dbr, wbr_ref[n])
        _accumulate(ggb_ref, jnp.concatenate(ggb, axis=1), first)

    row = lambda w: pl.BlockSpec((ts, w), lambda i: (i, 0))
    full = lambda shape: pl.BlockSpec(shape, lambda i: (0,) * len(shape))
    wide = jax.ShapeDtypeStruct((S, D_MODEL), BF16)
    return _call(
        body, name="gate_bwd", grid=(S // ts,),
        in_specs=[row(D_MODEL), row(D_MODEL), full((1, D_MODEL)), full((D_MODEL, D_MODEL)),
                  pl.BlockSpec((ts, 3072), lambda i: (i, P_G // 3072)), full((1, 3072)),
                  row(512), row(512), row(512), full((3, BR_WIDTH, D_MODEL))],
        out_specs=[pl.BlockSpec((ts, 3072), lambda i: (i, P_G // 3072)), row(D_MODEL), row(D_MODEL), row(D_MODEL),
                   row(D_MODEL), pl.BlockSpec((3, ts, 512), lambda i: (0, i, 0)), full((1, 3072)),
                   full((1, D_MODEL))],
        out_shape=[jax.ShapeDtypeStruct((S, P_W), BF16), wide, wide, wide, wide,
                   jax.ShapeDtypeStruct((3, S, 512), F32), jax.ShapeDtypeStruct((1, 3072), F32),
                   jax.ShapeDtypeStruct((1, D_MODEL), F32)],
        compiler_params=_params(1))(g, out, post_g, w_out, proj, gate_b, ya, yb, yc, wbr)


def ungate_bwd(dproj, dy, ob, oc, proj):
    S = proj.shape[0]
    ts = LIGHT_ROW_TILE

    def body(dp_in, dyb_ref, dyc_ref, ob_ref, oc_ref, zb_ref, zc_ref, dp_ref, dob_ref, doc_ref, dlb_ref, dlc_ref):
        del dp_in
        lane = lax.broadcasted_iota(jnp.int32, (ts, 128), 1)
        for n, (dy_ref, o_ref, z_ref, do_ref, dl_ref) in enumerate(
                ((dyb_ref, ob_ref, zb_ref, dob_ref, dlb_ref), (dyc_ref, oc_ref, zc_ref, doc_ref, dlc_ref))):
            zz = z_ref[...]
            dyv = dy_ref[0]
            sg = _sigmoid(zz)
            dp_ref[:, n * 512:(n + 1) * 512] = (dyv * _compact(o_ref) * (sg * (1.0 + zz * (1.0 - sg)))).astype(BF16)
            do_c = dyv * (zz * sg)
            for p in range(4):
                piece = do_c[:, p * 128:(p + 1) * 128]
                for h, d in ((2 * p, jnp.where(lane < 64, piece, 0.0)),
                             (2 * p + 1, jnp.where(lane < 64, pltpu.roll(piece, 64, 1), 0.0))):
                    do_ref[h] = d.astype(BF16)
                    dl_ref[h] = jnp.sum((d * o_ref[h].astype(F32)).T, axis=0, keepdims=True)

    col = lambda c: pl.BlockSpec((ts, 512), lambda i: (i, c))
    dysp = lambda n: pl.BlockSpec((1, ts, 512), lambda i: (n, i, 0))
    stat = pl.BlockSpec((N_HEADS, 1, ts), lambda i: (0, 0, i))
    hshape = jax.ShapeDtypeStruct((N_HEADS, S, HEAD_PAD), BF16)
    sshape = jax.ShapeDtypeStruct((N_HEADS, 1, S), F32)
    return _call(
        body, name="ungate_bwd", grid=(S // ts,),
        in_specs=[pl.BlockSpec(memory_space=pl.ANY), dysp(1), dysp(2), _hspec(ts), _hspec(ts),
                  col(P_ZB // 512), col(P_ZC // 512)],
        out_specs=[pl.BlockSpec((ts, 1024), lambda i: (i, P_ZB // 1024)), _hspec(ts), _hspec(ts), stat, stat],
        out_shape=[jax.ShapeDtypeStruct(dproj.shape, BF16), hshape, hshape, sshape, sshape],
        input_output_aliases={0: 0},
        compiler_params=_params(1))(dproj, dy, dy, ob, oc, proj, proj)


def loss_head(y, target):
    S, D = y.shape
    ts = LIGHT_ROW_TILE

    def body(y_ref, t_ref, dy_ref, sq_ref):
        d = y_ref[...] - t_ref[...]
        dy_ref[...] = d * (1.0 / D)
        _accumulate(sq_ref, jnp.sum(d * d, axis=0, keepdims=True), pl.program_id(0) == 0)

    row = pl.BlockSpec((ts, D), lambda i: (i, 0))
    return _call(
        body, name="loss_head", grid=(S // ts,), in_specs=[row, row],
        out_specs=[row, pl.BlockSpec((1, D), lambda i: (0, 0))],
        out_shape=[jax.ShapeDtypeStruct((S, D), F32), jax.ShapeDtypeStruct((1, D), F32)],
        compiler_params=_params(1))(y, target)


def _row_tile(rows, cols):
    for cand in (1024, 512, 256, 128, 64, 32, 16, 8):
        if rows % cand == 0 and cand * cols * 4 <= 1024 * 1024:
            return cand
    return rows


def adamw(w, grads, m, v):
    shape = w.shape
    cols = shape[-1]
    rows = int(np.prod(shape[:-1]))
    tr = _row_tile(rows, cols)
    n_g = len(grads)
    c1 = 1.0 - ADAM_B1 ** ADAM_STEP
    c2 = 1.0 - ADAM_B2 ** ADAM_STEP

    def body(*refs):
        w_ref, m_ref, v_ref = refs[:3]
        g_refs = refs[3:3 + n_g]
        go_ref, d_ref, mo_ref, vo_ref = refs[3 + n_g:]
        gv = g_refs[0][...]
        for g_ref in g_refs[1:]:
            gv = gv + g_ref[...]
        go_ref[...] = gv
        mn = ADAM_B1 * m_ref[...] + (1.0 - ADAM_B1) * gv
        vn = ADAM_B2 * v_ref[...] + (1.0 - ADAM_B2) * (gv * gv)
        mo_ref[...] = mn
        vo_ref[...] = vn
        d_ref[...] = -ADAM_LR * ((mn / c1) / (jnp.sqrt(vn / c2) + ADAM_EPS) + ADAM_WD * w_ref[...])

    blk = pl.BlockSpec((tr, cols), lambda i: (i, 0))
    sds = jax.ShapeDtypeStruct((rows, cols), F32)
    outs = _call(
        body, name="adamw", grid=(rows // tr,), in_specs=[blk] * (3 + n_g), out_specs=[blk] * 4,
        out_shape=[sds] * 4, compiler_params=_params(1))(
            *[a.reshape(rows, cols) for a in (w, m, v, *grads)])
    return [o.reshape(shape) for o in outs]


def add_lead(parts):
    n = parts.shape[0]
    shape = parts.shape[1:]
    cols = shape[-1]
    rows = int(np.prod(shape[:-1]))
    tr = _row_tile(rows, cols * n)

    def body(p_ref, o_ref):
        acc = p_ref[0].astype(F32)
        for s in range(1, n):
            acc = acc + p_ref[s].astype(F32)
        o_ref[...] = acc

    out = _call(
        body, name="add_lead", grid=(rows // tr,),
        in_specs=[pl.BlockSpec((n, tr, cols), lambda i: (0, i, 0))],
        out_specs=pl.BlockSpec((tr, cols), lambda i: (i, 0)),
        out_shape=jax.ShapeDtypeStruct((rows, cols), F32),
        compiler_params=_params(1))(parts.reshape(n, rows, cols))
    return out.reshape(shape)


ANY = pl.BlockSpec(memory_space=pl.ANY)


def _other_chips(x, y):
    return [(1 - x, y), (x, 1 - y), (1 - x, 1 - y)]


def chip_exchange(arrays, scatter, name):
    n = len(arrays)

    def body(*refs):
        _exchange_issue(refs[:n], refs[n:2 * n], refs[2 * n:], scatter, True)
        _exchange_issue(refs[:n], refs[n:2 * n], refs[2 * n:], scatter, False)

    return _call(
        body, name=name, in_specs=[ANY] * n, out_specs=[ANY] * n,
        out_shape=_exchange_out_shape(arrays, scatter), scratch_shapes=_exchange_sems(n))(*arrays)


def _exchange_out_shape(arrays, scatter):
    return [jax.ShapeDtypeStruct(a.shape if scatter else (4,) + a.shape, a.dtype) for a in arrays]


def _exchange_sems(n):
    if n == 0:
        return []
    return [pltpu.SemaphoreType.DMA((3 * n,)), pltpu.SemaphoreType.DMA((3 * n,)), pltpu.SemaphoreType.DMA((n,))]


def _exchange_issue(srcs, outs, sems, scatter, start):
    send_sems, recv_sems, local_sems = sems
    x, y, c = lax.axis_index("x"), lax.axis_index("y"), lax.axis_index("c")
    me = 2 * x + y
    for a in range(len(srcs)):
        local_src = srcs[a].at[me] if scatter else srcs[a]
        mine = pltpu.make_async_copy(local_src, outs[a].at[me], local_sems.at[a])
        sends = []
        for j, (px, py) in enumerate(_other_chips(x, y)):
            pair = dict(send_sem=send_sems.at[3 * a + j], recv_sem=recv_sems.at[3 * a + j],
                        device_id=(px, py, c), device_id_type=MESH)
            sends.append(pltpu.make_async_remote_copy(
                src_ref=srcs[a].at[2 * px + py] if scatter else srcs[a], dst_ref=outs[a].at[me], **pair))
            if not start:
                pltpu.make_async_remote_copy(src_ref=local_src, dst_ref=outs[a].at[2 * px + py], **pair).wait_recv()
        if start:
            mine.start()
            for cp in sends:
                cp.start()
        else:
            for cp in sends:
                cp.wait_send()
            mine.wait()


def sibling_exchange(arrays):
    n = len(arrays)

    def body(*refs):
        srcs, outs = refs[:n], refs[n:2 * n]
        send_sems, recv_sems = refs[2 * n:]
        x, y, c = lax.axis_index("x"), lax.axis_index("y"), lax.axis_index("c")
        copies = [pltpu.make_async_remote_copy(src_ref=srcs[a], dst_ref=outs[a], send_sem=send_sems.at[a],
                                               recv_sem=recv_sems.at[a], device_id=(x, y, 1 - c), device_id_type=MESH)
                  for a in range(n)]
        for cp in copies:
            cp.start()
        for cp in copies:
            cp.wait()

    return _call(
        body, name="sibling_exchange", in_specs=[ANY] * n, out_specs=[ANY] * n,
        out_shape=[jax.ShapeDtypeStruct(a.shape, a.dtype) for a in arrays],
        scratch_shapes=[pltpu.SemaphoreType.DMA((n,)), pltpu.SemaphoreType.DMA((n,))])(*arrays)


def _perm_from_shards(sh):
    rows = sh.shape[1]
    pieces, pos = [], 0
    for lo, hi, plo in sorted(NAT_SEGS, key=lambda s: s[2]):
        if plo > pos:
            pieces.append(jnp.zeros((rows, plo - pos), sh.dtype))
            pos = plo
        c = lo
        while c < hi:
            kk = c // SHARD_COLS
            e = min(hi, (kk + 1) * SHARD_COLS)
            pieces.append(sh[kk][:, c - kk * SHARD_COLS:e - kk * SHARD_COLS])
            c = e
        pos += hi - lo
    if pos < P_W:
        pieces.append(jnp.zeros((rows, P_W - pos), sh.dtype))
    return jnp.concatenate(pieces, axis=1)


def _shards_from_perm(p):
    out = []
    for kk in range(4):
        lo_k, hi_k = kk * SHARD_COLS, (kk + 1) * SHARD_COLS
        pieces = []
        for lo, hi, plo in NAT_SEGS:
            a, b = max(lo, lo_k), min(hi, hi_k)
            if a < b:
                pieces.append(p[:, plo + (a - lo):plo + (b - lo)])
        out.append(jnp.concatenate(pieces, axis=1))
    return jnp.stack(out)


def _split4(a, axis):
    shape = a.shape
    a = a.reshape(shape[:axis] + (4, shape[axis] // 4) + shape[axis + 1:])
    return jnp.moveaxis(a, axis, 0)


def _join4(a, axis):
    a = jnp.moveaxis(a, 0, axis)
    shape = a.shape
    return a.reshape(shape[:axis] + (4 * shape[axis + 1],) + shape[axis + 2:])


def _pad_heads(w, per_head, lo, hi):
    r = w.shape[0]
    wh = w.reshape(r, N_HEADS, per_head)[:, :, lo:hi]
    return jnp.pad(wh, ((0, 0), (0, 0), (0, HEAD_PAD - (hi - lo)))).reshape(r, N_HEADS * HEAD_PAD)


def _rope_table(S):
    half = MLA_ROPE // 2
    inv = ROPE_BASE ** (-jnp.arange(half, dtype=F32) / half)
    ang = jnp.arange(S).astype(F32)[:, None] * inv[None, :]
    cos, sin = jnp.cos(ang), jnp.sin(ang)
    z = lambda n: jnp.zeros((S, n), F32)
    c = jnp.concatenate([jnp.ones((S, MLA_NOPE), F32), cos, cos, z(32)], axis=1)
    a = jnp.concatenate([z(MLA_NOPE), -sin, z(48)], axis=1)
    b = jnp.concatenate([z(MLA_NOPE + half), sin, z(32)], axis=1)
    return jnp.stack([c, a, b])


def _band_onehot():
    t = ATT_T
    m = np.arange(2 * t)
    d = np.where(m < t, m, m - 2 * t)
    idx = np.stack([np.clip(off + d, -REL_CLIP, REL_CLIP) + REL_CLIP for off in (t, 0)])
    return (idx[:, :, None] == np.arange(2 * REL_CLIP + 1)[None, None, :]).astype(np.float32)


def bias_expand(diag):
    t = ATT_T

    def body(d_ref, o_ref):
        kc = lax.broadcasted_iota(jnp.int32, (t, t), 0) >> CHUNK_SHIFT
        qc = lax.broadcasted_iota(jnp.int32, (t, t), 1) >> CHUNK_SHIFT
        for w, visible in ((0, kc >= qc), (1, kc <= qc)):
            rows = jnp.broadcast_to(d_ref[0, w:w + 1, :], (t, 2 * t))
            skew = pltpu.roll(rows, 0, 1, stride=1, stride_axis=0)[:, :t]
            o_ref[0, 2 * w] = jnp.where(visible, skew * LOG2E, NEG_INF)
        o_ref[0, 1] = jnp.full((t, t), NEG_INF, F32)

    return _call(
        body, name="bias_expand", grid=(N_HEADS,),
        in_specs=[pl.BlockSpec((1, 2, 2 * t), lambda h: (h, 0, 0))],
        out_specs=pl.BlockSpec((1, 3, t, t), lambda h: (h, 0, 0, 0)),
        out_shape=jax.ShapeDtypeStruct((N_HEADS, 3, t, t), F32),
        compiler_params=_params(1))(diag)


def bias_fold(dtiles):
    t = ATT_T

    def body(d_ref, o_ref):
        pad = jnp.zeros((8, t), F32)
        for w in range(2):
            acc = jnp.concatenate([d_ref[0, w, 0:8, :], pad], axis=1)
            for g in range(1, t // 8):
                grp = jnp.concatenate([d_ref[0, w, 8 * g:8 * g + 8, :], pad], axis=1)
                acc = acc + pltpu.roll(grp, 2 * t - 8 * g, 1)
            out = acc[0:1, :]
            for s in range(1, 8):
                out = out + pltpu.roll(acc, 2 * t - s, 1)[s:s + 1, :]
            o_ref[0, w:w + 1, :] = out

    return _call(
        body, name="bias_fold", grid=(N_HEADS,),
        in_specs=[pl.BlockSpec((1, 2, t, t), lambda h: (h, 0, 0, 0))],
        out_specs=pl.BlockSpec((1, 2, 2 * t), lambda h: (h, 0, 0)),
        out_shape=jax.ShapeDtypeStruct((N_HEADS, 2, 2 * t), F32),
        compiler_params=_params(1))(dtiles)


def _bias_tiles(table):
    diag = jnp.einsum('hr,wdr->hwd', table, jnp.asarray(_band_onehot()), precision=lax.Precision.HIGHEST)
    return bias_expand(diag)


def _bias_tiles_grad(dtiles):
    return jnp.einsum('hwd,wdr->hr', bias_fold(dtiles), jnp.asarray(_band_onehot()),
                      precision=lax.Precision.HIGHEST)


def _layer_consts(lw):
    tri = np.tril(np.ones((SGU_BLOCK, SGU_BLOCK), np.float32))
    ws = (lw['sgu_w'] * tri).astype(BF16)
    return dict(
        ws=ws, ws_t=jnp.swapaxes(ws, 1, 2), sgu_bias=jnp.repeat(lw['sgu_b'].T, CA_HEAD_DIM, axis=1),
        bias=_bias_tiles(lw['ca_rel_bias']),
        wq=_pad_heads(lw['mla_w_uq'], MLA_QK, 0, MLA_QK),
        wk=_pad_heads(lw['mla_w_ukv'], MLA_NOPE + MLA_V, 0, MLA_NOPE),
        wv=_pad_heads(lw['mla_w_ukv'], MLA_NOPE + MLA_V, MLA_NOPE, MLA_NOPE + MLA_V),
        gate_b=lw['gate_b'].reshape(1, 3 * D_MODEL),
        pre_g=lw['pre_g'][None], post_g=lw['post_g'][None], ln_g=lw['sgu_ln_g'][None], ln_b=lw['sgu_ln_b'][None],
        qg=lw['mla_q_norm_g'][None], kvg=lw['mla_kv_norm_g'][None])


def _layer_fwd(x, lw, k, tab, next_shards):
    proj, xnt = norm_matmul(x, k['pre_g'], lw['w_in'])
    ya = sgu_fwd(proj, k['ln_g'], k['ln_b'], k['ws'], k['sgu_bias'])
    qb, kb, vb, qc, kc, vc, kbt, vbt, kct, vct, cq, ckv = prep_fwd(proj, tab, k['qg'], k['kvg'], k['wq'], k['wk'],
                                                                   k['wv'])
    ob, lse_b, gathered = mla_fwd(qb, kb, vbt, next_shards)
    oc, lse_c = band_fwd(qc, kc, vct, k['bias'])
    x_new, yb, yc, merged, out = merge_fwd(ob, oc, proj, ya, k['gate_b'], lw['w_branch'], lw['w_out'], x,
                                           k['post_g'])
    saved = dict(x=x, proj=proj, xnt=xnt, ya=ya, yb=yb, yc=yc, qb=qb, kb=kb, vb=vb, qc=qc, kc=kc, vc=vc, kbt=kbt, kct=kct,
                 cq=cq, ckv=ckv, ob=ob, oc=oc, lse_b=lse_b, lse_c=lse_c, merged=merged, out=out)
    return x_new, saved, gathered


def _layer_bwd(g, s, lw, k, tab, pending_parts, scatter_own):
    S = g.shape[0]
    H = N_HEADS
    dproj, dout, dba, dbb, dbc, dy, g_gate_b, g_post = gate_bwd(
        g, s['out'], k['post_g'], lw['w_out'], s['proj'], k['gate_b'], s['ya'], s['yb'], s['yc'], lw['w_branch'])
    g_w_out = matmul_tn(s['merged'], dout, 512)
    g_w_branch = jnp.stack([matmul_tn(y, d, 512) for y, d in ((s['ya'], dba), (s['yb'], dbb), (s['yc'], dbc))])
    dproj, dob, doc, dl_b, dl_c = ungate_bwd(dproj, dy, s['ob'], s['oc'], s['proj'])
    dqb, dkb, dvb, landed = mla_bwd(s['qb'], s['kb'], s['kbt'], s['vb'], dob, s['lse_b'], dl_b, pending_parts)
    dqc, dkc, dvc, dbias = band_bwd(s['qc'], s['kc'], s['kct'], s['vc'], doc, s['lse_c'], dl_c, k['bias'])
    dproj, dqf, dkf, dvf, g_qg, g_kvg = prep_bwd(dproj, dqb, dkb, dvb, dqc, dkc, dvc, s['proj'], tab,
                                                 k['qg'], k['kvg'], k['wq'], k['wk'], k['wv'])
    g_wq = matmul_tn(s['cq'], dqf, 512).reshape(MLA_Q_RANK, H, HEAD_PAD)[:, :, :MLA_QK]
    g_wk = matmul_tn(s['ckv'], dkf, 512).reshape(MLA_KV_RANK, H, HEAD_PAD)[:, :, :MLA_NOPE]
    g_wv = matmul_tn(s['ckv'], dvf, 512).reshape(MLA_KV_RANK, H, HEAD_PAD)[:, :, :MLA_V]
    dproj, g_ln_g, g_ln_b, g_ws, g_sgu_bias = sgu_bwd(dproj, dy, s['proj'], k['ln_g'], k['ln_b'], k['ws'],
                                                      k['ws_t'], k['sgu_bias'])
    g_w_in = matmul_acc(s['xnt'], dproj, MM_TN)
    sharded = _sharded_parts(dict(
        w_in=g_w_in, mla_w_uq=g_wq.reshape(MLA_Q_RANK, H * MLA_QK),
        mla_w_ukv=jnp.concatenate([g_wk, g_wv], axis=2).reshape(MLA_KV_RANK, H * (MLA_NOPE + MLA_V)),
        w_branch=g_w_branch, gate_b=g_gate_b.reshape(N_BRANCH, D_MODEL), w_out=g_w_out))
    dx, g_pre, own_landed = proj_bwd_x(dproj, lw['w_in'], s['x'], k['pre_g'], g, sharded if scatter_own else ())
    tri = np.tril(np.ones((SGU_BLOCK, SGU_BLOCK), np.float32))
    small = _small_pack(dict(
        pre_g=g_pre[0], post_g=g_post[0], sgu_ln_g=g_ln_g[0], sgu_ln_b=g_ln_b[0],
        sgu_w=g_ws * tri, sgu_b=jnp.sum(g_sgu_bias.reshape(SGU_BLOCK, 8, CA_HEAD_DIM), axis=2).T,
        mla_q_norm_g=g_qg[0], mla_kv_norm_g=g_kvg[0], ca_rel_bias=_bias_tiles_grad(dbias)))
    return dx, (own_landed if scatter_own else sharded), small, landed


BF16_PARTS = ('w_in', 'mla_w_uq', 'mla_w_ukv', 'w_branch', 'w_out')


def _weight_shards(w, l):
    return [w[n][l].astype(BF16) if n in BF16_PARTS else w[n][l] for n in SHARDED]


def _full_weights(gathered, small):
    lw = {n: _join4(a, SHARD_AXIS[n]) for n, a in zip(SHARDED, gathered) if n != 'w_in'}
    lw['w_in'] = _perm_from_shards(gathered[0])
    lw.update(small)
    return lw


def _small_pack(grads):
    flat = jnp.concatenate([grads[n].reshape(-1) for n in SMALL])
    quarter = -(-flat.size // (4 * 1024)) * 1024
    return jnp.pad(flat, (0, 4 * quarter - flat.size)).reshape(4, quarter // 128, 128)


def _sharded_parts(grads):
    parts = [_shards_from_perm(grads['w_in'])]
    parts += [_split4(grads[n], SHARD_AXIS[n]) for n in SHARDED if n != 'w_in']
    return [p.astype(BF16) if n in BF16_PARTS else p for n, p in zip(SHARDED, parts)]


def train_step_local(x, target, w):
    S = x.shape[0]
    depth = w['w_in'].shape[0]
    tab = _rope_table(S)
    gathered = chip_exchange(_weight_shards(w, 0), False, "gather_weights")
    layer_w, consts, saved = [], [], []
    for l in range(depth):
        lw = _full_weights(gathered, {n: w[n][l] for n in SMALL})
        k = _layer_consts(lw)
        x, s, gathered = _layer_fwd(x, lw, k, tab, _weight_shards(w, l + 1) if l + 1 < depth else ())
        layer_w.append(lw)
        consts.append(k)
        saved.append(s)
    g, sq = loss_head(x, target)
    mine = [None] * depth
    pending = ()
    for l in reversed(range(depth)):
        g, sharded, small, landed = _layer_bwd(g, saved[l], layer_w[l], consts[l], tab, pending, l == 0)
        if pending:
            mine[l + 1] = [add_lead(p) for p in landed]
        pending = list(sharded) + [small]
    mine[0] = [add_lead(p) for p in pending[:-1] + list(chip_exchange(pending[-1:], True, "scatter_small"))]
    n_parts = len(mine[0])
    theirs = sibling_exchange([p for layer in mine for p in layer])
    return sq, g, [(mine[l], theirs[l * n_parts:(l + 1) * n_parts]) for l in range(depth)]


def kernel(x, w_in, pre_g, post_g, sgu_ln_g, sgu_ln_b, sgu_w, sgu_b, mla_q_norm_g, mla_kv_norm_g, mla_w_uq, mla_w_ukv, ca_rel_bias, w_branch, gate_b, w_out, loss_target, m_w_in, m_pre_g, m_post_g, m_sgu_ln_g, m_sgu_ln_b, m_sgu_w, m_sgu_b, m_mla_q_norm_g, m_mla_kv_norm_g, m_mla_w_uq, m_mla_w_ukv, m_ca_rel_bias, m_w_branch, m_gate_b, m_w_out, v_w_in, v_pre_g, v_post_g, v_sgu_ln_g, v_sgu_ln_b, v_sgu_w, v_sgu_b, v_mla_q_norm_g, v_mla_kv_norm_g, v_mla_w_uq, v_mla_w_ukv, v_ca_rel_bias, v_w_branch, v_gate_b, v_w_out):
    w = dict(w_in=w_in, pre_g=pre_g, post_g=post_g, sgu_ln_g=sgu_ln_g, sgu_ln_b=sgu_ln_b, sgu_w=sgu_w, sgu_b=sgu_b,
             mla_q_norm_g=mla_q_norm_g, mla_kv_norm_g=mla_kv_norm_g, mla_w_uq=mla_w_uq, mla_w_ukv=mla_w_ukv,
             ca_rel_bias=ca_rel_bias, w_branch=w_branch, gate_b=gate_b, w_out=w_out)
    m = dict(w_in=m_w_in, pre_g=m_pre_g, post_g=m_post_g, sgu_ln_g=m_sgu_ln_g, sgu_ln_b=m_sgu_ln_b, sgu_w=m_sgu_w,
             sgu_b=m_sgu_b, mla_q_norm_g=m_mla_q_norm_g, mla_kv_norm_g=m_mla_kv_norm_g, mla_w_uq=m_mla_w_uq,
             mla_w_ukv=m_mla_w_ukv, ca_rel_bias=m_ca_rel_bias, w_branch=m_w_branch, gate_b=m_gate_b, w_out=m_w_out)
    v = dict(w_in=v_w_in, pre_g=v_pre_g, post_g=v_post_g, sgu_ln_g=v_sgu_ln_g, sgu_ln_b=v_sgu_ln_b, sgu_w=v_sgu_w,
             sgu_b=v_sgu_b, mla_q_norm_g=v_mla_q_norm_g, mla_kv_norm_g=v_mla_kv_norm_g, mla_w_uq=v_mla_w_uq,
             mla_w_ukv=v_mla_w_ukv, ca_rel_bias=v_ca_rel_bias, w_branch=v_w_branch, gate_b=v_gate_b, w_out=v_w_out)
    depth = w_in.shape[0]
    sq, grad_x, reduced = train_step_local(x[0], loss_target[0], w)
    loss = lax.psum(0.5 * jnp.sum(sq) / D_MODEL, ("x", "y", "c"))

    out = {}
    for a, n in enumerate(SHARDED):
        mine = jnp.stack([reduced[l][0][a] for l in range(depth)])
        theirs = jnp.stack([reduced[l][1][a] for l in range(depth)])
        out[n] = adamw(w[n], [mine, theirs], m[n], v[n])
    small = jnp.stack([jnp.stack([reduced[l][0][-1] for l in range(depth)]),
                       jnp.stack([reduced[l][1][-1] for l in range(depth)])])
    quarter = add_lead(small)
    full = chip_exchange([quarter], False, "gather_small")[0]
    full = jnp.moveaxis(full, 0, 1).reshape(depth, -1)
    off = 0
    for n in SMALL:
        size = int(np.prod(w[n].shape[1:]))
        out[n] = adamw(w[n], [full[:, off:off + size].reshape(w[n].shape)], m[n], v[n])
        off += size
    return (loss, grad_x[None], *[out[n][0] for n in WEIGHTS], *[out[n][1] for n in WEIGHTS],
            *[out[n][2] for n in WEIGHTS], *[out[n][3] for n in WEIGHTS])
```

```python
import numpy as np
import jax
import jax.numpy as jnp
from jax import lax
from jax.experimental import pallas as pl
from jax.experimental.pallas import tpu as pltpu

F32 = jnp.float32
BF16 = jnp.bfloat16
MESH = pl.DeviceIdType.MESH

EPS = 1e-6
NEG_INF = -1e30
D_MODEL = 1024
BR_WIDTH = 512
N_BRANCH = 3
N_HEADS = 8
HEAD_PAD = 128
CHUNK_SHIFT = 6
SGU_BLOCK = 128
MLA_NOPE, MLA_ROPE, MLA_V = 64, 32, 64
MLA_QK = MLA_NOPE + MLA_ROPE
MLA_Q_RANK, MLA_KV_RANK = 256, 128
CA_HEAD_DIM = 64
REL_CLIP = 128
ROPE_BASE = 10000.0
D_IN = 7584

ADAM_LR, ADAM_B1, ADAM_B2, ADAM_EPS, ADAM_WD, ADAM_STEP = 0.001, 0.9, 0.999, 1e-08, 0.01, 10

P_QC, P_KC, P_VC, P_QD, P_KVD, P_KR, P_ZB, P_ZC, P_G, P_U, P_V, P_ZA, P_W = (
    0, 512, 1024, 1536, 1792, 1920, 2048, 2560, 3072, 6144, 6656, 7168, 7680)
NAT_SEGS = [(0, 1536, P_U), (1536, 1920, P_QD), (1920, 1952, P_KR + MLA_NOPE), (1952, 2464, P_ZB),
            (2464, 4000, P_QC), (4000, 4512, P_ZC), (4512, 7584, P_G)]
SHARD_COLS = D_IN // 4

VMEM_LIMIT = 48 * 1024 * 1024
ATT_T = 512
BAND_FWD_HEADS_PER_STEP = 4
MLA_FWD_HEADS_PER_STEP = 8
MLA_BWD_HEADS_PER_STEP = 4
ROW_TILE = 256
LIGHT_ROW_TILE = 512
MM_TM = 512
MM_TN = 1536
LOG2E = 1.4426950408889634
MLA_SCALE = MLA_QK ** -0.5
CA_SCALE = CA_HEAD_DIM ** -0.5

WEIGHTS = ['w_in', 'pre_g', 'post_g', 'sgu_ln_g', 'sgu_ln_b', 'sgu_w', 'sgu_b', 'mla_q_norm_g',
           'mla_kv_norm_g', 'mla_w_uq', 'mla_w_ukv', 'ca_rel_bias', 'w_branch', 'gate_b', 'w_out']
SHARDED = ['w_in', 'mla_w_uq', 'mla_w_ukv', 'w_branch', 'gate_b', 'w_out']
SMALL = ['pre_g', 'post_g', 'sgu_ln_g', 'sgu_ln_b', 'sgu_w', 'sgu_b', 'mla_q_norm_g',
         'mla_kv_norm_g', 'ca_rel_bias']
SHARD_AXIS = {'w_in': 1, 'mla_w_uq': 1, 'mla_w_ukv': 1, 'w_branch': 2, 'gate_b': 1, 'w_out': 0}


def _call(body, **kw):
    return pl.pallas_call(body, **kw)


def _params(n_axes):
    return pltpu.CompilerParams(dimension_semantics=("arbitrary",) * n_axes,
                                vmem_limit_bytes=VMEM_LIMIT)


def _nt(a, b):
    return lax.dot_general(a, b, (((1,), (1,)), ((), ())), preferred_element_type=F32)


def _nn(a, b):
    return jnp.dot(a, b, preferred_element_type=F32)


def _tn(a, b):
    return lax.dot_general(a, b, (((0,), (0,)), ((), ())), preferred_element_type=F32)


def _rms(xv, g):
    r = lax.rsqrt(jnp.mean(xv * xv, axis=-1, keepdims=True) + EPS)
    return xv * r * g, r


def _rms_bwd(xv, g, r, dy):
    gy = dy * g
    dx = r * gy - xv * (r * r * r) * jnp.mean(xv * gy, axis=-1, keepdims=True)
    dg = jnp.sum(dy * (xv * r), axis=0, keepdims=True)
    return dx, dg


def _sigmoid(z):
    return 1.0 / (1.0 + jnp.exp(-z))


def _rope(xv, c, a, b):
    return xv * c + pltpu.roll(xv, 112, 1) * a + pltpu.roll(xv, 16, 1) * b


def _accumulate(ref, val, first):
    @pl.when(first)
    def _():
        ref[...] = val

    @pl.when(jnp.logical_not(first))
    def _():
        ref[...] += val


def norm_matmul(x, g, w):
    S, D = x.shape
    N = w.shape[1]
    tm, tn = min(S, 2 * MM_TM), MM_TN

    def body(x_ref, g_ref, w_ref, o_ref, xnt_ref, xn_s):
        @pl.when(pl.program_id(1) == 0)
        def _():
            y, _ = _rms(x_ref[...], g_ref[...])
            xn_s[...] = y.astype(BF16)
            xnt_ref[...] = y.T.astype(BF16)

        o_ref[...] = _nn(xn_s[...], w_ref[...])

    return _call(
        body, name="norm_matmul", grid=(S // tm, N // tn),
        in_specs=[pl.BlockSpec((tm, D), lambda i, j: (i, 0)),
                  pl.BlockSpec((1, D), lambda i, j: (0, 0)),
                  pl.BlockSpec((D, tn), lambda i, j: (0, j))],
        out_specs=[pl.BlockSpec((tm, tn), lambda i, j: (i, j)),
                   pl.BlockSpec((D, tm), lambda i, j: (0, i))],
        out_shape=[jax.ShapeDtypeStruct((S, N), F32), jax.ShapeDtypeStruct((D, S), BF16)],
        scratch_shapes=[pltpu.VMEM((tm, D), BF16)],
        compiler_params=_params(2))(x, g, w)


def matmul_acc(a, b, tn):
    M, S = a.shape
    N = b.shape[1]
    tk = min(S, 2 * MM_TM)

    def body(a_ref, b_ref, o_ref):
        @pl.when(pl.program_id(1) == 0)
        def _():
            o_ref[...] = jnp.zeros_like(o_ref)

        o_ref[...] += _nn(a_ref[...], b_ref[...])

    return _call(
        body, name="matmul_acc", grid=(N // tn, S // tk),
        in_specs=[pl.BlockSpec((M, tk), lambda j, k: (0, k)),
                  pl.BlockSpec((tk, tn), lambda j, k: (k, j))],
        out_specs=pl.BlockSpec((M, tn), lambda j, k: (0, j)),
        out_shape=jax.ShapeDtypeStruct((M, N), F32),
        compiler_params=_params(2))(a, b)


def proj_bwd_x(dproj, w, x, g, resid, exchange=()):
    S, N = dproj.shape
    D = x.shape[1]
    tm, tk = min(S, MM_TM), MM_TN
    nk, n_ex = N // tk, len(exchange)

    def body(dp_ref, w_ref, x_ref, g_ref, r_ref, *rest):
        ex_src, (dx_ref, dg_ref), ex_out, (acc_ref,), ex_sems = _split_refs(
            rest, (n_ex, 2, n_ex, 1, 3 if n_ex else 0))
        i, k = pl.program_id(0), pl.program_id(1)

        if n_ex:
            @pl.when(jnp.logical_and(i == 0, k == 0))
            def _():
                _exchange_issue(ex_src, ex_out, ex_sems, True, True)

        @pl.when(k == 0)
        def _():
            acc_ref[...] = jnp.zeros_like(acc_ref)

        acc_ref[...] += _nt(dp_ref[...].astype(BF16), w_ref[...])

        @pl.when(k == nk - 1)
        def _():
            xv = x_ref[...]
            _, r = _rms(xv, g_ref[...])
            dx, dg = _rms_bwd(xv, g_ref[...], r, acc_ref[...])
            dx_ref[...] = dx + r_ref[...]
            _accumulate(dg_ref, dg, i == 0)

        if n_ex:
            @pl.when(jnp.logical_and(i == S // tm - 1, k == nk - 1))
            def _():
                _exchange_issue(ex_src, ex_out, ex_sems, True, False)

    outs = _call(
        body, name="proj_bwd_x_scatter" if n_ex else "proj_bwd_x", grid=(S // tm, nk),
        in_specs=[pl.BlockSpec((tm, tk), lambda i, k: (i, k)),
                  pl.BlockSpec((D, tk), lambda i, k: (0, k)),
                  pl.BlockSpec((tm, D), lambda i, k: (i, 0)),
                  pl.BlockSpec((1, D), lambda i, k: (0, 0)),
                  pl.BlockSpec((tm, D), lambda i, k: (i, 0))] + [ANY] * n_ex,
        out_specs=[pl.BlockSpec((tm, D), lambda i, k: (i, 0)),
                   pl.BlockSpec((1, D), lambda i, k: (0, 0))] + [ANY] * n_ex,
        out_shape=[jax.ShapeDtypeStruct((S, D), F32), jax.ShapeDtypeStruct((1, D), F32)] +
        _exchange_out_shape(exchange, True),
        scratch_shapes=[pltpu.VMEM((tm, D), F32)] + _exchange_sems(n_ex),
        compiler_params=_params(2))(dproj, w, x, g, resid, *exchange)
    return outs[0], outs[1], outs[2:]


def matmul_tn(a, b, tn):
    S, M = a.shape
    N = b.shape[1]
    tk = min(S, 2 * MM_TM)

    def body(a_ref, b_ref, o_ref):
        @pl.when(pl.program_id(1) == 0)
        def _():
            o_ref[...] = jnp.zeros_like(o_ref)

        o_ref[...] += _tn(a_ref[...].astype(BF16), b_ref[...].astype(BF16))

    return _call(
        body, name="matmul_tn", grid=(N // tn, S // tk),
        in_specs=[pl.BlockSpec((tk, M), lambda j, k: (k, 0)),
                  pl.BlockSpec((tk, tn), lambda j, k: (k, j))],
        out_specs=pl.BlockSpec((M, tn), lambda j, k: (0, j)),
        out_shape=jax.ShapeDtypeStruct((M, N), F32),
        compiler_params=_params(2))(a, b)


def _sgu_block(vv, g, b, ws_ref, lane):
    mu = jnp.mean(vv, axis=-1, keepdims=True)
    xc = vv - mu
    r = lax.rsqrt(jnp.mean(xc * xc, axis=-1, keepdims=True) + EPS)
    xhat = xc * r
    vln = (xhat * g + b).astype(BF16)
    pieces = []
    for p in range(4):
        vp = vln[:, p * 128:(p + 1) * 128]
        pieces.append(jnp.where(lane < 64, _nn(ws_ref[2 * p], vp), _nn(ws_ref[2 * p + 1], vp)))
    return xhat, r, vln, jnp.concatenate(pieces, axis=1)


def sgu_fwd(proj, ln_g, ln_b, ws, bias_full):
    S = proj.shape[0]
    ts = LIGHT_ROW_TILE

    def body(u_ref, v_ref, z_ref, g_ref, b_ref, ws_ref, bf_ref, y_ref):
        lane = lax.broadcasted_iota(jnp.int32, (SGU_BLOCK, 128), 1)
        for blk in range(ts // SGU_BLOCK):
            rows = slice(blk * SGU_BLOCK, (blk + 1) * SGU_BLOCK)
            _, _, _, mixed = _sgu_block(v_ref[rows, :], g_ref[...], b_ref[...], ws_ref, lane)
            mixed = mixed + bf_ref[...]
            zz = z_ref[rows, :]
            y_ref[rows, :] = (u_ref[rows, :] * mixed * (zz * _sigmoid(zz))).astype(BF16)

    col = lambda c: pl.BlockSpec((ts, BR_WIDTH), lambda i: (i, c))
    full = lambda shape: pl.BlockSpec(shape, lambda i: (0,) * len(shape))
    return _call(
        body, name="sgu_fwd", grid=(S // ts,),
        in_specs=[col(P_U // 512), col(P_V // 512), col(P_ZA // 512),
                  full((1, BR_WIDTH)), full((1, BR_WIDTH)), full((8, 128, 128)), full((128, BR_WIDTH))],
        out_specs=pl.BlockSpec((ts, BR_WIDTH), lambda i: (i, 0)),
        out_shape=jax.ShapeDtypeStruct((S, BR_WIDTH), BF16),
        compiler_params=_params(1))(proj, proj, proj, ln_g, ln_b, ws, bias_full)


def sgu_bwd(dproj, dy, proj, ln_g, ln_b, ws, ws_t, bias_full):
    S = proj.shape[0]
    ts = LIGHT_ROW_TILE

    def body(dp_in, dy_ref, u_ref, v_ref, z_ref, g_ref, b_ref, ws_ref, wst_ref, bf_ref,
             dp_ref, gg_ref, gb_ref, gws_ref, gbf_ref):
        del dp_in
        first = pl.program_id(0) == 0

        @pl.when(first)
        def _():
            gg_ref[...] = jnp.zeros_like(gg_ref)
            gb_ref[...] = jnp.zeros_like(gb_ref)
            gws_ref[...] = jnp.zeros_like(gws_ref)
            gbf_ref[...] = jnp.zeros_like(gbf_ref)

        lane = lax.broadcasted_iota(jnp.int32, (SGU_BLOCK, 128), 1)
        for blk in range(ts // SGU_BLOCK):
            rows = slice(blk * SGU_BLOCK, (blk + 1) * SGU_BLOCK)
            g = g_ref[...]
            xhat, r, vln, mixed = _sgu_block(v_ref[rows, :], g, b_ref[...], ws_ref, lane)
            mixed = mixed + bf_ref[...]
            zz = z_ref[rows, :]
            uu = u_ref[rows, :]
            dyv = dy_ref[0, rows, :]
            sg = _sigmoid(zz)
            sil = zz * sg
            dmixed = dyv * uu * sil
            dp_ref[rows, 0:512] = (dyv * mixed * sil).astype(BF16)
            dp_ref[rows, 1024:1536] = (dyv * uu * mixed * (sg * (1.0 + zz * (1.0 - sg)))).astype(BF16)
            gbf_ref[...] += dmixed
            dmb = dmixed.astype(BF16)
            pieces = []
            for p in range(4):
                dmp = dmb[:, p * 128:(p + 1) * 128]
                vp = vln[:, p * 128:(p + 1) * 128]
                pieces.append(jnp.where(lane < 64, _nn(wst_ref[2 * p], dmp), _nn(wst_ref[2 * p + 1], dmp)))
                zero = jnp.zeros_like(dmp)
                gws_ref[2 * p] += _nt(jnp.where(lane < 64, dmp, zero), vp)
                gws_ref[2 * p + 1] += _nt(jnp.where(lane >= 64, dmp, zero), vp)
            dvln = jnp.concatenate(pieces, axis=1)
            dxh = dvln * g
            dp_ref[rows, 512:1024] = (r * (dxh - jnp.mean(dxh, axis=-1, keepdims=True)
                                           - xhat * jnp.mean(dxh * xhat, axis=-1, keepdims=True))).astype(BF16)
            gg_ref[...] += jnp.sum(dvln * xhat, axis=0, keepdims=True)
            gb_ref[...] += jnp.sum(dvln, axis=0, keepdims=True)

    col = lambda c: pl.BlockSpec((ts, BR_WIDTH), lambda i: (i, c))
    full = lambda shape: pl.BlockSpec(shape, lambda i: (0,) * len(shape))
    return _call(
        body, name="sgu_bwd", grid=(S // ts,),
        in_specs=[pl.BlockSpec(memory_space=pl.ANY),
                  pl.BlockSpec((1, ts, BR_WIDTH), lambda i: (0, i, 0)),
                  col(P_U // 512), col(P_V // 512), col(P_ZA // 512),
                  full((1, BR_WIDTH)), full((1, BR_WIDTH)), full((8, 128, 128)), full((8, 128, 128)),
                  full((128, BR_WIDTH))],
        out_specs=[pl.BlockSpec((ts, 1536), lambda i: (i, P_U // 1536)),
                   full((1, BR_WIDTH)), full((1, BR_WIDTH)), full((8, 128, 128)), full((128, BR_WIDTH))],
        out_shape=[jax.ShapeDtypeStruct(dproj.shape, BF16),
                   jax.ShapeDtypeStruct((1, BR_WIDTH), F32), jax.ShapeDtypeStruct((1, BR_WIDTH), F32),
                   jax.ShapeDtypeStruct((8, 128, 128), F32), jax.ShapeDtypeStruct((128, BR_WIDTH), F32)],
        input_output_aliases={0: 0},
        compiler_params=_params(1))(dproj, dy, proj, proj, proj, ln_g, ln_b, ws, ws_t, bias_full)


def _hspec(ts):
    return pl.BlockSpec((N_HEADS, ts, HEAD_PAD), lambda i: (0, i, 0))


def prep_fwd(proj, tab, qg, kvg, wq, wk, wv):
    S = proj.shape[0]
    ts = LIGHT_ROW_TILE

    def body(qc_ref, kc_ref, vc_ref, qd_ref, kvd_ref, kr_ref, tab_ref, qg_ref, kvg_ref,
             wq_ref, wk_ref, wv_ref, qb, kb, vb, qc, kc, vc, kbt, vbt, kct, vct, cq_o, ckv_o):
        c, a, b = tab_ref[0], tab_ref[1], tab_ref[2]
        cq, _ = _rms(qd_ref[...], qg_ref[...])
        ckv, _ = _rms(kvd_ref[...], kvg_ref[...])
        cqb, ckvb = cq.astype(BF16), ckv.astype(BF16)
        cq_o[...] = cqb
        ckv_o[...] = ckvb
        krr = _rope(kr_ref[...], c, a, b)
        lane = lax.broadcasted_iota(jnp.int32, (ts, 128), 1)
        ones_lane = jnp.where(lane == MLA_V, 1.0, 0.0)
        for h in range(N_HEADS):
            cols = slice(h * HEAD_PAD, (h + 1) * HEAD_PAD)
            qb[h] = (_rope(_nn(cqb, wq_ref[:, cols]), c, a, b) * (MLA_SCALE * LOG2E)).astype(BF16)
            kh = _nn(ckvb, wk_ref[:, cols]) + krr
            vh = _nn(ckvb, wv_ref[:, cols]) + ones_lane
            kb[h], kbt[h] = kh.astype(BF16), kh.T.astype(BF16)
            vb[h], vbt[h] = vh.astype(BF16), vh.T.astype(BF16)
        for p in range(4):
            piece = qc_ref[:, p * 128:(p + 1) * 128] * (CA_SCALE * LOG2E)
            qc[2 * p] = jnp.where(lane < 64, piece, 0.0).astype(BF16)
            qc[2 * p + 1] = jnp.where(lane < 64, pltpu.roll(piece, 64, 1), 0.0).astype(BF16)
            for src, dst, dst_t, pad in ((kc_ref, kc, kct, 0.0), (vc_ref, vc, vct, ones_lane)):
                piece = src[:, p * 128:(p + 1) * 128]
                for h, head in ((2 * p, jnp.where(lane < 64, piece, pad)),
                                (2 * p + 1, jnp.where(lane < 64, pltpu.roll(piece, 64, 1), pad))):
                    dst[h], dst_t[h] = head.astype(BF16), head.T.astype(BF16)

    col = lambda w, c: pl.BlockSpec((ts, w), lambda i: (i, c))
    full = lambda shape: pl.BlockSpec(shape, lambda i: (0,) * len(shape))
    hshape = jax.ShapeDtypeStruct((N_HEADS, S, HEAD_PAD), BF16)
    tshape = jax.ShapeDtypeStruct((N_HEADS, HEAD_PAD, S), BF16)
    tspec = pl.BlockSpec((N_HEADS, HEAD_PAD, ts), lambda i: (0, 0, i))
    return _call(
        body, name="prep_fwd", grid=(S // ts,),
        in_specs=[col(512, P_QC // 512), col(512, P_KC // 512), col(512, P_VC // 512),
                  col(256, P_QD // 256), col(128, P_KVD // 128), col(128, P_KR // 128),
                  pl.BlockSpec((3, ts, 128), lambda i: (0, i, 0)),
                  full((1, MLA_Q_RANK)), full((1, MLA_KV_RANK)),
                  full((MLA_Q_RANK, 1024)), full((MLA_KV_RANK, 1024)), full((MLA_KV_RANK, 1024))],
        out_specs=[_hspec(ts)] * 6 + [tspec] * 4 + [pl.BlockSpec((ts, MLA_Q_RANK), lambda i: (i, 0)),
                                                    pl.BlockSpec((ts, MLA_KV_RANK), lambda i: (i, 0))],
        out_shape=[hshape] * 6 + [tshape] * 4 + [jax.ShapeDtypeStruct((S, MLA_Q_RANK), BF16),
                                                 jax.ShapeDtypeStruct((S, MLA_KV_RANK), BF16)],
        compiler_params=_params(1))(proj, proj, proj, proj, proj, proj, tab, qg, kvg, wq, wk, wv)


def prep_bwd(dproj, dqb, dkb, dvb, dqc, dkc, dvc, proj, tab, qg, kvg, wq, wk, wv):
    S = proj.shape[0]
    ts = LIGHT_ROW_TILE

    def body(dp_in, dqb_r, dkb_r, dvb_r, dqc_r, dkc_r, dvc_r, qd_ref, kvd_ref, tab_ref, qg_ref, kvg_ref,
             wq_ref, wk_ref, wv_ref, dp_ref, dqf, dkf, dvf, gq_ref, gkv_ref):
        del dp_in
        c, a, b = tab_ref[0], -tab_ref[1], -tab_ref[2]
        qd, kvd = qd_ref[...], kvd_ref[...]
        _, rq = _rms(qd, qg_ref[...])
        _, rkv = _rms(kvd, kvg_ref[...])
        dcq = jnp.zeros((ts, MLA_Q_RANK), F32)
        dckv = jnp.zeros((ts, MLA_KV_RANK), F32)
        dksum = jnp.zeros((ts, HEAD_PAD), F32)
        for h in range(N_HEADS):
            cols = slice(h * HEAD_PAD, (h + 1) * HEAD_PAD)
            dqh = _rope(dqb_r[h].astype(F32) * MLA_SCALE, c, a, b).astype(BF16)
            dqf[:, cols] = dqh
            dcq = dcq + _nt(dqh, wq_ref[:, cols])
            dk = dkb_r[h].astype(F32) * (1.0 / LOG2E)
            dksum = dksum + dk
            dkh = dk.astype(BF16)
            dkf[:, cols] = dkh
            dvh = dvb_r[h].astype(BF16)
            dvf[:, cols] = dvh
            dckv = dckv + _nt(dkh, wk_ref[:, cols]) + _nt(dvh, wv_ref[:, cols])
        lane = lax.broadcasted_iota(jnp.int32, (ts, 128), 1)
        rope_lanes = jnp.logical_and(lane >= MLA_NOPE, lane < MLA_QK)
        dp_ref[:, P_KR:P_KR + 128] = jnp.where(rope_lanes, _rope(dksum, c, a, b), 0.0).astype(BF16)
        dqd, gq = _rms_bwd(qd, qg_ref[...], rq, dcq)
        dkvd, gkv = _rms_bwd(kvd, kvg_ref[...], rkv, dckv)
        dp_ref[:, P_QD:P_QD + 256] = dqd.astype(BF16)
        dp_ref[:, P_KVD:P_KVD + 128] = dkvd.astype(BF16)
        first = pl.program_id(0) == 0
        _accumulate(gq_ref, gq, first)
        _accumulate(gkv_ref, gkv, first)
        for src, base, factor in ((dqc_r, P_QC, CA_SCALE), (dkc_r, P_KC, 1.0 / LOG2E), (dvc_r, P_VC, 1.0)):
            for p in range(4):
                dp_ref[:, base + p * 128:base + (p + 1) * 128] = (
                    (src[2 * p].astype(F32) + pltpu.roll(src[2 * p + 1].astype(F32), 64, 1)) * factor).astype(BF16)

    col = lambda w, c: pl.BlockSpec((ts, w), lambda i: (i, c))
    full = lambda shape: pl.BlockSpec(shape, lambda i: (0,) * len(shape))
    wide = jax.ShapeDtypeStruct((S, 1024), BF16)
    return _call(
        body, name="prep_bwd", grid=(S // ts,),
        in_specs=[pl.BlockSpec(memory_space=pl.ANY)] + [_hspec(ts)] * 6 +
                 [col(256, P_QD // 256), col(128, P_KVD // 128),
                  pl.BlockSpec((3, ts, 128), lambda i: (0, i, 0)),
                  full((1, MLA_Q_RANK)), full((1, MLA_KV_RANK)),
                  full((MLA_Q_RANK, 1024)), full((MLA_KV_RANK, 1024)), full((MLA_KV_RANK, 1024))],
        out_specs=[pl.BlockSpec((ts, 2048), lambda i: (i, 0))] + [pl.BlockSpec((ts, 1024), lambda i: (i, 0))] * 3 +
                  [full((1, MLA_Q_RANK)), full((1, MLA_KV_RANK))],
        out_shape=[jax.ShapeDtypeStruct(dproj.shape, BF16), wide, wide, wide,
                   jax.ShapeDtypeStruct((1, MLA_Q_RANK), F32), jax.ShapeDtypeStruct((1, MLA_KV_RANK), F32)],
        input_output_aliases={0: 0},
        compiler_params=_params(1))(dproj, dqb, dkb, dvb, dqc, dkc, dvc, proj, proj, tab, qg, kvg, wq, wk, wv)


def _diag_visible(t):
    r = lax.broadcasted_iota(jnp.int32, (t, t), 0) >> CHUNK_SHIFT
    c = lax.broadcasted_iota(jnp.int32, (t, t), 1) >> CHUNK_SHIFT
    return r <= c


def _pair_tables(nq, kv_major):
    if kv_major:
        pairs = [(kb, qi) for kb in range(nq) for qi in range(kb, nq)]
    else:
        pairs = [(kb, qi) for qi in range(nq) for kb in range(qi + 1)]
    return (jnp.asarray(np.array([p[0] for p in pairs], np.int32)),
            jnp.asarray(np.array([p[1] for p in pairs], np.int32)), len(pairs))


def _finish_softmax(acc, m):
    l = acc[MLA_V:MLA_V + 1, :]
    row = lax.broadcasted_iota(jnp.int32, acc.shape, 0)
    return jnp.where(row < MLA_V, acc / l, 0.0).T.astype(BF16), m + jnp.log2(l)


def _split_refs(refs, counts):
    out, pos = [], 0
    for c in counts:
        out.append(refs[pos:pos + c])
        pos += c
    return out


def mla_fwd(q, k, vt, exchange=()):
    H, S, _ = q.shape
    t, hb, n_ex = ATT_T, MLA_FWD_HEADS_PER_STEP, len(exchange)
    kb_tab, qi_tab, n_pairs = _pair_tables(S // t, False)

    def body(kb_ref, qi_ref, q_ref, k_ref, vt_ref, *rest):
        ex_src, (o_ref, lse_ref), ex_out, (m_s, acc_s), ex_sems = _split_refs(rest, (n_ex, 2, n_ex, 2, 3 if n_ex else 0))
        hg, p_id = pl.program_id(0), pl.program_id(1)
        kb, qi = kb_ref[p_id], qi_ref[p_id]

        if n_ex:
            @pl.when(jnp.logical_and(hg == 0, p_id == 0))
            def _():
                _exchange_issue(ex_src, ex_out, ex_sems, False, True)

        @pl.when(kb == 0)
        def _():
            m_s[...] = jnp.full_like(m_s, NEG_INF)
            acc_s[...] = jnp.zeros_like(acc_s)

        def step(masked):
            for h in range(hb):
                st = _nt(k_ref[h], q_ref[h])
                if masked:
                    st = jnp.where(_diag_visible(t), st, NEG_INF)
                m_prev = m_s[h]
                m_new = jnp.maximum(m_prev, jnp.max(st, axis=0, keepdims=True))
                p = jnp.exp2(st - m_new)
                acc_s[h] = jnp.exp2(m_prev - m_new) * acc_s[h] + _nn(vt_ref[h], p.astype(BF16))
                m_s[h] = m_new

        @pl.when(kb < qi)
        def _():
            step(False)

        @pl.when(kb == qi)
        def _():
            step(True)
            for h in range(hb):
                o_ref[h], lse_ref[h] = _finish_softmax(acc_s[h], m_s[h])

        if n_ex:
            @pl.when(jnp.logical_and(hg == H // hb - 1, p_id == n_pairs - 1))
            def _():
                _exchange_issue(ex_src, ex_out, ex_sems, False, False)

    grid_spec = pltpu.PrefetchScalarGridSpec(
        num_scalar_prefetch=2, grid=(H // hb, n_pairs),
        in_specs=[pl.BlockSpec((hb, t, HEAD_PAD), lambda h, p, kb, qi: (h, qi[p], 0)),
                  pl.BlockSpec((hb, t, HEAD_PAD), lambda h, p, kb, qi: (h, kb[p], 0)),
                  pl.BlockSpec((hb, HEAD_PAD, t), lambda h, p, kb, qi: (h, 0, kb[p]))] + [ANY] * n_ex,
        out_specs=[pl.BlockSpec((hb, t, HEAD_PAD), lambda h, p, kb, qi: (h, qi[p], 0)),
                   pl.BlockSpec((hb, 1, t), lambda h, p, kb, qi: (h, 0, qi[p]))] + [ANY] * n_ex,
        scratch_shapes=[pltpu.VMEM((hb, 1, t), F32), pltpu.VMEM((hb, HEAD_PAD, t), F32)] + _exchange_sems(n_ex))
    outs = _call(
        body, name="mla_fwd_gather" if n_ex else "mla_fwd", grid_spec=grid_spec,
        out_shape=[jax.ShapeDtypeStruct((H, S, HEAD_PAD), BF16), jax.ShapeDtypeStruct((H, 1, S), F32)] +
        _exchange_out_shape(exchange, False),
        compiler_params=_params(2))(kb_tab, qi_tab, q, k, vt, *exchange)
    return outs[0], outs[1], outs[2:]


def mla_bwd(q, k, kt, v, do, lse, delta, exchange=()):
    H, S, _ = q.shape
    t, hb, n_ex = ATT_T, MLA_BWD_HEADS_PER_STEP, len(exchange)
    nq = S // t
    kb_tab, qi_tab, n_pairs = _pair_tables(nq, True)

    def body(kb_ref, qi_ref, q_ref, k_ref, kt_ref, v_ref, do_ref, lse_ref, dl_ref, *rest):
        ex_src, (dq_ref, dk_ref, dv_ref), ex_out, (dqt_s, dk_s, dv_s), ex_sems = _split_refs(
            rest, (n_ex, 3, n_ex, 3, 3 if n_ex else 0))
        hg, p_id = pl.program_id(0), pl.program_id(1)
        kb, qi = kb_ref[p_id], qi_ref[p_id]

        if n_ex:
            @pl.when(jnp.logical_and(hg == 0, p_id == 0))
            def _():
                _exchange_issue(ex_src, ex_out, ex_sems, True, True)

        @pl.when(p_id == 0)
        def _():
            dqt_s[...] = jnp.zeros_like(dqt_s)

        @pl.when(qi == kb)
        def _():
            dk_s[...] = jnp.zeros_like(dk_s)
            dv_s[...] = jnp.zeros_like(dv_s)

        def step(masked):
            for h in range(hb):
                st = _nt(k_ref[h], q_ref[h])
                if masked:
                    st = jnp.where(_diag_visible(t), st, NEG_INF)
                pt = jnp.exp2(st - lse_ref[h])
                dv_s[h] += _nn(pt.astype(BF16), do_ref[h])
                dsb = (pt * (_nt(v_ref[h], do_ref[h]) - dl_ref[h])).astype(BF16)
                dk_s[h] += _nn(dsb, q_ref[h])
                dqt_s[h, qi] += _nn(kt_ref[h], dsb)

        @pl.when(qi == kb)
        def _():
            step(True)
            rows = pl.ds(pl.multiple_of(qi * t, t), t)
            for h in range(hb):
                dq_ref[h, rows, :] = dqt_s[h, qi].T.astype(BF16)

        @pl.when(qi > kb)
        def _():
            step(False)

        @pl.when(qi == nq - 1)
        def _():
            dk_ref[...] = dk_s[...].astype(BF16)
            dv_ref[...] = dv_s[...].astype(BF16)

        if n_ex:
            @pl.when(jnp.logical_and(hg == H // hb - 1, p_id == n_pairs - 1))
            def _():
                _exchange_issue(ex_src, ex_out, ex_sems, True, False)

    qtile = pl.BlockSpec((hb, t, HEAD_PAD), lambda h, p, kb, qi: (h, qi[p], 0))
    ktile = pl.BlockSpec((hb, t, HEAD_PAD), lambda h, p, kb, qi: (h, kb[p], 0))
    stat = pl.BlockSpec((hb, 1, t), lambda h, p, kb, qi: (h, 0, qi[p]))
    grid_spec = pltpu.PrefetchScalarGridSpec(
        num_scalar_prefetch=2, grid=(H // hb, n_pairs),
        in_specs=[qtile, ktile, pl.BlockSpec((hb, HEAD_PAD, t), lambda h, p, kb, qi: (h, 0, kb[p])), ktile, qtile,
                  stat, stat] + [ANY] * n_ex,
        out_specs=[pl.BlockSpec((hb, S, HEAD_PAD), lambda h, p, kb, qi: (h, 0, 0)), ktile, ktile] + [ANY] * n_ex,
        scratch_shapes=[pltpu.VMEM((hb, nq, HEAD_PAD, t), F32), pltpu.VMEM((hb, t, HEAD_PAD), F32),
                        pltpu.VMEM((hb, t, HEAD_PAD), F32)] + _exchange_sems(n_ex))
    outs = _call(
        body, name="mla_bwd_scatter" if n_ex else "mla_bwd", grid_spec=grid_spec,
        out_shape=[jax.ShapeDtypeStruct((H, S, HEAD_PAD), BF16)] * 3 + _exchange_out_shape(exchange, True),
        compiler_params=_params(2))(kb_tab, qi_tab, q, k, kt, v, do, lse, delta, *exchange)
    return outs[0], outs[1], outs[2], outs[3:]


def _band_specs(t, hb):
    prev = lambda i: jnp.maximum(i - 1, 0)
    return dict(
        cur=pl.BlockSpec((hb, t, HEAD_PAD), lambda h, i: (h, i, 0)),
        prev=pl.BlockSpec((hb, t, HEAD_PAD), lambda h, i: (h, prev(i), 0)),
        cur_t=pl.BlockSpec((hb, HEAD_PAD, t), lambda h, i: (h, 0, i)),
        prev_t=pl.BlockSpec((hb, HEAD_PAD, t), lambda h, i: (h, 0, prev(i))),
        stat=pl.BlockSpec((hb, 1, t), lambda h, i: (h, 0, i)),
        bias_prev=pl.BlockSpec((hb, 1, t, t), lambda h, i: (h, jnp.where(i == 0, 1, 0), 0, 0)),
        bias_cur=pl.BlockSpec((hb, 1, t, t), lambda h, i: (h, 2, 0, 0)))


def band_fwd(q, k, vt, bias):
    H, S, _ = q.shape
    t, hb = ATT_T, BAND_FWD_HEADS_PER_STEP
    sp = _band_specs(t, hb)

    def body(q_ref, kp_ref, kc_ref, vtp_ref, vtc_ref, bp_ref, bc_ref, o_ref, lse_ref):
        for h in range(hb):
            s0 = _nt(kp_ref[h], q_ref[h]) + bp_ref[h, 0]
            s1 = _nt(kc_ref[h], q_ref[h]) + bc_ref[h, 0]
            m = jnp.maximum(jnp.max(s0, axis=0, keepdims=True), jnp.max(s1, axis=0, keepdims=True))
            ot = (_nn(vtp_ref[h], jnp.exp2(s0 - m).astype(BF16)) +
                  _nn(vtc_ref[h], jnp.exp2(s1 - m).astype(BF16)))
            o_ref[h], lse_ref[h] = _finish_softmax(ot, m)

    return _call(
        body, name="band_fwd", grid=(H // hb, S // t),
        in_specs=[sp['cur'], sp['prev'], sp['cur'], sp['prev_t'], sp['cur_t'], sp['bias_prev'], sp['bias_cur']],
        out_specs=[sp['cur'], sp['stat']],
        out_shape=[jax.ShapeDtypeStruct((H, S, HEAD_PAD), BF16), jax.ShapeDtypeStruct((H, 1, S), F32)],
        compiler_params=_params(2))(q, k, k, vt, vt, bias, bias)


def band_bwd(q, k, kt, v, do, lse, delta, bias):
    H, S, _ = q.shape
    t = ATT_T
    sp = _band_specs(t, 1)

    def body(q_ref, kp_ref, kc_ref, ktp_ref, ktc_ref, vp_ref, vc_ref, do_ref, lse_ref, dl_ref, bp_ref, bc_ref,
             dq_ref, dk_ref, dv_ref, db_ref):
        i = pl.program_id(1)

        @pl.when(i == 0)
        def _():
            dk_ref[...] = jnp.zeros_like(dk_ref)
            dv_ref[...] = jnp.zeros_like(dv_ref)
            db_ref[...] = jnp.zeros_like(db_ref)

        qv, dov = q_ref[0], do_ref[0]
        dqt = jnp.zeros((HEAD_PAD, t), F32)
        windows = ((0, jnp.maximum(i - 1, 0), kp_ref, ktp_ref, vp_ref, bp_ref),
                   (1, i, kc_ref, ktc_ref, vc_ref, bc_ref))
        for w, blk, k_ref, kt_ref, v_ref, b_ref in windows:
            rows = pl.ds(pl.multiple_of(blk * t, t), t)
            pt = jnp.exp2(_nt(k_ref[0], qv) + b_ref[0, 0] - lse_ref[0])
            dv_ref[0, rows, :] += _nn(pt.astype(BF16), dov)
            ds = pt * (_nt(v_ref[0], dov) - dl_ref[0])
            db_ref[0, w] += ds
            dsb = ds.astype(BF16)
            dk_ref[0, rows, :] += _nn(dsb, qv)
            dqt = dqt + _nn(kt_ref[0], dsb)
        dq_ref[0] = dqt.T.astype(BF16)

    whole = pl.BlockSpec((1, S, HEAD_PAD), lambda h, i: (h, 0, 0))
    return _call(
        body, name="band_bwd", grid=(H, S // t),
        in_specs=[sp['cur'], sp['prev'], sp['cur'], sp['prev_t'], sp['cur_t'], sp['prev'], sp['cur'], sp['cur'],
                  sp['stat'], sp['stat'], sp['bias_prev'], sp['bias_cur']],
        out_specs=[sp['cur'], whole, whole, pl.BlockSpec((1, 2, t, t), lambda h, i: (h, 0, 0, 0))],
        out_shape=[jax.ShapeDtypeStruct((H, S, HEAD_PAD), BF16), jax.ShapeDtypeStruct((H, S, HEAD_PAD), F32),
                   jax.ShapeDtypeStruct((H, S, HEAD_PAD), F32), jax.ShapeDtypeStruct((H, 2, t, t), F32)],
        compiler_params=_params(2))(q, k, k, kt, kt, v, v, do, lse, delta, bias, bias)


def _compact(o_ref):
    return jnp.concatenate([o_ref[2 * p].astype(F32) + pltpu.roll(o_ref[2 * p + 1].astype(F32), 64, 1)
                            for p in range(4)], axis=1)


def merge_fwd(ob, oc, proj, ya, gate_b, wbr, w_out, x, post_g):
    S = x.shape[0]
    ts = ROW_TILE

    def body(ob_ref, oc_ref, zb_ref, zc_ref, ya_ref, gl_ref, gb_ref, wbr_ref, wo_ref, x_ref, pg_ref,
             xo_ref, yb_ref, yc_ref, mg_ref, out_ref):
        zb, zc = zb_ref[...], zc_ref[...]
        yb = (_compact(ob_ref) * (zb * _sigmoid(zb))).astype(BF16)
        yc = (_compact(oc_ref) * (zc * _sigmoid(zc))).astype(BF16)
        yb_ref[...] = yb
        yc_ref[...] = yc
        merged = jnp.zeros((ts, D_MODEL), F32)
        for n, y in enumerate((ya_ref[...], yb, yc)):
            cols = slice(n * D_MODEL, (n + 1) * D_MODEL)
            gate = _sigmoid(gl_ref[:, cols] + gb_ref[:, cols])
            merged = merged + gate * _nn(y, wbr_ref[n])
        mb = merged.astype(BF16)
        mg_ref[...] = mb
        out = _nn(mb, wo_ref[...])
        out_ref[...] = out
        normed, _ = _rms(out, pg_ref[...])
        xo_ref[...] = x_ref[...] + normed

    row = lambda w: pl.BlockSpec((ts, w), lambda i: (i, 0))
    col = lambda w, c: pl.BlockSpec((ts, w), lambda i: (i, c))
    full = lambda shape: pl.BlockSpec(shape, lambda i: (0,) * len(shape))
    return _call(
        body, name="merge_fwd", grid=(S // ts,),
        in_specs=[_hspec(ts), _hspec(ts), col(512, P_ZB // 512), col(512, P_ZC // 512), row(512),
                  col(3072, P_G // 3072), full((1, 3072)), full((3, BR_WIDTH, D_MODEL)),
                  full((D_MODEL, D_MODEL)), row(D_MODEL), full((1, D_MODEL))],
        out_specs=[row(D_MODEL), row(512), row(512), row(D_MODEL), row(D_MODEL)],
        out_shape=[jax.ShapeDtypeStruct((S, D_MODEL), F32), jax.ShapeDtypeStruct((S, 512), BF16),
                   jax.ShapeDtypeStruct((S, 512), BF16), jax.ShapeDtypeStruct((S, D_MODEL), BF16),
                   jax.ShapeDtypeStruct((S, D_MODEL), F32)],
        compiler_params=_params(1))(ob, oc, proj, proj, ya, proj, gate_b, wbr, w_out, x, post_g)


def gate_bwd(g, out, post_g, w_out, proj, gate_b, ya, yb, yc, wbr):
    S = g.shape[0]
    ts = ROW_TILE

    def body(g_ref, out_ref, pg_ref, wo_ref, gl_ref, gb_ref, ya_ref, yb_ref, yc_ref, wbr_ref,
             dp_ref, do_ref, dba_ref, dbb_ref, dbc_ref, dy_ref, ggb_ref, gp_ref):
        first = pl.program_id(0) == 0
        ov = out_ref[...]
        _, r = _rms(ov, pg_ref[...])
        dout, gp = _rms_bwd(ov, pg_ref[...], r, g_ref[...])
        db = dout.astype(BF16)
        do_ref[...] = db
        _accumulate(gp_ref, gp, first)
        dm = _nt(db, wo_ref[...])
        ggb = []
        for n, (y_ref, dbr_ref) in enumerate(((ya_ref, dba_ref), (yb_ref, dbb_ref), (yc_ref, dbc_ref))):
            cols = slice(n * D_MODEL, (n + 1) * D_MODEL)
            br = _nn(y_ref[...], wbr_ref[n])
            sg = _sigmoid(gl_ref[:, cols] + gb_ref[:, cols])
            dgl = dm * br * (sg * (1.0 - sg))
            dp_ref[:, cols] = dgl.astype(BF16)
            ggb.append(jnp.sum(dgl, axis=0, keepdims=True))
            dbr = (dm * sg).astype(BF16)
            dbr_ref[...] = dbr
            dy_ref[n] = _nt(dbr, wbr_ref[n])
        _accumulate(ggb_ref, jnp.concatenate(ggb, axis=1), first)

    row = lambda w: pl.BlockSpec((ts, w), lambda i: (i, 0))
    full = lambda shape: pl.BlockSpec(shape, lambda i: (0,) * len(shape))
    wide = jax.ShapeDtypeStruct((S, D_MODEL), BF16)
    return _call(
        body, name="gate_bwd", grid=(S // ts,),
        in_specs=[row(D_MODEL), row(D_MODEL), full((1, D_MODEL)), full((D_MODEL, D_MODEL)),
                  pl.BlockSpec((ts, 3072), lambda i: (i, P_G // 3072)), full((1, 3072)),
                  row(512), row(512), row(512), full((3, BR_WIDTH, D_MODEL))],
        out_specs=[pl.BlockSpec((ts, 3072), lambda i: (i, P_G // 3072)), row(D_MODEL), row(D_MODEL), row(D_MODEL),
                   row(D_MODEL), pl.BlockSpec((3, ts, 512), lambda i: (0, i, 0)), full((1, 3072)),
                   full((1, D_MODEL))],
        out_shape=[jax.ShapeDtypeStruct((S, P_W), BF16), wide, wide, wide, wide,
                   jax.ShapeDtypeStruct((3, S, 512), F32), jax.ShapeDtypeStruct((1, 3072), F32),
                   jax.ShapeDtypeStruct((1, D_MODEL), F32)],
        compiler_params=_params(1))(g, out, post_g, w_out, proj, gate_b, ya, yb, yc, wbr)


def ungate_bwd(dproj, dy, ob, oc, proj):
    S = proj.shape[0]
    ts = LIGHT_ROW_TILE

    def body(dp_in, dyb_ref, dyc_ref, ob_ref, oc_ref, zb_ref, zc_ref, dp_ref, dob_ref, doc_ref, dlb_ref, dlc_ref):
        del dp_in
        lane = lax.broadcasted_iota(jnp.int32, (ts, 128), 1)
        for n, (dy_ref, o_ref, z_ref, do_ref, dl_ref) in enumerate(
                ((dyb_ref, ob_ref, zb_ref, dob_ref, dlb_ref), (dyc_ref, oc_ref, zc_ref, doc_ref, dlc_ref))):
            zz = z_ref[...]
            dyv = dy_ref[0]
            sg = _sigmoid(zz)
            dp_ref[:, n * 512:(n + 1) * 512] = (dyv * _compact(o_ref) * (sg * (1.0 + zz * (1.0 - sg)))).astype(BF16)
            do_c = dyv * (zz * sg)
            for p in range(4):
                piece = do_c[:, p * 128:(p + 1) * 128]
                for h, d in ((2 * p, jnp.where(lane < 64, piece, 0.0)),
                             (2 * p + 1, jnp.where(lane < 64, pltpu.roll(piece, 64, 1), 0.0))):
                    do_ref[h] = d.astype(BF16)
                    dl_ref[h] = jnp.sum((d * o_ref[h].astype(F32)).T, axis=0, keepdims=True)

    col = lambda c: pl.BlockSpec((ts, 512), lambda i: (i, c))
    dysp = lambda n: pl.BlockSpec((1, ts, 512), lambda i: (n, i, 0))
    stat = pl.BlockSpec((N_HEADS, 1, ts), lambda i: (0, 0, i))
    hshape = jax.ShapeDtypeStruct((N_HEADS, S, HEAD_PAD), BF16)
    sshape = jax.ShapeDtypeStruct((N_HEADS, 1, S), F32)
    return _call(
        body, name="ungate_bwd", grid=(S // ts,),
        in_specs=[pl.BlockSpec(memory_space=pl.ANY), dysp(1), dysp(2), _hspec(ts), _hspec(ts),
                  col(P_ZB // 512), col(P_ZC // 512)],
        out_specs=[pl.BlockSpec((ts, 1024), lambda i: (i, P_ZB // 1024)), _hspec(ts), _hspec(ts), stat, stat],
        out_shape=[jax.ShapeDtypeStruct(dproj.shape, BF16), hshape, hshape, sshape, sshape],
        input_output_aliases={0: 0},
        compiler_params=_params(1))(dproj, dy, dy, ob, oc, proj, proj)


def loss_head(y, target):
    S, D = y.shape
    ts = LIGHT_ROW_TILE

    def body(y_ref, t_ref, dy_ref, sq_ref):
        d = y_ref[...] - t_ref[...]
        dy_ref[...] = d * (1.0 / D)
        _accumulate(sq_ref, jnp.sum(d * d, axis=0, keepdims=True), pl.program_id(0) == 0)

    row = pl.BlockSpec((ts, D), lambda i: (i, 0))
    return _call(
        body, name="loss_head", grid=(S // ts,), in_specs=[row, row],
        out_specs=[row, pl.BlockSpec((1, D), lambda i: (0, 0))],
        out_shape=[jax.ShapeDtypeStruct((S, D), F32), jax.ShapeDtypeStruct((1, D), F32)],
        compiler_params=_params(1))(y, target)


def _row_tile(rows, cols):
    for cand in (1024, 512, 256, 128, 64, 32, 16, 8):
        if rows % cand == 0 and cand * cols * 4 <= 1024 * 1024:
            return cand
    return rows


def adamw(w, grads, m, v):
    shape = w.shape
    cols = shape[-1]
    rows = int(np.prod(shape[:-1]))
    tr = _row_tile(rows, cols)
    n_g = len(grads)
    c1 = 1.0 - ADAM_B1 ** ADAM_STEP
    c2 = 1.0 - ADAM_B2 ** ADAM_STEP

    def body(*refs):
        w_ref, m_ref, v_ref = refs[:3]
        g_refs = refs[3:3 + n_g]
        go_ref, d_ref, mo_ref, vo_ref = refs[3 + n_g:]
        gv = g_refs[0][...]
        for g_ref in g_refs[1:]:
            gv = gv + g_ref[...]
        go_ref[...] = gv
        mn = ADAM_B1 * m_ref[...] + (1.0 - ADAM_B1) * gv
        vn = ADAM_B2 * v_ref[...] + (1.0 - ADAM_B2) * (gv * gv)
        mo_ref[...] = mn
        vo_ref[...] = vn
        d_ref[...] = -ADAM_LR * ((mn / c1) / (jnp.sqrt(vn / c2) + ADAM_EPS) + ADAM_WD * w_ref[...])

    blk = pl.BlockSpec((tr, cols), lambda i: (i, 0))
    sds = jax.ShapeDtypeStruct((rows, cols), F32)
    outs = _call(
        body, name="adamw", grid=(rows // tr,), in_specs=[blk] * (3 + n_g), out_specs=[blk] * 4,
        out_shape=[sds] * 4, compiler_params=_params(1))(
            *[a.reshape(rows, cols) for a in (w, m, v, *grads)])
    return [o.reshape(shape) for o in outs]


def add_lead(parts):
    n = parts.shape[0]
    shape = parts.shape[1:]
    cols = shape[-1]
    rows = int(np.prod(shape[:-1]))
    tr = _row_tile(rows, cols * n)

    def body(p_ref, o_ref):
        acc = p_ref[0].astype(F32)
        for s in range(1, n):
            acc = acc + p_ref[s].astype(F32)
        o_ref[...] = acc

    out = _call(
        body, name="add_lead", grid=(rows // tr,),
        in_specs=[pl.BlockSpec((n, tr, cols), lambda i: (0, i, 0))],
        out_specs=pl.BlockSpec((tr, cols), lambda i: (i, 0)),
        out_shape=jax.ShapeDtypeStruct((rows, cols), F32),
        compiler_params=_params(1))(parts.reshape(n, rows, cols))
    return out.reshape(shape)


ANY = pl.BlockSpec(memory_space=pl.ANY)


def _other_chips(x, y):
    return [(1 - x, y), (x, 1 - y), (1 - x, 1 - y)]


def chip_exchange(arrays, scatter, name):
    n = len(arrays)

    def body(*refs):
        _exchange_issue(refs[:n], refs[n:2 * n], refs[2 * n:], scatter, True)
        _exchange_issue(refs[:n], refs[n:2 * n], refs[2 * n:], scatter, False)

    return _call(
        body, name=name, in_specs=[ANY] * n, out_specs=[ANY] * n,
        out_shape=_exchange_out_shape(arrays, scatter), scratch_shapes=_exchange_sems(n))(*arrays)


def _exchange_out_shape(arrays, scatter):
    return [jax.ShapeDtypeStruct(a.shape if scatter else (4,) + a.shape, a.dtype) for a in arrays]


def _exchange_sems(n):
    if n == 0:
        return []
    return [pltpu.SemaphoreType.DMA((3 * n,)), pltpu.SemaphoreType.DMA((3 * n,)), pltpu.SemaphoreType.DMA((n,))]


def _exchange_issue(srcs, outs, sems, scatter, start):
    send_sems, recv_sems, local_sems = sems
    x, y, c = lax.axis_index("x"), lax.axis_index("y"), lax.axis_index("c")
    me = 2 * x + y
    for a in range(len(srcs)):
        local_src = srcs[a].at[me] if scatter else srcs[a]
        mine = pltpu.make_async_copy(local_src, outs[a].at[me], local_sems.at[a])
        sends = []
        for j, (px, py) in enumerate(_other_chips(x, y)):
            pair = dict(send_sem=send_sems.at[3 * a + j], recv_sem=recv_sems.at[3 * a + j],
                        device_id=(px, py, c), device_id_type=MESH)
            sends.append(pltpu.make_async_remote_copy(
                src_ref=srcs[a].at[2 * px + py] if scatter else srcs[a], dst_ref=outs[a].at[me], **pair))
            if not start:
                pltpu.make_async_remote_copy(src_ref=local_src, dst_ref=outs[a].at[2 * px + py], **pair).wait_recv()
        if start:
            mine.start()
            for cp in sends:
                cp.start()
        else:
            for cp in sends:
                cp.wait_send()
            mine.wait()


def sibling_exchange(arrays):
    n = len(arrays)

    def body(*refs):
        srcs, outs = refs[:n], refs[n:2 * n]
        send_sems, recv_sems = refs[2 * n:]
        x, y, c = lax.axis_index("x"), lax.axis_index("y"), lax.axis_index("c")
        copies = [pltpu.make_async_remote_copy(src_ref=srcs[a], dst_ref=outs[a], send_sem=send_sems.at[a],
                                               recv_sem=recv_sems.at[a], device_id=(x, y, 1 - c), device_id_type=MESH)
                  for a in range(n)]
        for cp in copies:
            cp.start()
        for cp in copies:
            cp.wait()

    return _call(
        body, name="sibling_exchange", in_specs=[ANY] * n, out_specs=[ANY] * n,
        out_shape=[jax.ShapeDtypeStruct(a.shape, a.dtype) for a in arrays],
        scratch_shapes=[pltpu.SemaphoreType.DMA((n,)), pltpu.SemaphoreType.DMA((n,))])(*arrays)


def _perm_from_shards(sh):
    rows = sh.shape[1]
    pieces, pos = [], 0
    for lo, hi, plo in sorted(NAT_SEGS, key=lambda s: s[2]):
        if plo > pos:
            pieces.append(jnp.zeros((rows, plo - pos), sh.dtype))
            pos = plo
        c = lo
        while c < hi:
            kk = c // SHARD_COLS
            e = min(hi, (kk + 1) * SHARD_COLS)
            pieces.append(sh[kk][:, c - kk * SHARD_COLS:e - kk * SHARD_COLS])
            c = e
        pos += hi - lo
    if pos < P_W:
        pieces.append(jnp.zeros((rows, P_W - pos), sh.dtype))
    return jnp.concatenate(pieces, axis=1)


def _shards_from_perm(p):
    out = []
    for kk in range(4):
        lo_k, hi_k = kk * SHARD_COLS, (kk + 1) * SHARD_COLS
        pieces = []
        for lo, hi, plo in NAT_SEGS:
            a, b = max(lo, lo_k), min(hi, hi_k)
            if a < b:
                pieces.append(p[:, plo + (a - lo):plo + (b - lo)])
        out.append(jnp.concatenate(pieces, axis=1))
    return jnp.stack(out)


def _split4(a, axis):
    shape = a.shape
    a = a.reshape(shape[:axis] + (4, shape[axis] // 4) + shape[axis + 1:])
    return jnp.moveaxis(a, axis, 0)


def _join4(a, axis):
    a = jnp.moveaxis(a, 0, axis)
    shape = a.shape
    return a.reshape(shape[:axis] + (4 * shape[axis + 1],) + shape[axis + 2:])


def _pad_heads(w, per_head, lo, hi):
    r = w.shape[0]
    wh = w.reshape(r, N_HEADS, per_head)[:, :, lo:hi]
    return jnp.pad(wh, ((0, 0), (0, 0), (0, HEAD_PAD - (hi - lo)))).reshape(r, N_HEADS * HEAD_PAD)


def _rope_table(S):
    half = MLA_ROPE // 2
    inv = ROPE_BASE ** (-jnp.arange(half, dtype=F32) / half)
    ang = jnp.arange(S).astype(F32)[:, None] * inv[None, :]
    cos, sin = jnp.cos(ang), jnp.sin(ang)
    z = lambda n: jnp.zeros((S, n), F32)
    c = jnp.concatenate([jnp.ones((S, MLA_NOPE), F32), cos, cos, z(32)], axis=1)
    a = jnp.concatenate([z(MLA_NOPE), -sin, z(48)], axis=1)
    b = jnp.concatenate([z(MLA_NOPE + half), sin, z(32)], axis=1)
    return jnp.stack([c, a, b])


def _band_onehot():
    t = ATT_T
    m = np.arange(2 * t)
    d = np.where(m < t, m, m - 2 * t)
    idx = np.stack([np.clip(off + d, -REL_CLIP, REL_CLIP) + REL_CLIP for off in (t, 0)])
    return (idx[:, :, None] == np.arange(2 * REL_CLIP + 1)[None, None, :]).astype(np.float32)


def bias_expand(diag):
    t = ATT_T

    def body(d_ref, o_ref):
        kc = lax.broadcasted_iota(jnp.int32, (t, t), 0) >> CHUNK_SHIFT
        qc = lax.broadcasted_iota(jnp.int32, (t, t), 1) >> CHUNK_SHIFT
        for w, visible in ((0, kc >= qc), (1, kc <= qc)):
            rows = jnp.broadcast_to(d_ref[0, w:w + 1, :], (t, 2 * t))
            skew = pltpu.roll(rows, 0, 1, stride=1, stride_axis=0)[:, :t]
            o_ref[0, 2 * w] = jnp.where(visible, skew * LOG2E, NEG_INF)
        o_ref[0, 1] = jnp.full((t, t), NEG_INF, F32)

    return _call(
        body, name="bias_expand", grid=(N_HEADS,),
        in_specs=[pl.BlockSpec((1, 2, 2 * t), lambda h: (h, 0, 0))],
        out_specs=pl.BlockSpec((1, 3, t, t), lambda h: (h, 0, 0, 0)),
        out_shape=jax.ShapeDtypeStruct((N_HEADS, 3, t, t), F32),
        compiler_params=_params(1))(diag)


def bias_fold(dtiles):
    t = ATT_T

    def body(d_ref, o_ref):
        pad = jnp.zeros((8, t), F32)
        for w in range(2):
            acc = jnp.concatenate([d_ref[0, w, 0:8, :], pad], axis=1)
            for g in range(1, t // 8):
                grp = jnp.concatenate([d_ref[0, w, 8 * g:8 * g + 8, :], pad], axis=1)
                acc = acc + pltpu.roll(grp, 2 * t - 8 * g, 1)
            out = acc[0:1, :]
            for s in range(1, 8):
                out = out + pltpu.roll(acc, 2 * t - s, 1)[s:s + 1, :]
            o_ref[0, w:w + 1, :] = out

    return _call(
        body, name="bias_fold", grid=(N_HEADS,),
        in_specs=[pl.BlockSpec((1, 2, t, t), lambda h: (h, 0, 0, 0))],
        out_specs=pl.BlockSpec((1, 2, 2 * t), lambda h: (h, 0, 0)),
        out_shape=jax.ShapeDtypeStruct((N_HEADS, 2, 2 * t), F32),
        compiler_params=_params(1))(dtiles)


def _bias_tiles(table):
    diag = jnp.einsum('hr,wdr->hwd', table, jnp.asarray(_band_onehot()), precision=lax.Precision.HIGHEST)
    return bias_expand(diag)


def _bias_tiles_grad(dtiles):
    return jnp.einsum('hwd,wdr->hr', bias_fold(dtiles), jnp.asarray(_band_onehot()),
                      precision=lax.Precision.HIGHEST)


def _layer_consts(lw):
    tri = np.tril(np.ones((SGU_BLOCK, SGU_BLOCK), np.float32))
    ws = (lw['sgu_w'] * tri).astype(BF16)
    return dict(
        ws=ws, ws_t=jnp.swapaxes(ws, 1, 2), sgu_bias=jnp.repeat(lw['sgu_b'].T, CA_HEAD_DIM, axis=1),
        bias=_bias_tiles(lw['ca_rel_bias']),
        wq=_pad_heads(lw['mla_w_uq'], MLA_QK, 0, MLA_QK),
        wk=_pad_heads(lw['mla_w_ukv'], MLA_NOPE + MLA_V, 0, MLA_NOPE),
        wv=_pad_heads(lw['mla_w_ukv'], MLA_NOPE + MLA_V, MLA_NOPE, MLA_NOPE + MLA_V),
        gate_b=lw['gate_b'].reshape(1, 3 * D_MODEL),
        pre_g=lw['pre_g'][None], post_g=lw['post_g'][None], ln_g=lw['sgu_ln_g'][None], ln_b=lw['sgu_ln_b'][None],
        qg=lw['mla_q_norm_g'][None], kvg=lw['mla_kv_norm_g'][None])


def _layer_fwd(x, lw, k, tab, next_shards):
    proj, xnt = norm_matmul(x, k['pre_g'], lw['w_in'])
    ya = sgu_fwd(proj, k['ln_g'], k['ln_b'], k['ws'], k['sgu_bias'])
    qb, kb, vb, qc, kc, vc, kbt, vbt, kct, vct, cq, ckv = prep_fwd(proj, tab, k['qg'], k['kvg'], k['wq'], k['wk'],
                                                                   k['wv'])
    ob, lse_b, gathered = mla_fwd(qb, kb, vbt, next_shards)
    oc, lse_c = band_fwd(qc, kc, vct, k['bias'])
    x_new, yb, yc, merged, out = merge_fwd(ob, oc, proj, ya, k['gate_b'], lw['w_branch'], lw['w_out'], x,
                                           k['post_g'])
    saved = dict(x=x, proj=proj, xnt=xnt, ya=ya, yb=yb, yc=yc, qb=qb, kb=kb, vb=vb, qc=qc, kc=kc, vc=vc, kbt=kbt, kct=kct,
                 cq=cq, ckv=ckv, ob=ob, oc=oc, lse_b=lse_b, lse_c=lse_c, merged=merged, out=out)
    return x_new, saved, gathered


def _layer_bwd(g, s, lw, k, tab, pending_parts, scatter_own):
    S = g.shape[0]
    H = N_HEADS
    dproj, dout, dba, dbb, dbc, dy, g_gate_b, g_post = gate_bwd(
        g, s['out'], k['post_g'], lw['w_out'], s['proj'], k['gate_b'], s['ya'], s['yb'], s['yc'], lw['w_branch'])
    g_w_out = matmul_tn(s['merged'], dout, D_MODEL)
    g_w_branch = jnp.stack([matmul_tn(y, d, D_MODEL) for y, d in ((s['ya'], dba), (s['yb'], dbb), (s['yc'], dbc))])
    dproj, dob, doc, dl_b, dl_c = ungate_bwd(dproj, dy, s['ob'], s['oc'], s['proj'])
    dqb, dkb, dvb, landed = mla_bwd(s['qb'], s['kb'], s['kbt'], s['vb'], dob, s['lse_b'], dl_b, pending_parts)
    dqc, dkc, dvc, dbias = band_bwd(s['qc'], s['kc'], s['kct'], s['vc'], doc, s['lse_c'], dl_c, k['bias'])
    dproj, dqf, dkf, dvf, g_qg, g_kvg = prep_bwd(dproj, dqb, dkb, dvb, dqc, dkc, dvc, s['proj'], tab,
                                                 k['qg'], k['kvg'], k['wq'], k['wk'], k['wv'])
    wide = N_HEADS * HEAD_PAD
    g_wq = matmul_tn(s['cq'], dqf, wide).reshape(MLA_Q_RANK, H, HEAD_PAD)[:, :, :MLA_QK]
    g_wk = matmul_tn(s['ckv'], dkf, wide).reshape(MLA_KV_RANK, H, HEAD_PAD)[:, :, :MLA_NOPE]
    g_wv = matmul_tn(s['ckv'], dvf, wide).reshape(MLA_KV_RANK, H, HEAD_PAD)[:, :, :MLA_V]
    dproj, g_ln_g, g_ln_b, g_ws, g_sgu_bias = sgu_bwd(dproj, dy, s['proj'], k['ln_g'], k['ln_b'], k['ws'],
                                                      k['ws_t'], k['sgu_bias'])
    g_w_in = matmul_acc(s['xnt'], dproj, MM_TN)
    sharded = _sharded_parts(dict(
        w_in=g_w_in, mla_w_uq=g_wq.reshape(MLA_Q_RANK, H * MLA_QK),
        mla_w_ukv=jnp.concatenate([g_wk, g_wv], axis=2).reshape(MLA_KV_RANK, H * (MLA_NOPE + MLA_V)),
        w_branch=g_w_branch, gate_b=g_gate_b.reshape(N_BRANCH, D_MODEL), w_out=g_w_out))
    dx, g_pre, own_landed = proj_bwd_x(dproj, lw['w_in'], s['x'], k['pre_g'], g, sharded if scatter_own else ())
    tri = np.tril(np.ones((SGU_BLOCK, SGU_BLOCK), np.float32))
    small = _small_pack(dict(
        pre_g=g_pre[0], post_g=g_post[0], sgu_ln_g=g_ln_g[0], sgu_ln_b=g_ln_b[0],
        sgu_w=g_ws * tri, sgu_b=jnp.sum(g_sgu_bias.reshape(SGU_BLOCK, 8, CA_HEAD_DIM), axis=2).T,
        mla_q_norm_g=g_qg[0], mla_kv_norm_g=g_kvg[0], ca_rel_bias=_bias_tiles_grad(dbias)))
    return dx, (own_landed if scatter_own else sharded), small, landed


BF16_PARTS = ('w_in', 'mla_w_uq', 'mla_w_ukv', 'w_branch', 'w_out')


def _weight_shards(w, l):
    return [w[n][l].astype(BF16) if n in BF16_PARTS else w[n][l] for n in SHARDED]


def _full_weights(gathered, small):
    lw = {n: _join4(a, SHARD_AXIS[n]) for n, a in zip(SHARDED, gathered) if n != 'w_in'}
    lw['w_in'] = _perm_from_shards(gathered[0])
    lw.update(small)
    return lw


def _small_pack(grads):
    flat = jnp.concatenate([grads[n].reshape(-1) for n in SMALL])
    quarter = -(-flat.size // (4 * 1024)) * 1024
    return jnp.pad(flat, (0, 4 * quarter - flat.size)).reshape(4, quarter // 128, 128)


def _sharded_parts(grads):
    parts = [_shards_from_perm(grads['w_in'])]
    parts += [_split4(grads[n], SHARD_AXIS[n]) for n in SHARDED if n != 'w_in']
    return [p.astype(BF16) if n in BF16_PARTS else p for n, p in zip(SHARDED, parts)]


def train_step_local(x, target, w):
    S = x.shape[0]
    depth = w['w_in'].shape[0]
    tab = _rope_table(S)
    gathered = chip_exchange(_weight_shards(w, 0), False, "gather_weights")
    layer_w, consts, saved = [], [], []
    for l in range(depth):
        lw = _full_weights(gathered, {n: w[n][l] for n in SMALL})
        k = _layer_consts(lw)
        x, s, gathered = _layer_fwd(x, lw, k, tab, _weight_shards(w, l + 1) if l + 1 < depth else ())
        layer_w.append(lw)
        consts.append(k)
        saved.append(s)
    g, sq = loss_head(x, target)
    mine = [None] * depth
    pending = ()
    for l in reversed(range(depth)):
        g, sharded, small, landed = _layer_bwd(g, saved[l], layer_w[l], consts[l], tab, pending, l == 0)
        if pending:
            mine[l + 1] = [add_lead(p) for p in landed]
        pending = list(sharded) + [small]
    mine[0] = [add_lead(p) for p in pending[:-1] + list(chip_exchange(pending[-1:], True, "scatter_small"))]
    n_parts = len(mine[0])
    theirs = sibling_exchange([p for layer in mine for p in layer])
    return sq, g, [(mine[l], theirs[l * n_parts:(l + 1) * n_parts]) for l in range(depth)]


def kernel(x, w_in, pre_g, post_g, sgu_ln_g, sgu_ln_b, sgu_w, sgu_b, mla_q_norm_g, mla_kv_norm_g, mla_w_uq, mla_w_ukv, ca_rel_bias, w_branch, gate_b, w_out, loss_target, m_w_in, m_pre_g, m_post_g, m_sgu_ln_g, m_sgu_ln_b, m_sgu_w, m_sgu_b, m_mla_q_norm_g, m_mla_kv_norm_g, m_mla_w_uq, m_mla_w_ukv, m_ca_rel_bias, m_w_branch, m_gate_b, m_w_out, v_w_in, v_pre_g, v_post_g, v_sgu_ln_g, v_sgu_ln_b, v_sgu_w, v_sgu_b, v_mla_q_norm_g, v_mla_kv_norm_g, v_mla_w_uq, v_mla_w_ukv, v_ca_rel_bias, v_w_branch, v_gate_b, v_w_out):
    w = dict(w_in=w_in, pre_g=pre_g, post_g=post_g, sgu_ln_g=sgu_ln_g, sgu_ln_b=sgu_ln_b, sgu_w=sgu_w, sgu_b=sgu_b,
             mla_q_norm_g=mla_q_norm_g, mla_kv_norm_g=mla_kv_norm_g, mla_w_uq=mla_w_uq, mla_w_ukv=mla_w_ukv,
             ca_rel_bias=ca_rel_bias, w_branch=w_branch, gate_b=gate_b, w_out=w_out)
    m = dict(w_in=m_w_in, pre_g=m_pre_g, post_g=m_post_g, sgu_ln_g=m_sgu_ln_g, sgu_ln_b=m_sgu_ln_b, sgu_w=m_sgu_w,
             sgu_b=m_sgu_b, mla_q_norm_g=m_mla_q_norm_g, mla_kv_norm_g=m_mla_kv_norm_g, mla_w_uq=m_mla_w_uq,
             mla_w_ukv=m_mla_w_ukv, ca_rel_bias=m_ca_rel_bias, w_branch=m_w_branch, gate_b=m_gate_b, w_out=m_w_out)
    v = dict(w_in=v_w_in, pre_g=v_pre_g, post_g=v_post_g, sgu_ln_g=v_sgu_ln_g, sgu_ln_b=v_sgu_ln_b, sgu_w=v_sgu_w,
             sgu_b=v_sgu_b, mla_q_norm_g=v_mla_q_norm_g, mla_kv_norm_g=v_mla_kv_norm_g, mla_w_uq=v_mla_w_uq,
             mla_w_ukv=v_mla_w_ukv, ca_rel_bias=v_ca_rel_bias, w_branch=v_w_branch, gate_b=v_gate_b, w_out=v_w_out)
    depth = w_in.shape[0]
    sq, grad_x, reduced = train_step_local(x[0], loss_target[0], w)
    loss = lax.psum(0.5 * jnp.sum(sq) / D_MODEL, ("x", "y", "c"))

    out = {}
    for a, n in enumerate(SHARDED):
        mine = jnp.stack([reduced[l][0][a] for l in range(depth)])
        theirs = jnp.stack([reduced[l][1][a] for l in range(depth)])
        out[n] = adamw(w[n], [mine, theirs], m[n], v[n])
    small = jnp.stack([jnp.stack([reduced[l][0][-1] for l in range(depth)]),
                       jnp.stack([reduced[l][1][-1] for l in range(depth)])])
    quarter = add_lead(small)
    full = chip_exchange([quarter], False, "gather_small")[0]
    full = jnp.moveaxis(full, 0, 1).reshape(depth, -1)
    off = 0
    for n in SMALL:
        size = int(np.prod(w[n].shape[1:]))
        out[n] = adamw(w[n], [full[:, off:off + size].reshape(w[n].shape)], m[n], v[n])
        off += size
    return (loss, grad_x[None], *[out[n][0] for n in WEIGHTS], *[out[n][1] for n in WEIGHTS],
            *[out[n][2] for n in WEIGHTS], *[out[n][3] for n in WEIGHTS])
```

```python
import numpy as np
import jax
import jax.numpy as jnp
from jax import lax
from jax.experimental import pallas as pl
from jax.experimental.pallas import tpu as pltpu

F32 = jnp.float32
BF16 = jnp.bfloat16
MESH = pl.DeviceIdType.MESH

EPS = 1e-6
NEG_INF = -1e30
D_MODEL = 1024
BR_WIDTH = 512
N_BRANCH = 3
N_HEADS = 8
HEAD_PAD = 128
CHUNK_SHIFT = 6
SGU_BLOCK = 128
MLA_NOPE, MLA_ROPE, MLA_V = 64, 32, 64
MLA_QK = MLA_NOPE + MLA_ROPE
MLA_Q_RANK, MLA_KV_RANK = 256, 128
CA_HEAD_DIM = 64
REL_CLIP = 128
ROPE_BASE = 10000.0
D_IN = 7584

ADAM_LR, ADAM_B1, ADAM_B2, ADAM_EPS, ADAM_WD, ADAM_STEP = 0.001, 0.9, 0.999, 1e-08, 0.01, 10

P_QC, P_KC, P_VC, P_QD, P_KVD, P_KR, P_ZB, P_ZC, P_G, P_U, P_V, P_ZA, P_W = (
    0, 512, 1024, 1536, 1792, 1920, 2048, 2560, 3072, 6144, 6656, 7168, 7680)
NAT_SEGS = [(0, 1536, P_U), (1536, 1920, P_QD), (1920, 1952, P_KR + MLA_NOPE), (1952, 2464, P_ZB),
            (2464, 4000, P_QC), (4000, 4512, P_ZC), (4512, 7584, P_G)]
SHARD_COLS = D_IN // 4

VMEM_LIMIT = 48 * 1024 * 1024
ATT_T = 512
BAND_FWD_HEADS_PER_STEP = 4
MLA_FWD_HEADS_PER_STEP = 8
MLA_BWD_HEADS_PER_STEP = 4
ROW_TILE = 256
LIGHT_ROW_TILE = 512
MM_TM = 512
MM_TN = 1536
LOG2E = 1.4426950408889634
MLA_SCALE = MLA_QK ** -0.5
CA_SCALE = CA_HEAD_DIM ** -0.5

WEIGHTS = ['w_in', 'pre_g', 'post_g', 'sgu_ln_g', 'sgu_ln_b', 'sgu_w', 'sgu_b', 'mla_q_norm_g',
           'mla_kv_norm_g', 'mla_w_uq', 'mla_w_ukv', 'ca_rel_bias', 'w_branch', 'gate_b', 'w_out']
SHARDED = ['w_in', 'mla_w_uq', 'mla_w_ukv', 'w_branch', 'gate_b', 'w_out']
SMALL = ['pre_g', 'post_g', 'sgu_ln_g', 'sgu_ln_b', 'sgu_w', 'sgu_b', 'mla_q_norm_g',
         'mla_kv_norm_g', 'ca_rel_bias']
SHARD_AXIS = {'w_in': 1, 'mla_w_uq': 1, 'mla_w_ukv': 1, 'w_branch': 2, 'gate_b': 1, 'w_out': 0}


def _call(body, **kw):
    return pl.pallas_call(body, **kw)


def _params(n_axes):
    return pltpu.CompilerParams(dimension_semantics=("arbitrary",) * n_axes,
                                vmem_limit_bytes=VMEM_LIMIT)


def _nt(a, b):
    return lax.dot_general(a, b, (((1,), (1,)), ((), ())), preferred_element_type=F32)


def _nn(a, b):
    return jnp.dot(a, b, preferred_element_type=F32)


def _tn(a, b):
    return lax.dot_general(a, b, (((0,), (0,)), ((), ())), preferred_element_type=F32)


def _rms(xv, g):
    r = lax.rsqrt(jnp.mean(xv * xv, axis=-1, keepdims=True) + EPS)
    return xv * r * g, r


def _rms_bwd(xv, g, r, dy):
    gy = dy * g
    dx = r * gy - xv * (r * r * r) * jnp.mean(xv * gy, axis=-1, keepdims=True)
    dg = jnp.sum(dy * (xv * r), axis=0, keepdims=True)
    return dx, dg


def _sigmoid(z):
    return 1.0 / (1.0 + jnp.exp(-z))


def _rope(xv, c, a, b):
    return xv * c + pltpu.roll(xv, 112, 1) * a + pltpu.roll(xv, 16, 1) * b


def _accumulate(ref, val, first):
    @pl.when(first)
    def _():
        ref[...] = val

    @pl.when(jnp.logical_not(first))
    def _():
        ref[...] += val


def norm_matmul(x, g, w):
    S, D = x.shape
    N = w.shape[1]
    tm, tn = min(S, 2 * MM_TM), MM_TN

    def body(x_ref, g_ref, w_ref, o_ref, xnt_ref, xn_s):
        @pl.when(pl.program_id(1) == 0)
        def _():
            y, _ = _rms(x_ref[...], g_ref[...])
            xn_s[...] = y.astype(BF16)
            xnt_ref[...] = y.T.astype(BF16)

        o_ref[...] = _nn(xn_s[...], w_ref[...])

    return _call(
        body, name="norm_matmul", grid=(S // tm, N // tn),
        in_specs=[pl.BlockSpec((tm, D), lambda i, j: (i, 0)),
                  pl.BlockSpec((1, D), lambda i, j: (0, 0)),
                  pl.BlockSpec((D, tn), lambda i, j: (0, j))],
        out_specs=[pl.BlockSpec((tm, tn), lambda i, j: (i, j)),
                   pl.BlockSpec((D, tm), lambda i, j: (0, i))],
        out_shape=[jax.ShapeDtypeStruct((S, N), F32), jax.ShapeDtypeStruct((D, S), BF16)],
        scratch_shapes=[pltpu.VMEM((tm, D), BF16)],
        compiler_params=_params(2))(x, g, w)


def matmul_acc(a, b, tn):
    M, S = a.shape
    N = b.shape[1]
    tk = min(S, 2 * MM_TM)

    def body(a_ref, b_ref, o_ref):
        @pl.when(pl.program_id(1) == 0)
        def _():
            o_ref[...] = jnp.zeros_like(o_ref)

        o_ref[...] += _nn(a_ref[...], b_ref[...])

    return _call(
        body, name="matmul_acc", grid=(N // tn, S // tk),
        in_specs=[pl.BlockSpec((M, tk), lambda j, k: (0, k)),
                  pl.BlockSpec((tk, tn), lambda j, k: (k, j))],
        out_specs=pl.BlockSpec((M, tn), lambda j, k: (0, j)),
        out_shape=jax.ShapeDtypeStruct((M, N), F32),
        compiler_params=_params(2))(a, b)


def proj_bwd_x(dproj, w, x, g, resid, exchange=()):
    S, N = dproj.shape
    D = x.shape[1]
    tm, tk = min(S, MM_TM), MM_TN
    nk, n_ex = N // tk, len(exchange)

    def body(dp_ref, w_ref, x_ref, g_ref, r_ref, *rest):
        ex_src, (dx_ref, dg_ref), ex_out, (acc_ref,), ex_sems = _split_refs(
            rest, (n_ex, 2, n_ex, 1, 3 if n_ex else 0))
        i, k = pl.program_id(0), pl.program_id(1)

        if n_ex:
            @pl.when(jnp.logical_and(i == 0, k == 0))
            def _():
                _exchange_issue(ex_src, ex_out, ex_sems, True, True)

        @pl.when(k == 0)
        def _():
            acc_ref[...] = jnp.zeros_like(acc_ref)

        acc_ref[...] += _nt(dp_ref[...].astype(BF16), w_ref[...])

        @pl.when(k == nk - 1)
        def _():
            xv = x_ref[...]
            _, r = _rms(xv, g_ref[...])
            dx, dg = _rms_bwd(xv, g_ref[...], r, acc_ref[...])
            dx_ref[...] = dx + r_ref[...]
            _accumulate(dg_ref, dg, i == 0)

        if n_ex:
            @pl.when(jnp.logical_and(i == S // tm - 1, k == nk - 1))
            def _():
                _exchange_issue(ex_src, ex_out, ex_sems, True, False)

    outs = _call(
        body, name="proj_bwd_x_scatter" if n_ex else "proj_bwd_x", grid=(S // tm, nk),
        in_specs=[pl.BlockSpec((tm, tk), lambda i, k: (i, k)),
                  pl.BlockSpec((D, tk), lambda i, k: (0, k)),
                  pl.BlockSpec((tm, D), lambda i, k: (i, 0)),
                  pl.BlockSpec((1, D), lambda i, k: (0, 0)),
                  pl.BlockSpec((tm, D), lambda i, k: (i, 0))] + [ANY] * n_ex,
        out_specs=[pl.BlockSpec((tm, D), lambda i, k: (i, 0)),
                   pl.BlockSpec((1, D), lambda i, k: (0, 0))] + [ANY] * n_ex,
        out_shape=[jax.ShapeDtypeStruct((S, D), F32), jax.ShapeDtypeStruct((1, D), F32)] +
        _exchange_out_shape(exchange, True),
        scratch_shapes=[pltpu.VMEM((tm, D), F32)] + _exchange_sems(n_ex),
        compiler_params=_params(2))(dproj, w, x, g, resid, *exchange)
    return outs[0], outs[1], outs[2:]


def matmul_tn(a, b, tn):
    S, M = a.shape
    N = b.shape[1]
    tk = min(S, 2 * MM_TM)

    def body(a_ref, b_ref, o_ref):
        @pl.when(pl.program_id(1) == 0)
        def _():
            o_ref[...] = jnp.zeros_like(o_ref)

        o_ref[...] += _tn(a_ref[...].astype(BF16), b_ref[...].astype(BF16))

    return _call(
        body, name="matmul_tn", grid=(N // tn, S // tk),
        in_specs=[pl.BlockSpec((tk, M), lambda j, k: (k, 0)),
                  pl.BlockSpec((tk, tn), lambda j, k: (k, j))],
        out_specs=pl.BlockSpec((M, tn), lambda j, k: (0, j)),
        out_shape=jax.ShapeDtypeStruct((M, N), F32),
        compiler_params=_params(2))(a, b)


def _sgu_block(vv, g, b, ws_ref, lane):
    mu = jnp.mean(vv, axis=-1, keepdims=True)
    xc = vv - mu
    r = lax.rsqrt(jnp.mean(xc * xc, axis=-1, keepdims=True) + EPS)
    xhat = xc * r
    vln = (xhat * g + b).astype(BF16)
    pieces = []
    for p in range(4):
        vp = vln[:, p * 128:(p + 1) * 128]
        pieces.append(jnp.where(lane < 64, _nn(ws_ref[2 * p], vp), _nn(ws_ref[2 * p + 1], vp)))
    return xhat, r, vln, jnp.concatenate(pieces, axis=1)


def sgu_fwd(proj, ln_g, ln_b, ws, bias_full):
    S = proj.shape[0]
    ts = LIGHT_ROW_TILE

    def body(u_ref, v_ref, z_ref, g_ref, b_ref, ws_ref, bf_ref, y_ref):
        lane = lax.broadcasted_iota(jnp.int32, (SGU_BLOCK, 128), 1)
        for blk in range(ts // SGU_BLOCK):
            rows = slice(blk * SGU_BLOCK, (blk + 1) * SGU_BLOCK)
            _, _, _, mixed = _sgu_block(v_ref[rows, :], g_ref[...], b_ref[...], ws_ref, lane)
            mixed = mixed + bf_ref[...]
            zz = z_ref[rows, :]
            y_ref[rows, :] = (u_ref[rows, :] * mixed * (zz * _sigmoid(zz))).astype(BF16)

    col = lambda c: pl.BlockSpec((ts, BR_WIDTH), lambda i: (i, c))
    full = lambda shape: pl.BlockSpec(shape, lambda i: (0,) * len(shape))
    return _call(
        body, name="sgu_fwd", grid=(S // ts,),
        in_specs=[col(P_U // 512), col(P_V // 512), col(P_ZA // 512),
                  full((1, BR_WIDTH)), full((1, BR_WIDTH)), full((8, 128, 128)), full((128, BR_WIDTH))],
        out_specs=pl.BlockSpec((ts, BR_WIDTH), lambda i: (i, 0)),
        out_shape=jax.ShapeDtypeStruct((S, BR_WIDTH), BF16),
        compiler_params=_params(1))(proj, proj, proj, ln_g, ln_b, ws, bias_full)


def sgu_bwd(dproj, dy, proj, ln_g, ln_b, ws, ws_t, bias_full):
    S = proj.shape[0]
    ts = LIGHT_ROW_TILE

    def body(dp_in, dy_ref, u_ref, v_ref, z_ref, g_ref, b_ref, ws_ref, wst_ref, bf_ref,
             dp_ref, gg_ref, gb_ref, gws_ref, gbf_ref):
        del dp_in
        first = pl.program_id(0) == 0

        @pl.when(first)
        def _():
            gg_ref[...] = jnp.zeros_like(gg_ref)
            gb_ref[...] = jnp.zeros_like(gb_ref)
            gws_ref[...] = jnp.zeros_like(gws_ref)
            gbf_ref[...] = jnp.zeros_like(gbf_ref)

        lane = lax.broadcasted_iota(jnp.int32, (SGU_BLOCK, 128), 1)
        for blk in range(ts // SGU_BLOCK):
            rows = slice(blk * SGU_BLOCK, (blk + 1) * SGU_BLOCK)
            g = g_ref[...]
            xhat, r, vln, mixed = _sgu_block(v_ref[rows, :], g, b_ref[...], ws_ref, lane)
            mixed = mixed + bf_ref[...]
            zz = z_ref[rows, :]
            uu = u_ref[rows, :]
            dyv = dy_ref[0, rows, :]
            sg = _sigmoid(zz)
            sil = zz * sg
            dmixed = dyv * uu * sil
            dp_ref[rows, 0:512] = (dyv * mixed * sil).astype(BF16)
            dp_ref[rows, 1024:1536] = (dyv * uu * mixed * (sg * (1.0 + zz * (1.0 - sg)))).astype(BF16)
            gbf_ref[...] += dmixed
            dmb = dmixed.astype(BF16)
            pieces = []
            for p in range(4):
                dmp = dmb[:, p * 128:(p + 1) * 128]
                vp = vln[:, p * 128:(p + 1) * 128]
                pieces.append(jnp.where(lane < 64, _nn(wst_ref[2 * p], dmp), _nn(wst_ref[2 * p + 1], dmp)))
                zero = jnp.zeros_like(dmp)
                gws_ref[2 * p] += _nt(jnp.where(lane < 64, dmp, zero), vp)
                gws_ref[2 * p + 1] += _nt(jnp.where(lane >= 64, dmp, zero), vp)
            dvln = jnp.concatenate(pieces, axis=1)
            dxh = dvln * g
            dp_ref[rows, 512:1024] = (r * (dxh - jnp.mean(dxh, axis=-1, keepdims=True)
                                           - xhat * jnp.mean(dxh * xhat, axis=-1, keepdims=True))).astype(BF16)
            gg_ref[...] += jnp.sum(dvln * xhat, axis=0, keepdims=True)
            gb_ref[...] += jnp.sum(dvln, axis=0, keepdims=True)

    col = lambda c: pl.BlockSpec((ts, BR_WIDTH), lambda i: (i, c))
    full = lambda shape: pl.BlockSpec(shape, lambda i: (0,) * len(shape))
    return _call(
        body, name="sgu_bwd", grid=(S // ts,),
        in_specs=[pl.BlockSpec(memory_space=pl.ANY),
                  pl.BlockSpec((1, ts, BR_WIDTH), lambda i: (0, i, 0)),
                  col(P_U // 512), col(P_V // 512), col(P_ZA // 512),
                  full((1, BR_WIDTH)), full((1, BR_WIDTH)), full((8, 128, 128)), full((8, 128, 128)),
                  full((128, BR_WIDTH))],
        out_specs=[pl.BlockSpec((ts, 1536), lambda i: (i, P_U // 1536)),
                   full((1, BR_WIDTH)), full((1, BR_WIDTH)), full((8, 128, 128)), full((128, BR_WIDTH))],
        out_shape=[jax.ShapeDtypeStruct(dproj.shape, BF16),
                   jax.ShapeDtypeStruct((1, BR_WIDTH), F32), jax.ShapeDtypeStruct((1, BR_WIDTH), F32),
                   jax.ShapeDtypeStruct((8, 128, 128), F32), jax.ShapeDtypeStruct((128, BR_WIDTH), F32)],
        input_output_aliases={0: 0},
        compiler_params=_params(1))(dproj, dy, proj, proj, proj, ln_g, ln_b, ws, ws_t, bias_full)


def _hspec(ts):
    return pl.BlockSpec((N_HEADS, ts, HEAD_PAD), lambda i: (0, i, 0))


def prep_fwd(proj, tab, qg, kvg, wq, wk, wv):
    S = proj.shape[0]
    ts = LIGHT_ROW_TILE

    def body(qc_ref, kc_ref, vc_ref, qd_ref, kvd_ref, kr_ref, tab_ref, qg_ref, kvg_ref,
             wq_ref, wk_ref, wv_ref, qb, kb, vb, qc, kc, vc, kbt, vbt, kct, vct, cq_o, ckv_o):
        c, a, b = tab_ref[0], tab_ref[1], tab_ref[2]
        cq, _ = _rms(qd_ref[...], qg_ref[...])
        ckv, _ = _rms(kvd_ref[...], kvg_ref[...])
        cqb, ckvb = cq.astype(BF16), ckv.astype(BF16)
        cq_o[...] = cqb
        ckv_o[...] = ckvb
        krr = _rope(kr_ref[...], c, a, b)
        lane = lax.broadcasted_iota(jnp.int32, (ts, 128), 1)
        ones_lane = jnp.where(lane == MLA_V, 1.0, 0.0)
        for h in range(N_HEADS):
            cols = slice(h * HEAD_PAD, (h + 1) * HEAD_PAD)
            qb[h] = (_rope(_nn(cqb, wq_ref[:, cols]), c, a, b) * (MLA_SCALE * LOG2E)).astype(BF16)
            kh = _nn(ckvb, wk_ref[:, cols]) + krr
            vh = _nn(ckvb, wv_ref[:, cols]) + ones_lane
            kb[h], kbt[h] = kh.astype(BF16), kh.T.astype(BF16)
            vb[h], vbt[h] = vh.astype(BF16), vh.T.astype(BF16)
        for p in range(4):
            piece = qc_ref[:, p * 128:(p + 1) * 128] * (CA_SCALE * LOG2E)
            qc[2 * p] = jnp.where(lane < 64, piece, 0.0).astype(BF16)
            qc[2 * p + 1] = jnp.where(lane < 64, pltpu.roll(piece, 64, 1), 0.0).astype(BF16)
            for src, dst, dst_t, pad in ((kc_ref, kc, kct, 0.0), (vc_ref, vc, vct, ones_lane)):
                piece = src[:, p * 128:(p + 1) * 128]
                for h, head in ((2 * p, jnp.where(lane < 64, piece, pad)),
                                (2 * p + 1, jnp.where(lane < 64, pltpu.roll(piece, 64, 1), pad))):
                    dst[h], dst_t[h] = head.astype(BF16), head.T.astype(BF16)

    col = lambda w, c: pl.BlockSpec((ts, w), lambda i: (i, c))
    full = lambda shape: pl.BlockSpec(shape, lambda i: (0,) * len(shape))
    hshape = jax.ShapeDtypeStruct((N_HEADS, S, HEAD_PAD), BF16)
    tshape = jax.ShapeDtypeStruct((N_HEADS, HEAD_PAD, S), BF16)
    tspec = pl.BlockSpec((N_HEADS, HEAD_PAD, ts), lambda i: (0, 0, i))
    return _call(
        body, name="prep_fwd", grid=(S // ts,),
        in_specs=[col(512, P_QC // 512), col(512, P_KC // 512), col(512, P_VC // 512),
                  col(256, P_QD // 256), col(128, P_KVD // 128), col(128, P_KR // 128),
                  pl.BlockSpec((3, ts, 128), lambda i: (0, i, 0)),
                  full((1, MLA_Q_RANK)), full((1, MLA_KV_RANK)),
                  full((MLA_Q_RANK, 1024)), full((MLA_KV_RANK, 1024)), full((MLA_KV_RANK, 1024))],
        out_specs=[_hspec(ts)] * 6 + [tspec] * 4 + [pl.BlockSpec((ts, MLA_Q_RANK), lambda i: (i, 0)),
                                                    pl.BlockSpec((ts, MLA_KV_RANK), lambda i: (i, 0))],
        out_shape=[hshape] * 6 + [tshape] * 4 + [jax.ShapeDtypeStruct((S, MLA_Q_RANK), BF16),
                                                 jax.ShapeDtypeStruct((S, MLA_KV_RANK), BF16)],
        compiler_params=_params(1))(proj, proj, proj, proj, proj, proj, tab, qg, kvg, wq, wk, wv)


def prep_bwd(dproj, dqb, dkb, dvb, dqc, dkc, dvc, proj, tab, qg, kvg, wq, wk, wv):
    S = proj.shape[0]
    ts = LIGHT_ROW_TILE

    def body(dp_in, dqb_r, dkb_r, dvb_r, dqc_r, dkc_r, dvc_r, qd_ref, kvd_ref, tab_ref, qg_ref, kvg_ref,
             wq_ref, wk_ref, wv_ref, dp_ref, dqf, dkf, dvf, gq_ref, gkv_ref):
        del dp_in
        c, a, b = tab_ref[0], -tab_ref[1], -tab_ref[2]
        qd, kvd = qd_ref[...], kvd_ref[...]
        _, rq = _rms(qd, qg_ref[...])
        _, rkv = _rms(kvd, kvg_ref[...])
        dcq = jnp.zeros((ts, MLA_Q_RANK), F32)
        dckv = jnp.zeros((ts, MLA_KV_RANK), F32)
        dksum = jnp.zeros((ts, HEAD_PAD), F32)
        for h in range(N_HEADS):
            cols = slice(h * HEAD_PAD, (h + 1) * HEAD_PAD)
            dqh = _rope(dqb_r[h].astype(F32) * MLA_SCALE, c, a, b).astype(BF16)
            dqf[:, cols] = dqh
            dcq = dcq + _nt(dqh, wq_ref[:, cols])
            dk = dkb_r[h].astype(F32) * (1.0 / LOG2E)
            dksum = dksum + dk
            dkh = dk.astype(BF16)
            dkf[:, cols] = dkh
            dvh = dvb_r[h].astype(BF16)
            dvf[:, cols] = dvh
            dckv = dckv + _nt(dkh, wk_ref[:, cols]) + _nt(dvh, wv_ref[:, cols])
        lane = lax.broadcasted_iota(jnp.int32, (ts, 128), 1)
        rope_lanes = jnp.logical_and(lane >= MLA_NOPE, lane < MLA_QK)
        dp_ref[:, P_KR:P_KR + 128] = jnp.where(rope_lanes, _rope(dksum, c, a, b), 0.0).astype(BF16)
        dqd, gq = _rms_bwd(qd, qg_ref[...], rq, dcq)
        dkvd, gkv = _rms_bwd(kvd, kvg_ref[...], rkv, dckv)
        dp_ref[:, P_QD:P_QD + 256] = dqd.astype(BF16)
        dp_ref[:, P_KVD:P_KVD + 128] = dkvd.astype(BF16)
        first = pl.program_id(0) == 0
        _accumulate(gq_ref, gq, first)
        _accumulate(gkv_ref, gkv, first)
        for src, base, factor in ((dqc_r, P_QC, CA_SCALE), (dkc_r, P_KC, 1.0 / LOG2E), (dvc_r, P_VC, 1.0)):
            for p in range(4):
                dp_ref[:, base + p * 128:base + (p + 1) * 128] = (
                    (src[2 * p].astype(F32) + pltpu.roll(src[2 * p + 1].astype(F32), 64, 1)) * factor).astype(BF16)

    col = lambda w, c: pl.BlockSpec((ts, w), lambda i: (i, c))
    full = lambda shape: pl.BlockSpec(shape, lambda i: (0,) * len(shape))
    wide = jax.ShapeDtypeStruct((S, 1024), BF16)
    return _call(
        body, name="prep_bwd", grid=(S // ts,),
        in_specs=[pl.BlockSpec(memory_space=pl.ANY)] + [_hspec(ts)] * 6 +
                 [col(256, P_QD // 256), col(128, P_KVD // 128),
                  pl.BlockSpec((3, ts, 128), lambda i: (0, i, 0)),
                  full((1, MLA_Q_RANK)), full((1, MLA_KV_RANK)),
                  full((MLA_Q_RANK, 1024)), full((MLA_KV_RANK, 1024)), full((MLA_KV_RANK, 1024))],
        out_specs=[pl.BlockSpec((ts, 2048), lambda i: (i, 0))] + [pl.BlockSpec((ts, 1024), lambda i: (i, 0))] * 3 +
                  [full((1, MLA_Q_RANK)), full((1, MLA_KV_RANK))],
        out_shape=[jax.ShapeDtypeStruct(dproj.shape, BF16), wide, wide, wide,
                   jax.ShapeDtypeStruct((1, MLA_Q_RANK), F32), jax.ShapeDtypeStruct((1, MLA_KV_RANK), F32)],
        input_output_aliases={0: 0},
        compiler_params=_params(1))(dproj, dqb, dkb, dvb, dqc, dkc, dvc, proj, proj, tab, qg, kvg, wq, wk, wv)


def _diag_visible(t):
    r = lax.broadcasted_iota(jnp.int32, (t, t), 0) >> CHUNK_SHIFT
    c = lax.broadcasted_iota(jnp.int32, (t, t), 1) >> CHUNK_SHIFT
    return r <= c


def _pair_tables(nq, kv_major):
    if kv_major:
        pairs = [(kb, qi) for kb in range(nq) for qi in range(kb, nq)]
    else:
        pairs = [(kb, qi) for qi in range(nq) for kb in range(qi + 1)]
    return (jnp.asarray(np.array([p[0] for p in pairs], np.int32)),
            jnp.asarray(np.array([p[1] for p in pairs], np.int32)), len(pairs))


def _finish_softmax(acc, m):
    l = acc[MLA_V:MLA_V + 1, :]
    row = lax.broadcasted_iota(jnp.int32, acc.shape, 0)
    return jnp.where(row < MLA_V, acc / l, 0.0).T.astype(BF16), m + jnp.log2(l)


def _split_refs(refs, counts):
    out, pos = [], 0
    for c in counts:
        out.append(refs[pos:pos + c])
        pos += c
    return out


def mla_fwd(q, k, vt, exchange=()):
    H, S, _ = q.shape
    t, hb, n_ex = ATT_T, MLA_FWD_HEADS_PER_STEP, len(exchange)
    kb_tab, qi_tab, n_pairs = _pair_tables(S // t, False)

    def body(kb_ref, qi_ref, q_ref, k_ref, vt_ref, *rest):
        ex_src, (o_ref, lse_ref), ex_out, (m_s, acc_s), ex_sems = _split_refs(rest, (n_ex, 2, n_ex, 2, 3 if n_ex else 0))
        hg, p_id = pl.program_id(0), pl.program_id(1)
        kb, qi = kb_ref[p_id], qi_ref[p_id]

        if n_ex:
            @pl.when(jnp.logical_and(hg == 0, p_id == 0))
            def _():
                _exchange_issue(ex_src, ex_out, ex_sems, False, True)

        @pl.when(kb == 0)
        def _():
            m_s[...] = jnp.full_like(m_s, NEG_INF)
            acc_s[...] = jnp.zeros_like(acc_s)

        def step(masked):
            for h in range(hb):
                st = _nt(k_ref[h], q_ref[h])
                if masked:
                    st = jnp.where(_diag_visible(t), st, NEG_INF)
                m_prev = m_s[h]
                m_new = jnp.maximum(m_prev, jnp.max(st, axis=0, keepdims=True))
                p = jnp.exp2(st - m_new)
                acc_s[h] = jnp.exp2(m_prev - m_new) * acc_s[h] + _nn(vt_ref[h], p.astype(BF16))
                m_s[h] = m_new

        @pl.when(kb < qi)
        def _():
            step(False)

        @pl.when(kb == qi)
        def _():
            step(True)
            for h in range(hb):
                o_ref[h], lse_ref[h] = _finish_softmax(acc_s[h], m_s[h])

        if n_ex:
            @pl.when(jnp.logical_and(hg == H // hb - 1, p_id == n_pairs - 1))
            def _():
                _exchange_issue(ex_src, ex_out, ex_sems, False, False)

    grid_spec = pltpu.PrefetchScalarGridSpec(
        num_scalar_prefetch=2, grid=(H // hb, n_pairs),
        in_specs=[pl.BlockSpec((hb, t, HEAD_PAD), lambda h, p, kb, qi: (h, qi[p], 0)),
                  pl.BlockSpec((hb, t, HEAD_PAD), lambda h, p, kb, qi: (h, kb[p], 0)),
                  pl.BlockSpec((hb, HEAD_PAD, t), lambda h, p, kb, qi: (h, 0, kb[p]))] + [ANY] * n_ex,
        out_specs=[pl.BlockSpec((hb, t, HEAD_PAD), lambda h, p, kb, qi: (h, qi[p], 0)),
                   pl.BlockSpec((hb, 1, t), lambda h, p, kb, qi: (h, 0, qi[p]))] + [ANY] * n_ex,
        scratch_shapes=[pltpu.VMEM((hb, 1, t), F32), pltpu.VMEM((hb, HEAD_PAD, t), F32)] + _exchange_sems(n_ex))
    outs = _call(
        body, name="mla_fwd_gather" if n_ex else "mla_fwd", grid_spec=grid_spec,
        out_shape=[jax.ShapeDtypeStruct((H, S, HEAD_PAD), BF16), jax.ShapeDtypeStruct((H, 1, S), F32)] +
        _exchange_out_shape(exchange, False),
        compiler_params=_params(2))(kb_tab, qi_tab, q, k, vt, *exchange)
    return outs[0], outs[1], outs[2:]


def mla_bwd(q, k, kt, v, do, lse, delta, exchange=()):
    H, S, _ = q.shape
    t, hb, n_ex = ATT_T, MLA_BWD_HEADS_PER_STEP, len(exchange)
    nq = S // t
    kb_tab, qi_tab, n_pairs = _pair_tables(nq, True)

    def body(kb_ref, qi_ref, q_ref, k_ref, kt_ref, v_ref, do_ref, lse_ref, dl_ref, *rest):
        ex_src, (dq_ref, dk_ref, dv_ref), ex_out, (dqt_s, dk_s, dv_s), ex_sems = _split_refs(
            rest, (n_ex, 3, n_ex, 3, 3 if n_ex else 0))
        hg, p_id = pl.program_id(0), pl.program_id(1)
        kb, qi = kb_ref[p_id], qi_ref[p_id]

        if n_ex:
            @pl.when(jnp.logical_and(hg == 0, p_id == 0))
            def _():
                _exchange_issue(ex_src, ex_out, ex_sems, True, True)

        @pl.when(p_id == 0)
        def _():
            dqt_s[...] = jnp.zeros_like(dqt_s)

        @pl.when(qi == kb)
        def _():
            dk_s[...] = jnp.zeros_like(dk_s)
            dv_s[...] = jnp.zeros_like(dv_s)

        def step(masked):
            for h in range(hb):
                st = _nt(k_ref[h], q_ref[h])
                if masked:
                    st = jnp.where(_diag_visible(t), st, NEG_INF)
                pt = jnp.exp2(st - lse_ref[h])
                dv_s[h] += _nn(pt.astype(BF16), do_ref[h])
                dsb = (pt * (_nt(v_ref[h], do_ref[h]) - dl_ref[h])).astype(BF16)
                dk_s[h] += _nn(dsb, q_ref[h])
                dqt_s[h, qi] += _nn(kt_ref[h], dsb)

        @pl.when(qi == kb)
        def _():
            step(True)
            rows = pl.ds(pl.multiple_of(qi * t, t), t)
            for h in range(hb):
                dq_ref[h, rows, :] = dqt_s[h, qi].T.astype(BF16)

        @pl.when(qi > kb)
        def _():
            step(False)

        @pl.when(qi == nq - 1)
        def _():
            dk_ref[...] = dk_s[...].astype(BF16)
            dv_ref[...] = dv_s[...].astype(BF16)

        if n_ex:
            @pl.when(jnp.logical_and(hg == H // hb - 1, p_id == n_pairs - 1))
            def _():
                _exchange_issue(ex_src, ex_out, ex_sems, True, False)

    qtile = pl.BlockSpec((hb, t, HEAD_PAD), lambda h, p, kb, qi: (h, qi[p], 0))
    ktile = pl.BlockSpec((hb, t, HEAD_PAD), lambda h, p, kb, qi: (h, kb[p], 0))
    stat = pl.BlockSpec((hb, 1, t), lambda h, p, kb, qi: (h, 0, qi[p]))
    grid_spec = pltpu.PrefetchScalarGridSpec(
        num_scalar_prefetch=2, grid=(H // hb, n_pairs),
        in_specs=[qtile, ktile, pl.BlockSpec((hb, HEAD_PAD, t), lambda h, p, kb, qi: (h, 0, kb[p])), ktile, qtile,
                  stat, stat] + [ANY] * n_ex,
        out_specs=[pl.BlockSpec((hb, S, HEAD_PAD), lambda h, p, kb, qi: (h, 0, 0)), ktile, ktile] + [ANY] * n_ex,
        scratch_shapes=[pltpu.VMEM((hb, nq, HEAD_PAD, t), F32), pltpu.VMEM((hb, t, HEAD_PAD), F32),
                        pltpu.VMEM((hb, t, HEAD_PAD), F32)] + _exchange_sems(n_ex))
    outs = _call(
        body, name="mla_bwd_scatter" if n_ex else "mla_bwd", grid_spec=grid_spec,
        out_shape=[jax.ShapeDtypeStruct((H, S, HEAD_PAD), BF16)] * 3 + _exchange_out_shape(exchange, True),
        compiler_params=_params(2))(kb_tab, qi_tab, q, k, kt, v, do, lse, delta, *exchange)
    return outs[0], outs[1], outs[2], outs[3:]


def _band_specs(t, hb):
    prev = lambda i: jnp.maximum(i - 1, 0)
    return dict(
        cur=pl.BlockSpec((hb, t, HEAD_PAD), lambda h, i: (h, i, 0)),
        prev=pl.BlockSpec((hb, t, HEAD_PAD), lambda h, i: (h, prev(i), 0)),
        cur_t=pl.BlockSpec((hb, HEAD_PAD, t), lambda h, i: (h, 0, i)),
        prev_t=pl.BlockSpec((hb, HEAD_PAD, t), lambda h, i: (h, 0, prev(i))),
        stat=pl.BlockSpec((hb, 1, t), lambda h, i: (h, 0, i)),
        bias_prev=pl.BlockSpec((hb, 1, t, t), lambda h, i: (h, jnp.where(i == 0, 1, 0), 0, 0)),
        bias_cur=pl.BlockSpec((hb, 1, t, t), lambda h, i: (h, 2, 0, 0)))


def band_fwd(q, k, vt, bias):
    H, S, _ = q.shape
    t, hb = ATT_T, BAND_FWD_HEADS_PER_STEP
    sp = _band_specs(t, hb)

    def body(q_ref, kp_ref, kc_ref, vtp_ref, vtc_ref, bp_ref, bc_ref, o_ref, lse_ref):
        for h in range(hb):
            s0 = _nt(kp_ref[h], q_ref[h]) + bp_ref[h, 0]
            s1 = _nt(kc_ref[h], q_ref[h]) + bc_ref[h, 0]
            m = jnp.maximum(jnp.max(s0, axis=0, keepdims=True), jnp.max(s1, axis=0, keepdims=True))
            ot = (_nn(vtp_ref[h], jnp.exp2(s0 - m).astype(BF16)) +
                  _nn(vtc_ref[h], jnp.exp2(s1 - m).astype(BF16)))
            o_ref[h], lse_ref[h] = _finish_softmax(ot, m)

    return _call(
        body, name="band_fwd", grid=(H // hb, S // t),
        in_specs=[sp['cur'], sp['prev'], sp['cur'], sp['prev_t'], sp['cur_t'], sp['bias_prev'], sp['bias_cur']],
        out_specs=[sp['cur'], sp['stat']],
        out_shape=[jax.ShapeDtypeStruct((H, S, HEAD_PAD), BF16), jax.ShapeDtypeStruct((H, 1, S), F32)],
        compiler_params=_params(2))(q, k, k, vt, vt, bias, bias)


def band_bwd(q, k, kt, v, do, lse, delta, bias):
    H, S, _ = q.shape
    t = ATT_T
    sp = _band_specs(t, 1)

    def body(q_ref, kp_ref, kc_ref, ktp_ref, ktc_ref, vp_ref, vc_ref, do_ref, lse_ref, dl_ref, bp_ref, bc_ref,
             dq_ref, dk_ref, dv_ref, db_ref):
        i = pl.program_id(1)

        @pl.when(i == 0)
        def _():
            dk_ref[...] = jnp.zeros_like(dk_ref)
            dv_ref[...] = jnp.zeros_like(dv_ref)
            db_ref[...] = jnp.zeros_like(db_ref)

        qv, dov = q_ref[0], do_ref[0]
        dqt = jnp.zeros((HEAD_PAD, t), F32)
        windows = ((0, jnp.maximum(i - 1, 0), kp_ref, ktp_ref, vp_ref, bp_ref),
                   (1, i, kc_ref, ktc_ref, vc_ref, bc_ref))
        for w, blk, k_ref, kt_ref, v_ref, b_ref in windows:
            rows = pl.ds(pl.multiple_of(blk * t, t), t)
            pt = jnp.exp2(_nt(k_ref[0], qv) + b_ref[0, 0] - lse_ref[0])
            dv_ref[0, rows, :] += _nn(pt.astype(BF16), dov)
            ds = pt * (_nt(v_ref[0], dov) - dl_ref[0])
            db_ref[0, w] += ds
            dsb = ds.astype(BF16)
            dk_ref[0, rows, :] += _nn(dsb, qv)
            dqt = dqt + _nn(kt_ref[0], dsb)
        dq_ref[0] = dqt.T.astype(BF16)

    whole = pl.BlockSpec((1, S, HEAD_PAD), lambda h, i: (h, 0, 0))
    return _call(
        body, name="band_bwd", grid=(H, S // t),
        in_specs=[sp['cur'], sp['prev'], sp['cur'], sp['prev_t'], sp['cur_t'], sp['prev'], sp['cur'], sp['cur'],
                  sp['stat'], sp['stat'], sp['bias_prev'], sp['bias_cur']],
        out_specs=[sp['cur'], whole, whole, pl.BlockSpec((1, 2, t, t), lambda h, i: (h, 0, 0, 0))],
        out_shape=[jax.ShapeDtypeStruct((H, S, HEAD_PAD), BF16), jax.ShapeDtypeStruct((H, S, HEAD_PAD), F32),
                   jax.ShapeDtypeStruct((H, S, HEAD_PAD), F32), jax.ShapeDtypeStruct((H, 2, t, t), F32)],
        compiler_params=_params(2))(q, k, k, kt, kt, v, v, do, lse, delta, bias, bias)


def _compact(o_ref):
    return jnp.concatenate([o_ref[2 * p].astype(F32) + pltpu.roll(o_ref[2 * p + 1].astype(F32), 64, 1)
                            for p in range(4)], axis=1)


def merge_fwd(ob, oc, proj, ya, gate_b, wbr, w_out, x, post_g):
    S = x.shape[0]
    ts = ROW_TILE

    def body(ob_ref, oc_ref, zb_ref, zc_ref, ya_ref, gl_ref, gb_ref, wbr_ref, wo_ref, x_ref, pg_ref,
             xo_ref, yb_ref, yc_ref, mg_ref, out_ref):
        zb, zc = zb_ref[...], zc_ref[...]
        yb = (_compact(ob_ref) * (zb * _sigmoid(zb))).astype(BF16)
        yc = (_compact(oc_ref) * (zc * _sigmoid(zc))).astype(BF16)
        yb_ref[...] = yb
        yc_ref[...] = yc
        merged = jnp.zeros((ts, D_MODEL), F32)
        for n, y in enumerate((ya_ref[...], yb, yc)):
            cols = slice(n * D_MODEL, (n + 1) * D_MODEL)
            gate = _sigmoid(gl_ref[:, cols] + gb_ref[:, cols])
            merged = merged + gate * _nn(y, wbr_ref[n])
        mb = merged.astype(BF16)
        mg_ref[...] = mb
        out = _nn(mb, wo_ref[...])
        out_ref[...] = out
        normed, _ = _rms(out, pg_ref[...])
        xo_ref[...] = x_ref[...] + normed

    row = lambda w: pl.BlockSpec((ts, w), lambda i: (i, 0))
    col = lambda w, c: pl.BlockSpec((ts, w), lambda i: (i, c))
    full = lambda shape: pl.BlockSpec(shape, lambda i: (0,) * len(shape))
    return _call(
        body, name="merge_fwd", grid=(S // ts,),
        in_specs=[_hspec(ts), _hspec(ts), col(512, P_ZB // 512), col(512, P_ZC // 512), row(512),
                  col(3072, P_G // 3072), full((1, 3072)), full((3, BR_WIDTH, D_MODEL)),
                  full((D_MODEL, D_MODEL)), row(D_MODEL), full((1, D_MODEL))],
        out_specs=[row(D_MODEL), row(512), row(512), row(D_MODEL), row(D_MODEL)],
        out_shape=[jax.ShapeDtypeStruct((S, D_MODEL), F32), jax.ShapeDtypeStruct((S, 512), BF16),
                   jax.ShapeDtypeStruct((S, 512), BF16), jax.ShapeDtypeStruct((S, D_MODEL), BF16),
                   jax.ShapeDtypeStruct((S, D_MODEL), F32)],
        compiler_params=_params(1))(ob, oc, proj, proj, ya, proj, gate_b, wbr, w_out, x, post_g)


def gate_bwd(g, out, post_g, w_out, proj, gate_b, ya, yb, yc, wbr):
    S = g.shape[0]
    ts = ROW_TILE

    def body(g_ref, out_ref, pg_ref, wo_ref, gl_ref, gb_ref, ya_ref, yb_ref, yc_ref, wbr_ref,
             dp_ref, do_ref, dba_ref, dbb_ref, dbc_ref, dy_ref, ggb_ref, gp_ref):
        first = pl.program_id(0) == 0
        ov = out_ref[...]
        _, r = _rms(ov, pg_ref[...])
        dout, gp = _rms_bwd(ov, pg_ref[...], r, g_ref[...])
        db = dout.astype(BF16)
        do_ref[...] = db
        _accumulate(gp_ref, gp, first)
        dm = _nt(db, wo_ref[...])
        ggb = []
        for n, (y_ref, dbr_ref) in enumerate(((ya_ref, dba_ref), (yb_ref, dbb_ref), (yc_ref, dbc_ref))):
            cols = slice(n * D_MODEL, (n + 1) * D_MODEL)
            br = _nn(y_ref[...], wbr_ref[n])
            sg = _sigmoid(gl_ref[:, cols] + gb_ref[:, cols])
            dgl = dm * br * (sg * (1.0 - sg))
            dp_ref[:, cols] = dgl.astype(BF16)
            ggb.append(jnp.sum(dgl, axis=0, keepdims=True))
            dbr = (dm * sg).astype(BF16)
            dbr_ref[...] = dbr
            dy_ref[n] = _nt(dbr, wbr_ref[n])
        _accumulate(ggb_ref, jnp.concatenate(ggb, axis=1), first)

    row = lambda w: pl.BlockSpec((ts, w), lambda i: (i, 0))
    full = lambda shape: pl.BlockSpec(shape, lambda i: (0,) * len(shape))
    wide = jax.ShapeDtypeStruct((S, D_MODEL), BF16)
    return _call(
        body, name="gate_bwd", grid=(S // ts,),
        in_specs=[row(D_MODEL), row(D_MODEL), full((1, D_MODEL)), full((D_MODEL, D_MODEL)),
                  pl.BlockSpec((ts, 3072), lambda i: (i, P_G // 3072)), full((1, 3072)),
                  row(512), row(512), row(512), full((3, BR_WIDTH, D_MODEL))],
        out_specs=[pl.BlockSpec((ts, 3072), lambda i: (i, P_G // 3072)), row(D_MODEL), row(D_MODEL), row(D_MODEL),
                   row(D_MODEL), pl.BlockSpec((3, ts, 512), lambda i: (0, i, 0)), full((1, 3072)),
                   full((1, D_MODEL))],
        out_shape=[jax.ShapeDtypeStruct((S, P_W), BF16), wide, wide, wide, wide,
                   jax.ShapeDtypeStruct((3, S, 512), F32), jax.ShapeDtypeStruct((1, 3072), F32),
                   jax.ShapeDtypeStruct((1, D_MODEL), F32)],
        compiler_params=_params(1))(g, out, post_g, w_out, proj, gate_b, ya, yb, yc, wbr)


def ungate_bwd(dproj, dy, ob, oc, proj):
    S = proj.shape[0]
    ts = LIGHT_ROW_TILE

    def body(dp_in, dyb_ref, dyc_ref, ob_ref, oc_ref, zb_ref, zc_ref, dp_ref, dob_ref, doc_ref, dlb_ref, dlc_ref):
        del dp_in
        lane = lax.broadcasted_iota(jnp.int32, (ts, 128), 1)
        for n, (dy_ref, o_ref, z_ref, do_ref, dl_ref) in enumerate(
                ((dyb_ref, ob_ref, zb_ref, dob_ref, dlb_ref), (dyc_ref, oc_ref, zc_ref, doc_ref, dlc_ref))):
            zz = z_ref[...]
            dyv = dy_ref[0]
            sg = _sigmoid(zz)
            dp_ref[:, n * 512:(n + 1) * 512] = (dyv * _compact(o_ref) * (sg * (1.0 + zz * (1.0 - sg)))).astype(BF16)
            do_c = dyv * (zz * sg)
            for p in range(4):
                piece = do_c[:, p * 128:(p + 1) * 128]
                for h, d in ((2 * p, jnp.where(lane < 64, piece, 0.0)),
                             (2 * p + 1, jnp.where(lane < 64, pltpu.roll(piece, 64, 1), 0.0))):
                    do_ref[h] = d.astype(BF16)
                    dl_ref[h] = jnp.sum((d * o_ref[h].astype(F32)).T, axis=0, keepdims=True)

    col = lambda c: pl.BlockSpec((ts, 512), lambda i: (i, c))
    dysp = lambda n: pl.BlockSpec((1, ts, 512), lambda i: (n, i, 0))
    stat = pl.BlockSpec((N_HEADS, 1, ts), lambda i: (0, 0, i))
    hshape = jax.ShapeDtypeStruct((N_HEADS, S, HEAD_PAD), BF16)
    sshape = jax.ShapeDtypeStruct((N_HEADS, 1, S), F32)
    return _call(
        body, name="ungate_bwd", grid=(S // ts,),
        in_specs=[pl.BlockSpec(memory_space=pl.ANY), dysp(1), dysp(2), _hspec(ts), _hspec(ts),
                  col(P_ZB // 512), col(P_ZC // 512)],
        out_specs=[pl.BlockSpec((ts, 1024), lambda i: (i, P_ZB // 1024)), _hspec(ts), _hspec(ts), stat, stat],
        out_shape=[jax.ShapeDtypeStruct(dproj.shape, BF16), hshape, hshape, sshape, sshape],
        input_output_aliases={0: 0},
        compiler_params=_params(1))(dproj, dy, dy, ob, oc, proj, proj)


def loss_head(y, target):
    S, D = y.shape
    ts = LIGHT_ROW_TILE

    def body(y_ref, t_ref, dy_ref, sq_ref):
        d = y_ref[...] - t_ref[...]
        dy_ref[...] = d * (1.0 / D)
        _accumulate(sq_ref, jnp.sum(d * d, axis=0, keepdims=True), pl.program_id(0) == 0)

    row = pl.BlockSpec((ts, D), lambda i: (i, 0))
    return _call(
        body, name="loss_head", grid=(S // ts,), in_specs=[row, row],
        out_specs=[row, pl.BlockSpec((1, D), lambda i: (0, 0))],
        out_shape=[jax.ShapeDtypeStruct((S, D), F32), jax.ShapeDtypeStruct((1, D), F32)],
        compiler_params=_params(1))(y, target)


def _row_tile(rows, cols):
    for cand in (1024, 512, 256, 128, 64, 32, 16, 8):
        if rows % cand == 0 and cand * cols * 4 <= 1024 * 1024:
            return cand
    return rows


def adamw(w, grads, m, v):
    shape = w.shape
    cols = shape[-1]
    rows = int(np.prod(shape[:-1]))
    tr = _row_tile(rows, cols)
    n_g = len(grads)
    c1 = 1.0 - ADAM_B1 ** ADAM_STEP
    c2 = 1.0 - ADAM_B2 ** ADAM_STEP

    def body(*refs):
        w_ref, m_ref, v_ref = refs[:3]
        g_refs = refs[3:3 + n_g]
        go_ref, d_ref, mo_ref, vo_ref = refs[3 + n_g:]
        gv = g_refs[0][...]
        for g_ref in g_refs[1:]:
            gv = gv + g_ref[...]
        go_ref[...] = gv
        mn = ADAM_B1 * m_ref[...] + (1.0 - ADAM_B1) * gv
        vn = ADAM_B2 * v_ref[...] + (1.0 - ADAM_B2) * (gv * gv)
        mo_ref[...] = mn
        vo_ref[...] = vn
        d_ref[...] = -ADAM_LR * ((mn / c1) / (jnp.sqrt(vn / c2) + ADAM_EPS) + ADAM_WD * w_ref[...])

    blk = pl.BlockSpec((tr, cols), lambda i: (i, 0))
    sds = jax.ShapeDtypeStruct((rows, cols), F32)
    outs = _call(
        body, name="adamw", grid=(rows // tr,), in_specs=[blk] * (3 + n_g), out_specs=[blk] * 4,
        out_shape=[sds] * 4, compiler_params=_params(1))(
            *[a.reshape(rows, cols) for a in (w, m, v, *grads)])
    return [o.reshape(shape) for o in outs]


def add_lead(parts):
    n = parts.shape[0]
    shape = parts.shape[1:]
    cols = shape[-1]
    rows = int(np.prod(shape[:-1]))
    tr = _row_tile(rows, cols * n)

    def body(p_ref, o_ref):
        acc = p_ref[0].astype(F32)
        for s in range(1, n):
            acc = acc + p_ref[s].astype(F32)
        o_ref[...] = acc

    out = _call(
        body, name="add_lead", grid=(rows // tr,),
        in_specs=[pl.BlockSpec((n, tr, cols), lambda i: (0, i, 0))],
        out_specs=pl.BlockSpec((tr, cols), lambda i: (i, 0)),
        out_shape=jax.ShapeDtypeStruct((rows, cols), F32),
        compiler_params=_params(1))(parts.reshape(n, rows, cols))
    return out.reshape(shape)


ANY = pl.BlockSpec(memory_space=pl.ANY)


def _other_chips(x, y):
    return [(1 - x, y), (x, 1 - y), (1 - x, 1 - y)]


def chip_exchange(arrays, scatter, name):
    n = len(arrays)

    def body(*refs):
        _exchange_issue(refs[:n], refs[n:2 * n], refs[2 * n:], scatter, True)
        _exchange_issue(refs[:n], refs[n:2 * n], refs[2 * n:], scatter, False)

    return _call(
        body, name=name, in_specs=[ANY] * n, out_specs=[ANY] * n,
        out_shape=_exchange_out_shape(arrays, scatter), scratch_shapes=_exchange_sems(n))(*arrays)


def _exchange_out_shape(arrays, scatter):
    return [jax.ShapeDtypeStruct(a.shape if scatter else (4,) + a.shape, a.dtype) for a in arrays]


def _exchange_sems(n):
    if n == 0:
        return []
    return [pltpu.SemaphoreType.DMA((3 * n,)), pltpu.SemaphoreType.DMA((3 * n,)), pltpu.SemaphoreType.DMA((n,))]


def _exchange_issue(srcs, outs, sems, scatter, start):
    send_sems, recv_sems, local_sems = sems
    x, y, c = lax.axis_index("x"), lax.axis_index("y"), lax.axis_index("c")
    me = 2 * x + y
    for a in range(len(srcs)):
        local_src = srcs[a].at[me] if scatter else srcs[a]
        mine = pltpu.make_async_copy(local_src, outs[a].at[me], local_sems.at[a])
        sends = []
        for j, (px, py) in enumerate(_other_chips(x, y)):
            pair = dict(send_sem=send_sems.at[3 * a + j], recv_sem=recv_sems.at[3 * a + j],
                        device_id=(px, py, c), device_id_type=MESH)
            sends.append(pltpu.make_async_remote_copy(
                src_ref=srcs[a].at[2 * px + py] if scatter else srcs[a], dst_ref=outs[a].at[me], **pair))
            if not start:
                pltpu.make_async_remote_copy(src_ref=local_src, dst_ref=outs[a].at[2 * px + py], **pair).wait_recv()
        if start:
            mine.start()
            for cp in sends:
                cp.start()
        else:
            for cp in sends:
                cp.wait_send()
            mine.wait()


def gather_two_level(arrays):
    n = len(arrays)
    split = [a.shape[0] % 2 == 0 for a in arrays]

    def body(*refs):
        srcs, outs = refs[:n], refs[n:2 * n]
        send_sems, recv_sems, fwd_send, fwd_recv, local_sems = refs[2 * n:]
        x, y, c = lax.axis_index("x"), lax.axis_index("y"), lax.axis_index("c")
        me = 2 * x + y
        peers = _other_chips(x, y)

        def rows(a, half):
            h = arrays[a].shape[0] // 2
            return pl.ds(half * h, h)

        def over_ici(a, j, src, dst_chip, half):
            px, py = peers[j]
            part = (lambda r: r.at[rows(a, half)]) if split[a] else (lambda r: r)
            return pltpu.make_async_remote_copy(
                src_ref=part(src), dst_ref=part(outs[a].at[dst_chip]), send_sem=send_sems.at[3 * a + j],
                recv_sem=recv_sems.at[3 * a + j], device_id=(px, py, c), device_id_type=MESH)

        def over_d2d(a, j, half):
            px, py = peers[j]
            landed = outs[a].at[2 * px + py].at[rows(a, half)]
            return pltpu.make_async_remote_copy(
                src_ref=landed, dst_ref=landed, send_sem=fwd_send.at[3 * a + j], recv_sem=fwd_recv.at[3 * a + j],
                device_id=(x, y, 1 - c), device_id_type=MESH)

        own = [pltpu.make_async_copy(srcs[a], outs[a].at[me], local_sems.at[a]) for a in range(n)]
        sends = [over_ici(a, j, srcs[a], me, c) for a in range(n) for j in range(3)]
        for cp in own + sends:
            cp.start()
        forwards = []
        for a in range(n):
            for j, (px, py) in enumerate(peers):
                over_ici(a, j, srcs[a], 2 * px + py, c).wait_recv()
                if split[a]:
                    fwd = over_d2d(a, j, c)
                    fwd.start()
                    forwards.append(fwd)
        for a in range(n):
            if split[a]:
                for j in range(3):
                    over_d2d(a, j, 1 - c).wait_recv()
        for cp in sends + forwards:
            cp.wait_send()
        for cp in own:
            cp.wait()

    sems = [pltpu.SemaphoreType.DMA((3 * n,))] * 4 + [pltpu.SemaphoreType.DMA((n,))]
    return _call(
        body, name="gather_two_level", in_specs=[ANY] * n, out_specs=[ANY] * n,
        out_shape=_exchange_out_shape(arrays, False), scratch_shapes=sems)(*arrays)


def sibling_exchange(arrays):
    n = len(arrays)

    def body(*refs):
        srcs, outs = refs[:n], refs[n:2 * n]
        send_sems, recv_sems = refs[2 * n:]
        x, y, c = lax.axis_index("x"), lax.axis_index("y"), lax.axis_index("c")
        copies = [pltpu.make_async_remote_copy(src_ref=srcs[a], dst_ref=outs[a], send_sem=send_sems.at[a],
                                               recv_sem=recv_sems.at[a], device_id=(x, y, 1 - c), device_id_type=MESH)
                  for a in range(n)]
        for cp in copies:
            cp.start()
        for cp in copies:
            cp.wait()

    return _call(
        body, name="sibling_exchange", in_specs=[ANY] * n, out_specs=[ANY] * n,
        out_shape=[jax.ShapeDtypeStruct(a.shape, a.dtype) for a in arrays],
        scratch_shapes=[pltpu.SemaphoreType.DMA((n,)), pltpu.SemaphoreType.DMA((n,))])(*arrays)


def _perm_from_shards(sh):
    rows = sh.shape[1]
    pieces, pos = [], 0
    for lo, hi, plo in sorted(NAT_SEGS, key=lambda s: s[2]):
        if plo > pos:
            pieces.append(jnp.zeros((rows, plo - pos), sh.dtype))
            pos = plo
        c = lo
        while c < hi:
            kk = c // SHARD_COLS
            e = min(hi, (kk + 1) * SHARD_COLS)
            pieces.append(sh[kk][:, c - kk * SHARD_COLS:e - kk * SHARD_COLS])
            c = e
        pos += hi - lo
    if pos < P_W:
        pieces.append(jnp.zeros((rows, P_W - pos), sh.dtype))
    return jnp.concatenate(pieces, axis=1)


def _shards_from_perm(p):
    out = []
    for kk in range(4):
        lo_k, hi_k = kk * SHARD_COLS, (kk + 1) * SHARD_COLS
        pieces = []
        for lo, hi, plo in NAT_SEGS:
            a, b = max(lo, lo_k), min(hi, hi_k)
            if a < b:
                pieces.append(p[:, plo + (a - lo):plo + (b - lo)])
        out.append(jnp.concatenate(pieces, axis=1))
    return jnp.stack(out)


def _split4(a, axis):
    shape = a.shape
    a = a.reshape(shape[:axis] + (4, shape[axis] // 4) + shape[axis + 1:])
    return jnp.moveaxis(a, axis, 0)


def _join4(a, axis):
    a = jnp.moveaxis(a, 0, axis)
    shape = a.shape
    return a.reshape(shape[:axis] + (4 * shape[axis + 1],) + shape[axis + 2:])


def _pad_heads(w, per_head, lo, hi):
    r = w.shape[0]
    wh = w.reshape(r, N_HEADS, per_head)[:, :, lo:hi]
    return jnp.pad(wh, ((0, 0), (0, 0), (0, HEAD_PAD - (hi - lo)))).reshape(r, N_HEADS * HEAD_PAD)


def _rope_table(S):
    half = MLA_ROPE // 2
    inv = ROPE_BASE ** (-jnp.arange(half, dtype=F32) / half)
    ang = jnp.arange(S).astype(F32)[:, None] * inv[None, :]
    cos, sin = jnp.cos(ang), jnp.sin(ang)
    z = lambda n: jnp.zeros((S, n), F32)
    c = jnp.concatenate([jnp.ones((S, MLA_NOPE), F32), cos, cos, z(32)], axis=1)
    a = jnp.concatenate([z(MLA_NOPE), -sin, z(48)], axis=1)
    b = jnp.concatenate([z(MLA_NOPE + half), sin, z(32)], axis=1)
    return jnp.stack([c, a, b])


def _band_onehot():
    t = ATT_T
    m = np.arange(2 * t)
    d = np.where(m < t, m, m - 2 * t)
    idx = np.stack([np.clip(off + d, -REL_CLIP, REL_CLIP) + REL_CLIP for off in (t, 0)])
    return (idx[:, :, None] == np.arange(2 * REL_CLIP + 1)[None, None, :]).astype(np.float32)


def bias_expand(diag):
    t = ATT_T

    def body(d_ref, o_ref):
        kc = lax.broadcasted_iota(jnp.int32, (t, t), 0) >> CHUNK_SHIFT
        qc = lax.broadcasted_iota(jnp.int32, (t, t), 1) >> CHUNK_SHIFT
        for w, visible in ((0, kc >= qc), (1, kc <= qc)):
            rows = jnp.broadcast_to(d_ref[0, w:w + 1, :], (t, 2 * t))
            skew = pltpu.roll(rows, 0, 1, stride=1, stride_axis=0)[:, :t]
            o_ref[0, 2 * w] = jnp.where(visible, skew * LOG2E, NEG_INF)
        o_ref[0, 1] = jnp.full((t, t), NEG_INF, F32)

    return _call(
        body, name="bias_expand", grid=(N_HEADS,),
        in_specs=[pl.BlockSpec((1, 2, 2 * t), lambda h: (h, 0, 0))],
        out_specs=pl.BlockSpec((1, 3, t, t), lambda h: (h, 0, 0, 0)),
        out_shape=jax.ShapeDtypeStruct((N_HEADS, 3, t, t), F32),
        compiler_params=_params(1))(diag)


def bias_fold(dtiles):
    t = ATT_T

    def body(d_ref, o_ref):
        pad = jnp.zeros((8, t), F32)
        for w in range(2):
            acc = jnp.concatenate([d_ref[0, w, 0:8, :], pad], axis=1)
            for g in range(1, t // 8):
                grp = jnp.concatenate([d_ref[0, w, 8 * g:8 * g + 8, :], pad], axis=1)
                acc = acc + pltpu.roll(grp, 2 * t - 8 * g, 1)
            out = acc[0:1, :]
            for s in range(1, 8):
                out = out + pltpu.roll(acc, 2 * t - s, 1)[s:s + 1, :]
            o_ref[0, w:w + 1, :] = out

    return _call(
        body, name="bias_fold", grid=(N_HEADS,),
        in_specs=[pl.BlockSpec((1, 2, t, t), lambda h: (h, 0, 0, 0))],
        out_specs=pl.BlockSpec((1, 2, 2 * t), lambda h: (h, 0, 0)),
        out_shape=jax.ShapeDtypeStruct((N_HEADS, 2, 2 * t), F32),
        compiler_params=_params(1))(dtiles)


def _bias_tiles(table):
    diag = jnp.einsum('hr,wdr->hwd', table, jnp.asarray(_band_onehot()), precision=lax.Precision.HIGHEST)
    return bias_expand(diag)


def _bias_tiles_grad(dtiles):
    return jnp.einsum('hwd,wdr->hr', bias_fold(dtiles), jnp.asarray(_band_onehot()),
                      precision=lax.Precision.HIGHEST)


def _layer_consts(lw):
    tri = np.tril(np.ones((SGU_BLOCK, SGU_BLOCK), np.float32))
    ws = (lw['sgu_w'] * tri).astype(BF16)
    return dict(
        ws=ws, ws_t=jnp.swapaxes(ws, 1, 2), sgu_bias=jnp.repeat(lw['sgu_b'].T, CA_HEAD_DIM, axis=1),
        bias=_bias_tiles(lw['ca_rel_bias']),
        wq=_pad_heads(lw['mla_w_uq'], MLA_QK, 0, MLA_QK),
        wk=_pad_heads(lw['mla_w_ukv'], MLA_NOPE + MLA_V, 0, MLA_NOPE),
        wv=_pad_heads(lw['mla_w_ukv'], MLA_NOPE + MLA_V, MLA_NOPE, MLA_NOPE + MLA_V),
        gate_b=lw['gate_b'].reshape(1, 3 * D_MODEL),
        pre_g=lw['pre_g'][None], post_g=lw['post_g'][None], ln_g=lw['sgu_ln_g'][None], ln_b=lw['sgu_ln_b'][None],
        qg=lw['mla_q_norm_g'][None], kvg=lw['mla_kv_norm_g'][None])


def _layer_fwd(x, lw, k, tab, next_shards):
    proj, xnt = norm_matmul(x, k['pre_g'], lw['w_in'])
    ya = sgu_fwd(proj, k['ln_g'], k['ln_b'], k['ws'], k['sgu_bias'])
    qb, kb, vb, qc, kc, vc, kbt, vbt, kct, vct, cq, ckv = prep_fwd(proj, tab, k['qg'], k['kvg'], k['wq'], k['wk'],
                                                                   k['wv'])
    ob, lse_b, gathered = mla_fwd(qb, kb, vbt, next_shards)
    oc, lse_c = band_fwd(qc, kc, vct, k['bias'])
    x_new, yb, yc, merged, out = merge_fwd(ob, oc, proj, ya, k['gate_b'], lw['w_branch'], lw['w_out'], x,
                                           k['post_g'])
    saved = dict(x=x, proj=proj, xnt=xnt, ya=ya, yb=yb, yc=yc, qb=qb, kb=kb, vb=vb, qc=qc, kc=kc, vc=vc, kbt=kbt, kct=kct,
                 cq=cq, ckv=ckv, ob=ob, oc=oc, lse_b=lse_b, lse_c=lse_c, merged=merged, out=out)
    return x_new, saved, gathered


def _layer_bwd(g, s, lw, k, tab, pending_parts, scatter_own):
    S = g.shape[0]
    H = N_HEADS
    dproj, dout, dba, dbb, dbc, dy, g_gate_b, g_post = gate_bwd(
        g, s['out'], k['post_g'], lw['w_out'], s['proj'], k['gate_b'], s['ya'], s['yb'], s['yc'], lw['w_branch'])
    g_w_out = matmul_tn(s['merged'], dout, D_MODEL)
    g_w_branch = jnp.stack([matmul_tn(y, d, D_MODEL) for y, d in ((s['ya'], dba), (s['yb'], dbb), (s['yc'], dbc))])
    dproj, dob, doc, dl_b, dl_c = ungate_bwd(dproj, dy, s['ob'], s['oc'], s['proj'])
    dqb, dkb, dvb, landed = mla_bwd(s['qb'], s['kb'], s['kbt'], s['vb'], dob, s['lse_b'], dl_b, pending_parts)
    dqc, dkc, dvc, dbias = band_bwd(s['qc'], s['kc'], s['kct'], s['vc'], doc, s['lse_c'], dl_c, k['bias'])
    dproj, dqf, dkf, dvf, g_qg, g_kvg = prep_bwd(dproj, dqb, dkb, dvb, dqc, dkc, dvc, s['proj'], tab,
                                                 k['qg'], k['kvg'], k['wq'], k['wk'], k['wv'])
    wide = N_HEADS * HEAD_PAD
    g_wq = matmul_tn(s['cq'], dqf, wide).reshape(MLA_Q_RANK, H, HEAD_PAD)[:, :, :MLA_QK]
    g_wk = matmul_tn(s['ckv'], dkf, wide).reshape(MLA_KV_RANK, H, HEAD_PAD)[:, :, :MLA_NOPE]
    g_wv = matmul_tn(s['ckv'], dvf, wide).reshape(MLA_KV_RANK, H, HEAD_PAD)[:, :, :MLA_V]
    dproj, g_ln_g, g_ln_b, g_ws, g_sgu_bias = sgu_bwd(dproj, dy, s['proj'], k['ln_g'], k['ln_b'], k['ws'],
                                                      k['ws_t'], k['sgu_bias'])
    g_w_in = matmul_acc(s['xnt'], dproj, MM_TN)
    sharded = _sharded_parts(dict(
        w_in=g_w_in, mla_w_uq=g_wq.reshape(MLA_Q_RANK, H * MLA_QK),
        mla_w_ukv=jnp.concatenate([g_wk, g_wv], axis=2).reshape(MLA_KV_RANK, H * (MLA_NOPE + MLA_V)),
        w_branch=g_w_branch, gate_b=g_gate_b.reshape(N_BRANCH, D_MODEL), w_out=g_w_out))
    dx, g_pre, own_landed = proj_bwd_x(dproj, lw['w_in'], s['x'], k['pre_g'], g, sharded if scatter_own else ())
    tri = np.tril(np.ones((SGU_BLOCK, SGU_BLOCK), np.float32))
    small = _small_pack(dict(
        pre_g=g_pre[0], post_g=g_post[0], sgu_ln_g=g_ln_g[0], sgu_ln_b=g_ln_b[0],
        sgu_w=g_ws * tri, sgu_b=jnp.sum(g_sgu_bias.reshape(SGU_BLOCK, 8, CA_HEAD_DIM), axis=2).T,
        mla_q_norm_g=g_qg[0], mla_kv_norm_g=g_kvg[0], ca_rel_bias=_bias_tiles_grad(dbias)))
    return dx, (own_landed if scatter_own else sharded), small, landed


BF16_PARTS = ('w_in', 'mla_w_uq', 'mla_w_ukv', 'w_branch', 'w_out')


def _weight_shards(w, l):
    return [w[n][l].astype(BF16) if n in BF16_PARTS else w[n][l] for n in SHARDED]


def _full_weights(gathered, small):
    lw = {n: _join4(a, SHARD_AXIS[n]) for n, a in zip(SHARDED, gathered) if n != 'w_in'}
    lw['w_in'] = _perm_from_shards(gathered[0])
    lw.update(small)
    return lw


def _small_pack(grads):
    flat = jnp.concatenate([grads[n].reshape(-1) for n in SMALL])
    quarter = -(-flat.size // (4 * 1024)) * 1024
    return jnp.pad(flat, (0, 4 * quarter - flat.size)).reshape(4, quarter // 128, 128)


def _sharded_parts(grads):
    parts = [_shards_from_perm(grads['w_in'])]
    parts += [_split4(grads[n], SHARD_AXIS[n]) for n in SHARDED if n != 'w_in']
    return [p.astype(BF16) if n in BF16_PARTS else p for n, p in zip(SHARDED, parts)]


def train_step_local(x, target, w):
    S = x.shape[0]
    depth = w['w_in'].shape[0]
    tab = _rope_table(S)
    gathered = gather_two_level(_weight_shards(w, 0))
    layer_w, consts, saved = [], [], []
    for l in range(depth):
        lw = _full_weights(gathered, {n: w[n][l] for n in SMALL})
        k = _layer_consts(lw)
        x, s, gathered = _layer_fwd(x, lw, k, tab, _weight_shards(w, l + 1) if l + 1 < depth else ())
        layer_w.append(lw)
        consts.append(k)
        saved.append(s)
    g, sq = loss_head(x, target)
    mine = [None] * depth
    pending = ()
    for l in reversed(range(depth)):
        g, sharded, small, landed = _layer_bwd(g, saved[l], layer_w[l], consts[l], tab, pending, l == 0)
        if pending:
            mine[l + 1] = [add_lead(p) for p in landed]
        pending = list(sharded) + [small]
    mine[0] = [add_lead(p) for p in pending[:-1] + list(chip_exchange(pending[-1:], True, "scatter_small"))]
    n_parts = len(mine[0])
    theirs = sibling_exchange([p for layer in mine for p in layer])
    return sq, g, [(mine[l], theirs[l * n_parts:(l + 1) * n_parts]) for l in range(depth)]


def kernel(x, w_in, pre_g, post_g, sgu_ln_g, sgu_ln_b, sgu_w, sgu_b, mla_q_norm_g, mla_kv_norm_g, mla_w_uq, mla_w_ukv, ca_rel_bias, w_branch, gate_b, w_out, loss_target, m_w_in, m_pre_g, m_post_g, m_sgu_ln_g, m_sgu_ln_b, m_sgu_w, m_sgu_b, m_mla_q_norm_g, m_mla_kv_norm_g, m_mla_w_uq, m_mla_w_ukv, m_ca_rel_bias, m_w_branch, m_gate_b, m_w_out, v_w_in, v_pre_g, v_post_g, v_sgu_ln_g, v_sgu_ln_b, v_sgu_w, v_sgu_b, v_mla_q_norm_g, v_mla_kv_norm_g, v_mla_w_uq, v_mla_w_ukv, v_ca_rel_bias, v_w_branch, v_gate_b, v_w_out):
    w = dict(w_in=w_in, pre_g=pre_g, post_g=post_g, sgu_ln_g=sgu_ln_g, sgu_ln_b=sgu_ln_b, sgu_w=sgu_w, sgu_b=sgu_b,
             mla_q_norm_g=mla_q_norm_g, mla_kv_norm_g=mla_kv_norm_g, mla_w_uq=mla_w_uq, mla_w_ukv=mla_w_ukv,
             ca_rel_bias=ca_rel_bias, w_branch=w_branch, gate_b=gate_b, w_out=w_out)
    m = dict(w_in=m_w_in, pre_g=m_pre_g, post_g=m_post_g, sgu_ln_g=m_sgu_ln_g, sgu_ln_b=m_sgu_ln_b, sgu_w=m_sgu_w,
             sgu_b=m_sgu_b, mla_q_norm_g=m_mla_q_norm_g, mla_kv_norm_g=m_mla_kv_norm_g, mla_w_uq=m_mla_w_uq,
             mla_w_ukv=m_mla_w_ukv, ca_rel_bias=m_ca_rel_bias, w_branch=m_w_branch, gate_b=m_gate_b, w_out=m_w_out)
    v = dict(w_in=v_w_in, pre_g=v_pre_g, post_g=v_post_g, sgu_ln_g=v_sgu_ln_g, sgu_ln_b=v_sgu_ln_b, sgu_w=v_sgu_w,
             sgu_b=v_sgu_b, mla_q_norm_g=v_mla_q_norm_g, mla_kv_norm_g=v_mla_kv_norm_g, mla_w_uq=v_mla_w_uq,
             mla_w_ukv=v_mla_w_ukv, ca_rel_bias=v_ca_rel_bias, w_branch=v_w_branch, gate_b=v_gate_b, w_out=v_w_out)
    depth = w_in.shape[0]
    sq, grad_x, reduced = train_step_local(x[0], loss_target[0], w)
    loss = lax.psum(0.5 * jnp.sum(sq) / D_MODEL, ("x", "y", "c"))

    out = {}
    for a, n in enumerate(SHARDED):
        mine = jnp.stack([reduced[l][0][a] for l in range(depth)])
        theirs = jnp.stack([reduced[l][1][a] for l in range(depth)])
        out[n] = adamw(w[n], [mine, theirs], m[n], v[n])
    small = jnp.stack([jnp.stack([reduced[l][0][-1] for l in range(depth)]),
                       jnp.stack([reduced[l][1][-1] for l in range(depth)])])
    quarter = add_lead(small)
    full = chip_exchange([quarter], False, "gather_small")[0]
    full = jnp.moveaxis(full, 0, 1).reshape(depth, -1)
    off = 0
    for n in SMALL:
        size = int(np.prod(w[n].shape[1:]))
        out[n] = adamw(w[n], [full[:, off:off + size].reshape(w[n].shape)], m[n], v[n])
        off += size
    return (loss, grad_x[None], *[out[n][0] for n in WEIGHTS], *[out[n][1] for n in WEIGHTS],
            *[out[n][2] for n in WEIGHTS], *[out[n][3] for n in WEIGHTS])
```

```python
import numpy as np
import jax
import jax.numpy as jnp
from jax import lax
from jax.experimental import pallas as pl
from jax.experimental.pallas import tpu as pltpu

F32 = jnp.float32
BF16 = jnp.bfloat16
MESH = pl.DeviceIdType.MESH

EPS = 1e-6
NEG_INF = -1e30
D_MODEL = 1024
BR_WIDTH = 512
N_BRANCH = 3
N_HEADS = 8
HEAD_PAD = 128
CHUNK_SHIFT = 6
SGU_BLOCK = 128
MLA_NOPE, MLA_ROPE, MLA_V = 64, 32, 64
MLA_QK = MLA_NOPE + MLA_ROPE
MLA_Q_RANK, MLA_KV_RANK = 256, 128
CA_HEAD_DIM = 64
REL_CLIP = 128
ROPE_BASE = 10000.0
D_IN = 7584

ADAM_LR, ADAM_B1, ADAM_B2, ADAM_EPS, ADAM_WD, ADAM_STEP = 0.001, 0.9, 0.999, 1e-08, 0.01, 10

P_QC, P_KC, P_VC, P_QD, P_KVD, P_KR, P_ZB, P_ZC, P_G, P_U, P_V, P_ZA, P_W = (
    0, 512, 1024, 1536, 1792, 1920, 2048, 2560, 3072, 6144, 6656, 7168, 7680)
NAT_SEGS = [(0, 1536, P_U), (1536, 1920, P_QD), (1920, 1952, P_KR + MLA_NOPE), (1952, 2464, P_ZB),
            (2464, 4000, P_QC), (4000, 4512, P_ZC), (4512, 7584, P_G)]
SHARD_COLS = D_IN // 4

VMEM_LIMIT = 48 * 1024 * 1024
ATT_T = 512
BAND_FWD_HEADS_PER_STEP = 4
MLA_FWD_HEADS_PER_STEP = 8
MLA_BWD_HEADS_PER_STEP = 4
ROW_TILE = 256
LIGHT_ROW_TILE = 512
MM_TM = 512
MM_TN = 1536
LOG2E = 1.4426950408889634
MLA_SCALE = MLA_QK ** -0.5
CA_SCALE = CA_HEAD_DIM ** -0.5

WEIGHTS = ['w_in', 'pre_g', 'post_g', 'sgu_ln_g', 'sgu_ln_b', 'sgu_w', 'sgu_b', 'mla_q_norm_g',
           'mla_kv_norm_g', 'mla_w_uq', 'mla_w_ukv', 'ca_rel_bias', 'w_branch', 'gate_b', 'w_out']
SHARDED = ['w_in', 'mla_w_uq', 'mla_w_ukv', 'w_branch', 'gate_b', 'w_out']
SMALL = ['pre_g', 'post_g', 'sgu_ln_g', 'sgu_ln_b', 'sgu_w', 'sgu_b', 'mla_q_norm_g',
         'mla_kv_norm_g', 'ca_rel_bias']
SHARD_AXIS = {'w_in': 1, 'mla_w_uq': 1, 'mla_w_ukv': 1, 'w_branch': 2, 'gate_b': 1, 'w_out': 0}


def _call(body, **kw):
    return pl.pallas_call(body, **kw)


def _params(n_axes):
    return pltpu.CompilerParams(dimension_semantics=("arbitrary",) * n_axes,
                                vmem_limit_bytes=VMEM_LIMIT)


def _nt(a, b):
    return lax.dot_general(a, b, (((1,), (1,)), ((), ())), preferred_element_type=F32)


def _nn(a, b):
    return jnp.dot(a, b, preferred_element_type=F32)


def _tn(a, b):
    return lax.dot_general(a, b, (((0,), (0,)), ((), ())), preferred_element_type=F32)


def _rms(xv, g):
    r = lax.rsqrt(jnp.mean(xv * xv, axis=-1, keepdims=True) + EPS)
    return xv * r * g, r


def _rms_bwd(xv, g, r, dy):
    gy = dy * g
    dx = r * gy - xv * (r * r * r) * jnp.mean(xv * gy, axis=-1, keepdims=True)
    dg = jnp.sum(dy * (xv * r), axis=0, keepdims=True)
    return dx, dg


def _sigmoid(z):
    return 1.0 / (1.0 + jnp.exp(-z))


def _rope(xv, c, a, b):
    return xv * c + pltpu.roll(xv, 112, 1) * a + pltpu.roll(xv, 16, 1) * b


def _accumulate(ref, val, first):
    @pl.when(first)
    def _():
        ref[...] = val

    @pl.when(jnp.logical_not(first))
    def _():
        ref[...] += val


def norm_matmul(x, g, w):
    S, D = x.shape
    N = w.shape[1]
    tm, tn = min(S, 2 * MM_TM), MM_TN

    def body(x_ref, g_ref, w_ref, o_ref, xnt_ref, xn_s):
        @pl.when(pl.program_id(1) == 0)
        def _():
            y, _ = _rms(x_ref[...], g_ref[...])
            xn_s[...] = y.astype(BF16)
            xnt_ref[...] = y.T.astype(BF16)

        o_ref[...] = _nn(xn_s[...], w_ref[...])

    return _call(
        body, name="norm_matmul", grid=(S // tm, N // tn),
        in_specs=[pl.BlockSpec((tm, D), lambda i, j: (i, 0)),
                  pl.BlockSpec((1, D), lambda i, j: (0, 0)),
                  pl.BlockSpec((D, tn), lambda i, j: (0, j))],
        out_specs=[pl.BlockSpec((tm, tn), lambda i, j: (i, j)),
                   pl.BlockSpec((D, tm), lambda i, j: (0, i))],
        out_shape=[jax.ShapeDtypeStruct((S, N), F32), jax.ShapeDtypeStruct((D, S), BF16)],
        scratch_shapes=[pltpu.VMEM((tm, D), BF16)],
        compiler_params=_params(2))(x, g, w)


def matmul_acc(a, b, tn):
    M, S = a.shape
    N = b.shape[1]
    tk = min(S, 2 * MM_TM)

    def body(a_ref, b_ref, o_ref):
        @pl.when(pl.program_id(1) == 0)
        def _():
            o_ref[...] = jnp.zeros_like(o_ref)

        o_ref[...] += _nn(a_ref[...], b_ref[...])

    return _call(
        body, name="matmul_acc", grid=(N // tn, S // tk),
        in_specs=[pl.BlockSpec((M, tk), lambda j, k: (0, k)),
                  pl.BlockSpec((tk, tn), lambda j, k: (k, j))],
        out_specs=pl.BlockSpec((M, tn), lambda j, k: (0, j)),
        out_shape=jax.ShapeDtypeStruct((M, N), F32),
        compiler_params=_params(2))(a, b)


def proj_bwd_x(dproj, w, x, g, resid, exchange=()):
    S, N = dproj.shape
    D = x.shape[1]
    tm, tk = min(S, MM_TM), MM_TN
    nk, n_ex = N // tk, len(exchange)

    def body(dp_ref, w_ref, x_ref, g_ref, r_ref, *rest):
        ex_src, (dx_ref, dg_ref), ex_out, (acc_ref,), ex_sems = _split_refs(
            rest, (n_ex, 2, n_ex, 1, 3 if n_ex else 0))
        i, k = pl.program_id(0), pl.program_id(1)

        if n_ex:
            @pl.when(jnp.logical_and(i == 0, k == 0))
            def _():
                _exchange_issue(ex_src, ex_out, ex_sems, True, True)

        @pl.when(k == 0)
        def _():
            acc_ref[...] = jnp.zeros_like(acc_ref)

        acc_ref[...] += _nt(dp_ref[...].astype(BF16), w_ref[...])

        @pl.when(k == nk - 1)
        def _():
            xv = x_ref[...]
            _, r = _rms(xv, g_ref[...])
            dx, dg = _rms_bwd(xv, g_ref[...], r, acc_ref[...])
            dx_ref[...] = dx + r_ref[...]
            _accumulate(dg_ref, dg, i == 0)

        if n_ex:
            @pl.when(jnp.logical_and(i == S // tm - 1, k == nk - 1))
            def _():
                _exchange_issue(ex_src, ex_out, ex_sems, True, False)

    outs = _call(
        body, name="proj_bwd_x_scatter" if n_ex else "proj_bwd_x", grid=(S // tm, nk),
        in_specs=[pl.BlockSpec((tm, tk), lambda i, k: (i, k)),
                  pl.BlockSpec((D, tk), lambda i, k: (0, k)),
                  pl.BlockSpec((tm, D), lambda i, k: (i, 0)),
                  pl.BlockSpec((1, D), lambda i, k: (0, 0)),
                  pl.BlockSpec((tm, D), lambda i, k: (i, 0))] + [ANY] * n_ex,
        out_specs=[pl.BlockSpec((tm, D), lambda i, k: (i, 0)),
                   pl.BlockSpec((1, D), lambda i, k: (0, 0))] + [ANY] * n_ex,
        out_shape=[jax.ShapeDtypeStruct((S, D), F32), jax.ShapeDtypeStruct((1, D), F32)] +
        _exchange_out_shape(exchange, True),
        scratch_shapes=[pltpu.VMEM((tm, D), F32)] + _exchange_sems(n_ex),
        compiler_params=_params(2))(dproj, w, x, g, resid, *exchange)
    return outs[0], outs[1], outs[2:]


def matmul_tn(a, b, tn):
    S, M = a.shape
    N = b.shape[1]
    tk = min(S, 2 * MM_TM)

    def body(a_ref, b_ref, o_ref):
        @pl.when(pl.program_id(1) == 0)
        def _():
            o_ref[...] = jnp.zeros_like(o_ref)

        o_ref[...] += _tn(a_ref[...].astype(BF16), b_ref[...].astype(BF16))

    return _call(
        body, name="matmul_tn", grid=(N // tn, S // tk),
        in_specs=[pl.BlockSpec((tk, M), lambda j, k: (k, 0)),
                  pl.BlockSpec((tk, tn), lambda j, k: (k, j))],
        out_specs=pl.BlockSpec((M, tn), lambda j, k: (0, j)),
        out_shape=jax.ShapeDtypeStruct((M, N), F32),
        compiler_params=_params(2))(a, b)


def _sgu_block(vv, g, b, ws_ref, lane):
    mu = jnp.mean(vv, axis=-1, keepdims=True)
    xc = vv - mu
    r = lax.rsqrt(jnp.mean(xc * xc, axis=-1, keepdims=True) + EPS)
    xhat = xc * r
    vln = (xhat * g + b).astype(BF16)
    pieces = []
    for p in range(4):
        vp = vln[:, p * 128:(p + 1) * 128]
        pieces.append(jnp.where(lane < 64, _nn(ws_ref[2 * p], vp), _nn(ws_ref[2 * p + 1], vp)))
    return xhat, r, vln, jnp.concatenate(pieces, axis=1)


def sgu_fwd(proj, ln_g, ln_b, ws, bias_full):
    S = proj.shape[0]
    ts = LIGHT_ROW_TILE

    def body(u_ref, v_ref, z_ref, g_ref, b_ref, ws_ref, bf_ref, y_ref):
        lane = lax.broadcasted_iota(jnp.int32, (SGU_BLOCK, 128), 1)
        for blk in range(ts // SGU_BLOCK):
            rows = slice(blk * SGU_BLOCK, (blk + 1) * SGU_BLOCK)
            _, _, _, mixed = _sgu_block(v_ref[rows, :], g_ref[...], b_ref[...], ws_ref, lane)
            mixed = mixed + bf_ref[...]
            zz = z_ref[rows, :]
            y_ref[rows, :] = (u_ref[rows, :] * mixed * (zz * _sigmoid(zz))).astype(BF16)

    col = lambda c: pl.BlockSpec((ts, BR_WIDTH), lambda i: (i, c))
    full = lambda shape: pl.BlockSpec(shape, lambda i: (0,) * len(shape))
    return _call(
        body, name="sgu_fwd", grid=(S // ts,),
        in_specs=[col(P_U // 512), col(P_V // 512), col(P_ZA // 512),
                  full((1, BR_WIDTH)), full((1, BR_WIDTH)), full((8, 128, 128)), full((128, BR_WIDTH))],
        out_specs=pl.BlockSpec((ts, BR_WIDTH), lambda i: (i, 0)),
        out_shape=jax.ShapeDtypeStruct((S, BR_WIDTH), BF16),
        compiler_params=_params(1))(proj, proj, proj, ln_g, ln_b, ws, bias_full)


def sgu_bwd(dproj, dy, proj, ln_g, ln_b, ws, ws_t, bias_full):
    S = proj.shape[0]
    ts = LIGHT_ROW_TILE

    def body(dp_in, dy_ref, u_ref, v_ref, z_ref, g_ref, b_ref, ws_ref, wst_ref, bf_ref,
             dp_ref, gg_ref, gb_ref, gws_ref, gbf_ref):
        del dp_in
        first = pl.program_id(0) == 0

        @pl.when(first)
        def _():
            gg_ref[...] = jnp.zeros_like(gg_ref)
            gb_ref[...] = jnp.zeros_like(gb_ref)
            gws_ref[...] = jnp.zeros_like(gws_ref)
            gbf_ref[...] = jnp.zeros_like(gbf_ref)

        lane = lax.broadcasted_iota(jnp.int32, (SGU_BLOCK, 128), 1)
        for blk in range(ts // SGU_BLOCK):
            rows = slice(blk * SGU_BLOCK, (blk + 1) * SGU_BLOCK)
            g = g_ref[...]
            xhat, r, vln, mixed = _sgu_block(v_ref[rows, :], g, b_ref[...], ws_ref, lane)
            mixed = mixed + bf_ref[...]
            zz = z_ref[rows, :]
            uu = u_ref[rows, :]
            dyv = dy_ref[0, rows, :]
            sg = _sigmoid(zz)
            sil = zz * sg
            dmixed = dyv * uu * sil
            dp_ref[rows, 0:512] = (dyv * mixed * sil).astype(BF16)
            dp_ref[rows, 1024:1536] = (dyv * uu * mixed * (sg * (1.0 + zz * (1.0 - sg)))).astype(BF16)
            gbf_ref[...] += dmixed
            dmb = dmixed.astype(BF16)
            pieces = []
            for p in range(4):
                dmp = dmb[:, p * 128:(p + 1) * 128]
                vp = vln[:, p * 128:(p + 1) * 128]
                pieces.append(jnp.where(lane < 64, _nn(wst_ref[2 * p], dmp), _nn(wst_ref[2 * p + 1], dmp)))
                zero = jnp.zeros_like(dmp)
                gws_ref[2 * p] += _nt(jnp.where(lane < 64, dmp, zero), vp)
                gws_ref[2 * p + 1] += _nt(jnp.where(lane >= 64, dmp, zero), vp)
            dvln = jnp.concatenate(pieces, axis=1)
            dxh = dvln * g
            dp_ref[rows, 512:1024] = (r * (dxh - jnp.mean(dxh, axis=-1, keepdims=True)
                                           - xhat * jnp.mean(dxh * xhat, axis=-1, keepdims=True))).astype(BF16)
            gg_ref[...] += jnp.sum(dvln * xhat, axis=0, keepdims=True)
            gb_ref[...] += jnp.sum(dvln, axis=0, keepdims=True)

    col = lambda c: pl.BlockSpec((ts, BR_WIDTH), lambda i: (i, c))
    full = lambda shape: pl.BlockSpec(shape, lambda i: (0,) * len(shape))
    return _call(
        body, name="sgu_bwd", grid=(S // ts,),
        in_specs=[pl.BlockSpec(memory_space=pl.ANY),
                  pl.BlockSpec((1, ts, BR_WIDTH), lambda i: (0, i, 0)),
                  col(P_U // 512), col(P_V // 512), col(P_ZA // 512),
                  full((1, BR_WIDTH)), full((1, BR_WIDTH)), full((8, 128, 128)), full((8, 128, 128)),
                  full((128, BR_WIDTH))],
        out_specs=[pl.BlockSpec((ts, 1536), lambda i: (i, P_U // 1536)),
                   full((1, BR_WIDTH)), full((1, BR_WIDTH)), full((8, 128, 128)), full((128, BR_WIDTH))],
        out_shape=[jax.ShapeDtypeStruct(dproj.shape, BF16),
                   jax.ShapeDtypeStruct((1, BR_WIDTH), F32), jax.ShapeDtypeStruct((1, BR_WIDTH), F32),
                   jax.ShapeDtypeStruct((8, 128, 128), F32), jax.ShapeDtypeStruct((128, BR_WIDTH), F32)],
        input_output_aliases={0: 0},
        compiler_params=_params(1))(dproj, dy, proj, proj, proj, ln_g, ln_b, ws, ws_t, bias_full)


def _hspec(ts):
    return pl.BlockSpec((N_HEADS, ts, HEAD_PAD), lambda i: (0, i, 0))


def prep_fwd(proj, tab, qg, kvg, wq, wk, wv):
    S = proj.shape[0]
    ts = LIGHT_ROW_TILE

    def body(qc_ref, kc_ref, vc_ref, qd_ref, kvd_ref, kr_ref, tab_ref, qg_ref, kvg_ref,
             wq_ref, wk_ref, wv_ref, qb, kb, vb, qc, kc, vc, kbt, vbt, kct, vct, cq_o, ckv_o):
        c, a, b = tab_ref[0], tab_ref[1], tab_ref[2]
        cq, _ = _rms(qd_ref[...], qg_ref[...])
        ckv, _ = _rms(kvd_ref[...], kvg_ref[...])
        cqb, ckvb = cq.astype(BF16), ckv.astype(BF16)
        cq_o[...] = cqb
        ckv_o[...] = ckvb
        krr = _rope(kr_ref[...], c, a, b)
        lane = lax.broadcasted_iota(jnp.int32, (ts, 128), 1)
        ones_lane = jnp.where(lane == MLA_V, 1.0, 0.0)
        for h in range(N_HEADS):
            cols = slice(h * HEAD_PAD, (h + 1) * HEAD_PAD)
            qb[h] = (_rope(_nn(cqb, wq_ref[:, cols]), c, a, b) * (MLA_SCALE * LOG2E)).astype(BF16)
            kh = _nn(ckvb, wk_ref[:, cols]) + krr
            vh = _nn(ckvb, wv_ref[:, cols]) + ones_lane
            kb[h], kbt[h] = kh.astype(BF16), kh.T.astype(BF16)
            vb[h], vbt[h] = vh.astype(BF16), vh.T.astype(BF16)
        for p in range(4):
            piece = qc_ref[:, p * 128:(p + 1) * 128] * (CA_SCALE * LOG2E)
            qc[2 * p] = jnp.where(lane < 64, piece, 0.0).astype(BF16)
            qc[2 * p + 1] = jnp.where(lane < 64, pltpu.roll(piece, 64, 1), 0.0).astype(BF16)
            for src, dst, dst_t, pad in ((kc_ref, kc, kct, 0.0), (vc_ref, vc, vct, ones_lane)):
                piece = src[:, p * 128:(p + 1) * 128]
                for h, head in ((2 * p, jnp.where(lane < 64, piece, pad)),
                                (2 * p + 1, jnp.where(lane < 64, pltpu.roll(piece, 64, 1), pad))):
                    dst[h], dst_t[h] = head.astype(BF16), head.T.astype(BF16)

    col = lambda w, c: pl.BlockSpec((ts, w), lambda i: (i, c))
    full = lambda shape: pl.BlockSpec(shape, lambda i: (0,) * len(shape))
    hshape = jax.ShapeDtypeStruct((N_HEADS, S, HEAD_PAD), BF16)
    tshape = jax.ShapeDtypeStruct((N_HEADS, HEAD_PAD, S), BF16)
    tspec = pl.BlockSpec((N_HEADS, HEAD_PAD, ts), lambda i: (0, 0, i))
    return _call(
        body, name="prep_fwd", grid=(S // ts,),
        in_specs=[col(512, P_QC // 512), col(512, P_KC // 512), col(512, P_VC // 512),
                  col(256, P_QD // 256), col(128, P_KVD // 128), col(128, P_KR // 128),
                  pl.BlockSpec((3, ts, 128), lambda i: (0, i, 0)),
                  full((1, MLA_Q_RANK)), full((1, MLA_KV_RANK)),
                  full((MLA_Q_RANK, 1024)), full((MLA_KV_RANK, 1024)), full((MLA_KV_RANK, 1024))],
        out_specs=[_hspec(ts)] * 6 + [tspec] * 4 + [pl.BlockSpec((ts, MLA_Q_RANK), lambda i: (i, 0)),
                                                    pl.BlockSpec((ts, MLA_KV_RANK), lambda i: (i, 0))],
        out_shape=[hshape] * 6 + [tshape] * 4 + [jax.ShapeDtypeStruct((S, MLA_Q_RANK), BF16),
                                                 jax.ShapeDtypeStruct((S, MLA_KV_RANK), BF16)],
        compiler_params=_params(1))(proj, proj, proj, proj, proj, proj, tab, qg, kvg, wq, wk, wv)


def prep_bwd(dproj, dqb, dkb, dvb, dqc, dkc, dvc, proj, tab, qg, kvg, wq, wk, wv):
    S = proj.shape[0]
    ts = LIGHT_ROW_TILE

    def body(dp_in, dqb_r, dkb_r, dvb_r, dqc_r, dkc_r, dvc_r, qd_ref, kvd_ref, tab_ref, qg_ref, kvg_ref,
             wq_ref, wk_ref, wv_ref, dp_ref, dqf, dkf, dvf, gq_ref, gkv_ref):
        del dp_in
        c, a, b = tab_ref[0], -tab_ref[1], -tab_ref[2]
        qd, kvd = qd_ref[...], kvd_ref[...]
        _, rq = _rms(qd, qg_ref[...])
        _, rkv = _rms(kvd, kvg_ref[...])
        dcq = jnp.zeros((ts, MLA_Q_RANK), F32)
        dckv = jnp.zeros((ts, MLA_KV_RANK), F32)
        dksum = jnp.zeros((ts, HEAD_PAD), F32)
        for h in range(N_HEADS):
            cols = slice(h * HEAD_PAD, (h + 1) * HEAD_PAD)
            dqh = _rope(dqb_r[h].astype(F32) * MLA_SCALE, c, a, b).astype(BF16)
            dqf[:, cols] = dqh
            dcq = dcq + _nt(dqh, wq_ref[:, cols])
            dk = dkb_r[h].astype(F32) * (1.0 / LOG2E)
            dksum = dksum + dk
            dkh = dk.astype(BF16)
            dkf[:, cols] = dkh
            dvh = dvb_r[h].astype(BF16)
            dvf[:, cols] = dvh
            dckv = dckv + _nt(dkh, wk_ref[:, cols]) + _nt(dvh, wv_ref[:, cols])
        lane = lax.broadcasted_iota(jnp.int32, (ts, 128), 1)
        rope_lanes = jnp.logical_and(lane >= MLA_NOPE, lane < MLA_QK)
        dp_ref[:, P_KR:P_KR + 128] = jnp.where(rope_lanes, _rope(dksum, c, a, b), 0.0).astype(BF16)
        dqd, gq = _rms_bwd(qd, qg_ref[...], rq, dcq)
        dkvd, gkv = _rms_bwd(kvd, kvg_ref[...], rkv, dckv)
        dp_ref[:, P_QD:P_QD + 256] = dqd.astype(BF16)
        dp_ref[:, P_KVD:P_KVD + 128] = dkvd.astype(BF16)
        first = pl.program_id(0) == 0
        _accumulate(gq_ref, gq, first)
        _accumulate(gkv_ref, gkv, first)
        for src, base, factor in ((dqc_r, P_QC, CA_SCALE), (dkc_r, P_KC, 1.0 / LOG2E), (dvc_r, P_VC, 1.0)):
            for p in range(4):
                dp_ref[:, base + p * 128:base + (p + 1) * 128] = (
                    (src[2 * p].astype(F32) + pltpu.roll(src[2 * p + 1].astype(F32), 64, 1)) * factor).astype(BF16)

    col = lambda w, c: pl.BlockSpec((ts, w), lambda i: (i, c))
    full = lambda shape: pl.BlockSpec(shape, lambda i: (0,) * len(shape))
    wide = jax.ShapeDtypeStruct((S, 1024), BF16)
    return _call(
        body, name="prep_bwd", grid=(S // ts,),
        in_specs=[pl.BlockSpec(memory_space=pl.ANY)] + [_hspec(ts)] * 6 +
                 [col(256, P_QD // 256), col(128, P_KVD // 128),
                  pl.BlockSpec((3, ts, 128), lambda i: (0, i, 0)),
                  full((1, MLA_Q_RANK)), full((1, MLA_KV_RANK)),
                  full((MLA_Q_RANK, 1024)), full((MLA_KV_RANK, 1024)), full((MLA_KV_RANK, 1024))],
        out_specs=[pl.BlockSpec((ts, 2048), lambda i: (i, 0))] + [pl.BlockSpec((ts, 1024), lambda i: (i, 0))] * 3 +
                  [full((1, MLA_Q_RANK)), full((1, MLA_KV_RANK))],
        out_shape=[jax.ShapeDtypeStruct(dproj.shape, BF16), wide, wide, wide,
                   jax.ShapeDtypeStruct((1, MLA_Q_RANK), F32), jax.ShapeDtypeStruct((1, MLA_KV_RANK), F32)],
        input_output_aliases={0: 0},
        compiler_params=_params(1))(dproj, dqb, dkb, dvb, dqc, dkc, dvc, proj, proj, tab, qg, kvg, wq, wk, wv)


def _diag_visible(t):
    r = lax.broadcasted_iota(jnp.int32, (t, t), 0) >> CHUNK_SHIFT
    c = lax.broadcasted_iota(jnp.int32, (t, t), 1) >> CHUNK_SHIFT
    return r <= c


def _pair_tables(nq, kv_major):
    if kv_major:
        pairs = [(kb, qi) for kb in range(nq) for qi in range(kb, nq)]
    else:
        pairs = [(kb, qi) for qi in range(nq) for kb in range(qi + 1)]
    return (jnp.asarray(np.array([p[0] for p in pairs], np.int32)),
            jnp.asarray(np.array([p[1] for p in pairs], np.int32)), len(pairs))


def _finish_softmax(acc, m):
    l = acc[MLA_V:MLA_V + 1, :]
    row = lax.broadcasted_iota(jnp.int32, acc.shape, 0)
    return jnp.where(row < MLA_V, acc / l, 0.0).T.astype(BF16), m + jnp.log2(l)


def _split_refs(refs, counts):
    out, pos = [], 0
    for c in counts:
        out.append(refs[pos:pos + c])
        pos += c
    return out


def mla_fwd(q, k, vt, exchange=()):
    H, S, _ = q.shape
    t, hb, n_ex = ATT_T, MLA_FWD_HEADS_PER_STEP, len(exchange)
    kb_tab, qi_tab, n_pairs = _pair_tables(S // t, False)

    def body(kb_ref, qi_ref, q_ref, k_ref, vt_ref, *rest):
        ex_src, (o_ref, lse_ref), ex_out, (m_s, acc_s), ex_sems = _split_refs(rest, (n_ex, 2, n_ex, 2, 3 if n_ex else 0))
        hg, p_id = pl.program_id(0), pl.program_id(1)
        kb, qi = kb_ref[p_id], qi_ref[p_id]

        if n_ex:
            @pl.when(jnp.logical_and(hg == 0, p_id == 0))
            def _():
                _exchange_issue(ex_src, ex_out, ex_sems, False, True)

        @pl.when(kb == 0)
        def _():
            m_s[...] = jnp.full_like(m_s, NEG_INF)
            acc_s[...] = jnp.zeros_like(acc_s)

        def step(masked):
            for h in range(hb):
                st = _nt(k_ref[h], q_ref[h])
                if masked:
                    st = jnp.where(_diag_visible(t), st, NEG_INF)
                m_prev = m_s[h]
                m_new = jnp.maximum(m_prev, jnp.max(st, axis=0, keepdims=True))
                p = jnp.exp2(st - m_new)
                acc_s[h] = jnp.exp2(m_prev - m_new) * acc_s[h] + _nn(vt_ref[h], p.astype(BF16))
                m_s[h] = m_new

        @pl.when(kb < qi)
        def _():
            step(False)

        @pl.when(kb == qi)
        def _():
            step(True)
            for h in range(hb):
                o_ref[h], lse_ref[h] = _finish_softmax(acc_s[h], m_s[h])

        if n_ex:
            @pl.when(jnp.logical_and(hg == H // hb - 1, p_id == n_pairs - 1))
            def _():
                _exchange_issue(ex_src, ex_out, ex_sems, False, False)

    grid_spec = pltpu.PrefetchScalarGridSpec(
        num_scalar_prefetch=2, grid=(H // hb, n_pairs),
        in_specs=[pl.BlockSpec((hb, t, HEAD_PAD), lambda h, p, kb, qi: (h, qi[p], 0)),
                  pl.BlockSpec((hb, t, HEAD_PAD), lambda h, p, kb, qi: (h, kb[p], 0)),
                  pl.BlockSpec((hb, HEAD_PAD, t), lambda h, p, kb, qi: (h, 0, kb[p]))] + [ANY] * n_ex,
        out_specs=[pl.BlockSpec((hb, t, HEAD_PAD), lambda h, p, kb, qi: (h, qi[p], 0)),
                   pl.BlockSpec((hb, 1, t), lambda h, p, kb, qi: (h, 0, qi[p]))] + [ANY] * n_ex,
        scratch_shapes=[pltpu.VMEM((hb, 1, t), F32), pltpu.VMEM((hb, HEAD_PAD, t), F32)] + _exchange_sems(n_ex))
    outs = _call(
        body, name="mla_fwd_gather" if n_ex else "mla_fwd", grid_spec=grid_spec,
        out_shape=[jax.ShapeDtypeStruct((H, S, HEAD_PAD), BF16), jax.ShapeDtypeStruct((H, 1, S), F32)] +
        _exchange_out_shape(exchange, False),
        compiler_params=_params(2))(kb_tab, qi_tab, q, k, vt, *exchange)
    return outs[0], outs[1], outs[2:]


def mla_bwd(q, k, kt, v, do, lse, delta, exchange=()):
    H, S, _ = q.shape
    t, hb, n_ex = ATT_T, MLA_BWD_HEADS_PER_STEP, len(exchange)
    nq = S // t
    kb_tab, qi_tab, n_pairs = _pair_tables(nq, True)

    def body(kb_ref, qi_ref, q_ref, k_ref, kt_ref, v_ref, do_ref, lse_ref, dl_ref, *rest):
        ex_src, (dq_ref, dk_ref, dv_ref), ex_out, (dqt_s, dk_s, dv_s), ex_sems = _split_refs(
            rest, (n_ex, 3, n_ex, 3, 3 if n_ex else 0))
        hg, p_id = pl.program_id(0), pl.program_id(1)
        kb, qi = kb_ref[p_id], qi_ref[p_id]

        if n_ex:
            @pl.when(jnp.logical_and(hg == 0, p_id == 0))
            def _():
                _exchange_issue(ex_src, ex_out, ex_sems, True, True)

        @pl.when(p_id == 0)
        def _():
            dqt_s[...] = jnp.zeros_like(dqt_s)

        @pl.when(qi == kb)
        def _():
            dk_s[...] = jnp.zeros_like(dk_s)
            dv_s[...] = jnp.zeros_like(dv_s)

        def step(masked):
            for h in range(hb):
                st = _nt(k_ref[h], q_ref[h])
                if masked:
                    st = jnp.where(_diag_visible(t), st, NEG_INF)
                pt = jnp.exp2(st - lse_ref[h])
                dv_s[h] += _nn(pt.astype(BF16), do_ref[h])
                dsb = (pt * (_nt(v_ref[h], do_ref[h]) - dl_ref[h])).astype(BF16)
                dk_s[h] += _nn(dsb, q_ref[h])
                dqt_s[h, qi] += _nn(kt_ref[h], dsb)

        @pl.when(qi == kb)
        def _():
            step(True)
            rows = pl.ds(pl.multiple_of(qi * t, t), t)
            for h in range(hb):
                dq_ref[h, rows, :] = dqt_s[h, qi].T.astype(BF16)

        @pl.when(qi > kb)
        def _():
            step(False)

        @pl.when(qi == nq - 1)
        def _():
            dk_ref[...] = dk_s[...].astype(BF16)
            dv_ref[...] = dv_s[...].astype(BF16)

        if n_ex:
            @pl.when(jnp.logical_and(hg == H // hb - 1, p_id == n_pairs - 1))
            def _():
                _exchange_issue(ex_src, ex_out, ex_sems, True, False)

    qtile = pl.BlockSpec((hb, t, HEAD_PAD), lambda h, p, kb, qi: (h, qi[p], 0))
    ktile = pl.BlockSpec((hb, t, HEAD_PAD), lambda h, p, kb, qi: (h, kb[p], 0))
    stat = pl.BlockSpec((hb, 1, t), lambda h, p, kb, qi: (h, 0, qi[p]))
    grid_spec = pltpu.PrefetchScalarGridSpec(
        num_scalar_prefetch=2, grid=(H // hb, n_pairs),
        in_specs=[qtile, ktile, pl.BlockSpec((hb, HEAD_PAD, t), lambda h, p, kb, qi: (h, 0, kb[p])), ktile, qtile,
                  stat, stat] + [ANY] * n_ex,
        out_specs=[pl.BlockSpec((hb, S, HEAD_PAD), lambda h, p, kb, qi: (h, 0, 0)), ktile, ktile] + [ANY] * n_ex,
        scratch_shapes=[pltpu.VMEM((hb, nq, HEAD_PAD, t), F32), pltpu.VMEM((hb, t, HEAD_PAD), F32),
                        pltpu.VMEM((hb, t, HEAD_PAD), F32)] + _exchange_sems(n_ex))
    outs = _call(
        body, name="mla_bwd_scatter" if n_ex else "mla_bwd", grid_spec=grid_spec,
        out_shape=[jax.ShapeDtypeStruct((H, S, HEAD_PAD), BF16)] * 3 + _exchange_out_shape(exchange, True),
        compiler_params=_params(2))(kb_tab, qi_tab, q, k, kt, v, do, lse, delta, *exchange)
    return outs[0], outs[1], outs[2], outs[3:]


def _band_specs(t, hb):
    prev = lambda i: jnp.maximum(i - 1, 0)
    return dict(
        cur=pl.BlockSpec((hb, t, HEAD_PAD), lambda h, i: (h, i, 0)),
        prev=pl.BlockSpec((hb, t, HEAD_PAD), lambda h, i: (h, prev(i), 0)),
        cur_t=pl.BlockSpec((hb, HEAD_PAD, t), lambda h, i: (h, 0, i)),
        prev_t=pl.BlockSpec((hb, HEAD_PAD, t), lambda h, i: (h, 0, prev(i))),
        stat=pl.BlockSpec((hb, 1, t), lambda h, i: (h, 0, i)),
        bias_prev=pl.BlockSpec((hb, 1, t, t), lambda h, i: (h, jnp.where(i == 0, 1, 0), 0, 0)),
        bias_cur=pl.BlockSpec((hb, 1, t, t), lambda h, i: (h, 2, 0, 0)))


def band_fwd(q, k, vt, bias):
    H, S, _ = q.shape
    t, hb = ATT_T, BAND_FWD_HEADS_PER_STEP
    sp = _band_specs(t, hb)

    def body(q_ref, kp_ref, kc_ref, vtp_ref, vtc_ref, bp_ref, bc_ref, o_ref, lse_ref):
        for h in range(hb):
            s0 = _nt(kp_ref[h], q_ref[h]) + bp_ref[h, 0]
            s1 = _nt(kc_ref[h], q_ref[h]) + bc_ref[h, 0]
            m = jnp.maximum(jnp.max(s0, axis=0, keepdims=True), jnp.max(s1, axis=0, keepdims=True))
            ot = (_nn(vtp_ref[h], jnp.exp2(s0 - m).astype(BF16)) +
                  _nn(vtc_ref[h], jnp.exp2(s1 - m).astype(BF16)))
            o_ref[h], lse_ref[h] = _finish_softmax(ot, m)

    return _call(
        body, name="band_fwd", grid=(H // hb, S // t),
        in_specs=[sp['cur'], sp['prev'], sp['cur'], sp['prev_t'], sp['cur_t'], sp['bias_prev'], sp['bias_cur']],
        out_specs=[sp['cur'], sp['stat']],
        out_shape=[jax.ShapeDtypeStruct((H, S, HEAD_PAD), BF16), jax.ShapeDtypeStruct((H, 1, S), F32)],
        compiler_params=_params(2))(q, k, k, vt, vt, bias, bias)


def band_bwd(q, k, kt, v, do, lse, delta, bias):
    H, S, _ = q.shape
    t = ATT_T
    sp = _band_specs(t, 1)

    def body(q_ref, kp_ref, kc_ref, ktp_ref, ktc_ref, vp_ref, vc_ref, do_ref, lse_ref, dl_ref, bp_ref, bc_ref,
             dq_ref, dk_ref, dv_ref, db_ref):
        i = pl.program_id(1)

        @pl.when(i == 0)
        def _():
            dk_ref[...] = jnp.zeros_like(dk_ref)
            dv_ref[...] = jnp.zeros_like(dv_ref)
            db_ref[...] = jnp.zeros_like(db_ref)

        qv, dov = q_ref[0], do_ref[0]
        dqt = jnp.zeros((HEAD_PAD, t), F32)
        windows = ((0, jnp.maximum(i - 1, 0), kp_ref, ktp_ref, vp_ref, bp_ref),
                   (1, i, kc_ref, ktc_ref, vc_ref, bc_ref))
        for w, blk, k_ref, kt_ref, v_ref, b_ref in windows:
            rows = pl.ds(pl.multiple_of(blk * t, t), t)
            pt = jnp.exp2(_nt(k_ref[0], qv) + b_ref[0, 0] - lse_ref[0])
            dv_ref[0, rows, :] += _nn(pt.astype(BF16), dov)
            ds = pt * (_nt(v_ref[0], dov) - dl_ref[0])
            db_ref[0, w] += ds
            dsb = ds.astype(BF16)
            dk_ref[0, rows, :] += _nn(dsb, qv)
            dqt = dqt + _nn(kt_ref[0], dsb)
        dq_ref[0] = dqt.T.astype(BF16)

    whole = pl.BlockSpec((1, S, HEAD_PAD), lambda h, i: (h, 0, 0))
    return _call(
        body, name="band_bwd", grid=(H, S // t),
        in_specs=[sp['cur'], sp['prev'], sp['cur'], sp['prev_t'], sp['cur_t'], sp['prev'], sp['cur'], sp['cur'],
                  sp['stat'], sp['stat'], sp['bias_prev'], sp['bias_cur']],
        out_specs=[sp['cur'], whole, whole, pl.BlockSpec((1, 2, t, t), lambda h, i: (h, 0, 0, 0))],
        out_shape=[jax.ShapeDtypeStruct((H, S, HEAD_PAD), BF16), jax.ShapeDtypeStruct((H, S, HEAD_PAD), F32),
                   jax.ShapeDtypeStruct((H, S, HEAD_PAD), F32), jax.ShapeDtypeStruct((H, 2, t, t), F32)],
        compiler_params=_params(2))(q, k, k, kt, kt, v, v, do, lse, delta, bias, bias)


def _compact(o_ref):
    return jnp.concatenate([o_ref[2 * p].astype(F32) + pltpu.roll(o_ref[2 * p + 1].astype(F32), 64, 1)
                            for p in range(4)], axis=1)


def merge_fwd(ob, oc, proj, ya, gate_b, wbr, w_out, x, post_g):
    S = x.shape[0]
    ts = ROW_TILE

    def body(ob_ref, oc_ref, zb_ref, zc_ref, ya_ref, gl_ref, gb_ref, wbr_ref, wo_ref, x_ref, pg_ref,
             xo_ref, yb_ref, yc_ref, mg_ref, out_ref):
        zb, zc = zb_ref[...], zc_ref[...]
        yb = (_compact(ob_ref) * (zb * _sigmoid(zb))).astype(BF16)
        yc = (_compact(oc_ref) * (zc * _sigmoid(zc))).astype(BF16)
        yb_ref[...] = yb
        yc_ref[...] = yc
        merged = jnp.zeros((ts, D_MODEL), F32)
        for n, y in enumerate((ya_ref[...], yb, yc)):
            cols = slice(n * D_MODEL, (n + 1) * D_MODEL)
            gate = _sigmoid(gl_ref[:, cols] + gb_ref[:, cols])
            merged = merged + gate * _nn(y, wbr_ref[n])
        mb = merged.astype(BF16)
        mg_ref[...] = mb
        out = _nn(mb, wo_ref[...])
        out_ref[...] = out
        normed, _ = _rms(out, pg_ref[...])
        xo_ref[...] = x_ref[...] + normed

    row = lambda w: pl.BlockSpec((ts, w), lambda i: (i, 0))
    col = lambda w, c: pl.BlockSpec((ts, w), lambda i: (i, c))
    full = lambda shape: pl.BlockSpec(shape, lambda i: (0,) * len(shape))
    return _call(
        body, name="merge_fwd", grid=(S // ts,),
        in_specs=[_hspec(ts), _hspec(ts), col(512, P_ZB // 512), col(512, P_ZC // 512), row(512),
                  col(3072, P_G // 3072), full((1, 3072)), full((3, BR_WIDTH, D_MODEL)),
                  full((D_MODEL, D_MODEL)), row(D_MODEL), full((1, D_MODEL))],
        out_specs=[row(D_MODEL), row(512), row(512), row(D_MODEL), row(D_MODEL)],
        out_shape=[jax.ShapeDtypeStruct((S, D_MODEL), F32), jax.ShapeDtypeStruct((S, 512), BF16),
                   jax.ShapeDtypeStruct((S, 512), BF16), jax.ShapeDtypeStruct((S, D_MODEL), BF16),
                   jax.ShapeDtypeStruct((S, D_MODEL), F32)],
        compiler_params=_params(1))(ob, oc, proj, proj, ya, proj, gate_b, wbr, w_out, x, post_g)


def gate_bwd(g, out, post_g, w_out, proj, gate_b, ya, yb, yc, wbr):
    S = g.shape[0]
    ts = ROW_TILE

    def body(g_ref, out_ref, pg_ref, wo_ref, gl_ref, gb_ref, ya_ref, yb_ref, yc_ref, wbr_ref,
             dp_ref, do_ref, dba_ref, dbb_ref, dbc_ref, dy_ref, ggb_ref, gp_ref):
        first = pl.program_id(0) == 0
        ov = out_ref[...]
        _, r = _rms(ov, pg_ref[...])
        dout, gp = _rms_bwd(ov, pg_ref[...], r, g_ref[...])
        db = dout.astype(BF16)
        do_ref[...] = db
        _accumulate(gp_ref, gp, first)
        dm = _nt(db, wo_ref[...])
        ggb = []
        for n, (y_ref, dbr_ref) in enumerate(((ya_ref, dba_ref), (yb_ref, dbb_ref), (yc_ref, dbc_ref))):
            cols = slice(n * D_MODEL, (n + 1) * D_MODEL)
            br = _nn(y_ref[...], wbr_ref[n])
            sg = _sigmoid(gl_ref[:, cols] + gb_ref[:, cols])
            dgl = dm * br * (sg * (1.0 - sg))
            dp_ref[:, cols] = dgl.astype(BF16)
            ggb.append(jnp.sum(dgl, axis=0, keepdims=True))
            dbr = (dm * sg).astype(BF16)
            dbr_ref[...] = dbr
            dy_ref[n] = _nt(dbr, wbr_ref[n])
        _accumulate(ggb_ref, jnp.concatenate(ggb, axis=1), first)

    row = lambda w: pl.BlockSpec((ts, w), lambda i: (i, 0))
    full = lambda shape: pl.BlockSpec(shape, lambda i: (0,) * len(shape))
    wide = jax.ShapeDtypeStruct((S, D_MODEL), BF16)
    return _call(
        body, name="gate_bwd", grid=(S // ts,),
        in_specs=[row(D_MODEL), row(D_MODEL), full((1, D_MODEL)), full((D_MODEL, D_MODEL)),
                  pl.BlockSpec((ts, 3072), lambda i: (i, P_G // 3072)), full((1, 3072)),
                  row(512), row(512), row(512), full((3, BR_WIDTH, D_MODEL))],
        out_specs=[pl.BlockSpec((ts, 3072), lambda i: (i, P_G // 3072)), row(D_MODEL), row(D_MODEL), row(D_MODEL),
                   row(D_MODEL), pl.BlockSpec((3, ts, 512), lambda i: (0, i, 0)), full((1, 3072)),
                   full((1, D_MODEL))],
        out_shape=[jax.ShapeDtypeStruct((S, P_W), BF16), wide, wide, wide, wide,
                   jax.ShapeDtypeStruct((3, S, 512), F32), jax.ShapeDtypeStruct((1, 3072), F32),
                   jax.ShapeDtypeStruct((1, D_MODEL), F32)],
        compiler_params=_params(1))(g, out, post_g, w_out, proj, gate_b, ya, yb, yc, wbr)


def ungate_bwd(dproj, dy, ob, oc, proj):
    S = proj.shape[0]
    ts = LIGHT_ROW_TILE

    def body(dp_in, dyb_ref, dyc_ref, ob_ref, oc_ref, zb_ref, zc_ref, dp_ref, dob_ref, doc_ref, dlb_ref, dlc_ref):
        del dp_in
        lane = lax.broadcasted_iota(jnp.int32, (ts, 128), 1)
        for n, (dy_ref, o_ref, z_ref, do_ref, dl_ref) in enumerate(
                ((dyb_ref, ob_ref, zb_ref, dob_ref, dlb_ref), (dyc_ref, oc_ref, zc_ref, doc_ref, dlc_ref))):
            zz = z_ref[...]
            dyv = dy_ref[0]
            sg = _sigmoid(zz)
            dp_ref[:, n * 512:(n + 1) * 512] = (dyv * _compact(o_ref) * (sg * (1.0 + zz * (1.0 - sg)))).astype(BF16)
            do_c = dyv * (zz * sg)
            for p in range(4):
                piece = do_c[:, p * 128:(p + 1) * 128]
                for h, d in ((2 * p, jnp.where(lane < 64, piece, 0.0)),
                             (2 * p + 1, jnp.where(lane < 64, pltpu.roll(piece, 64, 1), 0.0))):
                    do_ref[h] = d.astype(BF16)
                    dl_ref[h] = jnp.sum((d * o_ref[h].astype(F32)).T, axis=0, keepdims=True)

    col = lambda c: pl.BlockSpec((ts, 512), lambda i: (i, c))
    dysp = lambda n: pl.BlockSpec((1, ts, 512), lambda i: (n, i, 0))
    stat = pl.BlockSpec((N_HEADS, 1, ts), lambda i: (0, 0, i))
    hshape = jax.ShapeDtypeStruct((N_HEADS, S, HEAD_PAD), BF16)
    sshape = jax.ShapeDtypeStruct((N_HEADS, 1, S), F32)
    return _call(
        body, name="ungate_bwd", grid=(S // ts,),
        in_specs=[pl.BlockSpec(memory_space=pl.ANY), dysp(1), dysp(2), _hspec(ts), _hspec(ts),
                  col(P_ZB // 512), col(P_ZC // 512)],
        out_specs=[pl.BlockSpec((ts, 1024), lambda i: (i, P_ZB // 1024)), _hspec(ts), _hspec(ts), stat, stat],
        out_shape=[jax.ShapeDtypeStruct(dproj.shape, BF16), hshape, hshape, sshape, sshape],
        input_output_aliases={0: 0},
        compiler_params=_params(1))(dproj, dy, dy, ob, oc, proj, proj)


def loss_head(y, target):
    S, D = y.shape
    ts = LIGHT_ROW_TILE

    def body(y_ref, t_ref, dy_ref, sq_ref):
        d = y_ref[...] - t_ref[...]
        dy_ref[...] = d * (1.0 / D)
        _accumulate(sq_ref, jnp.sum(d * d, axis=0, keepdims=True), pl.program_id(0) == 0)

    row = pl.BlockSpec((ts, D), lambda i: (i, 0))
    return _call(
        body, name="loss_head", grid=(S // ts,), in_specs=[row, row],
        out_specs=[row, pl.BlockSpec((1, D), lambda i: (0, 0))],
        out_shape=[jax.ShapeDtypeStruct((S, D), F32), jax.ShapeDtypeStruct((1, D), F32)],
        compiler_params=_params(1))(y, target)


def _row_tile(rows, cols):
    for cand in (1024, 512, 256, 128, 64, 32, 16, 8):
        if rows % cand == 0 and cand * cols * 4 <= 1024 * 1024:
            return cand
    return rows


def adamw(w, grads, m, v):
    shape = w.shape
    cols = shape[-1]
    rows = int(np.prod(shape[:-1]))
    tr = _row_tile(rows, cols)
    n_g = len(grads)
    c1 = 1.0 - ADAM_B1 ** ADAM_STEP
    c2 = 1.0 - ADAM_B2 ** ADAM_STEP

    def body(*refs):
        w_ref, m_ref, v_ref = refs[:3]
        g_refs = refs[3:3 + n_g]
        go_ref, d_ref, mo_ref, vo_ref = refs[3 + n_g:]
        gv = g_refs[0][...]
        for g_ref in g_refs[1:]:
            gv = gv + g_ref[...]
        go_ref[...] = gv
        mn = ADAM_B1 * m_ref[...] + (1.0 - ADAM_B1) * gv
        vn = ADAM_B2 * v_ref[...] + (1.0 - ADAM_B2) * (gv * gv)
        mo_ref[...] = mn
        vo_ref[...] = vn
        d_ref[...] = -ADAM_LR * ((mn / c1) / (jnp.sqrt(vn / c2) + ADAM_EPS) + ADAM_WD * w_ref[...])

    blk = pl.BlockSpec((tr, cols), lambda i: (i, 0))
    sds = jax.ShapeDtypeStruct((rows, cols), F32)
    outs = _call(
        body, name="adamw", grid=(rows // tr,), in_specs=[blk] * (3 + n_g), out_specs=[blk] * 4,
        out_shape=[sds] * 4, compiler_params=_params(1))(
            *[a.reshape(rows, cols) for a in (w, m, v, *grads)])
    return [o.reshape(shape) for o in outs]


def add_lead(parts):
    n = parts.shape[0]
    shape = parts.shape[1:]
    cols = shape[-1]
    rows = int(np.prod(shape[:-1]))
    tr = _row_tile(rows, cols * n)

    def body(p_ref, o_ref):
        acc = p_ref[0].astype(F32)
        for s in range(1, n):
            acc = acc + p_ref[s].astype(F32)
        o_ref[...] = acc

    out = _call(
        body, name="add_lead", grid=(rows // tr,),
        in_specs=[pl.BlockSpec((n, tr, cols), lambda i: (0, i, 0))],
        out_specs=pl.BlockSpec((tr, cols), lambda i: (i, 0)),
        out_shape=jax.ShapeDtypeStruct((rows, cols), F32),
        compiler_params=_params(1))(parts.reshape(n, rows, cols))
    return out.reshape(shape)


ANY = pl.BlockSpec(memory_space=pl.ANY)


def _other_chips(x, y):
    return [(1 - x, y), (x, 1 - y), (1 - x, 1 - y)]


def chip_exchange(arrays, scatter, name):
    n = len(arrays)

    def body(*refs):
        _exchange_issue(refs[:n], refs[n:2 * n], refs[2 * n:], scatter, True)
        _exchange_issue(refs[:n], refs[n:2 * n], refs[2 * n:], scatter, False)

    return _call(
        body, name=name, in_specs=[ANY] * n, out_specs=[ANY] * n,
        out_shape=_exchange_out_shape(arrays, scatter), scratch_shapes=_exchange_sems(n))(*arrays)


def _exchange_out_shape(arrays, scatter):
    return [jax.ShapeDtypeStruct(a.shape if scatter else (4,) + a.shape, a.dtype) for a in arrays]


def _exchange_sems(n):
    if n == 0:
        return []
    return [pltpu.SemaphoreType.DMA((3 * n,)), pltpu.SemaphoreType.DMA((3 * n,)), pltpu.SemaphoreType.DMA((n,))]


def _exchange_issue(srcs, outs, sems, scatter, start):
    send_sems, recv_sems, local_sems = sems
    x, y, c = lax.axis_index("x"), lax.axis_index("y"), lax.axis_index("c")
    me = 2 * x + y
    for a in range(len(srcs)):
        local_src = srcs[a].at[me] if scatter else srcs[a]
        mine = pltpu.make_async_copy(local_src, outs[a].at[me], local_sems.at[a])
        sends = []
        for j, (px, py) in enumerate(_other_chips(x, y)):
            pair = dict(send_sem=send_sems.at[3 * a + j], recv_sem=recv_sems.at[3 * a + j],
                        device_id=(px, py, c), device_id_type=MESH)
            sends.append(pltpu.make_async_remote_copy(
                src_ref=srcs[a].at[2 * px + py] if scatter else srcs[a], dst_ref=outs[a].at[me], **pair))
            if not start:
                pltpu.make_async_remote_copy(src_ref=local_src, dst_ref=outs[a].at[2 * px + py], **pair).wait_recv()
        if start:
            mine.start()
            for cp in sends:
                cp.start()
        else:
            for cp in sends:
                cp.wait_send()
            mine.wait()


def gather_two_level(arrays):
    n = len(arrays)
    cut = [next((d for d in range(min(2, a.ndim)) if a.shape[d] % 2 == 0), None) for a in arrays]
    split = [d is not None for d in cut]

    def body(*refs):
        srcs, outs = refs[:n], refs[n:2 * n]
        send_sems, recv_sems, fwd_send, fwd_recv, local_sems = refs[2 * n:]
        x, y, c = lax.axis_index("x"), lax.axis_index("y"), lax.axis_index("c")
        me = 2 * x + y
        peers = _other_chips(x, y)

        def rows(a, half):
            h = arrays[a].shape[cut[a]] // 2
            return (slice(None),) * cut[a] + (pl.ds(half * h, h),)

        def over_ici(a, j, src, dst_chip, half):
            px, py = peers[j]
            part = (lambda r: r.at[rows(a, half)]) if split[a] else (lambda r: r)
            return pltpu.make_async_remote_copy(
                src_ref=part(src), dst_ref=part(outs[a].at[dst_chip]), send_sem=send_sems.at[3 * a + j],
                recv_sem=recv_sems.at[3 * a + j], device_id=(px, py, c), device_id_type=MESH)

        def over_d2d(a, j, half):
            px, py = peers[j]
            landed = outs[a].at[2 * px + py].at[rows(a, half)]
            return pltpu.make_async_remote_copy(
                src_ref=landed, dst_ref=landed, send_sem=fwd_send.at[3 * a + j], recv_sem=fwd_recv.at[3 * a + j],
                device_id=(x, y, 1 - c), device_id_type=MESH)

        own = [pltpu.make_async_copy(srcs[a], outs[a].at[me], local_sems.at[a]) for a in range(n)]
        sends = [over_ici(a, j, srcs[a], me, c) for a in range(n) for j in range(3)]
        for cp in own + sends:
            cp.start()
        forwards = []
        for a in range(n):
            for j, (px, py) in enumerate(peers):
                over_ici(a, j, srcs[a], 2 * px + py, c).wait_recv()
                if split[a]:
                    fwd = over_d2d(a, j, c)
                    fwd.start()
                    forwards.append(fwd)
        for a in range(n):
            if split[a]:
                for j in range(3):
                    over_d2d(a, j, 1 - c).wait_recv()
        for cp in sends + forwards:
            cp.wait_send()
        for cp in own:
            cp.wait()

    sems = [pltpu.SemaphoreType.DMA((3 * n,))] * 4 + [pltpu.SemaphoreType.DMA((n,))]
    return _call(
        body, name="gather_two_level", in_specs=[ANY] * n, out_specs=[ANY] * n,
        out_shape=_exchange_out_shape(arrays, False), scratch_shapes=sems)(*arrays)


def sibling_exchange(arrays):
    n = len(arrays)

    def body(*refs):
        srcs, outs = refs[:n], refs[n:2 * n]
        send_sems, recv_sems = refs[2 * n:]
        x, y, c = lax.axis_index("x"), lax.axis_index("y"), lax.axis_index("c")
        copies = [pltpu.make_async_remote_copy(src_ref=srcs[a], dst_ref=outs[a], send_sem=send_sems.at[a],
                                               recv_sem=recv_sems.at[a], device_id=(x, y, 1 - c), device_id_type=MESH)
                  for a in range(n)]
        for cp in copies:
            cp.start()
        for cp in copies:
            cp.wait()

    return _call(
        body, name="sibling_exchange", in_specs=[ANY] * n, out_specs=[ANY] * n,
        out_shape=[jax.ShapeDtypeStruct(a.shape, a.dtype) for a in arrays],
        scratch_shapes=[pltpu.SemaphoreType.DMA((n,)), pltpu.SemaphoreType.DMA((n,))])(*arrays)


def _perm_from_shards(sh):
    rows = sh.shape[1]
    pieces, pos = [], 0
    for lo, hi, plo in sorted(NAT_SEGS, key=lambda s: s[2]):
        if plo > pos:
            pieces.append(jnp.zeros((rows, plo - pos), sh.dtype))
            pos = plo
        c = lo
        while c < hi:
            kk = c // SHARD_COLS
            e = min(hi, (kk + 1) * SHARD_COLS)
            pieces.append(sh[kk][:, c - kk * SHARD_COLS:e - kk * SHARD_COLS])
            c = e
        pos += hi - lo
    if pos < P_W:
        pieces.append(jnp.zeros((rows, P_W - pos), sh.dtype))
    return jnp.concatenate(pieces, axis=1)


def _shards_from_perm(p):
    out = []
    for kk in range(4):
        lo_k, hi_k = kk * SHARD_COLS, (kk + 1) * SHARD_COLS
        pieces = []
        for lo, hi, plo in NAT_SEGS:
            a, b = max(lo, lo_k), min(hi, hi_k)
            if a < b:
                pieces.append(p[:, plo + (a - lo):plo + (b - lo)])
        out.append(jnp.concatenate(pieces, axis=1))
    return jnp.stack(out)


def _split4(a, axis):
    shape = a.shape
    a = a.reshape(shape[:axis] + (4, shape[axis] // 4) + shape[axis + 1:])
    return jnp.moveaxis(a, axis, 0)


def _join4(a, axis):
    a = jnp.moveaxis(a, 0, axis)
    shape = a.shape
    return a.reshape(shape[:axis] + (4 * shape[axis + 1],) + shape[axis + 2:])


def _pad_heads(w, per_head, lo, hi):
    r = w.shape[0]
    wh = w.reshape(r, N_HEADS, per_head)[:, :, lo:hi]
    return jnp.pad(wh, ((0, 0), (0, 0), (0, HEAD_PAD - (hi - lo)))).reshape(r, N_HEADS * HEAD_PAD)


def _rope_table(S):
    half = MLA_ROPE // 2
    inv = ROPE_BASE ** (-jnp.arange(half, dtype=F32) / half)
    ang = jnp.arange(S).astype(F32)[:, None] * inv[None, :]
    cos, sin = jnp.cos(ang), jnp.sin(ang)
    z = lambda n: jnp.zeros((S, n), F32)
    c = jnp.concatenate([jnp.ones((S, MLA_NOPE), F32), cos, cos, z(32)], axis=1)
    a = jnp.concatenate([z(MLA_NOPE), -sin, z(48)], axis=1)
    b = jnp.concatenate([z(MLA_NOPE + half), sin, z(32)], axis=1)
    return jnp.stack([c, a, b])


def _band_onehot():
    t = ATT_T
    m = np.arange(2 * t)
    d = np.where(m < t, m, m - 2 * t)
    idx = np.stack([np.clip(off + d, -REL_CLIP, REL_CLIP) + REL_CLIP for off in (t, 0)])
    return (idx[:, :, None] == np.arange(2 * REL_CLIP + 1)[None, None, :]).astype(np.float32)


def bias_expand(diag):
    t = ATT_T

    def body(d_ref, o_ref):
        kc = lax.broadcasted_iota(jnp.int32, (t, t), 0) >> CHUNK_SHIFT
        qc = lax.broadcasted_iota(jnp.int32, (t, t), 1) >> CHUNK_SHIFT
        for w, visible in ((0, kc >= qc), (1, kc <= qc)):
            rows = jnp.broadcast_to(d_ref[0, w:w + 1, :], (t, 2 * t))
            skew = pltpu.roll(rows, 0, 1, stride=1, stride_axis=0)[:, :t]
            o_ref[0, 2 * w] = jnp.where(visible, skew * LOG2E, NEG_INF)
        o_ref[0, 1] = jnp.full((t, t), NEG_INF, F32)

    return _call(
        body, name="bias_expand", grid=(N_HEADS,),
        in_specs=[pl.BlockSpec((1, 2, 2 * t), lambda h: (h, 0, 0))],
        out_specs=pl.BlockSpec((1, 3, t, t), lambda h: (h, 0, 0, 0)),
        out_shape=jax.ShapeDtypeStruct((N_HEADS, 3, t, t), F32),
        compiler_params=_params(1))(diag)


def bias_fold(dtiles):
    t = ATT_T

    def body(d_ref, o_ref):
        pad = jnp.zeros((8, t), F32)
        for w in range(2):
            acc = jnp.concatenate([d_ref[0, w, 0:8, :], pad], axis=1)
            for g in range(1, t // 8):
                grp = jnp.concatenate([d_ref[0, w, 8 * g:8 * g + 8, :], pad], axis=1)
                acc = acc + pltpu.roll(grp, 2 * t - 8 * g, 1)
            out = acc[0:1, :]
            for s in range(1, 8):
                out = out + pltpu.roll(acc, 2 * t - s, 1)[s:s + 1, :]
            o_ref[0, w:w + 1, :] = out

    return _call(
        body, name="bias_fold", grid=(N_HEADS,),
        in_specs=[pl.BlockSpec((1, 2, t, t), lambda h: (h, 0, 0, 0))],
        out_specs=pl.BlockSpec((1, 2, 2 * t), lambda h: (h, 0, 0)),
        out_shape=jax.ShapeDtypeStruct((N_HEADS, 2, 2 * t), F32),
        compiler_params=_params(1))(dtiles)


def _bias_tiles(table):
    diag = jnp.einsum('hr,wdr->hwd', table, jnp.asarray(_band_onehot()), precision=lax.Precision.HIGHEST)
    return bias_expand(diag)


def _bias_tiles_grad(dtiles):
    return jnp.einsum('hwd,wdr->hr', bias_fold(dtiles), jnp.asarray(_band_onehot()),
                      precision=lax.Precision.HIGHEST)


def _layer_consts(lw):
    tri = np.tril(np.ones((SGU_BLOCK, SGU_BLOCK), np.float32))
    ws = (lw['sgu_w'] * tri).astype(BF16)
    return dict(
        ws=ws, ws_t=jnp.swapaxes(ws, 1, 2), sgu_bias=jnp.repeat(lw['sgu_b'].T, CA_HEAD_DIM, axis=1),
        bias=_bias_tiles(lw['ca_rel_bias']),
        wq=_pad_heads(lw['mla_w_uq'], MLA_QK, 0, MLA_QK),
        wk=_pad_heads(lw['mla_w_ukv'], MLA_NOPE + MLA_V, 0, MLA_NOPE),
        wv=_pad_heads(lw['mla_w_ukv'], MLA_NOPE + MLA_V, MLA_NOPE, MLA_NOPE + MLA_V),
        gate_b=lw['gate_b'].reshape(1, 3 * D_MODEL),
        pre_g=lw['pre_g'][None], post_g=lw['post_g'][None], ln_g=lw['sgu_ln_g'][None], ln_b=lw['sgu_ln_b'][None],
        qg=lw['mla_q_norm_g'][None], kvg=lw['mla_kv_norm_g'][None])


def _layer_fwd(x, lw, k, tab, next_shards):
    proj, xnt = norm_matmul(x, k['pre_g'], lw['w_in'])
    ya = sgu_fwd(proj, k['ln_g'], k['ln_b'], k['ws'], k['sgu_bias'])
    qb, kb, vb, qc, kc, vc, kbt, vbt, kct, vct, cq, ckv = prep_fwd(proj, tab, k['qg'], k['kvg'], k['wq'], k['wk'],
                                                                   k['wv'])
    ob, lse_b, gathered = mla_fwd(qb, kb, vbt, next_shards)
    oc, lse_c = band_fwd(qc, kc, vct, k['bias'])
    x_new, yb, yc, merged, out = merge_fwd(ob, oc, proj, ya, k['gate_b'], lw['w_branch'], lw['w_out'], x,
                                           k['post_g'])
    saved = dict(x=x, proj=proj, xnt=xnt, ya=ya, yb=yb, yc=yc, qb=qb, kb=kb, vb=vb, qc=qc, kc=kc, vc=vc, kbt=kbt, kct=kct,
                 cq=cq, ckv=ckv, ob=ob, oc=oc, lse_b=lse_b, lse_c=lse_c, merged=merged, out=out)
    return x_new, saved, gathered


def _layer_bwd(g, s, lw, k, tab, pending_parts, scatter_own):
    S = g.shape[0]
    H = N_HEADS
    dproj, dout, dba, dbb, dbc, dy, g_gate_b, g_post = gate_bwd(
        g, s['out'], k['post_g'], lw['w_out'], s['proj'], k['gate_b'], s['ya'], s['yb'], s['yc'], lw['w_branch'])
    g_w_out = matmul_tn(s['merged'], dout, D_MODEL)
    g_w_branch = jnp.stack([matmul_tn(y, d, D_MODEL) for y, d in ((s['ya'], dba), (s['yb'], dbb), (s['yc'], dbc))])
    dproj, dob, doc, dl_b, dl_c = ungate_bwd(dproj, dy, s['ob'], s['oc'], s['proj'])
    dqb, dkb, dvb, landed = mla_bwd(s['qb'], s['kb'], s['kbt'], s['vb'], dob, s['lse_b'], dl_b, pending_parts)
    dqc, dkc, dvc, dbias = band_bwd(s['qc'], s['kc'], s['kct'], s['vc'], doc, s['lse_c'], dl_c, k['bias'])
    dproj, dqf, dkf, dvf, g_qg, g_kvg = prep_bwd(dproj, dqb, dkb, dvb, dqc, dkc, dvc, s['proj'], tab,
                                                 k['qg'], k['kvg'], k['wq'], k['wk'], k['wv'])
    wide = N_HEADS * HEAD_PAD
    g_wq = matmul_tn(s['cq'], dqf, wide).reshape(MLA_Q_RANK, H, HEAD_PAD)[:, :, :MLA_QK]
    g_wk = matmul_tn(s['ckv'], dkf, wide).reshape(MLA_KV_RANK, H, HEAD_PAD)[:, :, :MLA_NOPE]
    g_wv = matmul_tn(s['ckv'], dvf, wide).reshape(MLA_KV_RANK, H, HEAD_PAD)[:, :, :MLA_V]
    dproj, g_ln_g, g_ln_b, g_ws, g_sgu_bias = sgu_bwd(dproj, dy, s['proj'], k['ln_g'], k['ln_b'], k['ws'],
                                                      k['ws_t'], k['sgu_bias'])
    g_w_in = matmul_acc(s['xnt'], dproj, MM_TN)
    sharded = _sharded_parts(dict(
        w_in=g_w_in, mla_w_uq=g_wq.reshape(MLA_Q_RANK, H * MLA_QK),
        mla_w_ukv=jnp.concatenate([g_wk, g_wv], axis=2).reshape(MLA_KV_RANK, H * (MLA_NOPE + MLA_V)),
        w_branch=g_w_branch, gate_b=g_gate_b.reshape(N_BRANCH, D_MODEL), w_out=g_w_out))
    dx, g_pre, own_landed = proj_bwd_x(dproj, lw['w_in'], s['x'], k['pre_g'], g, sharded if scatter_own else ())
    tri = np.tril(np.ones((SGU_BLOCK, SGU_BLOCK), np.float32))
    small = _small_pack(dict(
        pre_g=g_pre[0], post_g=g_post[0], sgu_ln_g=g_ln_g[0], sgu_ln_b=g_ln_b[0],
        sgu_w=g_ws * tri, sgu_b=jnp.sum(g_sgu_bias.reshape(SGU_BLOCK, 8, CA_HEAD_DIM), axis=2).T,
        mla_q_norm_g=g_qg[0], mla_kv_norm_g=g_kvg[0], ca_rel_bias=_bias_tiles_grad(dbias)))
    return dx, (own_landed if scatter_own else sharded), small, landed


BF16_PARTS = ('w_in', 'mla_w_uq', 'mla_w_ukv', 'w_branch', 'w_out')


def _weight_shards(w, l):
    return [w[n][l].astype(BF16) if n in BF16_PARTS else w[n][l] for n in SHARDED]


def _full_weights(gathered, small):
    lw = {n: _join4(a, SHARD_AXIS[n]) for n, a in zip(SHARDED, gathered) if n != 'w_in'}
    lw['w_in'] = _perm_from_shards(gathered[0])
    lw.update(small)
    return lw


def _small_pack(grads):
    flat = jnp.concatenate([grads[n].reshape(-1) for n in SMALL])
    quarter = -(-flat.size // (4 * 1024)) * 1024
    return jnp.pad(flat, (0, 4 * quarter - flat.size)).reshape(4, quarter // 128, 128)


def _sharded_parts(grads):
    parts = [_shards_from_perm(grads['w_in'])]
    parts += [_split4(grads[n], SHARD_AXIS[n]) for n in SHARDED if n != 'w_in']
    return [p.astype(BF16) if n in BF16_PARTS else p for n, p in zip(SHARDED, parts)]


def train_step_local(x, target, w):
    S = x.shape[0]
    depth = w['w_in'].shape[0]
    tab = _rope_table(S)
    gathered = gather_two_level(_weight_shards(w, 0))
    layer_w, consts, saved = [], [], []
    for l in range(depth):
        lw = _full_weights(gathered, {n: w[n][l] for n in SMALL})
        k = _layer_consts(lw)
        x, s, gathered = _layer_fwd(x, lw, k, tab, _weight_shards(w, l + 1) if l + 1 < depth else ())
        layer_w.append(lw)
        consts.append(k)
        saved.append(s)
    g, sq = loss_head(x, target)
    mine = [None] * depth
    pending = ()
    for l in reversed(range(depth)):
        g, sharded, small, landed = _layer_bwd(g, saved[l], layer_w[l], consts[l], tab, pending, l == 0)
        if pending:
            mine[l + 1] = [add_lead(p) for p in landed]
        pending = list(sharded) + [small]
    mine[0] = [add_lead(p) for p in pending[:-1] + list(chip_exchange(pending[-1:], True, "scatter_small"))]
    n_parts = len(mine[0])
    theirs = sibling_exchange([p for layer in mine for p in layer])
    return sq, g, [(mine[l], theirs[l * n_parts:(l + 1) * n_parts]) for l in range(depth)]


def kernel(x, w_in, pre_g, post_g, sgu_ln_g, sgu_ln_b, sgu_w, sgu_b, mla_q_norm_g, mla_kv_norm_g, mla_w_uq, mla_w_ukv, ca_rel_bias, w_branch, gate_b, w_out, loss_target, m_w_in, m_pre_g, m_post_g, m_sgu_ln_g, m_sgu_ln_b, m_sgu_w, m_sgu_b, m_mla_q_norm_g, m_mla_kv_norm_g, m_mla_w_uq, m_mla_w_ukv, m_ca_rel_bias, m_w_branch, m_gate_b, m_w_out, v_w_in, v_pre_g, v_post_g, v_sgu_ln_g, v_sgu_ln_b, v_sgu_w, v_sgu_b, v_mla_q_norm_g, v_mla_kv_norm_g, v_mla_w_uq, v_mla_w_ukv, v_ca_rel_bias, v_w_branch, v_gate_b, v_w_out):
    w = dict(w_in=w_in, pre_g=pre_g, post_g=post_g, sgu_ln_g=sgu_ln_g, sgu_ln_b=sgu_ln_b, sgu_w=sgu_w, sgu_b=sgu_b,
             mla_q_norm_g=mla_q_norm_g, mla_kv_norm_g=mla_kv_norm_g, mla_w_uq=mla_w_uq, mla_w_ukv=mla_w_ukv,
             ca_rel_bias=ca_rel_bias, w_branch=w_branch, gate_b=gate_b, w_out=w_out)
    m = dict(w_in=m_w_in, pre_g=m_pre_g, post_g=m_post_g, sgu_ln_g=m_sgu_ln_g, sgu_ln_b=m_sgu_ln_b, sgu_w=m_sgu_w,
             sgu_b=m_sgu_b, mla_q_norm_g=m_mla_q_norm_g, mla_kv_norm_g=m_mla_kv_norm_g, mla_w_uq=m_mla_w_uq,
             mla_w_ukv=m_mla_w_ukv, ca_rel_bias=m_ca_rel_bias, w_branch=m_w_branch, gate_b=m_gate_b, w_out=m_w_out)
    v = dict(w_in=v_w_in, pre_g=v_pre_g, post_g=v_post_g, sgu_ln_g=v_sgu_ln_g, sgu_ln_b=v_sgu_ln_b, sgu_w=v_sgu_w,
             sgu_b=v_sgu_b, mla_q_norm_g=v_mla_q_norm_g, mla_kv_norm_g=v_mla_kv_norm_g, mla_w_uq=v_mla_w_uq,
             mla_w_ukv=v_mla_w_ukv, ca_rel_bias=v_ca_rel_bias, w_branch=v_w_branch, gate_b=v_gate_b, w_out=v_w_out)
    depth = w_in.shape[0]
    sq, grad_x, reduced = train_step_local(x[0], loss_target[0], w)
    loss = lax.psum(0.5 * jnp.sum(sq) / D_MODEL, ("x", "y", "c"))

    out = {}
    for a, n in enumerate(SHARDED):
        mine = jnp.stack([reduced[l][0][a] for l in range(depth)])
        theirs = jnp.stack([reduced[l][1][a] for l in range(depth)])
        out[n] = adamw(w[n], [mine, theirs], m[n], v[n])
    small = jnp.stack([jnp.stack([reduced[l][0][-1] for l in range(depth)]),
                       jnp.stack([reduced[l][1][-1] for l in range(depth)])])
    quarter = add_lead(small)
    full = chip_exchange([quarter], False, "gather_small")[0]
    full = jnp.moveaxis(full, 0, 1).reshape(depth, -1)
    off = 0
    for n in SMALL:
        size = int(np.prod(w[n].shape[1:]))
        out[n] = adamw(w[n], [full[:, off:off + size].reshape(w[n].shape)], m[n], v[n])
        off += size
    return (loss, grad_x[None], *[out[n][0] for n in WEIGHTS], *[out[n][1] for n in WEIGHTS],
            *[out[n][2] for n in WEIGHTS], *[out[n][3] for n in WEIGHTS])
```
